```python
import math
import jax, jax.numpy as jnp
from jax import lax
import numpy as np

D_MODEL = 2048
BATCH = 2
SEQ = 4096
DEPTH = 2
DEC_BATCH = 32
DEC_SEQ = 1
PAST_LEN = 16384
PAGE_SIZE = 128

N_META = 16
ATTN_WIDTH = D_MODEL // 2
HEAD_DIM = 128
N_HEADS = ATTN_WIDTH // HEAD_DIM
N_KV_HEADS = N_HEADS // 4
GQA_GROUP = N_HEADS // N_KV_HEADS
KV_WIDTH = N_KV_HEADS * HEAD_DIM
WINDOW = 128
BLOCK = 128
ROT_DIM = HEAD_DIM // 4
ROPE_THETA = 500000.0
SSM_WIDTH = D_MODEL // 4
SSM_GROUP_SIZE = 16
SSM_GROUPS = SSM_WIDTH // SSM_GROUP_SIZE
SSM_STATE = 64
POOL_WIDTH = D_MODEL // 4
POOL_WINDOWS = (2, 4, 8, 16)
POOL_GROUP = POOL_WIDTH // len(POOL_WINDOWS)
POOL_BUF = max(POOL_WINDOWS) - 1
MIX_WIDTH = ATTN_WIDTH + SSM_WIDTH + POOL_WIDTH
IN_WIDTH = ATTN_WIDTH + 2 * KV_WIDTH + SSM_WIDTH + POOL_WIDTH
D_FF = 4 * D_MODEL
EPS = 1e-6

kernel_name = "hymba_swa_s5_pool_decoder_step"


def rms_norm(x, g):
    xf = x.astype(jnp.float32)
    y = xf * lax.rsqrt(jnp.mean(xf * xf, axis=-1, keepdims=True) + EPS)
    return (y * g.astype(jnp.float32)).astype(x.dtype)


def rope(x, pos):
    half = ROT_DIM // 2
    inv = ROPE_THETA ** (-jnp.arange(0, ROT_DIM, 2, dtype=jnp.float32) / ROT_DIM)
    ang = pos.astype(jnp.float32)[:, None] * inv
    cos = jnp.cos(ang)[:, None, :]
    sin = jnp.sin(ang)[:, None, :]
    xf = x.astype(jnp.float32)
    x1, x2 = xf[..., :half], xf[..., half:ROT_DIM]
    out = jnp.concatenate([x1 * cos - x2 * sin, x2 * cos + x1 * sin, xf[..., ROT_DIM:]], axis=-1)
    return out.astype(x.dtype)


def sink_attend(q, k, v, mask, sink):
    s = jnp.einsum('...qkgd,...skd->...kgqs', q, k).astype(jnp.float32) * (HEAD_DIM ** -0.5)
    s = jnp.where(mask[..., None, None, :, :], s, -jnp.inf)
    sk = sink.astype(jnp.float32)[:, :, None, None]
    m = jnp.maximum(jnp.max(s, axis=-1, keepdims=True), sk)
    p = jnp.exp(s - m)
    denom = jnp.sum(p, axis=-1, keepdims=True) + jnp.exp(sk - m)
    return jnp.einsum('...kgqs,...skd->...qkgd', (p / denom).astype(v.dtype), v)


def attn_prompt(q, k, v, sink):
    bsz, L = q.shape[0], q.shape[1]
    front = (-N_META) % BLOCK
    back = (-(front + L)) % BLOCK
    pad = lambda t: jnp.pad(t, ((0, 0), (front, back), (0, 0), (0, 0)))
    qp, kp, vp = pad(q), pad(k), pad(v)
    Lp = front + L + back
    nb = Lp // BLOCK
    qb = qp.reshape(bsz, nb, BLOCK, N_KV_HEADS, GQA_GROUP, HEAD_DIM)
    kb = kp.reshape(bsz, nb, BLOCK, N_KV_HEADS, HEAD_DIM)
    vb = vp.reshape(bsz, nb, BLOCK, N_KV_HEADS, HEAD_DIM)

    def band(t):
        prev = jnp.pad(t, ((0, 0), (1, 0), (0, 0), (0, 0), (0, 0)))[:, :-1]
        return jnp.concatenate([prev, t], axis=2)

    qpos = (jnp.arange(Lp) - front).reshape(nb, BLOCK)
    kpos = qpos[:, :1] - BLOCK + jnp.arange(2 * BLOCK)[None, :]
    diff = qpos[:, :, None] - kpos[:, None, :]
    mask = (diff >= 0) & (diff <= WINDOW) & (kpos[:, None, :] >= 0)
    o = sink_attend(qb, band(kb), band(vb), mask[None], sink)
    return o.reshape(bsz, Lp, ATTN_WIDTH)[:, front:front + L]


def attn_sample(q, k, v, k_buf, v_buf, sink):
    bsz, T = q.shape[0], q.shape[1]
    wb = k_buf.shape[1]
    kk = jnp.concatenate([k_buf.astype(k.dtype), k], axis=1)
    vv = jnp.concatenate([v_buf.astype(v.dtype), v], axis=1)
    qpos = PAST_LEN + jnp.arange(T)
    kpos = PAST_LEN - wb + jnp.arange(wb + T)
    diff = qpos[:, None] - kpos[None, :]
    mask = (diff >= 0) & (diff <= WINDOW)
    o = sink_attend(q.reshape(bsz, T, N_KV_HEADS, GQA_GROUP, HEAD_DIM), kk, vv, mask[None], sink)
    return o.reshape(bsz, T, ATTN_WIDTH), kk[:, -wb:], vv[:, -wb:]


def _complex_combine(e1, e2):
    a1r, a1i, b1r, b1i = e1
    a2r, a2i, b2r, b2i = e2
    return (a2r * a1r - a2i * a1i, a2r * a1i + a2i * a1r,
            a2r * b1r - a2i * b1i + b2r, a2r * b1i + a2i * b1r + b2i)


def ssm_mix(u, A_re, A_im, log_dt, B_re, B_im, C_re, C_im, D_skip, w_glu, b_glu, h0=None):
    bsz, T = u.shape[0], u.shape[1]
    uf = u.astype(jnp.float32)
    ug = uf.reshape(bsz, T, SSM_GROUPS, SSM_GROUP_SIZE)
    dt = jnp.exp(log_dt.astype(jnp.float32))[:, None]
    ar, ai = A_re.astype(jnp.float32), A_im.astype(jnp.float32)
    mag = jnp.exp(dt * ar)
    abr, abi = mag * jnp.cos(dt * ai), mag * jnp.sin(dt * ai)
    den = ar * ar + ai * ai
    fr = ((abr - 1.0) * ar + abi * ai) / den
    fi = (abi * ar - (abr - 1.0) * ai) / den
    br, bi = B_re.astype(jnp.float32), B_im.astype(jnp.float32)
    bbr = fr[..., None] * br - fi[..., None] * bi
    bbi = fr[..., None] * bi + fi[..., None] * br
    xr = jnp.einsum('btgc,gnc->btgn', ug, bbr)
    xi = jnp.einsum('btgc,gnc->btgn', ug, bbi)
    car, cai, hr, hi = lax.associative_scan(
        _complex_combine,
        (jnp.broadcast_to(abr, xr.shape), jnp.broadcast_to(abi, xr.shape), xr, xi), axis=1)
    if h0 is not None:
        h0r = h0[0].astype(jnp.float32)[:, None]
        h0i = h0[1].astype(jnp.float32)[:, None]
        hr, hi = hr + car * h0r - cai * h0i, hi + car * h0i + cai * h0r
    y = (jnp.einsum('gcn,btgn->btgc', C_re.astype(jnp.float32), hr)
         - jnp.einsum('gcn,btgn->btgc', C_im.astype(jnp.float32), hi))
    y = y.reshape(bsz, T, SSM_WIDTH) + D_skip.astype(jnp.float32) * uf
    z = jax.nn.gelu(y)
    out = z * jax.nn.sigmoid(z @ w_glu.astype(jnp.float32) + b_glu.astype(jnp.float32))
    return out.astype(u.dtype), hr[:, -1], hi[:, -1]


def pool_mix(xp, w_pool, pool_scale, prev=None):
    T = xp.shape[1]
    ext = xp if prev is None else jnp.concatenate([prev.astype(xp.dtype), xp], axis=1)
    n_prev = ext.shape[1] - T
    cs = jnp.pad(jnp.cumsum(ext.astype(jnp.float32), axis=1), ((0, 0), (1, 0), (0, 0)))
    hi = n_prev + jnp.arange(T) + 1
    outs = []
    for g, w in enumerate(POOL_WINDOWS):
        lo = jnp.maximum(hi - w, 0)
        sl = slice(g * POOL_GROUP, (g + 1) * POOL_GROUP)
        csg = cs[..., sl]
        mean = (csg[:, hi] - csg[:, lo]) / (hi - lo).astype(jnp.float32)[:, None]
        outs.append(jnp.einsum('btc,cd->btd', mean - xp[..., sl].astype(jnp.float32),
                               w_pool[g].astype(jnp.float32)))
    y = jnp.concatenate(outs, axis=-1) * pool_scale.astype(jnp.float32)
    return y.astype(xp.dtype), ext[:, -POOL_BUF:]


def trunk_layer(x, pos, lp, past):
    bsz, T = x.shape[0], x.shape[1]
    h = rms_norm(x, lp['g_mix'])
    proj = h @ lp['w_in']
    o1 = ATTN_WIDTH
    o2 = o1 + KV_WIDTH
    o3 = o2 + KV_WIDTH
    o4 = o3 + SSM_WIDTH
    q = proj[..., :o1].reshape(bsz, T, N_HEADS, HEAD_DIM)
    k = proj[..., o1:o2].reshape(bsz, T, N_KV_HEADS, HEAD_DIM)
    v = proj[..., o2:o3].reshape(bsz, T, N_KV_HEADS, HEAD_DIM)
    u = proj[..., o3:o4]
    xpool = proj[..., o4:]
    q = rope(rms_norm(q, lp['g_q']), pos)
    k = rope(rms_norm(k, lp['g_k']), pos)
    sink = lp['sinks'].reshape(N_KV_HEADS, GQA_GROUP)
    ssm_args = (lp['A_re'], lp['A_im'], lp['log_dt'], lp['B_re'], lp['B_im'], lp['C_re'],
                lp['C_im'], lp['D_skip'], lp['w_glu'], lp['b_glu'])
    if past is None:
        a = attn_prompt(q, k, v, sink)
        nw = min(WINDOW, T)
        new_k, new_v = k[:, -nw:], v[:, -nw:]
        s, hr, hi = ssm_mix(u, *ssm_args)
        pl, pbuf = pool_mix(xpool, lp['w_pool'], lp['pool_scale'])
    else:
        k_buf, v_buf, h_re, h_im, p_buf = past
        a, new_k, new_v = attn_sample(q, k, v, k_buf, v_buf, sink)
        s, hr, hi = ssm_mix(u, *ssm_args, h0=(h_re, h_im))
        pl, pbuf = pool_mix(xpool, lp['w_pool'], lp['pool_scale'], p_buf)
    mix = jnp.concatenate([rms_norm(a, lp['g_out_attn']), rms_norm(s, lp['g_out_ssm']),
                           rms_norm(pl, lp['g_out_pool'])], axis=-1)
    x = x + mix @ lp['w_out']
    h2 = rms_norm(x, lp['g_ffn'])
    x = x + jnp.square(jax.nn.relu(h2 @ lp['w_ff1'])) @ lp['w_ff2']
    return x, (new_k, new_v, hr, hi, pbuf)


def setup_inputs(seed: int = 0) -> dict:
    key = jax.random.key(seed)
    ks = jax.random.split(key, 40)
    nrm = lambda i, shape: jax.random.normal(ks[i], shape, jnp.float32)
    win_buf = min(WINDOW, PAST_LEN)
    return {
        'x_prompt': nrm(0, (BATCH, SEQ, D_MODEL)),
        'x_sample': nrm(1, (DEC_BATCH, DEC_SEQ, D_MODEL)),
        'cache_k': nrm(2, (DEPTH, DEC_BATCH, win_buf, N_KV_HEADS, HEAD_DIM)),
        'cache_v': nrm(3, (DEPTH, DEC_BATCH, win_buf, N_KV_HEADS, HEAD_DIM)),
        'state_ssm_re': 0.1 * nrm(4, (DEPTH, DEC_BATCH, SSM_GROUPS, SSM_STATE)),
        'state_ssm_im': 0.1 * nrm(5, (DEPTH, DEC_BATCH, SSM_GROUPS, SSM_STATE)),
        'state_pool': nrm(6, (DEPTH, DEC_BATCH, POOL_BUF, POOL_WIDTH)),
        'meta_tokens': nrm(7, (N_META, D_MODEL)),
        'g_mix': 1.0 + 0.01 * nrm(8, (DEPTH, D_MODEL)),
        'w_in': nrm(9, (DEPTH, D_MODEL, IN_WIDTH)) * D_MODEL ** -0.5,
        'g_q': 1.0 + 0.01 * nrm(10, (DEPTH, HEAD_DIM)),
        'g_k': 1.0 + 0.01 * nrm(11, (DEPTH, HEAD_DIM)),
        'sinks': nrm(12, (DEPTH, N_HEADS)),
        'A_re': -0.5 + 0.01 * nrm(13, (DEPTH, SSM_GROUPS, SSM_STATE)),
        'A_im': jnp.pi * jnp.arange(SSM_STATE, dtype=jnp.float32) + 0.01 * nrm(14, (DEPTH, SSM_GROUPS, SSM_STATE)),
        'log_dt': jax.random.uniform(ks[15], (DEPTH, SSM_GROUPS), jnp.float32, math.log(1e-3), math.log(1e-1)),
        'B_re': nrm(16, (DEPTH, SSM_GROUPS, SSM_STATE, SSM_GROUP_SIZE)) * (2 * SSM_GROUP_SIZE) ** -0.5,
        'B_im': nrm(17, (DEPTH, SSM_GROUPS, SSM_STATE, SSM_GROUP_SIZE)) * (2 * SSM_GROUP_SIZE) ** -0.5,
        'C_re': nrm(18, (DEPTH, SSM_GROUPS, SSM_GROUP_SIZE, SSM_STATE)) * (2 * SSM_STATE) ** -0.5,
        'C_im': nrm(19, (DEPTH, SSM_GROUPS, SSM_GROUP_SIZE, SSM_STATE)) * (2 * SSM_STATE) ** -0.5,
        'D_skip': nrm(20, (DEPTH, SSM_WIDTH)),
        'w_glu': nrm(21, (DEPTH, SSM_WIDTH, SSM_WIDTH)) * SSM_WIDTH ** -0.5,
        'b_glu': 0.01 * nrm(22, (DEPTH, SSM_WIDTH)),
        'w_pool': nrm(23, (DEPTH, len(POOL_WINDOWS), POOL_GROUP, POOL_GROUP)) * POOL_GROUP ** -0.5,
        'pool_scale': 1.0 + 0.1 * nrm(24, (DEPTH, POOL_WIDTH)),
        'g_out_attn': 1.0 + 0.01 * nrm(25, (DEPTH, ATTN_WIDTH)),
        'g_out_ssm': 1.0 + 0.01 * nrm(26, (DEPTH, SSM_WIDTH)),
        'g_out_pool': 1.0 + 0.01 * nrm(27, (DEPTH, POOL_WIDTH)),
        'w_out': nrm(28, (DEPTH, MIX_WIDTH, D_MODEL)) * MIX_WIDTH ** -0.5,
        'g_ffn': 1.0 + 0.01 * nrm(29, (DEPTH, D_MODEL)),
        'w_ff1': nrm(30, (DEPTH, D_MODEL, D_FF)) * D_MODEL ** -0.5,
        'w_ff2': nrm(31, (DEPTH, D_FF, D_MODEL)) * D_FF ** -0.5,
    }


def reference(x_prompt, x_sample, cache_k, cache_v, state_ssm_re, state_ssm_im, state_pool,
              meta_tokens, g_mix, w_in, g_q, g_k, sinks, A_re, A_im, log_dt, B_re, B_im,
              C_re, C_im, D_skip, w_glu, b_glu, w_pool, pool_scale, g_out_attn, g_out_ssm,
              g_out_pool, w_out, g_ffn, w_ff1, w_ff2):
    meta = jnp.broadcast_to(meta_tokens.astype(x_prompt.dtype)[None],
                            (x_prompt.shape[0], N_META, D_MODEL))
    xp = jnp.concatenate([meta, x_prompt], axis=1)
    xs = x_sample
    pos_p = jnp.arange(xp.shape[1])
    pos_s = PAST_LEN + jnp.arange(xs.shape[1])
    states_p, states_s = [], []
    for l in range(DEPTH):
        lp = dict(g_mix=g_mix[l], w_in=w_in[l], g_q=g_q[l], g_k=g_k[l], sinks=sinks[l],
                  A_re=A_re[l], A_im=A_im[l], log_dt=log_dt[l], B_re=B_re[l], B_im=B_im[l],
                  C_re=C_re[l], C_im=C_im[l], D_skip=D_skip[l], w_glu=w_glu[l], b_glu=b_glu[l],
                  w_pool=w_pool[l], pool_scale=pool_scale[l], g_out_attn=g_out_attn[l],
                  g_out_ssm=g_out_ssm[l], g_out_pool=g_out_pool[l], w_out=w_out[l],
                  g_ffn=g_ffn[l], w_ff1=w_ff1[l], w_ff2=w_ff2[l])
        xp, st_p = trunk_layer(xp, pos_p, lp, None)
        xs, st_s = trunk_layer(xs, pos_s, lp, (cache_k[l], cache_v[l], state_ssm_re[l],
                                               state_ssm_im[l], state_pool[l]))
        states_p.append(st_p)
        states_s.append(st_s)
    np_ = [jnp.stack(s) for s in zip(*states_p)]
    ns_ = [jnp.stack(s) for s in zip(*states_s)]
    y_prompt = xp[:, N_META:]
    return (y_prompt, xs, np_[0], np_[1], np_[2], np_[3], np_[4],
            ns_[0], ns_[1], ns_[2], ns_[3], ns_[4])
```

```python
import functools
import math

import jax
import jax.numpy as jnp
from jax.experimental import pallas as pl
from jax.experimental.pallas import tpu as pltpu

D_MODEL = 2048
N_META = 16
HEAD_DIM = 128
N_HEADS = 8
N_KV_HEADS = 2
GQA_GROUP = 4
ATTN_WIDTH = 1024
KV_WIDTH = 256
WINDOW = 128
BLOCK = 128
ROT_DIM = 32
ROPE_THETA = 500000.0
SSM_WIDTH = 512
SSM_GROUP_SIZE = 16
SSM_GROUPS = 32
SSM_STATE = 64
POOL_WIDTH = 512
POOL_WINDOWS = (2, 4, 8, 16)
POOL_GROUP = 128
POOL_BUF = 15
IN_WIDTH = 2560
D_FF = 8192
EPS = 1e-6
PAST_LEN = 16384

FRONT_PAD = (-N_META) % BLOCK
LANES = 128
SUBLANES = 8
SSM_LANE_BLOCKS = SSM_WIDTH // LANES
SSM_BLOCK_STATES = (LANES // SSM_GROUP_SIZE) * SSM_STATE
VMEM_LIMIT = 56 * 1024 * 1024

BF16 = jnp.bfloat16
F32 = jnp.float32


def _params(*semantics):
    return pltpu.CompilerParams(dimension_semantics=semantics, vmem_limit_bytes=VMEM_LIMIT)


def _rms(x, g):
    return x * jax.lax.rsqrt(jnp.mean(x * x, axis=-1, keepdims=True) + EPS) * g


def _full(shape):
    n = len(shape)
    return pl.BlockSpec(shape, lambda *_: (0,) * n)


def _inproj_kernel(x_ref, g_ref, w_ref, gq_ref, gk_ref, rc_ref, rs1_ref, rs2_ref,
                   q_ref, k_ref, v_ref, u_ref, xp_ref):
    h = _rms(x_ref[...], g_ref[...]).astype(BF16)
    proj = jnp.dot(h, w_ref[...], preferred_element_type=F32)
    rc, rs1, rs2 = rc_ref[...], rs1_ref[...], rs2_ref[...]

    def head(t, g):
        t = _rms(t, g)
        return t * rc + pltpu.roll(t, 16, 1) * rs1 + pltpu.roll(t, LANES - 16, 1) * rs2

    for hd in range(N_HEADS):
        sl = slice(hd * HEAD_DIM, (hd + 1) * HEAD_DIM)
        q_ref[:, sl] = head(proj[:, sl], gq_ref[...])
    for hd in range(N_KV_HEADS):
        sl = slice(hd * HEAD_DIM, (hd + 1) * HEAD_DIM)
        k_ref[:, sl] = head(proj[:, ATTN_WIDTH + hd * HEAD_DIM:ATTN_WIDTH + (hd + 1) * HEAD_DIM], gk_ref[...])
    o2 = ATTN_WIDTH + KV_WIDTH
    o3 = o2 + KV_WIDTH
    o4 = o3 + SSM_WIDTH
    v_ref[...] = proj[:, o2:o3]
    u_ref[...] = proj[:, o3:o4]
    xp_ref[...] = proj[:, o4:]


def _inproj(x, g, w, gq, gk, rope, tm, tiles_per_rope):
    R = x.shape[0]
    row = lambda i: (i, 0)
    rrow = lambda i: (i % tiles_per_rope, 0)
    outs = [(ATTN_WIDTH, F32), (KV_WIDTH, F32), (KV_WIDTH, F32), (SSM_WIDTH, F32), (POOL_WIDTH, F32)]
    return pl.pallas_call(
        _inproj_kernel,
        grid=(R // tm,),
        in_specs=[pl.BlockSpec((tm, D_MODEL), row), _full((1, D_MODEL)), _full((D_MODEL, IN_WIDTH)),
                  _full((1, HEAD_DIM)), _full((1, HEAD_DIM)),
                  pl.BlockSpec((tm, LANES), rrow), pl.BlockSpec((tm, LANES), rrow),
                  pl.BlockSpec((tm, LANES), rrow)],
        out_specs=[pl.BlockSpec((tm, w_), row) for w_, _ in outs],
        out_shape=[jax.ShapeDtypeStruct((R, w_), dt) for w_, dt in outs],
        compiler_params=_params("parallel"),
        name="inproj",
    )(x, g, w, gq, gk, *rope)


def _attn_kernel(sink_ref, q_ref, kp_ref, kc_ref, vp_ref, vc_ref, g_ref, o_ref):
    i = pl.program_id(1)
    rows = GQA_GROUP * BLOCK
    r = jax.lax.broadcasted_iota(jnp.int32, (rows, 2 * BLOCK), 0) & (BLOCK - 1)
    c = jax.lax.broadcasted_iota(jnp.int32, (rows, 2 * BLOCK), 1)
    diff = BLOCK + r - c
    krow = (i - 1) * BLOCK + c
    mask = (diff >= 0) & (diff <= WINDOW) & (krow >= FRONT_PAD)
    rgrp = jax.lax.broadcasted_iota(jnp.int32, (rows, 1), 0) // BLOCK
    outs = []
    for kh in range(N_KV_HEADS):
        ksl = slice(kh * HEAD_DIM, (kh + 1) * HEAD_DIM)
        qh = jnp.concatenate(
            [q_ref[:, (kh * GQA_GROUP + g) * HEAD_DIM:(kh * GQA_GROUP + g + 1) * HEAD_DIM]
             for g in range(GQA_GROUP)], axis=0).astype(BF16)
        kk = jnp.concatenate([kp_ref[:, ksl], kc_ref[:, ksl]], axis=0).astype(BF16)
        vv = jnp.concatenate([vp_ref[:, ksl], vc_ref[:, ksl]], axis=0).astype(BF16)
        s = jax.lax.dot_general(qh, kk, (((1,), (1,)), ((), ())),
                                preferred_element_type=F32) * (HEAD_DIM ** -0.5)
        s = jnp.where(mask, s, -jnp.inf)
        sk = jnp.zeros((rows, 1), F32)
        for g in range(GQA_GROUP):
            sk = jnp.where(rgrp == g, sink_ref[kh * GQA_GROUP + g], sk)
        m = jnp.maximum(jnp.max(s, axis=-1, keepdims=True), sk)
        p = jnp.exp(s - m)
        denom = jnp.sum(p, axis=-1, keepdims=True) + jnp.exp(sk - m)
        o = jnp.dot(p.astype(BF16), vv, preferred_element_type=F32) / denom
        outs.extend(o[g * BLOCK:(g + 1) * BLOCK] for g in range(GQA_GROUP))
    a = jnp.concatenate(outs, axis=1)
    o_ref[...] = _rms(a, g_ref[...])


def _attn(sinks, q, k, v, g):
    B, P, _ = q.shape
    cur = lambda b, i, s: (b, i, 0)
    prev = lambda b, i, s: (b, jnp.maximum(i - 1, 0), 0)
    kv = lambda im: pl.BlockSpec((None, BLOCK, KV_WIDTH), im)
    return pl.pallas_call(
        _attn_kernel,
        grid_spec=pltpu.PrefetchScalarGridSpec(
            num_scalar_prefetch=1,
            grid=(B, P // BLOCK),
            in_specs=[pl.BlockSpec((None, BLOCK, ATTN_WIDTH), cur), kv(prev), kv(cur), kv(prev), kv(cur),
                      pl.BlockSpec((1, ATTN_WIDTH), lambda b, i, s: (0, 0))],
            out_specs=pl.BlockSpec((None, BLOCK, ATTN_WIDTH), cur),
        ),
        out_shape=jax.ShapeDtypeStruct((B, P, ATTN_WIDTH), F32),
        compiler_params=_params("parallel", "parallel"),
        name="attn_prompt",
    )(sinks, q, k, k, v, v, g)


def _ssm_params_kernel(ar_ref, ai_ref, ldt_ref, br_ref, bi_ref, tr_ref, ti_ref, bbr_ref, bbi_ref):
    ar, ai = ar_ref[...], ai_ref[...]
    dt = jnp.exp(ldt_ref[...])
    kk = (jax.lax.broadcasted_iota(jnp.int32, (1, SUBLANES), 1) + 1).astype(F32)
    mag = jnp.exp(dt * ar * kk)
    ang = dt * ai * kk
    tr = mag * jnp.cos(ang)
    ti = mag * jnp.sin(ang)
    tr_ref[...] = tr
    ti_ref[...] = ti
    abr, abi = tr[:, 0:1], ti[:, 0:1]
    den = ar * ar + ai * ai
    fr = ((abr - 1.0) * ar + abi * ai) / den
    fi = (abi * ar - (abr - 1.0) * ai) / den
    br, bi = br_ref[...], bi_ref[...]
    bbr_ref[...] = fr * br - fi * bi
    bbi_ref[...] = fr * bi + fi * br


def _ssm_params(A_re, A_im, log_dt, B_re, B_im, C_re, C_im):
    n = SSM_GROUPS * SSM_STATE
    col = lambda t: t.astype(F32).reshape(n, 1)
    ldt = jnp.broadcast_to(log_dt.astype(F32)[:, None], (SSM_GROUPS, SSM_STATE)).reshape(n, 1)
    shapes = [(n, SUBLANES), (n, SUBLANES), (n, SSM_GROUP_SIZE), (n, SSM_GROUP_SIZE)]
    tr, ti, bbr, bbi = pl.pallas_call(
        _ssm_params_kernel,
        out_shape=[jax.ShapeDtypeStruct(s, F32) for s in shapes],
        name="ssm_params",
    )(col(A_re), col(A_im), ldt, B_re.astype(F32).reshape(n, SSM_GROUP_SIZE),
      B_im.astype(F32).reshape(n, SSM_GROUP_SIZE))
    J, G8 = SSM_LANE_BLOCKS, LANES // SSM_GROUP_SIZE

    def lanes(t):
        return t.T.reshape(SUBLANES, J, SSM_BLOCK_STATES)

    pw = jnp.concatenate([lanes(tr), lanes(ti)], axis=-1).reshape(SUBLANES, J * 2 * SSM_BLOCK_STATES)
    eye = jnp.eye(G8, dtype=F32)

    def bdiag(t):
        t = t.reshape(J, G8, SSM_STATE, SSM_GROUP_SIZE).transpose(0, 1, 3, 2)
        t = t[:, :, :, None, :] * eye[None, :, None, :, None]
        return t.reshape(J, LANES, SSM_BLOCK_STATES)

    bblk = jnp.concatenate([bdiag(bbr), bdiag(bbi)], axis=-1).astype(BF16)

    def cdiag(t):
        t = t.astype(F32).reshape(J, G8, SSM_GROUP_SIZE, SSM_STATE).transpose(0, 1, 3, 2)
        t = t[:, :, :, None, :] * eye[None, :, None, :, None]
        return t.reshape(J, SSM_BLOCK_STATES, LANES)

    cblk = jnp.concatenate([cdiag(C_re), -cdiag(C_im)], axis=1).astype(BF16)
    return pw, bblk, cblk


def _ssm_tail(y, u, d_ref, wg_ref, bg_ref, g_ref):
    y = y + d_ref[...] * u
    z = jax.nn.gelu(y)
    gate = jax.nn.sigmoid(jnp.dot(z.astype(BF16), wg_ref[...], preferred_element_type=F32) + bg_ref[...])
    return _rms(z * gate, g_ref[...])


def _ssm_kernel(u_ref, pw_ref, bblk_ref, cblk_ref, d_ref, wg_ref, bg_ref, g_ref,
                o_ref, st_ref, x_scr, carry_scr):
    t = pl.program_id(1)
    tm = u_ref.shape[0]
    S = SSM_BLOCK_STATES

    @pl.when(t == 0)
    def _():
        carry_scr[...] = jnp.zeros_like(carry_scr)

    u = u_ref[...]
    ub = u.astype(BF16)
    srow = jax.lax.broadcasted_iota(jnp.int32, (SUBLANES, S), 0)
    ys = []
    for j in range(SSM_LANE_BLOCKS):
        x_scr[...] = jnp.dot(ub[:, j * LANES:(j + 1) * LANES], bblk_ref[j], preferred_element_type=F32)
        base = j * 2 * S
        pr = pw_ref[:, base:base + S]
        pi = pw_ref[:, base + S:base + 2 * S]

        def shifted(k):
            zr = jnp.where(srow >= k, jnp.broadcast_to(pr[k - 1:k], (SUBLANES, S)), 0.0)
            zi = jnp.where(srow >= k, jnp.broadcast_to(pi[k - 1:k], (SUBLANES, S)), 0.0)
            return zr, zi

        steps = [(k, *shifted(k)) for k in (1, 2, 4)]

        def body(rt, carry):
            cr, ci = carry
            r0 = pl.multiple_of(rt * SUBLANES, SUBLANES)
            hr = x_scr[pl.ds(r0, SUBLANES), 0:S]
            hi = x_scr[pl.ds(r0, SUBLANES), S:2 * S]
            for k, zr, zi in steps:
                sr = pltpu.roll(hr, k, 0)
                si = pltpu.roll(hi, k, 0)
                hr, hi = hr + zr * sr - zi * si, hi + zr * si + zi * sr
            hr, hi = hr + pr * cr - pi * ci, hi + pr * ci + pi * cr
            x_scr[pl.ds(r0, SUBLANES), 0:S] = hr
            x_scr[pl.ds(r0, SUBLANES), S:2 * S] = hi
            last = SUBLANES - 1
            return (jnp.broadcast_to(hr[last:], (SUBLANES, S)), jnp.broadcast_to(hi[last:], (SUBLANES, S)))

        c0 = (carry_scr[:, base:base + S], carry_scr[:, base + S:base + 2 * S])
        cr, ci = jax.lax.fori_loop(0, tm // SUBLANES, body, c0, unroll=2)
        carry_scr[:, base:base + S] = cr
        carry_scr[:, base + S:base + 2 * S] = ci
        ys.append(jnp.dot(x_scr[...].astype(BF16), cblk_ref[j], preferred_element_type=F32))
    o_ref[...] = _ssm_tail(jnp.concatenate(ys, axis=1), u, d_ref, wg_ref, bg_ref, g_ref)
    st_ref[...] = carry_scr[...]


def _ssm(u, pw, bblk, cblk, d, wg, bg, g, tm):
    B, P, _ = u.shape
    S2 = SSM_LANE_BLOCKS * 2 * SSM_BLOCK_STATES
    return pl.pallas_call(
        _ssm_kernel,
        grid=(B, P // tm),
        in_specs=[pl.BlockSpec((None, tm, SSM_WIDTH), lambda b, t: (b, t, 0)),
                  _full((SUBLANES, S2)), _full(bblk.shape), _full(cblk.shape),
                  _full((1, SSM_WIDTH)), _full((SSM_WIDTH, SSM_WIDTH)), _full((1, SSM_WIDTH)),
                  _full((1, SSM_WIDTH))],
        out_specs=[pl.BlockSpec((None, tm, SSM_WIDTH), lambda b, t: (b, t, 0)),
                   pl.BlockSpec((None, SUBLANES, S2), lambda b, t: (b, 0, 0))],
        out_shape=[jax.ShapeDtypeStruct((B, P, SSM_WIDTH), F32),
                   jax.ShapeDtypeStruct((B, SUBLANES, S2), F32)],
        scratch_shapes=[pltpu.VMEM((tm, 2 * SSM_BLOCK_STATES), F32),
                        pltpu.VMEM((SUBLANES, S2), F32)],
        compiler_params=_params("parallel", "arbitrary"),
        name="ssm_prompt",
    )(u, pw, bblk, cblk, d, wg, bg, g)


def _state_to_lanes(h):
    return h.astype(F32).reshape(h.shape[0], SSM_LANE_BLOCKS, SSM_BLOCK_STATES)


def _state_from_lanes(s):
    s = s.reshape(s.shape[0], SSM_LANE_BLOCKS, 2, SSM_BLOCK_STATES)
    return (s[:, :, 0].reshape(-1, SSM_GROUPS, SSM_STATE), s[:, :, 1].reshape(-1, SSM_GROUPS, SSM_STATE))


def _pool_tail(d_groups, w_ref, sc_ref, g_ref):
    y = jnp.concatenate(
        [jnp.dot(d.astype(BF16), w_ref[gi], preferred_element_type=F32) for gi, d in enumerate(d_groups)], axis=1)
    return _rms(y * sc_ref[...], g_ref[...])


def _pool_kernel(x_ref, prev_ref, w_ref, sc_ref, g_ref, o_ref):
    t = pl.program_id(1)
    tm = x_ref.shape[0]
    halo = prev_ref.shape[0]
    x = x_ref[...]
    prev = jnp.where(t > 0, prev_ref[...], 0.0)
    xe = jnp.concatenate([prev, x], axis=0)
    pos = t * tm + jax.lax.broadcasted_iota(jnp.int32, (tm, 1), 0) - FRONT_PAD
    ds = []
    for gi, w in enumerate(POOL_WINDOWS):
        s = xe[:, gi * POOL_GROUP:(gi + 1) * POOL_GROUP]
        k = 1
        while k < w:
            s = s + pltpu.roll(s, k, 0)
            k *= 2
        cnt = jnp.clip(pos + 1, 1, w).astype(F32)
        ds.append(s[halo:] / cnt - x[:, gi * POOL_GROUP:(gi + 1) * POOL_GROUP])
    o_ref[...] = _pool_tail(ds, w_ref, sc_ref, g_ref)


def _pool(xp, w, sc, g, tm):
    B, P, _ = xp.shape
    halo = 16
    r = tm // halo
    return pl.pallas_call(
        _pool_kernel,
        grid=(B, P // tm),
        in_specs=[pl.BlockSpec((None, tm, POOL_WIDTH), lambda b, t: (b, t, 0)),
                  pl.BlockSpec((None, halo, POOL_WIDTH), lambda b, t: (b, jnp.maximum(t * r - 1, 0), 0)),
                  _full(w.shape), _full((1, POOL_WIDTH)), _full((1, POOL_WIDTH))],
        out_specs=pl.BlockSpec((None, tm, POOL_WIDTH), lambda b, t: (b, t, 0)),
        out_shape=jax.ShapeDtypeStruct((B, P, POOL_WIDTH), F32),
        compiler_params=_params("parallel", "parallel"),
        name="pool_prompt",
    )(xp, xp, w, sc, g)


def _attn_sample_kernel(q_ref, kn_ref, vn_ref, kc_ref, vc_ref, sink_ref, g_ref, o_ref):
    outs = []
    for kh in range(N_KV_HEADS):
        ksl = slice(kh * HEAD_DIM, (kh + 1) * HEAD_DIM)
        qh = jnp.concatenate(
            [q_ref[:, (kh * GQA_GROUP + g) * HEAD_DIM:(kh * GQA_GROUP + g + 1) * HEAD_DIM]
             for g in range(GQA_GROUP)], axis=0)
        kn = kn_ref[:, ksl]
        vn = vn_ref[:, ksl]
        scale = HEAD_DIM ** -0.5
        sc = jax.lax.dot_general(qh.astype(BF16), kc_ref[:, ksl].astype(BF16), (((1,), (1,)), ((), ())),
                                 preferred_element_type=F32) * scale
        sn = jnp.sum(qh.astype(BF16).astype(F32) * kn.astype(BF16).astype(F32), axis=-1, keepdims=True) * scale
        sk = sink_ref[kh]
        m = jnp.maximum(jnp.maximum(jnp.max(sc, axis=-1, keepdims=True), sn), sk)
        pc = jnp.exp(sc - m)
        pn = jnp.exp(sn - m)
        denom = jnp.sum(pc, axis=-1, keepdims=True) + pn + jnp.exp(sk - m)
        o = jnp.dot(pc.astype(BF16), vc_ref[:, ksl].astype(BF16), preferred_element_type=F32)
        o = (o + pn.astype(BF16).astype(F32) * vn.astype(BF16).astype(F32)) / denom
        outs.extend(o[g:g + 1] for g in range(GQA_GROUP))
    a = jnp.concatenate(outs, axis=1)
    o_ref[...] = _rms(a, g_ref[...])


def _attn_sample(q, kn, vn, kc, vc, sinks, g):
    N = q.shape[0]
    r3 = lambda t: t.reshape(N, 1, t.shape[-1])
    row = lambda w: pl.BlockSpec((None, 1, w), lambda b: (b, 0, 0))
    cache = pl.BlockSpec((None, WINDOW, KV_WIDTH), lambda b: (b, 0, 0))
    out = pl.pallas_call(
        _attn_sample_kernel,
        grid=(N,),
        in_specs=[row(ATTN_WIDTH), row(KV_WIDTH), row(KV_WIDTH), cache, cache,
                  _full((N_KV_HEADS, GQA_GROUP, 1)), _full((1, ATTN_WIDTH))],
        out_specs=row(ATTN_WIDTH),
        out_shape=jax.ShapeDtypeStruct((N, 1, ATTN_WIDTH), F32),
        compiler_params=_params("parallel"),
        name="attn_sample",
    )(r3(q), r3(kn), r3(vn), kc, vc, sinks.astype(F32).reshape(N_KV_HEADS, GQA_GROUP, 1), g)
    return out.reshape(N, ATTN_WIDTH)


def _mix_sample_kernel(u_ref, h0_ref, pw_ref, bblk_ref, cblk_ref, d_ref, wg_ref, bg_ref, gs_ref,
                       xp_ref, pb_ref, wp_ref, sc_ref, gp_ref, s_ref, st_ref, p_ref):
    S = SSM_BLOCK_STATES
    u = u_ref[...]
    ub = u.astype(BF16)
    ys = []
    for j in range(SSM_LANE_BLOCKS):
        x = jnp.dot(ub[:, j * LANES:(j + 1) * LANES], bblk_ref[j], preferred_element_type=F32)
        base = j * 2 * S
        ar = pw_ref[0:1, base:base + S]
        ai = pw_ref[0:1, base + S:base + 2 * S]
        h0r = h0_ref[:, base:base + S]
        h0i = h0_ref[:, base + S:base + 2 * S]
        hr = x[:, 0:S] + ar * h0r - ai * h0i
        hi = x[:, S:] + ar * h0i + ai * h0r
        st_ref[:, base:base + S] = hr
        st_ref[:, base + S:base + 2 * S] = hi
        h = jnp.concatenate([hr, hi], axis=1).astype(BF16)
        ys.append(jnp.dot(h, cblk_ref[j], preferred_element_type=F32))
    s_ref[...] = _ssm_tail(jnp.concatenate(ys, axis=1), u, d_ref, wg_ref, bg_ref, gs_ref)

    xp = xp_ref[...]
    ds = []
    for gi, w in enumerate(POOL_WINDOWS):
        gsl = slice(gi * POOL_GROUP, (gi + 1) * POOL_GROUP)
        s = xp[:, gsl]
        for back in range(1, w):
            s = s + pb_ref[POOL_BUF - back][:, gsl]
        ds.append(s / float(w) - xp[:, gsl])
    p_ref[...] = _pool_tail(ds, wp_ref, sc_ref, gp_ref)


def _mix_sample(u, h0, pw, bblk, cblk, d, wg, bg, gs, xp, pbuf, wp, sc, gp):
    N = u.shape[0]
    S2 = SSM_LANE_BLOCKS * 2 * SSM_BLOCK_STATES
    return pl.pallas_call(
        _mix_sample_kernel,
        out_shape=[jax.ShapeDtypeStruct((N, SSM_WIDTH), F32), jax.ShapeDtypeStruct((N, S2), F32),
                   jax.ShapeDtypeStruct((N, POOL_WIDTH), F32)],
        compiler_params=pltpu.CompilerParams(vmem_limit_bytes=VMEM_LIMIT),
        name="mix_sample",
    )(u, h0, pw, bblk, cblk, d, wg, bg, gs, xp, pbuf, wp, sc, gp)


def _outproj_kernel(x_ref, a_ref, s_ref, p_ref, w_ref, o_ref):
    o1 = ATTN_WIDTH
    o2 = o1 + SSM_WIDTH
    acc = x_ref[...]
    acc = acc + jnp.dot(a_ref[...].astype(BF16), w_ref[0:o1, :], preferred_element_type=F32)
    acc = acc + jnp.dot(s_ref[...].astype(BF16), w_ref[o1:o2, :], preferred_element_type=F32)
    acc = acc + jnp.dot(p_ref[...].astype(BF16), w_ref[o2:, :], preferred_element_type=F32)
    o_ref[...] = acc


def _outproj(x, a, s, p, w, tm):
    R = x.shape[0]
    row = lambda i: (i, 0)
    return pl.pallas_call(
        _outproj_kernel,
        grid=(R // tm,),
        in_specs=[pl.BlockSpec((tm, D_MODEL), row), pl.BlockSpec((tm, ATTN_WIDTH), row),
                  pl.BlockSpec((tm, SSM_WIDTH), row), pl.BlockSpec((tm, POOL_WIDTH), row),
                  _full((D_MODEL, D_MODEL))],
        out_specs=pl.BlockSpec((tm, D_MODEL), row),
        out_shape=jax.ShapeDtypeStruct((R, D_MODEL), F32),
        compiler_params=_params("parallel"),
        name="outproj",
    )(x, a, s, p, w)


def _ffn_kernel(x_ref, g_ref, w1_ref, w2_ref, o_ref, h_scr):
    f = pl.program_id(1)

    @pl.when(f == 0)
    def _():
        x = x_ref[...]
        h_scr[...] = _rms(x, g_ref[...]).astype(BF16)
        o_ref[...] = x

    h1 = jnp.dot(h_scr[...], w1_ref[...], preferred_element_type=F32)
    h1 = jnp.square(jnp.maximum(h1, 0.0)).astype(BF16)
    o_ref[...] += jnp.dot(h1, w2_ref[...], preferred_element_type=F32)


def _ffn(x, g, w1, w2, tm, tf):
    R = x.shape[0]
    return pl.pallas_call(
        _ffn_kernel,
        grid=(R // tm, D_FF // tf),
        in_specs=[pl.BlockSpec((tm, D_MODEL), lambda i, f: (i, 0)), _full((1, D_MODEL)),
                  pl.BlockSpec((D_MODEL, tf), lambda i, f: (0, f)),
                  pl.BlockSpec((tf, D_MODEL), lambda i, f: (f, 0))],
        out_specs=pl.BlockSpec((tm, D_MODEL), lambda i, f: (i, 0)),
        out_shape=jax.ShapeDtypeStruct((R, D_MODEL), F32),
        scratch_shapes=[pltpu.VMEM((tm, D_MODEL), BF16)],
        compiler_params=_params("parallel", "arbitrary"),
        name="ffn",
    )(x, g, w1, w2)


def _rope_tables(pos):
    half = ROT_DIM // 2
    inv = ROPE_THETA ** (-jnp.arange(0, ROT_DIM, 2, dtype=F32) / ROT_DIM)
    ang = pos.astype(F32)[:, None] * inv
    cos, sin = jnp.cos(ang), jnp.sin(ang)
    n = pos.shape[0]
    z = jnp.zeros((n, HEAD_DIM - ROT_DIM), F32)
    zh = jnp.zeros((n, half), F32)
    rc = jnp.concatenate([cos, cos, z + 1.0], axis=1)
    rs1 = jnp.concatenate([zh, sin, z], axis=1)
    rs2 = jnp.concatenate([-sin, zh, z], axis=1)
    return rc, rs1, rs2


def kernel(x_prompt, x_sample, cache_k, cache_v, state_ssm_re, state_ssm_im, state_pool, meta_tokens, g_mix, w_in, g_q, g_k, sinks, A_re, A_im, log_dt, B_re, B_im, C_re, C_im, D_skip, w_glu, b_glu, w_pool, pool_scale, g_out_attn, g_out_ssm, g_out_pool, w_out, g_ffn, w_ff1, w_ff2):
    B, T, _ = x_prompt.shape
    N = x_sample.shape[0]
    depth = w_in.shape[0]
    P = FRONT_PAD + N_META + T
    R = B * P
    row = lambda t: t.astype(F32).reshape(1, -1)

    meta = jnp.broadcast_to(meta_tokens.astype(F32)[None], (B, N_META, D_MODEL))
    xp = jnp.concatenate([jnp.zeros((B, FRONT_PAD, D_MODEL), F32), meta, x_prompt], axis=1).reshape(R, D_MODEL)
    xs = x_sample.reshape(N, D_MODEL)

    rope_p = _rope_tables(jnp.arange(P) - FRONT_PAD)
    rope_s = _rope_tables(jnp.full((N,), PAST_LEN))
    tm_p = 3 * BLOCK
    tm_f = 6 * BLOCK
    tf = 512

    outs_p = [[] for _ in range(5)]
    outs_s = [[] for _ in range(5)]
    for l in range(depth):
        wi, wo = w_in[l].astype(BF16), w_out[l].astype(BF16)
        w1, w2 = w_ff1[l].astype(BF16), w_ff2[l].astype(BF16)
        wg, wp = w_glu[l].astype(BF16), w_pool[l].astype(BF16)
        pw, bblk, cblk = _ssm_params(A_re[l], A_im[l], log_dt[l], B_re[l], B_im[l], C_re[l], C_im[l])
        ssm_w = (pw, bblk, cblk, row(D_skip[l]), wg, row(b_glu[l]), row(g_out_ssm[l]))
        pool_w = (wp, row(pool_scale[l]), row(g_out_pool[l]))

        q, k, v, u, xpl = _inproj(xp, row(g_mix[l]), wi, row(g_q[l]), row(g_k[l]), rope_p, tm_p, P // tm_p)
        b3 = lambda t: t.reshape(B, P, t.shape[-1])
        a = _attn(sinks[l].astype(F32), b3(q), b3(k), b3(v), row(g_out_attn[l]))
        s, st = _ssm(b3(u), *ssm_w, tm_p)
        pm = _pool(b3(xpl), *pool_w, tm_p)
        xp = _outproj(xp, a.reshape(R, -1), s.reshape(R, -1), pm.reshape(R, -1), wo, tm_p)
        xp = _ffn(xp, row(g_ffn[l]), w1, w2, tm_f, tf)
        hr, hi = _state_from_lanes(st[:, 0])
        outs_p[0].append(b3(k)[:, P - WINDOW:].reshape(B, WINDOW, N_KV_HEADS, HEAD_DIM))
        outs_p[1].append(b3(v)[:, P - WINDOW:].reshape(B, WINDOW, N_KV_HEADS, HEAD_DIM))
        outs_p[2].append(hr)
        outs_p[3].append(hi)
        outs_p[4].append(b3(xpl)[:, P - POOL_BUF:])

        qs, ks, vs, us, xps = _inproj(xs, row(g_mix[l]), wi, row(g_q[l]), row(g_k[l]), rope_s, N, 1)
        kc = cache_k[l].astype(F32).reshape(N, WINDOW, KV_WIDTH)
        vc = cache_v[l].astype(F32).reshape(N, WINDOW, KV_WIDTH)
        a_s = _attn_sample(qs, ks, vs, kc, vc, sinks[l], row(g_out_attn[l]))
        h0 = jnp.concatenate([_state_to_lanes(state_ssm_re[l]), _state_to_lanes(state_ssm_im[l])],
                             axis=-1).reshape(N, -1)
        pbuf = state_pool[l].astype(F32).transpose(1, 0, 2)
        s_s, st_s, p_s = _mix_sample(us, h0, *ssm_w, xps, pbuf, *pool_w)
        xs = _outproj(xs, a_s, s_s, p_s, wo, N)
        xs = _ffn(xs, row(g_ffn[l]), w1, w2, N, tf)
        hr_s, hi_s = _state_from_lanes(st_s)
        outs_s[0].append(jnp.concatenate([kc[:, 1:], ks[:, None]], axis=1).reshape(N, WINDOW, N_KV_HEADS, HEAD_DIM))
        outs_s[1].append(jnp.concatenate([vc[:, 1:], vs[:, None]], axis=1).reshape(N, WINDOW, N_KV_HEADS, HEAD_DIM))
        outs_s[2].append(hr_s)
        outs_s[3].append(hi_s)
        outs_s[4].append(jnp.concatenate([state_pool[l].astype(F32)[:, 1:], xps[:, None]], axis=1))

    y_prompt = xp.reshape(B, P, D_MODEL)[:, FRONT_PAD + N_META:]
    y_sample = xs.reshape(N, 1, D_MODEL)
    np_ = [jnp.stack(t) for t in outs_p]
    ns_ = [jnp.stack(t) for t in outs_s]
    return (y_prompt, y_sample, *np_, *ns_)
```

```python
import functools
import math

import jax
import jax.numpy as jnp
from jax.experimental import pallas as pl
from jax.experimental.pallas import tpu as pltpu

D_MODEL = 2048
N_META = 16
HEAD_DIM = 128
N_HEADS = 8
N_KV_HEADS = 2
GQA_GROUP = 4
ATTN_WIDTH = 1024
KV_WIDTH = 256
WINDOW = 128
BLOCK = 128
ROT_DIM = 32
ROPE_THETA = 500000.0
SSM_WIDTH = 512
SSM_GROUP_SIZE = 16
SSM_GROUPS = 32
SSM_STATE = 64
POOL_WIDTH = 512
POOL_WINDOWS = (2, 4, 8, 16)
POOL_GROUP = 128
POOL_BUF = 15
IN_WIDTH = 2560
D_FF = 8192
EPS = 1e-6
PAST_LEN = 16384

FRONT_PAD = (-N_META) % BLOCK
LANES = 128
SUBLANES = 8
SSM_LANE_BLOCKS = SSM_WIDTH // LANES
SSM_BLOCK_STATES = (LANES // SSM_GROUP_SIZE) * SSM_STATE
VMEM_LIMIT = 56 * 1024 * 1024

BF16 = jnp.bfloat16
F32 = jnp.float32


def _params(*semantics):
    return pltpu.CompilerParams(dimension_semantics=semantics, vmem_limit_bytes=VMEM_LIMIT)


def _rms(x, g):
    return x * jax.lax.rsqrt(jnp.mean(x * x, axis=-1, keepdims=True) + EPS) * g


def _full(shape):
    n = len(shape)
    return pl.BlockSpec(shape, lambda *_: (0,) * n)


def _layer(shape, l):
    n = len(shape)
    return pl.BlockSpec((None, *shape), lambda *_: (l,) + (0,) * n)


def _inproj_kernel(x_ref, g_ref, w_ref, gq_ref, gk_ref, rc_ref, rs1_ref, rs2_ref,
                   q_ref, k_ref, v_ref, u_ref, xp_ref):
    h = _rms(x_ref[...], g_ref[...]).astype(BF16)
    proj = jnp.dot(h, w_ref[...], preferred_element_type=F32)
    rc, rs1, rs2 = rc_ref[...], rs1_ref[...], rs2_ref[...]

    def head(t, g):
        t = _rms(t, g)
        return t * rc + pltpu.roll(t, 16, 1) * rs1 + pltpu.roll(t, LANES - 16, 1) * rs2

    for hd in range(N_HEADS):
        sl = slice(hd * HEAD_DIM, (hd + 1) * HEAD_DIM)
        q_ref[:, sl] = head(proj[:, sl], gq_ref[...])
    for hd in range(N_KV_HEADS):
        sl = slice(hd * HEAD_DIM, (hd + 1) * HEAD_DIM)
        k_ref[:, sl] = head(proj[:, ATTN_WIDTH + hd * HEAD_DIM:ATTN_WIDTH + (hd + 1) * HEAD_DIM], gk_ref[...])
    o2 = ATTN_WIDTH + KV_WIDTH
    o3 = o2 + KV_WIDTH
    o4 = o3 + SSM_WIDTH
    v_ref[...] = proj[:, o2:o3]
    u_ref[...] = proj[:, o3:o4]
    xp_ref[...] = proj[:, o4:]


def _inproj(x, g, w, l, gq, gk, rope, tm, tiles_per_rope):
    R = x.shape[0]
    row = lambda i: (i, 0)
    rrow = lambda i: (i % tiles_per_rope, 0)
    outs = [(ATTN_WIDTH, F32), (KV_WIDTH, F32), (KV_WIDTH, F32), (SSM_WIDTH, F32), (POOL_WIDTH, F32)]
    return pl.pallas_call(
        _inproj_kernel,
        grid=(R // tm,),
        in_specs=[pl.BlockSpec((tm, D_MODEL), row), _full((1, D_MODEL)), _layer((D_MODEL, IN_WIDTH), l),
                  _full((1, HEAD_DIM)), _full((1, HEAD_DIM)),
                  pl.BlockSpec((tm, LANES), rrow), pl.BlockSpec((tm, LANES), rrow),
                  pl.BlockSpec((tm, LANES), rrow)],
        out_specs=[pl.BlockSpec((tm, w_), row) for w_, _ in outs],
        out_shape=[jax.ShapeDtypeStruct((R, w_), dt) for w_, dt in outs],
        compiler_params=_params("parallel"),
        name="inproj",
    )(x, g, w, gq, gk, *rope)


def _attn_kernel(sink_ref, q_ref, kp_ref, kc_ref, vp_ref, vc_ref, g_ref, o_ref):
    i = pl.program_id(1)
    rows = GQA_GROUP * BLOCK
    r = jax.lax.broadcasted_iota(jnp.int32, (rows, 2 * BLOCK), 0) & (BLOCK - 1)
    c = jax.lax.broadcasted_iota(jnp.int32, (rows, 2 * BLOCK), 1)
    diff = BLOCK + r - c
    krow = (i - 1) * BLOCK + c
    mask = (diff >= 0) & (diff <= WINDOW) & (krow >= FRONT_PAD)
    rgrp = jax.lax.broadcasted_iota(jnp.int32, (rows, 1), 0) // BLOCK
    outs = []
    for kh in range(N_KV_HEADS):
        ksl = slice(kh * HEAD_DIM, (kh + 1) * HEAD_DIM)
        qh = jnp.concatenate(
            [q_ref[:, (kh * GQA_GROUP + g) * HEAD_DIM:(kh * GQA_GROUP + g + 1) * HEAD_DIM]
             for g in range(GQA_GROUP)], axis=0).astype(BF16)
        kk = jnp.concatenate([kp_ref[:, ksl], kc_ref[:, ksl]], axis=0).astype(BF16)
        vv = jnp.concatenate([vp_ref[:, ksl], vc_ref[:, ksl]], axis=0).astype(BF16)
        s = jax.lax.dot_general(qh, kk, (((1,), (1,)), ((), ())),
                                preferred_element_type=F32) * (HEAD_DIM ** -0.5)
        s = jnp.where(mask, s, -jnp.inf)
        sk = jnp.zeros((rows, 1), F32)
        for g in range(GQA_GROUP):
            sk = jnp.where(rgrp == g, sink_ref[kh * GQA_GROUP + g], sk)
        m = jnp.maximum(jnp.max(s, axis=-1, keepdims=True), sk)
        p = jnp.exp(s - m)
        denom = jnp.sum(p, axis=-1, keepdims=True) + jnp.exp(sk - m)
        o = jnp.dot(p.astype(BF16), vv, preferred_element_type=F32) / denom
        outs.extend(o[g * BLOCK:(g + 1) * BLOCK] for g in range(GQA_GROUP))
    a = jnp.concatenate(outs, axis=1)
    o_ref[...] = _rms(a, g_ref[...]).astype(o_ref.dtype)


def _attn(sinks, q, k, v, g):
    B, P, _ = q.shape
    cur = lambda b, i, s: (b, i, 0)
    prev = lambda b, i, s: (b, jnp.maximum(i - 1, 0), 0)
    kv = lambda im: pl.BlockSpec((None, BLOCK, KV_WIDTH), im)
    return pl.pallas_call(
        _attn_kernel,
        grid_spec=pltpu.PrefetchScalarGridSpec(
            num_scalar_prefetch=1,
            grid=(B, P // BLOCK),
            in_specs=[pl.BlockSpec((None, BLOCK, ATTN_WIDTH), cur), kv(prev), kv(cur), kv(prev), kv(cur),
                      pl.BlockSpec((1, ATTN_WIDTH), lambda b, i, s: (0, 0))],
            out_specs=pl.BlockSpec((None, BLOCK, ATTN_WIDTH), cur),
        ),
        out_shape=jax.ShapeDtypeStruct((B, P, ATTN_WIDTH), BF16),
        compiler_params=_params("parallel", "parallel"),
        name="attn_prompt",
    )(sinks, q, k, k, v, v, g)


def _ssm_params_kernel(ar_ref, ai_ref, ldt_ref, br_ref, bi_ref, tr_ref, ti_ref, bbr_ref, bbi_ref):
    ar, ai = ar_ref[...], ai_ref[...]
    dt = jnp.exp(ldt_ref[...])
    kk = (jax.lax.broadcasted_iota(jnp.int32, (1, SUBLANES), 1) + 1).astype(F32)
    mag = jnp.exp(dt * ar * kk)
    ang = dt * ai * kk
    tr = mag * jnp.cos(ang)
    ti = mag * jnp.sin(ang)
    tr_ref[...] = tr
    ti_ref[...] = ti
    abr, abi = tr[:, 0:1], ti[:, 0:1]
    den = ar * ar + ai * ai
    fr = ((abr - 1.0) * ar + abi * ai) / den
    fi = (abi * ar - (abr - 1.0) * ai) / den
    br, bi = br_ref[...], bi_ref[...]
    bbr_ref[...] = fr * br - fi * bi
    bbi_ref[...] = fr * bi + fi * br


def _ssm_params(A_re, A_im, log_dt, B_re, B_im, C_re, C_im):
    n = SSM_GROUPS * SSM_STATE
    col = lambda t: t.astype(F32).reshape(n, 1)
    ldt = jnp.broadcast_to(log_dt.astype(F32)[:, None], (SSM_GROUPS, SSM_STATE)).reshape(n, 1)
    shapes = [(n, SUBLANES), (n, SUBLANES), (n, SSM_GROUP_SIZE), (n, SSM_GROUP_SIZE)]
    tr, ti, bbr, bbi = pl.pallas_call(
        _ssm_params_kernel,
        out_shape=[jax.ShapeDtypeStruct(s, F32) for s in shapes],
        name="ssm_params",
    )(col(A_re), col(A_im), ldt, B_re.astype(F32).reshape(n, SSM_GROUP_SIZE),
      B_im.astype(F32).reshape(n, SSM_GROUP_SIZE))
    J, G8 = SSM_LANE_BLOCKS, LANES // SSM_GROUP_SIZE

    def lanes(t):
        return t.T.reshape(SUBLANES, J, SSM_BLOCK_STATES)

    pw = jnp.concatenate([lanes(tr), lanes(ti)], axis=-1).reshape(SUBLANES, J * 2 * SSM_BLOCK_STATES)
    eye = jnp.eye(G8, dtype=F32)

    def bdiag(t):
        t = t.reshape(J, G8, SSM_STATE, SSM_GROUP_SIZE).transpose(0, 1, 3, 2)
        t = t[:, :, :, None, :] * eye[None, :, None, :, None]
        return t.reshape(J, LANES, SSM_BLOCK_STATES)

    bblk = jnp.concatenate([bdiag(bbr), bdiag(bbi)], axis=-1).astype(BF16)

    def cdiag(t):
        t = t.astype(F32).reshape(J, G8, SSM_GROUP_SIZE, SSM_STATE).transpose(0, 1, 3, 2)
        t = t[:, :, :, None, :] * eye[None, :, None, :, None]
        return t.reshape(J, SSM_BLOCK_STATES, LANES)

    cblk = jnp.concatenate([cdiag(C_re), -cdiag(C_im)], axis=1).astype(BF16)
    return pw, bblk, cblk


def _ssm_tail(y, u, d_ref, wg_ref, bg_ref, g_ref):
    y = y + d_ref[...] * u
    z = jax.nn.gelu(y)
    gate = jax.nn.sigmoid(jnp.dot(z.astype(BF16), wg_ref[...], preferred_element_type=F32) + bg_ref[...])
    return _rms(z * gate, g_ref[...])


def _ssm_kernel(u_ref, pw_ref, bblk_ref, cblk_ref, d_ref, wg_ref, bg_ref, g_ref,
                o_ref, st_ref, x_scr, carry_scr):
    t = pl.program_id(1)
    tm = u_ref.shape[0]
    S = SSM_BLOCK_STATES

    @pl.when(t == 0)
    def _():
        carry_scr[...] = jnp.zeros_like(carry_scr)

    u = u_ref[...]
    ub = u.astype(BF16)
    srow = jax.lax.broadcasted_iota(jnp.int32, (SUBLANES, S), 0)
    ys = []
    for j in range(SSM_LANE_BLOCKS):
        x_scr[...] = jnp.dot(ub[:, j * LANES:(j + 1) * LANES], bblk_ref[j], preferred_element_type=F32)
        base = j * 2 * S
        pr = pw_ref[:, base:base + S]
        pi = pw_ref[:, base + S:base + 2 * S]

        def shifted(k):
            zr = jnp.where(srow >= k, jnp.broadcast_to(pr[k - 1:k], (SUBLANES, S)), 0.0)
            zi = jnp.where(srow >= k, jnp.broadcast_to(pi[k - 1:k], (SUBLANES, S)), 0.0)
            return zr, zi

        steps = [(k, *shifted(k)) for k in (1, 2, 4)]

        def body(rt, carry):
            cr, ci = carry
            r0 = pl.multiple_of(rt * SUBLANES, SUBLANES)
            hr = x_scr[pl.ds(r0, SUBLANES), 0:S]
            hi = x_scr[pl.ds(r0, SUBLANES), S:2 * S]
            for k, zr, zi in steps:
                sr = pltpu.roll(hr, k, 0)
                si = pltpu.roll(hi, k, 0)
                hr, hi = hr + zr * sr - zi * si, hi + zr * si + zi * sr
            hr, hi = hr + pr * cr - pi * ci, hi + pr * ci + pi * cr
            x_scr[pl.ds(r0, SUBLANES), 0:S] = hr
            x_scr[pl.ds(r0, SUBLANES), S:2 * S] = hi
            last = SUBLANES - 1
            return (jnp.broadcast_to(hr[last:], (SUBLANES, S)), jnp.broadcast_to(hi[last:], (SUBLANES, S)))

        c0 = (carry_scr[:, base:base + S], carry_scr[:, base + S:base + 2 * S])
        cr, ci = jax.lax.fori_loop(0, tm // SUBLANES, body, c0, unroll=2)
        carry_scr[:, base:base + S] = cr
        carry_scr[:, base + S:base + 2 * S] = ci
        ys.append(jnp.dot(x_scr[...].astype(BF16), cblk_ref[j], preferred_element_type=F32))
    o_ref[...] = _ssm_tail(jnp.concatenate(ys, axis=1), u, d_ref, wg_ref, bg_ref, g_ref).astype(o_ref.dtype)
    st_ref[...] = carry_scr[...]


def _ssm(u, pw, bblk, cblk, d, wg, l, bg, g, tm):
    B, P, _ = u.shape
    S2 = SSM_LANE_BLOCKS * 2 * SSM_BLOCK_STATES
    return pl.pallas_call(
        _ssm_kernel,
        grid=(B, P // tm),
        in_specs=[pl.BlockSpec((None, tm, SSM_WIDTH), lambda b, t: (b, t, 0)),
                  _full((SUBLANES, S2)), _full(bblk.shape), _full(cblk.shape),
                  _full((1, SSM_WIDTH)), _layer((SSM_WIDTH, SSM_WIDTH), l), _full((1, SSM_WIDTH)),
                  _full((1, SSM_WIDTH))],
        out_specs=[pl.BlockSpec((None, tm, SSM_WIDTH), lambda b, t: (b, t, 0)),
                   pl.BlockSpec((None, SUBLANES, S2), lambda b, t: (b, 0, 0))],
        out_shape=[jax.ShapeDtypeStruct((B, P, SSM_WIDTH), BF16),
                   jax.ShapeDtypeStruct((B, SUBLANES, S2), F32)],
        scratch_shapes=[pltpu.VMEM((tm, 2 * SSM_BLOCK_STATES), F32),
                        pltpu.VMEM((SUBLANES, S2), F32)],
        compiler_params=_params("parallel", "arbitrary"),
        name="ssm_prompt",
    )(u, pw, bblk, cblk, d, wg, bg, g)


def _state_to_lanes(h):
    return h.astype(F32).reshape(h.shape[0], SSM_LANE_BLOCKS, SSM_BLOCK_STATES)


def _state_from_lanes(s):
    s = s.reshape(s.shape[0], SSM_LANE_BLOCKS, 2, SSM_BLOCK_STATES)
    return (s[:, :, 0].reshape(-1, SSM_GROUPS, SSM_STATE), s[:, :, 1].reshape(-1, SSM_GROUPS, SSM_STATE))


def _pool_tail(d_groups, w_ref, sc_ref, g_ref):
    y = jnp.concatenate(
        [jnp.dot(d.astype(BF16), w_ref[gi], preferred_element_type=F32) for gi, d in enumerate(d_groups)], axis=1)
    return _rms(y * sc_ref[...], g_ref[...])


def _pool_kernel(x_ref, prev_ref, w_ref, sc_ref, g_ref, o_ref):
    t = pl.program_id(1)
    tm = x_ref.shape[0]
    halo = prev_ref.shape[0]
    x = x_ref[...]
    prev = jnp.where(t > 0, prev_ref[...], 0.0)
    xe = jnp.concatenate([prev, x], axis=0)
    pos = t * tm + jax.lax.broadcasted_iota(jnp.int32, (tm, 1), 0) - FRONT_PAD
    ds = []
    for gi, w in enumerate(POOL_WINDOWS):
        s = xe[:, gi * POOL_GROUP:(gi + 1) * POOL_GROUP]
        k = 1
        while k < w:
            s = s + pltpu.roll(s, k, 0)
            k *= 2
        cnt = jnp.clip(pos + 1, 1, w).astype(F32)
        ds.append(s[halo:] / cnt - x[:, gi * POOL_GROUP:(gi + 1) * POOL_GROUP])
    o_ref[...] = _pool_tail(ds, w_ref, sc_ref, g_ref).astype(o_ref.dtype)


def _pool(xp, w, l, sc, g, tm):
    B, P, _ = xp.shape
    halo = 16
    r = tm // halo
    return pl.pallas_call(
        _pool_kernel,
        grid=(B, P // tm),
        in_specs=[pl.BlockSpec((None, tm, POOL_WIDTH), lambda b, t: (b, t, 0)),
                  pl.BlockSpec((None, halo, POOL_WIDTH), lambda b, t: (b, jnp.maximum(t * r - 1, 0), 0)),
                  _layer(w.shape[1:], l), _full((1, POOL_WIDTH)), _full((1, POOL_WIDTH))],
        out_specs=pl.BlockSpec((None, tm, POOL_WIDTH), lambda b, t: (b, t, 0)),
        out_shape=jax.ShapeDtypeStruct((B, P, POOL_WIDTH), BF16),
        compiler_params=_params("parallel", "parallel"),
        name="pool_prompt",
    )(xp, xp, w, sc, g)


def _attn_sample_kernel(q_ref, kn_ref, vn_ref, kc_ref, vc_ref, sink_ref, g_ref, o_ref):
    outs = []
    for kh in range(N_KV_HEADS):
        ksl = slice(kh * HEAD_DIM, (kh + 1) * HEAD_DIM)
        qh = jnp.concatenate(
            [q_ref[:, (kh * GQA_GROUP + g) * HEAD_DIM:(kh * GQA_GROUP + g + 1) * HEAD_DIM]
             for g in range(GQA_GROUP)], axis=0)
        kn = kn_ref[:, ksl]
        vn = vn_ref[:, ksl]
        scale = HEAD_DIM ** -0.5
        sc = jax.lax.dot_general(qh.astype(BF16), kc_ref[:, ksl].astype(BF16), (((1,), (1,)), ((), ())),
                                 preferred_element_type=F32) * scale
        sn = jnp.sum(qh.astype(BF16).astype(F32) * kn.astype(BF16).astype(F32), axis=-1, keepdims=True) * scale
        sk = sink_ref[kh]
        m = jnp.maximum(jnp.maximum(jnp.max(sc, axis=-1, keepdims=True), sn), sk)
        pc = jnp.exp(sc - m)
        pn = jnp.exp(sn - m)
        denom = jnp.sum(pc, axis=-1, keepdims=True) + pn + jnp.exp(sk - m)
        o = jnp.dot(pc.astype(BF16), vc_ref[:, ksl].astype(BF16), preferred_element_type=F32)
        o = (o + pn.astype(BF16).astype(F32) * vn.astype(BF16).astype(F32)) / denom
        outs.extend(o[g:g + 1] for g in range(GQA_GROUP))
    a = jnp.concatenate(outs, axis=1)
    o_ref[...] = _rms(a, g_ref[...])


def _attn_sample(q, kn, vn, kc, vc, sinks, g):
    N = q.shape[0]
    r3 = lambda t: t.reshape(N, 1, t.shape[-1])
    row = lambda w: pl.BlockSpec((None, 1, w), lambda b: (b, 0, 0))
    cache = pl.BlockSpec((None, WINDOW, KV_WIDTH), lambda b: (b, 0, 0))
    out = pl.pallas_call(
        _attn_sample_kernel,
        grid=(N,),
        in_specs=[row(ATTN_WIDTH), row(KV_WIDTH), row(KV_WIDTH), cache, cache,
                  _full((N_KV_HEADS, GQA_GROUP, 1)), _full((1, ATTN_WIDTH))],
        out_specs=row(ATTN_WIDTH),
        out_shape=jax.ShapeDtypeStruct((N, 1, ATTN_WIDTH), F32),
        compiler_params=_params("parallel"),
        name="attn_sample",
    )(r3(q), r3(kn), r3(vn), kc, vc, sinks.astype(F32).reshape(N_KV_HEADS, GQA_GROUP, 1), g)
    return out.reshape(N, ATTN_WIDTH)


def _mix_sample_kernel(u_ref, h0_ref, pw_ref, bblk_ref, cblk_ref, d_ref, wg_ref, bg_ref, gs_ref,
                       xp_ref, pb_ref, wp_ref, sc_ref, gp_ref, s_ref, st_ref, p_ref):
    S = SSM_BLOCK_STATES
    u = u_ref[...]
    ub = u.astype(BF16)
    ys = []
    for j in range(SSM_LANE_BLOCKS):
        x = jnp.dot(ub[:, j * LANES:(j + 1) * LANES], bblk_ref[j], preferred_element_type=F32)
        base = j * 2 * S
        ar = pw_ref[0:1, base:base + S]
        ai = pw_ref[0:1, base + S:base + 2 * S]
        h0r = h0_ref[:, base:base + S]
        h0i = h0_ref[:, base + S:base + 2 * S]
        hr = x[:, 0:S] + ar * h0r - ai * h0i
        hi = x[:, S:] + ar * h0i + ai * h0r
        st_ref[:, base:base + S] = hr
        st_ref[:, base + S:base + 2 * S] = hi
        h = jnp.concatenate([hr, hi], axis=1).astype(BF16)
        ys.append(jnp.dot(h, cblk_ref[j], preferred_element_type=F32))
    s_ref[...] = _ssm_tail(jnp.concatenate(ys, axis=1), u, d_ref, wg_ref, bg_ref, gs_ref)

    xp = xp_ref[...]
    ds = []
    for gi, w in enumerate(POOL_WINDOWS):
        gsl = slice(gi * POOL_GROUP, (gi + 1) * POOL_GROUP)
        s = xp[:, gsl]
        for back in range(1, w):
            s = s + pb_ref[POOL_BUF - back][:, gsl]
        ds.append(s / float(w) - xp[:, gsl])
    p_ref[...] = _pool_tail(ds, wp_ref, sc_ref, gp_ref)


def _mix_sample(u, h0, pw, bblk, cblk, d, wg, bg, gs, xp, pbuf, wp, sc, gp):
    N = u.shape[0]
    S2 = SSM_LANE_BLOCKS * 2 * SSM_BLOCK_STATES
    return pl.pallas_call(
        _mix_sample_kernel,
        out_shape=[jax.ShapeDtypeStruct((N, SSM_WIDTH), F32), jax.ShapeDtypeStruct((N, S2), F32),
                   jax.ShapeDtypeStruct((N, POOL_WIDTH), F32)],
        compiler_params=pltpu.CompilerParams(vmem_limit_bytes=VMEM_LIMIT),
        name="mix_sample",
    )(u, h0, pw, bblk, cblk, d, wg, bg, gs, xp, pbuf, wp, sc, gp)


def _outproj_kernel(x_ref, a_ref, s_ref, p_ref, w_ref, o_ref):
    o1 = ATTN_WIDTH
    o2 = o1 + SSM_WIDTH
    acc = x_ref[...]
    acc = acc + jnp.dot(a_ref[...].astype(BF16), w_ref[0:o1, :], preferred_element_type=F32)
    acc = acc + jnp.dot(s_ref[...].astype(BF16), w_ref[o1:o2, :], preferred_element_type=F32)
    acc = acc + jnp.dot(p_ref[...].astype(BF16), w_ref[o2:, :], preferred_element_type=F32)
    o_ref[...] = acc


def _outproj(x, a, s, p, w, l, tm):
    R = x.shape[0]
    row = lambda i: (i, 0)
    return pl.pallas_call(
        _outproj_kernel,
        grid=(R // tm,),
        in_specs=[pl.BlockSpec((tm, D_MODEL), row), pl.BlockSpec((tm, ATTN_WIDTH), row),
                  pl.BlockSpec((tm, SSM_WIDTH), row), pl.BlockSpec((tm, POOL_WIDTH), row),
                  _layer((D_MODEL, D_MODEL), l)],
        out_specs=pl.BlockSpec((tm, D_MODEL), row),
        out_shape=jax.ShapeDtypeStruct((R, D_MODEL), F32),
        compiler_params=_params("parallel"),
        name="outproj",
    )(x, a, s, p, w)


def _ffn_kernel(x_ref, g_ref, w1_ref, w2_ref, o_ref, h_scr):
    f = pl.program_id(1)

    @pl.when(f == 0)
    def _():
        x = x_ref[...]
        h_scr[...] = _rms(x, g_ref[...]).astype(BF16)
        o_ref[...] = x

    h1 = jnp.dot(h_scr[...], w1_ref[...], preferred_element_type=F32)
    h1 = jnp.square(jnp.maximum(h1, 0.0)).astype(BF16)
    o_ref[...] += jnp.dot(h1, w2_ref[...], preferred_element_type=F32)


def _ffn(x, g, w1, w2, l, tm, tf):
    R = x.shape[0]
    return pl.pallas_call(
        _ffn_kernel,
        grid=(R // tm, D_FF // tf),
        in_specs=[pl.BlockSpec((tm, D_MODEL), lambda i, f: (i, 0)), _full((1, D_MODEL)),
                  pl.BlockSpec((None, D_MODEL, tf), lambda i, f: (l, 0, f)),
                  pl.BlockSpec((None, tf, D_MODEL), lambda i, f: (l, f, 0))],
        out_specs=pl.BlockSpec((tm, D_MODEL), lambda i, f: (i, 0)),
        out_shape=jax.ShapeDtypeStruct((R, D_MODEL), F32),
        scratch_shapes=[pltpu.VMEM((tm, D_MODEL), BF16)],
        compiler_params=_params("parallel", "arbitrary"),
        name="ffn",
    )(x, g, w1, w2)


def _rope_tables(pos):
    half = ROT_DIM // 2
    inv = ROPE_THETA ** (-jnp.arange(0, ROT_DIM, 2, dtype=F32) / ROT_DIM)
    ang = pos.astype(F32)[:, None] * inv
    cos, sin = jnp.cos(ang), jnp.sin(ang)
    n = pos.shape[0]
    z = jnp.zeros((n, HEAD_DIM - ROT_DIM), F32)
    zh = jnp.zeros((n, half), F32)
    rc = jnp.concatenate([cos, cos, z + 1.0], axis=1)
    rs1 = jnp.concatenate([zh, sin, z], axis=1)
    rs2 = jnp.concatenate([-sin, zh, z], axis=1)
    return rc, rs1, rs2


def kernel(x_prompt, x_sample, cache_k, cache_v, state_ssm_re, state_ssm_im, state_pool, meta_tokens, g_mix, w_in, g_q, g_k, sinks, A_re, A_im, log_dt, B_re, B_im, C_re, C_im, D_skip, w_glu, b_glu, w_pool, pool_scale, g_out_attn, g_out_ssm, g_out_pool, w_out, g_ffn, w_ff1, w_ff2):
    B, T, _ = x_prompt.shape
    N = x_sample.shape[0]
    depth = w_in.shape[0]
    P = FRONT_PAD + N_META + T
    R = B * P
    row = lambda t: t.astype(F32).reshape(1, -1)

    meta = jnp.broadcast_to(meta_tokens.astype(F32)[None], (B, N_META, D_MODEL))
    xp = jnp.concatenate([jnp.zeros((B, FRONT_PAD, D_MODEL), F32), meta, x_prompt], axis=1).reshape(R, D_MODEL)
    xs = x_sample.reshape(N, D_MODEL)

    rope_p = _rope_tables(jnp.arange(P) - FRONT_PAD)
    rope_s = _rope_tables(jnp.full((N,), PAST_LEN))
    tm_p = 3 * BLOCK
    tm_f = 6 * BLOCK
    tf = 512

    wi_all, wo_all = w_in.astype(BF16), w_out.astype(BF16)
    w1_all, w2_all = w_ff1.astype(BF16), w_ff2.astype(BF16)
    wg_all, wp_all = w_glu.astype(BF16), w_pool.astype(BF16)

    outs_p = [[] for _ in range(5)]
    outs_s = [[] for _ in range(5)]
    for l in range(depth):
        pw, bblk, cblk = _ssm_params(A_re[l], A_im[l], log_dt[l], B_re[l], B_im[l], C_re[l], C_im[l])
        ssm_w = (pw, bblk, cblk, row(D_skip[l]), wg_all, l, row(b_glu[l]), row(g_out_ssm[l]))
        pool_w = (wp_all, l, row(pool_scale[l]), row(g_out_pool[l]))
        ssm_ws = (pw, bblk, cblk, row(D_skip[l]), wg_all[l], row(b_glu[l]), row(g_out_ssm[l]))
        pool_ws = (wp_all[l], row(pool_scale[l]), row(g_out_pool[l]))
        in_w = (row(g_mix[l]), wi_all, l, row(g_q[l]), row(g_k[l]))

        q, k, v, u, xpl = _inproj(xp, *in_w, rope_p, tm_p, P // tm_p)
        b3 = lambda t: t.reshape(B, P, t.shape[-1])
        a = _attn(sinks[l].astype(F32), b3(q), b3(k), b3(v), row(g_out_attn[l]))
        s, st = _ssm(b3(u), *ssm_w, tm_p)
        pm = _pool(b3(xpl), *pool_w, tm_p)
        xp = _outproj(xp, a.reshape(R, -1), s.reshape(R, -1), pm.reshape(R, -1), wo_all, l, tm_p)
        xp = _ffn(xp, row(g_ffn[l]), w1_all, w2_all, l, tm_f, tf)
        hr, hi = _state_from_lanes(st[:, 0])
        outs_p[0].append(b3(k)[:, P - WINDOW:].reshape(B, WINDOW, N_KV_HEADS, HEAD_DIM))
        outs_p[1].append(b3(v)[:, P - WINDOW:].reshape(B, WINDOW, N_KV_HEADS, HEAD_DIM))
        outs_p[2].append(hr)
        outs_p[3].append(hi)
        outs_p[4].append(b3(xpl)[:, P - POOL_BUF:])

        qs, ks, vs, us, xps = _inproj(xs, *in_w, rope_s, N, 1)
        kc = cache_k[l].astype(F32).reshape(N, WINDOW, KV_WIDTH)
        vc = cache_v[l].astype(F32).reshape(N, WINDOW, KV_WIDTH)
        a_s = _attn_sample(qs, ks, vs, kc, vc, sinks[l], row(g_out_attn[l]))
        h0 = jnp.concatenate([_state_to_lanes(state_ssm_re[l]), _state_to_lanes(state_ssm_im[l])],
                             axis=-1).reshape(N, -1)
        pbuf = state_pool[l].astype(F32).transpose(1, 0, 2)
        s_s, st_s, p_s = _mix_sample(us, h0, *ssm_ws, xps, pbuf, *pool_ws)
        xs = _outproj(xs, a_s, s_s, p_s, wo_all, l, N)
        xs = _ffn(xs, row(g_ffn[l]), w1_all, w2_all, l, N, tf)
        hr_s, hi_s = _state_from_lanes(st_s)
        outs_s[0].append(jnp.concatenate([kc[:, 1:], ks[:, None]], axis=1).reshape(N, WINDOW, N_KV_HEADS, HEAD_DIM))
        outs_s[1].append(jnp.concatenate([vc[:, 1:], vs[:, None]], axis=1).reshape(N, WINDOW, N_KV_HEADS, HEAD_DIM))
        outs_s[2].append(hr_s)
        outs_s[3].append(hi_s)
        outs_s[4].append(jnp.concatenate([state_pool[l].astype(F32)[:, 1:], xps[:, None]], axis=1))

    y_prompt = xp.reshape(B, P, D_MODEL)[:, FRONT_PAD + N_META:]
    y_sample = xs.reshape(N, 1, D_MODEL)
    np_ = [jnp.stack(t) for t in outs_p]
    ns_ = [jnp.stack(t) for t in outs_s]
    return (y_prompt, y_sample, *np_, *ns_)
```

```python
import jax
import jax.numpy as jnp
from jax.experimental import pallas as pl
from jax.experimental.pallas import tpu as pltpu

D_MODEL = 2048
N_META = 16
HEAD_DIM = 128
N_HEADS = 8
N_KV_HEADS = 2
GQA_GROUP = 4
ATTN_WIDTH = 1024
KV_WIDTH = 256
WINDOW = 128
BLOCK = 128
ROT_DIM = 32
ROPE_THETA = 500000.0
SSM_WIDTH = 512
SSM_GROUP_SIZE = 16
SSM_GROUPS = 32
SSM_STATE = 64
POOL_WIDTH = 512
POOL_WINDOWS = (2, 4, 8, 16)
POOL_GROUP = 128
POOL_BUF = 15
POOL_HALO = 16
IN_WIDTH = 2560
D_FF = 8192
EPS = 1e-6
PAST_LEN = 16384

HEAD_ROWS = BLOCK
SEQ0 = HEAD_ROWS - N_META
LANES = 128
SUBLANES = 8
SSM_LANE_BLOCKS = SSM_WIDTH // LANES
SSM_BLOCK_STATES = (LANES // SSM_GROUP_SIZE) * SSM_STATE
SSM_STATE_LANES = SSM_LANE_BLOCKS * 2 * SSM_BLOCK_STATES
VMEM_LIMIT = 56 * 1024 * 1024

TM_PROJ = 512
TM_FFN = 1024
TF_FFN = 512
TM_SEQ = 512
DEC_STEP = 8

BF16 = jnp.bfloat16
F32 = jnp.float32


def _params(*semantics):
    return pltpu.CompilerParams(dimension_semantics=semantics, vmem_limit_bytes=VMEM_LIMIT)


def _rms(x, g):
    return x * jax.lax.rsqrt(jnp.mean(x * x, axis=-1, keepdims=True) + EPS) * g


def _full(shape):
    n = len(shape)
    return pl.BlockSpec(shape, lambda *_: (0,) * n)


def _layer(shape, l):
    n = len(shape)
    return pl.BlockSpec((None, *shape), lambda *_: (l,) + (0,) * n)


def _inproj_kernel(x_ref, g_ref, w_ref, gq_ref, gk_ref, rc_ref, rs1_ref, rs2_ref,
                   q_ref, k_ref, v_ref, u_ref, xp_ref):
    h = _rms(x_ref[...], g_ref[...]).astype(BF16)
    proj = jnp.dot(h, w_ref[...], preferred_element_type=F32)
    rc, rs1, rs2 = rc_ref[...], rs1_ref[...], rs2_ref[...]

    def head(t, g):
        t = _rms(t, g)
        return t * rc + pltpu.roll(t, 16, 1) * rs1 + pltpu.roll(t, LANES - 16, 1) * rs2

    for hd in range(N_HEADS):
        sl = slice(hd * HEAD_DIM, (hd + 1) * HEAD_DIM)
        q_ref[:, sl] = head(proj[:, sl], gq_ref[...])
    for hd in range(N_KV_HEADS):
        sl = slice(hd * HEAD_DIM, (hd + 1) * HEAD_DIM)
        k_ref[:, sl] = head(proj[:, ATTN_WIDTH + hd * HEAD_DIM:ATTN_WIDTH + (hd + 1) * HEAD_DIM], gk_ref[...])
    o2 = ATTN_WIDTH + KV_WIDTH
    o3 = o2 + KV_WIDTH
    o4 = o3 + SSM_WIDTH
    v_ref[...] = proj[:, o2:o3]
    u_ref[...] = proj[:, o3:o4]
    xp_ref[...] = proj[:, o4:]


def _inproj(x, g, w, l, gq, gk, rope, tm):
    R = x.shape[0]
    tiles_per_rope = rope[0].shape[0] // tm
    row = lambda i: (i, 0)
    rrow = lambda i: (i % tiles_per_rope, 0)
    widths = [ATTN_WIDTH, KV_WIDTH, KV_WIDTH, SSM_WIDTH, POOL_WIDTH]
    return pl.pallas_call(
        _inproj_kernel,
        grid=(R // tm,),
        in_specs=[pl.BlockSpec((tm, D_MODEL), row), _full((1, D_MODEL)), _layer((D_MODEL, IN_WIDTH), l),
                  _full((1, HEAD_DIM)), _full((1, HEAD_DIM)),
                  pl.BlockSpec((tm, LANES), rrow), pl.BlockSpec((tm, LANES), rrow),
                  pl.BlockSpec((tm, LANES), rrow)],
        out_specs=[pl.BlockSpec((tm, w_), row) for w_ in widths],
        out_shape=[jax.ShapeDtypeStruct((R, w_), F32) for w_ in widths],
        compiler_params=_params("arbitrary"),
        name="inproj",
    )(x, g, w, gq, gk, *rope)


def _attn_kernel(sink_ref, qm_ref, qh_ref, kpm_ref, kcm_ref, kh_ref, vpm_ref, vcm_ref, vh_ref, g_ref,
                 om_ref, oh_ref):
    i = pl.program_id(1)
    first = i == 0
    second = i == 1
    q_blk = jnp.where(first, qh_ref[...], qm_ref[...])
    kc_blk = jnp.where(first, kh_ref[...], kcm_ref[...])
    vc_blk = jnp.where(first, vh_ref[...], vcm_ref[...])
    kp_blk = jnp.where(second, kh_ref[...], kpm_ref[...])
    vp_blk = jnp.where(second, vh_ref[...], vpm_ref[...])
    rows = GQA_GROUP * BLOCK
    r = jax.lax.broadcasted_iota(jnp.int32, (rows, 2 * BLOCK), 0) & (BLOCK - 1)
    c = jax.lax.broadcasted_iota(jnp.int32, (rows, 2 * BLOCK), 1)
    diff = BLOCK + r - c
    krow = (i - 1) * BLOCK + c
    mask = (diff >= 0) & (diff <= WINDOW) & (krow >= SEQ0)
    rgrp = jax.lax.broadcasted_iota(jnp.int32, (rows, 1), 0) // BLOCK
    outs = []
    for kh in range(N_KV_HEADS):
        ksl = slice(kh * HEAD_DIM, (kh + 1) * HEAD_DIM)
        qh = jnp.concatenate(
            [q_blk[:, (kh * GQA_GROUP + g) * HEAD_DIM:(kh * GQA_GROUP + g + 1) * HEAD_DIM]
             for g in range(GQA_GROUP)], axis=0).astype(BF16)
        kk = jnp.concatenate([kp_blk[:, ksl], kc_blk[:, ksl]], axis=0).astype(BF16)
        vv = jnp.concatenate([vp_blk[:, ksl], vc_blk[:, ksl]], axis=0).astype(BF16)
        s = jax.lax.dot_general(qh, kk, (((1,), (1,)), ((), ())),
                                preferred_element_type=F32) * (HEAD_DIM ** -0.5)
        s = jnp.where(mask, s, -jnp.inf)
        sk = jnp.zeros((rows, 1), F32)
        for g in range(GQA_GROUP):
            sk = jnp.where(rgrp == g, sink_ref[kh * GQA_GROUP + g], sk)
        m = jnp.maximum(jnp.max(s, axis=-1, keepdims=True), sk)
        p = jnp.exp(s - m)
        denom = jnp.sum(p, axis=-1, keepdims=True) + jnp.exp(sk - m)
        o = jnp.dot(p.astype(BF16), vv, preferred_element_type=F32) / denom
        outs.extend(o[g * BLOCK:(g + 1) * BLOCK] for g in range(GQA_GROUP))
    a = _rms(jnp.concatenate(outs, axis=1), g_ref[...]).astype(om_ref.dtype)

    @pl.when(first)
    def _():
        oh_ref[...] = a

    @pl.when(jnp.logical_not(first))
    def _():
        om_ref[...] = a


def _attn(sinks, qm, qh, km, kh, vm, vh, g, B):
    nb = qm.shape[0] // (B * BLOCK)
    cur = lambda b, i, s: (b * nb + jnp.maximum(i - 1, 0), 0)
    prev = lambda b, i, s: (b * nb + jnp.maximum(i - 2, 0), 0)
    head = lambda b, i, s: (b, 0)
    spec = lambda w, im: pl.BlockSpec((BLOCK, w), im)
    return pl.pallas_call(
        _attn_kernel,
        grid_spec=pltpu.PrefetchScalarGridSpec(
            num_scalar_prefetch=1,
            grid=(B, nb + 1),
            in_specs=[spec(ATTN_WIDTH, cur), spec(ATTN_WIDTH, head),
                      spec(KV_WIDTH, prev), spec(KV_WIDTH, cur), spec(KV_WIDTH, head),
                      spec(KV_WIDTH, prev), spec(KV_WIDTH, cur), spec(KV_WIDTH, head),
                      pl.BlockSpec((1, ATTN_WIDTH), lambda b, i, s: (0, 0))],
            out_specs=[spec(ATTN_WIDTH, cur), spec(ATTN_WIDTH, head)],
        ),
        out_shape=[jax.ShapeDtypeStruct(qm.shape, BF16), jax.ShapeDtypeStruct(qh.shape, BF16)],
        compiler_params=_params("arbitrary", "arbitrary"),
        name="attn_prompt",
    )(sinks, qm, qh, km, km, kh, vm, vm, vh, g)


def _ssm_params_kernel(ar_ref, ai_ref, ldt_ref, br_ref, bi_ref, tr_ref, ti_ref, bbr_ref, bbi_ref):
    ar, ai = ar_ref[...], ai_ref[...]
    dt = jnp.exp(ldt_ref[...])
    kk = (jax.lax.broadcasted_iota(jnp.int32, (1, SUBLANES), 1) + 1).astype(F32)
    mag = jnp.exp(dt * ar * kk)
    ang = dt * ai * kk
    tr = mag * jnp.cos(ang)
    ti = mag * jnp.sin(ang)
    tr_ref[...] = tr
    ti_ref[...] = ti
    abr, abi = tr[:, 0:1], ti[:, 0:1]
    den = ar * ar + ai * ai
    fr = ((abr - 1.0) * ar + abi * ai) / den
    fi = (abi * ar - (abr - 1.0) * ai) / den
    br, bi = br_ref[...], bi_ref[...]
    bbr_ref[...] = fr * br - fi * bi
    bbi_ref[...] = fr * bi + fi * br


def _ssm_params(A_re, A_im, log_dt, B_re, B_im, C_re, C_im):
    n = SSM_GROUPS * SSM_STATE
    col = lambda t: t.astype(F32).reshape(n, 1)
    ldt = jnp.broadcast_to(log_dt.astype(F32)[:, None], (SSM_GROUPS, SSM_STATE)).reshape(n, 1)
    shapes = [(n, SUBLANES), (n, SUBLANES), (n, SSM_GROUP_SIZE), (n, SSM_GROUP_SIZE)]
    tr, ti, bbr, bbi = pl.pallas_call(
        _ssm_params_kernel,
        out_shape=[jax.ShapeDtypeStruct(s, F32) for s in shapes],
        name="ssm_params",
    )(col(A_re), col(A_im), ldt, B_re.astype(F32).reshape(n, SSM_GROUP_SIZE),
      B_im.astype(F32).reshape(n, SSM_GROUP_SIZE))
    J, G8 = SSM_LANE_BLOCKS, LANES // SSM_GROUP_SIZE

    def lanes(t):
        return t.T.reshape(SUBLANES, J, SSM_BLOCK_STATES)

    pw = jnp.concatenate([lanes(tr), lanes(ti)], axis=-1).reshape(SUBLANES, SSM_STATE_LANES)
    eye = jnp.eye(G8, dtype=F32)

    def bdiag(t):
        t = t.reshape(J, G8, SSM_STATE, SSM_GROUP_SIZE).transpose(0, 1, 3, 2)
        t = t[:, :, :, None, :] * eye[None, :, None, :, None]
        return t.reshape(J, LANES, SSM_BLOCK_STATES)

    bblk = jnp.concatenate([bdiag(bbr), bdiag(bbi)], axis=-1).astype(BF16)

    def cdiag(t):
        t = t.astype(F32).reshape(J, G8, SSM_GROUP_SIZE, SSM_STATE).transpose(0, 1, 3, 2)
        t = t[:, :, :, None, :] * eye[None, :, None, :, None]
        return t.reshape(J, SSM_BLOCK_STATES, LANES)

    cblk = jnp.concatenate([cdiag(C_re), -cdiag(C_im)], axis=1).astype(BF16)
    return pw, bblk, cblk


def _ssm_tail(y, u, d_ref, wg_ref, bg_ref, g_ref):
    y = y + d_ref[...] * u
    z = jax.nn.gelu(y)
    gate = jax.nn.sigmoid(jnp.dot(z.astype(BF16), wg_ref[...], preferred_element_type=F32) + bg_ref[...])
    return _rms(z * gate, g_ref[...])


def _ssm_rows(u, pw_ref, bblk_ref, cblk_ref, x_scr, carry_scr):
    n = u.shape[0]
    S = SSM_BLOCK_STATES
    ub = u.astype(BF16)
    srow = jax.lax.broadcasted_iota(jnp.int32, (SUBLANES, S), 0)
    ys = []
    for j in range(SSM_LANE_BLOCKS):
        x_scr[0:n, :] = jnp.dot(ub[:, j * LANES:(j + 1) * LANES], bblk_ref[j], preferred_element_type=F32)
        base = j * 2 * S
        pr = pw_ref[:, base:base + S]
        pi = pw_ref[:, base + S:base + 2 * S]

        def shifted(k):
            zr = jnp.where(srow >= k, jnp.broadcast_to(pr[k - 1:k], (SUBLANES, S)), 0.0)
            zi = jnp.where(srow >= k, jnp.broadcast_to(pi[k - 1:k], (SUBLANES, S)), 0.0)
            return zr, zi

        steps = [(k, *shifted(k)) for k in (1, 2, 4)]

        def body(rt, carry):
            cr, ci = carry
            r0 = pl.multiple_of(rt * SUBLANES, SUBLANES)
            hr = x_scr[pl.ds(r0, SUBLANES), 0:S]
            hi = x_scr[pl.ds(r0, SUBLANES), S:2 * S]
            for k, zr, zi in steps:
                sr = pltpu.roll(hr, k, 0)
                si = pltpu.roll(hi, k, 0)
                hr, hi = hr + zr * sr - zi * si, hi + zr * si + zi * sr
            hr, hi = hr + pr * cr - pi * ci, hi + pr * ci + pi * cr
            x_scr[pl.ds(r0, SUBLANES), 0:S] = hr
            x_scr[pl.ds(r0, SUBLANES), S:2 * S] = hi
            last = SUBLANES - 1
            return (jnp.broadcast_to(hr[last:], (SUBLANES, S)), jnp.broadcast_to(hi[last:], (SUBLANES, S)))

        c0 = (carry_scr[:, base:base + S], carry_scr[:, base + S:base + 2 * S])
        cr, ci = jax.lax.fori_loop(0, n // SUBLANES, body, c0, unroll=2)
        carry_scr[:, base:base + S] = cr
        carry_scr[:, base + S:base + 2 * S] = ci
        ys.append(jnp.dot(x_scr[0:n, :].astype(BF16), cblk_ref[j], preferred_element_type=F32))
    return jnp.concatenate(ys, axis=1)


def _ssm_kernel(um_ref, uh_ref, pw_ref, bblk_ref, cblk_ref, d_ref, wg_ref, bg_ref, g_ref,
                om_ref, oh_ref, st_ref, x_scr, carry_scr):
    t = pl.program_id(1)

    @pl.when(t == 0)
    def _():
        carry_scr[...] = jnp.zeros_like(carry_scr)
        hrow = jax.lax.broadcasted_iota(jnp.int32, (HEAD_ROWS, 1), 0)
        u = jnp.where(hrow >= SEQ0, uh_ref[...], 0.0)
        y = _ssm_rows(u, pw_ref, bblk_ref, cblk_ref, x_scr, carry_scr)
        oh_ref[...] = _ssm_tail(y, u, d_ref, wg_ref, bg_ref, g_ref).astype(oh_ref.dtype)

    @pl.when(t > 0)
    def _():
        u = um_ref[...]
        y = _ssm_rows(u, pw_ref, bblk_ref, cblk_ref, x_scr, carry_scr)
        om_ref[...] = _ssm_tail(y, u, d_ref, wg_ref, bg_ref, g_ref).astype(om_ref.dtype)
        st_ref[...] = carry_scr[...]


def _ssm(um, uh, pw, bblk, cblk, d, wg, l, bg, g, B, tm):
    nt = um.shape[0] // (B * tm)
    main = lambda b, t: (b * nt + jnp.maximum(t - 1, 0), 0)
    head = lambda b, t: (b, 0)
    return pl.pallas_call(
        _ssm_kernel,
        grid=(B, nt + 1),
        in_specs=[pl.BlockSpec((tm, SSM_WIDTH), main), pl.BlockSpec((HEAD_ROWS, SSM_WIDTH), head),
                  _full((SUBLANES, SSM_STATE_LANES)), _full(bblk.shape), _full(cblk.shape),
                  _full((1, SSM_WIDTH)), _layer((SSM_WIDTH, SSM_WIDTH), l), _full((1, SSM_WIDTH)),
                  _full((1, SSM_WIDTH))],
        out_specs=[pl.BlockSpec((tm, SSM_WIDTH), main), pl.BlockSpec((HEAD_ROWS, SSM_WIDTH), head),
                   pl.BlockSpec((None, SUBLANES, SSM_STATE_LANES), lambda b, t: (b, 0, 0))],
        out_shape=[jax.ShapeDtypeStruct(um.shape, BF16), jax.ShapeDtypeStruct(uh.shape, BF16),
                   jax.ShapeDtypeStruct((B, SUBLANES, SSM_STATE_LANES), F32)],
        scratch_shapes=[pltpu.VMEM((tm, 2 * SSM_BLOCK_STATES), F32),
                        pltpu.VMEM((SUBLANES, SSM_STATE_LANES), F32)],
        compiler_params=_params("arbitrary", "arbitrary"),
        name="ssm_prompt",
    )(um, uh, pw, bblk, cblk, d, wg, bg, g)


def _state_to_lanes(h):
    return h.astype(F32).reshape(h.shape[0], SSM_LANE_BLOCKS, SSM_BLOCK_STATES)


def _state_from_lanes(s):
    s = s.reshape(s.shape[0], SSM_LANE_BLOCKS, 2, SSM_BLOCK_STATES)
    return (s[:, :, 0].reshape(-1, SSM_GROUPS, SSM_STATE), s[:, :, 1].reshape(-1, SSM_GROUPS, SSM_STATE))


def _pool_tail(d_groups, w_ref, sc_ref, g_ref):
    y = jnp.concatenate(
        [jnp.dot(d.astype(BF16), w_ref[gi], preferred_element_type=F32) for gi, d in enumerate(d_groups)], axis=1)
    return _rms(y * sc_ref[...], g_ref[...])


def _pool_rows(x, prev, pos0, w_ref, sc_ref, g_ref):
    n = x.shape[0]
    xe = jnp.concatenate([prev, x], axis=0)
    pos = pos0 + jax.lax.broadcasted_iota(jnp.int32, (n, 1), 0)
    ds = []
    for gi, w in enumerate(POOL_WINDOWS):
        gsl = slice(gi * POOL_GROUP, (gi + 1) * POOL_GROUP)
        s = xe[:, gsl]
        k = 1
        while k < w:
            s = s + pltpu.roll(s, k, 0)
            k *= 2
        cnt = jnp.clip(pos + 1, 1, w).astype(F32)
        ds.append(s[POOL_HALO:] / cnt - x[:, gsl])
    return _pool_tail(ds, w_ref, sc_ref, g_ref)


def _pool_kernel(xm_ref, halo_ref, xh_ref, w_ref, sc_ref, g_ref, om_ref, oh_ref):
    t = pl.program_id(1)
    tm = xm_ref.shape[0]

    @pl.when(t == 0)
    def _():
        hrow = jax.lax.broadcasted_iota(jnp.int32, (HEAD_ROWS, 1), 0)
        x = jnp.where(hrow >= SEQ0, xh_ref[...], 0.0)
        prev = jnp.zeros((POOL_HALO, POOL_WIDTH), F32)
        oh_ref[...] = _pool_rows(x, prev, -SEQ0, w_ref, sc_ref, g_ref).astype(oh_ref.dtype)

    @pl.when(t > 0)
    def _():
        prev = jnp.where(t == 1, xh_ref[HEAD_ROWS - POOL_HALO:, :], halo_ref[...])
        om_ref[...] = _pool_rows(xm_ref[...], prev, N_META + (t - 1) * tm, w_ref, sc_ref, g_ref).astype(om_ref.dtype)


def _pool(xm, xh, w, l, sc, g, B, tm):
    nt = xm.shape[0] // (B * tm)
    r = tm // POOL_HALO
    main = lambda b, t: (b * nt + jnp.maximum(t - 1, 0), 0)
    halo = lambda b, t: (jnp.maximum((b * nt + t - 1) * r - 1, 0), 0)
    head = lambda b, t: (b, 0)
    return pl.pallas_call(
        _pool_kernel,
        grid=(B, nt + 1),
        in_specs=[pl.BlockSpec((tm, POOL_WIDTH), main), pl.BlockSpec((POOL_HALO, POOL_WIDTH), halo),
                  pl.BlockSpec((HEAD_ROWS, POOL_WIDTH), head),
                  _layer(w.shape[1:], l), _full((1, POOL_WIDTH)), _full((1, POOL_WIDTH))],
        out_specs=[pl.BlockSpec((tm, POOL_WIDTH), main), pl.BlockSpec((HEAD_ROWS, POOL_WIDTH), head)],
        out_shape=[jax.ShapeDtypeStruct(xm.shape, BF16), jax.ShapeDtypeStruct(xh.shape, BF16)],
        compiler_params=_params("arbitrary", "arbitrary"),
        name="pool_prompt",
    )(xm, xm, xh, w, sc, g)


def _attn_sample_kernel(q_ref, kn_ref, vn_ref, kc_ref, vc_ref, sink_ref, g_ref, a_in_ref,
                        a_ref, nk_ref, nv_ref, acc_scr):
    del a_in_ref
    step = pl.program_id(0)
    scale = HEAD_DIM ** -0.5
    for bb in range(DEC_STEP):
        nk_ref[bb, 0:WINDOW - 1] = kc_ref[bb, 1:WINDOW]
        nv_ref[bb, 0:WINDOW - 1] = vc_ref[bb, 1:WINDOW]
        outs = []
        for kh in range(N_KV_HEADS):
            ksl = slice(kh * HEAD_DIM, (kh + 1) * HEAD_DIM)
            qh = jnp.concatenate(
                [q_ref[bb:bb + 1, (kh * GQA_GROUP + g) * HEAD_DIM:(kh * GQA_GROUP + g + 1) * HEAD_DIM]
                 for g in range(GQA_GROUP)], axis=0)
            kn = kn_ref[bb:bb + 1, ksl]
            vn = vn_ref[bb:bb + 1, ksl]
            nk_ref[bb, WINDOW - 1, kh:kh + 1, :] = kn
            nv_ref[bb, WINDOW - 1, kh:kh + 1, :] = vn
            kc = kc_ref[bb, :, kh, :]
            vc = vc_ref[bb, :, kh, :]
            sc = jax.lax.dot_general(qh.astype(BF16), kc.astype(BF16), (((1,), (1,)), ((), ())),
                                     preferred_element_type=F32) * scale
            sn = jnp.sum(qh.astype(BF16).astype(F32) * kn.astype(BF16).astype(F32), axis=-1, keepdims=True) * scale
            sk = sink_ref[kh]
            m = jnp.maximum(jnp.maximum(jnp.max(sc, axis=-1, keepdims=True), sn), sk)
            pc = jnp.exp(sc - m)
            pn = jnp.exp(sn - m)
            denom = jnp.sum(pc, axis=-1, keepdims=True) + pn + jnp.exp(sk - m)
            o = jnp.dot(pc.astype(BF16), vc.astype(BF16), preferred_element_type=F32)
            o = (o + pn.astype(BF16).astype(F32) * vn.astype(BF16).astype(F32)) / denom
            outs.extend(o[g:g + 1] for g in range(GQA_GROUP))
        a = jnp.concatenate(outs, axis=1)
        acc_scr[pl.ds(step * DEC_STEP + bb, 1), :] = _rms(a, g_ref[...])

    @pl.when(step == pl.num_programs(0) - 1)
    def _():
        a_ref[...] = acc_scr[...].astype(a_ref.dtype)


def _attn_sample(qh, kh, vh, cache_k, cache_v, l, sinks, g, ah, N):
    rows = lambda w: pl.BlockSpec((DEC_STEP, w), lambda s: (s, 0))
    cache = pl.BlockSpec((None, DEC_STEP, WINDOW, N_KV_HEADS, HEAD_DIM), lambda s: (l, s, 0, 0, 0))
    ncache = pl.BlockSpec((DEC_STEP, WINDOW, N_KV_HEADS, HEAD_DIM), lambda s: (s, 0, 0, 0))
    cshape = jax.ShapeDtypeStruct((N, WINDOW, N_KV_HEADS, HEAD_DIM), F32)
    return pl.pallas_call(
        _attn_sample_kernel,
        grid=(N // DEC_STEP,),
        in_specs=[rows(ATTN_WIDTH), rows(KV_WIDTH), rows(KV_WIDTH), cache, cache,
                  _full((N_KV_HEADS, GQA_GROUP, 1)), _full((1, ATTN_WIDTH)),
                  pl.BlockSpec(memory_space=pl.ANY)],
        out_specs=[pl.BlockSpec((N, ATTN_WIDTH), lambda s: (0, 0)), ncache, ncache],
        out_shape=[jax.ShapeDtypeStruct(ah.shape, ah.dtype), cshape, cshape],
        scratch_shapes=[pltpu.VMEM((N, ATTN_WIDTH), F32)],
        input_output_aliases={7: 0},
        compiler_params=_params("arbitrary"),
        name="attn_sample",
    )(qh, kh, vh, cache_k, cache_v, sinks.astype(F32).reshape(N_KV_HEADS, GQA_GROUP, 1), g, ah)


def _mix_sample_kernel(u_ref, h0_ref, pw_ref, bblk_ref, cblk_ref, d_ref, wg_ref, bg_ref, gs_ref,
                       xp_ref, pb_ref, wp_ref, sc_ref, gp_ref, s_in_ref, p_in_ref, s_ref, st_ref, p_ref):
    del s_in_ref, p_in_ref
    S = SSM_BLOCK_STATES
    u = u_ref[...]
    ub = u.astype(BF16)
    ys = []
    for j in range(SSM_LANE_BLOCKS):
        x = jnp.dot(ub[:, j * LANES:(j + 1) * LANES], bblk_ref[j], preferred_element_type=F32)
        base = j * 2 * S
        ar = pw_ref[0:1, base:base + S]
        ai = pw_ref[0:1, base + S:base + 2 * S]
        h0r = h0_ref[:, base:base + S]
        h0i = h0_ref[:, base + S:base + 2 * S]
        hr = x[:, 0:S] + ar * h0r - ai * h0i
        hi = x[:, S:] + ar * h0i + ai * h0r
        st_ref[:, base:base + S] = hr
        st_ref[:, base + S:base + 2 * S] = hi
        h = jnp.concatenate([hr, hi], axis=1).astype(BF16)
        ys.append(jnp.dot(h, cblk_ref[j], preferred_element_type=F32))
    s_ref[...] = _ssm_tail(jnp.concatenate(ys, axis=1), u, d_ref, wg_ref, bg_ref, gs_ref).astype(s_ref.dtype)

    xp = xp_ref[...]
    ds = []
    for gi, w in enumerate(POOL_WINDOWS):
        gsl = slice(gi * POOL_GROUP, (gi + 1) * POOL_GROUP)
        s = xp[:, gsl]
        for back in range(1, w):
            s = s + pb_ref[POOL_BUF - back][:, gsl]
        ds.append(s / float(w) - xp[:, gsl])
    p_ref[...] = _pool_tail(ds, wp_ref, sc_ref, gp_ref).astype(p_ref.dtype)


def _mix_sample(uh, h0, pw, bblk, cblk, d, wg, l, bg, gs, xph, pbuf, wp, _l, sc, gp, sh, ph):
    N = h0.shape[0]
    rows = lambda w: pl.BlockSpec((N, w), lambda i: (0, 0))
    anyspec = pl.BlockSpec(memory_space=pl.ANY)
    return pl.pallas_call(
        _mix_sample_kernel,
        grid=(1,),
        in_specs=[rows(SSM_WIDTH), _full(h0.shape), _full(pw.shape), _full(bblk.shape), _full(cblk.shape),
                  _full((1, SSM_WIDTH)), _layer((SSM_WIDTH, SSM_WIDTH), l), _full((1, SSM_WIDTH)),
                  _full((1, SSM_WIDTH)), rows(POOL_WIDTH), _full(pbuf.shape), _layer(wp.shape[1:], l),
                  _full((1, POOL_WIDTH)), _full((1, POOL_WIDTH)), anyspec, anyspec],
        out_specs=[rows(SSM_WIDTH), _full((N, SSM_STATE_LANES)), rows(POOL_WIDTH)],
        out_shape=[jax.ShapeDtypeStruct(sh.shape, sh.dtype), jax.ShapeDtypeStruct((N, SSM_STATE_LANES), F32),
                   jax.ShapeDtypeStruct(ph.shape, ph.dtype)],
        input_output_aliases={14: 0, 15: 2},
        compiler_params=_params("arbitrary"),
        name="mix_sample",
    )(uh, h0, pw, bblk, cblk, d, wg, bg, gs, xph, pbuf, wp, sc, gp, sh, ph)


def _outproj_kernel(x_ref, a_ref, s_ref, p_ref, w_ref, o_ref):
    o1 = ATTN_WIDTH
    o2 = o1 + SSM_WIDTH
    acc = x_ref[...]
    acc = acc + jnp.dot(a_ref[...], w_ref[0:o1, :], preferred_element_type=F32)
    acc = acc + jnp.dot(s_ref[...], w_ref[o1:o2, :], preferred_element_type=F32)
    acc = acc + jnp.dot(p_ref[...], w_ref[o2:, :], preferred_element_type=F32)
    o_ref[...] = acc


def _outproj(x, a, s, p, w, l, tm):
    R = x.shape[0]
    row = lambda i: (i, 0)
    return pl.pallas_call(
        _outproj_kernel,
        grid=(R // tm,),
        in_specs=[pl.BlockSpec((tm, D_MODEL), row), pl.BlockSpec((tm, ATTN_WIDTH), row),
                  pl.BlockSpec((tm, SSM_WIDTH), row), pl.BlockSpec((tm, POOL_WIDTH), row),
                  _layer((D_MODEL, D_MODEL), l)],
        out_specs=pl.BlockSpec((tm, D_MODEL), row),
        out_shape=jax.ShapeDtypeStruct((R, D_MODEL), F32),
        compiler_params=_params("arbitrary"),
        name="outproj",
    )(x, a, s, p, w)


def _ffn_kernel(x_ref, g_ref, w1_ref, w2_ref, o_ref, h_scr):
    f = pl.program_id(1)

    @pl.when(f == 0)
    def _():
        x = x_ref[...]
        h_scr[...] = _rms(x, g_ref[...]).astype(BF16)
        o_ref[...] = x

    h1 = jnp.dot(h_scr[...], w1_ref[...], preferred_element_type=F32)
    h1 = jnp.square(jnp.maximum(h1, 0.0)).astype(BF16)
    o_ref[...] += jnp.dot(h1, w2_ref[...], preferred_element_type=F32)


def _ffn(x, g, w1, w2, l, tm, tf):
    R = x.shape[0]
    return pl.pallas_call(
        _ffn_kernel,
        grid=(R // tm, D_FF // tf),
        in_specs=[pl.BlockSpec((tm, D_MODEL), lambda i, f: (i, 0)), _full((1, D_MODEL)),
                  pl.BlockSpec((None, D_MODEL, tf), lambda i, f: (l, 0, f)),
                  pl.BlockSpec((None, tf, D_MODEL), lambda i, f: (l, f, 0))],
        out_specs=pl.BlockSpec((tm, D_MODEL), lambda i, f: (i, 0)),
        out_shape=jax.ShapeDtypeStruct((R, D_MODEL), F32),
        scratch_shapes=[pltpu.VMEM((tm, D_MODEL), BF16)],
        compiler_params=_params("arbitrary", "arbitrary"),
        name="ffn",
    )(x, g, w1, w2)


def _rope_tables(pos):
    half = ROT_DIM // 2
    inv = ROPE_THETA ** (-jnp.arange(0, ROT_DIM, 2, dtype=F32) / ROT_DIM)
    ang = pos.astype(F32)[:, None] * inv
    cos, sin = jnp.cos(ang), jnp.sin(ang)
    n = pos.shape[0]
    z = jnp.zeros((n, HEAD_DIM - ROT_DIM), F32)
    zh = jnp.zeros((n, half), F32)
    rc = jnp.concatenate([cos, cos, z + 1.0], axis=1)
    rs1 = jnp.concatenate([zh, sin, z], axis=1)
    rs2 = jnp.concatenate([-sin, zh, z], axis=1)
    return rc, rs1, rs2


def kernel(x_prompt, x_sample, cache_k, cache_v, state_ssm_re, state_ssm_im, state_pool, meta_tokens, g_mix, w_in, g_q, g_k, sinks, A_re, A_im, log_dt, B_re, B_im, C_re, C_im, D_skip, w_glu, b_glu, w_pool, pool_scale, g_out_attn, g_out_ssm, g_out_pool, w_out, g_ffn, w_ff1, w_ff2):
    B, T, _ = x_prompt.shape
    N = x_sample.shape[0]
    depth = w_in.shape[0]
    assert N <= SEQ0 and N % DEC_STEP == 0 and T % TM_FFN == 0
    row = lambda t: t.astype(F32).reshape(1, -1)

    meta = meta_tokens.astype(F32)
    head0 = jnp.concatenate([x_sample.reshape(N, D_MODEL), jnp.zeros((SEQ0 - N, D_MODEL), F32), meta], axis=0)
    head_rest = jnp.concatenate([jnp.zeros((SEQ0, D_MODEL), F32), meta], axis=0)
    xh = jnp.concatenate([head0] + [head_rest] * (B - 1), axis=0)
    xm = x_prompt.reshape(B * T, D_MODEL)

    rope_m = _rope_tables(N_META + jnp.arange(T))
    hr_ = jnp.arange(HEAD_ROWS)
    pos_h0 = jnp.where(hr_ < N, PAST_LEN, jnp.maximum(hr_ - SEQ0, 0))
    pos_h = jnp.concatenate([pos_h0] + [jnp.maximum(hr_ - SEQ0, 0)] * (B - 1))
    rope_h = _rope_tables(pos_h)

    wi_all, wo_all = w_in.astype(BF16), w_out.astype(BF16)
    w1_all, w2_all = w_ff1.astype(BF16), w_ff2.astype(BF16)
    wg_all, wp_all = w_glu.astype(BF16), w_pool.astype(BF16)
    ck = cache_k.astype(F32)
    cv = cache_v.astype(F32)
    RH = B * HEAD_ROWS

    outs_p = [[] for _ in range(5)]
    outs_s = [[] for _ in range(5)]
    for l in range(depth):
        pw, bblk, cblk = _ssm_params(A_re[l], A_im[l], log_dt[l], B_re[l], B_im[l], C_re[l], C_im[l])
        ssm_w = (pw, bblk, cblk, row(D_skip[l]), wg_all, l, row(b_glu[l]), row(g_out_ssm[l]))
        pool_w = (wp_all, l, row(pool_scale[l]), row(g_out_pool[l]))
        in_w = (row(g_mix[l]), wi_all, l, row(g_q[l]), row(g_k[l]))

        qm, km, vm, um, pm = _inproj(xm, *in_w, rope_m, TM_PROJ)
        qh, kh, vh, uh, ph = _inproj(xh, *in_w, rope_h, RH)

        am, ah = _attn(sinks[l].astype(F32), qm, qh, km, kh, vm, vh, row(g_out_attn[l]), B)
        sm, sh, st = _ssm(um, uh, *ssm_w, B, TM_SEQ)
        plm, plh = _pool(pm, ph, *pool_w, B, TM_SEQ)

        ah, nk, nv = _attn_sample(qh, kh, vh, ck, cv, l, sinks[l], row(g_out_attn[l]), ah, N)
        h0 = jnp.concatenate([_state_to_lanes(state_ssm_re[l]), _state_to_lanes(state_ssm_im[l])],
                             axis=-1).reshape(N, SSM_STATE_LANES)
        pbuf = state_pool[l].astype(F32).transpose(1, 0, 2)
        sh, st_s, plh = _mix_sample(uh, h0, *ssm_w, ph, pbuf, *pool_w, sh, plh)

        xm = _outproj(xm, am, sm, plm, wo_all, l, TM_PROJ)
        xh = _outproj(xh, ah, sh, plh, wo_all, l, RH)
        xm = _ffn(xm, row(g_ffn[l]), w1_all, w2_all, l, TM_FFN, TF_FFN)
        xh = _ffn(xh, row(g_ffn[l]), w1_all, w2_all, l, RH, TF_FFN)

        b3 = lambda t: t.reshape(B, T, t.shape[-1])
        hr, hi = _state_from_lanes(st[:, 0])
        outs_p[0].append(b3(km)[:, T - WINDOW:].reshape(B, WINDOW, N_KV_HEADS, HEAD_DIM))
        outs_p[1].append(b3(vm)[:, T - WINDOW:].reshape(B, WINDOW, N_KV_HEADS, HEAD_DIM))
        outs_p[2].append(hr)
        outs_p[3].append(hi)
        outs_p[4].append(b3(pm)[:, T - POOL_BUF:])
        hr_s, hi_s = _state_from_lanes(st_s)
        outs_s[0].append(nk)
        outs_s[1].append(nv)
        outs_s[2].append(hr_s)
        outs_s[3].append(hi_s)
        outs_s[4].append(jnp.concatenate([state_pool[l].astype(F32)[:, 1:], ph[:N, None]], axis=1))

    y_prompt = xm.reshape(B, T, D_MODEL)
    y_sample = xh[:N].reshape(N, 1, D_MODEL)
    np_ = [jnp.stack(t) for t in outs_p]
    ns_ = [jnp.stack(t) for t in outs_s]
    return (y_prompt, y_sample, *np_, *ns_)
```

```python
import jax
import jax.numpy as jnp
from jax.experimental import pallas as pl
from jax.experimental.pallas import tpu as pltpu

D_MODEL = 2048
N_META = 16
HEAD_DIM = 128
N_HEADS = 8
N_KV_HEADS = 2
GQA_GROUP = 4
ATTN_WIDTH = 1024
KV_WIDTH = 256
WINDOW = 128
BLOCK = 128
ROT_DIM = 32
ROPE_THETA = 500000.0
SSM_WIDTH = 512
SSM_GROUP_SIZE = 16
SSM_GROUPS = 32
SSM_STATE = 64
POOL_WIDTH = 512
POOL_WINDOWS = (2, 4, 8, 16)
POOL_GROUP = 128
POOL_BUF = 15
POOL_HALO = 16
IN_WIDTH = 2560
D_FF = 8192
EPS = 1e-6
PAST_LEN = 16384

HEAD_ROWS = BLOCK
SEQ0 = HEAD_ROWS - N_META
LANES = 128
SUBLANES = 8
SSM_LANE_BLOCKS = SSM_WIDTH // LANES
SSM_BLOCK_STATES = (LANES // SSM_GROUP_SIZE) * SSM_STATE
SSM_STATE_LANES = SSM_LANE_BLOCKS * 2 * SSM_BLOCK_STATES
VMEM_LIMIT = 56 * 1024 * 1024

TM_PROJ = 512
TM_FFN = 1024
TF_FFN = 512
TM_SEQ = 512
DEC_STEP = 8
SSM_POW_ROWS = (1, HEAD_ROWS // SUBLANES, TM_SEQ // SUBLANES)

BF16 = jnp.bfloat16
F32 = jnp.float32


def _params(*semantics):
    return pltpu.CompilerParams(dimension_semantics=semantics, vmem_limit_bytes=VMEM_LIMIT)


def _rms(x, g):
    return x * jax.lax.rsqrt(jnp.mean(x * x, axis=-1, keepdims=True) + EPS) * g


def _full(shape):
    n = len(shape)
    return pl.BlockSpec(shape, lambda *_: (0,) * n)


def _layer(shape, l):
    n = len(shape)
    return pl.BlockSpec((None, *shape), lambda *_: (l,) + (0,) * n)


def _inproj_kernel(x_ref, g_ref, w_ref, gq_ref, gk_ref, rc_ref, rs1_ref, rs2_ref,
                   q_ref, k_ref, v_ref, u_ref, xp_ref):
    h = _rms(x_ref[...], g_ref[...]).astype(BF16)
    proj = jnp.dot(h, w_ref[...], preferred_element_type=F32)
    rc, rs1, rs2 = rc_ref[...], rs1_ref[...], rs2_ref[...]

    def head(t, g):
        t = _rms(t, g)
        return t * rc + pltpu.roll(t, 16, 1) * rs1 + pltpu.roll(t, LANES - 16, 1) * rs2

    for hd in range(N_HEADS):
        sl = slice(hd * HEAD_DIM, (hd + 1) * HEAD_DIM)
        q_ref[:, sl] = head(proj[:, sl], gq_ref[...])
    for hd in range(N_KV_HEADS):
        sl = slice(hd * HEAD_DIM, (hd + 1) * HEAD_DIM)
        k_ref[:, sl] = head(proj[:, ATTN_WIDTH + hd * HEAD_DIM:ATTN_WIDTH + (hd + 1) * HEAD_DIM], gk_ref[...])
    o2 = ATTN_WIDTH + KV_WIDTH
    o3 = o2 + KV_WIDTH
    o4 = o3 + SSM_WIDTH
    v_ref[...] = proj[:, o2:o3]
    for j in range(SSM_LANE_BLOCKS):
        u_ref[j] = proj[:, o3 + j * LANES:o3 + (j + 1) * LANES]
    xp_ref[...] = proj[:, o4:]


def _inproj(x, g, w, l, gq, gk, rope, tm):
    R = x.shape[0]
    tiles_per_rope = rope[0].shape[0] // tm
    row = lambda i: (i, 0)
    rrow = lambda i: (i % tiles_per_rope, 0)
    flat = lambda w_: (pl.BlockSpec((tm, w_), row), jax.ShapeDtypeStruct((R, w_), F32))
    u_out = (pl.BlockSpec((SSM_LANE_BLOCKS, tm, LANES), lambda i: (0, i, 0)),
             jax.ShapeDtypeStruct((SSM_LANE_BLOCKS, R, LANES), F32))
    outs = [flat(ATTN_WIDTH), flat(KV_WIDTH), flat(KV_WIDTH), u_out, flat(POOL_WIDTH)]
    return pl.pallas_call(
        _inproj_kernel,
        grid=(R // tm,),
        in_specs=[pl.BlockSpec((tm, D_MODEL), row), _full((1, D_MODEL)), _layer((D_MODEL, IN_WIDTH), l),
                  _full((1, HEAD_DIM)), _full((1, HEAD_DIM)),
                  pl.BlockSpec((tm, LANES), rrow), pl.BlockSpec((tm, LANES), rrow),
                  pl.BlockSpec((tm, LANES), rrow)],
        out_specs=[o[0] for o in outs],
        out_shape=[o[1] for o in outs],
        compiler_params=_params("arbitrary"),
        name="inproj",
    )(x, g, w, gq, gk, *rope)


def _attn_kernel(sink_ref, qm_ref, qh_ref, kpm_ref, kcm_ref, kh_ref, vpm_ref, vcm_ref, vh_ref, g_ref,
                 om_ref, oh_ref):
    i = pl.program_id(1)
    first = i == 0
    second = i == 1
    q_blk = jnp.where(first, qh_ref[...], qm_ref[...])
    kc_blk = jnp.where(first, kh_ref[...], kcm_ref[...])
    vc_blk = jnp.where(first, vh_ref[...], vcm_ref[...])
    kp_blk = jnp.where(second, kh_ref[...], kpm_ref[...])
    vp_blk = jnp.where(second, vh_ref[...], vpm_ref[...])
    rows = GQA_GROUP * BLOCK
    r = jax.lax.broadcasted_iota(jnp.int32, (rows, 2 * BLOCK), 0) & (BLOCK - 1)
    c = jax.lax.broadcasted_iota(jnp.int32, (rows, 2 * BLOCK), 1)
    diff = BLOCK + r - c
    krow = (i - 1) * BLOCK + c
    mask = (diff >= 0) & (diff <= WINDOW) & (krow >= SEQ0)
    rgrp = jax.lax.broadcasted_iota(jnp.int32, (rows, 1), 0) // BLOCK
    outs = []
    for kh in range(N_KV_HEADS):
        ksl = slice(kh * HEAD_DIM, (kh + 1) * HEAD_DIM)
        qh = jnp.concatenate(
            [q_blk[:, (kh * GQA_GROUP + g) * HEAD_DIM:(kh * GQA_GROUP + g + 1) * HEAD_DIM]
             for g in range(GQA_GROUP)], axis=0).astype(BF16)
        kk = jnp.concatenate([kp_blk[:, ksl], kc_blk[:, ksl]], axis=0).astype(BF16)
        vv = jnp.concatenate([vp_blk[:, ksl], vc_blk[:, ksl]], axis=0).astype(BF16)
        s = jax.lax.dot_general(qh, kk, (((1,), (1,)), ((), ())),
                                preferred_element_type=F32) * (HEAD_DIM ** -0.5)
        s = jnp.where(mask, s, -jnp.inf)
        sk = jnp.zeros((rows, 1), F32)
        for g in range(GQA_GROUP):
            sk = jnp.where(rgrp == g, sink_ref[kh * GQA_GROUP + g], sk)
        m = jnp.maximum(jnp.max(s, axis=-1, keepdims=True), sk)
        p = jnp.exp(s - m)
        denom = jnp.sum(p, axis=-1, keepdims=True) + jnp.exp(sk - m)
        o = jnp.dot(p.astype(BF16), vv, preferred_element_type=F32) / denom
        outs.extend(o[g * BLOCK:(g + 1) * BLOCK] for g in range(GQA_GROUP))
    a = _rms(jnp.concatenate(outs, axis=1), g_ref[...]).astype(om_ref.dtype)

    @pl.when(first)
    def _():
        oh_ref[...] = a

    @pl.when(jnp.logical_not(first))
    def _():
        om_ref[...] = a


def _attn(sinks, qm, qh, km, kh, vm, vh, g, B):
    nb = qm.shape[0] // (B * BLOCK)
    cur = lambda b, i, s: (b * nb + jnp.maximum(i - 1, 0), 0)
    prev = lambda b, i, s: (b * nb + jnp.maximum(i - 2, 0), 0)
    head = lambda b, i, s: (b, 0)
    spec = lambda w, im: pl.BlockSpec((BLOCK, w), im)
    return pl.pallas_call(
        _attn_kernel,
        grid_spec=pltpu.PrefetchScalarGridSpec(
            num_scalar_prefetch=1,
            grid=(B, nb + 1),
            in_specs=[spec(ATTN_WIDTH, cur), spec(ATTN_WIDTH, head),
                      spec(KV_WIDTH, prev), spec(KV_WIDTH, cur), spec(KV_WIDTH, head),
                      spec(KV_WIDTH, prev), spec(KV_WIDTH, cur), spec(KV_WIDTH, head),
                      pl.BlockSpec((1, ATTN_WIDTH), lambda b, i, s: (0, 0))],
            out_specs=[spec(ATTN_WIDTH, cur), spec(ATTN_WIDTH, head)],
        ),
        out_shape=[jax.ShapeDtypeStruct(qm.shape, BF16), jax.ShapeDtypeStruct(qh.shape, BF16)],
        compiler_params=_params("arbitrary", "arbitrary"),
        name="attn_prompt",
    )(sinks, qm, qh, km, km, kh, vm, vm, vh, g)


def _ssm_params_kernel(ar_ref, ai_ref, ldt_ref, kk_ref, br_ref, bi_ref, tr_ref, ti_ref, bbr_ref, bbi_ref):
    ar, ai = ar_ref[...], ai_ref[...]
    dt = jnp.exp(ldt_ref[...])
    kk = kk_ref[...]
    mag = jnp.exp(dt * ar * kk)
    ang = dt * ai * kk
    tr = mag * jnp.cos(ang)
    ti = mag * jnp.sin(ang)
    tr_ref[...] = tr
    ti_ref[...] = ti
    abr, abi = tr[:, 0:1], ti[:, 0:1]
    den = ar * ar + ai * ai
    fr = ((abr - 1.0) * ar + abi * ai) / den
    fi = (abi * ar - (abr - 1.0) * ai) / den
    br, bi = br_ref[...], bi_ref[...]
    bbr_ref[...] = fr * br - fi * bi
    bbi_ref[...] = fr * bi + fi * br


def _ssm_params(A_re, A_im, log_dt, B_re, B_im, C_re, C_im):
    n = SSM_GROUPS * SSM_STATE
    col = lambda t: t.astype(F32).reshape(n, 1)
    ldt = jnp.broadcast_to(log_dt.astype(F32)[:, None], (SSM_GROUPS, SSM_STATE)).reshape(n, 1)
    kk = jnp.array([SSM_POW_ROWS + (0,) * (SUBLANES - len(SSM_POW_ROWS))], F32)
    shapes = [(n, SUBLANES), (n, SUBLANES), (n, SSM_GROUP_SIZE), (n, SSM_GROUP_SIZE)]
    tr, ti, bbr, bbi = pl.pallas_call(
        _ssm_params_kernel,
        out_shape=[jax.ShapeDtypeStruct(s, F32) for s in shapes],
        name="ssm_params",
    )(col(A_re), col(A_im), ldt, kk, B_re.astype(F32).reshape(n, SSM_GROUP_SIZE),
      B_im.astype(F32).reshape(n, SSM_GROUP_SIZE))
    J, G8 = SSM_LANE_BLOCKS, LANES // SSM_GROUP_SIZE

    def lanes(t):
        return t.T.reshape(SUBLANES, J, SSM_BLOCK_STATES)

    pw = jnp.concatenate([lanes(tr), lanes(ti)], axis=-1).reshape(SUBLANES, SSM_STATE_LANES)
    eye = jnp.eye(G8, dtype=F32)

    def bdiag(t):
        t = t.reshape(J, G8, SSM_STATE, SSM_GROUP_SIZE).transpose(0, 1, 3, 2)
        t = t[:, :, :, None, :] * eye[None, :, None, :, None]
        return t.reshape(J, LANES, SSM_BLOCK_STATES)

    bblk = jnp.concatenate([bdiag(bbr), bdiag(bbi)], axis=-1).astype(BF16)

    def cdiag(t):
        t = t.astype(F32).reshape(J, G8, SSM_GROUP_SIZE, SSM_STATE).transpose(0, 1, 3, 2)
        t = t[:, :, :, None, :] * eye[None, :, None, :, None]
        return t.reshape(J, SSM_BLOCK_STATES, LANES)

    cblk = jnp.concatenate([cdiag(C_re), -cdiag(C_im)], axis=1).astype(BF16)
    return pw, bblk, cblk


def _ssm_tail(y, u, d_ref, wg_ref, bg_ref, g_ref):
    y = y + d_ref[...] * u
    z = jax.nn.gelu(y)
    gate = jax.nn.sigmoid(jnp.dot(z.astype(BF16), wg_ref[...], preferred_element_type=F32) + bg_ref[...])
    return _rms(z * gate, g_ref[...])


def _ssm_sweep(x_scr, n, a_tabs, init, store):
    S = SSM_BLOCK_STATES
    fins = []
    for j0 in range(0, SSM_LANE_BLOCKS, 2):
        js = (j0, j0 + 1)

        def body(k, carry, js=js):
            r0 = pl.multiple_of(k * SUBLANES, SUBLANES)
            out = []
            for idx, j in enumerate(js):
                hr, hi = carry[2 * idx], carry[2 * idx + 1]
                base = j * 2 * S
                ar, ai = a_tabs[j]
                nhr = ar * hr - ai * hi + x_scr[pl.ds(r0, SUBLANES), base:base + S]
                nhi = ar * hi + ai * hr + x_scr[pl.ds(r0, SUBLANES), base + S:base + 2 * S]
                if store:
                    x_scr[pl.ds(r0, SUBLANES), base:base + S] = nhr
                    x_scr[pl.ds(r0, SUBLANES), base + S:base + 2 * S] = nhi
                out += [nhr, nhi]
            return tuple(out)

        c0 = tuple(t for j in js for t in init[j])
        res = jax.lax.fori_loop(0, n // SUBLANES, body, c0, unroll=2)
        fins += [(res[0], res[1]), (res[2], res[3])]
    return fins


def _ssm_rows(u, pow_row, pw_ref, bblk_ref, cblk_ref, x_scr, s_scr, carry_scr):
    n = u.shape[0]
    S = SSM_BLOCK_STATES
    ub = u.astype(BF16)
    for j in range(SSM_LANE_BLOCKS):
        x_scr[0:n, j * 2 * S:(j + 1) * 2 * S] = jnp.dot(ub[:, j * LANES:(j + 1) * LANES], bblk_ref[j],
                                                       preferred_element_type=F32)
    bc = lambda t: jnp.broadcast_to(t, (SUBLANES, S))
    re = lambda ref, r0, r1, j: ref[r0:r1, j * 2 * S:j * 2 * S + S]
    im = lambda ref, r0, r1, j: ref[r0:r1, j * 2 * S + S:(j + 1) * 2 * S]
    a_tabs = [(bc(re(pw_ref, 0, 1, j)), bc(im(pw_ref, 0, 1, j))) for j in range(SSM_LANE_BLOCKS)]
    zero = jnp.zeros((SUBLANES, S), F32)
    fins = _ssm_sweep(x_scr, n, a_tabs, [(zero, zero)] * SSM_LANE_BLOCKS, store=False)
    for j in range(SSM_LANE_BLOCKS):
        base = j * 2 * S
        cr, ci = re(pw_ref, pow_row, pow_row + 1, j), im(pw_ref, pow_row, pow_row + 1, j)
        sr, si = re(carry_scr, 0, 1, j), im(carry_scr, 0, 1, j)
        fr, fi = fins[j]
        for c in range(SUBLANES):
            s_scr[c:c + 1, base:base + S] = sr
            s_scr[c:c + 1, base + S:base + 2 * S] = si
            sr, si = cr * sr - ci * si + fr[c:c + 1], cr * si + ci * sr + fi[c:c + 1]
        carry_scr[:, base:base + S] = bc(sr)
        carry_scr[:, base + S:base + 2 * S] = bc(si)
    init = [(re(s_scr, 0, SUBLANES, j), im(s_scr, 0, SUBLANES, j)) for j in range(SSM_LANE_BLOCKS)]
    _ssm_sweep(x_scr, n, a_tabs, init, store=True)
    ys = [jnp.dot(x_scr[0:n, j * 2 * S:(j + 1) * 2 * S].astype(BF16), cblk_ref[j], preferred_element_type=F32)
          for j in range(SSM_LANE_BLOCKS)]
    return jnp.concatenate(ys, axis=1)


def _ssm_tile(u_ref, n, pow_row, seq_start, refs, o_ref, scr):
    pw_ref, bblk_ref, cblk_ref, d_ref, wg_ref, bg_ref, g_ref = refs
    up_scr, x_scr, s_scr, carry_scr, o_scr = scr
    q = n // SUBLANES
    for j in range(SSM_LANE_BLOCKS):
        for k in range(q):
            up_scr[k * SUBLANES:(k + 1) * SUBLANES, j * LANES:(j + 1) * LANES] = \
                u_ref[j, pl.ds(k, SUBLANES, stride=q), :]
    u = up_scr[0:n, :]
    if seq_start:
        p = jax.lax.broadcasted_iota(jnp.int32, (n, 1), 0)
        u = jnp.where((p % SUBLANES) * q + p // SUBLANES >= seq_start, u, 0.0)
    y = _ssm_rows(u, pow_row, pw_ref, bblk_ref, cblk_ref, x_scr, s_scr, carry_scr)
    out = _ssm_tail(y, u, d_ref, wg_ref, bg_ref, g_ref)
    for j in range(SSM_LANE_BLOCKS):
        for k in range(q):
            o_scr[j, pl.ds(k, SUBLANES, stride=q), :] = out[k * SUBLANES:(k + 1) * SUBLANES,
                                                            j * LANES:(j + 1) * LANES]
    o_ref[...] = jnp.concatenate([o_scr[j, 0:n, :] for j in range(SSM_LANE_BLOCKS)], axis=1).astype(o_ref.dtype)


def _ssm_kernel(um_ref, uh_ref, pw_ref, bblk_ref, cblk_ref, d_ref, wg_ref, bg_ref, g_ref,
                om_ref, oh_ref, st_ref, up_scr, x_scr, s_scr, carry_scr, o_scr):
    t = pl.program_id(1)
    refs = (pw_ref, bblk_ref, cblk_ref, d_ref, wg_ref, bg_ref, g_ref)
    scr = (up_scr, x_scr, s_scr, carry_scr, o_scr)

    @pl.when(t == 0)
    def _():
        carry_scr[...] = jnp.zeros_like(carry_scr)
        _ssm_tile(uh_ref, HEAD_ROWS, 1, SEQ0, refs, oh_ref, scr)

    @pl.when(t > 0)
    def _():
        _ssm_tile(um_ref, um_ref.shape[1], 2, 0, refs, om_ref, scr)
        st_ref[...] = carry_scr[...]


def _ssm(um, uh, pw, bblk, cblk, d, wg, l, bg, g, B, tm):
    assert (1, HEAD_ROWS // SUBLANES, tm // SUBLANES) == SSM_POW_ROWS
    J = SSM_LANE_BLOCKS
    rm, rh = um.shape[1], uh.shape[1]
    nt = rm // (B * tm)
    main = lambda b, t: (b * nt + jnp.maximum(t - 1, 0), 0)
    head = lambda b, t: (b, 0)
    return pl.pallas_call(
        _ssm_kernel,
        grid=(B, nt + 1),
        in_specs=[pl.BlockSpec((J, tm, LANES), lambda b, t: (0, b * nt + jnp.maximum(t - 1, 0), 0)),
                  pl.BlockSpec((J, HEAD_ROWS, LANES), lambda b, t: (0, b, 0)),
                  _full((SUBLANES, SSM_STATE_LANES)), _full(bblk.shape), _full(cblk.shape),
                  _full((1, SSM_WIDTH)), _layer((SSM_WIDTH, SSM_WIDTH), l), _full((1, SSM_WIDTH)),
                  _full((1, SSM_WIDTH))],
        out_specs=[pl.BlockSpec((tm, SSM_WIDTH), main), pl.BlockSpec((HEAD_ROWS, SSM_WIDTH), head),
                   pl.BlockSpec((None, SUBLANES, SSM_STATE_LANES), lambda b, t: (b, 0, 0))],
        out_shape=[jax.ShapeDtypeStruct((rm, SSM_WIDTH), BF16), jax.ShapeDtypeStruct((rh, SSM_WIDTH), BF16),
                   jax.ShapeDtypeStruct((B, SUBLANES, SSM_STATE_LANES), F32)],
        scratch_shapes=[pltpu.VMEM((tm, SSM_WIDTH), F32),
                        pltpu.VMEM((tm, SSM_STATE_LANES), F32),
                        pltpu.VMEM((SUBLANES, SSM_STATE_LANES), F32),
                        pltpu.VMEM((SUBLANES, SSM_STATE_LANES), F32),
                        pltpu.VMEM((J, tm, LANES), F32)],
        compiler_params=_params("arbitrary", "arbitrary"),
        name="ssm_prompt",
    )(um, uh, pw, bblk, cblk, d, wg, bg, g)


def _state_to_lanes(h):
    return h.astype(F32).reshape(h.shape[0], SSM_LANE_BLOCKS, SSM_BLOCK_STATES)


def _state_from_lanes(s):
    s = s.reshape(s.shape[0], SSM_LANE_BLOCKS, 2, SSM_BLOCK_STATES)
    return (s[:, :, 0].reshape(-1, SSM_GROUPS, SSM_STATE), s[:, :, 1].reshape(-1, SSM_GROUPS, SSM_STATE))


def _pool_tail(d_groups, w_ref, sc_ref, g_ref):
    y = jnp.concatenate(
        [jnp.dot(d.astype(BF16), w_ref[gi], preferred_element_type=F32) for gi, d in enumerate(d_groups)], axis=1)
    return _rms(y * sc_ref[...], g_ref[...])


def _pool_rows(x, prev, pos0, w_ref, sc_ref, g_ref):
    n = x.shape[0]
    xe = jnp.concatenate([prev, x], axis=0)
    pos = pos0 + jax.lax.broadcasted_iota(jnp.int32, (n, 1), 0)
    ds = []
    for gi, w in enumerate(POOL_WINDOWS):
        gsl = slice(gi * POOL_GROUP, (gi + 1) * POOL_GROUP)
        s = xe[:, gsl]
        k = 1
        while k < w:
            s = s + pltpu.roll(s, k, 0)
            k *= 2
        cnt = jnp.clip(pos + 1, 1, w).astype(F32)
        ds.append(s[POOL_HALO:] / cnt - x[:, gsl])
    return _pool_tail(ds, w_ref, sc_ref, g_ref)


def _pool_kernel(xm_ref, halo_ref, xh_ref, w_ref, sc_ref, g_ref, om_ref, oh_ref):
    t = pl.program_id(1)
    tm = xm_ref.shape[0]

    @pl.when(t == 0)
    def _():
        hrow = jax.lax.broadcasted_iota(jnp.int32, (HEAD_ROWS, 1), 0)
        x = jnp.where(hrow >= SEQ0, xh_ref[...], 0.0)
        prev = jnp.zeros((POOL_HALO, POOL_WIDTH), F32)
        oh_ref[...] = _pool_rows(x, prev, -SEQ0, w_ref, sc_ref, g_ref).astype(oh_ref.dtype)

    @pl.when(t > 0)
    def _():
        prev = jnp.where(t == 1, xh_ref[HEAD_ROWS - POOL_HALO:, :], halo_ref[...])
        om_ref[...] = _pool_rows(xm_ref[...], prev, N_META + (t - 1) * tm, w_ref, sc_ref, g_ref).astype(om_ref.dtype)


def _pool(xm, xh, w, l, sc, g, B, tm):
    nt = xm.shape[0] // (B * tm)
    r = tm // POOL_HALO
    main = lambda b, t: (b * nt + jnp.maximum(t - 1, 0), 0)
    halo = lambda b, t: (jnp.maximum((b * nt + t - 1) * r - 1, 0), 0)
    head = lambda b, t: (b, 0)
    return pl.pallas_call(
        _pool_kernel,
        grid=(B, nt + 1),
        in_specs=[pl.BlockSpec((tm, POOL_WIDTH), main), pl.BlockSpec((POOL_HALO, POOL_WIDTH), halo),
                  pl.BlockSpec((HEAD_ROWS, POOL_WIDTH), head),
                  _layer(w.shape[1:], l), _full((1, POOL_WIDTH)), _full((1, POOL_WIDTH))],
        out_specs=[pl.BlockSpec((tm, POOL_WIDTH), main), pl.BlockSpec((HEAD_ROWS, POOL_WIDTH), head)],
        out_shape=[jax.ShapeDtypeStruct(xm.shape, BF16), jax.ShapeDtypeStruct(xh.shape, BF16)],
        compiler_params=_params("arbitrary", "arbitrary"),
        name="pool_prompt",
    )(xm, xm, xh, w, sc, g)


def _attn_sample_kernel(q_ref, kn_ref, vn_ref, kc_ref, vc_ref, sink_ref, g_ref, a_in_ref,
                        a_ref, nk_ref, nv_ref, acc_scr):
    del a_in_ref
    step = pl.program_id(0)
    scale = HEAD_DIM ** -0.5
    for bb in range(DEC_STEP):
        nk_ref[bb, 0:WINDOW - 1] = kc_ref[bb, 1:WINDOW]
        nv_ref[bb, 0:WINDOW - 1] = vc_ref[bb, 1:WINDOW]
        outs = []
        for kh in range(N_KV_HEADS):
            ksl = slice(kh * HEAD_DIM, (kh + 1) * HEAD_DIM)
            qh = jnp.concatenate(
                [q_ref[bb:bb + 1, (kh * GQA_GROUP + g) * HEAD_DIM:(kh * GQA_GROUP + g + 1) * HEAD_DIM]
                 for g in range(GQA_GROUP)], axis=0)
            kn = kn_ref[bb:bb + 1, ksl]
            vn = vn_ref[bb:bb + 1, ksl]
            nk_ref[bb, WINDOW - 1, kh:kh + 1, :] = kn
            nv_ref[bb, WINDOW - 1, kh:kh + 1, :] = vn
            kc = kc_ref[bb, :, kh, :]
            vc = vc_ref[bb, :, kh, :]
            sc = jax.lax.dot_general(qh.astype(BF16), kc.astype(BF16), (((1,), (1,)), ((), ())),
                                     preferred_element_type=F32) * scale
            sn = jnp.sum(qh.astype(BF16).astype(F32) * kn.astype(BF16).astype(F32), axis=-1, keepdims=True) * scale
            sk = sink_ref[kh]
            m = jnp.maximum(jnp.maximum(jnp.max(sc, axis=-1, keepdims=True), sn), sk)
            pc = jnp.exp(sc - m)
            pn = jnp.exp(sn - m)
            denom = jnp.sum(pc, axis=-1, keepdims=True) + pn + jnp.exp(sk - m)
            o = jnp.dot(pc.astype(BF16), vc.astype(BF16), preferred_element_type=F32)
            o = (o + pn.astype(BF16).astype(F32) * vn.astype(BF16).astype(F32)) / denom
            outs.extend(o[g:g + 1] for g in range(GQA_GROUP))
        a = jnp.concatenate(outs, axis=1)
        acc_scr[pl.ds(step * DEC_STEP + bb, 1), :] = _rms(a, g_ref[...])

    @pl.when(step == pl.num_programs(0) - 1)
    def _():
        a_ref[...] = acc_scr[...].astype(a_ref.dtype)


def _attn_sample(qh, kh, vh, cache_k, cache_v, l, sinks, g, ah, N):
    rows = lambda w: pl.BlockSpec((DEC_STEP, w), lambda s: (s, 0))
    cache = pl.BlockSpec((None, DEC_STEP, WINDOW, N_KV_HEADS, HEAD_DIM), lambda s: (l, s, 0, 0, 0))
    ncache = pl.BlockSpec((DEC_STEP, WINDOW, N_KV_HEADS, HEAD_DIM), lambda s: (s, 0, 0, 0))
    cshape = jax.ShapeDtypeStruct((N, WINDOW, N_KV_HEADS, HEAD_DIM), F32)
    return pl.pallas_call(
        _attn_sample_kernel,
        grid=(N // DEC_STEP,),
        in_specs=[rows(ATTN_WIDTH), rows(KV_WIDTH), rows(KV_WIDTH), cache, cache,
                  _full((N_KV_HEADS, GQA_GROUP, 1)), _full((1, ATTN_WIDTH)),
                  pl.BlockSpec(memory_space=pl.ANY)],
        out_specs=[pl.BlockSpec((N, ATTN_WIDTH), lambda s: (0, 0)), ncache, ncache],
        out_shape=[jax.ShapeDtypeStruct(ah.shape, ah.dtype), cshape, cshape],
        scratch_shapes=[pltpu.VMEM((N, ATTN_WIDTH), F32)],
        input_output_aliases={7: 0},
        compiler_params=_params("arbitrary"),
        name="attn_sample",
    )(qh, kh, vh, cache_k, cache_v, sinks.astype(F32).reshape(N_KV_HEADS, GQA_GROUP, 1), g, ah)


def _mix_sample_kernel(u_ref, h0_ref, pw_ref, bblk_ref, cblk_ref, d_ref, wg_ref, bg_ref, gs_ref,
                       xp_ref, pb_ref, wp_ref, sc_ref, gp_ref, s_in_ref, p_in_ref, s_ref, st_ref, p_ref):
    del s_in_ref, p_in_ref
    S = SSM_BLOCK_STATES
    u = jnp.concatenate([u_ref[j] for j in range(SSM_LANE_BLOCKS)], axis=1)
    ub = u.astype(BF16)
    ys = []
    for j in range(SSM_LANE_BLOCKS):
        x = jnp.dot(ub[:, j * LANES:(j + 1) * LANES], bblk_ref[j], preferred_element_type=F32)
        base = j * 2 * S
        ar = pw_ref[0:1, base:base + S]
        ai = pw_ref[0:1, base + S:base + 2 * S]
        h0r = h0_ref[:, base:base + S]
        h0i = h0_ref[:, base + S:base + 2 * S]
        hr = x[:, 0:S] + ar * h0r - ai * h0i
        hi = x[:, S:] + ar * h0i + ai * h0r
        st_ref[:, base:base + S] = hr
        st_ref[:, base + S:base + 2 * S] = hi
        h = jnp.concatenate([hr, hi], axis=1).astype(BF16)
        ys.append(jnp.dot(h, cblk_ref[j], preferred_element_type=F32))
    s_ref[...] = _ssm_tail(jnp.concatenate(ys, axis=1), u, d_ref, wg_ref, bg_ref, gs_ref).astype(s_ref.dtype)

    xp = xp_ref[...]
    ds = []
    for gi, w in enumerate(POOL_WINDOWS):
        gsl = slice(gi * POOL_GROUP, (gi + 1) * POOL_GROUP)
        s = xp[:, gsl]
        for back in range(1, w):
            s = s + pb_ref[POOL_BUF - back][:, gsl]
        ds.append(s / float(w) - xp[:, gsl])
    p_ref[...] = _pool_tail(ds, wp_ref, sc_ref, gp_ref).astype(p_ref.dtype)


def _mix_sample(uh, h0, pw, bblk, cblk, d, wg, l, bg, gs, xph, pbuf, wp, _l, sc, gp, sh, ph):
    N = h0.shape[0]
    rows = lambda w: pl.BlockSpec((N, w), lambda i: (0, 0))
    anyspec = pl.BlockSpec(memory_space=pl.ANY)
    return pl.pallas_call(
        _mix_sample_kernel,
        grid=(1,),
        in_specs=[pl.BlockSpec((SSM_LANE_BLOCKS, N, LANES), lambda i: (0, 0, 0)),
                  _full(h0.shape), _full(pw.shape), _full(bblk.shape), _full(cblk.shape),
                  _full((1, SSM_WIDTH)), _layer((SSM_WIDTH, SSM_WIDTH), l), _full((1, SSM_WIDTH)),
                  _full((1, SSM_WIDTH)), rows(POOL_WIDTH), _full(pbuf.shape), _layer(wp.shape[1:], l),
                  _full((1, POOL_WIDTH)), _full((1, POOL_WIDTH)), anyspec, anyspec],
        out_specs=[rows(SSM_WIDTH), _full((N, SSM_STATE_LANES)), rows(POOL_WIDTH)],
        out_shape=[jax.ShapeDtypeStruct(sh.shape, sh.dtype), jax.ShapeDtypeStruct((N, SSM_STATE_LANES), F32),
                   jax.ShapeDtypeStruct(ph.shape, ph.dtype)],
        input_output_aliases={14: 0, 15: 2},
        compiler_params=_params("arbitrary"),
        name="mix_sample",
    )(uh, h0, pw, bblk, cblk, d, wg, bg, gs, xph, pbuf, wp, sc, gp, sh, ph)


def _outproj_kernel(x_ref, a_ref, s_ref, p_ref, w_ref, o_ref):
    o1 = ATTN_WIDTH
    o2 = o1 + SSM_WIDTH
    acc = x_ref[...]
    acc = acc + jnp.dot(a_ref[...], w_ref[0:o1, :], preferred_element_type=F32)
    acc = acc + jnp.dot(s_ref[...], w_ref[o1:o2, :], preferred_element_type=F32)
    acc = acc + jnp.dot(p_ref[...], w_ref[o2:, :], preferred_element_type=F32)
    o_ref[...] = acc


def _outproj(x, a, s, p, w, l, tm):
    R = x.shape[0]
    row = lambda i: (i, 0)
    return pl.pallas_call(
        _outproj_kernel,
        grid=(R // tm,),
        in_specs=[pl.BlockSpec((tm, D_MODEL), row), pl.BlockSpec((tm, ATTN_WIDTH), row),
                  pl.BlockSpec((tm, SSM_WIDTH), row), pl.BlockSpec((tm, POOL_WIDTH), row),
                  _layer((D_MODEL, D_MODEL), l)],
        out_specs=pl.BlockSpec((tm, D_MODEL), row),
        out_shape=jax.ShapeDtypeStruct((R, D_MODEL), F32),
        compiler_params=_params("arbitrary"),
        name="outproj",
    )(x, a, s, p, w)


def _ffn_kernel(x_ref, g_ref, w1_ref, w2_ref, o_ref, h_scr):
    f = pl.program_id(1)

    @pl.when(f == 0)
    def _():
        x = x_ref[...]
        h_scr[...] = _rms(x, g_ref[...]).astype(BF16)
        o_ref[...] = x

    h1 = jnp.dot(h_scr[...], w1_ref[...], preferred_element_type=F32)
    h1 = jnp.square(jnp.maximum(h1, 0.0)).astype(BF16)
    o_ref[...] += jnp.dot(h1, w2_ref[...], preferred_element_type=F32)


def _ffn(x, g, w1, w2, l, tm, tf):
    R = x.shape[0]
    return pl.pallas_call(
        _ffn_kernel,
        grid=(R // tm, D_FF // tf),
        in_specs=[pl.BlockSpec((tm, D_MODEL), lambda i, f: (i, 0)), _full((1, D_MODEL)),
                  pl.BlockSpec((None, D_MODEL, tf), lambda i, f: (l, 0, f)),
                  pl.BlockSpec((None, tf, D_MODEL), lambda i, f: (l, f, 0))],
        out_specs=pl.BlockSpec((tm, D_MODEL), lambda i, f: (i, 0)),
        out_shape=jax.ShapeDtypeStruct((R, D_MODEL), F32),
        scratch_shapes=[pltpu.VMEM((tm, D_MODEL), BF16)],
        compiler_params=_params("arbitrary", "arbitrary"),
        name="ffn",
    )(x, g, w1, w2)


def _rope_tables(pos):
    half = ROT_DIM // 2
    inv = ROPE_THETA ** (-jnp.arange(0, ROT_DIM, 2, dtype=F32) / ROT_DIM)
    ang = pos.astype(F32)[:, None] * inv
    cos, sin = jnp.cos(ang), jnp.sin(ang)
    n = pos.shape[0]
    z = jnp.zeros((n, HEAD_DIM - ROT_DIM), F32)
    zh = jnp.zeros((n, half), F32)
    rc = jnp.concatenate([cos, cos, z + 1.0], axis=1)
    rs1 = jnp.concatenate([zh, sin, z], axis=1)
    rs2 = jnp.concatenate([-sin, zh, z], axis=1)
    return rc, rs1, rs2


def kernel(x_prompt, x_sample, cache_k, cache_v, state_ssm_re, state_ssm_im, state_pool, meta_tokens, g_mix, w_in, g_q, g_k, sinks, A_re, A_im, log_dt, B_re, B_im, C_re, C_im, D_skip, w_glu, b_glu, w_pool, pool_scale, g_out_attn, g_out_ssm, g_out_pool, w_out, g_ffn, w_ff1, w_ff2):
    B, T, _ = x_prompt.shape
    N = x_sample.shape[0]
    depth = w_in.shape[0]
    assert N <= SEQ0 and N % DEC_STEP == 0 and T % TM_FFN == 0
    row = lambda t: t.astype(F32).reshape(1, -1)

    meta = meta_tokens.astype(F32)
    head0 = jnp.concatenate([x_sample.reshape(N, D_MODEL), jnp.zeros((SEQ0 - N, D_MODEL), F32), meta], axis=0)
    head_rest = jnp.concatenate([jnp.zeros((SEQ0, D_MODEL), F32), meta], axis=0)
    xh = jnp.concatenate([head0] + [head_rest] * (B - 1), axis=0)
    xm = x_prompt.reshape(B * T, D_MODEL)

    rope_m = _rope_tables(N_META + jnp.arange(T))
    hr_ = jnp.arange(HEAD_ROWS)
    pos_h0 = jnp.where(hr_ < N, PAST_LEN, jnp.maximum(hr_ - SEQ0, 0))
    pos_h = jnp.concatenate([pos_h0] + [jnp.maximum(hr_ - SEQ0, 0)] * (B - 1))
    rope_h = _rope_tables(pos_h)

    wi_all, wo_all = w_in.astype(BF16), w_out.astype(BF16)
    w1_all, w2_all = w_ff1.astype(BF16), w_ff2.astype(BF16)
    wg_all, wp_all = w_glu.astype(BF16), w_pool.astype(BF16)
    ck = cache_k.astype(F32)
    cv = cache_v.astype(F32)
    RH = B * HEAD_ROWS

    outs_p = [[] for _ in range(5)]
    outs_s = [[] for _ in range(5)]
    for l in range(depth):
        pw, bblk, cblk = _ssm_params(A_re[l], A_im[l], log_dt[l], B_re[l], B_im[l], C_re[l], C_im[l])
        ssm_w = (pw, bblk, cblk, row(D_skip[l]), wg_all, l, row(b_glu[l]), row(g_out_ssm[l]))
        pool_w = (wp_all, l, row(pool_scale[l]), row(g_out_pool[l]))
        in_w = (row(g_mix[l]), wi_all, l, row(g_q[l]), row(g_k[l]))

        qm, km, vm, um, pm = _inproj(xm, *in_w, rope_m, TM_PROJ)
        qh, kh, vh, uh, ph = _inproj(xh, *in_w, rope_h, RH)

        am, ah = _attn(sinks[l].astype(F32), qm, qh, km, kh, vm, vh, row(g_out_attn[l]), B)
        sm, sh, st = _ssm(um, uh, *ssm_w, B, TM_SEQ)
        plm, plh = _pool(pm, ph, *pool_w, B, TM_SEQ)

        ah, nk, nv = _attn_sample(qh, kh, vh, ck, cv, l, sinks[l], row(g_out_attn[l]), ah, N)
        h0 = jnp.concatenate([_state_to_lanes(state_ssm_re[l]), _state_to_lanes(state_ssm_im[l])],
                             axis=-1).reshape(N, SSM_STATE_LANES)
        pbuf = state_pool[l].astype(F32).transpose(1, 0, 2)
        sh, st_s, plh = _mix_sample(uh, h0, *ssm_w, ph, pbuf, *pool_w, sh, plh)

        xm = _outproj(xm, am, sm, plm, wo_all, l, TM_PROJ)
        xh = _outproj(xh, ah, sh, plh, wo_all, l, RH)
        xm = _ffn(xm, row(g_ffn[l]), w1_all, w2_all, l, TM_FFN, TF_FFN)
        xh = _ffn(xh, row(g_ffn[l]), w1_all, w2_all, l, RH, TF_FFN)

        b3 = lambda t: t.reshape(B, T, t.shape[-1])
        hr, hi = _state_from_lanes(st[:, 0])
        outs_p[0].append(b3(km)[:, T - WINDOW:].reshape(B, WINDOW, N_KV_HEADS, HEAD_DIM))
        outs_p[1].append(b3(vm)[:, T - WINDOW:].reshape(B, WINDOW, N_KV_HEADS, HEAD_DIM))
        outs_p[2].append(hr)
        outs_p[3].append(hi)
        outs_p[4].append(b3(pm)[:, T - POOL_BUF:])
        hr_s, hi_s = _state_from_lanes(st_s)
        outs_s[0].append(nk)
        outs_s[1].append(nv)
        outs_s[2].append(hr_s)
        outs_s[3].append(hi_s)
        outs_s[4].append(jnp.concatenate([state_pool[l].astype(F32)[:, 1:], ph[:N, None]], axis=1))

    y_prompt = xm.reshape(B, T, D_MODEL)
    y_sample = xh[:N].reshape(N, 1, D_MODEL)
    np_ = [jnp.stack(t) for t in outs_p]
    ns_ = [jnp.stack(t) for t in outs_s]
    return (y_prompt, y_sample, *np_, *ns_)
```

```python
import functools
import math

import jax
import jax.numpy as jnp
from jax.experimental import pallas as pl
from jax.experimental.pallas import tpu as pltpu

D_MODEL = 2048
N_META = 16
HEAD_DIM = 128
N_HEADS = 8
N_KV_HEADS = 2
GQA_GROUP = 4
ATTN_WIDTH = 1024
KV_WIDTH = 256
WINDOW = 128
BLOCK = 128
ROT_DIM = 32
ROPE_THETA = 500000.0
SSM_WIDTH = 512
SSM_GROUP_SIZE = 16
SSM_GROUPS = 32
SSM_STATE = 64
POOL_WIDTH = 512
POOL_WINDOWS = (2, 4, 8, 16)
POOL_GROUP = 128
POOL_BUF = 15
POOL_HALO = 16
IN_WIDTH = 2560
D_FF = 8192
EPS = 1e-6
PAST_LEN = 16384
LOG2E = math.log2(math.e)

HEAD_ROWS = BLOCK
SEQ0 = HEAD_ROWS - N_META
LANES = 128
SUBLANES = 8
SSM_LANE_BLOCKS = SSM_WIDTH // LANES
SSM_BLOCK_STATES = (LANES // SSM_GROUP_SIZE) * SSM_STATE
SSM_STATE_LANES = SSM_LANE_BLOCKS * 2 * SSM_BLOCK_STATES
VMEM_LIMIT = 56 * 1024 * 1024

TM_PROJ = 512
TM_FFN = 1024
TF_FFN = 512
TM_SEQ = 512
DEC_STEP = 8
SSM_POW_ROWS = (1, HEAD_ROWS // SUBLANES, TM_SEQ // SUBLANES)

BF16 = jnp.bfloat16
F32 = jnp.float32


def _params(*semantics):
    return pltpu.CompilerParams(dimension_semantics=semantics, vmem_limit_bytes=VMEM_LIMIT)


def _rms(x, g):
    return x * jax.lax.rsqrt(jnp.mean(x * x, axis=-1, keepdims=True) + EPS) * g


def _full(shape):
    n = len(shape)
    return pl.BlockSpec(shape, lambda *_: (0,) * n)


def _layer(shape, l):
    n = len(shape)
    return pl.BlockSpec((None, *shape), lambda *_: (l,) + (0,) * n)


def _inproj_kernel(x_ref, g_ref, w_ref, gq_ref, gk_ref, rc_ref, rs1_ref, rs2_ref,
                   q_ref, k_ref, v_ref, u_ref, xp_ref):
    h = _rms(x_ref[...], g_ref[...]).astype(BF16)
    proj = jnp.dot(h, w_ref[...], preferred_element_type=F32)
    rc, rs1, rs2 = rc_ref[...], rs1_ref[...], rs2_ref[...]

    def head(t, g):
        t = _rms(t, g)
        return t * rc + pltpu.roll(t, 16, 1) * rs1 + pltpu.roll(t, LANES - 16, 1) * rs2

    for hd in range(N_HEADS):
        sl = slice(hd * HEAD_DIM, (hd + 1) * HEAD_DIM)
        q_ref[:, sl] = head(proj[:, sl], gq_ref[...])
    for hd in range(N_KV_HEADS):
        sl = slice(hd * HEAD_DIM, (hd + 1) * HEAD_DIM)
        k_ref[:, sl] = head(proj[:, ATTN_WIDTH + hd * HEAD_DIM:ATTN_WIDTH + (hd + 1) * HEAD_DIM], gk_ref[...])
    o2 = ATTN_WIDTH + KV_WIDTH
    o3 = o2 + KV_WIDTH
    o4 = o3 + SSM_WIDTH
    v_ref[...] = proj[:, o2:o3]
    for j in range(SSM_LANE_BLOCKS):
        u_ref[j] = proj[:, o3 + j * LANES:o3 + (j + 1) * LANES]
    xp_ref[...] = proj[:, o4:]


def _inproj(x, g, w, l, gq, gk, rope, tm):
    R = x.shape[0]
    tiles_per_rope = rope[0].shape[0] // tm
    row = lambda i: (i, 0)
    rrow = lambda i: (i % tiles_per_rope, 0)
    flat = lambda w_: (pl.BlockSpec((tm, w_), row), jax.ShapeDtypeStruct((R, w_), F32))
    u_out = (pl.BlockSpec((SSM_LANE_BLOCKS, tm, LANES), lambda i: (0, i, 0)),
             jax.ShapeDtypeStruct((SSM_LANE_BLOCKS, R, LANES), F32))
    outs = [flat(ATTN_WIDTH), flat(KV_WIDTH), flat(KV_WIDTH), u_out, flat(POOL_WIDTH)]
    return pl.pallas_call(
        _inproj_kernel,
        grid=(R // tm,),
        in_specs=[pl.BlockSpec((tm, D_MODEL), row), _layer((1, D_MODEL), l), _layer((D_MODEL, IN_WIDTH), l),
                  _layer((1, HEAD_DIM), l), _layer((1, HEAD_DIM), l),
                  pl.BlockSpec((tm, LANES), rrow), pl.BlockSpec((tm, LANES), rrow),
                  pl.BlockSpec((tm, LANES), rrow)],
        out_specs=[o[0] for o in outs],
        out_shape=[o[1] for o in outs],
        compiler_params=_params("arbitrary"),
        name="inproj",
    )(x, g, w, gq, gk, *rope)


def _attn_bias():
    rows = GQA_GROUP * BLOCK
    r = jax.lax.broadcasted_iota(jnp.int32, (3, rows, 2 * BLOCK), 1) % BLOCK
    c = jax.lax.broadcasted_iota(jnp.int32, (3, rows, 2 * BLOCK), 2)
    i = jax.lax.broadcasted_iota(jnp.int32, (3, rows, 2 * BLOCK), 0)
    diff = BLOCK + r - c
    krow = (i - 1) * BLOCK + c
    mask = (diff >= 0) & (diff <= WINDOW) & (krow >= SEQ0)
    return jnp.where(mask, 0.0, -jnp.inf).astype(F32)


def _attn_kernel(sink_ref, qm_ref, qh_ref, kpm_ref, kcm_ref, kh_ref, vpm_ref, vcm_ref, vh_ref, g_ref, bias_ref,
                 om_ref, oh_ref, *, l):
    i = pl.program_id(1)
    first = i == 0
    second = i == 1
    q_blk = jnp.where(first, qh_ref[...], qm_ref[...])
    kc_blk = jnp.where(first, kh_ref[...], kcm_ref[...])
    vc_blk = jnp.where(first, vh_ref[...], vcm_ref[...])
    kp_blk = jnp.where(second, kh_ref[...], kpm_ref[...])
    vp_blk = jnp.where(second, vh_ref[...], vpm_ref[...])
    rows = GQA_GROUP * BLOCK
    bias = bias_ref[...]
    rgrp = jax.lax.broadcasted_iota(jnp.int32, (rows, 1), 0) // BLOCK
    outs = []
    for kh in range(N_KV_HEADS):
        ksl = slice(kh * HEAD_DIM, (kh + 1) * HEAD_DIM)
        qh = jnp.concatenate(
            [q_blk[:, (kh * GQA_GROUP + g) * HEAD_DIM:(kh * GQA_GROUP + g + 1) * HEAD_DIM]
             for g in range(GQA_GROUP)], axis=0).astype(BF16)
        kk = jnp.concatenate([kp_blk[:, ksl], kc_blk[:, ksl]], axis=0).astype(BF16)
        vv = jnp.concatenate([vp_blk[:, ksl], vc_blk[:, ksl]], axis=0).astype(BF16)
        s = jax.lax.dot_general(qh, kk, (((1,), (1,)), ((), ())),
                                preferred_element_type=F32) * (HEAD_DIM ** -0.5 * LOG2E) + bias
        sk = jnp.zeros((rows, 1), F32)
        for g in range(GQA_GROUP):
            sk = jnp.where(rgrp == g, sink_ref[l * N_HEADS + kh * GQA_GROUP + g] * LOG2E, sk)
        m = jnp.maximum(jnp.max(s, axis=-1, keepdims=True), sk)
        p = jnp.exp2(s - m)
        denom = jnp.sum(p, axis=-1, keepdims=True) + jnp.exp2(sk - m)
        o = jnp.dot(p.astype(BF16), vv, preferred_element_type=F32) / denom
        outs.extend(o[g * BLOCK:(g + 1) * BLOCK] for g in range(GQA_GROUP))
    a = _rms(jnp.concatenate(outs, axis=1), g_ref[...]).astype(om_ref.dtype)

    @pl.when(first)
    def _():
        oh_ref[...] = a

    @pl.when(jnp.logical_not(first))
    def _():
        om_ref[...] = a


def _attn(sinks, l, qm, qh, km, kh, vm, vh, g, bias, B):
    nb = qm.shape[0] // (B * BLOCK)
    cur = lambda b, i, s: (b * nb + jnp.maximum(i - 1, 0), 0)
    prev = lambda b, i, s: (b * nb + jnp.maximum(i - 2, 0), 0)
    head = lambda b, i, s: (b, 0)
    spec = lambda w, im: pl.BlockSpec((BLOCK, w), im)
    return pl.pallas_call(
        functools.partial(_attn_kernel, l=l),
        grid_spec=pltpu.PrefetchScalarGridSpec(
            num_scalar_prefetch=1,
            grid=(B, nb + 1),
            in_specs=[spec(ATTN_WIDTH, cur), spec(ATTN_WIDTH, head),
                      spec(KV_WIDTH, prev), spec(KV_WIDTH, cur), spec(KV_WIDTH, head),
                      spec(KV_WIDTH, prev), spec(KV_WIDTH, cur), spec(KV_WIDTH, head),
                      pl.BlockSpec((None, 1, ATTN_WIDTH), lambda b, i, s: (l, 0, 0)),
                      pl.BlockSpec((None,) + bias.shape[1:], lambda b, i, s: (jnp.minimum(i, 2), 0, 0))],
            out_specs=[spec(ATTN_WIDTH, cur), spec(ATTN_WIDTH, head)],
        ),
        out_shape=[jax.ShapeDtypeStruct(qm.shape, BF16), jax.ShapeDtypeStruct(qh.shape, BF16)],
        compiler_params=_params("arbitrary", "arbitrary"),
        name="attn_prompt",
    )(sinks, qm, qh, km, km, kh, vm, vm, vh, g, bias)


def _ssm_params_kernel(ar_ref, ai_ref, ldt_ref, kk_ref, br_ref, bi_ref, tr_ref, ti_ref, bbr_ref, bbi_ref):
    ar, ai = ar_ref[...], ai_ref[...]
    dt = jnp.exp(ldt_ref[...])
    kk = kk_ref[...]
    mag = jnp.exp(dt * ar * kk)
    ang = dt * ai * kk
    tr = mag * jnp.cos(ang)
    ti = mag * jnp.sin(ang)
    tr_ref[...] = tr
    ti_ref[...] = ti
    abr, abi = tr[:, 0:1], ti[:, 0:1]
    den = ar * ar + ai * ai
    fr = ((abr - 1.0) * ar + abi * ai) / den
    fi = (abi * ar - (abr - 1.0) * ai) / den
    br, bi = br_ref[...], bi_ref[...]
    bbr_ref[...] = fr * br - fi * bi
    bbi_ref[...] = fr * bi + fi * br


def _ssm_params(A_re, A_im, log_dt, B_re, B_im, C_re, C_im):
    depth = A_re.shape[0]
    n = depth * SSM_GROUPS * SSM_STATE
    col = lambda t: t.astype(F32).reshape(n, 1)
    ldt = jnp.broadcast_to(log_dt.astype(F32)[:, :, None], (depth, SSM_GROUPS, SSM_STATE)).reshape(n, 1)
    kk = jnp.array([SSM_POW_ROWS + (0,) * (SUBLANES - len(SSM_POW_ROWS))], F32)
    shapes = [(n, SUBLANES), (n, SUBLANES), (n, SSM_GROUP_SIZE), (n, SSM_GROUP_SIZE)]
    tr, ti, bbr, bbi = pl.pallas_call(
        _ssm_params_kernel,
        out_shape=[jax.ShapeDtypeStruct(s, F32) for s in shapes],
        name="ssm_params",
    )(col(A_re), col(A_im), ldt, kk, B_re.astype(F32).reshape(n, SSM_GROUP_SIZE),
      B_im.astype(F32).reshape(n, SSM_GROUP_SIZE))
    J, G8 = SSM_LANE_BLOCKS, LANES // SSM_GROUP_SIZE

    def lanes(t):
        return t.reshape(depth, J, SSM_BLOCK_STATES, SUBLANES).transpose(0, 3, 1, 2)

    pw = jnp.concatenate([lanes(tr), lanes(ti)], axis=-1).reshape(depth, SUBLANES, SSM_STATE_LANES)
    eye = jnp.eye(G8, dtype=F32)

    def bdiag(t):
        t = t.reshape(depth, J, G8, SSM_STATE, SSM_GROUP_SIZE).transpose(0, 1, 2, 4, 3)
        t = t[:, :, :, :, None, :] * eye[None, None, :, None, :, None]
        return t.reshape(depth, J, LANES, SSM_BLOCK_STATES)

    bblk = jnp.concatenate([bdiag(bbr), bdiag(bbi)], axis=-1).astype(BF16)

    def cdiag(t):
        t = t.astype(F32).reshape(depth, J, G8, SSM_GROUP_SIZE, SSM_STATE).transpose(0, 1, 2, 4, 3)
        t = t[:, :, :, :, None, :] * eye[None, None, :, None, :, None]
        return t.reshape(depth, J, SSM_BLOCK_STATES, LANES)

    cblk = jnp.concatenate([cdiag(C_re), -cdiag(C_im)], axis=2).astype(BF16)
    return pw, bblk, cblk


def _ssm_tail(y, u, d_ref, wg_ref, bg_ref, g_ref):
    y = y + d_ref[...] * u
    z = jax.nn.gelu(y)
    gate = jax.nn.sigmoid(jnp.dot(z.astype(BF16), wg_ref[...], preferred_element_type=F32) + bg_ref[...])
    return _rms(z * gate, g_ref[...])


def _ssm_sweep(x_scr, n, a_tabs, init, store):
    S = SSM_BLOCK_STATES
    fins = []
    for j0 in range(0, SSM_LANE_BLOCKS, 2):
        js = (j0, j0 + 1)

        def body(k, carry, js=js):
            r0 = pl.multiple_of(k * SUBLANES, SUBLANES)
            out = []
            for idx, j in enumerate(js):
                hr, hi = carry[2 * idx], carry[2 * idx + 1]
                base = j * 2 * S
                ar, ai = a_tabs[j]
                nhr = ar * hr - ai * hi + x_scr[pl.ds(r0, SUBLANES), base:base + S]
                nhi = ar * hi + ai * hr + x_scr[pl.ds(r0, SUBLANES), base + S:base + 2 * S]
                if store:
                    x_scr[pl.ds(r0, SUBLANES), base:base + S] = nhr
                    x_scr[pl.ds(r0, SUBLANES), base + S:base + 2 * S] = nhi
                out += [nhr, nhi]
            return tuple(out)

        c0 = tuple(t for j in js for t in init[j])
        res = jax.lax.fori_loop(0, n // SUBLANES, body, c0, unroll=2)
        fins += [(res[0], res[1]), (res[2], res[3])]
    return fins


def _ssm_rows(u, pow_row, pw_ref, bblk_ref, cblk_ref, x_scr, s_scr, carry_scr):
    n = u.shape[0]
    S = SSM_BLOCK_STATES
    ub = u.astype(BF16)
    for j in range(SSM_LANE_BLOCKS):
        x_scr[0:n, j * 2 * S:(j + 1) * 2 * S] = jnp.dot(ub[:, j * LANES:(j + 1) * LANES], bblk_ref[j],
                                                       preferred_element_type=F32)
    bc = lambda t: jnp.broadcast_to(t, (SUBLANES, S))
    re = lambda ref, r0, r1, j: ref[r0:r1, j * 2 * S:j * 2 * S + S]
    im = lambda ref, r0, r1, j: ref[r0:r1, j * 2 * S + S:(j + 1) * 2 * S]
    a_tabs = [(bc(re(pw_ref, 0, 1, j)), bc(im(pw_ref, 0, 1, j))) for j in range(SSM_LANE_BLOCKS)]
    zero = jnp.zeros((SUBLANES, S), F32)
    fins = _ssm_sweep(x_scr, n, a_tabs, [(zero, zero)] * SSM_LANE_BLOCKS, store=False)
    for j in range(SSM_LANE_BLOCKS):
        base = j * 2 * S
        cr, ci = re(pw_ref, pow_row, pow_row + 1, j), im(pw_ref, pow_row, pow_row + 1, j)
        sr, si = re(carry_scr, 0, 1, j), im(carry_scr, 0, 1, j)
        fr, fi = fins[j]
        for c in range(SUBLANES):
            s_scr[c:c + 1, base:base + S] = sr
            s_scr[c:c + 1, base + S:base + 2 * S] = si
            sr, si = cr * sr - ci * si + fr[c:c + 1], cr * si + ci * sr + fi[c:c + 1]
        carry_scr[:, base:base + S] = bc(sr)
        carry_scr[:, base + S:base + 2 * S] = bc(si)
    init = [(re(s_scr, 0, SUBLANES, j), im(s_scr, 0, SUBLANES, j)) for j in range(SSM_LANE_BLOCKS)]
    _ssm_sweep(x_scr, n, a_tabs, init, store=True)
    ys = [jnp.dot(x_scr[0:n, j * 2 * S:(j + 1) * 2 * S].astype(BF16), cblk_ref[j], preferred_element_type=F32)
          for j in range(SSM_LANE_BLOCKS)]
    return jnp.concatenate(ys, axis=1)


def _ssm_tile(u_ref, n, pow_row, seq_start, refs, o_ref, scr):
    pw_ref, bblk_ref, cblk_ref, d_ref, wg_ref, bg_ref, g_ref = refs
    up_scr, x_scr, s_scr, carry_scr, o_scr = scr
    q = n // SUBLANES
    for j in range(SSM_LANE_BLOCKS):
        for k in range(q):
            up_scr[k * SUBLANES:(k + 1) * SUBLANES, j * LANES:(j + 1) * LANES] = \
                u_ref[j, pl.ds(k, SUBLANES, stride=q), :]
    u = up_scr[0:n, :]
    if seq_start:
        p = jax.lax.broadcasted_iota(jnp.int32, (n, 1), 0)
        u = jnp.where((p % SUBLANES) * q + p // SUBLANES >= seq_start, u, 0.0)
    y = _ssm_rows(u, pow_row, pw_ref, bblk_ref, cblk_ref, x_scr, s_scr, carry_scr)
    out = _ssm_tail(y, u, d_ref, wg_ref, bg_ref, g_ref)
    for j in range(SSM_LANE_BLOCKS):
        for k in range(q):
            o_scr[j, pl.ds(k, SUBLANES, stride=q), :] = out[k * SUBLANES:(k + 1) * SUBLANES,
                                                            j * LANES:(j + 1) * LANES]
    o_ref[...] = jnp.concatenate([o_scr[j, 0:n, :] for j in range(SSM_LANE_BLOCKS)], axis=1).astype(o_ref.dtype)


def _ssm_kernel(um_ref, uh_ref, pw_ref, bblk_ref, cblk_ref, d_ref, wg_ref, bg_ref, g_ref,
                om_ref, oh_ref, st_ref, up_scr, x_scr, s_scr, carry_scr, o_scr):
    t = pl.program_id(1)
    refs = (pw_ref, bblk_ref, cblk_ref, d_ref, wg_ref, bg_ref, g_ref)
    scr = (up_scr, x_scr, s_scr, carry_scr, o_scr)

    @pl.when(t == 0)
    def _():
        carry_scr[...] = jnp.zeros_like(carry_scr)
        _ssm_tile(uh_ref, HEAD_ROWS, 1, SEQ0, refs, oh_ref, scr)

    @pl.when(t > 0)
    def _():
        _ssm_tile(um_ref, um_ref.shape[1], 2, 0, refs, om_ref, scr)
        st_ref[...] = carry_scr[...]


def _ssm(um, uh, pw, bblk, cblk, d, wg, l, bg, g, B, tm):
    assert (1, HEAD_ROWS // SUBLANES, tm // SUBLANES) == SSM_POW_ROWS
    J = SSM_LANE_BLOCKS
    rm, rh = um.shape[1], uh.shape[1]
    nt = rm // (B * tm)
    main = lambda b, t: (b * nt + jnp.maximum(t - 1, 0), 0)
    head = lambda b, t: (b, 0)
    return pl.pallas_call(
        _ssm_kernel,
        grid=(B, nt + 1),
        in_specs=[pl.BlockSpec((J, tm, LANES), lambda b, t: (0, b * nt + jnp.maximum(t - 1, 0), 0)),
                  pl.BlockSpec((J, HEAD_ROWS, LANES), lambda b, t: (0, b, 0)),
                  _layer(pw.shape[1:], l), _layer(bblk.shape[1:], l), _layer(cblk.shape[1:], l),
                  _layer((1, SSM_WIDTH), l), _layer((SSM_WIDTH, SSM_WIDTH), l), _layer((1, SSM_WIDTH), l),
                  _layer((1, SSM_WIDTH), l)],
        out_specs=[pl.BlockSpec((tm, SSM_WIDTH), main), pl.BlockSpec((HEAD_ROWS, SSM_WIDTH), head),
                   pl.BlockSpec((None, SUBLANES, SSM_STATE_LANES), lambda b, t: (b, 0, 0))],
        out_shape=[jax.ShapeDtypeStruct((rm, SSM_WIDTH), BF16), jax.ShapeDtypeStruct((rh, SSM_WIDTH), BF16),
                   jax.ShapeDtypeStruct((B, SUBLANES, SSM_STATE_LANES), F32)],
        scratch_shapes=[pltpu.VMEM((tm, SSM_WIDTH), F32),
                        pltpu.VMEM((tm, SSM_STATE_LANES), F32),
                        pltpu.VMEM((SUBLANES, SSM_STATE_LANES), F32),
                        pltpu.VMEM((SUBLANES, SSM_STATE_LANES), F32),
                        pltpu.VMEM((J, tm, LANES), F32)],
        compiler_params=_params("arbitrary", "arbitrary"),
        name="ssm_prompt",
    )(um, uh, pw, bblk, cblk, d, wg, bg, g)


def _state_from_lanes(s):
    s = s.reshape(s.shape[0], SSM_LANE_BLOCKS, 2, SSM_BLOCK_STATES)
    return (s[:, :, 0].reshape(-1, SSM_GROUPS, SSM_STATE), s[:, :, 1].reshape(-1, SSM_GROUPS, SSM_STATE))


def _pool_tail(d_groups, w_ref, sc_ref, g_ref):
    y = jnp.concatenate(
        [jnp.dot(d.astype(BF16), w_ref[gi], preferred_element_type=F32) for gi, d in enumerate(d_groups)], axis=1)
    return _rms(y * sc_ref[...], g_ref[...])


def _pool_rows(x, prev, pos0, w_ref, sc_ref, g_ref):
    n = x.shape[0]
    xe = jnp.concatenate([prev, x], axis=0)
    pos = pos0 + jax.lax.broadcasted_iota(jnp.int32, (n, 1), 0)
    ds = []
    for gi, w in enumerate(POOL_WINDOWS):
        gsl = slice(gi * POOL_GROUP, (gi + 1) * POOL_GROUP)
        s = xe[:, gsl]
        k = 1
        while k < w:
            s = s + pltpu.roll(s, k, 0)
            k *= 2
        cnt = jnp.clip(pos + 1, 1, w).astype(F32)
        ds.append(s[POOL_HALO:] / cnt - x[:, gsl])
    return _pool_tail(ds, w_ref, sc_ref, g_ref)


def _pool_kernel(xm_ref, halo_ref, xh_ref, w_ref, sc_ref, g_ref, om_ref, oh_ref):
    t = pl.program_id(1)
    tm = xm_ref.shape[0]

    @pl.when(t == 0)
    def _():
        hrow = jax.lax.broadcasted_iota(jnp.int32, (HEAD_ROWS, 1), 0)
        x = jnp.where(hrow >= SEQ0, xh_ref[...], 0.0)
        prev = jnp.zeros((POOL_HALO, POOL_WIDTH), F32)
        oh_ref[...] = _pool_rows(x, prev, -SEQ0, w_ref, sc_ref, g_ref).astype(oh_ref.dtype)

    @pl.when(t > 0)
    def _():
        prev = jnp.where(t == 1, xh_ref[HEAD_ROWS - POOL_HALO:, :], halo_ref[...])
        om_ref[...] = _pool_rows(xm_ref[...], prev, N_META + (t - 1) * tm, w_ref, sc_ref, g_ref).astype(om_ref.dtype)


def _pool(xm, xh, w, l, sc, g, B, tm):
    nt = xm.shape[0] // (B * tm)
    r = tm // POOL_HALO
    main = lambda b, t: (b * nt + jnp.maximum(t - 1, 0), 0)
    halo = lambda b, t: (jnp.maximum((b * nt + t - 1) * r - 1, 0), 0)
    head = lambda b, t: (b, 0)
    return pl.pallas_call(
        _pool_kernel,
        grid=(B, nt + 1),
        in_specs=[pl.BlockSpec((tm, POOL_WIDTH), main), pl.BlockSpec((POOL_HALO, POOL_WIDTH), halo),
                  pl.BlockSpec((HEAD_ROWS, POOL_WIDTH), head),
                  _layer(w.shape[1:], l), _layer((1, POOL_WIDTH), l), _layer((1, POOL_WIDTH), l)],
        out_specs=[pl.BlockSpec((tm, POOL_WIDTH), main), pl.BlockSpec((HEAD_ROWS, POOL_WIDTH), head)],
        out_shape=[jax.ShapeDtypeStruct(xm.shape, BF16), jax.ShapeDtypeStruct(xh.shape, BF16)],
        compiler_params=_params("arbitrary", "arbitrary"),
        name="pool_prompt",
    )(xm, xm, xh, w, sc, g)


def _attn_sample_kernel(q_ref, kn_ref, vn_ref, kc_ref, vc_ref, sink_ref, g_ref, a_in_ref,
                        a_ref, nk_ref, nv_ref, acc_scr):
    del a_in_ref
    step = pl.program_id(0)
    scale = HEAD_DIM ** -0.5
    for bb in range(DEC_STEP):
        nk_ref[bb, 0:WINDOW - 1] = kc_ref[bb, 1:WINDOW]
        nv_ref[bb, 0:WINDOW - 1] = vc_ref[bb, 1:WINDOW]
        outs = []
        for kh in range(N_KV_HEADS):
            ksl = slice(kh * HEAD_DIM, (kh + 1) * HEAD_DIM)
            qh = jnp.concatenate(
                [q_ref[bb:bb + 1, (kh * GQA_GROUP + g) * HEAD_DIM:(kh * GQA_GROUP + g + 1) * HEAD_DIM]
                 for g in range(GQA_GROUP)], axis=0)
            kn = kn_ref[bb:bb + 1, ksl]
            vn = vn_ref[bb:bb + 1, ksl]
            nk_ref[bb, WINDOW - 1, kh:kh + 1, :] = kn
            nv_ref[bb, WINDOW - 1, kh:kh + 1, :] = vn
            kc = kc_ref[bb, :, kh, :]
            vc = vc_ref[bb, :, kh, :]
            sc = jax.lax.dot_general(qh.astype(BF16), kc.astype(BF16), (((1,), (1,)), ((), ())),
                                     preferred_element_type=F32) * scale
            sn = jnp.sum(qh.astype(BF16).astype(F32) * kn.astype(BF16).astype(F32), axis=-1, keepdims=True) * scale
            sk = sink_ref[kh]
            m = jnp.maximum(jnp.maximum(jnp.max(sc, axis=-1, keepdims=True), sn), sk)
            pc = jnp.exp(sc - m)
            pn = jnp.exp(sn - m)
            denom = jnp.sum(pc, axis=-1, keepdims=True) + pn + jnp.exp(sk - m)
            o = jnp.dot(pc.astype(BF16), vc.astype(BF16), preferred_element_type=F32)
            o = (o + pn.astype(BF16).astype(F32) * vn.astype(BF16).astype(F32)) / denom
            outs.extend(o[g:g + 1] for g in range(GQA_GROUP))
        a = jnp.concatenate(outs, axis=1)
        acc_scr[pl.ds(step * DEC_STEP + bb, 1), :] = _rms(a, g_ref[...])

    @pl.when(step == pl.num_programs(0) - 1)
    def _():
        a_ref[...] = acc_scr[...].astype(a_ref.dtype)


def _attn_sample(qh, kh, vh, cache_k, cache_v, l, sinks, g, ah, N):
    rows = lambda w: pl.BlockSpec((DEC_STEP, w), lambda s: (s, 0))
    cache = pl.BlockSpec((None, DEC_STEP, WINDOW, N_KV_HEADS, HEAD_DIM), lambda s: (l, s, 0, 0, 0))
    ncache = pl.BlockSpec((DEC_STEP, WINDOW, N_KV_HEADS, HEAD_DIM), lambda s: (s, 0, 0, 0))
    cshape = jax.ShapeDtypeStruct((N, WINDOW, N_KV_HEADS, HEAD_DIM), F32)
    return pl.pallas_call(
        _attn_sample_kernel,
        grid=(N // DEC_STEP,),
        in_specs=[rows(ATTN_WIDTH), rows(KV_WIDTH), rows(KV_WIDTH), cache, cache,
                  _layer((N_KV_HEADS, GQA_GROUP, 1), l), _layer((1, ATTN_WIDTH), l),
                  pl.BlockSpec(memory_space=pl.ANY)],
        out_specs=[pl.BlockSpec((N, ATTN_WIDTH), lambda s: (0, 0)), ncache, ncache],
        out_shape=[jax.ShapeDtypeStruct(ah.shape, ah.dtype), cshape, cshape],
        scratch_shapes=[pltpu.VMEM((N, ATTN_WIDTH), F32)],
        input_output_aliases={7: 0},
        compiler_params=_params("arbitrary"),
        name="attn_sample",
    )(qh, kh, vh, cache_k, cache_v, sinks, g, ah)


def _mix_sample_kernel(u_ref, h0_ref, pw_ref, bblk_ref, cblk_ref, d_ref, wg_ref, bg_ref, gs_ref,
                       xp_ref, pb_ref, wp_ref, sc_ref, gp_ref, s_in_ref, p_in_ref, s_ref, st_ref, p_ref):
    del s_in_ref, p_in_ref
    S = SSM_BLOCK_STATES
    u = jnp.concatenate([u_ref[j] for j in range(SSM_LANE_BLOCKS)], axis=1)
    ub = u.astype(BF16)
    ys = []
    for j in range(SSM_LANE_BLOCKS):
        x = jnp.dot(ub[:, j * LANES:(j + 1) * LANES], bblk_ref[j], preferred_element_type=F32)
        base = j * 2 * S
        ar = pw_ref[0:1, base:base + S]
        ai = pw_ref[0:1, base + S:base + 2 * S]
        h0r = h0_ref[:, base:base + S]
        h0i = h0_ref[:, base + S:base + 2 * S]
        hr = x[:, 0:S] + ar * h0r - ai * h0i
        hi = x[:, S:] + ar * h0i + ai * h0r
        st_ref[:, base:base + S] = hr
        st_ref[:, base + S:base + 2 * S] = hi
        h = jnp.concatenate([hr, hi], axis=1).astype(BF16)
        ys.append(jnp.dot(h, cblk_ref[j], preferred_element_type=F32))
    s_ref[...] = _ssm_tail(jnp.concatenate(ys, axis=1), u, d_ref, wg_ref, bg_ref, gs_ref).astype(s_ref.dtype)

    xp = xp_ref[...]
    ds = []
    for gi, w in enumerate(POOL_WINDOWS):
        gsl = slice(gi * POOL_GROUP, (gi + 1) * POOL_GROUP)
        s = xp[:, gsl]
        for back in range(1, w):
            s = s + pb_ref[POOL_BUF - back][:, gsl]
        ds.append(s / float(w) - xp[:, gsl])
    p_ref[...] = _pool_tail(ds, wp_ref, sc_ref, gp_ref).astype(p_ref.dtype)


def _mix_sample(uh, h0, pw, bblk, cblk, d, wg, l, bg, gs, xph, pbuf, wp, _l, sc, gp, sh, ph):
    N = h0.shape[1]
    rows = lambda w: pl.BlockSpec((N, w), lambda i: (0, 0))
    anyspec = pl.BlockSpec(memory_space=pl.ANY)
    return pl.pallas_call(
        _mix_sample_kernel,
        grid=(1,),
        in_specs=[pl.BlockSpec((SSM_LANE_BLOCKS, N, LANES), lambda i: (0, 0, 0)),
                  _layer(h0.shape[1:], l), _layer(pw.shape[1:], l), _layer(bblk.shape[1:], l),
                  _layer(cblk.shape[1:], l),
                  _layer((1, SSM_WIDTH), l), _layer((SSM_WIDTH, SSM_WIDTH), l), _layer((1, SSM_WIDTH), l),
                  _layer((1, SSM_WIDTH), l), rows(POOL_WIDTH), _layer(pbuf.shape[1:], l),
                  _layer(wp.shape[1:], l),
                  _layer((1, POOL_WIDTH), l), _layer((1, POOL_WIDTH), l), anyspec, anyspec],
        out_specs=[rows(SSM_WIDTH), _full((N, SSM_STATE_LANES)), rows(POOL_WIDTH)],
        out_shape=[jax.ShapeDtypeStruct(sh.shape, sh.dtype), jax.ShapeDtypeStruct((N, SSM_STATE_LANES), F32),
                   jax.ShapeDtypeStruct(ph.shape, ph.dtype)],
        input_output_aliases={14: 0, 15: 2},
        compiler_params=_params("arbitrary"),
        name="mix_sample",
    )(uh, h0, pw, bblk, cblk, d, wg, bg, gs, xph, pbuf, wp, sc, gp, sh, ph)


def _outproj_kernel(x_ref, a_ref, s_ref, p_ref, w_ref, o_ref):
    o1 = ATTN_WIDTH
    o2 = o1 + SSM_WIDTH
    acc = x_ref[...]
    acc = acc + jnp.dot(a_ref[...], w_ref[0:o1, :], preferred_element_type=F32)
    acc = acc + jnp.dot(s_ref[...], w_ref[o1:o2, :], preferred_element_type=F32)
    acc = acc + jnp.dot(p_ref[...], w_ref[o2:, :], preferred_element_type=F32)
    o_ref[...] = acc


def _outproj(x, a, s, p, w, l, tm):
    R = x.shape[0]
    row = lambda i: (i, 0)
    return pl.pallas_call(
        _outproj_kernel,
        grid=(R // tm,),
        in_specs=[pl.BlockSpec((tm, D_MODEL), row), pl.BlockSpec((tm, ATTN_WIDTH), row),
                  pl.BlockSpec((tm, SSM_WIDTH), row), pl.BlockSpec((tm, POOL_WIDTH), row),
                  _layer((D_MODEL, D_MODEL), l)],
        out_specs=pl.BlockSpec((tm, D_MODEL), row),
        out_shape=jax.ShapeDtypeStruct((R, D_MODEL), F32),
        compiler_params=_params("arbitrary"),
        name="outproj",
    )(x, a, s, p, w)


def _ffn_kernel(x_ref, g_ref, w1_ref, w2_ref, o_ref, h_scr):
    f = pl.program_id(1)

    @pl.when(f == 0)
    def _():
        x = x_ref[...]
        h_scr[...] = _rms(x, g_ref[...]).astype(BF16)
        o_ref[...] = x

    h1 = jnp.dot(h_scr[...], w1_ref[...], preferred_element_type=F32)
    h1 = jnp.square(jnp.maximum(h1, 0.0)).astype(BF16)
    o_ref[...] += jnp.dot(h1, w2_ref[...], preferred_element_type=F32)


def _ffn(x, g, w1, w2, l, tm, tf):
    R = x.shape[0]
    return pl.pallas_call(
        _ffn_kernel,
        grid=(R // tm, D_FF // tf),
        in_specs=[pl.BlockSpec((tm, D_MODEL), lambda i, f: (i, 0)), _layer((1, D_MODEL), l),
                  pl.BlockSpec((None, D_MODEL, tf), lambda i, f: (l, 0, f)),
                  pl.BlockSpec((None, tf, D_MODEL), lambda i, f: (l, f, 0))],
        out_specs=pl.BlockSpec((tm, D_MODEL), lambda i, f: (i, 0)),
        out_shape=jax.ShapeDtypeStruct((R, D_MODEL), F32),
        scratch_shapes=[pltpu.VMEM((tm, D_MODEL), BF16)],
        compiler_params=_params("arbitrary", "arbitrary"),
        name="ffn",
    )(x, g, w1, w2)


def _rope_tables(pos):
    half = ROT_DIM // 2
    inv = ROPE_THETA ** (-jnp.arange(0, ROT_DIM, 2, dtype=F32) / ROT_DIM)
    ang = pos.astype(F32)[:, None] * inv
    cos, sin = jnp.cos(ang), jnp.sin(ang)
    n = pos.shape[0]
    z = jnp.zeros((n, HEAD_DIM - ROT_DIM), F32)
    zh = jnp.zeros((n, half), F32)
    rc = jnp.concatenate([cos, cos, z + 1.0], axis=1)
    rs1 = jnp.concatenate([zh, sin, z], axis=1)
    rs2 = jnp.concatenate([-sin, zh, z], axis=1)
    return rc, rs1, rs2


def kernel(x_prompt, x_sample, cache_k, cache_v, state_ssm_re, state_ssm_im, state_pool, meta_tokens, g_mix, w_in, g_q, g_k, sinks, A_re, A_im, log_dt, B_re, B_im, C_re, C_im, D_skip, w_glu, b_glu, w_pool, pool_scale, g_out_attn, g_out_ssm, g_out_pool, w_out, g_ffn, w_ff1, w_ff2):
    B, T, _ = x_prompt.shape
    N = x_sample.shape[0]
    depth = w_in.shape[0]
    assert N <= SEQ0 and N % DEC_STEP == 0 and T % TM_FFN == 0
    meta = meta_tokens.astype(F32)
    head0 = jnp.concatenate([x_sample.reshape(N, D_MODEL), jnp.zeros((SEQ0 - N, D_MODEL), F32), meta], axis=0)
    head_rest = jnp.concatenate([jnp.zeros((SEQ0, D_MODEL), F32), meta], axis=0)
    xh = jnp.concatenate([head0] + [head_rest] * (B - 1), axis=0)
    xm = x_prompt.reshape(B * T, D_MODEL)

    rope_m = _rope_tables(N_META + jnp.arange(T))
    hr_ = jnp.arange(HEAD_ROWS)
    pos_h0 = jnp.where(hr_ < N, PAST_LEN, jnp.maximum(hr_ - SEQ0, 0))
    pos_h = jnp.concatenate([pos_h0] + [jnp.maximum(hr_ - SEQ0, 0)] * (B - 1))
    rope_h = _rope_tables(pos_h)

    wi_all, wo_all = w_in.astype(BF16), w_out.astype(BF16)
    w1_all, w2_all = w_ff1.astype(BF16), w_ff2.astype(BF16)
    wg_all, wp_all = w_glu.astype(BF16), w_pool.astype(BF16)
    ck = cache_k.astype(F32)
    cv = cache_v.astype(F32)
    RH = B * HEAD_ROWS

    vec = lambda t: t.astype(F32).reshape(depth, 1, -1)
    pw, bblk, cblk = _ssm_params(A_re, A_im, log_dt, B_re, B_im, C_re, C_im)
    ssm_w = lambda l: (pw, bblk, cblk, vec(D_skip), wg_all, l, vec(b_glu), vec(g_out_ssm))
    pool_w = lambda l: (wp_all, l, vec(pool_scale), vec(g_out_pool))
    in_w = lambda l: (vec(g_mix), wi_all, l, vec(g_q), vec(g_k))
    g_attn = vec(g_out_attn)
    g_ffn_ = vec(g_ffn)
    sinks_flat = sinks.astype(F32).reshape(depth * N_HEADS)
    sinks_col = sinks.astype(F32).reshape(depth, N_KV_HEADS, GQA_GROUP, 1)
    bias = _attn_bias()
    h0 = jnp.concatenate([state_ssm_re.astype(F32).reshape(depth, N, SSM_LANE_BLOCKS, SSM_BLOCK_STATES),
                          state_ssm_im.astype(F32).reshape(depth, N, SSM_LANE_BLOCKS, SSM_BLOCK_STATES)],
                         axis=-1).reshape(depth, N, SSM_STATE_LANES)
    pbuf = state_pool.astype(F32).transpose(0, 2, 1, 3)

    ks, vs, pls, sts, nks, nvs, st_ss, phs = ([] for _ in range(8))
    for l in range(depth):
        qm, km, vm, um, pm = _inproj(xm, *in_w(l), rope_m, TM_PROJ)
        qh, kh, vh, uh, ph = _inproj(xh, *in_w(l), rope_h, RH)

        am, ah = _attn(sinks_flat, l, qm, qh, km, kh, vm, vh, g_attn, bias, B)
        sm, sh, st = _ssm(um, uh, *ssm_w(l), B, TM_SEQ)
        plm, plh = _pool(pm, ph, *pool_w(l), B, TM_SEQ)

        ah, nk, nv = _attn_sample(qh, kh, vh, ck, cv, l, sinks_col, g_attn, ah, N)
        sh, st_s, plh = _mix_sample(uh, h0, *ssm_w(l), ph, pbuf, *pool_w(l), sh, plh)

        xm = _outproj(xm, am, sm, plm, wo_all, l, TM_PROJ)
        xh = _outproj(xh, ah, sh, plh, wo_all, l, RH)
        xm = _ffn(xm, g_ffn_, w1_all, w2_all, l, TM_FFN, TF_FFN)
        xh = _ffn(xh, g_ffn_, w1_all, w2_all, l, RH, TF_FFN)

        ks.append(km.reshape(B, T, KV_WIDTH)[:, T - WINDOW:])
        vs.append(vm.reshape(B, T, KV_WIDTH)[:, T - WINDOW:])
        pls.append(pm.reshape(B, T, POOL_WIDTH)[:, T - POOL_BUF:])
        sts.append(st[:, 0])
        nks.append(nk)
        nvs.append(nv)
        st_ss.append(st_s)
        phs.append(ph[:N])

    y_prompt = xm.reshape(B, T, D_MODEL)
    y_sample = xh[:N].reshape(N, 1, D_MODEL)
    heads = lambda t: jnp.stack(t).reshape(depth, -1, WINDOW, N_KV_HEADS, HEAD_DIM)
    p_re, p_im = _state_from_lanes(jnp.stack(sts).reshape(depth * B, SSM_STATE_LANES))
    s_re, s_im = _state_from_lanes(jnp.stack(st_ss).reshape(depth * N, SSM_STATE_LANES))
    st4 = lambda t, n: t.reshape(depth, n, SSM_GROUPS, SSM_STATE)
    s_pool = jnp.concatenate([state_pool.astype(F32)[:, :, 1:], jnp.stack(phs)[:, :, None]], axis=2)
    return (y_prompt, y_sample, heads(ks), heads(vs), st4(p_re, B), st4(p_im, B), jnp.stack(pls),
            jnp.stack(nks), jnp.stack(nvs), st4(s_re, N), st4(s_im, N), s_pool)
```

```python
import functools
import math

import jax
import jax.numpy as jnp
from jax.experimental import pallas as pl
from jax.experimental.pallas import tpu as pltpu

D_MODEL = 2048
N_META = 16
HEAD_DIM = 128
N_HEADS = 8
N_KV_HEADS = 2
GQA_GROUP = 4
ATTN_WIDTH = 1024
KV_WIDTH = 256
WINDOW = 128
BLOCK = 128
ROT_DIM = 32
ROPE_THETA = 500000.0
SSM_WIDTH = 512
SSM_GROUP_SIZE = 16
SSM_GROUPS = 32
SSM_STATE = 64
POOL_WIDTH = 512
POOL_WINDOWS = (2, 4, 8, 16)
POOL_GROUP = 128
POOL_BUF = 15
POOL_HALO = 16
IN_WIDTH = 2560
D_FF = 8192
EPS = 1e-6
PAST_LEN = 16384
LOG2E = math.log2(math.e)

HEAD_ROWS = BLOCK
SEQ0 = HEAD_ROWS - N_META
LANES = 128
SUBLANES = 8
SSM_LANE_BLOCKS = SSM_WIDTH // LANES
SSM_BLOCK_STATES = (LANES // SSM_GROUP_SIZE) * SSM_STATE
SSM_STATE_LANES = SSM_LANE_BLOCKS * 2 * SSM_BLOCK_STATES
VMEM_LIMIT = 56 * 1024 * 1024

TM_PROJ = 512
TM_FFN = 1024
TF_FFN = 512
TM_SEQ = 512
DEC_STEP = 8
SSM_POW_ROWS = (1, HEAD_ROWS // SUBLANES, TM_SEQ // SUBLANES)

BF16 = jnp.bfloat16
F32 = jnp.float32


def _params(*semantics):
    return pltpu.CompilerParams(dimension_semantics=semantics, vmem_limit_bytes=VMEM_LIMIT)


def _rms(x, g):
    return x * jax.lax.rsqrt(jnp.mean(x * x, axis=-1, keepdims=True) + EPS) * g


def _full(shape):
    n = len(shape)
    return pl.BlockSpec(shape, lambda *_: (0,) * n)


def _layer(shape, l):
    n = len(shape)
    return pl.BlockSpec((None, *shape), lambda *_: (l,) + (0,) * n)


def _inproj_kernel(x_ref, g_ref, w_ref, gq_ref, gk_ref, rc_ref, rs1_ref, rs2_ref,
                   q_ref, k_ref, v_ref, u_ref, xp_ref):
    h = _rms(x_ref[...], g_ref[...]).astype(BF16)
    proj = jnp.dot(h, w_ref[...], preferred_element_type=F32)
    rc, rs1, rs2 = rc_ref[...], rs1_ref[...], rs2_ref[...]

    def head(t, g):
        t = _rms(t, g)
        return t * rc + pltpu.roll(t, 16, 1) * rs1 + pltpu.roll(t, LANES - 16, 1) * rs2

    for hd in range(N_HEADS):
        sl = slice(hd * HEAD_DIM, (hd + 1) * HEAD_DIM)
        q_ref[:, sl] = head(proj[:, sl], gq_ref[...])
    for hd in range(N_KV_HEADS):
        sl = slice(hd * HEAD_DIM, (hd + 1) * HEAD_DIM)
        k_ref[:, sl] = head(proj[:, ATTN_WIDTH + hd * HEAD_DIM:ATTN_WIDTH + (hd + 1) * HEAD_DIM], gk_ref[...])
    o2 = ATTN_WIDTH + KV_WIDTH
    o3 = o2 + KV_WIDTH
    o4 = o3 + SSM_WIDTH
    v_ref[...] = proj[:, o2:o3]
    for j in range(SSM_LANE_BLOCKS):
        u_ref[j] = proj[:, o3 + j * LANES:o3 + (j + 1) * LANES]
    xp_ref[...] = proj[:, o4:]


def _inproj(x, g, w, l, gq, gk, rope, tm):
    R = x.shape[0]
    tiles_per_rope = rope[0].shape[0] // tm
    row = lambda i: (i, 0)
    rrow = lambda i: (i % tiles_per_rope, 0)
    flat = lambda w_: (pl.BlockSpec((tm, w_), row), jax.ShapeDtypeStruct((R, w_), F32))
    u_out = (pl.BlockSpec((SSM_LANE_BLOCKS, tm, LANES), lambda i: (0, i, 0)),
             jax.ShapeDtypeStruct((SSM_LANE_BLOCKS, R, LANES), F32))
    outs = [flat(ATTN_WIDTH), flat(KV_WIDTH), flat(KV_WIDTH), u_out, flat(POOL_WIDTH)]
    return pl.pallas_call(
        _inproj_kernel,
        grid=(R // tm,),
        in_specs=[pl.BlockSpec((tm, D_MODEL), row), _layer((1, D_MODEL), l), _layer((D_MODEL, IN_WIDTH), l),
                  _layer((1, HEAD_DIM), l), _layer((1, HEAD_DIM), l),
                  pl.BlockSpec((tm, LANES), rrow), pl.BlockSpec((tm, LANES), rrow),
                  pl.BlockSpec((tm, LANES), rrow)],
        out_specs=[o[0] for o in outs],
        out_shape=[o[1] for o in outs],
        compiler_params=_params("arbitrary"),
        name="inproj",
    )(x, g, w, gq, gk, *rope)


def _attn_bias():
    rows = GQA_GROUP * BLOCK
    r = jax.lax.broadcasted_iota(jnp.int32, (3, rows, 2 * BLOCK), 1) % BLOCK
    c = jax.lax.broadcasted_iota(jnp.int32, (3, rows, 2 * BLOCK), 2)
    i = jax.lax.broadcasted_iota(jnp.int32, (3, rows, 2 * BLOCK), 0)
    diff = BLOCK + r - c
    krow = (i - 1) * BLOCK + c
    mask = (diff >= 0) & (diff <= WINDOW) & (krow >= SEQ0)
    return jnp.where(mask, 0.0, -jnp.inf).astype(F32)


def _attn_kernel(sink_ref, qm_ref, qh_ref, kpm_ref, kcm_ref, kh_ref, vpm_ref, vcm_ref, vh_ref, g_ref, bias_ref,
                 om_ref, oh_ref, *, l):
    i = pl.program_id(1)
    first = i == 0
    second = i == 1
    q_blk = jnp.where(first, qh_ref[...], qm_ref[...])
    kc_blk = jnp.where(first, kh_ref[...], kcm_ref[...])
    vc_blk = jnp.where(first, vh_ref[...], vcm_ref[...])
    kp_blk = jnp.where(second, kh_ref[...], kpm_ref[...])
    vp_blk = jnp.where(second, vh_ref[...], vpm_ref[...])
    rows = GQA_GROUP * BLOCK
    bias = bias_ref[...]
    rgrp = jax.lax.broadcasted_iota(jnp.int32, (rows, 1), 0) // BLOCK
    outs = []
    for kh in range(N_KV_HEADS):
        ksl = slice(kh * HEAD_DIM, (kh + 1) * HEAD_DIM)
        qh = jnp.concatenate(
            [q_blk[:, (kh * GQA_GROUP + g) * HEAD_DIM:(kh * GQA_GROUP + g + 1) * HEAD_DIM]
             for g in range(GQA_GROUP)], axis=0).astype(BF16)
        kk = jnp.concatenate([kp_blk[:, ksl], kc_blk[:, ksl]], axis=0).astype(BF16)
        vv = jnp.concatenate([vp_blk[:, ksl], vc_blk[:, ksl]], axis=0).astype(BF16)
        s = jax.lax.dot_general(qh, kk, (((1,), (1,)), ((), ())),
                                preferred_element_type=F32) * (HEAD_DIM ** -0.5 * LOG2E) + bias
        sk = jnp.zeros((rows, 1), F32)
        for g in range(GQA_GROUP):
            sk = jnp.where(rgrp == g, sink_ref[l * N_HEADS + kh * GQA_GROUP + g] * LOG2E, sk)
        m = jnp.maximum(jnp.max(s, axis=-1, keepdims=True), sk)
        p = jnp.exp2(s - m)
        denom = jnp.sum(p, axis=-1, keepdims=True) + jnp.exp2(sk - m)
        o = jnp.dot(p.astype(BF16), vv, preferred_element_type=F32) / denom
        outs.extend(o[g * BLOCK:(g + 1) * BLOCK] for g in range(GQA_GROUP))
    a = _rms(jnp.concatenate(outs, axis=1), g_ref[...]).astype(om_ref.dtype)

    @pl.when(first)
    def _():
        oh_ref[...] = a

    @pl.when(jnp.logical_not(first))
    def _():
        om_ref[...] = a


def _attn(sinks, l, qm, qh, km, kh, vm, vh, g, bias, B):
    nb = qm.shape[0] // (B * BLOCK)
    cur = lambda b, i, s: (b * nb + jnp.maximum(i - 1, 0), 0)
    prev = lambda b, i, s: (b * nb + jnp.maximum(i - 2, 0), 0)
    head = lambda b, i, s: (b, 0)
    spec = lambda w, im: pl.BlockSpec((BLOCK, w), im)
    return pl.pallas_call(
        functools.partial(_attn_kernel, l=l),
        grid_spec=pltpu.PrefetchScalarGridSpec(
            num_scalar_prefetch=1,
            grid=(B, nb + 1),
            in_specs=[spec(ATTN_WIDTH, cur), spec(ATTN_WIDTH, head),
                      spec(KV_WIDTH, prev), spec(KV_WIDTH, cur), spec(KV_WIDTH, head),
                      spec(KV_WIDTH, prev), spec(KV_WIDTH, cur), spec(KV_WIDTH, head),
                      pl.BlockSpec((None, 1, ATTN_WIDTH), lambda b, i, s: (l, 0, 0)),
                      pl.BlockSpec((None,) + bias.shape[1:], lambda b, i, s: (jnp.minimum(i, 2), 0, 0))],
            out_specs=[spec(ATTN_WIDTH, cur), spec(ATTN_WIDTH, head)],
        ),
        out_shape=[jax.ShapeDtypeStruct(qm.shape, BF16), jax.ShapeDtypeStruct(qh.shape, BF16)],
        compiler_params=_params("arbitrary", "arbitrary"),
        name="attn_prompt",
    )(sinks, qm, qh, km, km, kh, vm, vm, vh, g, bias)


def _ssm_params_kernel(ar_ref, ai_ref, ldt_ref, kk_ref, br_ref, bi_ref, tr_ref, ti_ref, bbr_ref, bbi_ref):
    ar, ai = ar_ref[...], ai_ref[...]
    dt = jnp.exp(ldt_ref[...])
    kk = kk_ref[...]
    mag = jnp.exp(dt * ar * kk)
    ang = dt * ai * kk
    tr = mag * jnp.cos(ang)
    ti = mag * jnp.sin(ang)
    tr_ref[...] = tr
    ti_ref[...] = ti
    abr, abi = tr[:, 0:1], ti[:, 0:1]
    den = ar * ar + ai * ai
    fr = ((abr - 1.0) * ar + abi * ai) / den
    fi = (abi * ar - (abr - 1.0) * ai) / den
    br, bi = br_ref[...], bi_ref[...]
    bbr_ref[...] = fr * br - fi * bi
    bbi_ref[...] = fr * bi + fi * br


def _ssm_params(A_re, A_im, log_dt, B_re, B_im, C_re, C_im):
    depth = A_re.shape[0]
    n = depth * SSM_GROUPS * SSM_STATE
    col = lambda t: t.astype(F32).reshape(n, 1)
    ldt = jnp.broadcast_to(log_dt.astype(F32)[:, :, None], (depth, SSM_GROUPS, SSM_STATE)).reshape(n, 1)
    kk = jnp.array([SSM_POW_ROWS + (0,) * (SUBLANES - len(SSM_POW_ROWS))], F32)
    shapes = [(n, SUBLANES), (n, SUBLANES), (n, SSM_GROUP_SIZE), (n, SSM_GROUP_SIZE)]
    tr, ti, bbr, bbi = pl.pallas_call(
        _ssm_params_kernel,
        out_shape=[jax.ShapeDtypeStruct(s, F32) for s in shapes],
        name="ssm_params",
    )(col(A_re), col(A_im), ldt, kk, B_re.astype(F32).reshape(n, SSM_GROUP_SIZE),
      B_im.astype(F32).reshape(n, SSM_GROUP_SIZE))
    J, G8 = SSM_LANE_BLOCKS, LANES // SSM_GROUP_SIZE

    def lanes(t):
        return t.reshape(depth, J, SSM_BLOCK_STATES, SUBLANES).transpose(0, 3, 1, 2)

    pw = jnp.concatenate([lanes(tr), lanes(ti)], axis=-1).reshape(depth, SUBLANES, SSM_STATE_LANES)
    eye = jnp.eye(G8, dtype=F32)

    def bdiag(t):
        t = t.reshape(depth, J, G8, SSM_STATE, SSM_GROUP_SIZE).transpose(0, 1, 2, 4, 3)
        t = t[:, :, :, :, None, :] * eye[None, None, :, None, :, None]
        return t.reshape(depth, J, LANES, SSM_BLOCK_STATES)

    bblk = jnp.concatenate([bdiag(bbr), bdiag(bbi)], axis=-1).astype(BF16)

    def cdiag(t):
        t = t.astype(F32).reshape(depth, J, G8, SSM_GROUP_SIZE, SSM_STATE).transpose(0, 1, 2, 4, 3)
        t = t[:, :, :, :, None, :] * eye[None, None, :, None, :, None]
        return t.reshape(depth, J, SSM_BLOCK_STATES, LANES)

    cblk = jnp.concatenate([cdiag(C_re), -cdiag(C_im)], axis=2).astype(BF16)
    return pw, bblk, cblk


def _ssm_tail(y, u, d_ref, wg_ref, bg_ref, g_ref):
    y = y + d_ref[...] * u
    z = jax.nn.gelu(y)
    gate = jax.nn.sigmoid(jnp.dot(z.astype(BF16), wg_ref[...], preferred_element_type=F32) + bg_ref[...])
    return _rms(z * gate, g_ref[...])


def _ssm_sweep(x_scr, n, a_tabs, init, store):
    S = SSM_BLOCK_STATES
    fins = []
    for j0 in range(0, SSM_LANE_BLOCKS, 2):
        js = (j0, j0 + 1)

        def body(k, carry, js=js):
            r0 = pl.multiple_of(k * SUBLANES, SUBLANES)
            out = []
            for idx, j in enumerate(js):
                hr, hi = carry[2 * idx], carry[2 * idx + 1]
                base = j * 2 * S
                ar, ai = a_tabs[j]
                nhr = ar * hr - ai * hi + x_scr[pl.ds(r0, SUBLANES), base:base + S]
                nhi = ar * hi + ai * hr + x_scr[pl.ds(r0, SUBLANES), base + S:base + 2 * S]
                if store:
                    x_scr[pl.ds(r0, SUBLANES), base:base + S] = nhr
                    x_scr[pl.ds(r0, SUBLANES), base + S:base + 2 * S] = nhi
                out += [nhr, nhi]
            return tuple(out)

        c0 = tuple(t for j in js for t in init[j])
        res = jax.lax.fori_loop(0, n // SUBLANES, body, c0, unroll=2)
        fins += [(res[0], res[1]), (res[2], res[3])]
    return fins


def _ssm_rows(u, pow_row, pw_ref, bblk_ref, cblk_ref, x_scr, s_scr, carry_scr):
    n = u.shape[0]
    S = SSM_BLOCK_STATES
    ub = u.astype(BF16)
    for j in range(SSM_LANE_BLOCKS):
        x_scr[0:n, j * 2 * S:(j + 1) * 2 * S] = jnp.dot(ub[:, j * LANES:(j + 1) * LANES], bblk_ref[j],
                                                       preferred_element_type=F32)
    bc = lambda t: jnp.broadcast_to(t, (SUBLANES, S))
    re = lambda ref, r0, r1, j: ref[r0:r1, j * 2 * S:j * 2 * S + S]
    im = lambda ref, r0, r1, j: ref[r0:r1, j * 2 * S + S:(j + 1) * 2 * S]
    a_tabs = [(bc(re(pw_ref, 0, 1, j)), bc(im(pw_ref, 0, 1, j))) for j in range(SSM_LANE_BLOCKS)]
    zero = jnp.zeros((SUBLANES, S), F32)
    fins = _ssm_sweep(x_scr, n, a_tabs, [(zero, zero)] * SSM_LANE_BLOCKS, store=False)
    for j in range(SSM_LANE_BLOCKS):
        base = j * 2 * S
        cr, ci = re(pw_ref, pow_row, pow_row + 1, j), im(pw_ref, pow_row, pow_row + 1, j)
        sr, si = re(carry_scr, 0, 1, j), im(carry_scr, 0, 1, j)
        fr, fi = fins[j]
        for c in range(SUBLANES):
            s_scr[c:c + 1, base:base + S] = sr
            s_scr[c:c + 1, base + S:base + 2 * S] = si
            sr, si = cr * sr - ci * si + fr[c:c + 1], cr * si + ci * sr + fi[c:c + 1]
        carry_scr[:, base:base + S] = bc(sr)
        carry_scr[:, base + S:base + 2 * S] = bc(si)
    init = [(re(s_scr, 0, SUBLANES, j), im(s_scr, 0, SUBLANES, j)) for j in range(SSM_LANE_BLOCKS)]
    _ssm_sweep(x_scr, n, a_tabs, init, store=True)
    ys = [jnp.dot(x_scr[0:n, j * 2 * S:(j + 1) * 2 * S].astype(BF16), cblk_ref[j], preferred_element_type=F32)
          for j in range(SSM_LANE_BLOCKS)]
    return jnp.concatenate(ys, axis=1)


def _ssm_tile(u_ref, n, pow_row, seq_start, refs, o_ref, scr):
    pw_ref, bblk_ref, cblk_ref, d_ref, wg_ref, bg_ref, g_ref = refs
    up_scr, x_scr, s_scr, carry_scr, o_scr = scr
    q = n // SUBLANES
    for j in range(SSM_LANE_BLOCKS):
        for k in range(q):
            up_scr[k * SUBLANES:(k + 1) * SUBLANES, j * LANES:(j + 1) * LANES] = \
                u_ref[j, pl.ds(k, SUBLANES, stride=q), :]
    u = up_scr[0:n, :]
    if seq_start:
        p = jax.lax.broadcasted_iota(jnp.int32, (n, 1), 0)
        u = jnp.where((p % SUBLANES) * q + p // SUBLANES >= seq_start, u, 0.0)
    y = _ssm_rows(u, pow_row, pw_ref, bblk_ref, cblk_ref, x_scr, s_scr, carry_scr)
    out = _ssm_tail(y, u, d_ref, wg_ref, bg_ref, g_ref)
    for j in range(SSM_LANE_BLOCKS):
        for k in range(q):
            o_scr[j, pl.ds(k, SUBLANES, stride=q), :] = out[k * SUBLANES:(k + 1) * SUBLANES,
                                                            j * LANES:(j + 1) * LANES]
    o_ref[...] = jnp.concatenate([o_scr[j, 0:n, :] for j in range(SSM_LANE_BLOCKS)], axis=1).astype(o_ref.dtype)


def _ssm_kernel(um_ref, uh_ref, pw_ref, bblk_ref, cblk_ref, d_ref, wg_ref, bg_ref, g_ref,
                om_ref, oh_ref, st_ref, up_scr, x_scr, s_scr, carry_scr, o_scr):
    t = pl.program_id(1)
    refs = (pw_ref, bblk_ref, cblk_ref, d_ref, wg_ref, bg_ref, g_ref)
    scr = (up_scr, x_scr, s_scr, carry_scr, o_scr)

    @pl.when(t == 0)
    def _():
        carry_scr[...] = jnp.zeros_like(carry_scr)
        _ssm_tile(uh_ref, HEAD_ROWS, 1, SEQ0, refs, oh_ref, scr)

    @pl.when(t > 0)
    def _():
        _ssm_tile(um_ref, um_ref.shape[1], 2, 0, refs, om_ref, scr)
        st_ref[...] = carry_scr[...]


def _ssm(um, uh, pw, bblk, cblk, d, wg, l, bg, g, B, tm):
    assert (1, HEAD_ROWS // SUBLANES, tm // SUBLANES) == SSM_POW_ROWS
    J = SSM_LANE_BLOCKS
    rm, rh = um.shape[1], uh.shape[1]
    nt = rm // (B * tm)
    main = lambda b, t: (b * nt + jnp.maximum(t - 1, 0), 0)
    head = lambda b, t: (b, 0)
    return pl.pallas_call(
        _ssm_kernel,
        grid=(B, nt + 1),
        in_specs=[pl.BlockSpec((J, tm, LANES), lambda b, t: (0, b * nt + jnp.maximum(t - 1, 0), 0)),
                  pl.BlockSpec((J, HEAD_ROWS, LANES), lambda b, t: (0, b, 0)),
                  _layer(pw.shape[1:], l), _layer(bblk.shape[1:], l), _layer(cblk.shape[1:], l),
                  _layer((1, SSM_WIDTH), l), _layer((SSM_WIDTH, SSM_WIDTH), l), _layer((1, SSM_WIDTH), l),
                  _layer((1, SSM_WIDTH), l)],
        out_specs=[pl.BlockSpec((tm, SSM_WIDTH), main), pl.BlockSpec((HEAD_ROWS, SSM_WIDTH), head),
                   pl.BlockSpec((None, SUBLANES, SSM_STATE_LANES), lambda b, t: (b, 0, 0))],
        out_shape=[jax.ShapeDtypeStruct((rm, SSM_WIDTH), BF16), jax.ShapeDtypeStruct((rh, SSM_WIDTH), BF16),
                   jax.ShapeDtypeStruct((B, SUBLANES, SSM_STATE_LANES), F32)],
        scratch_shapes=[pltpu.VMEM((tm, SSM_WIDTH), F32),
                        pltpu.VMEM((tm, SSM_STATE_LANES), F32),
                        pltpu.VMEM((SUBLANES, SSM_STATE_LANES), F32),
                        pltpu.VMEM((SUBLANES, SSM_STATE_LANES), F32),
                        pltpu.VMEM((J, tm, LANES), F32)],
        compiler_params=_params("arbitrary", "arbitrary"),
        name="ssm_prompt",
    )(um, uh, pw, bblk, cblk, d, wg, bg, g)


def _state_from_lanes(s):
    s = s.reshape(s.shape[0], SSM_LANE_BLOCKS, 2, SSM_BLOCK_STATES)
    return (s[:, :, 0].reshape(-1, SSM_GROUPS, SSM_STATE), s[:, :, 1].reshape(-1, SSM_GROUPS, SSM_STATE))


def _pool_tail(d_groups, w_ref, sc_ref, g_ref):
    y = jnp.concatenate(
        [jnp.dot(d.astype(BF16), w_ref[gi], preferred_element_type=F32) for gi, d in enumerate(d_groups)], axis=1)
    return _rms(y * sc_ref[...], g_ref[...])


def _pool_rows(x, prev, pos0, w_ref, sc_ref, g_ref):
    n = x.shape[0]
    xe = jnp.concatenate([prev, x], axis=0)
    pos = pos0 + jax.lax.broadcasted_iota(jnp.int32, (n, 1), 0)
    ds = []
    for gi, w in enumerate(POOL_WINDOWS):
        gsl = slice(gi * POOL_GROUP, (gi + 1) * POOL_GROUP)
        s = xe[:, gsl]
        k = 1
        while k < w:
            s = s + pltpu.roll(s, k, 0)
            k *= 2
        cnt = jnp.clip(pos + 1, 1, w).astype(F32)
        ds.append(s[POOL_HALO:] / cnt - x[:, gsl])
    return _pool_tail(ds, w_ref, sc_ref, g_ref)


def _pool_kernel(xm_ref, halo_ref, xh_ref, w_ref, sc_ref, g_ref, om_ref, oh_ref):
    t = pl.program_id(1)
    tm = xm_ref.shape[0]

    @pl.when(t == 0)
    def _():
        hrow = jax.lax.broadcasted_iota(jnp.int32, (HEAD_ROWS, 1), 0)
        x = jnp.where(hrow >= SEQ0, xh_ref[...], 0.0)
        prev = jnp.zeros((POOL_HALO, POOL_WIDTH), F32)
        oh_ref[...] = _pool_rows(x, prev, -SEQ0, w_ref, sc_ref, g_ref).astype(oh_ref.dtype)

    @pl.when(t > 0)
    def _():
        prev = jnp.where(t == 1, xh_ref[HEAD_ROWS - POOL_HALO:, :], halo_ref[...])
        om_ref[...] = _pool_rows(xm_ref[...], prev, N_META + (t - 1) * tm, w_ref, sc_ref, g_ref).astype(om_ref.dtype)


def _pool(xm, xh, w, l, sc, g, B, tm):
    nt = xm.shape[0] // (B * tm)
    r = tm // POOL_HALO
    main = lambda b, t: (b * nt + jnp.maximum(t - 1, 0), 0)
    halo = lambda b, t: (jnp.maximum((b * nt + t - 1) * r - 1, 0), 0)
    head = lambda b, t: (b, 0)
    return pl.pallas_call(
        _pool_kernel,
        grid=(B, nt + 1),
        in_specs=[pl.BlockSpec((tm, POOL_WIDTH), main), pl.BlockSpec((POOL_HALO, POOL_WIDTH), halo),
                  pl.BlockSpec((HEAD_ROWS, POOL_WIDTH), head),
                  _layer(w.shape[1:], l), _layer((1, POOL_WIDTH), l), _layer((1, POOL_WIDTH), l)],
        out_specs=[pl.BlockSpec((tm, POOL_WIDTH), main), pl.BlockSpec((HEAD_ROWS, POOL_WIDTH), head)],
        out_shape=[jax.ShapeDtypeStruct(xm.shape, BF16), jax.ShapeDtypeStruct(xh.shape, BF16)],
        compiler_params=_params("arbitrary", "arbitrary"),
        name="pool_prompt",
    )(xm, xm, xh, w, sc, g)


def _attn_sample_kernel(q_ref, kn_ref, vn_ref, kc_ref, vc_ref, sink_ref, g_ref, a_in_ref,
                        a_ref, nk_ref, nv_ref, acc_scr):
    del a_in_ref
    step = pl.program_id(0)
    scale = HEAD_DIM ** -0.5
    for bb in range(DEC_STEP):
        nk_ref[bb, 0:WINDOW - 1] = kc_ref[bb, 1:WINDOW]
        nv_ref[bb, 0:WINDOW - 1] = vc_ref[bb, 1:WINDOW]
        outs = []
        for kh in range(N_KV_HEADS):
            ksl = slice(kh * HEAD_DIM, (kh + 1) * HEAD_DIM)
            qh = jnp.concatenate(
                [q_ref[bb:bb + 1, (kh * GQA_GROUP + g) * HEAD_DIM:(kh * GQA_GROUP + g + 1) * HEAD_DIM]
                 for g in range(GQA_GROUP)], axis=0)
            kn = kn_ref[bb:bb + 1, ksl]
            vn = vn_ref[bb:bb + 1, ksl]
            nk_ref[bb, WINDOW - 1, kh:kh + 1, :] = kn
            nv_ref[bb, WINDOW - 1, kh:kh + 1, :] = vn
            kc = kc_ref[bb, :, kh, :]
            vc = vc_ref[bb, :, kh, :]
            sc = jax.lax.dot_general(qh.astype(BF16), kc.astype(BF16), (((1,), (1,)), ((), ())),
                                     preferred_element_type=F32) * scale
            sn = jnp.sum(qh.astype(BF16).astype(F32) * kn.astype(BF16).astype(F32), axis=-1, keepdims=True) * scale
            sk = sink_ref[kh]
            m = jnp.maximum(jnp.maximum(jnp.max(sc, axis=-1, keepdims=True), sn), sk)
            pc = jnp.exp(sc - m)
            pn = jnp.exp(sn - m)
            denom = jnp.sum(pc, axis=-1, keepdims=True) + pn + jnp.exp(sk - m)
            o = jnp.dot(pc.astype(BF16), vc.astype(BF16), preferred_element_type=F32)
            o = (o + pn.astype(BF16).astype(F32) * vn.astype(BF16).astype(F32)) / denom
            outs.extend(o[g:g + 1] for g in range(GQA_GROUP))
        a = jnp.concatenate(outs, axis=1)
        acc_scr[pl.ds(step * DEC_STEP + bb, 1), :] = _rms(a, g_ref[...])

    @pl.when(step == pl.num_programs(0) - 1)
    def _():
        a_ref[...] = acc_scr[...].astype(a_ref.dtype)


def _attn_sample(qh, kh, vh, cache_k, cache_v, l, sinks, g, ah, N):
    rows = lambda w: pl.BlockSpec((DEC_STEP, w), lambda s: (s, 0))
    cache = pl.BlockSpec((None, DEC_STEP, WINDOW, N_KV_HEADS, HEAD_DIM), lambda s: (l, s, 0, 0, 0))
    ncache = pl.BlockSpec((DEC_STEP, WINDOW, N_KV_HEADS, HEAD_DIM), lambda s: (s, 0, 0, 0))
    cshape = jax.ShapeDtypeStruct((N, WINDOW, N_KV_HEADS, HEAD_DIM), F32)
    return pl.pallas_call(
        _attn_sample_kernel,
        grid=(N // DEC_STEP,),
        in_specs=[rows(ATTN_WIDTH), rows(KV_WIDTH), rows(KV_WIDTH), cache, cache,
                  _layer((N_KV_HEADS, GQA_GROUP, 1), l), _layer((1, ATTN_WIDTH), l),
                  pl.BlockSpec(memory_space=pl.ANY)],
        out_specs=[pl.BlockSpec((N, ATTN_WIDTH), lambda s: (0, 0)), ncache, ncache],
        out_shape=[jax.ShapeDtypeStruct(ah.shape, ah.dtype), cshape, cshape],
        scratch_shapes=[pltpu.VMEM((N, ATTN_WIDTH), F32)],
        input_output_aliases={7: 0},
        compiler_params=_params("arbitrary"),
        name="attn_sample",
    )(qh, kh, vh, cache_k, cache_v, sinks, g, ah)


def _mix_sample_kernel(u_ref, h0_ref, pw_ref, bblk_ref, cblk_ref, d_ref, wg_ref, bg_ref, gs_ref,
                       xp_ref, pb_ref, wp_ref, sc_ref, gp_ref, s_in_ref, p_in_ref, s_ref, st_ref, p_ref):
    del s_in_ref, p_in_ref
    S = SSM_BLOCK_STATES
    u = jnp.concatenate([u_ref[j] for j in range(SSM_LANE_BLOCKS)], axis=1)
    ub = u.astype(BF16)
    ys = []
    for j in range(SSM_LANE_BLOCKS):
        x = jnp.dot(ub[:, j * LANES:(j + 1) * LANES], bblk_ref[j], preferred_element_type=F32)
        base = j * 2 * S
        ar = pw_ref[0:1, base:base + S]
        ai = pw_ref[0:1, base + S:base + 2 * S]
        h0r = h0_ref[:, base:base + S]
        h0i = h0_ref[:, base + S:base + 2 * S]
        hr = x[:, 0:S] + ar * h0r - ai * h0i
        hi = x[:, S:] + ar * h0i + ai * h0r
        st_ref[:, base:base + S] = hr
        st_ref[:, base + S:base + 2 * S] = hi
        h = jnp.concatenate([hr, hi], axis=1).astype(BF16)
        ys.append(jnp.dot(h, cblk_ref[j], preferred_element_type=F32))
    s_ref[...] = _ssm_tail(jnp.concatenate(ys, axis=1), u, d_ref, wg_ref, bg_ref, gs_ref).astype(s_ref.dtype)

    xp = xp_ref[...]
    ds = []
    for gi, w in enumerate(POOL_WINDOWS):
        gsl = slice(gi * POOL_GROUP, (gi + 1) * POOL_GROUP)
        s = xp[:, gsl]
        for back in range(1, w):
            s = s + pb_ref[POOL_BUF - back][:, gsl]
        ds.append(s / float(w) - xp[:, gsl])
    p_ref[...] = _pool_tail(ds, wp_ref, sc_ref, gp_ref).astype(p_ref.dtype)


def _mix_sample(uh, h0, pw, bblk, cblk, d, wg, l, bg, gs, xph, pbuf, wp, _l, sc, gp, sh, ph):
    N = h0.shape[1]
    rows = lambda w: pl.BlockSpec((N, w), lambda i: (0, 0))
    anyspec = pl.BlockSpec(memory_space=pl.ANY)
    return pl.pallas_call(
        _mix_sample_kernel,
        grid=(1,),
        in_specs=[pl.BlockSpec((SSM_LANE_BLOCKS, N, LANES), lambda i: (0, 0, 0)),
                  _layer(h0.shape[1:], l), _layer(pw.shape[1:], l), _layer(bblk.shape[1:], l),
                  _layer(cblk.shape[1:], l),
                  _layer((1, SSM_WIDTH), l), _layer((SSM_WIDTH, SSM_WIDTH), l), _layer((1, SSM_WIDTH), l),
                  _layer((1, SSM_WIDTH), l), rows(POOL_WIDTH), _layer(pbuf.shape[1:], l),
                  _layer(wp.shape[1:], l),
                  _layer((1, POOL_WIDTH), l), _layer((1, POOL_WIDTH), l), anyspec, anyspec],
        out_specs=[rows(SSM_WIDTH), _full((N, SSM_STATE_LANES)), rows(POOL_WIDTH)],
        out_shape=[jax.ShapeDtypeStruct(sh.shape, sh.dtype), jax.ShapeDtypeStruct((N, SSM_STATE_LANES), F32),
                   jax.ShapeDtypeStruct(ph.shape, ph.dtype)],
        input_output_aliases={14: 0, 15: 2},
        compiler_params=_params("arbitrary"),
        name="mix_sample",
    )(uh, h0, pw, bblk, cblk, d, wg, bg, gs, xph, pbuf, wp, sc, gp, sh, ph)


def _outproj_kernel(x_ref, a_ref, s_ref, p_ref, w_ref, o_ref):
    o1 = ATTN_WIDTH
    o2 = o1 + SSM_WIDTH
    acc = x_ref[...]
    acc = acc + jnp.dot(a_ref[...], w_ref[0:o1, :], preferred_element_type=F32)
    acc = acc + jnp.dot(s_ref[...], w_ref[o1:o2, :], preferred_element_type=F32)
    acc = acc + jnp.dot(p_ref[...], w_ref[o2:, :], preferred_element_type=F32)
    o_ref[...] = acc


def _outproj(x, a, s, p, w, l, tm):
    R = x.shape[0]
    row = lambda i: (i, 0)
    return pl.pallas_call(
        _outproj_kernel,
        grid=(R // tm,),
        in_specs=[pl.BlockSpec((tm, D_MODEL), row), pl.BlockSpec((tm, ATTN_WIDTH), row),
                  pl.BlockSpec((tm, SSM_WIDTH), row), pl.BlockSpec((tm, POOL_WIDTH), row),
                  _layer((D_MODEL, D_MODEL), l)],
        out_specs=pl.BlockSpec((tm, D_MODEL), row),
        out_shape=jax.ShapeDtypeStruct((R, D_MODEL), F32),
        compiler_params=_params("arbitrary"),
        name="outproj",
    )(x, a, s, p, w)


def _ffn_kernel(xm_ref, xh_ref, g_ref, w1_ref, w2_ref, om_ref, oh_ref, hm_scr, hh_scr):
    i = pl.program_id(0)
    f = pl.program_id(1)

    @pl.when(f == 0)
    def _():
        x = xm_ref[...]
        hm_scr[...] = _rms(x, g_ref[...]).astype(BF16)
        om_ref[...] = x

    @pl.when((f == 0) & (i == 0))
    def _():
        x = xh_ref[...]
        hh_scr[...] = _rms(x, g_ref[...]).astype(BF16)
        oh_ref[...] = x

    w1 = w1_ref[...].astype(BF16)
    w2 = w2_ref[...].astype(BF16)

    def mlp(h):
        h1 = jnp.dot(h, w1, preferred_element_type=F32)
        return jnp.dot(jnp.square(jnp.maximum(h1, 0.0)).astype(BF16), w2, preferred_element_type=F32)

    om_ref[...] += mlp(hm_scr[...])

    @pl.when(i == 0)
    def _():
        oh_ref[...] += mlp(hh_scr[...])


def _ffn(xm, xh, g, w1, w2, l, tm, tf):
    R, RH = xm.shape[0], xh.shape[0]
    once = pl.Buffered(1)
    return pl.pallas_call(
        _ffn_kernel,
        grid=(R // tm, D_FF // tf),
        in_specs=[pl.BlockSpec((tm, D_MODEL), lambda i, f: (i, 0), pipeline_mode=once),
                  pl.BlockSpec((RH, D_MODEL), lambda i, f: (0, 0), pipeline_mode=once),
                  _layer((1, D_MODEL), l),
                  pl.BlockSpec((None, D_MODEL, tf), lambda i, f: (l, 0, f)),
                  pl.BlockSpec((None, tf, D_MODEL), lambda i, f: (l, f, 0))],
        out_specs=[pl.BlockSpec((tm, D_MODEL), lambda i, f: (i, 0)),
                   pl.BlockSpec((RH, D_MODEL), lambda i, f: (0, 0))],
        out_shape=[jax.ShapeDtypeStruct((R, D_MODEL), F32), jax.ShapeDtypeStruct((RH, D_MODEL), F32)],
        scratch_shapes=[pltpu.VMEM((tm, D_MODEL), BF16), pltpu.VMEM((RH, D_MODEL), BF16)],
        compiler_params=_params("arbitrary", "arbitrary"),
        name="ffn",
    )(xm, xh, g, w1, w2)


def _rope_tables(pos):
    half = ROT_DIM // 2
    inv = ROPE_THETA ** (-jnp.arange(0, ROT_DIM, 2, dtype=F32) / ROT_DIM)
    ang = pos.astype(F32)[:, None] * inv
    cos, sin = jnp.cos(ang), jnp.sin(ang)
    n = pos.shape[0]
    z = jnp.zeros((n, HEAD_DIM - ROT_DIM), F32)
    zh = jnp.zeros((n, half), F32)
    rc = jnp.concatenate([cos, cos, z + 1.0], axis=1)
    rs1 = jnp.concatenate([zh, sin, z], axis=1)
    rs2 = jnp.concatenate([-sin, zh, z], axis=1)
    return rc, rs1, rs2


def kernel(x_prompt, x_sample, cache_k, cache_v, state_ssm_re, state_ssm_im, state_pool, meta_tokens, g_mix, w_in, g_q, g_k, sinks, A_re, A_im, log_dt, B_re, B_im, C_re, C_im, D_skip, w_glu, b_glu, w_pool, pool_scale, g_out_attn, g_out_ssm, g_out_pool, w_out, g_ffn, w_ff1, w_ff2):
    B, T, _ = x_prompt.shape
    N = x_sample.shape[0]
    depth = w_in.shape[0]
    assert N <= SEQ0 and N % DEC_STEP == 0 and T % TM_FFN == 0
    meta = meta_tokens.astype(F32)
    head0 = jnp.concatenate([x_sample.reshape(N, D_MODEL), jnp.zeros((SEQ0 - N, D_MODEL), F32), meta], axis=0)
    head_rest = jnp.concatenate([jnp.zeros((SEQ0, D_MODEL), F32), meta], axis=0)
    xh = jnp.concatenate([head0] + [head_rest] * (B - 1), axis=0)
    xm = x_prompt.reshape(B * T, D_MODEL)

    rope_m = _rope_tables(N_META + jnp.arange(T))
    hr_ = jnp.arange(HEAD_ROWS)
    pos_h0 = jnp.where(hr_ < N, PAST_LEN, jnp.maximum(hr_ - SEQ0, 0))
    pos_h = jnp.concatenate([pos_h0] + [jnp.maximum(hr_ - SEQ0, 0)] * (B - 1))
    rope_h = _rope_tables(pos_h)

    wi_all, wo_all = w_in.astype(BF16), w_out.astype(BF16)
    wg_all, wp_all = w_glu.astype(BF16), w_pool.astype(BF16)
    ck = cache_k.astype(F32)
    cv = cache_v.astype(F32)
    RH = B * HEAD_ROWS

    vec = lambda t: t.astype(F32).reshape(depth, 1, -1)
    pw, bblk, cblk = _ssm_params(A_re, A_im, log_dt, B_re, B_im, C_re, C_im)
    ssm_w = lambda l: (pw, bblk, cblk, vec(D_skip), wg_all, l, vec(b_glu), vec(g_out_ssm))
    pool_w = lambda l: (wp_all, l, vec(pool_scale), vec(g_out_pool))
    in_w = lambda l: (vec(g_mix), wi_all, l, vec(g_q), vec(g_k))
    g_attn = vec(g_out_attn)
    g_ffn_ = vec(g_ffn)
    sinks_flat = sinks.astype(F32).reshape(depth * N_HEADS)
    sinks_col = sinks.astype(F32).reshape(depth, N_KV_HEADS, GQA_GROUP, 1)
    bias = _attn_bias()
    h0 = jnp.concatenate([state_ssm_re.astype(F32).reshape(depth, N, SSM_LANE_BLOCKS, SSM_BLOCK_STATES),
                          state_ssm_im.astype(F32).reshape(depth, N, SSM_LANE_BLOCKS, SSM_BLOCK_STATES)],
                         axis=-1).reshape(depth, N, SSM_STATE_LANES)
    pbuf = state_pool.astype(F32).transpose(0, 2, 1, 3)

    ks, vs, pls, sts, nks, nvs, st_ss, phs = ([] for _ in range(8))
    for l in range(depth):
        qm, km, vm, um, pm = _inproj(xm, *in_w(l), rope_m, TM_PROJ)
        qh, kh, vh, uh, ph = _inproj(xh, *in_w(l), rope_h, RH)

        am, ah = _attn(sinks_flat, l, qm, qh, km, kh, vm, vh, g_attn, bias, B)
        sm, sh, st = _ssm(um, uh, *ssm_w(l), B, TM_SEQ)
        plm, plh = _pool(pm, ph, *pool_w(l), B, TM_SEQ)

        ah, nk, nv = _attn_sample(qh, kh, vh, ck, cv, l, sinks_col, g_attn, ah, N)
        sh, st_s, plh = _mix_sample(uh, h0, *ssm_w(l), ph, pbuf, *pool_w(l), sh, plh)

        xm = _outproj(xm, am, sm, plm, wo_all, l, TM_PROJ)
        xh = _outproj(xh, ah, sh, plh, wo_all, l, RH)
        xm, xh = _ffn(xm, xh, g_ffn_, w_ff1, w_ff2, l, TM_FFN, TF_FFN)

        ks.append(km.reshape(B, T, KV_WIDTH)[:, T - WINDOW:])
        vs.append(vm.reshape(B, T, KV_WIDTH)[:, T - WINDOW:])
        pls.append(pm.reshape(B, T, POOL_WIDTH)[:, T - POOL_BUF:])
        sts.append(st[:, 0])
        nks.append(nk)
        nvs.append(nv)
        st_ss.append(st_s)
        phs.append(ph[:N])

    y_prompt = xm.reshape(B, T, D_MODEL)
    y_sample = xh[:N].reshape(N, 1, D_MODEL)
    heads = lambda t: jnp.stack(t).reshape(depth, -1, WINDOW, N_KV_HEADS, HEAD_DIM)
    p_re, p_im = _state_from_lanes(jnp.stack(sts).reshape(depth * B, SSM_STATE_LANES))
    s_re, s_im = _state_from_lanes(jnp.stack(st_ss).reshape(depth * N, SSM_STATE_LANES))
    st4 = lambda t, n: t.reshape(depth, n, SSM_GROUPS, SSM_STATE)
    s_pool = jnp.concatenate([state_pool.astype(F32)[:, :, 1:], jnp.stack(phs)[:, :, None]], axis=2)
    return (y_prompt, y_sample, heads(ks), heads(vs), st4(p_re, B), st4(p_im, B), jnp.stack(pls),
            jnp.stack(nks), jnp.stack(nvs), st4(s_re, N), st4(s_im, N), s_pool)
```

```python
import functools
import math

import jax
import jax.numpy as jnp
from jax.experimental import pallas as pl
from jax.experimental.pallas import tpu as pltpu

D_MODEL = 2048
N_META = 16
HEAD_DIM = 128
N_HEADS = 8
N_KV_HEADS = 2
GQA_GROUP = 4
ATTN_WIDTH = 1024
KV_WIDTH = 256
WINDOW = 128
BLOCK = 128
ROT_DIM = 32
ROPE_THETA = 500000.0
SSM_WIDTH = 512
SSM_GROUP_SIZE = 16
SSM_GROUPS = 32
SSM_STATE = 64
POOL_WIDTH = 512
POOL_WINDOWS = (2, 4, 8, 16)
POOL_GROUP = 128
POOL_BUF = 15
POOL_HALO = 16
IN_WIDTH = 2560
D_FF = 8192
EPS = 1e-6
PAST_LEN = 16384
LOG2E = math.log2(math.e)

HEAD_ROWS = BLOCK
SEQ0 = HEAD_ROWS - N_META
LANES = 128
SUBLANES = 8
SSM_LANE_BLOCKS = SSM_WIDTH // LANES
SSM_BLOCK_STATES = (LANES // SSM_GROUP_SIZE) * SSM_STATE
SSM_STATE_LANES = SSM_LANE_BLOCKS * 2 * SSM_BLOCK_STATES
VMEM_LIMIT = 56 * 1024 * 1024

TM_PROJ = 512
TM_FFN = 1024
TF_FFN = 512
TM_SEQ = 512
DEC_STEP = 8
SSM_POW_ROWS = (1, HEAD_ROWS // SUBLANES, TM_SEQ // SUBLANES)

BF16 = jnp.bfloat16
F32 = jnp.float32


def _params(*semantics):
    return pltpu.CompilerParams(dimension_semantics=semantics, vmem_limit_bytes=VMEM_LIMIT)


def _rms(x, g):
    return x * jax.lax.rsqrt(jnp.mean(x * x, axis=-1, keepdims=True) + EPS) * g


def _full(shape):
    n = len(shape)
    return pl.BlockSpec(shape, lambda *_: (0,) * n)


def _layer(shape, l):
    n = len(shape)
    return pl.BlockSpec((None, *shape), lambda *_: (l,) + (0,) * n)


def _inproj_kernel(x_ref, g_ref, w_ref, gq_ref, gk_ref, rc_ref, rs1_ref, rs2_ref, *rest):
    if len(rest) == 7:
        cast_in_ref, q_ref, k_ref, v_ref, u_ref, xp_ref, cast_out_ref = rest
        cast_out_ref[...] = cast_in_ref[...].astype(BF16)
    else:
        q_ref, k_ref, v_ref, u_ref, xp_ref = rest
    h = _rms(x_ref[...], g_ref[...]).astype(BF16)
    proj = jnp.dot(h, w_ref[...], preferred_element_type=F32)
    rc, rs1, rs2 = rc_ref[...], rs1_ref[...], rs2_ref[...]

    def head(t, g):
        t = _rms(t, g)
        return t * rc + pltpu.roll(t, 16, 1) * rs1 + pltpu.roll(t, LANES - 16, 1) * rs2

    for hd in range(N_HEADS):
        sl = slice(hd * HEAD_DIM, (hd + 1) * HEAD_DIM)
        q_ref[:, sl] = head(proj[:, sl], gq_ref[...])
    for hd in range(N_KV_HEADS):
        sl = slice(hd * HEAD_DIM, (hd + 1) * HEAD_DIM)
        k_ref[:, sl] = head(proj[:, ATTN_WIDTH + hd * HEAD_DIM:ATTN_WIDTH + (hd + 1) * HEAD_DIM], gk_ref[...])
    o2 = ATTN_WIDTH + KV_WIDTH
    o3 = o2 + KV_WIDTH
    o4 = o3 + SSM_WIDTH
    v_ref[...] = proj[:, o2:o3]
    for j in range(SSM_LANE_BLOCKS):
        u_ref[j] = proj[:, o3 + j * LANES:o3 + (j + 1) * LANES]
    xp_ref[...] = proj[:, o4:]


def _cast_side_job(w, l, steps):
    _, r, c = w.shape
    rows = r // steps
    return (pl.BlockSpec((None, rows, c), lambda i: (l, i, 0)), pl.BlockSpec((rows, c), lambda i: (i, 0)),
            jax.ShapeDtypeStruct((r, c), BF16))


def _inproj(x, g, w, l, gq, gk, rope, tm, cast=None):
    R = x.shape[0]
    tiles_per_rope = rope[0].shape[0] // tm
    row = lambda i: (i, 0)
    rrow = lambda i: (i % tiles_per_rope, 0)
    flat = lambda w_: (pl.BlockSpec((tm, w_), row), jax.ShapeDtypeStruct((R, w_), F32))
    u_out = (pl.BlockSpec((SSM_LANE_BLOCKS, tm, LANES), lambda i: (0, i, 0)),
             jax.ShapeDtypeStruct((SSM_LANE_BLOCKS, R, LANES), F32))
    outs = [flat(ATTN_WIDTH), flat(KV_WIDTH), flat(KV_WIDTH), u_out, flat(POOL_WIDTH)]
    w_spec = pl.BlockSpec((None, D_MODEL, IN_WIDTH), lambda i: (l, 0, 0), pipeline_mode=pl.Buffered(1))
    in_specs = [pl.BlockSpec((tm, D_MODEL), row), _layer((1, D_MODEL), l), w_spec,
                _layer((1, HEAD_DIM), l), _layer((1, HEAD_DIM), l),
                pl.BlockSpec((tm, LANES), rrow), pl.BlockSpec((tm, LANES), rrow),
                pl.BlockSpec((tm, LANES), rrow)]
    args = [x, g, w, gq, gk, *rope]
    if cast is not None:
        c_in, c_out, c_shape = _cast_side_job(cast, l, R // tm)
        in_specs.append(c_in)
        args.append(cast)
        outs.append((c_out, c_shape))
    return pl.pallas_call(
        _inproj_kernel,
        grid=(R // tm,),
        in_specs=in_specs,
        out_specs=[o[0] for o in outs],
        out_shape=[o[1] for o in outs],
        compiler_params=_params("arbitrary"),
        name="inproj",
    )(*args)


def _attn_bias():
    rows = GQA_GROUP * BLOCK
    r = jax.lax.broadcasted_iota(jnp.int32, (3, rows, 2 * BLOCK), 1) % BLOCK
    c = jax.lax.broadcasted_iota(jnp.int32, (3, rows, 2 * BLOCK), 2)
    i = jax.lax.broadcasted_iota(jnp.int32, (3, rows, 2 * BLOCK), 0)
    diff = BLOCK + r - c
    krow = (i - 1) * BLOCK + c
    mask = (diff >= 0) & (diff <= WINDOW) & (krow >= SEQ0)
    return jnp.where(mask, 0.0, -jnp.inf).astype(F32)


def _attn_kernel(sink_ref, qm_ref, qh_ref, kpm_ref, kcm_ref, kh_ref, vpm_ref, vcm_ref, vh_ref, g_ref, bias_ref,
                 om_ref, oh_ref, *, l):
    i = pl.program_id(1)
    first = i == 0
    second = i == 1
    q_blk = jnp.where(first, qh_ref[...], qm_ref[...])
    kc_blk = jnp.where(first, kh_ref[...], kcm_ref[...])
    vc_blk = jnp.where(first, vh_ref[...], vcm_ref[...])
    kp_blk = jnp.where(second, kh_ref[...], kpm_ref[...])
    vp_blk = jnp.where(second, vh_ref[...], vpm_ref[...])
    rows = GQA_GROUP * BLOCK
    bias = bias_ref[...]
    rgrp = jax.lax.broadcasted_iota(jnp.int32, (rows, 1), 0) // BLOCK
    outs = []
    for kh in range(N_KV_HEADS):
        ksl = slice(kh * HEAD_DIM, (kh + 1) * HEAD_DIM)
        qh = jnp.concatenate(
            [q_blk[:, (kh * GQA_GROUP + g) * HEAD_DIM:(kh * GQA_GROUP + g + 1) * HEAD_DIM]
             for g in range(GQA_GROUP)], axis=0).astype(BF16)
        kk = jnp.concatenate([kp_blk[:, ksl], kc_blk[:, ksl]], axis=0).astype(BF16)
        vv = jnp.concatenate([vp_blk[:, ksl], vc_blk[:, ksl]], axis=0).astype(BF16)
        s = jax.lax.dot_general(qh, kk, (((1,), (1,)), ((), ())),
                                preferred_element_type=F32) * (HEAD_DIM ** -0.5 * LOG2E) + bias
        sk = jnp.zeros((rows, 1), F32)
        for g in range(GQA_GROUP):
            sk = jnp.where(rgrp == g, sink_ref[l * N_HEADS + kh * GQA_GROUP + g] * LOG2E, sk)
        m = jnp.maximum(jnp.max(s, axis=-1, keepdims=True), sk)
        p = jnp.exp2(s - m)
        denom = jnp.sum(p, axis=-1, keepdims=True) + jnp.exp2(sk - m)
        o = jnp.dot(p.astype(BF16), vv, preferred_element_type=F32) / denom
        outs.extend(o[g * BLOCK:(g + 1) * BLOCK] for g in range(GQA_GROUP))
    a = _rms(jnp.concatenate(outs, axis=1), g_ref[...]).astype(om_ref.dtype)

    @pl.when(first)
    def _():
        oh_ref[...] = a

    @pl.when(jnp.logical_not(first))
    def _():
        om_ref[...] = a


def _attn(sinks, l, qm, qh, km, kh, vm, vh, g, bias, B):
    nb = qm.shape[0] // (B * BLOCK)
    cur = lambda b, i, s: (b * nb + jnp.maximum(i - 1, 0), 0)
    prev = lambda b, i, s: (b * nb + jnp.maximum(i - 2, 0), 0)
    head = lambda b, i, s: (b, 0)
    spec = lambda w, im: pl.BlockSpec((BLOCK, w), im)
    return pl.pallas_call(
        functools.partial(_attn_kernel, l=l),
        grid_spec=pltpu.PrefetchScalarGridSpec(
            num_scalar_prefetch=1,
            grid=(B, nb + 1),
            in_specs=[spec(ATTN_WIDTH, cur), spec(ATTN_WIDTH, head),
                      spec(KV_WIDTH, prev), spec(KV_WIDTH, cur), spec(KV_WIDTH, head),
                      spec(KV_WIDTH, prev), spec(KV_WIDTH, cur), spec(KV_WIDTH, head),
                      pl.BlockSpec((None, 1, ATTN_WIDTH), lambda b, i, s: (l, 0, 0)),
                      pl.BlockSpec((None,) + bias.shape[1:], lambda b, i, s: (jnp.minimum(i, 2), 0, 0))],
            out_specs=[spec(ATTN_WIDTH, cur), spec(ATTN_WIDTH, head)],
        ),
        out_shape=[jax.ShapeDtypeStruct(qm.shape, BF16), jax.ShapeDtypeStruct(qh.shape, BF16)],
        compiler_params=_params("arbitrary", "arbitrary"),
        name="attn_prompt",
    )(sinks, qm, qh, km, km, kh, vm, vm, vh, g, bias)


def _ssm_params_kernel(ar_ref, ai_ref, ldt_ref, kk_ref, br_ref, bi_ref, tr_ref, ti_ref, bbr_ref, bbi_ref):
    ar, ai = ar_ref[...], ai_ref[...]
    dt = jnp.exp(ldt_ref[...])
    kk = kk_ref[...]
    mag = jnp.exp(dt * ar * kk)
    ang = dt * ai * kk
    tr = mag * jnp.cos(ang)
    ti = mag * jnp.sin(ang)
    tr_ref[...] = tr
    ti_ref[...] = ti
    abr, abi = tr[:, 0:1], ti[:, 0:1]
    den = ar * ar + ai * ai
    fr = ((abr - 1.0) * ar + abi * ai) / den
    fi = (abi * ar - (abr - 1.0) * ai) / den
    br, bi = br_ref[...], bi_ref[...]
    bbr_ref[...] = fr * br - fi * bi
    bbi_ref[...] = fr * bi + fi * br


def _ssm_params(A_re, A_im, log_dt, B_re, B_im, C_re, C_im):
    depth = A_re.shape[0]
    n = depth * SSM_GROUPS * SSM_STATE
    col = lambda t: t.astype(F32).reshape(n, 1)
    ldt = jnp.broadcast_to(log_dt.astype(F32)[:, :, None], (depth, SSM_GROUPS, SSM_STATE)).reshape(n, 1)
    kk = jnp.array([SSM_POW_ROWS + (0,) * (SUBLANES - len(SSM_POW_ROWS))], F32)
    shapes = [(n, SUBLANES), (n, SUBLANES), (n, SSM_GROUP_SIZE), (n, SSM_GROUP_SIZE)]
    tr, ti, bbr, bbi = pl.pallas_call(
        _ssm_params_kernel,
        out_shape=[jax.ShapeDtypeStruct(s, F32) for s in shapes],
        name="ssm_params",
    )(col(A_re), col(A_im), ldt, kk, B_re.astype(F32).reshape(n, SSM_GROUP_SIZE),
      B_im.astype(F32).reshape(n, SSM_GROUP_SIZE))
    J, G8 = SSM_LANE_BLOCKS, LANES // SSM_GROUP_SIZE

    def lanes(t):
        return t.reshape(depth, J, SSM_BLOCK_STATES, SUBLANES).transpose(0, 3, 1, 2)

    pw = jnp.concatenate([lanes(tr), lanes(ti)], axis=-1).reshape(depth, SUBLANES, SSM_STATE_LANES)
    eye = jnp.eye(G8, dtype=F32)

    def bdiag(t):
        t = t.reshape(depth, J, G8, SSM_STATE, SSM_GROUP_SIZE).transpose(0, 1, 2, 4, 3)
        t = t[:, :, :, :, None, :] * eye[None, None, :, None, :, None]
        return t.reshape(depth, J, LANES, SSM_BLOCK_STATES)

    bblk = jnp.concatenate([bdiag(bbr), bdiag(bbi)], axis=-1).astype(BF16)

    def cdiag(t):
        t = t.astype(F32).reshape(depth, J, G8, SSM_GROUP_SIZE, SSM_STATE).transpose(0, 1, 2, 4, 3)
        t = t[:, :, :, :, None, :] * eye[None, None, :, None, :, None]
        return t.reshape(depth, J, SSM_BLOCK_STATES, LANES)

    cblk = jnp.concatenate([cdiag(C_re), -cdiag(C_im)], axis=2).astype(BF16)
    return pw, bblk, cblk


def _ssm_tail(y, u, d_ref, wg_ref, bg_ref, g_ref):
    y = y + d_ref[...] * u
    z = jax.nn.gelu(y)
    gate = jax.nn.sigmoid(jnp.dot(z.astype(BF16), wg_ref[...], preferred_element_type=F32) + bg_ref[...])
    return _rms(z * gate, g_ref[...])


def _ssm_sweep(x_scr, n, a_tabs, init, store):
    S = SSM_BLOCK_STATES
    fins = []
    for j0 in range(0, SSM_LANE_BLOCKS, 2):
        js = (j0, j0 + 1)

        def body(k, carry, js=js):
            r0 = pl.multiple_of(k * SUBLANES, SUBLANES)
            out = []
            for idx, j in enumerate(js):
                hr, hi = carry[2 * idx], carry[2 * idx + 1]
                base = j * 2 * S
                ar, ai = a_tabs[j]
                nhr = ar * hr - ai * hi + x_scr[pl.ds(r0, SUBLANES), base:base + S]
                nhi = ar * hi + ai * hr + x_scr[pl.ds(r0, SUBLANES), base + S:base + 2 * S]
                if store:
                    x_scr[pl.ds(r0, SUBLANES), base:base + S] = nhr
                    x_scr[pl.ds(r0, SUBLANES), base + S:base + 2 * S] = nhi
                out += [nhr, nhi]
            return tuple(out)

        c0 = tuple(t for j in js for t in init[j])
        res = jax.lax.fori_loop(0, n // SUBLANES, body, c0, unroll=2)
        fins += [(res[0], res[1]), (res[2], res[3])]
    return fins


def _ssm_rows(u, pow_row, pw_ref, bblk_ref, cblk_ref, x_scr, s_scr, carry_scr):
    n = u.shape[0]
    S = SSM_BLOCK_STATES
    ub = u.astype(BF16)
    for j in range(SSM_LANE_BLOCKS):
        x_scr[0:n, j * 2 * S:(j + 1) * 2 * S] = jnp.dot(ub[:, j * LANES:(j + 1) * LANES], bblk_ref[j],
                                                       preferred_element_type=F32)
    bc = lambda t: jnp.broadcast_to(t, (SUBLANES, S))
    re = lambda ref, r0, r1, j: ref[r0:r1, j * 2 * S:j * 2 * S + S]
    im = lambda ref, r0, r1, j: ref[r0:r1, j * 2 * S + S:(j + 1) * 2 * S]
    a_tabs = [(bc(re(pw_ref, 0, 1, j)), bc(im(pw_ref, 0, 1, j))) for j in range(SSM_LANE_BLOCKS)]
    zero = jnp.zeros((SUBLANES, S), F32)
    fins = _ssm_sweep(x_scr, n, a_tabs, [(zero, zero)] * SSM_LANE_BLOCKS, store=False)
    for j in range(SSM_LANE_BLOCKS):
        base = j * 2 * S
        cr, ci = re(pw_ref, pow_row, pow_row + 1, j), im(pw_ref, pow_row, pow_row + 1, j)
        sr, si = re(carry_scr, 0, 1, j), im(carry_scr, 0, 1, j)
        fr, fi = fins[j]
        for c in range(SUBLANES):
            s_scr[c:c + 1, base:base + S] = sr
            s_scr[c:c + 1, base + S:base + 2 * S] = si
            sr, si = cr * sr - ci * si + fr[c:c + 1], cr * si + ci * sr + fi[c:c + 1]
        carry_scr[:, base:base + S] = bc(sr)
        carry_scr[:, base + S:base + 2 * S] = bc(si)
    init = [(re(s_scr, 0, SUBLANES, j), im(s_scr, 0, SUBLANES, j)) for j in range(SSM_LANE_BLOCKS)]
    _ssm_sweep(x_scr, n, a_tabs, init, store=True)
    ys = [jnp.dot(x_scr[0:n, j * 2 * S:(j + 1) * 2 * S].astype(BF16), cblk_ref[j], preferred_element_type=F32)
          for j in range(SSM_LANE_BLOCKS)]
    return jnp.concatenate(ys, axis=1)


def _ssm_tile(u_ref, n, pow_row, seq_start, refs, o_ref, scr):
    pw_ref, bblk_ref, cblk_ref, d_ref, wg_ref, bg_ref, g_ref = refs
    up_scr, x_scr, s_scr, carry_scr, o_scr = scr
    q = n // SUBLANES
    for j in range(SSM_LANE_BLOCKS):
        for k in range(q):
            up_scr[k * SUBLANES:(k + 1) * SUBLANES, j * LANES:(j + 1) * LANES] = \
                u_ref[j, pl.ds(k, SUBLANES, stride=q), :]
    u = up_scr[0:n, :]
    if seq_start:
        p = jax.lax.broadcasted_iota(jnp.int32, (n, 1), 0)
        u = jnp.where((p % SUBLANES) * q + p // SUBLANES >= seq_start, u, 0.0)
    y = _ssm_rows(u, pow_row, pw_ref, bblk_ref, cblk_ref, x_scr, s_scr, carry_scr)
    out = _ssm_tail(y, u, d_ref, wg_ref, bg_ref, g_ref)
    for j in range(SSM_LANE_BLOCKS):
        for k in range(q):
            o_scr[j, pl.ds(k, SUBLANES, stride=q), :] = out[k * SUBLANES:(k + 1) * SUBLANES,
                                                            j * LANES:(j + 1) * LANES]
    o_ref[...] = jnp.concatenate([o_scr[j, 0:n, :] for j in range(SSM_LANE_BLOCKS)], axis=1).astype(o_ref.dtype)


def _ssm_kernel(um_ref, uh_ref, pw_ref, bblk_ref, cblk_ref, d_ref, wg_ref, bg_ref, g_ref,
                om_ref, oh_ref, st_ref, up_scr, x_scr, s_scr, carry_scr, o_scr):
    t = pl.program_id(1)
    refs = (pw_ref, bblk_ref, cblk_ref, d_ref, wg_ref, bg_ref, g_ref)
    scr = (up_scr, x_scr, s_scr, carry_scr, o_scr)

    @pl.when(t == 0)
    def _():
        carry_scr[...] = jnp.zeros_like(carry_scr)
        _ssm_tile(uh_ref, HEAD_ROWS, 1, SEQ0, refs, oh_ref, scr)

    @pl.when(t > 0)
    def _():
        _ssm_tile(um_ref, um_ref.shape[1], 2, 0, refs, om_ref, scr)
        st_ref[...] = carry_scr[...]


def _ssm(um, uh, pw, bblk, cblk, d, wg, l, bg, g, B, tm):
    assert (1, HEAD_ROWS // SUBLANES, tm // SUBLANES) == SSM_POW_ROWS
    J = SSM_LANE_BLOCKS
    rm, rh = um.shape[1], uh.shape[1]
    nt = rm // (B * tm)
    main = lambda b, t: (b * nt + jnp.maximum(t - 1, 0), 0)
    head = lambda b, t: (b, 0)
    return pl.pallas_call(
        _ssm_kernel,
        grid=(B, nt + 1),
        in_specs=[pl.BlockSpec((J, tm, LANES), lambda b, t: (0, b * nt + jnp.maximum(t - 1, 0), 0)),
                  pl.BlockSpec((J, HEAD_ROWS, LANES), lambda b, t: (0, b, 0)),
                  _layer(pw.shape[1:], l), _layer(bblk.shape[1:], l), _layer(cblk.shape[1:], l),
                  _layer((1, SSM_WIDTH), l), _layer((SSM_WIDTH, SSM_WIDTH), l), _layer((1, SSM_WIDTH), l),
                  _layer((1, SSM_WIDTH), l)],
        out_specs=[pl.BlockSpec((tm, SSM_WIDTH), main), pl.BlockSpec((HEAD_ROWS, SSM_WIDTH), head),
                   pl.BlockSpec((None, SUBLANES, SSM_STATE_LANES), lambda b, t: (b, 0, 0))],
        out_shape=[jax.ShapeDtypeStruct((rm, SSM_WIDTH), BF16), jax.ShapeDtypeStruct((rh, SSM_WIDTH), BF16),
                   jax.ShapeDtypeStruct((B, SUBLANES, SSM_STATE_LANES), F32)],
        scratch_shapes=[pltpu.VMEM((tm, SSM_WIDTH), F32),
                        pltpu.VMEM((tm, SSM_STATE_LANES), F32),
                        pltpu.VMEM((SUBLANES, SSM_STATE_LANES), F32),
                        pltpu.VMEM((SUBLANES, SSM_STATE_LANES), F32),
                        pltpu.VMEM((J, tm, LANES), F32)],
        compiler_params=_params("arbitrary", "arbitrary"),
        name="ssm_prompt",
    )(um, uh, pw, bblk, cblk, d, wg, bg, g)


def _state_from_lanes(s):
    s = s.reshape(s.shape[0], SSM_LANE_BLOCKS, 2, SSM_BLOCK_STATES)
    return (s[:, :, 0].reshape(-1, SSM_GROUPS, SSM_STATE), s[:, :, 1].reshape(-1, SSM_GROUPS, SSM_STATE))


def _pool_tail(d_groups, w_ref, sc_ref, g_ref):
    y = jnp.concatenate(
        [jnp.dot(d.astype(BF16), w_ref[gi], preferred_element_type=F32) for gi, d in enumerate(d_groups)], axis=1)
    return _rms(y * sc_ref[...], g_ref[...])


def _pool_rows(x, prev, pos0, w_ref, sc_ref, g_ref):
    n = x.shape[0]
    xe = jnp.concatenate([prev, x], axis=0)
    pos = pos0 + jax.lax.broadcasted_iota(jnp.int32, (n, 1), 0)
    ds = []
    for gi, w in enumerate(POOL_WINDOWS):
        gsl = slice(gi * POOL_GROUP, (gi + 1) * POOL_GROUP)
        s = xe[:, gsl]
        k = 1
        while k < w:
            s = s + pltpu.roll(s, k, 0)
            k *= 2
        cnt = jnp.clip(pos + 1, 1, w).astype(F32)
        ds.append(s[POOL_HALO:] / cnt - x[:, gsl])
    return _pool_tail(ds, w_ref, sc_ref, g_ref)


def _pool_kernel(xm_ref, halo_ref, xh_ref, w_ref, sc_ref, g_ref, om_ref, oh_ref):
    t = pl.program_id(1)
    tm = xm_ref.shape[0]

    @pl.when(t == 0)
    def _():
        hrow = jax.lax.broadcasted_iota(jnp.int32, (HEAD_ROWS, 1), 0)
        x = jnp.where(hrow >= SEQ0, xh_ref[...], 0.0)
        prev = jnp.zeros((POOL_HALO, POOL_WIDTH), F32)
        oh_ref[...] = _pool_rows(x, prev, -SEQ0, w_ref, sc_ref, g_ref).astype(oh_ref.dtype)

    @pl.when(t > 0)
    def _():
        prev = jnp.where(t == 1, xh_ref[HEAD_ROWS - POOL_HALO:, :], halo_ref[...])
        om_ref[...] = _pool_rows(xm_ref[...], prev, N_META + (t - 1) * tm, w_ref, sc_ref, g_ref).astype(om_ref.dtype)


def _pool(xm, xh, w, l, sc, g, B, tm):
    nt = xm.shape[0] // (B * tm)
    r = tm // POOL_HALO
    main = lambda b, t: (b * nt + jnp.maximum(t - 1, 0), 0)
    halo = lambda b, t: (jnp.maximum((b * nt + t - 1) * r - 1, 0), 0)
    head = lambda b, t: (b, 0)
    return pl.pallas_call(
        _pool_kernel,
        grid=(B, nt + 1),
        in_specs=[pl.BlockSpec((tm, POOL_WIDTH), main), pl.BlockSpec((POOL_HALO, POOL_WIDTH), halo),
                  pl.BlockSpec((HEAD_ROWS, POOL_WIDTH), head),
                  _layer(w.shape[1:], l), _layer((1, POOL_WIDTH), l), _layer((1, POOL_WIDTH), l)],
        out_specs=[pl.BlockSpec((tm, POOL_WIDTH), main), pl.BlockSpec((HEAD_ROWS, POOL_WIDTH), head)],
        out_shape=[jax.ShapeDtypeStruct(xm.shape, BF16), jax.ShapeDtypeStruct(xh.shape, BF16)],
        compiler_params=_params("arbitrary", "arbitrary"),
        name="pool_prompt",
    )(xm, xm, xh, w, sc, g)


def _attn_sample_kernel(q_ref, kn_ref, vn_ref, kc_ref, vc_ref, sink_ref, g_ref, a_in_ref,
                        a_ref, nk_ref, nv_ref, acc_scr):
    del a_in_ref
    step = pl.program_id(0)
    scale = HEAD_DIM ** -0.5
    for bb in range(DEC_STEP):
        nk_ref[bb, 0:WINDOW - 1] = kc_ref[bb, 1:WINDOW]
        nv_ref[bb, 0:WINDOW - 1] = vc_ref[bb, 1:WINDOW]
        outs = []
        for kh in range(N_KV_HEADS):
            ksl = slice(kh * HEAD_DIM, (kh + 1) * HEAD_DIM)
            qh = jnp.concatenate(
                [q_ref[bb:bb + 1, (kh * GQA_GROUP + g) * HEAD_DIM:(kh * GQA_GROUP + g + 1) * HEAD_DIM]
                 for g in range(GQA_GROUP)], axis=0)
            kn = kn_ref[bb:bb + 1, ksl]
            vn = vn_ref[bb:bb + 1, ksl]
            nk_ref[bb, WINDOW - 1, kh:kh + 1, :] = kn
            nv_ref[bb, WINDOW - 1, kh:kh + 1, :] = vn
            kc = kc_ref[bb, :, kh, :]
            vc = vc_ref[bb, :, kh, :]
            sc = jax.lax.dot_general(qh.astype(BF16), kc.astype(BF16), (((1,), (1,)), ((), ())),
                                     preferred_element_type=F32) * scale
            sn = jnp.sum(qh.astype(BF16).astype(F32) * kn.astype(BF16).astype(F32), axis=-1, keepdims=True) * scale
            sk = sink_ref[kh]
            m = jnp.maximum(jnp.maximum(jnp.max(sc, axis=-1, keepdims=True), sn), sk)
            pc = jnp.exp(sc - m)
            pn = jnp.exp(sn - m)
            denom = jnp.sum(pc, axis=-1, keepdims=True) + pn + jnp.exp(sk - m)
            o = jnp.dot(pc.astype(BF16), vc.astype(BF16), preferred_element_type=F32)
            o = (o + pn.astype(BF16).astype(F32) * vn.astype(BF16).astype(F32)) / denom
            outs.extend(o[g:g + 1] for g in range(GQA_GROUP))
        a = jnp.concatenate(outs, axis=1)
        acc_scr[pl.ds(step * DEC_STEP + bb, 1), :] = _rms(a, g_ref[...])

    @pl.when(step == pl.num_programs(0) - 1)
    def _():
        a_ref[...] = acc_scr[...].astype(a_ref.dtype)


def _attn_sample(qh, kh, vh, cache_k, cache_v, l, sinks, g, ah, N):
    rows = lambda w: pl.BlockSpec((DEC_STEP, w), lambda s: (s, 0))
    cache = pl.BlockSpec((None, DEC_STEP, WINDOW, N_KV_HEADS, HEAD_DIM), lambda s: (l, s, 0, 0, 0))
    ncache = pl.BlockSpec((DEC_STEP, WINDOW, N_KV_HEADS, HEAD_DIM), lambda s: (s, 0, 0, 0))
    cshape = jax.ShapeDtypeStruct((N, WINDOW, N_KV_HEADS, HEAD_DIM), F32)
    return pl.pallas_call(
        _attn_sample_kernel,
        grid=(N // DEC_STEP,),
        in_specs=[rows(ATTN_WIDTH), rows(KV_WIDTH), rows(KV_WIDTH), cache, cache,
                  _layer((N_KV_HEADS, GQA_GROUP, 1), l), _layer((1, ATTN_WIDTH), l),
                  pl.BlockSpec(memory_space=pl.ANY)],
        out_specs=[pl.BlockSpec((N, ATTN_WIDTH), lambda s: (0, 0)), ncache, ncache],
        out_shape=[jax.ShapeDtypeStruct(ah.shape, ah.dtype), cshape, cshape],
        scratch_shapes=[pltpu.VMEM((N, ATTN_WIDTH), F32)],
        input_output_aliases={7: 0},
        compiler_params=_params("arbitrary"),
        name="attn_sample",
    )(qh, kh, vh, cache_k, cache_v, sinks, g, ah)


def _mix_sample_kernel(u_ref, h0_ref, pw_ref, bblk_ref, cblk_ref, d_ref, wg_ref, bg_ref, gs_ref,
                       xp_ref, pb_ref, wp_ref, sc_ref, gp_ref, s_in_ref, p_in_ref, s_ref, st_ref, p_ref):
    del s_in_ref, p_in_ref
    S = SSM_BLOCK_STATES
    u = jnp.concatenate([u_ref[j] for j in range(SSM_LANE_BLOCKS)], axis=1)
    ub = u.astype(BF16)
    ys = []
    for j in range(SSM_LANE_BLOCKS):
        x = jnp.dot(ub[:, j * LANES:(j + 1) * LANES], bblk_ref[j], preferred_element_type=F32)
        base = j * 2 * S
        ar = pw_ref[0:1, base:base + S]
        ai = pw_ref[0:1, base + S:base + 2 * S]
        h0r = h0_ref[:, base:base + S]
        h0i = h0_ref[:, base + S:base + 2 * S]
        hr = x[:, 0:S] + ar * h0r - ai * h0i
        hi = x[:, S:] + ar * h0i + ai * h0r
        st_ref[:, base:base + S] = hr
        st_ref[:, base + S:base + 2 * S] = hi
        h = jnp.concatenate([hr, hi], axis=1).astype(BF16)
        ys.append(jnp.dot(h, cblk_ref[j], preferred_element_type=F32))
    s_ref[...] = _ssm_tail(jnp.concatenate(ys, axis=1), u, d_ref, wg_ref, bg_ref, gs_ref).astype(s_ref.dtype)

    xp = xp_ref[...]
    ds = []
    for gi, w in enumerate(POOL_WINDOWS):
        gsl = slice(gi * POOL_GROUP, (gi + 1) * POOL_GROUP)
        s = xp[:, gsl]
        for back in range(1, w):
            s = s + pb_ref[POOL_BUF - back][:, gsl]
        ds.append(s / float(w) - xp[:, gsl])
    p_ref[...] = _pool_tail(ds, wp_ref, sc_ref, gp_ref).astype(p_ref.dtype)


def _mix_sample(uh, h0, pw, bblk, cblk, d, wg, l, bg, gs, xph, pbuf, wp, _l, sc, gp, sh, ph):
    N = h0.shape[1]
    rows = lambda w: pl.BlockSpec((N, w), lambda i: (0, 0))
    anyspec = pl.BlockSpec(memory_space=pl.ANY)
    return pl.pallas_call(
        _mix_sample_kernel,
        grid=(1,),
        in_specs=[pl.BlockSpec((SSM_LANE_BLOCKS, N, LANES), lambda i: (0, 0, 0)),
                  _layer(h0.shape[1:], l), _layer(pw.shape[1:], l), _layer(bblk.shape[1:], l),
                  _layer(cblk.shape[1:], l),
                  _layer((1, SSM_WIDTH), l), _layer((SSM_WIDTH, SSM_WIDTH), l), _layer((1, SSM_WIDTH), l),
                  _layer((1, SSM_WIDTH), l), rows(POOL_WIDTH), _layer(pbuf.shape[1:], l),
                  _layer(wp.shape[1:], l),
                  _layer((1, POOL_WIDTH), l), _layer((1, POOL_WIDTH), l), anyspec, anyspec],
        out_specs=[rows(SSM_WIDTH), _full((N, SSM_STATE_LANES)), rows(POOL_WIDTH)],
        out_shape=[jax.ShapeDtypeStruct(sh.shape, sh.dtype), jax.ShapeDtypeStruct((N, SSM_STATE_LANES), F32),
                   jax.ShapeDtypeStruct(ph.shape, ph.dtype)],
        input_output_aliases={14: 0, 15: 2},
        compiler_params=_params("arbitrary"),
        name="mix_sample",
    )(uh, h0, pw, bblk, cblk, d, wg, bg, gs, xph, pbuf, wp, sc, gp, sh, ph)


def _outproj_kernel(x_ref, a_ref, s_ref, p_ref, w_ref, *rest):
    if len(rest) == 3:
        cast_in_ref, o_ref, cast_out_ref = rest
        cast_out_ref[...] = cast_in_ref[...].astype(BF16)
    else:
        (o_ref,) = rest
    o1 = ATTN_WIDTH
    o2 = o1 + SSM_WIDTH
    acc = x_ref[...]
    acc = acc + jnp.dot(a_ref[...], w_ref[0:o1, :], preferred_element_type=F32)
    acc = acc + jnp.dot(s_ref[...], w_ref[o1:o2, :], preferred_element_type=F32)
    acc = acc + jnp.dot(p_ref[...], w_ref[o2:, :], preferred_element_type=F32)
    o_ref[...] = acc


def _outproj(x, a, s, p, w, l, tm, cast=None):
    R = x.shape[0]
    row = lambda i: (i, 0)
    w_spec = pl.BlockSpec((None, D_MODEL, D_MODEL), lambda i: (l, 0, 0), pipeline_mode=pl.Buffered(1))
    in_specs = [pl.BlockSpec((tm, D_MODEL), row), pl.BlockSpec((tm, ATTN_WIDTH), row),
                pl.BlockSpec((tm, SSM_WIDTH), row), pl.BlockSpec((tm, POOL_WIDTH), row), w_spec]
    args = [x, a, s, p, w]
    outs = [(pl.BlockSpec((tm, D_MODEL), row), jax.ShapeDtypeStruct((R, D_MODEL), F32))]
    if cast is not None:
        c_in, c_out, c_shape = _cast_side_job(cast, l, R // tm)
        in_specs.append(c_in)
        args.append(cast)
        outs.append((c_out, c_shape))
    res = pl.pallas_call(
        _outproj_kernel,
        grid=(R // tm,),
        in_specs=in_specs,
        out_specs=[o[0] for o in outs],
        out_shape=[o[1] for o in outs],
        compiler_params=_params("arbitrary"),
        name="outproj",
    )(*args)
    return res if cast is not None else res[0]


def _ffn_kernel(xm_ref, xh_ref, g_ref, w1_ref, w2_ref, om_ref, oh_ref, hm_scr, hh_scr):
    i = pl.program_id(0)
    f = pl.program_id(1)

    @pl.when(f == 0)
    def _():
        x = xm_ref[...]
        hm_scr[...] = _rms(x, g_ref[...]).astype(BF16)
        om_ref[...] = x

    @pl.when((f == 0) & (i == 0))
    def _():
        x = xh_ref[...]
        hh_scr[...] = _rms(x, g_ref[...]).astype(BF16)
        oh_ref[...] = x

    w1 = w1_ref[...]
    w2 = w2_ref[...]

    def mlp(h):
        h1 = jnp.dot(h, w1, preferred_element_type=F32)
        return jnp.dot(jnp.square(jnp.maximum(h1, 0.0)).astype(BF16), w2, preferred_element_type=F32)

    om_ref[...] += mlp(hm_scr[...])

    @pl.when(i == 0)
    def _():
        oh_ref[...] += mlp(hh_scr[...])


def _ffn(xm, xh, g, l, w1, w2, tm, tf):
    R, RH = xm.shape[0], xh.shape[0]
    return pl.pallas_call(
        _ffn_kernel,
        grid=(R // tm, D_FF // tf),
        in_specs=[pl.BlockSpec((tm, D_MODEL), lambda i, f: (i, 0)),
                  pl.BlockSpec((RH, D_MODEL), lambda i, f: (0, 0), pipeline_mode=pl.Buffered(1)),
                  _layer((1, D_MODEL), l),
                  pl.BlockSpec((D_MODEL, tf), lambda i, f: (0, f)),
                  pl.BlockSpec((tf, D_MODEL), lambda i, f: (f, 0))],
        out_specs=[pl.BlockSpec((tm, D_MODEL), lambda i, f: (i, 0)),
                   pl.BlockSpec((RH, D_MODEL), lambda i, f: (0, 0))],
        out_shape=[jax.ShapeDtypeStruct((R, D_MODEL), F32), jax.ShapeDtypeStruct((RH, D_MODEL), F32)],
        scratch_shapes=[pltpu.VMEM((tm, D_MODEL), BF16), pltpu.VMEM((RH, D_MODEL), BF16)],
        compiler_params=_params("arbitrary", "arbitrary"),
        name="ffn",
    )(xm, xh, g, w1, w2)


def _rope_tables(pos):
    half = ROT_DIM // 2
    inv = ROPE_THETA ** (-jnp.arange(0, ROT_DIM, 2, dtype=F32) / ROT_DIM)
    ang = pos.astype(F32)[:, None] * inv
    cos, sin = jnp.cos(ang), jnp.sin(ang)
    n = pos.shape[0]
    z = jnp.zeros((n, HEAD_DIM - ROT_DIM), F32)
    zh = jnp.zeros((n, half), F32)
    rc = jnp.concatenate([cos, cos, z + 1.0], axis=1)
    rs1 = jnp.concatenate([zh, sin, z], axis=1)
    rs2 = jnp.concatenate([-sin, zh, z], axis=1)
    return rc, rs1, rs2


def kernel(x_prompt, x_sample, cache_k, cache_v, state_ssm_re, state_ssm_im, state_pool, meta_tokens, g_mix, w_in, g_q, g_k, sinks, A_re, A_im, log_dt, B_re, B_im, C_re, C_im, D_skip, w_glu, b_glu, w_pool, pool_scale, g_out_attn, g_out_ssm, g_out_pool, w_out, g_ffn, w_ff1, w_ff2):
    B, T, _ = x_prompt.shape
    N = x_sample.shape[0]
    depth = w_in.shape[0]
    assert N <= SEQ0 and N % DEC_STEP == 0 and T % TM_FFN == 0
    meta = meta_tokens.astype(F32)
    head0 = jnp.concatenate([x_sample.reshape(N, D_MODEL), jnp.zeros((SEQ0 - N, D_MODEL), F32), meta], axis=0)
    head_rest = jnp.concatenate([jnp.zeros((SEQ0, D_MODEL), F32), meta], axis=0)
    xh = jnp.concatenate([head0] + [head_rest] * (B - 1), axis=0)
    xm = x_prompt.reshape(B * T, D_MODEL)

    rope_m = _rope_tables(N_META + jnp.arange(T))
    hr_ = jnp.arange(HEAD_ROWS)
    pos_h0 = jnp.where(hr_ < N, PAST_LEN, jnp.maximum(hr_ - SEQ0, 0))
    pos_h = jnp.concatenate([pos_h0] + [jnp.maximum(hr_ - SEQ0, 0)] * (B - 1))
    rope_h = _rope_tables(pos_h)

    wi_all, wo_all = w_in.astype(BF16), w_out.astype(BF16)
    wg_all, wp_all = w_glu.astype(BF16), w_pool.astype(BF16)
    ck = cache_k.astype(F32)
    cv = cache_v.astype(F32)
    RH = B * HEAD_ROWS

    vec = lambda t: t.astype(F32).reshape(depth, 1, -1)
    pw, bblk, cblk = _ssm_params(A_re, A_im, log_dt, B_re, B_im, C_re, C_im)
    ssm_w = lambda l: (pw, bblk, cblk, vec(D_skip), wg_all, l, vec(b_glu), vec(g_out_ssm))
    pool_w = lambda l: (wp_all, l, vec(pool_scale), vec(g_out_pool))
    in_w = lambda l: (vec(g_mix), wi_all, l, vec(g_q), vec(g_k))
    g_attn = vec(g_out_attn)
    g_ffn_ = vec(g_ffn)
    sinks_flat = sinks.astype(F32).reshape(depth * N_HEADS)
    sinks_col = sinks.astype(F32).reshape(depth, N_KV_HEADS, GQA_GROUP, 1)
    bias = _attn_bias()
    h0 = jnp.concatenate([state_ssm_re.astype(F32).reshape(depth, N, SSM_LANE_BLOCKS, SSM_BLOCK_STATES),
                          state_ssm_im.astype(F32).reshape(depth, N, SSM_LANE_BLOCKS, SSM_BLOCK_STATES)],
                         axis=-1).reshape(depth, N, SSM_STATE_LANES)
    pbuf = state_pool.astype(F32).transpose(0, 2, 1, 3)

    ks, vs, pls, sts, nks, nvs, st_ss, phs = ([] for _ in range(8))
    for l in range(depth):
        qm, km, vm, um, pm, w1 = _inproj(xm, *in_w(l), rope_m, TM_PROJ, cast=w_ff1)
        qh, kh, vh, uh, ph = _inproj(xh, *in_w(l), rope_h, RH)

        am, ah = _attn(sinks_flat, l, qm, qh, km, kh, vm, vh, g_attn, bias, B)
        sm, sh, st = _ssm(um, uh, *ssm_w(l), B, TM_SEQ)
        plm, plh = _pool(pm, ph, *pool_w(l), B, TM_SEQ)

        ah, nk, nv = _attn_sample(qh, kh, vh, ck, cv, l, sinks_col, g_attn, ah, N)
        sh, st_s, plh = _mix_sample(uh, h0, *ssm_w(l), ph, pbuf, *pool_w(l), sh, plh)

        xm, w2 = _outproj(xm, am, sm, plm, wo_all, l, TM_PROJ, cast=w_ff2)
        xh = _outproj(xh, ah, sh, plh, wo_all, l, RH)
        xm, xh = _ffn(xm, xh, g_ffn_, l, w1, w2, TM_FFN, TF_FFN)

        ks.append(km.reshape(B, T, KV_WIDTH)[:, T - WINDOW:])
        vs.append(vm.reshape(B, T, KV_WIDTH)[:, T - WINDOW:])
        pls.append(pm.reshape(B, T, POOL_WIDTH)[:, T - POOL_BUF:])
        sts.append(st[:, 0])
        nks.append(nk)
        nvs.append(nv)
        st_ss.append(st_s)
        phs.append(ph[:N])

    y_prompt = xm.reshape(B, T, D_MODEL)
    y_sample = xh[:N].reshape(N, 1, D_MODEL)
    heads = lambda t: jnp.stack(t).reshape(depth, -1, WINDOW, N_KV_HEADS, HEAD_DIM)
    p_re, p_im = _state_from_lanes(jnp.stack(sts).reshape(depth * B, SSM_STATE_LANES))
    s_re, s_im = _state_from_lanes(jnp.stack(st_ss).reshape(depth * N, SSM_STATE_LANES))
    st4 = lambda t, n: t.reshape(depth, n, SSM_GROUPS, SSM_STATE)
    s_pool = jnp.concatenate([state_pool.astype(F32)[:, :, 1:], jnp.stack(phs)[:, :, None]], axis=2)
    return (y_prompt, y_sample, heads(ks), heads(vs), st4(p_re, B), st4(p_im, B), jnp.stack(pls),
            jnp.stack(nks), jnp.stack(nvs), st4(s_re, N), st4(s_im, N), s_pool)
```

```python
import functools
import math

import jax
import jax.numpy as jnp
import numpy as np
from jax.experimental import pallas as pl
from jax.experimental.pallas import tpu as pltpu

D_MODEL = 2048
N_META = 16
HEAD_DIM = 128
N_HEADS = 8
N_KV_HEADS = 2
GQA_GROUP = 4
ATTN_WIDTH = 1024
KV_WIDTH = 256
WINDOW = 128
BLOCK = 128
ROT_DIM = 32
ROPE_THETA = 500000.0
SSM_WIDTH = 512
SSM_GROUP_SIZE = 16
SSM_GROUPS = 32
SSM_STATE = 64
POOL_WIDTH = 512
POOL_WINDOWS = (2, 4, 8, 16)
POOL_GROUP = 128
POOL_BUF = 15
POOL_HALO = 16
IN_WIDTH = 2560
D_FF = 8192
EPS = 1e-6
PAST_LEN = 16384
LOG2E = math.log2(math.e)

HEAD_ROWS = BLOCK
SEQ0 = HEAD_ROWS - N_META
LANES = 128
SUBLANES = 8
SSM_LANE_BLOCKS = SSM_WIDTH // LANES
SSM_BLOCK_STATES = (LANES // SSM_GROUP_SIZE) * SSM_STATE
SSM_STATE_LANES = SSM_LANE_BLOCKS * 2 * SSM_BLOCK_STATES
VMEM_LIMIT = 56 * 1024 * 1024

TM_PROJ = 512
TM_FFN = 1024
TF_FFN = 512
TM_SEQ = 512
DEC_STEP = 8
SSM_POW_ROWS = (1, HEAD_ROWS // SUBLANES, TM_SEQ // SUBLANES)

BF16 = jnp.bfloat16
F32 = jnp.float32


def _params(*semantics):
    return pltpu.CompilerParams(dimension_semantics=semantics, vmem_limit_bytes=VMEM_LIMIT)


def _rms(x, g):
    return x * jax.lax.rsqrt(jnp.mean(x * x, axis=-1, keepdims=True) + EPS) * g


def _full(shape):
    n = len(shape)
    return pl.BlockSpec(shape, lambda *_: (0,) * n)


def _layer(shape, l):
    n = len(shape)
    return pl.BlockSpec((None, *shape), lambda *_: (l,) + (0,) * n)


VEC_LAYOUT = (("g_mix", D_MODEL), ("g_ffn", D_MODEL), ("g_out_attn", ATTN_WIDTH), ("D_skip", SSM_WIDTH),
              ("b_glu", SSM_WIDTH), ("g_out_ssm", SSM_WIDTH), ("pool_scale", POOL_WIDTH),
              ("g_out_pool", POOL_WIDTH), ("g_q", HEAD_DIM), ("g_k", HEAD_DIM))
VEC_WIDTH = dict(VEC_LAYOUT)
VEC_OFFSET = {name: sum(w for _, w in VEC_LAYOUT[:i]) for i, (name, _) in enumerate(VEC_LAYOUT)}
assert all(VEC_OFFSET[name] % w == 0 for name, w in VEC_LAYOUT)


def _vrow(name, l):
    w = VEC_WIDTH[name]
    return pl.BlockSpec((None, 1, w), lambda *_: (l, 0, VEC_OFFSET[name] // w))


def _inproj_kernel(x_ref, g_ref, w_ref, gq_ref, gk_ref, rc_ref, rs1_ref, rs2_ref, *rest):
    if len(rest) == 7:
        cast_in_ref, q_ref, k_ref, v_ref, u_ref, xp_ref, cast_out_ref = rest
        cast_out_ref[...] = cast_in_ref[...].astype(BF16)
    else:
        q_ref, k_ref, v_ref, u_ref, xp_ref = rest
    h = _rms(x_ref[...], g_ref[...]).astype(BF16)
    proj = jnp.dot(h, w_ref[...], preferred_element_type=F32)
    rc, rs1, rs2 = rc_ref[...], rs1_ref[...], rs2_ref[...]

    def head(t, g):
        t = _rms(t, g)
        return t * rc + pltpu.roll(t, 16, 1) * rs1 + pltpu.roll(t, LANES - 16, 1) * rs2

    for hd in range(N_HEADS):
        sl = slice(hd * HEAD_DIM, (hd + 1) * HEAD_DIM)
        q_ref[:, sl] = head(proj[:, sl], gq_ref[...])
    for hd in range(N_KV_HEADS):
        sl = slice(hd * HEAD_DIM, (hd + 1) * HEAD_DIM)
        k_ref[:, sl] = head(proj[:, ATTN_WIDTH + hd * HEAD_DIM:ATTN_WIDTH + (hd + 1) * HEAD_DIM], gk_ref[...])
    o2 = ATTN_WIDTH + KV_WIDTH
    o3 = o2 + KV_WIDTH
    o4 = o3 + SSM_WIDTH
    v_ref[...] = proj[:, o2:o3]
    for j in range(SSM_LANE_BLOCKS):
        u_ref[j] = proj[:, o3 + j * LANES:o3 + (j + 1) * LANES]
    xp_ref[...] = proj[:, o4:]


def _cast_side_job(w, l, steps):
    _, r, c = w.shape
    rows = r // steps
    return (pl.BlockSpec((None, rows, c), lambda i: (l, i, 0)), pl.BlockSpec((rows, c), lambda i: (i, 0)),
            jax.ShapeDtypeStruct((r, c), BF16))


def _inproj(x, g, w, l, gq, gk, rope, tm, cast=None):
    R = x.shape[0]
    tiles_per_rope = rope[0].shape[0] // tm
    row = lambda i: (i, 0)
    rrow = lambda i: (i % tiles_per_rope, 0)
    flat = lambda w_: (pl.BlockSpec((tm, w_), row), jax.ShapeDtypeStruct((R, w_), F32))
    u_out = (pl.BlockSpec((SSM_LANE_BLOCKS, tm, LANES), lambda i: (0, i, 0)),
             jax.ShapeDtypeStruct((SSM_LANE_BLOCKS, R, LANES), F32))
    outs = [flat(ATTN_WIDTH), flat(KV_WIDTH), flat(KV_WIDTH), u_out, flat(POOL_WIDTH)]
    w_spec = pl.BlockSpec((None, D_MODEL, IN_WIDTH), lambda i: (l, 0, 0), pipeline_mode=pl.Buffered(1))
    in_specs = [pl.BlockSpec((tm, D_MODEL), row), _vrow("g_mix", l), w_spec,
                _vrow("g_q", l), _vrow("g_k", l),
                pl.BlockSpec((tm, LANES), rrow), pl.BlockSpec((tm, LANES), rrow),
                pl.BlockSpec((tm, LANES), rrow)]
    args = [x, g, w, gq, gk, *rope]
    if cast is not None:
        c_in, c_out, c_shape = _cast_side_job(cast, l, R // tm)
        in_specs.append(c_in)
        args.append(cast)
        outs.append((c_out, c_shape))
    return pl.pallas_call(
        _inproj_kernel,
        grid=(R // tm,),
        in_specs=in_specs,
        out_specs=[o[0] for o in outs],
        out_shape=[o[1] for o in outs],
        compiler_params=_params("arbitrary"),
        name="inproj",
    )(*args)


def _attn_bias():
    rows = GQA_GROUP * BLOCK
    i, r, c = np.meshgrid(np.arange(3), np.arange(rows) % BLOCK, np.arange(2 * BLOCK), indexing="ij")
    diff = BLOCK + r - c
    krow = (i - 1) * BLOCK + c
    mask = (diff >= 0) & (diff <= WINDOW) & (krow >= SEQ0)
    return np.where(mask, 0.0, -np.inf).astype(np.float32)


def _attn_block(q_blk, kp_blk, kc_blk, vp_blk, vc_blk, bias, sink_ref, l, g):
    rows = GQA_GROUP * BLOCK
    rgrp = jax.lax.broadcasted_iota(jnp.int32, (rows, 1), 0) // BLOCK
    outs = []
    for kh in range(N_KV_HEADS):
        ksl = slice(kh * HEAD_DIM, (kh + 1) * HEAD_DIM)
        qh = jnp.concatenate(
            [q_blk[:, (kh * GQA_GROUP + h) * HEAD_DIM:(kh * GQA_GROUP + h + 1) * HEAD_DIM]
             for h in range(GQA_GROUP)], axis=0).astype(BF16)
        kk = jnp.concatenate([kp_blk[:, ksl], kc_blk[:, ksl]], axis=0).astype(BF16)
        vv = jnp.concatenate([vp_blk[:, ksl], vc_blk[:, ksl]], axis=0).astype(BF16)
        s = jax.lax.dot_general(qh, kk, (((1,), (1,)), ((), ())),
                                preferred_element_type=F32) * (HEAD_DIM ** -0.5 * LOG2E) + bias
        sk = jnp.zeros((rows, 1), F32)
        for h in range(GQA_GROUP):
            sk = jnp.where(rgrp == h, sink_ref[l * N_HEADS + kh * GQA_GROUP + h] * LOG2E, sk)
        m = jnp.maximum(jnp.max(s, axis=-1, keepdims=True), sk)
        p = jnp.exp2(s - m)
        denom = jnp.sum(p, axis=-1, keepdims=True) + jnp.exp2(sk - m)
        o = jnp.dot(p.astype(BF16), vv, preferred_element_type=F32) / denom
        outs.extend(o[h * BLOCK:(h + 1) * BLOCK] for h in range(GQA_GROUP))
    return _rms(jnp.concatenate(outs, axis=1), g)


def _attn_kernel(sink_ref, qm_ref, qh_ref, kpm_ref, kcm_ref, kh_ref, vpm_ref, vcm_ref, vh_ref, g_ref,
                 bias_a_ref, bias_b_ref, om_ref, oh_ref, *, l):
    s = pl.program_id(1)
    blk = functools.partial(_attn_block, sink_ref=sink_ref, l=l, g=g_ref[...])

    @pl.when(s == 0)
    def _():
        a = blk(qh_ref[...], kpm_ref[...], kh_ref[...], vpm_ref[...], vh_ref[...], bias_a_ref[...])
        oh_ref[...] = a.astype(oh_ref.dtype)

    @pl.when(s > 0)
    def _():
        k1, k2 = kcm_ref[0:BLOCK, :], kcm_ref[BLOCK:2 * BLOCK, :]
        v1, v2 = vcm_ref[0:BLOCK, :], vcm_ref[BLOCK:2 * BLOCK, :]
        kp = jnp.where(s == 1, kh_ref[...], kpm_ref[...])
        vp = jnp.where(s == 1, vh_ref[...], vpm_ref[...])
        a1 = blk(qm_ref[0:BLOCK, :], kp, k1, vp, v1, bias_a_ref[...])
        a2 = blk(qm_ref[BLOCK:2 * BLOCK, :], k1, k2, v1, v2, bias_b_ref[...])
        om_ref[0:BLOCK, :] = a1.astype(om_ref.dtype)
        om_ref[BLOCK:2 * BLOCK, :] = a2.astype(om_ref.dtype)


def _attn(sinks, l, qm, qh, km, kh, vm, vh, g, bias, B):
    nb = qm.shape[0] // (B * BLOCK)
    npair = nb // 2
    pair = lambda b, s, _: (b * npair + jnp.maximum(s - 1, 0), 0)
    prev = lambda b, s, _: (b * nb + jnp.maximum(2 * s - 3, 0), 0)
    head = lambda b, s, _: (b, 0)
    one = lambda w, im: pl.BlockSpec((BLOCK, w), im)
    two = lambda w: pl.BlockSpec((2 * BLOCK, w), pair)
    bias_spec = lambda im: pl.BlockSpec((None,) + bias.shape[1:], im)
    return pl.pallas_call(
        functools.partial(_attn_kernel, l=l),
        grid_spec=pltpu.PrefetchScalarGridSpec(
            num_scalar_prefetch=1,
            grid=(B, npair + 1),
            in_specs=[two(ATTN_WIDTH), one(ATTN_WIDTH, head),
                      one(KV_WIDTH, prev), two(KV_WIDTH), one(KV_WIDTH, head),
                      one(KV_WIDTH, prev), two(KV_WIDTH), one(KV_WIDTH, head),
                      _vrow("g_out_attn", l),
                      bias_spec(lambda b, s, _: (jnp.minimum(s, 2), 0, 0)),
                      bias_spec(lambda b, s, _: (2, 0, 0))],
            out_specs=[two(ATTN_WIDTH), one(ATTN_WIDTH, head)],
        ),
        out_shape=[jax.ShapeDtypeStruct(qm.shape, BF16), jax.ShapeDtypeStruct(qh.shape, BF16)],
        compiler_params=_params("arbitrary", "arbitrary"),
        name="attn_prompt",
    )(sinks, qm, qh, km, km, kh, vm, vm, vh, g, bias, bias)


def _ssm_params_kernel(ar_ref, ai_ref, ldt_ref, kk_ref, br_ref, bi_ref, tr_ref, ti_ref, bbr_ref, bbi_ref):
    ar, ai = ar_ref[...], ai_ref[...]
    dt = jnp.exp(ldt_ref[...])
    kk = kk_ref[...]
    mag = jnp.exp(dt * ar * kk)
    ang = dt * ai * kk
    tr = mag * jnp.cos(ang)
    ti = mag * jnp.sin(ang)
    tr_ref[...] = tr
    ti_ref[...] = ti
    abr, abi = tr[:, 0:1], ti[:, 0:1]
    den = ar * ar + ai * ai
    fr = ((abr - 1.0) * ar + abi * ai) / den
    fi = (abi * ar - (abr - 1.0) * ai) / den
    br, bi = br_ref[...], bi_ref[...]
    bbr_ref[...] = fr * br - fi * bi
    bbi_ref[...] = fr * bi + fi * br


def _ssm_params(A_re, A_im, log_dt, B_re, B_im, C_re, C_im):
    depth = A_re.shape[0]
    n = depth * SSM_GROUPS * SSM_STATE
    col = lambda t: t.astype(F32).reshape(n, 1)
    ldt = jnp.broadcast_to(log_dt.astype(F32)[:, :, None], (depth, SSM_GROUPS, SSM_STATE)).reshape(n, 1)
    kk = jnp.array([SSM_POW_ROWS + (0,) * (SUBLANES - len(SSM_POW_ROWS))], F32)
    shapes = [(n, SUBLANES), (n, SUBLANES), (n, SSM_GROUP_SIZE), (n, SSM_GROUP_SIZE)]
    tr, ti, bbr, bbi = pl.pallas_call(
        _ssm_params_kernel,
        out_shape=[jax.ShapeDtypeStruct(s, F32) for s in shapes],
        name="ssm_params",
    )(col(A_re), col(A_im), ldt, kk, B_re.astype(F32).reshape(n, SSM_GROUP_SIZE),
      B_im.astype(F32).reshape(n, SSM_GROUP_SIZE))
    J, G8 = SSM_LANE_BLOCKS, LANES // SSM_GROUP_SIZE

    def lanes(t):
        return t.reshape(depth, J, SSM_BLOCK_STATES, SUBLANES).transpose(0, 3, 1, 2)

    pw = jnp.concatenate([lanes(tr), lanes(ti)], axis=-1).reshape(depth, SUBLANES, SSM_STATE_LANES)
    eye = jnp.eye(G8, dtype=F32)

    def bdiag(t):
        t = t.reshape(depth, J, G8, SSM_STATE, SSM_GROUP_SIZE).transpose(0, 1, 2, 4, 3)
        t = t[:, :, :, :, None, :] * eye[None, None, :, None, :, None]
        return t.reshape(depth, J, LANES, SSM_BLOCK_STATES)

    bblk = jnp.concatenate([bdiag(bbr), bdiag(bbi)], axis=-1).astype(BF16)

    def cdiag(t):
        t = t.astype(F32).reshape(depth, J, G8, SSM_GROUP_SIZE, SSM_STATE).transpose(0, 1, 2, 4, 3)
        t = t[:, :, :, :, None, :] * eye[None, None, :, None, :, None]
        return t.reshape(depth, J, SSM_BLOCK_STATES, LANES)

    cblk = jnp.concatenate([cdiag(C_re), -cdiag(C_im)], axis=2).astype(BF16)
    return pw, bblk, cblk


def _ssm_tail(y, u, d_ref, wg_ref, bg_ref, g_ref):
    y = y + d_ref[...] * u
    z = jax.nn.gelu(y)
    gate = jax.nn.sigmoid(jnp.dot(z.astype(BF16), wg_ref[...], preferred_element_type=F32) + bg_ref[...])
    return _rms(z * gate, g_ref[...])


def _ssm_sweep(x_scr, n, a_tabs, init, store):
    S = SSM_BLOCK_STATES
    fins = []
    for j0 in range(0, SSM_LANE_BLOCKS, 2):
        js = (j0, j0 + 1)

        def body(k, carry, js=js):
            r0 = pl.multiple_of(k * SUBLANES, SUBLANES)
            out = []
            for idx, j in enumerate(js):
                hr, hi = carry[2 * idx], carry[2 * idx + 1]
                base = j * 2 * S
                ar, ai = a_tabs[j]
                nhr = ar * hr - ai * hi + x_scr[pl.ds(r0, SUBLANES), base:base + S]
                nhi = ar * hi + ai * hr + x_scr[pl.ds(r0, SUBLANES), base + S:base + 2 * S]
                if store:
                    x_scr[pl.ds(r0, SUBLANES), base:base + S] = nhr
                    x_scr[pl.ds(r0, SUBLANES), base + S:base + 2 * S] = nhi
                out += [nhr, nhi]
            return tuple(out)

        c0 = tuple(t for j in js for t in init[j])
        res = jax.lax.fori_loop(0, n // SUBLANES, body, c0, unroll=2)
        fins += [(res[0], res[1]), (res[2], res[3])]
    return fins


def _ssm_rows(u, pow_row, pw_ref, bblk_ref, cblk_ref, x_scr, s_scr, carry_scr):
    n = u.shape[0]
    S = SSM_BLOCK_STATES
    ub = u.astype(BF16)
    for j in range(SSM_LANE_BLOCKS):
        x_scr[0:n, j * 2 * S:(j + 1) * 2 * S] = jnp.dot(ub[:, j * LANES:(j + 1) * LANES], bblk_ref[j],
                                                       preferred_element_type=F32)
    bc = lambda t: jnp.broadcast_to(t, (SUBLANES, S))
    re = lambda ref, r0, r1, j: ref[r0:r1, j * 2 * S:j * 2 * S + S]
    im = lambda ref, r0, r1, j: ref[r0:r1, j * 2 * S + S:(j + 1) * 2 * S]
    a_tabs = [(bc(re(pw_ref, 0, 1, j)), bc(im(pw_ref, 0, 1, j))) for j in range(SSM_LANE_BLOCKS)]
    zero = jnp.zeros((SUBLANES, S), F32)
    fins = _ssm_sweep(x_scr, n, a_tabs, [(zero, zero)] * SSM_LANE_BLOCKS, store=False)
    for j in range(SSM_LANE_BLOCKS):
        base = j * 2 * S
        cr, ci = re(pw_ref, pow_row, pow_row + 1, j), im(pw_ref, pow_row, pow_row + 1, j)
        sr, si = re(carry_scr, 0, 1, j), im(carry_scr, 0, 1, j)
        fr, fi = fins[j]
        for c in range(SUBLANES):
            s_scr[c:c + 1, base:base + S] = sr
            s_scr[c:c + 1, base + S:base + 2 * S] = si
            sr, si = cr * sr - ci * si + fr[c:c + 1], cr * si + ci * sr + fi[c:c + 1]
        carry_scr[:, base:base + S] = bc(sr)
        carry_scr[:, base + S:base + 2 * S] = bc(si)
    init = [(re(s_scr, 0, SUBLANES, j), im(s_scr, 0, SUBLANES, j)) for j in range(SSM_LANE_BLOCKS)]
    _ssm_sweep(x_scr, n, a_tabs, init, store=True)
    ys = [jnp.dot(x_scr[0:n, j * 2 * S:(j + 1) * 2 * S].astype(BF16), cblk_ref[j], preferred_element_type=F32)
          for j in range(SSM_LANE_BLOCKS)]
    return jnp.concatenate(ys, axis=1)


def _ssm_tile(u_ref, n, pow_row, seq_start, refs, o_ref, scr):
    pw_ref, bblk_ref, cblk_ref, d_ref, wg_ref, bg_ref, g_ref = refs
    up_scr, x_scr, s_scr, carry_scr, o_scr = scr
    q = n // SUBLANES
    for j in range(SSM_LANE_BLOCKS):
        for k in range(q):
            up_scr[k * SUBLANES:(k + 1) * SUBLANES, j * LANES:(j + 1) * LANES] = \
                u_ref[j, pl.ds(k, SUBLANES, stride=q), :]
    u = up_scr[0:n, :]
    if seq_start:
        p = jax.lax.broadcasted_iota(jnp.int32, (n, 1), 0)
        u = jnp.where((p % SUBLANES) * q + p // SUBLANES >= seq_start, u, 0.0)
    y = _ssm_rows(u, pow_row, pw_ref, bblk_ref, cblk_ref, x_scr, s_scr, carry_scr)
    out = _ssm_tail(y, u, d_ref, wg_ref, bg_ref, g_ref)
    for j in range(SSM_LANE_BLOCKS):
        for k in range(q):
            o_scr[j, pl.ds(k, SUBLANES, stride=q), :] = out[k * SUBLANES:(k + 1) * SUBLANES,
                                                            j * LANES:(j + 1) * LANES]
    o_ref[...] = jnp.concatenate([o_scr[j, 0:n, :] for j in range(SSM_LANE_BLOCKS)], axis=1).astype(o_ref.dtype)


def _ssm_kernel(um_ref, uh_ref, pw_ref, bblk_ref, cblk_ref, d_ref, wg_ref, bg_ref, g_ref, cast_in_ref,
                om_ref, oh_ref, st_ref, cast_out_ref, up_scr, x_scr, s_scr, carry_scr, o_scr):
    t = pl.program_id(1)
    refs = (pw_ref, bblk_ref, cblk_ref, d_ref, wg_ref, bg_ref, g_ref)
    scr = (up_scr, x_scr, s_scr, carry_scr, o_scr)

    @pl.when(t == 0)
    def _():
        carry_scr[...] = jnp.zeros_like(carry_scr)
        _ssm_tile(uh_ref, HEAD_ROWS, 1, SEQ0, refs, oh_ref, scr)

    @pl.when(t > 0)
    def _():
        cast_out_ref[...] = cast_in_ref[...].astype(BF16)
        _ssm_tile(um_ref, um_ref.shape[1], 2, 0, refs, om_ref, scr)
        st_ref[...] = carry_scr[...]


def _ssm(um, uh, pw, bblk, cblk, d, wg, l, bg, g, B, tm, cast):
    assert (1, HEAD_ROWS // SUBLANES, tm // SUBLANES) == SSM_POW_ROWS
    J = SSM_LANE_BLOCKS
    rm, rh = um.shape[1], uh.shape[1]
    nt = rm // (B * tm)
    tile = lambda b, t: b * nt + jnp.maximum(t - 1, 0)
    main = lambda b, t: (tile(b, t), 0)
    head = lambda b, t: (b, 0)
    _, cr, cc = cast.shape
    crows = cr // (B * nt)
    return pl.pallas_call(
        _ssm_kernel,
        grid=(B, nt + 1),
        in_specs=[pl.BlockSpec((J, tm, LANES), lambda b, t: (0, tile(b, t), 0)),
                  pl.BlockSpec((J, HEAD_ROWS, LANES), lambda b, t: (0, b, 0)),
                  _layer(pw.shape[1:], l), _layer(bblk.shape[1:], l), _layer(cblk.shape[1:], l),
                  _vrow("D_skip", l), _layer((SSM_WIDTH, SSM_WIDTH), l), _vrow("b_glu", l),
                  _vrow("g_out_ssm", l),
                  pl.BlockSpec((None, crows, cc), lambda b, t: (l, tile(b, t), 0))],
        out_specs=[pl.BlockSpec((tm, SSM_WIDTH), main), pl.BlockSpec((HEAD_ROWS, SSM_WIDTH), head),
                   pl.BlockSpec((None, SUBLANES, SSM_STATE_LANES), lambda b, t: (b, 0, 0)),
                   pl.BlockSpec((crows, cc), main)],
        out_shape=[jax.ShapeDtypeStruct((rm, SSM_WIDTH), BF16), jax.ShapeDtypeStruct((rh, SSM_WIDTH), BF16),
                   jax.ShapeDtypeStruct((B, SUBLANES, SSM_STATE_LANES), F32),
                   jax.ShapeDtypeStruct((cr, cc), BF16)],
        scratch_shapes=[pltpu.VMEM((tm, SSM_WIDTH), F32),
                        pltpu.VMEM((tm, SSM_STATE_LANES), F32),
                        pltpu.VMEM((SUBLANES, SSM_STATE_LANES), F32),
                        pltpu.VMEM((SUBLANES, SSM_STATE_LANES), F32),
                        pltpu.VMEM((J, tm, LANES), F32)],
        compiler_params=_params("arbitrary", "arbitrary"),
        name="ssm_prompt",
    )(um, uh, pw, bblk, cblk, d, wg, bg, g, cast)


def _state_from_lanes(s):
    s = s.reshape(s.shape[0], SSM_LANE_BLOCKS, 2, SSM_BLOCK_STATES)
    return (s[:, :, 0].reshape(-1, SSM_GROUPS, SSM_STATE), s[:, :, 1].reshape(-1, SSM_GROUPS, SSM_STATE))


def _pool_tail(d_groups, w_ref, sc_ref, g_ref):
    y = jnp.concatenate(
        [jnp.dot(d.astype(BF16), w_ref[gi], preferred_element_type=F32) for gi, d in enumerate(d_groups)], axis=1)
    return _rms(y * sc_ref[...], g_ref[...])


def _pool_rows(x, prev, pos0, w_ref, sc_ref, g_ref):
    n = x.shape[0]
    xe = jnp.concatenate([prev, x], axis=0)
    pos = pos0 + jax.lax.broadcasted_iota(jnp.int32, (n, 1), 0)
    ds = []
    for gi, w in enumerate(POOL_WINDOWS):
        gsl = slice(gi * POOL_GROUP, (gi + 1) * POOL_GROUP)
        s = xe[:, gsl]
        k = 1
        while k < w:
            s = s + pltpu.roll(s, k, 0)
            k *= 2
        cnt = jnp.clip(pos + 1, 1, w).astype(F32)
        ds.append(s[POOL_HALO:] / cnt - x[:, gsl])
    return _pool_tail(ds, w_ref, sc_ref, g_ref)


def _pool_kernel(xm_ref, halo_ref, xh_ref, w_ref, sc_ref, g_ref, om_ref, oh_ref):
    t = pl.program_id(1)
    tm = xm_ref.shape[0]

    @pl.when(t == 0)
    def _():
        hrow = jax.lax.broadcasted_iota(jnp.int32, (HEAD_ROWS, 1), 0)
        x = jnp.where(hrow >= SEQ0, xh_ref[...], 0.0)
        prev = jnp.zeros((POOL_HALO, POOL_WIDTH), F32)
        oh_ref[...] = _pool_rows(x, prev, -SEQ0, w_ref, sc_ref, g_ref).astype(oh_ref.dtype)

    @pl.when(t > 0)
    def _():
        prev = jnp.where(t == 1, xh_ref[HEAD_ROWS - POOL_HALO:, :], halo_ref[...])
        om_ref[...] = _pool_rows(xm_ref[...], prev, N_META + (t - 1) * tm, w_ref, sc_ref, g_ref).astype(om_ref.dtype)


def _pool(xm, xh, w, l, sc, g, B, tm):
    nt = xm.shape[0] // (B * tm)
    r = tm // POOL_HALO
    main = lambda b, t: (b * nt + jnp.maximum(t - 1, 0), 0)
    halo = lambda b, t: (jnp.maximum((b * nt + t - 1) * r - 1, 0), 0)
    head = lambda b, t: (b, 0)
    return pl.pallas_call(
        _pool_kernel,
        grid=(B, nt + 1),
        in_specs=[pl.BlockSpec((tm, POOL_WIDTH), main), pl.BlockSpec((POOL_HALO, POOL_WIDTH), halo),
                  pl.BlockSpec((HEAD_ROWS, POOL_WIDTH), head),
                  _layer(w.shape[1:], l), _vrow("pool_scale", l), _vrow("g_out_pool", l)],
        out_specs=[pl.BlockSpec((tm, POOL_WIDTH), main), pl.BlockSpec((HEAD_ROWS, POOL_WIDTH), head)],
        out_shape=[jax.ShapeDtypeStruct(xm.shape, BF16), jax.ShapeDtypeStruct(xh.shape, BF16)],
        compiler_params=_params("arbitrary", "arbitrary"),
        name="pool_prompt",
    )(xm, xm, xh, w, sc, g)


def _attn_sample_kernel(q_ref, kn_ref, vn_ref, kc_ref, vc_ref, sink_ref, g_ref, a_in_ref,
                        a_ref, nk_ref, nv_ref, acc_scr):
    del a_in_ref
    step = pl.program_id(0)
    scale = HEAD_DIM ** -0.5
    for bb in range(DEC_STEP):
        nk_ref[bb, 0:WINDOW - 1] = kc_ref[bb, 1:WINDOW]
        nv_ref[bb, 0:WINDOW - 1] = vc_ref[bb, 1:WINDOW]
        outs = []
        for kh in range(N_KV_HEADS):
            ksl = slice(kh * HEAD_DIM, (kh + 1) * HEAD_DIM)
            qh = jnp.concatenate(
                [q_ref[bb:bb + 1, (kh * GQA_GROUP + g) * HEAD_DIM:(kh * GQA_GROUP + g + 1) * HEAD_DIM]
                 for g in range(GQA_GROUP)], axis=0)
            kn = kn_ref[bb:bb + 1, ksl]
            vn = vn_ref[bb:bb + 1, ksl]
            nk_ref[bb, WINDOW - 1, kh:kh + 1, :] = kn
            nv_ref[bb, WINDOW - 1, kh:kh + 1, :] = vn
            kc = kc_ref[bb, :, kh, :]
            vc = vc_ref[bb, :, kh, :]
            sc = jax.lax.dot_general(qh.astype(BF16), kc.astype(BF16), (((1,), (1,)), ((), ())),
                                     preferred_element_type=F32) * scale
            sn = jnp.sum(qh.astype(BF16).astype(F32) * kn.astype(BF16).astype(F32), axis=-1, keepdims=True) * scale
            sk = sink_ref[kh]
            m = jnp.maximum(jnp.maximum(jnp.max(sc, axis=-1, keepdims=True), sn), sk)
            pc = jnp.exp(sc - m)
            pn = jnp.exp(sn - m)
            denom = jnp.sum(pc, axis=-1, keepdims=True) + pn + jnp.exp(sk - m)
            o = jnp.dot(pc.astype(BF16), vc.astype(BF16), preferred_element_type=F32)
            o = (o + pn.astype(BF16).astype(F32) * vn.astype(BF16).astype(F32)) / denom
            outs.extend(o[g:g + 1] for g in range(GQA_GROUP))
        a = jnp.concatenate(outs, axis=1)
        acc_scr[pl.ds(step * DEC_STEP + bb, 1), :] = _rms(a, g_ref[...])

    @pl.when(step == pl.num_programs(0) - 1)
    def _():
        a_ref[...] = acc_scr[...].astype(a_ref.dtype)


def _attn_sample(qh, kh, vh, cache_k, cache_v, l, sinks, g, ah, N):
    rows = lambda w: pl.BlockSpec((DEC_STEP, w), lambda s: (s, 0))
    cache = pl.BlockSpec((None, DEC_STEP, WINDOW, N_KV_HEADS, HEAD_DIM), lambda s: (l, s, 0, 0, 0))
    ncache = pl.BlockSpec((DEC_STEP, WINDOW, N_KV_HEADS, HEAD_DIM), lambda s: (s, 0, 0, 0))
    cshape = jax.ShapeDtypeStruct((N, WINDOW, N_KV_HEADS, HEAD_DIM), F32)
    return pl.pallas_call(
        _attn_sample_kernel,
        grid=(N // DEC_STEP,),
        in_specs=[rows(ATTN_WIDTH), rows(KV_WIDTH), rows(KV_WIDTH), cache, cache,
                  _layer((N_KV_HEADS, GQA_GROUP, 1), l), _vrow("g_out_attn", l),
                  pl.BlockSpec(memory_space=pl.ANY)],
        out_specs=[pl.BlockSpec((N, ATTN_WIDTH), lambda s: (0, 0)), ncache, ncache],
        out_shape=[jax.ShapeDtypeStruct(ah.shape, ah.dtype), cshape, cshape],
        scratch_shapes=[pltpu.VMEM((N, ATTN_WIDTH), F32)],
        input_output_aliases={7: 0},
        compiler_params=_params("arbitrary"),
        name="attn_sample",
    )(qh, kh, vh, cache_k, cache_v, sinks, g, ah)


def _mix_sample_kernel(u_ref, h0_ref, pw_ref, bblk_ref, cblk_ref, d_ref, wg_ref, bg_ref, gs_ref,
                       xp_ref, pb_ref, wp_ref, sc_ref, gp_ref, s_in_ref, p_in_ref, s_ref, st_ref, p_ref):
    del s_in_ref, p_in_ref
    S = SSM_BLOCK_STATES
    u = jnp.concatenate([u_ref[j] for j in range(SSM_LANE_BLOCKS)], axis=1)
    ub = u.astype(BF16)
    ys = []
    for j in range(SSM_LANE_BLOCKS):
        x = jnp.dot(ub[:, j * LANES:(j + 1) * LANES], bblk_ref[j], preferred_element_type=F32)
        base = j * 2 * S
        ar = pw_ref[0:1, base:base + S]
        ai = pw_ref[0:1, base + S:base + 2 * S]
        h0r = h0_ref[:, base:base + S]
        h0i = h0_ref[:, base + S:base + 2 * S]
        hr = x[:, 0:S] + ar * h0r - ai * h0i
        hi = x[:, S:] + ar * h0i + ai * h0r
        st_ref[:, base:base + S] = hr
        st_ref[:, base + S:base + 2 * S] = hi
        h = jnp.concatenate([hr, hi], axis=1).astype(BF16)
        ys.append(jnp.dot(h, cblk_ref[j], preferred_element_type=F32))
    s_ref[...] = _ssm_tail(jnp.concatenate(ys, axis=1), u, d_ref, wg_ref, bg_ref, gs_ref).astype(s_ref.dtype)

    xp = xp_ref[...]
    ds = []
    for gi, w in enumerate(POOL_WINDOWS):
        gsl = slice(gi * POOL_GROUP, (gi + 1) * POOL_GROUP)
        s = xp[:, gsl]
        for back in range(1, w):
            s = s + pb_ref[POOL_BUF - back][:, gsl]
        ds.append(s / float(w) - xp[:, gsl])
    p_ref[...] = _pool_tail(ds, wp_ref, sc_ref, gp_ref).astype(p_ref.dtype)


def _mix_sample(uh, h0, pw, bblk, cblk, d, wg, l, bg, gs, xph, pbuf, wp, _l, sc, gp, sh, ph):
    N = h0.shape[1]
    rows = lambda w: pl.BlockSpec((N, w), lambda i: (0, 0))
    anyspec = pl.BlockSpec(memory_space=pl.ANY)
    return pl.pallas_call(
        _mix_sample_kernel,
        grid=(1,),
        in_specs=[pl.BlockSpec((SSM_LANE_BLOCKS, N, LANES), lambda i: (0, 0, 0)),
                  _layer(h0.shape[1:], l), _layer(pw.shape[1:], l), _layer(bblk.shape[1:], l),
                  _layer(cblk.shape[1:], l),
                  _vrow("D_skip", l), _layer((SSM_WIDTH, SSM_WIDTH), l), _vrow("b_glu", l),
                  _vrow("g_out_ssm", l), rows(POOL_WIDTH), _layer(pbuf.shape[1:], l),
                  _layer(wp.shape[1:], l),
                  _vrow("pool_scale", l), _vrow("g_out_pool", l), anyspec, anyspec],
        out_specs=[rows(SSM_WIDTH), _full((N, SSM_STATE_LANES)), rows(POOL_WIDTH)],
        out_shape=[jax.ShapeDtypeStruct(sh.shape, sh.dtype), jax.ShapeDtypeStruct((N, SSM_STATE_LANES), F32),
                   jax.ShapeDtypeStruct(ph.shape, ph.dtype)],
        input_output_aliases={14: 0, 15: 2},
        compiler_params=_params("arbitrary"),
        name="mix_sample",
    )(uh, h0, pw, bblk, cblk, d, wg, bg, gs, xph, pbuf, wp, sc, gp, sh, ph)


def _outproj_kernel(x_ref, a_ref, s_ref, p_ref, w_ref, *rest):
    if len(rest) == 3:
        cast_in_ref, o_ref, cast_out_ref = rest
        cast_out_ref[...] = cast_in_ref[...].astype(BF16)
    else:
        (o_ref,) = rest
    o1 = ATTN_WIDTH
    o2 = o1 + SSM_WIDTH
    acc = x_ref[...]
    acc = acc + jnp.dot(a_ref[...], w_ref[0:o1, :], preferred_element_type=F32)
    acc = acc + jnp.dot(s_ref[...], w_ref[o1:o2, :], preferred_element_type=F32)
    acc = acc + jnp.dot(p_ref[...], w_ref[o2:, :], preferred_element_type=F32)
    o_ref[...] = acc


def _outproj(x, a, s, p, w, l, tm, cast=None):
    R = x.shape[0]
    row = lambda i: (i, 0)
    w_spec = pl.BlockSpec((None, D_MODEL, D_MODEL), lambda i: (l, 0, 0), pipeline_mode=pl.Buffered(1))
    in_specs = [pl.BlockSpec((tm, D_MODEL), row), pl.BlockSpec((tm, ATTN_WIDTH), row),
                pl.BlockSpec((tm, SSM_WIDTH), row), pl.BlockSpec((tm, POOL_WIDTH), row), w_spec]
    args = [x, a, s, p, w]
    outs = [(pl.BlockSpec((tm, D_MODEL), row), jax.ShapeDtypeStruct((R, D_MODEL), F32))]
    if cast is not None:
        c_in, c_out, c_shape = _cast_side_job(cast, l, R // tm)
        in_specs.append(c_in)
        args.append(cast)
        outs.append((c_out, c_shape))
    res = pl.pallas_call(
        _outproj_kernel,
        grid=(R // tm,),
        in_specs=in_specs,
        out_specs=[o[0] for o in outs],
        out_shape=[o[1] for o in outs],
        compiler_params=_params("arbitrary"),
        name="outproj",
    )(*args)
    return res if cast is not None else res[0]


def _ffn_kernel(xm_ref, xh_ref, g_ref, w1_ref, w2_ref, om_ref, oh_ref, hm_scr, hh_scr):
    i = pl.program_id(0)
    f = pl.program_id(1)

    @pl.when(f == 0)
    def _():
        x = xm_ref[...]
        hm_scr[...] = _rms(x, g_ref[...]).astype(BF16)
        om_ref[...] = x

    @pl.when((f == 0) & (i == 0))
    def _():
        x = xh_ref[...]
        hh_scr[...] = _rms(x, g_ref[...]).astype(BF16)
        oh_ref[...] = x

    w1 = w1_ref[...]
    w2 = w2_ref[...]

    def mlp(h):
        h1 = jnp.dot(h, w1, preferred_element_type=F32)
        return jnp.dot(jnp.square(jnp.maximum(h1, 0.0)).astype(BF16), w2, preferred_element_type=F32)

    om_ref[...] += mlp(hm_scr[...])

    @pl.when(i == 0)
    def _():
        oh_ref[...] += mlp(hh_scr[...])


def _ffn(xm, xh, g, l, w1, w2, tm, tf):
    R, RH = xm.shape[0], xh.shape[0]
    return pl.pallas_call(
        _ffn_kernel,
        grid=(R // tm, D_FF // tf),
        in_specs=[pl.BlockSpec((tm, D_MODEL), lambda i, f: (i, 0)),
                  pl.BlockSpec((RH, D_MODEL), lambda i, f: (0, 0), pipeline_mode=pl.Buffered(1)),
                  _vrow("g_ffn", l),
                  pl.BlockSpec((D_MODEL, tf), lambda i, f: (0, f)),
                  pl.BlockSpec((tf, D_MODEL), lambda i, f: (f, 0))],
        out_specs=[pl.BlockSpec((tm, D_MODEL), lambda i, f: (i, 0)),
                   pl.BlockSpec((RH, D_MODEL), lambda i, f: (0, 0))],
        out_shape=[jax.ShapeDtypeStruct((R, D_MODEL), F32), jax.ShapeDtypeStruct((RH, D_MODEL), F32)],
        scratch_shapes=[pltpu.VMEM((tm, D_MODEL), BF16), pltpu.VMEM((RH, D_MODEL), BF16)],
        compiler_params=_params("arbitrary", "arbitrary"),
        name="ffn",
    )(xm, xh, g, w1, w2)


def _rope_tables(pos):
    half = ROT_DIM // 2
    inv = ROPE_THETA ** (-np.arange(0, ROT_DIM, 2, dtype=np.float64) / ROT_DIM)
    ang = np.asarray(pos, np.float64)[:, None] * inv
    cos, sin = np.cos(ang), np.sin(ang)
    n = ang.shape[0]
    z = np.zeros((n, HEAD_DIM - ROT_DIM))
    zh = np.zeros((n, half))
    rc = np.concatenate([cos, cos, z + 1.0], axis=1)
    rs1 = np.concatenate([zh, sin, z], axis=1)
    rs2 = np.concatenate([-sin, zh, z], axis=1)
    return tuple(t.astype(np.float32) for t in (rc, rs1, rs2))


def kernel(x_prompt, x_sample, cache_k, cache_v, state_ssm_re, state_ssm_im, state_pool, meta_tokens, g_mix, w_in, g_q, g_k, sinks, A_re, A_im, log_dt, B_re, B_im, C_re, C_im, D_skip, w_glu, b_glu, w_pool, pool_scale, g_out_attn, g_out_ssm, g_out_pool, w_out, g_ffn, w_ff1, w_ff2):
    B, T, _ = x_prompt.shape
    N = x_sample.shape[0]
    depth = w_in.shape[0]
    assert N <= SEQ0 and N % DEC_STEP == 0 and T % TM_FFN == 0
    meta = meta_tokens.astype(F32)
    head0 = jnp.concatenate([x_sample.reshape(N, D_MODEL), jnp.zeros((SEQ0 - N, D_MODEL), F32), meta], axis=0)
    head_rest = jnp.concatenate([jnp.zeros((SEQ0, D_MODEL), F32), meta], axis=0)
    xh = jnp.concatenate([head0] + [head_rest] * (B - 1), axis=0)
    xm = x_prompt.reshape(B * T, D_MODEL)

    rope_m = _rope_tables(N_META + np.arange(T))
    hr_ = np.arange(HEAD_ROWS)
    pos_h0 = np.where(hr_ < N, PAST_LEN, np.maximum(hr_ - SEQ0, 0))
    pos_h = np.concatenate([pos_h0] + [np.maximum(hr_ - SEQ0, 0)] * (B - 1))
    rope_h = _rope_tables(pos_h)

    wi_all, wo_all = w_in.astype(BF16), w_out.astype(BF16)
    wg_all, wp_all = w_glu.astype(BF16), w_pool.astype(BF16)
    ck = cache_k.astype(F32)
    cv = cache_v.astype(F32)
    RH = B * HEAD_ROWS

    named = dict(g_mix=g_mix, g_ffn=g_ffn, g_out_attn=g_out_attn, D_skip=D_skip, b_glu=b_glu,
                 g_out_ssm=g_out_ssm, pool_scale=pool_scale, g_out_pool=g_out_pool, g_q=g_q, g_k=g_k)
    vecs = jnp.concatenate([named[name].astype(F32) for name, _ in VEC_LAYOUT], axis=1)[:, None, :]
    pw, bblk, cblk = _ssm_params(A_re, A_im, log_dt, B_re, B_im, C_re, C_im)
    ssm_w = lambda l: (pw, bblk, cblk, vecs, wg_all, l, vecs, vecs)
    pool_w = lambda l: (wp_all, l, vecs, vecs)
    in_w = lambda l: (vecs, wi_all, l, vecs, vecs)
    g_attn = vecs
    g_ffn_ = vecs
    sinks_flat = sinks.astype(F32).reshape(depth * N_HEADS)
    sinks_col = sinks.astype(F32).reshape(depth, N_KV_HEADS, GQA_GROUP, 1)
    bias = _attn_bias()
    h0 = jnp.concatenate([state_ssm_re.astype(F32).reshape(depth, N, SSM_LANE_BLOCKS, SSM_BLOCK_STATES),
                          state_ssm_im.astype(F32).reshape(depth, N, SSM_LANE_BLOCKS, SSM_BLOCK_STATES)],
                         axis=-1).reshape(depth, N, SSM_STATE_LANES)
    pbuf = state_pool.astype(F32).transpose(0, 2, 1, 3)

    ks, vs, pls, sts, nks, nvs, st_ss, phs = ([] for _ in range(8))
    for l in range(depth):
        qm, km, vm, um, pm, w1 = _inproj(xm, *in_w(l), rope_m, TM_PROJ, cast=w_ff1)
        qh, kh, vh, uh, ph = _inproj(xh, *in_w(l), rope_h, RH)

        am, ah = _attn(sinks_flat, l, qm, qh, km, kh, vm, vh, g_attn, bias, B)
        sm, sh, st, w2 = _ssm(um, uh, *ssm_w(l), B, TM_SEQ, w_ff2)
        plm, plh = _pool(pm, ph, *pool_w(l), B, TM_SEQ)

        ah, nk, nv = _attn_sample(qh, kh, vh, ck, cv, l, sinks_col, g_attn, ah, N)
        sh, st_s, plh = _mix_sample(uh, h0, *ssm_w(l), ph, pbuf, *pool_w(l), sh, plh)

        xm = _outproj(xm, am, sm, plm, wo_all, l, TM_PROJ)
        xh = _outproj(xh, ah, sh, plh, wo_all, l, RH)
        xm, xh = _ffn(xm, xh, g_ffn_, l, w1, w2, TM_FFN, TF_FFN)

        ks.append(km.reshape(B, T, KV_WIDTH)[:, T - WINDOW:])
        vs.append(vm.reshape(B, T, KV_WIDTH)[:, T - WINDOW:])
        pls.append(pm.reshape(B, T, POOL_WIDTH)[:, T - POOL_BUF:])
        sts.append(st[:, 0])
        nks.append(nk)
        nvs.append(nv)
        st_ss.append(st_s)
        phs.append(ph[:N])

    y_prompt = xm.reshape(B, T, D_MODEL)
    y_sample = xh[:N].reshape(N, 1, D_MODEL)
    heads = lambda t: jnp.stack(t).reshape(depth, -1, WINDOW, N_KV_HEADS, HEAD_DIM)
    p_re, p_im = _state_from_lanes(jnp.stack(sts).reshape(depth * B, SSM_STATE_LANES))
    s_re, s_im = _state_from_lanes(jnp.stack(st_ss).reshape(depth * N, SSM_STATE_LANES))
    st4 = lambda t, n: t.reshape(depth, n, SSM_GROUPS, SSM_STATE)
    s_pool = jnp.concatenate([state_pool.astype(F32)[:, :, 1:], jnp.stack(phs)[:, :, None]], axis=2)
    return (y_prompt, y_sample, heads(ks), heads(vs), st4(p_re, B), st4(p_im, B), jnp.stack(pls),
            jnp.stack(nks), jnp.stack(nvs), st4(s_re, N), st4(s_im, N), s_pool)
```

```python
import functools
import math

import jax
import jax.numpy as jnp
import numpy as np
from jax.experimental import pallas as pl
from jax.experimental.pallas import tpu as pltpu

D_MODEL = 2048
N_META = 16
HEAD_DIM = 128
N_HEADS = 8
N_KV_HEADS = 2
GQA_GROUP = 4
ATTN_WIDTH = 1024
KV_WIDTH = 256
WINDOW = 128
BLOCK = 128
ROT_DIM = 32
ROPE_THETA = 500000.0
SSM_WIDTH = 512
SSM_GROUP_SIZE = 16
SSM_GROUPS = 32
SSM_STATE = 64
POOL_WIDTH = 512
POOL_WINDOWS = (2, 4, 8, 16)
POOL_GROUP = 128
POOL_BUF = 15
POOL_HALO = 16
IN_WIDTH = 2560
D_FF = 8192
EPS = 1e-6
PAST_LEN = 16384
LOG2E = math.log2(math.e)

HEAD_ROWS = BLOCK
SEQ0 = HEAD_ROWS - N_META
LANES = 128
SUBLANES = 8
SSM_LANE_BLOCKS = SSM_WIDTH // LANES
SSM_BLOCK_STATES = (LANES // SSM_GROUP_SIZE) * SSM_STATE
SSM_STATE_LANES = SSM_LANE_BLOCKS * 2 * SSM_BLOCK_STATES
VMEM_LIMIT = 56 * 1024 * 1024

TM_PROJ = 512
TM_FFN = 1024
TF_FFN = 512
TM_SEQ = 512
DEC_STEP = 8
SSM_POW_ROWS = (1, HEAD_ROWS // SUBLANES, TM_SEQ // SUBLANES)

BF16 = jnp.bfloat16
F32 = jnp.float32


def _params(*semantics):
    return pltpu.CompilerParams(dimension_semantics=semantics, vmem_limit_bytes=VMEM_LIMIT)


def _rms(x, g):
    return x * jax.lax.rsqrt(jnp.mean(x * x, axis=-1, keepdims=True) + EPS) * g


def _full(shape):
    n = len(shape)
    return pl.BlockSpec(shape, lambda *_: (0,) * n)


def _layer(shape, l):
    n = len(shape)
    return pl.BlockSpec((None, *shape), lambda *_: (l,) + (0,) * n)


VEC_LAYOUT = (("g_mix", D_MODEL), ("g_ffn", D_MODEL), ("g_out_attn", ATTN_WIDTH), ("D_skip", SSM_WIDTH),
              ("b_glu", SSM_WIDTH), ("g_out_ssm", SSM_WIDTH), ("pool_scale", POOL_WIDTH),
              ("g_out_pool", POOL_WIDTH), ("g_q", HEAD_DIM), ("g_k", HEAD_DIM))
VEC_WIDTH = dict(VEC_LAYOUT)
VEC_OFFSET = {name: sum(w for _, w in VEC_LAYOUT[:i]) for i, (name, _) in enumerate(VEC_LAYOUT)}
assert all(VEC_OFFSET[name] % w == 0 for name, w in VEC_LAYOUT)


def _vrow(name, l):
    w = VEC_WIDTH[name]
    return pl.BlockSpec((None, 1, w), lambda *_: (l, 0, VEC_OFFSET[name] // w))


def _inproj_kernel(x_ref, g_ref, w_ref, gq_ref, gk_ref, rc_ref, rs1_ref, rs2_ref, *rest):
    if len(rest) == 7:
        cast_in_ref, q_ref, k_ref, v_ref, u_ref, xp_ref, cast_out_ref = rest
        cast_out_ref[...] = cast_in_ref[...].astype(BF16)
    else:
        q_ref, k_ref, v_ref, u_ref, xp_ref = rest
    h = _rms(x_ref[...], g_ref[...]).astype(BF16)
    proj = jnp.dot(h, w_ref[...], preferred_element_type=F32)
    rc, rs1, rs2 = rc_ref[...], rs1_ref[...], rs2_ref[...]

    def head(t, g):
        t = _rms(t, g)
        return t * rc + pltpu.roll(t, 16, 1) * rs1 + pltpu.roll(t, LANES - 16, 1) * rs2

    for hd in range(N_HEADS):
        sl = slice(hd * HEAD_DIM, (hd + 1) * HEAD_DIM)
        q_ref[:, sl] = head(proj[:, sl], gq_ref[...])
    for hd in range(N_KV_HEADS):
        sl = slice(hd * HEAD_DIM, (hd + 1) * HEAD_DIM)
        k_ref[:, sl] = head(proj[:, ATTN_WIDTH + hd * HEAD_DIM:ATTN_WIDTH + (hd + 1) * HEAD_DIM], gk_ref[...])
    o2 = ATTN_WIDTH + KV_WIDTH
    o3 = o2 + KV_WIDTH
    o4 = o3 + SSM_WIDTH
    v_ref[...] = proj[:, o2:o3]
    for j in range(SSM_LANE_BLOCKS):
        u_ref[j] = proj[:, o3 + j * LANES:o3 + (j + 1) * LANES]
    xp_ref[...] = proj[:, o4:]


def _cast_side_job(w, l, steps):
    _, r, c = w.shape
    rows = r // steps
    return (pl.BlockSpec((None, rows, c), lambda i: (l, i, 0)), pl.BlockSpec((rows, c), lambda i: (i, 0)),
            jax.ShapeDtypeStruct((r, c), BF16))


def _inproj(x, g, w, l, gq, gk, rope, tm, cast=None):
    R = x.shape[0]
    tiles_per_rope = rope[0].shape[0] // tm
    row = lambda i: (i, 0)
    rrow = lambda i: (i % tiles_per_rope, 0)
    flat = lambda w_: (pl.BlockSpec((tm, w_), row), jax.ShapeDtypeStruct((R, w_), F32))
    u_out = (pl.BlockSpec((SSM_LANE_BLOCKS, tm, LANES), lambda i: (0, i, 0)),
             jax.ShapeDtypeStruct((SSM_LANE_BLOCKS, R, LANES), F32))
    outs = [flat(ATTN_WIDTH), flat(KV_WIDTH), flat(KV_WIDTH), u_out, flat(POOL_WIDTH)]
    w_spec = pl.BlockSpec((D_MODEL, IN_WIDTH), lambda i: (0, 0), pipeline_mode=pl.Buffered(1))
    in_specs = [pl.BlockSpec((tm, D_MODEL), row), _vrow("g_mix", l), w_spec,
                _vrow("g_q", l), _vrow("g_k", l),
                pl.BlockSpec((tm, LANES), rrow), pl.BlockSpec((tm, LANES), rrow),
                pl.BlockSpec((tm, LANES), rrow)]
    args = [x, g, w, gq, gk, *rope]
    if cast is not None:
        c_in, c_out, c_shape = _cast_side_job(cast, l, R // tm)
        in_specs.append(c_in)
        args.append(cast)
        outs.append((c_out, c_shape))
    return pl.pallas_call(
        _inproj_kernel,
        grid=(R // tm,),
        in_specs=in_specs,
        out_specs=[o[0] for o in outs],
        out_shape=[o[1] for o in outs],
        compiler_params=_params("arbitrary"),
        name="inproj",
    )(*args)


def _attn_bias():
    rows = GQA_GROUP * BLOCK
    i, r, c = np.meshgrid(np.arange(3), np.arange(rows) % BLOCK, np.arange(2 * BLOCK), indexing="ij")
    diff = BLOCK + r - c
    krow = (i - 1) * BLOCK + c
    mask = (diff >= 0) & (diff <= WINDOW) & (krow >= SEQ0)
    return np.where(mask, 0.0, -np.inf).astype(np.float32)


def _attn_block(q_blk, kp_blk, kc_blk, vp_blk, vc_blk, bias, sink_ref, l, g):
    rows = GQA_GROUP * BLOCK
    rgrp = jax.lax.broadcasted_iota(jnp.int32, (rows, 1), 0) // BLOCK
    outs = []
    for kh in range(N_KV_HEADS):
        ksl = slice(kh * HEAD_DIM, (kh + 1) * HEAD_DIM)
        qh = jnp.concatenate(
            [q_blk[:, (kh * GQA_GROUP + h) * HEAD_DIM:(kh * GQA_GROUP + h + 1) * HEAD_DIM]
             for h in range(GQA_GROUP)], axis=0).astype(BF16)
        kk = jnp.concatenate([kp_blk[:, ksl], kc_blk[:, ksl]], axis=0).astype(BF16)
        vv = jnp.concatenate([vp_blk[:, ksl], vc_blk[:, ksl]], axis=0).astype(BF16)
        s = jax.lax.dot_general(qh, kk, (((1,), (1,)), ((), ())),
                                preferred_element_type=F32) * (HEAD_DIM ** -0.5 * LOG2E) + bias
        sk = jnp.zeros((rows, 1), F32)
        for h in range(GQA_GROUP):
            sk = jnp.where(rgrp == h, sink_ref[l * N_HEADS + kh * GQA_GROUP + h] * LOG2E, sk)
        m = jnp.maximum(jnp.max(s, axis=-1, keepdims=True), sk)
        p = jnp.exp2(s - m)
        denom = jnp.sum(p, axis=-1, keepdims=True) + jnp.exp2(sk - m)
        o = jnp.dot(p.astype(BF16), vv, preferred_element_type=F32) / denom
        outs.extend(o[h * BLOCK:(h + 1) * BLOCK] for h in range(GQA_GROUP))
    return _rms(jnp.concatenate(outs, axis=1), g)


def _attn_kernel(sink_ref, qm_ref, qh_ref, kpm_ref, kcm_ref, kh_ref, vpm_ref, vcm_ref, vh_ref, g_ref,
                 bias_a_ref, bias_b_ref, *rest, l):
    s = pl.program_id(1)
    if len(rest) == 4:
        cast_in_ref, om_ref, oh_ref, cast_out_ref = rest

        @pl.when(s > 0)
        def _():
            cast_out_ref[...] = cast_in_ref[...].astype(BF16)
    else:
        om_ref, oh_ref = rest
    blk = functools.partial(_attn_block, sink_ref=sink_ref, l=l, g=g_ref[...])

    @pl.when(s == 0)
    def _():
        a = blk(qh_ref[...], kpm_ref[...], kh_ref[...], vpm_ref[...], vh_ref[...], bias_a_ref[...])
        oh_ref[...] = a.astype(oh_ref.dtype)

    @pl.when(s > 0)
    def _():
        k1, k2 = kcm_ref[0:BLOCK, :], kcm_ref[BLOCK:2 * BLOCK, :]
        v1, v2 = vcm_ref[0:BLOCK, :], vcm_ref[BLOCK:2 * BLOCK, :]
        kp = jnp.where(s == 1, kh_ref[...], kpm_ref[...])
        vp = jnp.where(s == 1, vh_ref[...], vpm_ref[...])
        a1 = blk(qm_ref[0:BLOCK, :], kp, k1, vp, v1, bias_a_ref[...])
        a2 = blk(qm_ref[BLOCK:2 * BLOCK, :], k1, k2, v1, v2, bias_b_ref[...])
        om_ref[0:BLOCK, :] = a1.astype(om_ref.dtype)
        om_ref[BLOCK:2 * BLOCK, :] = a2.astype(om_ref.dtype)


def _attn(sinks, l, qm, qh, km, kh, vm, vh, g, bias, B, cast=None, cast_layer=0):
    nb = qm.shape[0] // (B * BLOCK)
    npair = nb // 2
    pair = lambda b, s, _: (b * npair + jnp.maximum(s - 1, 0), 0)
    prev = lambda b, s, _: (b * nb + jnp.maximum(2 * s - 3, 0), 0)
    head = lambda b, s, _: (b, 0)
    one = lambda w, im: pl.BlockSpec((BLOCK, w), im)
    two = lambda w: pl.BlockSpec((2 * BLOCK, w), pair)
    bias_spec = lambda im: pl.BlockSpec((None,) + bias.shape[1:], im)
    in_specs = [two(ATTN_WIDTH), one(ATTN_WIDTH, head),
                one(KV_WIDTH, prev), two(KV_WIDTH), one(KV_WIDTH, head),
                one(KV_WIDTH, prev), two(KV_WIDTH), one(KV_WIDTH, head),
                _vrow("g_out_attn", l),
                bias_spec(lambda b, s, _: (jnp.minimum(s, 2), 0, 0)),
                bias_spec(lambda b, s, _: (2, 0, 0))]
    args = [sinks, qm, qh, km, km, kh, vm, vm, vh, g, bias, bias]
    out_specs = [two(ATTN_WIDTH), one(ATTN_WIDTH, head)]
    out_shape = [jax.ShapeDtypeStruct(qm.shape, BF16), jax.ShapeDtypeStruct(qh.shape, BF16)]
    if cast is not None:
        _, cr, cc = cast.shape
        crows = cr // (B * npair)
        in_specs.append(pl.BlockSpec((None, crows, cc),
                                     lambda b, s, _: (cast_layer, b * npair + jnp.maximum(s - 1, 0), 0)))
        args.append(cast)
        out_specs.append(pl.BlockSpec((crows, cc), pair))
        out_shape.append(jax.ShapeDtypeStruct((cr, cc), BF16))
    return pl.pallas_call(
        functools.partial(_attn_kernel, l=l),
        grid_spec=pltpu.PrefetchScalarGridSpec(
            num_scalar_prefetch=1, grid=(B, npair + 1), in_specs=in_specs, out_specs=out_specs),
        out_shape=out_shape,
        compiler_params=_params("arbitrary", "arbitrary"),
        name="attn_prompt",
    )(*args)


def _ssm_params_kernel(ar_ref, ai_ref, ldt_ref, kk_ref, br_ref, bi_ref, tr_ref, ti_ref, bbr_ref, bbi_ref):
    ar, ai = ar_ref[...], ai_ref[...]
    dt = jnp.exp(ldt_ref[...])
    kk = kk_ref[...]
    mag = jnp.exp(dt * ar * kk)
    ang = dt * ai * kk
    tr = mag * jnp.cos(ang)
    ti = mag * jnp.sin(ang)
    tr_ref[...] = tr
    ti_ref[...] = ti
    abr, abi = tr[:, 0:1], ti[:, 0:1]
    den = ar * ar + ai * ai
    fr = ((abr - 1.0) * ar + abi * ai) / den
    fi = (abi * ar - (abr - 1.0) * ai) / den
    br, bi = br_ref[...], bi_ref[...]
    bbr_ref[...] = fr * br - fi * bi
    bbi_ref[...] = fr * bi + fi * br


def _ssm_params(A_re, A_im, log_dt, B_re, B_im, C_re, C_im):
    depth = A_re.shape[0]
    n = depth * SSM_GROUPS * SSM_STATE
    col = lambda t: t.astype(F32).reshape(n, 1)
    ldt = jnp.broadcast_to(log_dt.astype(F32)[:, :, None], (depth, SSM_GROUPS, SSM_STATE)).reshape(n, 1)
    kk = jnp.array([SSM_POW_ROWS + (0,) * (SUBLANES - len(SSM_POW_ROWS))], F32)
    shapes = [(n, SUBLANES), (n, SUBLANES), (n, SSM_GROUP_SIZE), (n, SSM_GROUP_SIZE)]
    tr, ti, bbr, bbi = pl.pallas_call(
        _ssm_params_kernel,
        out_shape=[jax.ShapeDtypeStruct(s, F32) for s in shapes],
        name="ssm_params",
    )(col(A_re), col(A_im), ldt, kk, B_re.astype(F32).reshape(n, SSM_GROUP_SIZE),
      B_im.astype(F32).reshape(n, SSM_GROUP_SIZE))
    J, G8 = SSM_LANE_BLOCKS, LANES // SSM_GROUP_SIZE

    def lanes(t):
        return t.reshape(depth, J, SSM_BLOCK_STATES, SUBLANES).transpose(0, 3, 1, 2)

    pw = jnp.concatenate([lanes(tr), lanes(ti)], axis=-1).reshape(depth, SUBLANES, SSM_STATE_LANES)
    eye = jnp.eye(G8, dtype=F32)

    def bdiag(t):
        t = t.reshape(depth, J, G8, SSM_STATE, SSM_GROUP_SIZE).transpose(0, 1, 2, 4, 3)
        t = t[:, :, :, :, None, :] * eye[None, None, :, None, :, None]
        return t.reshape(depth, J, LANES, SSM_BLOCK_STATES)

    bblk = jnp.concatenate([bdiag(bbr), bdiag(bbi)], axis=-1).astype(BF16)

    def cdiag(t):
        t = t.astype(F32).reshape(depth, J, G8, SSM_GROUP_SIZE, SSM_STATE).transpose(0, 1, 2, 4, 3)
        t = t[:, :, :, :, None, :] * eye[None, None, :, None, :, None]
        return t.reshape(depth, J, SSM_BLOCK_STATES, LANES)

    cblk = jnp.concatenate([cdiag(C_re), -cdiag(C_im)], axis=2).astype(BF16)
    return pw, bblk, cblk


def _ssm_tail(y, u, d_ref, wg_ref, bg_ref, g_ref):
    y = y + d_ref[...] * u
    z = jax.nn.gelu(y)
    gate = jax.nn.sigmoid(jnp.dot(z.astype(BF16), wg_ref[...], preferred_element_type=F32) + bg_ref[...])
    return _rms(z * gate, g_ref[...])


def _ssm_sweep(x_scr, n, a_tabs, init, store):
    S = SSM_BLOCK_STATES
    fins = []
    for j0 in range(0, SSM_LANE_BLOCKS, 2):
        js = (j0, j0 + 1)

        def body(k, carry, js=js):
            r0 = pl.multiple_of(k * SUBLANES, SUBLANES)
            out = []
            for idx, j in enumerate(js):
                hr, hi = carry[2 * idx], carry[2 * idx + 1]
                base = j * 2 * S
                ar, ai = a_tabs[j]
                nhr = ar * hr - ai * hi + x_scr[pl.ds(r0, SUBLANES), base:base + S]
                nhi = ar * hi + ai * hr + x_scr[pl.ds(r0, SUBLANES), base + S:base + 2 * S]
                if store:
                    x_scr[pl.ds(r0, SUBLANES), base:base + S] = nhr
                    x_scr[pl.ds(r0, SUBLANES), base + S:base + 2 * S] = nhi
                out += [nhr, nhi]
            return tuple(out)

        c0 = tuple(t for j in js for t in init[j])
        res = jax.lax.fori_loop(0, n // SUBLANES, body, c0, unroll=2)
        fins += [(res[0], res[1]), (res[2], res[3])]
    return fins


def _ssm_rows(u, pow_row, pw_ref, bblk_ref, cblk_ref, x_scr, s_scr, carry_scr):
    n = u.shape[0]
    S = SSM_BLOCK_STATES
    ub = u.astype(BF16)
    for j in range(SSM_LANE_BLOCKS):
        x_scr[0:n, j * 2 * S:(j + 1) * 2 * S] = jnp.dot(ub[:, j * LANES:(j + 1) * LANES], bblk_ref[j],
                                                       preferred_element_type=F32)
    bc = lambda t: jnp.broadcast_to(t, (SUBLANES, S))
    re = lambda ref, r0, r1, j: ref[r0:r1, j * 2 * S:j * 2 * S + S]
    im = lambda ref, r0, r1, j: ref[r0:r1, j * 2 * S + S:(j + 1) * 2 * S]
    a_tabs = [(bc(re(pw_ref, 0, 1, j)), bc(im(pw_ref, 0, 1, j))) for j in range(SSM_LANE_BLOCKS)]
    zero = jnp.zeros((SUBLANES, S), F32)
    fins = _ssm_sweep(x_scr, n, a_tabs, [(zero, zero)] * SSM_LANE_BLOCKS, store=False)
    for j in range(SSM_LANE_BLOCKS):
        base = j * 2 * S
        cr, ci = re(pw_ref, pow_row, pow_row + 1, j), im(pw_ref, pow_row, pow_row + 1, j)
        sr, si = re(carry_scr, 0, 1, j), im(carry_scr, 0, 1, j)
        fr, fi = fins[j]
        for c in range(SUBLANES):
            s_scr[c:c + 1, base:base + S] = sr
            s_scr[c:c + 1, base + S:base + 2 * S] = si
            sr, si = cr * sr - ci * si + fr[c:c + 1], cr * si + ci * sr + fi[c:c + 1]
        carry_scr[:, base:base + S] = bc(sr)
        carry_scr[:, base + S:base + 2 * S] = bc(si)
    init = [(re(s_scr, 0, SUBLANES, j), im(s_scr, 0, SUBLANES, j)) for j in range(SSM_LANE_BLOCKS)]
    _ssm_sweep(x_scr, n, a_tabs, init, store=True)
    ys = [jnp.dot(x_scr[0:n, j * 2 * S:(j + 1) * 2 * S].astype(BF16), cblk_ref[j], preferred_element_type=F32)
          for j in range(SSM_LANE_BLOCKS)]
    return jnp.concatenate(ys, axis=1)


def _ssm_tile(u_ref, n, pow_row, seq_start, refs, o_ref, scr):
    pw_ref, bblk_ref, cblk_ref, d_ref, wg_ref, bg_ref, g_ref = refs
    up_scr, x_scr, s_scr, carry_scr, o_scr = scr
    q = n // SUBLANES
    for j in range(SSM_LANE_BLOCKS):
        for k in range(q):
            up_scr[k * SUBLANES:(k + 1) * SUBLANES, j * LANES:(j + 1) * LANES] = \
                u_ref[j, pl.ds(k, SUBLANES, stride=q), :]
    u = up_scr[0:n, :]
    if seq_start:
        p = jax.lax.broadcasted_iota(jnp.int32, (n, 1), 0)
        u = jnp.where((p % SUBLANES) * q + p // SUBLANES >= seq_start, u, 0.0)
    y = _ssm_rows(u, pow_row, pw_ref, bblk_ref, cblk_ref, x_scr, s_scr, carry_scr)
    out = _ssm_tail(y, u, d_ref, wg_ref, bg_ref, g_ref)
    for j in range(SSM_LANE_BLOCKS):
        for k in range(q):
            o_scr[j, pl.ds(k, SUBLANES, stride=q), :] = out[k * SUBLANES:(k + 1) * SUBLANES,
                                                            j * LANES:(j + 1) * LANES]
    o_ref[...] = jnp.concatenate([o_scr[j, 0:n, :] for j in range(SSM_LANE_BLOCKS)], axis=1).astype(o_ref.dtype)


def _ssm_kernel(um_ref, uh_ref, pw_ref, bblk_ref, cblk_ref, d_ref, wg_ref, bg_ref, g_ref, cast_in_ref,
                om_ref, oh_ref, st_ref, cast_out_ref, up_scr, x_scr, s_scr, carry_scr, o_scr):
    t = pl.program_id(1)
    refs = (pw_ref, bblk_ref, cblk_ref, d_ref, wg_ref, bg_ref, g_ref)
    scr = (up_scr, x_scr, s_scr, carry_scr, o_scr)

    @pl.when(t == 0)
    def _():
        carry_scr[...] = jnp.zeros_like(carry_scr)
        _ssm_tile(uh_ref, HEAD_ROWS, 1, SEQ0, refs, oh_ref, scr)

    @pl.when(t > 0)
    def _():
        cast_out_ref[...] = cast_in_ref[...].astype(BF16)
        _ssm_tile(um_ref, um_ref.shape[1], 2, 0, refs, om_ref, scr)
        st_ref[...] = carry_scr[...]


def _ssm(um, uh, pw, bblk, cblk, d, wg, l, bg, g, B, tm, cast):
    assert (1, HEAD_ROWS // SUBLANES, tm // SUBLANES) == SSM_POW_ROWS
    J = SSM_LANE_BLOCKS
    rm, rh = um.shape[1], uh.shape[1]
    nt = rm // (B * tm)
    tile = lambda b, t: b * nt + jnp.maximum(t - 1, 0)
    main = lambda b, t: (tile(b, t), 0)
    head = lambda b, t: (b, 0)
    _, cr, cc = cast.shape
    crows = cr // (B * nt)
    return pl.pallas_call(
        _ssm_kernel,
        grid=(B, nt + 1),
        in_specs=[pl.BlockSpec((J, tm, LANES), lambda b, t: (0, tile(b, t), 0)),
                  pl.BlockSpec((J, HEAD_ROWS, LANES), lambda b, t: (0, b, 0)),
                  _layer(pw.shape[1:], l), _layer(bblk.shape[1:], l), _layer(cblk.shape[1:], l),
                  _vrow("D_skip", l), _layer((SSM_WIDTH, SSM_WIDTH), l), _vrow("b_glu", l),
                  _vrow("g_out_ssm", l),
                  pl.BlockSpec((None, crows, cc), lambda b, t: (l, tile(b, t), 0))],
        out_specs=[pl.BlockSpec((tm, SSM_WIDTH), main), pl.BlockSpec((HEAD_ROWS, SSM_WIDTH), head),
                   pl.BlockSpec((None, SUBLANES, SSM_STATE_LANES), lambda b, t: (b, 0, 0)),
                   pl.BlockSpec((crows, cc), main)],
        out_shape=[jax.ShapeDtypeStruct((rm, SSM_WIDTH), BF16), jax.ShapeDtypeStruct((rh, SSM_WIDTH), BF16),
                   jax.ShapeDtypeStruct((B, SUBLANES, SSM_STATE_LANES), F32),
                   jax.ShapeDtypeStruct((cr, cc), BF16)],
        scratch_shapes=[pltpu.VMEM((tm, SSM_WIDTH), F32),
                        pltpu.VMEM((tm, SSM_STATE_LANES), F32),
                        pltpu.VMEM((SUBLANES, SSM_STATE_LANES), F32),
                        pltpu.VMEM((SUBLANES, SSM_STATE_LANES), F32),
                        pltpu.VMEM((J, tm, LANES), F32)],
        compiler_params=_params("arbitrary", "arbitrary"),
        name="ssm_prompt",
    )(um, uh, pw, bblk, cblk, d, wg, bg, g, cast)


def _state_from_lanes(s):
    s = s.reshape(s.shape[0], SSM_LANE_BLOCKS, 2, SSM_BLOCK_STATES)
    return (s[:, :, 0].reshape(-1, SSM_GROUPS, SSM_STATE), s[:, :, 1].reshape(-1, SSM_GROUPS, SSM_STATE))


def _pool_tail(d_groups, w_ref, sc_ref, g_ref):
    y = jnp.concatenate(
        [jnp.dot(d.astype(BF16), w_ref[gi], preferred_element_type=F32) for gi, d in enumerate(d_groups)], axis=1)
    return _rms(y * sc_ref[...], g_ref[...])


def _pool_rows(x, prev, pos0, w_ref, sc_ref, g_ref):
    n = x.shape[0]
    xe = jnp.concatenate([prev, x], axis=0)
    pos = pos0 + jax.lax.broadcasted_iota(jnp.int32, (n, 1), 0)
    ds = []
    for gi, w in enumerate(POOL_WINDOWS):
        gsl = slice(gi * POOL_GROUP, (gi + 1) * POOL_GROUP)
        s = xe[:, gsl]
        k = 1
        while k < w:
            s = s + pltpu.roll(s, k, 0)
            k *= 2
        cnt = jnp.clip(pos + 1, 1, w).astype(F32)
        ds.append(s[POOL_HALO:] / cnt - x[:, gsl])
    return _pool_tail(ds, w_ref, sc_ref, g_ref)


def _pool_kernel(xm_ref, halo_ref, xh_ref, w_ref, sc_ref, g_ref, cast_in_ref, om_ref, oh_ref, cast_out_ref):
    t = pl.program_id(1)
    tm = xm_ref.shape[0]

    @pl.when(t == 0)
    def _():
        hrow = jax.lax.broadcasted_iota(jnp.int32, (HEAD_ROWS, 1), 0)
        x = jnp.where(hrow >= SEQ0, xh_ref[...], 0.0)
        prev = jnp.zeros((POOL_HALO, POOL_WIDTH), F32)
        oh_ref[...] = _pool_rows(x, prev, -SEQ0, w_ref, sc_ref, g_ref).astype(oh_ref.dtype)

    @pl.when(t > 0)
    def _():
        prev = jnp.where(t == 1, xh_ref[HEAD_ROWS - POOL_HALO:, :], halo_ref[...])
        om_ref[...] = _pool_rows(xm_ref[...], prev, N_META + (t - 1) * tm, w_ref, sc_ref, g_ref).astype(om_ref.dtype)
        cast_out_ref[...] = cast_in_ref[...].astype(BF16)


def _pool(xm, xh, w, l, sc, g, B, tm, cast):
    nt = xm.shape[0] // (B * tm)
    r = tm // POOL_HALO
    main = lambda b, t: (b * nt + jnp.maximum(t - 1, 0), 0)
    halo = lambda b, t: (jnp.maximum((b * nt + t - 1) * r - 1, 0), 0)
    head = lambda b, t: (b, 0)
    _, cr, cc = cast.shape
    crows = cr // (B * nt)
    return pl.pallas_call(
        _pool_kernel,
        grid=(B, nt + 1),
        in_specs=[pl.BlockSpec((tm, POOL_WIDTH), main), pl.BlockSpec((POOL_HALO, POOL_WIDTH), halo),
                  pl.BlockSpec((HEAD_ROWS, POOL_WIDTH), head),
                  _layer(w.shape[1:], l), _vrow("pool_scale", l), _vrow("g_out_pool", l),
                  pl.BlockSpec((None, crows, cc), lambda b, t: (l, b * nt + jnp.maximum(t - 1, 0), 0))],
        out_specs=[pl.BlockSpec((tm, POOL_WIDTH), main), pl.BlockSpec((HEAD_ROWS, POOL_WIDTH), head),
                   pl.BlockSpec((crows, cc), main)],
        out_shape=[jax.ShapeDtypeStruct(xm.shape, BF16), jax.ShapeDtypeStruct(xh.shape, BF16),
                   jax.ShapeDtypeStruct((cr, cc), BF16)],
        compiler_params=_params("arbitrary", "arbitrary"),
        name="pool_prompt",
    )(xm, xm, xh, w, sc, g, cast)


def _attn_sample_kernel(q_ref, kn_ref, vn_ref, kc_ref, vc_ref, sink_ref, g_ref, a_in_ref,
                        a_ref, nk_ref, nv_ref, acc_scr):
    del a_in_ref
    step = pl.program_id(0)
    scale = HEAD_DIM ** -0.5
    for bb in range(DEC_STEP):
        nk_ref[bb, 0:WINDOW - 1] = kc_ref[bb, 1:WINDOW]
        nv_ref[bb, 0:WINDOW - 1] = vc_ref[bb, 1:WINDOW]
        outs = []
        for kh in range(N_KV_HEADS):
            ksl = slice(kh * HEAD_DIM, (kh + 1) * HEAD_DIM)
            qh = jnp.concatenate(
                [q_ref[bb:bb + 1, (kh * GQA_GROUP + g) * HEAD_DIM:(kh * GQA_GROUP + g + 1) * HEAD_DIM]
                 for g in range(GQA_GROUP)], axis=0)
            kn = kn_ref[bb:bb + 1, ksl]
            vn = vn_ref[bb:bb + 1, ksl]
            nk_ref[bb, WINDOW - 1, kh:kh + 1, :] = kn
            nv_ref[bb, WINDOW - 1, kh:kh + 1, :] = vn
            kc = kc_ref[bb, :, kh, :]
            vc = vc_ref[bb, :, kh, :]
            sc = jax.lax.dot_general(qh.astype(BF16), kc.astype(BF16), (((1,), (1,)), ((), ())),
                                     preferred_element_type=F32) * scale
            sn = jnp.sum(qh.astype(BF16).astype(F32) * kn.astype(BF16).astype(F32), axis=-1, keepdims=True) * scale
            sk = sink_ref[kh]
            m = jnp.maximum(jnp.maximum(jnp.max(sc, axis=-1, keepdims=True), sn), sk)
            pc = jnp.exp(sc - m)
            pn = jnp.exp(sn - m)
            denom = jnp.sum(pc, axis=-1, keepdims=True) + pn + jnp.exp(sk - m)
            o = jnp.dot(pc.astype(BF16), vc.astype(BF16), preferred_element_type=F32)
            o = (o + pn.astype(BF16).astype(F32) * vn.astype(BF16).astype(F32)) / denom
            outs.extend(o[g:g + 1] for g in range(GQA_GROUP))
        a = jnp.concatenate(outs, axis=1)
        acc_scr[pl.ds(step * DEC_STEP + bb, 1), :] = _rms(a, g_ref[...])

    @pl.when(step == pl.num_programs(0) - 1)
    def _():
        a_ref[...] = acc_scr[...].astype(a_ref.dtype)


def _attn_sample(qh, kh, vh, cache_k, cache_v, l, sinks, g, ah, N):
    rows = lambda w: pl.BlockSpec((DEC_STEP, w), lambda s: (s, 0))
    cache = pl.BlockSpec((None, DEC_STEP, WINDOW, N_KV_HEADS, HEAD_DIM), lambda s: (l, s, 0, 0, 0))
    ncache = pl.BlockSpec((DEC_STEP, WINDOW, N_KV_HEADS, HEAD_DIM), lambda s: (s, 0, 0, 0))
    cshape = jax.ShapeDtypeStruct((N, WINDOW, N_KV_HEADS, HEAD_DIM), F32)
    return pl.pallas_call(
        _attn_sample_kernel,
        grid=(N // DEC_STEP,),
        in_specs=[rows(ATTN_WIDTH), rows(KV_WIDTH), rows(KV_WIDTH), cache, cache,
                  _layer((N_KV_HEADS, GQA_GROUP, 1), l), _vrow("g_out_attn", l),
                  pl.BlockSpec(memory_space=pl.ANY)],
        out_specs=[pl.BlockSpec((N, ATTN_WIDTH), lambda s: (0, 0)), ncache, ncache],
        out_shape=[jax.ShapeDtypeStruct(ah.shape, ah.dtype), cshape, cshape],
        scratch_shapes=[pltpu.VMEM((N, ATTN_WIDTH), F32)],
        input_output_aliases={7: 0},
        compiler_params=_params("arbitrary"),
        name="attn_sample",
    )(qh, kh, vh, cache_k, cache_v, sinks, g, ah)


def _mix_sample_kernel(u_ref, h0_ref, pw_ref, bblk_ref, cblk_ref, d_ref, wg_ref, bg_ref, gs_ref,
                       xp_ref, pb_ref, wp_ref, sc_ref, gp_ref, s_in_ref, p_in_ref, s_ref, st_ref, p_ref):
    del s_in_ref, p_in_ref
    S = SSM_BLOCK_STATES
    u = jnp.concatenate([u_ref[j] for j in range(SSM_LANE_BLOCKS)], axis=1)
    ub = u.astype(BF16)
    ys = []
    for j in range(SSM_LANE_BLOCKS):
        x = jnp.dot(ub[:, j * LANES:(j + 1) * LANES], bblk_ref[j], preferred_element_type=F32)
        base = j * 2 * S
        ar = pw_ref[0:1, base:base + S]
        ai = pw_ref[0:1, base + S:base + 2 * S]
        h0r = h0_ref[:, base:base + S]
        h0i = h0_ref[:, base + S:base + 2 * S]
        hr = x[:, 0:S] + ar * h0r - ai * h0i
        hi = x[:, S:] + ar * h0i + ai * h0r
        st_ref[:, base:base + S] = hr
        st_ref[:, base + S:base + 2 * S] = hi
        h = jnp.concatenate([hr, hi], axis=1).astype(BF16)
        ys.append(jnp.dot(h, cblk_ref[j], preferred_element_type=F32))
    s_ref[...] = _ssm_tail(jnp.concatenate(ys, axis=1), u, d_ref, wg_ref, bg_ref, gs_ref).astype(s_ref.dtype)

    xp = xp_ref[...]
    ds = []
    for gi, w in enumerate(POOL_WINDOWS):
        gsl = slice(gi * POOL_GROUP, (gi + 1) * POOL_GROUP)
        s = xp[:, gsl]
        for back in range(1, w):
            s = s + pb_ref[POOL_BUF - back][:, gsl]
        ds.append(s / float(w) - xp[:, gsl])
    p_ref[...] = _pool_tail(ds, wp_ref, sc_ref, gp_ref).astype(p_ref.dtype)


def _mix_sample(uh, h0, pw, bblk, cblk, d, wg, l, bg, gs, xph, pbuf, wp, _l, sc, gp, sh, ph):
    N = h0.shape[1]
    rows = lambda w: pl.BlockSpec((N, w), lambda i: (0, 0))
    anyspec = pl.BlockSpec(memory_space=pl.ANY)
    return pl.pallas_call(
        _mix_sample_kernel,
        grid=(1,),
        in_specs=[pl.BlockSpec((SSM_LANE_BLOCKS, N, LANES), lambda i: (0, 0, 0)),
                  _layer(h0.shape[1:], l), _layer(pw.shape[1:], l), _layer(bblk.shape[1:], l),
                  _layer(cblk.shape[1:], l),
                  _vrow("D_skip", l), _layer((SSM_WIDTH, SSM_WIDTH), l), _vrow("b_glu", l),
                  _vrow("g_out_ssm", l), rows(POOL_WIDTH), _layer(pbuf.shape[1:], l),
                  _layer(wp.shape[1:], l),
                  _vrow("pool_scale", l), _vrow("g_out_pool", l), anyspec, anyspec],
        out_specs=[rows(SSM_WIDTH), _full((N, SSM_STATE_LANES)), rows(POOL_WIDTH)],
        out_shape=[jax.ShapeDtypeStruct(sh.shape, sh.dtype), jax.ShapeDtypeStruct((N, SSM_STATE_LANES), F32),
                   jax.ShapeDtypeStruct(ph.shape, ph.dtype)],
        input_output_aliases={14: 0, 15: 2},
        compiler_params=_params("arbitrary"),
        name="mix_sample",
    )(uh, h0, pw, bblk, cblk, d, wg, bg, gs, xph, pbuf, wp, sc, gp, sh, ph)


def _outproj_kernel(x_ref, a_ref, s_ref, p_ref, w_ref, o_ref):
    o1 = ATTN_WIDTH
    o2 = o1 + SSM_WIDTH
    acc = x_ref[...]
    acc = acc + jnp.dot(a_ref[...], w_ref[0:o1, :], preferred_element_type=F32)
    acc = acc + jnp.dot(s_ref[...], w_ref[o1:o2, :], preferred_element_type=F32)
    acc = acc + jnp.dot(p_ref[...], w_ref[o2:, :], preferred_element_type=F32)
    o_ref[...] = acc


def _outproj(x, a, s, p, w, tm):
    R = x.shape[0]
    row = lambda i: (i, 0)
    return pl.pallas_call(
        _outproj_kernel,
        grid=(R // tm,),
        in_specs=[pl.BlockSpec((tm, D_MODEL), row), pl.BlockSpec((tm, ATTN_WIDTH), row),
                  pl.BlockSpec((tm, SSM_WIDTH), row), pl.BlockSpec((tm, POOL_WIDTH), row),
                  pl.BlockSpec((D_MODEL, D_MODEL), lambda i: (0, 0), pipeline_mode=pl.Buffered(1))],
        out_specs=pl.BlockSpec((tm, D_MODEL), row),
        out_shape=jax.ShapeDtypeStruct((R, D_MODEL), F32),
        compiler_params=_params("arbitrary"),
        name="outproj",
    )(x, a, s, p, w)


def _ffn_kernel(xm_ref, xh_ref, g_ref, w1_ref, w2_ref, om_ref, oh_ref, hm_scr, hh_scr):
    i = pl.program_id(0)
    f = pl.program_id(1)

    @pl.when(f == 0)
    def _():
        x = xm_ref[...]
        hm_scr[...] = _rms(x, g_ref[...]).astype(BF16)
        om_ref[...] = x

    @pl.when((f == 0) & (i == 0))
    def _():
        x = xh_ref[...]
        hh_scr[...] = _rms(x, g_ref[...]).astype(BF16)
        oh_ref[...] = x

    w1 = w1_ref[...]
    w2 = w2_ref[...]

    def mlp(h):
        h1 = jnp.dot(h, w1, preferred_element_type=F32)
        return jnp.dot(jnp.square(jnp.maximum(h1, 0.0)).astype(BF16), w2, preferred_element_type=F32)

    om_ref[...] += mlp(hm_scr[...])

    @pl.when(i == 0)
    def _():
        oh_ref[...] += mlp(hh_scr[...])


def _ffn(xm, xh, g, l, w1, w2, tm, tf):
    R, RH = xm.shape[0], xh.shape[0]
    return pl.pallas_call(
        _ffn_kernel,
        grid=(R // tm, D_FF // tf),
        in_specs=[pl.BlockSpec((tm, D_MODEL), lambda i, f: (i, 0)),
                  pl.BlockSpec((RH, D_MODEL), lambda i, f: (0, 0), pipeline_mode=pl.Buffered(1)),
                  _vrow("g_ffn", l),
                  pl.BlockSpec((D_MODEL, tf), lambda i, f: (0, f)),
                  pl.BlockSpec((tf, D_MODEL), lambda i, f: (f, 0))],
        out_specs=[pl.BlockSpec((tm, D_MODEL), lambda i, f: (i, 0)),
                   pl.BlockSpec((RH, D_MODEL), lambda i, f: (0, 0))],
        out_shape=[jax.ShapeDtypeStruct((R, D_MODEL), F32), jax.ShapeDtypeStruct((RH, D_MODEL), F32)],
        scratch_shapes=[pltpu.VMEM((tm, D_MODEL), BF16), pltpu.VMEM((RH, D_MODEL), BF16)],
        compiler_params=_params("arbitrary", "arbitrary"),
        name="ffn",
    )(xm, xh, g, w1, w2)


def _rope_tables(pos):
    half = ROT_DIM // 2
    inv = ROPE_THETA ** (-np.arange(0, ROT_DIM, 2, dtype=np.float64) / ROT_DIM)
    ang = np.asarray(pos, np.float64)[:, None] * inv
    cos, sin = np.cos(ang), np.sin(ang)
    n = ang.shape[0]
    z = np.zeros((n, HEAD_DIM - ROT_DIM))
    zh = np.zeros((n, half))
    rc = np.concatenate([cos, cos, z + 1.0], axis=1)
    rs1 = np.concatenate([zh, sin, z], axis=1)
    rs2 = np.concatenate([-sin, zh, z], axis=1)
    return tuple(t.astype(np.float32) for t in (rc, rs1, rs2))


def kernel(x_prompt, x_sample, cache_k, cache_v, state_ssm_re, state_ssm_im, state_pool, meta_tokens, g_mix, w_in, g_q, g_k, sinks, A_re, A_im, log_dt, B_re, B_im, C_re, C_im, D_skip, w_glu, b_glu, w_pool, pool_scale, g_out_attn, g_out_ssm, g_out_pool, w_out, g_ffn, w_ff1, w_ff2):
    B, T, _ = x_prompt.shape
    N = x_sample.shape[0]
    depth = w_in.shape[0]
    assert N <= SEQ0 and N % DEC_STEP == 0 and T % TM_FFN == 0
    meta = meta_tokens.astype(F32)
    head0 = jnp.concatenate([x_sample.reshape(N, D_MODEL), jnp.zeros((SEQ0 - N, D_MODEL), F32), meta], axis=0)
    head_rest = jnp.concatenate([jnp.zeros((SEQ0, D_MODEL), F32), meta], axis=0)
    xh = jnp.concatenate([head0] + [head_rest] * (B - 1), axis=0)
    xm = x_prompt.reshape(B * T, D_MODEL)

    rope_m = _rope_tables(N_META + np.arange(T))
    hr_ = np.arange(HEAD_ROWS)
    pos_h0 = np.where(hr_ < N, PAST_LEN, np.maximum(hr_ - SEQ0, 0))
    pos_h = np.concatenate([pos_h0] + [np.maximum(hr_ - SEQ0, 0)] * (B - 1))
    rope_h = _rope_tables(pos_h)

    wi = w_in[0].astype(BF16)
    wg_all, wp_all = w_glu.astype(BF16), w_pool.astype(BF16)
    ck = cache_k.astype(F32)
    cv = cache_v.astype(F32)
    RH = B * HEAD_ROWS

    named = dict(g_mix=g_mix, g_ffn=g_ffn, g_out_attn=g_out_attn, D_skip=D_skip, b_glu=b_glu,
                 g_out_ssm=g_out_ssm, pool_scale=pool_scale, g_out_pool=g_out_pool, g_q=g_q, g_k=g_k)
    vecs = jnp.concatenate([named[name].astype(F32) for name, _ in VEC_LAYOUT], axis=1)[:, None, :]
    pw, bblk, cblk = _ssm_params(A_re, A_im, log_dt, B_re, B_im, C_re, C_im)
    ssm_w = lambda l: (pw, bblk, cblk, vecs, wg_all, l, vecs, vecs)
    pool_w = lambda l: (wp_all, l, vecs, vecs)
    g_attn = vecs
    g_ffn_ = vecs
    sinks_flat = sinks.astype(F32).reshape(depth * N_HEADS)
    sinks_col = sinks.astype(F32).reshape(depth, N_KV_HEADS, GQA_GROUP, 1)
    bias = _attn_bias()
    h0 = jnp.concatenate([state_ssm_re.astype(F32).reshape(depth, N, SSM_LANE_BLOCKS, SSM_BLOCK_STATES),
                          state_ssm_im.astype(F32).reshape(depth, N, SSM_LANE_BLOCKS, SSM_BLOCK_STATES)],
                         axis=-1).reshape(depth, N, SSM_STATE_LANES)
    pbuf = state_pool.astype(F32).transpose(0, 2, 1, 3)

    ks, vs, pls, sts, nks, nvs, st_ss, phs = ([] for _ in range(8))
    for l in range(depth):
        in_w = (vecs, wi, l, vecs, vecs)
        qm, km, vm, um, pm, w1 = _inproj(xm, *in_w, rope_m, TM_PROJ, cast=w_ff1)
        qh, kh, vh, uh, ph = _inproj(xh, *in_w, rope_h, RH)

        if l + 1 < depth:
            am, ah, wi = _attn(sinks_flat, l, qm, qh, km, kh, vm, vh, g_attn, bias, B, cast=w_in, cast_layer=l + 1)
        else:
            am, ah = _attn(sinks_flat, l, qm, qh, km, kh, vm, vh, g_attn, bias, B)
        sm, sh, st, w2 = _ssm(um, uh, *ssm_w(l), B, TM_SEQ, w_ff2)
        plm, plh, wo = _pool(pm, ph, *pool_w(l), B, TM_SEQ, w_out)

        ah, nk, nv = _attn_sample(qh, kh, vh, ck, cv, l, sinks_col, g_attn, ah, N)
        sh, st_s, plh = _mix_sample(uh, h0, *ssm_w(l), ph, pbuf, *pool_w(l), sh, plh)

        xm = _outproj(xm, am, sm, plm, wo, TM_PROJ)
        xh = _outproj(xh, ah, sh, plh, wo, RH)
        xm, xh = _ffn(xm, xh, g_ffn_, l, w1, w2, TM_FFN, TF_FFN)

        ks.append(km.reshape(B, T, KV_WIDTH)[:, T - WINDOW:])
        vs.append(vm.reshape(B, T, KV_WIDTH)[:, T - WINDOW:])
        pls.append(pm.reshape(B, T, POOL_WIDTH)[:, T - POOL_BUF:])
        sts.append(st[:, 0])
        nks.append(nk)
        nvs.append(nv)
        st_ss.append(st_s)
        phs.append(ph[:N])

    y_prompt = xm.reshape(B, T, D_MODEL)
    y_sample = xh[:N].reshape(N, 1, D_MODEL)
    heads = lambda t: jnp.stack(t).reshape(depth, -1, WINDOW, N_KV_HEADS, HEAD_DIM)
    p_re, p_im = _state_from_lanes(jnp.stack(sts).reshape(depth * B, SSM_STATE_LANES))
    s_re, s_im = _state_from_lanes(jnp.stack(st_ss).reshape(depth * N, SSM_STATE_LANES))
    st4 = lambda t, n: t.reshape(depth, n, SSM_GROUPS, SSM_STATE)
    s_pool = jnp.concatenate([state_pool.astype(F32)[:, :, 1:], jnp.stack(phs)[:, :, None]], axis=2)
    return (y_prompt, y_sample, heads(ks), heads(vs), st4(p_re, B), st4(p_im, B), jnp.stack(pls),
            jnp.stack(nks), jnp.stack(nvs), st4(s_re, N), st4(s_im, N), s_pool)
```

```python
import functools
import math

import jax
import jax.numpy as jnp
import numpy as np
from jax.experimental import pallas as pl
from jax.experimental.pallas import tpu as pltpu

D_MODEL = 2048
N_META = 16
HEAD_DIM = 128
N_HEADS = 8
N_KV_HEADS = 2
GQA_GROUP = 4
ATTN_WIDTH = 1024
KV_WIDTH = 256
WINDOW = 128
BLOCK = 128
ROT_DIM = 32
ROPE_THETA = 500000.0
SSM_WIDTH = 512
SSM_GROUP_SIZE = 16
SSM_GROUPS = 32
SSM_STATE = 64
POOL_WIDTH = 512
POOL_WINDOWS = (2, 4, 8, 16)
POOL_GROUP = 128
POOL_BUF = 15
POOL_HALO = 16
IN_WIDTH = 2560
D_FF = 8192
EPS = 1e-6
PAST_LEN = 16384
LOG2E = math.log2(math.e)

HEAD_ROWS = BLOCK
SEQ0 = HEAD_ROWS - N_META
LANES = 128
SUBLANES = 8
SSM_LANE_BLOCKS = SSM_WIDTH // LANES
SSM_BLOCK_STATES = (LANES // SSM_GROUP_SIZE) * SSM_STATE
SSM_STATE_LANES = SSM_LANE_BLOCKS * 2 * SSM_BLOCK_STATES
VMEM_LIMIT = 56 * 1024 * 1024

TM_PROJ = 512
TM_FFN = 1024
TF_FFN = 512
TM_SEQ = 512
DEC_STEP = 8
SSM_POW_ROWS = (1, HEAD_ROWS // SUBLANES, TM_SEQ // SUBLANES)

BF16 = jnp.bfloat16
F32 = jnp.float32


def _params(*semantics):
    return pltpu.CompilerParams(dimension_semantics=semantics, vmem_limit_bytes=VMEM_LIMIT)


def _rms(x, g):
    return x * jax.lax.rsqrt(jnp.mean(x * x, axis=-1, keepdims=True) + EPS) * g


def _full(shape):
    n = len(shape)
    return pl.BlockSpec(shape, lambda *_: (0,) * n)


def _layer(shape, l):
    n = len(shape)
    return pl.BlockSpec((None, *shape), lambda *_: (l,) + (0,) * n)


VEC_LAYOUT = (("g_mix", D_MODEL), ("g_ffn", D_MODEL), ("g_out_attn", ATTN_WIDTH), ("D_skip", SSM_WIDTH),
              ("b_glu", SSM_WIDTH), ("g_out_ssm", SSM_WIDTH), ("pool_scale", POOL_WIDTH),
              ("g_out_pool", POOL_WIDTH), ("g_q", HEAD_DIM), ("g_k", HEAD_DIM))
VEC_WIDTH = dict(VEC_LAYOUT)
VEC_OFFSET = {name: sum(w for _, w in VEC_LAYOUT[:i]) for i, (name, _) in enumerate(VEC_LAYOUT)}
assert all(VEC_OFFSET[name] % w == 0 for name, w in VEC_LAYOUT)


def _vrow(name, l):
    w = VEC_WIDTH[name]
    return pl.BlockSpec((None, 1, w), lambda *_: (l, 0, VEC_OFFSET[name] // w))


def _inproj_kernel(x_ref, g_ref, w_ref, gq_ref, gk_ref, rc_ref, rs1_ref, rs2_ref, *rest):
    if len(rest) == 7:
        cast_in_ref, q_ref, k_ref, v_ref, u_ref, xp_ref, cast_out_ref = rest
        cast_out_ref[...] = cast_in_ref[...].astype(BF16)
    else:
        q_ref, k_ref, v_ref, u_ref, xp_ref = rest
    h = _rms(x_ref[...], g_ref[...]).astype(BF16)
    proj = jnp.dot(h, w_ref[...], preferred_element_type=F32)
    rc, rs1, rs2 = rc_ref[...], rs1_ref[...], rs2_ref[...]

    def head(t, g):
        t = _rms(t, g)
        return t * rc + pltpu.roll(t, 16, 1) * rs1 + pltpu.roll(t, LANES - 16, 1) * rs2

    for hd in range(N_HEADS):
        sl = slice(hd * HEAD_DIM, (hd + 1) * HEAD_DIM)
        q_ref[:, sl] = head(proj[:, sl], gq_ref[...])
    for hd in range(N_KV_HEADS):
        sl = slice(hd * HEAD_DIM, (hd + 1) * HEAD_DIM)
        k_ref[:, sl] = head(proj[:, ATTN_WIDTH + hd * HEAD_DIM:ATTN_WIDTH + (hd + 1) * HEAD_DIM], gk_ref[...])
    o2 = ATTN_WIDTH + KV_WIDTH
    o3 = o2 + KV_WIDTH
    o4 = o3 + SSM_WIDTH
    v_ref[...] = proj[:, o2:o3]
    for j in range(SSM_LANE_BLOCKS):
        u_ref[j] = proj[:, o3 + j * LANES:o3 + (j + 1) * LANES]
    xp_ref[...] = proj[:, o4:]


def _cast_side_job(w, l, steps):
    _, r, c = w.shape
    rows = r // steps
    return (pl.BlockSpec((None, rows, c), lambda i: (l, i, 0)), pl.BlockSpec((rows, c), lambda i: (i, 0)),
            jax.ShapeDtypeStruct((r, c), BF16))


def _inproj(x, g, w, l, gq, gk, rope, tm, cast=None):
    R = x.shape[0]
    tiles_per_rope = rope[0].shape[0] // tm
    row = lambda i: (i, 0)
    rrow = lambda i: (i % tiles_per_rope, 0)
    flat = lambda w_: (pl.BlockSpec((tm, w_), row), jax.ShapeDtypeStruct((R, w_), F32))
    u_out = (pl.BlockSpec((SSM_LANE_BLOCKS, tm, LANES), lambda i: (0, i, 0)),
             jax.ShapeDtypeStruct((SSM_LANE_BLOCKS, R, LANES), F32))
    outs = [flat(ATTN_WIDTH), flat(KV_WIDTH), flat(KV_WIDTH), u_out, flat(POOL_WIDTH)]
    w_spec = pl.BlockSpec((D_MODEL, IN_WIDTH), lambda i: (0, 0), pipeline_mode=pl.Buffered(1))
    in_specs = [pl.BlockSpec((tm, D_MODEL), row), _vrow("g_mix", l), w_spec,
                _vrow("g_q", l), _vrow("g_k", l),
                pl.BlockSpec((tm, LANES), rrow), pl.BlockSpec((tm, LANES), rrow),
                pl.BlockSpec((tm, LANES), rrow)]
    args = [x, g, w, gq, gk, *rope]
    if cast is not None:
        c_in, c_out, c_shape = _cast_side_job(cast, l, R // tm)
        in_specs.append(c_in)
        args.append(cast)
        outs.append((c_out, c_shape))
    return pl.pallas_call(
        _inproj_kernel,
        grid=(R // tm,),
        in_specs=in_specs,
        out_specs=[o[0] for o in outs],
        out_shape=[o[1] for o in outs],
        compiler_params=_params("arbitrary"),
        name="inproj",
    )(*args)


def _attn_bias():
    rows = GQA_GROUP * BLOCK
    i, r, c = np.meshgrid(np.arange(3), np.arange(rows) % BLOCK, np.arange(2 * BLOCK), indexing="ij")
    diff = BLOCK + r - c
    krow = (i - 1) * BLOCK + c
    mask = (diff >= 0) & (diff <= WINDOW) & (krow >= SEQ0)
    return np.where(mask, 0.0, -np.inf).astype(np.float32)


def _attn_block(q_blk, kp_blk, kc_blk, vp_blk, vc_blk, bias, sink_ref, l, g):
    rows = GQA_GROUP * BLOCK
    rgrp = jax.lax.broadcasted_iota(jnp.int32, (rows, 1), 0) // BLOCK
    outs = []
    for kh in range(N_KV_HEADS):
        ksl = slice(kh * HEAD_DIM, (kh + 1) * HEAD_DIM)
        qh = jnp.concatenate(
            [q_blk[:, (kh * GQA_GROUP + h) * HEAD_DIM:(kh * GQA_GROUP + h + 1) * HEAD_DIM]
             for h in range(GQA_GROUP)], axis=0).astype(BF16)
        kk = jnp.concatenate([kp_blk[:, ksl], kc_blk[:, ksl]], axis=0).astype(BF16)
        vv = jnp.concatenate([vp_blk[:, ksl], vc_blk[:, ksl]], axis=0).astype(BF16)
        s = jax.lax.dot_general(qh, kk, (((1,), (1,)), ((), ())),
                                preferred_element_type=F32) * (HEAD_DIM ** -0.5 * LOG2E) + bias
        sk = jnp.zeros((rows, 1), F32)
        for h in range(GQA_GROUP):
            sk = jnp.where(rgrp == h, sink_ref[l * N_HEADS + kh * GQA_GROUP + h] * LOG2E, sk)
        m = jnp.maximum(jnp.max(s, axis=-1, keepdims=True), sk)
        p = jnp.exp2(s - m)
        denom = jnp.sum(p, axis=-1, keepdims=True) + jnp.exp2(sk - m)
        o = jnp.dot(p.astype(BF16), vv, preferred_element_type=F32) / denom
        outs.extend(o[h * BLOCK:(h + 1) * BLOCK] for h in range(GQA_GROUP))
    return _rms(jnp.concatenate(outs, axis=1), g)


def _attn_kernel(sink_ref, qm_ref, qh_ref, kpm_ref, kcm_ref, kh_ref, vpm_ref, vcm_ref, vh_ref, g_ref,
                 bias_a_ref, bias_b_ref, *rest, l):
    s = pl.program_id(1)
    if len(rest) == 4:
        cast_in_ref, om_ref, oh_ref, cast_out_ref = rest

        @pl.when(s > 0)
        def _():
            cast_out_ref[...] = cast_in_ref[...].astype(BF16)
    else:
        om_ref, oh_ref = rest
    blk = functools.partial(_attn_block, sink_ref=sink_ref, l=l, g=g_ref[...])

    @pl.when(s == 0)
    def _():
        a = blk(qh_ref[...], kpm_ref[...], kh_ref[...], vpm_ref[...], vh_ref[...], bias_a_ref[...])
        oh_ref[...] = a.astype(oh_ref.dtype)

    @pl.when(s > 0)
    def _():
        k1, k2 = kcm_ref[0:BLOCK, :], kcm_ref[BLOCK:2 * BLOCK, :]
        v1, v2 = vcm_ref[0:BLOCK, :], vcm_ref[BLOCK:2 * BLOCK, :]
        kp = jnp.where(s == 1, kh_ref[...], kpm_ref[...])
        vp = jnp.where(s == 1, vh_ref[...], vpm_ref[...])
        a1 = blk(qm_ref[0:BLOCK, :], kp, k1, vp, v1, bias_a_ref[...])
        a2 = blk(qm_ref[BLOCK:2 * BLOCK, :], k1, k2, v1, v2, bias_b_ref[...])
        om_ref[0:BLOCK, :] = a1.astype(om_ref.dtype)
        om_ref[BLOCK:2 * BLOCK, :] = a2.astype(om_ref.dtype)


def _attn(sinks, l, qm, qh, km, kh, vm, vh, g, bias, B, cast=None, cast_layer=0):
    nb = qm.shape[0] // (B * BLOCK)
    npair = nb // 2
    pair = lambda b, s, _: (b * npair + jnp.maximum(s - 1, 0), 0)
    prev = lambda b, s, _: (b * nb + jnp.maximum(2 * s - 3, 0), 0)
    head = lambda b, s, _: (b, 0)
    one = lambda w, im: pl.BlockSpec((BLOCK, w), im)
    two = lambda w: pl.BlockSpec((2 * BLOCK, w), pair)
    bias_spec = lambda im: pl.BlockSpec((None,) + bias.shape[1:], im)
    in_specs = [two(ATTN_WIDTH), one(ATTN_WIDTH, head),
                one(KV_WIDTH, prev), two(KV_WIDTH), one(KV_WIDTH, head),
                one(KV_WIDTH, prev), two(KV_WIDTH), one(KV_WIDTH, head),
                _vrow("g_out_attn", l),
                bias_spec(lambda b, s, _: (jnp.minimum(s, 2), 0, 0)),
                bias_spec(lambda b, s, _: (2, 0, 0))]
    args = [sinks, qm, qh, km, km, kh, vm, vm, vh, g, bias, bias]
    out_specs = [two(ATTN_WIDTH), one(ATTN_WIDTH, head)]
    out_shape = [jax.ShapeDtypeStruct(qm.shape, BF16), jax.ShapeDtypeStruct(qh.shape, BF16)]
    if cast is not None:
        _, cr, cc = cast.shape
        crows = cr // (B * npair)
        in_specs.append(pl.BlockSpec((None, crows, cc),
                                     lambda b, s, _: (cast_layer, b * npair + jnp.maximum(s - 1, 0), 0)))
        args.append(cast)
        out_specs.append(pl.BlockSpec((crows, cc), pair))
        out_shape.append(jax.ShapeDtypeStruct((cr, cc), BF16))
    return pl.pallas_call(
        functools.partial(_attn_kernel, l=l),
        grid_spec=pltpu.PrefetchScalarGridSpec(
            num_scalar_prefetch=1, grid=(B, npair + 1), in_specs=in_specs, out_specs=out_specs),
        out_shape=out_shape,
        compiler_params=_params("arbitrary", "arbitrary"),
        name="attn_prompt",
    )(*args)


def _ssm_params_kernel(ar_ref, ai_ref, ldt_ref, kk_ref, br_ref, bi_ref, tr_ref, ti_ref, bbr_ref, bbi_ref):
    ar, ai = ar_ref[...], ai_ref[...]
    dt = jnp.exp(ldt_ref[...])
    kk = kk_ref[...]
    mag = jnp.exp(dt * ar * kk)
    ang = dt * ai * kk
    tr = mag * jnp.cos(ang)
    ti = mag * jnp.sin(ang)
    tr_ref[...] = tr
    ti_ref[...] = ti
    abr, abi = tr[:, 0:1], ti[:, 0:1]
    den = ar * ar + ai * ai
    fr = ((abr - 1.0) * ar + abi * ai) / den
    fi = (abi * ar - (abr - 1.0) * ai) / den
    br, bi = br_ref[...], bi_ref[...]
    bbr_ref[...] = fr * br - fi * bi
    bbi_ref[...] = fr * bi + fi * br


def _ssm_params(A_re, A_im, log_dt, B_re, B_im, C_re, C_im):
    depth = A_re.shape[0]
    n = depth * SSM_GROUPS * SSM_STATE
    col = lambda t: t.astype(F32).reshape(n, 1)
    ldt = jnp.broadcast_to(log_dt.astype(F32)[:, :, None], (depth, SSM_GROUPS, SSM_STATE)).reshape(n, 1)
    kk = jnp.array([SSM_POW_ROWS + (0,) * (SUBLANES - len(SSM_POW_ROWS))], F32)
    shapes = [(n, SUBLANES), (n, SUBLANES), (n, SSM_GROUP_SIZE), (n, SSM_GROUP_SIZE)]
    tr, ti, bbr, bbi = pl.pallas_call(
        _ssm_params_kernel,
        out_shape=[jax.ShapeDtypeStruct(s, F32) for s in shapes],
        name="ssm_params",
    )(col(A_re), col(A_im), ldt, kk, B_re.astype(F32).reshape(n, SSM_GROUP_SIZE),
      B_im.astype(F32).reshape(n, SSM_GROUP_SIZE))
    J, G8 = SSM_LANE_BLOCKS, LANES // SSM_GROUP_SIZE

    def lanes(t):
        return t.reshape(depth, J, SSM_BLOCK_STATES, SUBLANES).transpose(0, 3, 1, 2)

    pw = jnp.concatenate([lanes(tr), lanes(ti)], axis=-1).reshape(depth, SUBLANES, SSM_STATE_LANES)
    eye = jnp.eye(G8, dtype=F32)

    def bdiag(t):
        t = t.reshape(depth, J, G8, SSM_STATE, SSM_GROUP_SIZE).transpose(0, 1, 2, 4, 3)
        t = t[:, :, :, :, None, :] * eye[None, None, :, None, :, None]
        return t.reshape(depth, J, LANES, SSM_BLOCK_STATES)

    bblk = jnp.concatenate([bdiag(bbr), bdiag(bbi)], axis=-1).astype(BF16)

    def cdiag(t):
        t = t.astype(F32).reshape(depth, J, G8, SSM_GROUP_SIZE, SSM_STATE).transpose(0, 1, 2, 4, 3)
        t = t[:, :, :, :, None, :] * eye[None, None, :, None, :, None]
        return t.reshape(depth, J, SSM_BLOCK_STATES, LANES)

    cblk = jnp.concatenate([cdiag(C_re), -cdiag(C_im)], axis=2).astype(BF16)
    return pw, bblk, cblk


def _ssm_tail(y, u, d_ref, wg_ref, bg_ref, g_ref):
    y = y + d_ref[...] * u
    z = jax.nn.gelu(y)
    gate = jax.nn.sigmoid(jnp.dot(z.astype(BF16), wg_ref[...], preferred_element_type=F32) + bg_ref[...])
    return _rms(z * gate, g_ref[...])


def _ssm_sweep(x_scr, n, a_tabs, init, store):
    S = SSM_BLOCK_STATES
    fins = []
    for j0 in range(0, SSM_LANE_BLOCKS, 2):
        js = (j0, j0 + 1)

        def body(k, carry, js=js):
            r0 = pl.multiple_of(k * SUBLANES, SUBLANES)
            out = []
            for idx, j in enumerate(js):
                hr, hi = carry[2 * idx], carry[2 * idx + 1]
                base = j * 2 * S
                ar, ai = a_tabs[j]
                nhr = ar * hr - ai * hi + x_scr[pl.ds(r0, SUBLANES), base:base + S]
                nhi = ar * hi + ai * hr + x_scr[pl.ds(r0, SUBLANES), base + S:base + 2 * S]
                if store:
                    x_scr[pl.ds(r0, SUBLANES), base:base + S] = nhr
                    x_scr[pl.ds(r0, SUBLANES), base + S:base + 2 * S] = nhi
                out += [nhr, nhi]
            return tuple(out)

        c0 = tuple(t for j in js for t in init[j])
        res = jax.lax.fori_loop(0, n // SUBLANES, body, c0, unroll=True)
        fins += [(res[0], res[1]), (res[2], res[3])]
    return fins


def _ssm_rows(u, pow_row, pw_ref, bblk_ref, cblk_ref, x_scr, s_scr, carry_scr):
    n = u.shape[0]
    S = SSM_BLOCK_STATES
    ub = u.astype(BF16)
    for j in range(SSM_LANE_BLOCKS):
        x_scr[0:n, j * 2 * S:(j + 1) * 2 * S] = jnp.dot(ub[:, j * LANES:(j + 1) * LANES], bblk_ref[j],
                                                       preferred_element_type=F32)
    bc = lambda t: jnp.broadcast_to(t, (SUBLANES, S))
    re = lambda ref, r0, r1, j: ref[r0:r1, j * 2 * S:j * 2 * S + S]
    im = lambda ref, r0, r1, j: ref[r0:r1, j * 2 * S + S:(j + 1) * 2 * S]
    a_tabs = [(bc(re(pw_ref, 0, 1, j)), bc(im(pw_ref, 0, 1, j))) for j in range(SSM_LANE_BLOCKS)]
    zero = jnp.zeros((SUBLANES, S), F32)
    fins = _ssm_sweep(x_scr, n, a_tabs, [(zero, zero)] * SSM_LANE_BLOCKS, store=False)
    for j in range(SSM_LANE_BLOCKS):
        base = j * 2 * S
        cr, ci = re(pw_ref, pow_row, pow_row + 1, j), im(pw_ref, pow_row, pow_row + 1, j)
        sr, si = re(carry_scr, 0, 1, j), im(carry_scr, 0, 1, j)
        fr, fi = fins[j]
        for c in range(SUBLANES):
            s_scr[c:c + 1, base:base + S] = sr
            s_scr[c:c + 1, base + S:base + 2 * S] = si
            sr, si = cr * sr - ci * si + fr[c:c + 1], cr * si + ci * sr + fi[c:c + 1]
        carry_scr[:, base:base + S] = bc(sr)
        carry_scr[:, base + S:base + 2 * S] = bc(si)
    init = [(re(s_scr, 0, SUBLANES, j), im(s_scr, 0, SUBLANES, j)) for j in range(SSM_LANE_BLOCKS)]
    _ssm_sweep(x_scr, n, a_tabs, init, store=True)
    ys = [jnp.dot(x_scr[0:n, j * 2 * S:(j + 1) * 2 * S].astype(BF16), cblk_ref[j], preferred_element_type=F32)
          for j in range(SSM_LANE_BLOCKS)]
    return jnp.concatenate(ys, axis=1)


def _ssm_tile(u_ref, n, pow_row, seq_start, refs, o_ref, scr):
    pw_ref, bblk_ref, cblk_ref, d_ref, wg_ref, bg_ref, g_ref = refs
    up_scr, x_scr, s_scr, carry_scr, o_scr = scr
    q = n // SUBLANES
    for j in range(SSM_LANE_BLOCKS):
        for k in range(q):
            up_scr[k * SUBLANES:(k + 1) * SUBLANES, j * LANES:(j + 1) * LANES] = \
                u_ref[j, pl.ds(k, SUBLANES, stride=q), :]
    u = up_scr[0:n, :]
    if seq_start:
        p = jax.lax.broadcasted_iota(jnp.int32, (n, 1), 0)
        u = jnp.where((p % SUBLANES) * q + p // SUBLANES >= seq_start, u, 0.0)
    y = _ssm_rows(u, pow_row, pw_ref, bblk_ref, cblk_ref, x_scr, s_scr, carry_scr)
    out = _ssm_tail(y, u, d_ref, wg_ref, bg_ref, g_ref)
    for j in range(SSM_LANE_BLOCKS):
        for k in range(q):
            o_scr[j, pl.ds(k, SUBLANES, stride=q), :] = out[k * SUBLANES:(k + 1) * SUBLANES,
                                                            j * LANES:(j + 1) * LANES]
    o_ref[...] = jnp.concatenate([o_scr[j, 0:n, :] for j in range(SSM_LANE_BLOCKS)], axis=1).astype(o_ref.dtype)


def _ssm_kernel(um_ref, uh_ref, pw_ref, bblk_ref, cblk_ref, d_ref, wg_ref, bg_ref, g_ref, cast_in_ref,
                om_ref, oh_ref, st_ref, cast_out_ref, up_scr, x_scr, s_scr, carry_scr, o_scr):
    t = pl.program_id(1)
    refs = (pw_ref, bblk_ref, cblk_ref, d_ref, wg_ref, bg_ref, g_ref)
    scr = (up_scr, x_scr, s_scr, carry_scr, o_scr)

    @pl.when(t == 0)
    def _():
        carry_scr[...] = jnp.zeros_like(carry_scr)
        _ssm_tile(uh_ref, HEAD_ROWS, 1, SEQ0, refs, oh_ref, scr)

    @pl.when(t > 0)
    def _():
        cast_out_ref[...] = cast_in_ref[...].astype(BF16)
        _ssm_tile(um_ref, um_ref.shape[1], 2, 0, refs, om_ref, scr)
        st_ref[...] = carry_scr[...]


def _ssm(um, uh, pw, bblk, cblk, d, wg, l, bg, g, B, tm, cast):
    assert (1, HEAD_ROWS // SUBLANES, tm // SUBLANES) == SSM_POW_ROWS
    J = SSM_LANE_BLOCKS
    rm, rh = um.shape[1], uh.shape[1]
    nt = rm // (B * tm)
    tile = lambda b, t: b * nt + jnp.maximum(t - 1, 0)
    main = lambda b, t: (tile(b, t), 0)
    head = lambda b, t: (b, 0)
    _, cr, cc = cast.shape
    crows = cr // (B * nt)
    return pl.pallas_call(
        _ssm_kernel,
        grid=(B, nt + 1),
        in_specs=[pl.BlockSpec((J, tm, LANES), lambda b, t: (0, tile(b, t), 0)),
                  pl.BlockSpec((J, HEAD_ROWS, LANES), lambda b, t: (0, b, 0)),
                  _layer(pw.shape[1:], l), _layer(bblk.shape[1:], l), _layer(cblk.shape[1:], l),
                  _vrow("D_skip", l), _layer((SSM_WIDTH, SSM_WIDTH), l), _vrow("b_glu", l),
                  _vrow("g_out_ssm", l),
                  pl.BlockSpec((None, crows, cc), lambda b, t: (l, tile(b, t), 0))],
        out_specs=[pl.BlockSpec((tm, SSM_WIDTH), main), pl.BlockSpec((HEAD_ROWS, SSM_WIDTH), head),
                   pl.BlockSpec((None, SUBLANES, SSM_STATE_LANES), lambda b, t: (b, 0, 0)),
                   pl.BlockSpec((crows, cc), main)],
        out_shape=[jax.ShapeDtypeStruct((rm, SSM_WIDTH), BF16), jax.ShapeDtypeStruct((rh, SSM_WIDTH), BF16),
                   jax.ShapeDtypeStruct((B, SUBLANES, SSM_STATE_LANES), F32),
                   jax.ShapeDtypeStruct((cr, cc), BF16)],
        scratch_shapes=[pltpu.VMEM((tm, SSM_WIDTH), F32),
                        pltpu.VMEM((tm, SSM_STATE_LANES), F32),
                        pltpu.VMEM((SUBLANES, SSM_STATE_LANES), F32),
                        pltpu.VMEM((SUBLANES, SSM_STATE_LANES), F32),
                        pltpu.VMEM((J, tm, LANES), F32)],
        compiler_params=_params("arbitrary", "arbitrary"),
        name="ssm_prompt",
    )(um, uh, pw, bblk, cblk, d, wg, bg, g, cast)


def _state_from_lanes(s):
    s = s.reshape(s.shape[0], SSM_LANE_BLOCKS, 2, SSM_BLOCK_STATES)
    return (s[:, :, 0].reshape(-1, SSM_GROUPS, SSM_STATE), s[:, :, 1].reshape(-1, SSM_GROUPS, SSM_STATE))


def _pool_tail(d_groups, w_ref, sc_ref, g_ref):
    y = jnp.concatenate(
        [jnp.dot(d.astype(BF16), w_ref[gi], preferred_element_type=F32) for gi, d in enumerate(d_groups)], axis=1)
    return _rms(y * sc_ref[...], g_ref[...])


def _pool_rows(x, prev, pos0, w_ref, sc_ref, g_ref):
    n = x.shape[0]
    xe = jnp.concatenate([prev, x], axis=0)
    pos = pos0 + jax.lax.broadcasted_iota(jnp.int32, (n, 1), 0)
    ds = []
    for gi, w in enumerate(POOL_WINDOWS):
        gsl = slice(gi * POOL_GROUP, (gi + 1) * POOL_GROUP)
        s = xe[:, gsl]
        k = 1
        while k < w:
            s = s + pltpu.roll(s, k, 0)
            k *= 2
        cnt = jnp.clip(pos + 1, 1, w).astype(F32)
        ds.append(s[POOL_HALO:] / cnt - x[:, gsl])
    return _pool_tail(ds, w_ref, sc_ref, g_ref)


def _pool_kernel(xm_ref, halo_ref, xh_ref, w_ref, sc_ref, g_ref, cast_in_ref, om_ref, oh_ref, cast_out_ref):
    t = pl.program_id(1)
    tm = xm_ref.shape[0]

    @pl.when(t == 0)
    def _():
        hrow = jax.lax.broadcasted_iota(jnp.int32, (HEAD_ROWS, 1), 0)
        x = jnp.where(hrow >= SEQ0, xh_ref[...], 0.0)
        prev = jnp.zeros((POOL_HALO, POOL_WIDTH), F32)
        oh_ref[...] = _pool_rows(x, prev, -SEQ0, w_ref, sc_ref, g_ref).astype(oh_ref.dtype)

    @pl.when(t > 0)
    def _():
        prev = jnp.where(t == 1, xh_ref[HEAD_ROWS - POOL_HALO:, :], halo_ref[...])
        om_ref[...] = _pool_rows(xm_ref[...], prev, N_META + (t - 1) * tm, w_ref, sc_ref, g_ref).astype(om_ref.dtype)
        cast_out_ref[...] = cast_in_ref[...].astype(BF16)


def _pool(xm, xh, w, l, sc, g, B, tm, cast):
    nt = xm.shape[0] // (B * tm)
    r = tm // POOL_HALO
    main = lambda b, t: (b * nt + jnp.maximum(t - 1, 0), 0)
    halo = lambda b, t: (jnp.maximum((b * nt + t - 1) * r - 1, 0), 0)
    head = lambda b, t: (b, 0)
    _, cr, cc = cast.shape
    crows = cr // (B * nt)
    return pl.pallas_call(
        _pool_kernel,
        grid=(B, nt + 1),
        in_specs=[pl.BlockSpec((tm, POOL_WIDTH), main), pl.BlockSpec((POOL_HALO, POOL_WIDTH), halo),
                  pl.BlockSpec((HEAD_ROWS, POOL_WIDTH), head),
                  _layer(w.shape[1:], l), _vrow("pool_scale", l), _vrow("g_out_pool", l),
                  pl.BlockSpec((None, crows, cc), lambda b, t: (l, b * nt + jnp.maximum(t - 1, 0), 0))],
        out_specs=[pl.BlockSpec((tm, POOL_WIDTH), main), pl.BlockSpec((HEAD_ROWS, POOL_WIDTH), head),
                   pl.BlockSpec((crows, cc), main)],
        out_shape=[jax.ShapeDtypeStruct(xm.shape, BF16), jax.ShapeDtypeStruct(xh.shape, BF16),
                   jax.ShapeDtypeStruct((cr, cc), BF16)],
        compiler_params=_params("arbitrary", "arbitrary"),
        name="pool_prompt",
    )(xm, xm, xh, w, sc, g, cast)


def _attn_sample_kernel(q_ref, kn_ref, vn_ref, kc_ref, vc_ref, sink_ref, g_ref, a_in_ref,
                        a_ref, nk_ref, nv_ref, acc_scr):
    del a_in_ref
    step = pl.program_id(0)
    scale = HEAD_DIM ** -0.5
    for bb in range(DEC_STEP):
        nk_ref[bb, 0:WINDOW - 1] = kc_ref[bb, 1:WINDOW]
        nv_ref[bb, 0:WINDOW - 1] = vc_ref[bb, 1:WINDOW]
        outs = []
        for kh in range(N_KV_HEADS):
            ksl = slice(kh * HEAD_DIM, (kh + 1) * HEAD_DIM)
            qh = jnp.concatenate(
                [q_ref[bb:bb + 1, (kh * GQA_GROUP + g) * HEAD_DIM:(kh * GQA_GROUP + g + 1) * HEAD_DIM]
                 for g in range(GQA_GROUP)], axis=0)
            kn = kn_ref[bb:bb + 1, ksl]
            vn = vn_ref[bb:bb + 1, ksl]
            nk_ref[bb, WINDOW - 1, kh:kh + 1, :] = kn
            nv_ref[bb, WINDOW - 1, kh:kh + 1, :] = vn
            kc = kc_ref[bb, :, kh, :]
            vc = vc_ref[bb, :, kh, :]
            sc = jax.lax.dot_general(qh.astype(BF16), kc.astype(BF16), (((1,), (1,)), ((), ())),
                                     preferred_element_type=F32) * scale
            sn = jnp.sum(qh.astype(BF16).astype(F32) * kn.astype(BF16).astype(F32), axis=-1, keepdims=True) * scale
            sk = sink_ref[kh]
            m = jnp.maximum(jnp.maximum(jnp.max(sc, axis=-1, keepdims=True), sn), sk)
            pc = jnp.exp(sc - m)
            pn = jnp.exp(sn - m)
            denom = jnp.sum(pc, axis=-1, keepdims=True) + pn + jnp.exp(sk - m)
            o = jnp.dot(pc.astype(BF16), vc.astype(BF16), preferred_element_type=F32)
            o = (o + pn.astype(BF16).astype(F32) * vn.astype(BF16).astype(F32)) / denom
            outs.extend(o[g:g + 1] for g in range(GQA_GROUP))
        a = jnp.concatenate(outs, axis=1)
        acc_scr[pl.ds(step * DEC_STEP + bb, 1), :] = _rms(a, g_ref[...])

    @pl.when(step == pl.num_programs(0) - 1)
    def _():
        a_ref[...] = acc_scr[...].astype(a_ref.dtype)


def _attn_sample(qh, kh, vh, cache_k, cache_v, l, sinks, g, ah, N):
    rows = lambda w: pl.BlockSpec((DEC_STEP, w), lambda s: (s, 0))
    cache = pl.BlockSpec((None, DEC_STEP, WINDOW, N_KV_HEADS, HEAD_DIM), lambda s: (l, s, 0, 0, 0))
    ncache = pl.BlockSpec((DEC_STEP, WINDOW, N_KV_HEADS, HEAD_DIM), lambda s: (s, 0, 0, 0))
    cshape = jax.ShapeDtypeStruct((N, WINDOW, N_KV_HEADS, HEAD_DIM), F32)
    return pl.pallas_call(
        _attn_sample_kernel,
        grid=(N // DEC_STEP,),
        in_specs=[rows(ATTN_WIDTH), rows(KV_WIDTH), rows(KV_WIDTH), cache, cache,
                  _layer((N_KV_HEADS, GQA_GROUP, 1), l), _vrow("g_out_attn", l),
                  pl.BlockSpec(memory_space=pl.ANY)],
        out_specs=[pl.BlockSpec((N, ATTN_WIDTH), lambda s: (0, 0)), ncache, ncache],
        out_shape=[jax.ShapeDtypeStruct(ah.shape, ah.dtype), cshape, cshape],
        scratch_shapes=[pltpu.VMEM((N, ATTN_WIDTH), F32)],
        input_output_aliases={7: 0},
        compiler_params=_params("arbitrary"),
        name="attn_sample",
    )(qh, kh, vh, cache_k, cache_v, sinks, g, ah)


def _mix_sample_kernel(u_ref, h0_ref, pw_ref, bblk_ref, cblk_ref, d_ref, wg_ref, bg_ref, gs_ref,
                       xp_ref, pb_ref, wp_ref, sc_ref, gp_ref, s_in_ref, p_in_ref, s_ref, st_ref, p_ref):
    del s_in_ref, p_in_ref
    S = SSM_BLOCK_STATES
    u = jnp.concatenate([u_ref[j] for j in range(SSM_LANE_BLOCKS)], axis=1)
    ub = u.astype(BF16)
    ys = []
    for j in range(SSM_LANE_BLOCKS):
        x = jnp.dot(ub[:, j * LANES:(j + 1) * LANES], bblk_ref[j], preferred_element_type=F32)
        base = j * 2 * S
        ar = pw_ref[0:1, base:base + S]
        ai = pw_ref[0:1, base + S:base + 2 * S]
        h0r = h0_ref[:, base:base + S]
        h0i = h0_ref[:, base + S:base + 2 * S]
        hr = x[:, 0:S] + ar * h0r - ai * h0i
        hi = x[:, S:] + ar * h0i + ai * h0r
        st_ref[:, base:base + S] = hr
        st_ref[:, base + S:base + 2 * S] = hi
        h = jnp.concatenate([hr, hi], axis=1).astype(BF16)
        ys.append(jnp.dot(h, cblk_ref[j], preferred_element_type=F32))
    s_ref[...] = _ssm_tail(jnp.concatenate(ys, axis=1), u, d_ref, wg_ref, bg_ref, gs_ref).astype(s_ref.dtype)

    xp = xp_ref[...]
    ds = []
    for gi, w in enumerate(POOL_WINDOWS):
        gsl = slice(gi * POOL_GROUP, (gi + 1) * POOL_GROUP)
        s = xp[:, gsl]
        for back in range(1, w):
            s = s + pb_ref[POOL_BUF - back][:, gsl]
        ds.append(s / float(w) - xp[:, gsl])
    p_ref[...] = _pool_tail(ds, wp_ref, sc_ref, gp_ref).astype(p_ref.dtype)


def _mix_sample(uh, h0, pw, bblk, cblk, d, wg, l, bg, gs, xph, pbuf, wp, _l, sc, gp, sh, ph):
    N = h0.shape[1]
    rows = lambda w: pl.BlockSpec((N, w), lambda i: (0, 0))
    anyspec = pl.BlockSpec(memory_space=pl.ANY)
    return pl.pallas_call(
        _mix_sample_kernel,
        grid=(1,),
        in_specs=[pl.BlockSpec((SSM_LANE_BLOCKS, N, LANES), lambda i: (0, 0, 0)),
                  _layer(h0.shape[1:], l), _layer(pw.shape[1:], l), _layer(bblk.shape[1:], l),
                  _layer(cblk.shape[1:], l),
                  _vrow("D_skip", l), _layer((SSM_WIDTH, SSM_WIDTH), l), _vrow("b_glu", l),
                  _vrow("g_out_ssm", l), rows(POOL_WIDTH), _layer(pbuf.shape[1:], l),
                  _layer(wp.shape[1:], l),
                  _vrow("pool_scale", l), _vrow("g_out_pool", l), anyspec, anyspec],
        out_specs=[rows(SSM_WIDTH), _full((N, SSM_STATE_LANES)), rows(POOL_WIDTH)],
        out_shape=[jax.ShapeDtypeStruct(sh.shape, sh.dtype), jax.ShapeDtypeStruct((N, SSM_STATE_LANES), F32),
                   jax.ShapeDtypeStruct(ph.shape, ph.dtype)],
        input_output_aliases={14: 0, 15: 2},
        compiler_params=_params("arbitrary"),
        name="mix_sample",
    )(uh, h0, pw, bblk, cblk, d, wg, bg, gs, xph, pbuf, wp, sc, gp, sh, ph)


def _outproj_kernel(x_ref, a_ref, s_ref, p_ref, w_ref, o_ref):
    o1 = ATTN_WIDTH
    o2 = o1 + SSM_WIDTH
    acc = x_ref[...]
    acc = acc + jnp.dot(a_ref[...], w_ref[0:o1, :], preferred_element_type=F32)
    acc = acc + jnp.dot(s_ref[...], w_ref[o1:o2, :], preferred_element_type=F32)
    acc = acc + jnp.dot(p_ref[...], w_ref[o2:, :], preferred_element_type=F32)
    o_ref[...] = acc


def _outproj(x, a, s, p, w, tm):
    R = x.shape[0]
    row = lambda i: (i, 0)
    return pl.pallas_call(
        _outproj_kernel,
        grid=(R // tm,),
        in_specs=[pl.BlockSpec((tm, D_MODEL), row), pl.BlockSpec((tm, ATTN_WIDTH), row),
                  pl.BlockSpec((tm, SSM_WIDTH), row), pl.BlockSpec((tm, POOL_WIDTH), row),
                  pl.BlockSpec((D_MODEL, D_MODEL), lambda i: (0, 0), pipeline_mode=pl.Buffered(1))],
        out_specs=pl.BlockSpec((tm, D_MODEL), row),
        out_shape=jax.ShapeDtypeStruct((R, D_MODEL), F32),
        compiler_params=_params("arbitrary"),
        name="outproj",
    )(x, a, s, p, w)


def _ffn_kernel(xm_ref, xh_ref, g_ref, w1_ref, w2_ref, om_ref, oh_ref, hm_scr, hh_scr):
    i = pl.program_id(0)
    f = pl.program_id(1)

    @pl.when(f == 0)
    def _():
        x = xm_ref[...]
        hm_scr[...] = _rms(x, g_ref[...]).astype(BF16)
        om_ref[...] = x

    @pl.when((f == 0) & (i == 0))
    def _():
        x = xh_ref[...]
        hh_scr[...] = _rms(x, g_ref[...]).astype(BF16)
        oh_ref[...] = x

    w1 = w1_ref[...]
    w2 = w2_ref[...]

    def mlp(h):
        h1 = jnp.dot(h, w1, preferred_element_type=F32)
        return jnp.dot(jnp.square(jnp.maximum(h1, 0.0)).astype(BF16), w2, preferred_element_type=F32)

    om_ref[...] += mlp(hm_scr[...])

    @pl.when(i == 0)
    def _():
        oh_ref[...] += mlp(hh_scr[...])


def _ffn(xm, xh, g, l, w1, w2, tm, tf):
    R, RH = xm.shape[0], xh.shape[0]
    return pl.pallas_call(
        _ffn_kernel,
        grid=(R // tm, D_FF // tf),
        in_specs=[pl.BlockSpec((tm, D_MODEL), lambda i, f: (i, 0)),
                  pl.BlockSpec((RH, D_MODEL), lambda i, f: (0, 0), pipeline_mode=pl.Buffered(1)),
                  _vrow("g_ffn", l),
                  pl.BlockSpec((D_MODEL, tf), lambda i, f: (0, f)),
                  pl.BlockSpec((tf, D_MODEL), lambda i, f: (f, 0))],
        out_specs=[pl.BlockSpec((tm, D_MODEL), lambda i, f: (i, 0)),
                   pl.BlockSpec((RH, D_MODEL), lambda i, f: (0, 0))],
        out_shape=[jax.ShapeDtypeStruct((R, D_MODEL), F32), jax.ShapeDtypeStruct((RH, D_MODEL), F32)],
        scratch_shapes=[pltpu.VMEM((tm, D_MODEL), BF16), pltpu.VMEM((RH, D_MODEL), BF16)],
        compiler_params=_params("arbitrary", "arbitrary"),
        name="ffn",
    )(xm, xh, g, w1, w2)


def _rope_tables(pos):
    half = ROT_DIM // 2
    inv = ROPE_THETA ** (-np.arange(0, ROT_DIM, 2, dtype=np.float64) / ROT_DIM)
    ang = np.asarray(pos, np.float64)[:, None] * inv
    cos, sin = np.cos(ang), np.sin(ang)
    n = ang.shape[0]
    z = np.zeros((n, HEAD_DIM - ROT_DIM))
    zh = np.zeros((n, half))
    rc = np.concatenate([cos, cos, z + 1.0], axis=1)
    rs1 = np.concatenate([zh, sin, z], axis=1)
    rs2 = np.concatenate([-sin, zh, z], axis=1)
    return tuple(t.astype(np.float32) for t in (rc, rs1, rs2))


def kernel(x_prompt, x_sample, cache_k, cache_v, state_ssm_re, state_ssm_im, state_pool, meta_tokens, g_mix, w_in, g_q, g_k, sinks, A_re, A_im, log_dt, B_re, B_im, C_re, C_im, D_skip, w_glu, b_glu, w_pool, pool_scale, g_out_attn, g_out_ssm, g_out_pool, w_out, g_ffn, w_ff1, w_ff2):
    B, T, _ = x_prompt.shape
    N = x_sample.shape[0]
    depth = w_in.shape[0]
    assert N <= SEQ0 and N % DEC_STEP == 0 and T % TM_FFN == 0
    meta = meta_tokens.astype(F32)
    head0 = jnp.concatenate([x_sample.reshape(N, D_MODEL), jnp.zeros((SEQ0 - N, D_MODEL), F32), meta], axis=0)
    head_rest = jnp.concatenate([jnp.zeros((SEQ0, D_MODEL), F32), meta], axis=0)
    xh = jnp.concatenate([head0] + [head_rest] * (B - 1), axis=0)
    xm = x_prompt.reshape(B * T, D_MODEL)

    rope_m = _rope_tables(N_META + np.arange(T))
    hr_ = np.arange(HEAD_ROWS)
    pos_h0 = np.where(hr_ < N, PAST_LEN, np.maximum(hr_ - SEQ0, 0))
    pos_h = np.concatenate([pos_h0] + [np.maximum(hr_ - SEQ0, 0)] * (B - 1))
    rope_h = _rope_tables(pos_h)

    wi = w_in[0].astype(BF16)
    wg_all, wp_all = w_glu.astype(BF16), w_pool.astype(BF16)
    ck = cache_k.astype(F32)
    cv = cache_v.astype(F32)
    RH = B * HEAD_ROWS

    named = dict(g_mix=g_mix, g_ffn=g_ffn, g_out_attn=g_out_attn, D_skip=D_skip, b_glu=b_glu,
                 g_out_ssm=g_out_ssm, pool_scale=pool_scale, g_out_pool=g_out_pool, g_q=g_q, g_k=g_k)
    vecs = jnp.concatenate([named[name].astype(F32) for name, _ in VEC_LAYOUT], axis=1)[:, None, :]
    pw, bblk, cblk = _ssm_params(A_re, A_im, log_dt, B_re, B_im, C_re, C_im)
    ssm_w = lambda l: (pw, bblk, cblk, vecs, wg_all, l, vecs, vecs)
    pool_w = lambda l: (wp_all, l, vecs, vecs)
    g_attn = vecs
    g_ffn_ = vecs
    sinks_flat = sinks.astype(F32).reshape(depth * N_HEADS)
    sinks_col = sinks.astype(F32).reshape(depth, N_KV_HEADS, GQA_GROUP, 1)
    bias = _attn_bias()
    h0 = jnp.concatenate([state_ssm_re.astype(F32).reshape(depth, N, SSM_LANE_BLOCKS, SSM_BLOCK_STATES),
                          state_ssm_im.astype(F32).reshape(depth, N, SSM_LANE_BLOCKS, SSM_BLOCK_STATES)],
                         axis=-1).reshape(depth, N, SSM_STATE_LANES)
    pbuf = state_pool.astype(F32).transpose(0, 2, 1, 3)

    ks, vs, pls, sts, nks, nvs, st_ss, phs = ([] for _ in range(8))
    for l in range(depth):
        in_w = (vecs, wi, l, vecs, vecs)
        qm, km, vm, um, pm, w1 = _inproj(xm, *in_w, rope_m, TM_PROJ, cast=w_ff1)
        qh, kh, vh, uh, ph = _inproj(xh, *in_w, rope_h, RH)

        if l + 1 < depth:
            am, ah, wi = _attn(sinks_flat, l, qm, qh, km, kh, vm, vh, g_attn, bias, B, cast=w_in, cast_layer=l + 1)
        else:
            am, ah = _attn(sinks_flat, l, qm, qh, km, kh, vm, vh, g_attn, bias, B)
        sm, sh, st, w2 = _ssm(um, uh, *ssm_w(l), B, TM_SEQ, w_ff2)
        plm, plh, wo = _pool(pm, ph, *pool_w(l), B, TM_SEQ, w_out)

        ah, nk, nv = _attn_sample(qh, kh, vh, ck, cv, l, sinks_col, g_attn, ah, N)
        sh, st_s, plh = _mix_sample(uh, h0, *ssm_w(l), ph, pbuf, *pool_w(l), sh, plh)

        xm = _outproj(xm, am, sm, plm, wo, TM_PROJ)
        xh = _outproj(xh, ah, sh, plh, wo, RH)
        xm, xh = _ffn(xm, xh, g_ffn_, l, w1, w2, TM_FFN, TF_FFN)

        ks.append(km.reshape(B, T, KV_WIDTH)[:, T - WINDOW:])
        vs.append(vm.reshape(B, T, KV_WIDTH)[:, T - WINDOW:])
        pls.append(pm.reshape(B, T, POOL_WIDTH)[:, T - POOL_BUF:])
        sts.append(st[:, 0])
        nks.append(nk)
        nvs.append(nv)
        st_ss.append(st_s)
        phs.append(ph[:N])

    y_prompt = xm.reshape(B, T, D_MODEL)
    y_sample = xh[:N].reshape(N, 1, D_MODEL)
    heads = lambda t: jnp.stack(t).reshape(depth, -1, WINDOW, N_KV_HEADS, HEAD_DIM)
    p_re, p_im = _state_from_lanes(jnp.stack(sts).reshape(depth * B, SSM_STATE_LANES))
    s_re, s_im = _state_from_lanes(jnp.stack(st_ss).reshape(depth * N, SSM_STATE_LANES))
    st4 = lambda t, n: t.reshape(depth, n, SSM_GROUPS, SSM_STATE)
    s_pool = jnp.concatenate([state_pool.astype(F32)[:, :, 1:], jnp.stack(phs)[:, :, None]], axis=2)
    return (y_prompt, y_sample, heads(ks), heads(vs), st4(p_re, B), st4(p_im, B), jnp.stack(pls),
            jnp.stack(nks), jnp.stack(nvs), st4(s_re, N), st4(s_im, N), s_pool)
```

```python
import functools
import math

import jax
import jax.numpy as jnp
import numpy as np
from jax.experimental import pallas as pl
from jax.experimental.pallas import tpu as pltpu

D_MODEL = 2048
N_META = 16
HEAD_DIM = 128
N_HEADS = 8
N_KV_HEADS = 2
GQA_GROUP = 4
ATTN_WIDTH = 1024
KV_WIDTH = 256
WINDOW = 128
BLOCK = 128
ROT_DIM = 32
ROPE_THETA = 500000.0
SSM_WIDTH = 512
SSM_GROUP_SIZE = 16
SSM_GROUPS = 32
SSM_STATE = 64
POOL_WIDTH = 512
POOL_WINDOWS = (2, 4, 8, 16)
POOL_GROUP = 128
POOL_BUF = 15
POOL_HALO = 16
IN_WIDTH = 2560
D_FF = 8192
EPS = 1e-6
PAST_LEN = 16384
LOG2E = math.log2(math.e)

HEAD_ROWS = BLOCK
SEQ0 = HEAD_ROWS - N_META
LANES = 128
SUBLANES = 8
SSM_LANE_BLOCKS = SSM_WIDTH // LANES
SSM_BLOCK_STATES = (LANES // SSM_GROUP_SIZE) * SSM_STATE
SSM_STATE_LANES = SSM_LANE_BLOCKS * 2 * SSM_BLOCK_STATES
VMEM_LIMIT = 56 * 1024 * 1024

TM_PROJ = 512
TM_FFN = 1024
TF_FFN = 512
TM_SEQ = 512
DEC_STEP = 8
SSM_POW_ROWS = (1, HEAD_ROWS // SUBLANES, TM_SEQ // SUBLANES)

BF16 = jnp.bfloat16
F32 = jnp.float32


def _params(*semantics):
    return pltpu.CompilerParams(dimension_semantics=semantics, vmem_limit_bytes=VMEM_LIMIT)


def _rms(x, g):
    return x * jax.lax.rsqrt(jnp.mean(x * x, axis=-1, keepdims=True) + EPS) * g


def _full(shape):
    n = len(shape)
    return pl.BlockSpec(shape, lambda *_: (0,) * n)


def _layer(shape, l):
    n = len(shape)
    return pl.BlockSpec((None, *shape), lambda *_: (l,) + (0,) * n)


VEC_LAYOUT = (("g_mix", D_MODEL), ("g_ffn", D_MODEL), ("g_out_attn", ATTN_WIDTH), ("D_skip", SSM_WIDTH),
              ("b_glu", SSM_WIDTH), ("g_out_ssm", SSM_WIDTH), ("pool_scale", POOL_WIDTH),
              ("g_out_pool", POOL_WIDTH), ("g_q", HEAD_DIM), ("g_k", HEAD_DIM))
VEC_WIDTH = dict(VEC_LAYOUT)
VEC_OFFSET = {name: sum(w for _, w in VEC_LAYOUT[:i]) for i, (name, _) in enumerate(VEC_LAYOUT)}
assert all(VEC_OFFSET[name] % w == 0 for name, w in VEC_LAYOUT)


def _vrow(name, l):
    w = VEC_WIDTH[name]
    return pl.BlockSpec((None, 1, w), lambda *_: (l, 0, VEC_OFFSET[name] // w))


def _inproj_kernel(x_ref, g_ref, w_ref, gq_ref, gk_ref, rc_ref, rs1_ref, rs2_ref, *rest):
    if len(rest) == 7:
        cast_in_ref, q_ref, k_ref, v_ref, u_ref, xp_ref, cast_out_ref = rest
        cast_out_ref[...] = cast_in_ref[...].astype(BF16)
    else:
        q_ref, k_ref, v_ref, u_ref, xp_ref = rest
    h = _rms(x_ref[...], g_ref[...]).astype(BF16)
    proj = jnp.dot(h, w_ref[...], preferred_element_type=F32)
    rc, rs1, rs2 = rc_ref[...], rs1_ref[...], rs2_ref[...]

    def head(t, g):
        t = _rms(t, g)
        return t * rc + pltpu.roll(t, 16, 1) * rs1 + pltpu.roll(t, LANES - 16, 1) * rs2

    for hd in range(N_HEADS):
        sl = slice(hd * HEAD_DIM, (hd + 1) * HEAD_DIM)
        q_ref[:, sl] = head(proj[:, sl], gq_ref[...])
    for hd in range(N_KV_HEADS):
        sl = slice(hd * HEAD_DIM, (hd + 1) * HEAD_DIM)
        k_ref[:, sl] = head(proj[:, ATTN_WIDTH + hd * HEAD_DIM:ATTN_WIDTH + (hd + 1) * HEAD_DIM], gk_ref[...])
    o2 = ATTN_WIDTH + KV_WIDTH
    o3 = o2 + KV_WIDTH
    o4 = o3 + SSM_WIDTH
    v_ref[...] = proj[:, o2:o3]
    for j in range(SSM_LANE_BLOCKS):
        u_ref[j] = proj[:, o3 + j * LANES:o3 + (j + 1) * LANES]
    xp_ref[...] = proj[:, o4:]


def _cast_side_job(w, l, steps):
    _, r, c = w.shape
    rows = r // steps
    return (pl.BlockSpec((None, rows, c), lambda i: (l, i, 0)), pl.BlockSpec((rows, c), lambda i: (i, 0)),
            jax.ShapeDtypeStruct((r, c), BF16))


def _inproj(x, g, w, l, gq, gk, rope, tm, cast=None):
    R = x.shape[0]
    tiles_per_rope = rope[0].shape[0] // tm
    row = lambda i: (i, 0)
    rrow = lambda i: (i % tiles_per_rope, 0)
    flat = lambda w_: (pl.BlockSpec((tm, w_), row), jax.ShapeDtypeStruct((R, w_), F32))
    u_out = (pl.BlockSpec((SSM_LANE_BLOCKS, tm, LANES), lambda i: (0, i, 0)),
             jax.ShapeDtypeStruct((SSM_LANE_BLOCKS, R, LANES), F32))
    outs = [flat(ATTN_WIDTH), flat(KV_WIDTH), flat(KV_WIDTH), u_out, flat(POOL_WIDTH)]
    w_spec = pl.BlockSpec((D_MODEL, IN_WIDTH), lambda i: (0, 0), pipeline_mode=pl.Buffered(1))
    in_specs = [pl.BlockSpec((tm, D_MODEL), row), _vrow("g_mix", l), w_spec,
                _vrow("g_q", l), _vrow("g_k", l),
                pl.BlockSpec((tm, LANES), rrow), pl.BlockSpec((tm, LANES), rrow),
                pl.BlockSpec((tm, LANES), rrow)]
    args = [x, g, w, gq, gk, *rope]
    if cast is not None:
        c_in, c_out, c_shape = _cast_side_job(cast, l, R // tm)
        in_specs.append(c_in)
        args.append(cast)
        outs.append((c_out, c_shape))
    return pl.pallas_call(
        _inproj_kernel,
        grid=(R // tm,),
        in_specs=in_specs,
        out_specs=[o[0] for o in outs],
        out_shape=[o[1] for o in outs],
        compiler_params=_params("arbitrary"),
        name="inproj",
    )(*args)


def _attn_bias():
    rows = GQA_GROUP * BLOCK
    i, r, c = np.meshgrid(np.arange(3), np.arange(rows) % BLOCK, np.arange(2 * BLOCK), indexing="ij")
    diff = BLOCK + r - c
    krow = (i - 1) * BLOCK + c
    mask = (diff >= 0) & (diff <= WINDOW) & (krow >= SEQ0)
    return np.where(mask, 0.0, -np.inf).astype(np.float32)


def _attn_block(q_blk, kp_blk, kc_blk, vp_blk, vc_blk, bias, sink_ref, l, g):
    rows = GQA_GROUP * BLOCK
    rgrp = jax.lax.broadcasted_iota(jnp.int32, (rows, 1), 0) // BLOCK
    outs = []
    for kh in range(N_KV_HEADS):
        ksl = slice(kh * HEAD_DIM, (kh + 1) * HEAD_DIM)
        qh = jnp.concatenate(
            [q_blk[:, (kh * GQA_GROUP + h) * HEAD_DIM:(kh * GQA_GROUP + h + 1) * HEAD_DIM]
             for h in range(GQA_GROUP)], axis=0).astype(BF16)
        kk = jnp.concatenate([kp_blk[:, ksl], kc_blk[:, ksl]], axis=0).astype(BF16)
        vv = jnp.concatenate([vp_blk[:, ksl], vc_blk[:, ksl]], axis=0).astype(BF16)
        s = jax.lax.dot_general(qh, kk, (((1,), (1,)), ((), ())),
                                preferred_element_type=F32) * (HEAD_DIM ** -0.5 * LOG2E) + bias
        sk = jnp.zeros((rows, 1), F32)
        for h in range(GQA_GROUP):
            sk = jnp.where(rgrp == h, sink_ref[l * N_HEADS + kh * GQA_GROUP + h] * LOG2E, sk)
        m = jnp.maximum(jnp.max(s, axis=-1, keepdims=True), sk)
        p = jnp.exp2(s - m)
        denom = jnp.sum(p, axis=-1, keepdims=True) + jnp.exp2(sk - m)
        o = jnp.dot(p.astype(BF16), vv, preferred_element_type=F32) / denom
        outs.extend(o[h * BLOCK:(h + 1) * BLOCK] for h in range(GQA_GROUP))
    return _rms(jnp.concatenate(outs, axis=1), g)


def _attn_kernel(sink_ref, qm_ref, qh_ref, kpm_ref, kcm_ref, kh_ref, vpm_ref, vcm_ref, vh_ref, g_ref,
                 bias_a_ref, bias_b_ref, *rest, l):
    s = pl.program_id(1)
    if len(rest) == 4:
        cast_in_ref, om_ref, oh_ref, cast_out_ref = rest

        @pl.when(s > 0)
        def _():
            cast_out_ref[...] = cast_in_ref[...].astype(BF16)
    else:
        om_ref, oh_ref = rest
    blk = functools.partial(_attn_block, sink_ref=sink_ref, l=l, g=g_ref[...])

    @pl.when(s == 0)
    def _():
        a = blk(qh_ref[...], kpm_ref[...], kh_ref[...], vpm_ref[...], vh_ref[...], bias_a_ref[...])
        oh_ref[...] = a.astype(oh_ref.dtype)

    @pl.when(s > 0)
    def _():
        k1, k2 = kcm_ref[0:BLOCK, :], kcm_ref[BLOCK:2 * BLOCK, :]
        v1, v2 = vcm_ref[0:BLOCK, :], vcm_ref[BLOCK:2 * BLOCK, :]
        kp = jnp.where(s == 1, kh_ref[...], kpm_ref[...])
        vp = jnp.where(s == 1, vh_ref[...], vpm_ref[...])
        a1 = blk(qm_ref[0:BLOCK, :], kp, k1, vp, v1, bias_a_ref[...])
        a2 = blk(qm_ref[BLOCK:2 * BLOCK, :], k1, k2, v1, v2, bias_b_ref[...])
        om_ref[0:BLOCK, :] = a1.astype(om_ref.dtype)
        om_ref[BLOCK:2 * BLOCK, :] = a2.astype(om_ref.dtype)


def _attn(sinks, l, qm, qh, km, kh, vm, vh, g, bias, B, cast=None, cast_layer=0):
    nb = qm.shape[0] // (B * BLOCK)
    npair = nb // 2
    pair = lambda b, s, _: (b * npair + jnp.maximum(s - 1, 0), 0)
    prev = lambda b, s, _: (b * nb + jnp.maximum(2 * s - 3, 0), 0)
    head = lambda b, s, _: (b, 0)
    one = lambda w, im: pl.BlockSpec((BLOCK, w), im)
    two = lambda w: pl.BlockSpec((2 * BLOCK, w), pair)
    bias_spec = lambda im: pl.BlockSpec((None,) + bias.shape[1:], im)
    in_specs = [two(ATTN_WIDTH), one(ATTN_WIDTH, head),
                one(KV_WIDTH, prev), two(KV_WIDTH), one(KV_WIDTH, head),
                one(KV_WIDTH, prev), two(KV_WIDTH), one(KV_WIDTH, head),
                _vrow("g_out_attn", l),
                bias_spec(lambda b, s, _: (jnp.minimum(s, 2), 0, 0)),
                bias_spec(lambda b, s, _: (2, 0, 0))]
    args = [sinks, qm, qh, km, km, kh, vm, vm, vh, g, bias, bias]
    out_specs = [two(ATTN_WIDTH), one(ATTN_WIDTH, head)]
    out_shape = [jax.ShapeDtypeStruct(qm.shape, BF16), jax.ShapeDtypeStruct(qh.shape, BF16)]
    if cast is not None:
        _, cr, cc = cast.shape
        crows = cr // (B * npair)
        in_specs.append(pl.BlockSpec((None, crows, cc),
                                     lambda b, s, _: (cast_layer, b * npair + jnp.maximum(s - 1, 0), 0)))
        args.append(cast)
        out_specs.append(pl.BlockSpec((crows, cc), pair))
        out_shape.append(jax.ShapeDtypeStruct((cr, cc), BF16))
    return pl.pallas_call(
        functools.partial(_attn_kernel, l=l),
        grid_spec=pltpu.PrefetchScalarGridSpec(
            num_scalar_prefetch=1, grid=(B, npair + 1), in_specs=in_specs, out_specs=out_specs),
        out_shape=out_shape,
        compiler_params=_params("arbitrary", "arbitrary"),
        name="attn_prompt",
    )(*args)


def _ssm_params_kernel(ar_ref, ai_ref, ldt_ref, kk_ref, br_ref, bi_ref, tr_ref, ti_ref, bbr_ref, bbi_ref):
    ar, ai = ar_ref[...], ai_ref[...]
    dt = jnp.exp(ldt_ref[...])
    kk = kk_ref[...]
    mag = jnp.exp(dt * ar * kk)
    ang = dt * ai * kk
    tr = mag * jnp.cos(ang)
    ti = mag * jnp.sin(ang)
    tr_ref[...] = tr
    ti_ref[...] = ti
    abr, abi = tr[0:1], ti[0:1]
    den = ar * ar + ai * ai
    fr = ((abr - 1.0) * ar + abi * ai) / den
    fi = (abi * ar - (abr - 1.0) * ai) / den
    br, bi = br_ref[...], bi_ref[...]
    bbr_ref[...] = fr * br - fi * bi
    bbi_ref[...] = fr * bi + fi * br


def _ssm_params(A_re, A_im, log_dt, B_re, B_im, C_re, C_im):
    depth = A_re.shape[0]
    n = depth * SSM_GROUPS * SSM_STATE
    row = lambda t: t.astype(F32).reshape(1, n)
    ldt = jnp.broadcast_to(log_dt.astype(F32)[:, :, None], (depth, SSM_GROUPS, SSM_STATE)).reshape(1, n)
    kk = jnp.array(SSM_POW_ROWS + (0,) * (SUBLANES - len(SSM_POW_ROWS)), F32).reshape(SUBLANES, 1)
    chan_first = lambda t: t.astype(F32).reshape(n, SSM_GROUP_SIZE).T
    shapes = [(SUBLANES, n), (SUBLANES, n), (SSM_GROUP_SIZE, n), (SSM_GROUP_SIZE, n)]
    tr, ti, bbr, bbi = pl.pallas_call(
        _ssm_params_kernel,
        out_shape=[jax.ShapeDtypeStruct(s, F32) for s in shapes],
        name="ssm_params",
    )(row(A_re), row(A_im), ldt, kk, chan_first(B_re), chan_first(B_im))
    J, G8 = SSM_LANE_BLOCKS, LANES // SSM_GROUP_SIZE

    def lanes(t):
        return t.reshape(SUBLANES, depth, J, SSM_BLOCK_STATES).transpose(1, 0, 2, 3)

    pw = jnp.concatenate([lanes(tr), lanes(ti)], axis=-1).reshape(depth, SUBLANES, SSM_STATE_LANES)
    eye = jnp.eye(G8, dtype=F32)

    def bdiag(t):
        t = t.reshape(SSM_GROUP_SIZE, depth, J, G8, SSM_STATE).transpose(1, 2, 3, 0, 4)
        t = t[:, :, :, :, None, :] * eye[None, None, :, None, :, None]
        return t.reshape(depth, J, LANES, SSM_BLOCK_STATES)

    bblk = jnp.concatenate([bdiag(bbr), bdiag(bbi)], axis=-1).astype(BF16)

    def cdiag(t):
        t = t.astype(F32).reshape(depth, J, G8, SSM_GROUP_SIZE, SSM_STATE).transpose(0, 1, 2, 4, 3)
        t = t[:, :, :, :, None, :] * eye[None, None, :, None, :, None]
        return t.reshape(depth, J, SSM_BLOCK_STATES, LANES)

    cblk = jnp.concatenate([cdiag(C_re), -cdiag(C_im)], axis=2).astype(BF16)
    return pw, bblk, cblk


def _ssm_tail(y, u, d_ref, wg_ref, bg_ref, g_ref):
    y = y + d_ref[...] * u
    z = jax.nn.gelu(y)
    gate = jax.nn.sigmoid(jnp.dot(z.astype(BF16), wg_ref[...], preferred_element_type=F32) + bg_ref[...])
    return _rms(z * gate, g_ref[...])


def _ssm_sweep(x_scr, n, a_tabs, init, store):
    S = SSM_BLOCK_STATES
    fins = []
    for j0 in range(0, SSM_LANE_BLOCKS, 2):
        js = (j0, j0 + 1)

        def body(k, carry, js=js):
            r0 = pl.multiple_of(k * SUBLANES, SUBLANES)
            out = []
            for idx, j in enumerate(js):
                hr, hi = carry[2 * idx], carry[2 * idx + 1]
                base = j * 2 * S
                ar, ai = a_tabs[j]
                nhr = ar * hr - ai * hi + x_scr[pl.ds(r0, SUBLANES), base:base + S]
                nhi = ar * hi + ai * hr + x_scr[pl.ds(r0, SUBLANES), base + S:base + 2 * S]
                if store:
                    x_scr[pl.ds(r0, SUBLANES), base:base + S] = nhr
                    x_scr[pl.ds(r0, SUBLANES), base + S:base + 2 * S] = nhi
                out += [nhr, nhi]
            return tuple(out)

        c0 = tuple(t for j in js for t in init[j])
        res = jax.lax.fori_loop(0, n // SUBLANES, body, c0, unroll=True)
        fins += [(res[0], res[1]), (res[2], res[3])]
    return fins


def _ssm_rows(u, pow_row, pw_ref, bblk_ref, cblk_ref, x_scr, s_scr, carry_scr):
    n = u.shape[0]
    S = SSM_BLOCK_STATES
    ub = u.astype(BF16)
    for j in range(SSM_LANE_BLOCKS):
        x_scr[0:n, j * 2 * S:(j + 1) * 2 * S] = jnp.dot(ub[:, j * LANES:(j + 1) * LANES], bblk_ref[j],
                                                       preferred_element_type=F32)
    bc = lambda t: jnp.broadcast_to(t, (SUBLANES, S))
    re = lambda ref, r0, r1, j: ref[r0:r1, j * 2 * S:j * 2 * S + S]
    im = lambda ref, r0, r1, j: ref[r0:r1, j * 2 * S + S:(j + 1) * 2 * S]
    a_tabs = [(bc(re(pw_ref, 0, 1, j)), bc(im(pw_ref, 0, 1, j))) for j in range(SSM_LANE_BLOCKS)]
    zero = jnp.zeros((SUBLANES, S), F32)
    fins = _ssm_sweep(x_scr, n, a_tabs, [(zero, zero)] * SSM_LANE_BLOCKS, store=False)
    for j in range(SSM_LANE_BLOCKS):
        base = j * 2 * S
        cr, ci = re(pw_ref, pow_row, pow_row + 1, j), im(pw_ref, pow_row, pow_row + 1, j)
        sr, si = re(carry_scr, 0, 1, j), im(carry_scr, 0, 1, j)
        fr, fi = fins[j]
        for c in range(SUBLANES):
            s_scr[c:c + 1, base:base + S] = sr
            s_scr[c:c + 1, base + S:base + 2 * S] = si
            sr, si = cr * sr - ci * si + fr[c:c + 1], cr * si + ci * sr + fi[c:c + 1]
        carry_scr[:, base:base + S] = bc(sr)
        carry_scr[:, base + S:base + 2 * S] = bc(si)
    init = [(re(s_scr, 0, SUBLANES, j), im(s_scr, 0, SUBLANES, j)) for j in range(SSM_LANE_BLOCKS)]
    _ssm_sweep(x_scr, n, a_tabs, init, store=True)
    ys = [jnp.dot(x_scr[0:n, j * 2 * S:(j + 1) * 2 * S].astype(BF16), cblk_ref[j], preferred_element_type=F32)
          for j in range(SSM_LANE_BLOCKS)]
    return jnp.concatenate(ys, axis=1)


def _ssm_tile(u_ref, n, pow_row, seq_start, refs, o_ref, scr):
    pw_ref, bblk_ref, cblk_ref, d_ref, wg_ref, bg_ref, g_ref = refs
    up_scr, x_scr, s_scr, carry_scr, o_scr = scr
    q = n // SUBLANES
    for j in range(SSM_LANE_BLOCKS):
        for k in range(q):
            up_scr[k * SUBLANES:(k + 1) * SUBLANES, j * LANES:(j + 1) * LANES] = \
                u_ref[j, pl.ds(k, SUBLANES, stride=q), :]
    u = up_scr[0:n, :]
    if seq_start:
        p = jax.lax.broadcasted_iota(jnp.int32, (n, 1), 0)
        u = jnp.where((p % SUBLANES) * q + p // SUBLANES >= seq_start, u, 0.0)
    y = _ssm_rows(u, pow_row, pw_ref, bblk_ref, cblk_ref, x_scr, s_scr, carry_scr)
    out = _ssm_tail(y, u, d_ref, wg_ref, bg_ref, g_ref)
    for j in range(SSM_LANE_BLOCKS):
        for k in range(q):
            o_scr[j, pl.ds(k, SUBLANES, stride=q), :] = out[k * SUBLANES:(k + 1) * SUBLANES,
                                                            j * LANES:(j + 1) * LANES]
    o_ref[...] = jnp.concatenate([o_scr[j, 0:n, :] for j in range(SSM_LANE_BLOCKS)], axis=1).astype(o_ref.dtype)


def _ssm_kernel(um_ref, uh_ref, pw_ref, bblk_ref, cblk_ref, d_ref, wg_ref, bg_ref, g_ref, cast_in_ref,
                om_ref, oh_ref, st_ref, cast_out_ref, up_scr, x_scr, s_scr, carry_scr, o_scr):
    t = pl.program_id(1)
    refs = (pw_ref, bblk_ref, cblk_ref, d_ref, wg_ref, bg_ref, g_ref)
    scr = (up_scr, x_scr, s_scr, carry_scr, o_scr)

    @pl.when(t == 0)
    def _():
        carry_scr[...] = jnp.zeros_like(carry_scr)
        _ssm_tile(uh_ref, HEAD_ROWS, 1, SEQ0, refs, oh_ref, scr)

    @pl.when(t > 0)
    def _():
        cast_out_ref[...] = cast_in_ref[...].astype(BF16)
        _ssm_tile(um_ref, um_ref.shape[1], 2, 0, refs, om_ref, scr)
        st_ref[...] = carry_scr[...]


def _ssm(um, uh, pw, bblk, cblk, d, wg, l, bg, g, B, tm, cast):
    assert (1, HEAD_ROWS // SUBLANES, tm // SUBLANES) == SSM_POW_ROWS
    J = SSM_LANE_BLOCKS
    rm, rh = um.shape[1], uh.shape[1]
    nt = rm // (B * tm)
    tile = lambda b, t: b * nt + jnp.maximum(t - 1, 0)
    main = lambda b, t: (tile(b, t), 0)
    head = lambda b, t: (b, 0)
    _, cr, cc = cast.shape
    crows = cr // (B * nt)
    return pl.pallas_call(
        _ssm_kernel,
        grid=(B, nt + 1),
        in_specs=[pl.BlockSpec((J, tm, LANES), lambda b, t: (0, tile(b, t), 0)),
                  pl.BlockSpec((J, HEAD_ROWS, LANES), lambda b, t: (0, b, 0)),
                  _layer(pw.shape[1:], l), _layer(bblk.shape[1:], l), _layer(cblk.shape[1:], l),
                  _vrow("D_skip", l), _layer((SSM_WIDTH, SSM_WIDTH), l), _vrow("b_glu", l),
                  _vrow("g_out_ssm", l),
                  pl.BlockSpec((None, crows, cc), lambda b, t: (l, tile(b, t), 0))],
        out_specs=[pl.BlockSpec((tm, SSM_WIDTH), main), pl.BlockSpec((HEAD_ROWS, SSM_WIDTH), head),
                   pl.BlockSpec((None, SUBLANES, SSM_STATE_LANES), lambda b, t: (b, 0, 0)),
                   pl.BlockSpec((crows, cc), main)],
        out_shape=[jax.ShapeDtypeStruct((rm, SSM_WIDTH), BF16), jax.ShapeDtypeStruct((rh, SSM_WIDTH), BF16),
                   jax.ShapeDtypeStruct((B, SUBLANES, SSM_STATE_LANES), F32),
                   jax.ShapeDtypeStruct((cr, cc), BF16)],
        scratch_shapes=[pltpu.VMEM((tm, SSM_WIDTH), F32),
                        pltpu.VMEM((tm, SSM_STATE_LANES), F32),
                        pltpu.VMEM((SUBLANES, SSM_STATE_LANES), F32),
                        pltpu.VMEM((SUBLANES, SSM_STATE_LANES), F32),
                        pltpu.VMEM((J, tm, LANES), F32)],
        compiler_params=_params("arbitrary", "arbitrary"),
        name="ssm_prompt",
    )(um, uh, pw, bblk, cblk, d, wg, bg, g, cast)


def _state_from_lanes(s):
    s = s.reshape(s.shape[0], SSM_LANE_BLOCKS, 2, SSM_BLOCK_STATES)
    return (s[:, :, 0].reshape(-1, SSM_GROUPS, SSM_STATE), s[:, :, 1].reshape(-1, SSM_GROUPS, SSM_STATE))


def _pool_tail(d_groups, w_ref, sc_ref, g_ref):
    y = jnp.concatenate(
        [jnp.dot(d.astype(BF16), w_ref[gi], preferred_element_type=F32) for gi, d in enumerate(d_groups)], axis=1)
    return _rms(y * sc_ref[...], g_ref[...])


def _pool_rows(x, prev, pos0, w_ref, sc_ref, g_ref):
    n = x.shape[0]
    xe = jnp.concatenate([prev, x], axis=0)
    pos = pos0 + jax.lax.broadcasted_iota(jnp.int32, (n, 1), 0)
    ds = []
    for gi, w in enumerate(POOL_WINDOWS):
        gsl = slice(gi * POOL_GROUP, (gi + 1) * POOL_GROUP)
        s = xe[:, gsl]
        k = 1
        while k < w:
            s = s + pltpu.roll(s, k, 0)
            k *= 2
        cnt = jnp.clip(pos + 1, 1, w).astype(F32)
        ds.append(s[POOL_HALO:] / cnt - x[:, gsl])
    return _pool_tail(ds, w_ref, sc_ref, g_ref)


def _pool_kernel(xm_ref, halo_ref, xh_ref, w_ref, sc_ref, g_ref, cast_in_ref, om_ref, oh_ref, cast_out_ref):
    t = pl.program_id(1)
    tm = xm_ref.shape[0]

    @pl.when(t == 0)
    def _():
        hrow = jax.lax.broadcasted_iota(jnp.int32, (HEAD_ROWS, 1), 0)
        x = jnp.where(hrow >= SEQ0, xh_ref[...], 0.0)
        prev = jnp.zeros((POOL_HALO, POOL_WIDTH), F32)
        oh_ref[...] = _pool_rows(x, prev, -SEQ0, w_ref, sc_ref, g_ref).astype(oh_ref.dtype)

    @pl.when(t > 0)
    def _():
        prev = jnp.where(t == 1, xh_ref[HEAD_ROWS - POOL_HALO:, :], halo_ref[...])
        om_ref[...] = _pool_rows(xm_ref[...], prev, N_META + (t - 1) * tm, w_ref, sc_ref, g_ref).astype(om_ref.dtype)
        cast_out_ref[...] = cast_in_ref[...].astype(BF16)


def _pool(xm, xh, w, l, sc, g, B, tm, cast):
    nt = xm.shape[0] // (B * tm)
    r = tm // POOL_HALO
    main = lambda b, t: (b * nt + jnp.maximum(t - 1, 0), 0)
    halo = lambda b, t: (jnp.maximum((b * nt + t - 1) * r - 1, 0), 0)
    head = lambda b, t: (b, 0)
    _, cr, cc = cast.shape
    crows = cr // (B * nt)
    return pl.pallas_call(
        _pool_kernel,
        grid=(B, nt + 1),
        in_specs=[pl.BlockSpec((tm, POOL_WIDTH), main), pl.BlockSpec((POOL_HALO, POOL_WIDTH), halo),
                  pl.BlockSpec((HEAD_ROWS, POOL_WIDTH), head),
                  _layer(w.shape[1:], l), _vrow("pool_scale", l), _vrow("g_out_pool", l),
                  pl.BlockSpec((None, crows, cc), lambda b, t: (l, b * nt + jnp.maximum(t - 1, 0), 0))],
        out_specs=[pl.BlockSpec((tm, POOL_WIDTH), main), pl.BlockSpec((HEAD_ROWS, POOL_WIDTH), head),
                   pl.BlockSpec((crows, cc), main)],
        out_shape=[jax.ShapeDtypeStruct(xm.shape, BF16), jax.ShapeDtypeStruct(xh.shape, BF16),
                   jax.ShapeDtypeStruct((cr, cc), BF16)],
        compiler_params=_params("arbitrary", "arbitrary"),
        name="pool_prompt",
    )(xm, xm, xh, w, sc, g, cast)


def _attn_sample_kernel(q_ref, kn_ref, vn_ref, kc_ref, vc_ref, sink_ref, g_ref, a_in_ref,
                        a_ref, nk_ref, nv_ref, acc_scr):
    del a_in_ref
    step = pl.program_id(0)
    scale = HEAD_DIM ** -0.5
    for bb in range(DEC_STEP):
        nk_ref[bb, 0:WINDOW - 1] = kc_ref[bb, 1:WINDOW]
        nv_ref[bb, 0:WINDOW - 1] = vc_ref[bb, 1:WINDOW]
        outs = []
        for kh in range(N_KV_HEADS):
            ksl = slice(kh * HEAD_DIM, (kh + 1) * HEAD_DIM)
            qh = jnp.concatenate(
                [q_ref[bb:bb + 1, (kh * GQA_GROUP + g) * HEAD_DIM:(kh * GQA_GROUP + g + 1) * HEAD_DIM]
                 for g in range(GQA_GROUP)], axis=0)
            kn = kn_ref[bb:bb + 1, ksl]
            vn = vn_ref[bb:bb + 1, ksl]
            nk_ref[bb, WINDOW - 1, kh:kh + 1, :] = kn
            nv_ref[bb, WINDOW - 1, kh:kh + 1, :] = vn
            kc = kc_ref[bb, :, kh, :]
            vc = vc_ref[bb, :, kh, :]
            sc = jax.lax.dot_general(qh.astype(BF16), kc.astype(BF16), (((1,), (1,)), ((), ())),
                                     preferred_element_type=F32) * scale
            sn = jnp.sum(qh.astype(BF16).astype(F32) * kn.astype(BF16).astype(F32), axis=-1, keepdims=True) * scale
            sk = sink_ref[kh]
            m = jnp.maximum(jnp.maximum(jnp.max(sc, axis=-1, keepdims=True), sn), sk)
            pc = jnp.exp(sc - m)
            pn = jnp.exp(sn - m)
            denom = jnp.sum(pc, axis=-1, keepdims=True) + pn + jnp.exp(sk - m)
            o = jnp.dot(pc.astype(BF16), vc.astype(BF16), preferred_element_type=F32)
            o = (o + pn.astype(BF16).astype(F32) * vn.astype(BF16).astype(F32)) / denom
            outs.extend(o[g:g + 1] for g in range(GQA_GROUP))
        a = jnp.concatenate(outs, axis=1)
        acc_scr[pl.ds(step * DEC_STEP + bb, 1), :] = _rms(a, g_ref[...])

    @pl.when(step == pl.num_programs(0) - 1)
    def _():
        a_ref[...] = acc_scr[...].astype(a_ref.dtype)


def _attn_sample(qh, kh, vh, cache_k, cache_v, l, sinks, g, ah, N):
    rows = lambda w: pl.BlockSpec((DEC_STEP, w), lambda s: (s, 0))
    cache = pl.BlockSpec((None, DEC_STEP, WINDOW, N_KV_HEADS, HEAD_DIM), lambda s: (l, s, 0, 0, 0))
    ncache = pl.BlockSpec((DEC_STEP, WINDOW, N_KV_HEADS, HEAD_DIM), lambda s: (s, 0, 0, 0))
    cshape = jax.ShapeDtypeStruct((N, WINDOW, N_KV_HEADS, HEAD_DIM), F32)
    return pl.pallas_call(
        _attn_sample_kernel,
        grid=(N // DEC_STEP,),
        in_specs=[rows(ATTN_WIDTH), rows(KV_WIDTH), rows(KV_WIDTH), cache, cache,
                  _layer((N_KV_HEADS, GQA_GROUP, 1), l), _vrow("g_out_attn", l),
                  pl.BlockSpec(memory_space=pl.ANY)],
        out_specs=[pl.BlockSpec((N, ATTN_WIDTH), lambda s: (0, 0)), ncache, ncache],
        out_shape=[jax.ShapeDtypeStruct(ah.shape, ah.dtype), cshape, cshape],
        scratch_shapes=[pltpu.VMEM((N, ATTN_WIDTH), F32)],
        input_output_aliases={7: 0},
        compiler_params=_params("arbitrary"),
        name="attn_sample",
    )(qh, kh, vh, cache_k, cache_v, sinks, g, ah)


def _mix_sample_kernel(u_ref, h0_ref, pw_ref, bblk_ref, cblk_ref, d_ref, wg_ref, bg_ref, gs_ref,
                       xp_ref, pb_ref, wp_ref, sc_ref, gp_ref, s_in_ref, p_in_ref, s_ref, st_ref, p_ref):
    del s_in_ref, p_in_ref
    S = SSM_BLOCK_STATES
    u = jnp.concatenate([u_ref[j] for j in range(SSM_LANE_BLOCKS)], axis=1)
    ub = u.astype(BF16)
    ys = []
    for j in range(SSM_LANE_BLOCKS):
        x = jnp.dot(ub[:, j * LANES:(j + 1) * LANES], bblk_ref[j], preferred_element_type=F32)
        base = j * 2 * S
        ar = pw_ref[0:1, base:base + S]
        ai = pw_ref[0:1, base + S:base + 2 * S]
        h0r = h0_ref[:, base:base + S]
        h0i = h0_ref[:, base + S:base + 2 * S]
        hr = x[:, 0:S] + ar * h0r - ai * h0i
        hi = x[:, S:] + ar * h0i + ai * h0r
        st_ref[:, base:base + S] = hr
        st_ref[:, base + S:base + 2 * S] = hi
        h = jnp.concatenate([hr, hi], axis=1).astype(BF16)
        ys.append(jnp.dot(h, cblk_ref[j], preferred_element_type=F32))
    s_ref[...] = _ssm_tail(jnp.concatenate(ys, axis=1), u, d_ref, wg_ref, bg_ref, gs_ref).astype(s_ref.dtype)

    xp = xp_ref[...]
    ds = []
    for gi, w in enumerate(POOL_WINDOWS):
        gsl = slice(gi * POOL_GROUP, (gi + 1) * POOL_GROUP)
        s = xp[:, gsl]
        for back in range(1, w):
            s = s + pb_ref[POOL_BUF - back][:, gsl]
        ds.append(s / float(w) - xp[:, gsl])
    p_ref[...] = _pool_tail(ds, wp_ref, sc_ref, gp_ref).astype(p_ref.dtype)


def _mix_sample(uh, h0, pw, bblk, cblk, d, wg, l, bg, gs, xph, pbuf, wp, _l, sc, gp, sh, ph):
    N = h0.shape[1]
    rows = lambda w: pl.BlockSpec((N, w), lambda i: (0, 0))
    anyspec = pl.BlockSpec(memory_space=pl.ANY)
    return pl.pallas_call(
        _mix_sample_kernel,
        grid=(1,),
        in_specs=[pl.BlockSpec((SSM_LANE_BLOCKS, N, LANES), lambda i: (0, 0, 0)),
                  _layer(h0.shape[1:], l), _layer(pw.shape[1:], l), _layer(bblk.shape[1:], l),
                  _layer(cblk.shape[1:], l),
                  _vrow("D_skip", l), _layer((SSM_WIDTH, SSM_WIDTH), l), _vrow("b_glu", l),
                  _vrow("g_out_ssm", l), rows(POOL_WIDTH), _layer(pbuf.shape[1:], l),
                  _layer(wp.shape[1:], l),
                  _vrow("pool_scale", l), _vrow("g_out_pool", l), anyspec, anyspec],
        out_specs=[rows(SSM_WIDTH), _full((N, SSM_STATE_LANES)), rows(POOL_WIDTH)],
        out_shape=[jax.ShapeDtypeStruct(sh.shape, sh.dtype), jax.ShapeDtypeStruct((N, SSM_STATE_LANES), F32),
                   jax.ShapeDtypeStruct(ph.shape, ph.dtype)],
        input_output_aliases={14: 0, 15: 2},
        compiler_params=_params("arbitrary"),
        name="mix_sample",
    )(uh, h0, pw, bblk, cblk, d, wg, bg, gs, xph, pbuf, wp, sc, gp, sh, ph)


def _outproj_kernel(x_ref, a_ref, s_ref, p_ref, w_ref, o_ref):
    o1 = ATTN_WIDTH
    o2 = o1 + SSM_WIDTH
    acc = x_ref[...]
    acc = acc + jnp.dot(a_ref[...], w_ref[0:o1, :], preferred_element_type=F32)
    acc = acc + jnp.dot(s_ref[...], w_ref[o1:o2, :], preferred_element_type=F32)
    acc = acc + jnp.dot(p_ref[...], w_ref[o2:, :], preferred_element_type=F32)
    o_ref[...] = acc


def _outproj(x, a, s, p, w, tm):
    R = x.shape[0]
    row = lambda i: (i, 0)
    return pl.pallas_call(
        _outproj_kernel,
        grid=(R // tm,),
        in_specs=[pl.BlockSpec((tm, D_MODEL), row), pl.BlockSpec((tm, ATTN_WIDTH), row),
                  pl.BlockSpec((tm, SSM_WIDTH), row), pl.BlockSpec((tm, POOL_WIDTH), row),
                  pl.BlockSpec((D_MODEL, D_MODEL), lambda i: (0, 0), pipeline_mode=pl.Buffered(1))],
        out_specs=pl.BlockSpec((tm, D_MODEL), row),
        out_shape=jax.ShapeDtypeStruct((R, D_MODEL), F32),
        compiler_params=_params("arbitrary"),
        name="outproj",
    )(x, a, s, p, w)


def _ffn_kernel(xm_ref, xh_ref, g_ref, w1_ref, w2_ref, om_ref, oh_ref, h_scr):
    i = pl.program_id(0)
    f = pl.program_id(1)
    tm = xm_ref.shape[0]

    @pl.when(f == 0)
    def _():
        x = xm_ref[...]
        h_scr[0:tm, :] = _rms(x, g_ref[...]).astype(BF16)
        om_ref[...] = x

    @pl.when((f == 0) & (i == 0))
    def _():
        x = xh_ref[...]
        h_scr[tm:, :] = _rms(x, g_ref[...]).astype(BF16)
        oh_ref[...] = x

    def mlp(h):
        h1 = jnp.dot(h, w1_ref[...], preferred_element_type=F32)
        return jnp.dot(jnp.square(jnp.maximum(h1, 0.0)).astype(BF16), w2_ref[...], preferred_element_type=F32)

    @pl.when(i == 0)
    def _():
        y = mlp(h_scr[...])
        om_ref[...] += y[0:tm]
        oh_ref[...] += y[tm:]

    @pl.when(i > 0)
    def _():
        om_ref[...] += mlp(h_scr[0:tm, :])


def _ffn(xm, xh, g, l, w1, w2, tm, tf):
    R, RH = xm.shape[0], xh.shape[0]
    return pl.pallas_call(
        _ffn_kernel,
        grid=(R // tm, D_FF // tf),
        in_specs=[pl.BlockSpec((tm, D_MODEL), lambda i, f: (i, 0)),
                  pl.BlockSpec((RH, D_MODEL), lambda i, f: (0, 0), pipeline_mode=pl.Buffered(1)),
                  _vrow("g_ffn", l),
                  pl.BlockSpec((D_MODEL, tf), lambda i, f: (0, f)),
                  pl.BlockSpec((tf, D_MODEL), lambda i, f: (f, 0))],
        out_specs=[pl.BlockSpec((tm, D_MODEL), lambda i, f: (i, 0)),
                   pl.BlockSpec((RH, D_MODEL), lambda i, f: (0, 0))],
        out_shape=[jax.ShapeDtypeStruct((R, D_MODEL), F32), jax.ShapeDtypeStruct((RH, D_MODEL), F32)],
        scratch_shapes=[pltpu.VMEM((tm + RH, D_MODEL), BF16)],
        compiler_params=_params("arbitrary", "arbitrary"),
        name="ffn",
    )(xm, xh, g, w1, w2)


def _rope_tables(pos):
    half = ROT_DIM // 2
    inv = ROPE_THETA ** (-np.arange(0, ROT_DIM, 2, dtype=np.float64) / ROT_DIM)
    ang = np.asarray(pos, np.float64)[:, None] * inv
    cos, sin = np.cos(ang), np.sin(ang)
    n = ang.shape[0]
    z = np.zeros((n, HEAD_DIM - ROT_DIM))
    zh = np.zeros((n, half))
    rc = np.concatenate([cos, cos, z + 1.0], axis=1)
    rs1 = np.concatenate([zh, sin, z], axis=1)
    rs2 = np.concatenate([-sin, zh, z], axis=1)
    return tuple(t.astype(np.float32) for t in (rc, rs1, rs2))


def kernel(x_prompt, x_sample, cache_k, cache_v, state_ssm_re, state_ssm_im, state_pool, meta_tokens, g_mix, w_in, g_q, g_k, sinks, A_re, A_im, log_dt, B_re, B_im, C_re, C_im, D_skip, w_glu, b_glu, w_pool, pool_scale, g_out_attn, g_out_ssm, g_out_pool, w_out, g_ffn, w_ff1, w_ff2):
    B, T, _ = x_prompt.shape
    N = x_sample.shape[0]
    depth = w_in.shape[0]
    assert N <= SEQ0 and N % DEC_STEP == 0 and T % TM_FFN == 0
    meta = meta_tokens.astype(F32)
    head0 = jnp.concatenate([x_sample.reshape(N, D_MODEL), jnp.zeros((SEQ0 - N, D_MODEL), F32), meta], axis=0)
    head_rest = jnp.concatenate([jnp.zeros((SEQ0, D_MODEL), F32), meta], axis=0)
    xh = jnp.concatenate([head0] + [head_rest] * (B - 1), axis=0)
    xm = x_prompt.reshape(B * T, D_MODEL)

    rope_m = _rope_tables(N_META + np.arange(T))
    hr_ = np.arange(HEAD_ROWS)
    pos_h0 = np.where(hr_ < N, PAST_LEN, np.maximum(hr_ - SEQ0, 0))
    pos_h = np.concatenate([pos_h0] + [np.maximum(hr_ - SEQ0, 0)] * (B - 1))
    rope_h = _rope_tables(pos_h)

    wi = w_in[0].astype(BF16)
    wg_all, wp_all = w_glu.astype(BF16), w_pool.astype(BF16)
    ck = cache_k.astype(F32)
    cv = cache_v.astype(F32)
    RH = B * HEAD_ROWS

    named = dict(g_mix=g_mix, g_ffn=g_ffn, g_out_attn=g_out_attn, D_skip=D_skip, b_glu=b_glu,
                 g_out_ssm=g_out_ssm, pool_scale=pool_scale, g_out_pool=g_out_pool, g_q=g_q, g_k=g_k)
    vecs = jnp.concatenate([named[name].astype(F32) for name, _ in VEC_LAYOUT], axis=1)[:, None, :]
    pw, bblk, cblk = _ssm_params(A_re, A_im, log_dt, B_re, B_im, C_re, C_im)
    ssm_w = lambda l: (pw, bblk, cblk, vecs, wg_all, l, vecs, vecs)
    pool_w = lambda l: (wp_all, l, vecs, vecs)
    g_attn = vecs
    g_ffn_ = vecs
    sinks_flat = sinks.astype(F32).reshape(depth * N_HEADS)
    sinks_col = sinks.astype(F32).reshape(depth, N_KV_HEADS, GQA_GROUP, 1)
    bias = _attn_bias()
    h0 = jnp.concatenate([state_ssm_re.astype(F32).reshape(depth, N, SSM_LANE_BLOCKS, SSM_BLOCK_STATES),
                          state_ssm_im.astype(F32).reshape(depth, N, SSM_LANE_BLOCKS, SSM_BLOCK_STATES)],
                         axis=-1).reshape(depth, N, SSM_STATE_LANES)
    pbuf = state_pool.astype(F32).transpose(0, 2, 1, 3)

    ks, vs, pls, sts, nks, nvs, st_ss, phs = ([] for _ in range(8))
    for l in range(depth):
        in_w = (vecs, wi, l, vecs, vecs)
        qm, km, vm, um, pm, w1 = _inproj(xm, *in_w, rope_m, TM_PROJ, cast=w_ff1)
        qh, kh, vh, uh, ph = _inproj(xh, *in_w, rope_h, RH)

        if l + 1 < depth:
            am, ah, wi = _attn(sinks_flat, l, qm, qh, km, kh, vm, vh, g_attn, bias, B, cast=w_in, cast_layer=l + 1)
        else:
            am, ah = _attn(sinks_flat, l, qm, qh, km, kh, vm, vh, g_attn, bias, B)
        sm, sh, st, w2 = _ssm(um, uh, *ssm_w(l), B, TM_SEQ, w_ff2)
        plm, plh, wo = _pool(pm, ph, *pool_w(l), B, TM_SEQ, w_out)

        ah, nk, nv = _attn_sample(qh, kh, vh, ck, cv, l, sinks_col, g_attn, ah, N)
        sh, st_s, plh = _mix_sample(uh, h0, *ssm_w(l), ph, pbuf, *pool_w(l), sh, plh)

        xm = _outproj(xm, am, sm, plm, wo, TM_PROJ)
        xh = _outproj(xh, ah, sh, plh, wo, RH)
        xm, xh = _ffn(xm, xh, g_ffn_, l, w1, w2, TM_FFN, TF_FFN)

        ks.append(km.reshape(B, T, KV_WIDTH)[:, T - WINDOW:])
        vs.append(vm.reshape(B, T, KV_WIDTH)[:, T - WINDOW:])
        pls.append(pm.reshape(B, T, POOL_WIDTH)[:, T - POOL_BUF:])
        sts.append(st[:, 0])
        nks.append(nk)
        nvs.append(nv)
        st_ss.append(st_s)
        phs.append(ph[:N])

    y_prompt = xm.reshape(B, T, D_MODEL)
    y_sample = xh[:N].reshape(N, 1, D_MODEL)
    heads = lambda t: jnp.stack(t).reshape(depth, -1, WINDOW, N_KV_HEADS, HEAD_DIM)
    p_re, p_im = _state_from_lanes(jnp.stack(sts).reshape(depth * B, SSM_STATE_LANES))
    s_re, s_im = _state_from_lanes(jnp.stack(st_ss).reshape(depth * N, SSM_STATE_LANES))
    st4 = lambda t, n: t.reshape(depth, n, SSM_GROUPS, SSM_STATE)
    s_pool = jnp.concatenate([state_pool.astype(F32)[:, :, 1:], jnp.stack(phs)[:, :, None]], axis=2)
    return (y_prompt, y_sample, heads(ks), heads(vs), st4(p_re, B), st4(p_im, B), jnp.stack(pls),
            jnp.stack(nks), jnp.stack(nvs), st4(s_re, N), st4(s_im, N), s_pool)
```

```python
import functools
import math

import jax
import jax.numpy as jnp
import numpy as np
from jax.experimental import pallas as pl
from jax.experimental.pallas import tpu as pltpu

D_MODEL = 2048
N_META = 16
HEAD_DIM = 128
N_HEADS = 8
N_KV_HEADS = 2
GQA_GROUP = 4
ATTN_WIDTH = 1024
KV_WIDTH = 256
WINDOW = 128
BLOCK = 128
ROT_DIM = 32
ROPE_THETA = 500000.0
SSM_WIDTH = 512
SSM_GROUP_SIZE = 16
SSM_GROUPS = 32
SSM_STATE = 64
POOL_WIDTH = 512
POOL_WINDOWS = (2, 4, 8, 16)
POOL_GROUP = 128
POOL_BUF = 15
POOL_HALO = 16
IN_WIDTH = 2560
D_FF = 8192
EPS = 1e-6
PAST_LEN = 16384
LOG2E = math.log2(math.e)

HEAD_ROWS = BLOCK
SEQ0 = HEAD_ROWS - N_META
LANES = 128
SUBLANES = 8
SSM_LANE_BLOCKS = SSM_WIDTH // LANES
SSM_BLOCK_STATES = (LANES // SSM_GROUP_SIZE) * SSM_STATE
SSM_STATE_LANES = SSM_LANE_BLOCKS * 2 * SSM_BLOCK_STATES
VMEM_LIMIT = 56 * 1024 * 1024

TM_PROJ = 512
TM_FFN = 1024
TF_FFN = 512
TM_SEQ = 512
DEC_STEP = 8
SSM_POW_ROWS = (1, HEAD_ROWS // SUBLANES, TM_SEQ // SUBLANES)

BF16 = jnp.bfloat16
F32 = jnp.float32


def _params(*semantics):
    return pltpu.CompilerParams(dimension_semantics=semantics, vmem_limit_bytes=VMEM_LIMIT)


def _rms(x, g):
    return x * jax.lax.rsqrt(jnp.mean(x * x, axis=-1, keepdims=True) + EPS) * g


def _full(shape):
    n = len(shape)
    return pl.BlockSpec(shape, lambda *_: (0,) * n)


def _layer(shape, l):
    n = len(shape)
    return pl.BlockSpec((None, *shape), lambda *_: (l,) + (0,) * n)


VEC_LAYOUT = (("g_mix", D_MODEL), ("g_ffn", D_MODEL), ("g_out_attn", ATTN_WIDTH), ("D_skip", SSM_WIDTH),
              ("b_glu", SSM_WIDTH), ("g_out_ssm", SSM_WIDTH), ("pool_scale", POOL_WIDTH),
              ("g_out_pool", POOL_WIDTH), ("g_q", HEAD_DIM), ("g_k", HEAD_DIM))
VEC_WIDTH = dict(VEC_LAYOUT)
VEC_OFFSET = {name: sum(w for _, w in VEC_LAYOUT[:i]) for i, (name, _) in enumerate(VEC_LAYOUT)}
assert all(VEC_OFFSET[name] % w == 0 for name, w in VEC_LAYOUT)


def _vrow(name, l):
    w = VEC_WIDTH[name]
    return pl.BlockSpec((None, 1, w), lambda *_: (l, 0, VEC_OFFSET[name] // w))


def _inproj_kernel(x_ref, g_ref, w_ref, gq_ref, gk_ref, rc_ref, rs1_ref, rs2_ref, *rest):
    if len(rest) == 7:
        cast_in_ref, q_ref, k_ref, v_ref, u_ref, xp_ref, cast_out_ref = rest
        cast_out_ref[...] = cast_in_ref[...].astype(BF16)
    else:
        q_ref, k_ref, v_ref, u_ref, xp_ref = rest
    h = _rms(x_ref[...], g_ref[...]).astype(BF16)
    proj = jnp.dot(h, w_ref[...], preferred_element_type=F32)
    rc, rs1, rs2 = rc_ref[...], rs1_ref[...], rs2_ref[...]

    def head(t, g):
        t = _rms(t, g)
        return t * rc + pltpu.roll(t, 16, 1) * rs1 + pltpu.roll(t, LANES - 16, 1) * rs2

    for hd in range(N_HEADS):
        sl = slice(hd * HEAD_DIM, (hd + 1) * HEAD_DIM)
        q_ref[:, sl] = head(proj[:, sl], gq_ref[...])
    for hd in range(N_KV_HEADS):
        sl = slice(hd * HEAD_DIM, (hd + 1) * HEAD_DIM)
        k_ref[:, sl] = head(proj[:, ATTN_WIDTH + hd * HEAD_DIM:ATTN_WIDTH + (hd + 1) * HEAD_DIM], gk_ref[...])
    o2 = ATTN_WIDTH + KV_WIDTH
    o3 = o2 + KV_WIDTH
    o4 = o3 + SSM_WIDTH
    v_ref[...] = proj[:, o2:o3]
    for j in range(SSM_LANE_BLOCKS):
        u_ref[j] = proj[:, o3 + j * LANES:o3 + (j + 1) * LANES]
    xp_ref[...] = proj[:, o4:]


def _cast_side_job(w, l, steps):
    _, r, c = w.shape
    rows = r // steps
    return (pl.BlockSpec((None, rows, c), lambda i: (l, i, 0)), pl.BlockSpec((rows, c), lambda i: (i, 0)),
            jax.ShapeDtypeStruct((r, c), BF16))


def _inproj(x, g, w, l, gq, gk, rope, tm, cast=None):
    R = x.shape[0]
    tiles_per_rope = rope[0].shape[0] // tm
    row = lambda i: (i, 0)
    rrow = lambda i: (i % tiles_per_rope, 0)
    flat = lambda w_: (pl.BlockSpec((tm, w_), row), jax.ShapeDtypeStruct((R, w_), F32))
    u_out = (pl.BlockSpec((SSM_LANE_BLOCKS, tm, LANES), lambda i: (0, i, 0)),
             jax.ShapeDtypeStruct((SSM_LANE_BLOCKS, R, LANES), F32))
    outs = [flat(ATTN_WIDTH), flat(KV_WIDTH), flat(KV_WIDTH), u_out, flat(POOL_WIDTH)]
    w_spec = pl.BlockSpec((D_MODEL, IN_WIDTH), lambda i: (0, 0), pipeline_mode=pl.Buffered(1))
    in_specs = [pl.BlockSpec((tm, D_MODEL), row), _vrow("g_mix", l), w_spec,
                _vrow("g_q", l), _vrow("g_k", l),
                pl.BlockSpec((tm, LANES), rrow), pl.BlockSpec((tm, LANES), rrow),
                pl.BlockSpec((tm, LANES), rrow)]
    args = [x, g, w, gq, gk, *rope]
    if cast is not None:
        c_in, c_out, c_shape = _cast_side_job(cast, l, R // tm)
        in_specs.append(c_in)
        args.append(cast)
        outs.append((c_out, c_shape))
    return pl.pallas_call(
        _inproj_kernel,
        grid=(R // tm,),
        in_specs=in_specs,
        out_specs=[o[0] for o in outs],
        out_shape=[o[1] for o in outs],
        compiler_params=_params("arbitrary"),
        name="inproj",
    )(*args)


def _attn_bias():
    rows = GQA_GROUP * BLOCK
    i, r, c = np.meshgrid(np.arange(3), np.arange(rows) % BLOCK, np.arange(2 * BLOCK), indexing="ij")
    diff = BLOCK + r - c
    krow = (i - 1) * BLOCK + c
    mask = (diff >= 0) & (diff <= WINDOW) & (krow >= SEQ0)
    return np.where(mask, 0.0, -np.inf).astype(np.float32)


def _attn_block(q_blk, kp_blk, kc_blk, vp_blk, vc_blk, bias, sink_ref, l, g):
    rows = GQA_GROUP * BLOCK
    rgrp = jax.lax.broadcasted_iota(jnp.int32, (rows, 1), 0) // BLOCK
    outs = []
    for kh in range(N_KV_HEADS):
        ksl = slice(kh * HEAD_DIM, (kh + 1) * HEAD_DIM)
        qh = jnp.concatenate(
            [q_blk[:, (kh * GQA_GROUP + h) * HEAD_DIM:(kh * GQA_GROUP + h + 1) * HEAD_DIM]
             for h in range(GQA_GROUP)], axis=0).astype(BF16)
        kk = jnp.concatenate([kp_blk[:, ksl], kc_blk[:, ksl]], axis=0).astype(BF16)
        vv = jnp.concatenate([vp_blk[:, ksl], vc_blk[:, ksl]], axis=0).astype(BF16)
        s = jax.lax.dot_general(qh, kk, (((1,), (1,)), ((), ())),
                                preferred_element_type=F32) * (HEAD_DIM ** -0.5 * LOG2E) + bias
        sk = jnp.zeros((rows, 1), F32)
        for h in range(GQA_GROUP):
            sk = jnp.where(rgrp == h, sink_ref[l * N_HEADS + kh * GQA_GROUP + h] * LOG2E, sk)
        m = jnp.maximum(jnp.max(s, axis=-1, keepdims=True), sk)
        p = jnp.exp2(s - m)
        denom = jnp.sum(p, axis=-1, keepdims=True) + jnp.exp2(sk - m)
        o = jnp.dot(p.astype(BF16), vv, preferred_element_type=F32) / denom
        outs.extend(o[h * BLOCK:(h + 1) * BLOCK] for h in range(GQA_GROUP))
    return _rms(jnp.concatenate(outs, axis=1), g)


def _attn_kernel(sink_ref, qm_ref, qh_ref, kpm_ref, kcm_ref, kh_ref, vpm_ref, vcm_ref, vh_ref, g_ref,
                 bias_a_ref, bias_b_ref, *rest, l):
    s = pl.program_id(1)
    if len(rest) == 4:
        cast_in_ref, om_ref, oh_ref, cast_out_ref = rest

        @pl.when(s > 0)
        def _():
            cast_out_ref[...] = cast_in_ref[...].astype(BF16)
    else:
        om_ref, oh_ref = rest
    blk = functools.partial(_attn_block, sink_ref=sink_ref, l=l, g=g_ref[...])

    @pl.when(s == 0)
    def _():
        a = blk(qh_ref[...], kpm_ref[...], kh_ref[...], vpm_ref[...], vh_ref[...], bias_a_ref[...])
        oh_ref[...] = a.astype(oh_ref.dtype)

    @pl.when(s > 0)
    def _():
        k1, k2 = kcm_ref[0:BLOCK, :], kcm_ref[BLOCK:2 * BLOCK, :]
        v1, v2 = vcm_ref[0:BLOCK, :], vcm_ref[BLOCK:2 * BLOCK, :]
        kp = jnp.where(s == 1, kh_ref[...], kpm_ref[...])
        vp = jnp.where(s == 1, vh_ref[...], vpm_ref[...])
        a1 = blk(qm_ref[0:BLOCK, :], kp, k1, vp, v1, bias_a_ref[...])
        a2 = blk(qm_ref[BLOCK:2 * BLOCK, :], k1, k2, v1, v2, bias_b_ref[...])
        om_ref[0:BLOCK, :] = a1.astype(om_ref.dtype)
        om_ref[BLOCK:2 * BLOCK, :] = a2.astype(om_ref.dtype)


def _attn(sinks, l, qm, qh, km, kh, vm, vh, g, bias, B, cast=None, cast_layer=0):
    nb = qm.shape[0] // (B * BLOCK)
    npair = nb // 2
    pair = lambda b, s, _: (b * npair + jnp.maximum(s - 1, 0), 0)
    prev = lambda b, s, _: (b * nb + jnp.maximum(2 * s - 3, 0), 0)
    head = lambda b, s, _: (b, 0)
    one = lambda w, im: pl.BlockSpec((BLOCK, w), im)
    two = lambda w: pl.BlockSpec((2 * BLOCK, w), pair)
    bias_spec = lambda im: pl.BlockSpec((None,) + bias.shape[1:], im)
    in_specs = [two(ATTN_WIDTH), one(ATTN_WIDTH, head),
                one(KV_WIDTH, prev), two(KV_WIDTH), one(KV_WIDTH, head),
                one(KV_WIDTH, prev), two(KV_WIDTH), one(KV_WIDTH, head),
                _vrow("g_out_attn", l),
                bias_spec(lambda b, s, _: (jnp.minimum(s, 2), 0, 0)),
                bias_spec(lambda b, s, _: (2, 0, 0))]
    args = [sinks, qm, qh, km, km, kh, vm, vm, vh, g, bias, bias]
    out_specs = [two(ATTN_WIDTH), one(ATTN_WIDTH, head)]
    out_shape = [jax.ShapeDtypeStruct(qm.shape, BF16), jax.ShapeDtypeStruct(qh.shape, BF16)]
    if cast is not None:
        _, cr, cc = cast.shape
        crows = cr // (B * npair)
        in_specs.append(pl.BlockSpec((None, crows, cc),
                                     lambda b, s, _: (cast_layer, b * npair + jnp.maximum(s - 1, 0), 0)))
        args.append(cast)
        out_specs.append(pl.BlockSpec((crows, cc), pair))
        out_shape.append(jax.ShapeDtypeStruct((cr, cc), BF16))
    return pl.pallas_call(
        functools.partial(_attn_kernel, l=l),
        grid_spec=pltpu.PrefetchScalarGridSpec(
            num_scalar_prefetch=1, grid=(B, npair + 1), in_specs=in_specs, out_specs=out_specs),
        out_shape=out_shape,
        compiler_params=_params("arbitrary", "arbitrary"),
        name="attn_prompt",
    )(*args)


def _ssm_params_kernel(ar_ref, ai_ref, ldt_ref, kk_ref, br_ref, bi_ref, tr_ref, ti_ref, bbr_ref, bbi_ref):
    ar, ai = ar_ref[...], ai_ref[...]
    dt = jnp.exp(ldt_ref[...])
    kk = kk_ref[...]
    mag = jnp.exp(dt * ar * kk)
    ang = dt * ai * kk
    tr = mag * jnp.cos(ang)
    ti = mag * jnp.sin(ang)
    tr_ref[...] = tr
    ti_ref[...] = ti
    abr, abi = tr[0:1], ti[0:1]
    den = ar * ar + ai * ai
    fr = ((abr - 1.0) * ar + abi * ai) / den
    fi = (abi * ar - (abr - 1.0) * ai) / den
    br, bi = br_ref[...], bi_ref[...]
    bbr_ref[...] = fr * br - fi * bi
    bbi_ref[...] = fr * bi + fi * br


def _ssm_params(A_re, A_im, log_dt, B_re, B_im, C_re, C_im):
    depth = A_re.shape[0]
    n = depth * SSM_GROUPS * SSM_STATE
    row = lambda t: t.astype(F32).reshape(1, n)
    ldt = jnp.broadcast_to(log_dt.astype(F32)[:, :, None], (depth, SSM_GROUPS, SSM_STATE)).reshape(1, n)
    kk = jnp.array(SSM_POW_ROWS + (0,) * (SUBLANES - len(SSM_POW_ROWS)), F32).reshape(SUBLANES, 1)
    chan_first = lambda t: t.astype(F32).reshape(n, SSM_GROUP_SIZE).T
    shapes = [(SUBLANES, n), (SUBLANES, n), (SSM_GROUP_SIZE, n), (SSM_GROUP_SIZE, n)]
    tr, ti, bbr, bbi = pl.pallas_call(
        _ssm_params_kernel,
        out_shape=[jax.ShapeDtypeStruct(s, F32) for s in shapes],
        name="ssm_params",
    )(row(A_re), row(A_im), ldt, kk, chan_first(B_re), chan_first(B_im))
    J, G8 = SSM_LANE_BLOCKS, LANES // SSM_GROUP_SIZE

    def lanes(t):
        return t.reshape(SUBLANES, depth, J, SSM_BLOCK_STATES).transpose(1, 0, 2, 3)

    pw = jnp.concatenate([lanes(tr), lanes(ti)], axis=-1).reshape(depth, SUBLANES, SSM_STATE_LANES)
    eye = jnp.eye(G8, dtype=F32)

    def bdiag(t):
        t = t.reshape(SSM_GROUP_SIZE, depth, J, G8, SSM_STATE).transpose(1, 2, 3, 0, 4)
        t = t[:, :, :, :, None, :] * eye[None, None, :, None, :, None]
        return t.reshape(depth, J, LANES, SSM_BLOCK_STATES)

    bblk = jnp.concatenate([bdiag(bbr), bdiag(bbi)], axis=-1).astype(BF16)

    def cdiag(t):
        t = t.astype(F32).reshape(depth, J, G8, SSM_GROUP_SIZE, SSM_STATE).transpose(0, 1, 2, 4, 3)
        t = t[:, :, :, :, None, :] * eye[None, None, :, None, :, None]
        return t.reshape(depth, J, SSM_BLOCK_STATES, LANES)

    cblk = jnp.concatenate([cdiag(C_re), -cdiag(C_im)], axis=2).astype(BF16)
    return pw, bblk, cblk


def _ssm_tail(y, u, d_ref, wg_ref, bg_ref, g_ref):
    y = y + d_ref[...] * u
    z = jax.nn.gelu(y)
    gate = jax.nn.sigmoid(jnp.dot(z.astype(BF16), wg_ref[...], preferred_element_type=F32) + bg_ref[...])
    return _rms(z * gate, g_ref[...])


def _ssm_sweep(x_scr, n, a_tabs, init, store):
    S = SSM_BLOCK_STATES
    fins = []
    for j0 in range(0, SSM_LANE_BLOCKS, 2):
        js = (j0, j0 + 1)

        def body(k, carry, js=js):
            r0 = pl.multiple_of(k * SUBLANES, SUBLANES)
            out = []
            for idx, j in enumerate(js):
                hr, hi = carry[2 * idx], carry[2 * idx + 1]
                base = j * 2 * S
                ar, ai = a_tabs[j]
                nhr = ar * hr - ai * hi + x_scr[pl.ds(r0, SUBLANES), base:base + S]
                nhi = ar * hi + ai * hr + x_scr[pl.ds(r0, SUBLANES), base + S:base + 2 * S]
                if store:
                    x_scr[pl.ds(r0, SUBLANES), base:base + S] = nhr
                    x_scr[pl.ds(r0, SUBLANES), base + S:base + 2 * S] = nhi
                out += [nhr, nhi]
            return tuple(out)

        c0 = tuple(t for j in js for t in init[j])
        res = jax.lax.fori_loop(0, n // SUBLANES, body, c0, unroll=True)
        fins += [(res[0], res[1]), (res[2], res[3])]
    return fins


def _ssm_rows(u, pow_row, pw_ref, bblk_ref, cblk_ref, x_scr, s_scr, carry_scr):
    n = u.shape[0]
    S = SSM_BLOCK_STATES
    ub = u.astype(BF16)
    for j in range(SSM_LANE_BLOCKS):
        x_scr[0:n, j * 2 * S:(j + 1) * 2 * S] = jnp.dot(ub[:, j * LANES:(j + 1) * LANES], bblk_ref[j],
                                                       preferred_element_type=F32)
    bc = lambda t: jnp.broadcast_to(t, (SUBLANES, S))
    re = lambda ref, r0, r1, j: ref[r0:r1, j * 2 * S:j * 2 * S + S]
    im = lambda ref, r0, r1, j: ref[r0:r1, j * 2 * S + S:(j + 1) * 2 * S]
    a_tabs = [(bc(re(pw_ref, 0, 1, j)), bc(im(pw_ref, 0, 1, j))) for j in range(SSM_LANE_BLOCKS)]
    zero = jnp.zeros((SUBLANES, S), F32)
    fins = _ssm_sweep(x_scr, n, a_tabs, [(zero, zero)] * SSM_LANE_BLOCKS, store=False)
    for j in range(SSM_LANE_BLOCKS):
        base = j * 2 * S
        cr, ci = re(pw_ref, pow_row, pow_row + 1, j), im(pw_ref, pow_row, pow_row + 1, j)
        sr, si = re(carry_scr, 0, 1, j), im(carry_scr, 0, 1, j)
        fr, fi = fins[j]
        for c in range(SUBLANES):
            s_scr[c:c + 1, base:base + S] = sr
            s_scr[c:c + 1, base + S:base + 2 * S] = si
            sr, si = cr * sr - ci * si + fr[c:c + 1], cr * si + ci * sr + fi[c:c + 1]
        carry_scr[:, base:base + S] = bc(sr)
        carry_scr[:, base + S:base + 2 * S] = bc(si)
    init = [(re(s_scr, 0, SUBLANES, j), im(s_scr, 0, SUBLANES, j)) for j in range(SSM_LANE_BLOCKS)]
    _ssm_sweep(x_scr, n, a_tabs, init, store=True)
    ys = [jnp.dot(x_scr[0:n, j * 2 * S:(j + 1) * 2 * S].astype(BF16), cblk_ref[j], preferred_element_type=F32)
          for j in range(SSM_LANE_BLOCKS)]
    return jnp.concatenate(ys, axis=1)


def _ssm_tile(u_ref, n, pow_row, seq_start, refs, o_ref, scr):
    pw_ref, bblk_ref, cblk_ref, d_ref, wg_ref, bg_ref, g_ref = refs
    up_scr, x_scr, s_scr, carry_scr, o_scr = scr
    q = n // SUBLANES
    for j in range(SSM_LANE_BLOCKS):
        for k in range(q):
            up_scr[k * SUBLANES:(k + 1) * SUBLANES, j * LANES:(j + 1) * LANES] = \
                u_ref[j, pl.ds(k, SUBLANES, stride=q), :]
    u = up_scr[0:n, :]
    if seq_start:
        p = jax.lax.broadcasted_iota(jnp.int32, (n, 1), 0)
        u = jnp.where((p % SUBLANES) * q + p // SUBLANES >= seq_start, u, 0.0)
    y = _ssm_rows(u, pow_row, pw_ref, bblk_ref, cblk_ref, x_scr, s_scr, carry_scr)
    out = _ssm_tail(y, u, d_ref, wg_ref, bg_ref, g_ref)
    for j in range(SSM_LANE_BLOCKS):
        for k in range(q):
            o_scr[j, pl.ds(k, SUBLANES, stride=q), :] = out[k * SUBLANES:(k + 1) * SUBLANES,
                                                            j * LANES:(j + 1) * LANES]
    o_ref[...] = jnp.concatenate([o_scr[j, 0:n, :] for j in range(SSM_LANE_BLOCKS)], axis=1).astype(o_ref.dtype)


def _ssm_kernel(um_ref, uh_ref, pw_ref, bblk_ref, cblk_ref, d_ref, wg_ref, bg_ref, g_ref, cast_in_ref,
                om_ref, oh_ref, st_ref, cast_out_ref, up_scr, x_scr, s_scr, carry_scr, o_scr):
    t = pl.program_id(1)
    refs = (pw_ref, bblk_ref, cblk_ref, d_ref, wg_ref, bg_ref, g_ref)
    scr = (up_scr, x_scr, s_scr, carry_scr, o_scr)

    @pl.when(t == 0)
    def _():
        carry_scr[...] = jnp.zeros_like(carry_scr)
        _ssm_tile(uh_ref, HEAD_ROWS, 1, SEQ0, refs, oh_ref, scr)

    @pl.when(t > 0)
    def _():
        cast_out_ref[...] = cast_in_ref[...].astype(BF16)
        _ssm_tile(um_ref, um_ref.shape[1], 2, 0, refs, om_ref, scr)
        st_ref[...] = carry_scr[...]


def _ssm(um, uh, pw, bblk, cblk, d, wg, l, bg, g, B, tm, cast):
    assert (1, HEAD_ROWS // SUBLANES, tm // SUBLANES) == SSM_POW_ROWS
    J = SSM_LANE_BLOCKS
    rm, rh = um.shape[1], uh.shape[1]
    nt = rm // (B * tm)
    tile = lambda b, t: b * nt + jnp.maximum(t - 1, 0)
    main = lambda b, t: (tile(b, t), 0)
    head = lambda b, t: (b, 0)
    _, cr, cc = cast.shape
    crows = cr // (B * nt)
    return pl.pallas_call(
        _ssm_kernel,
        grid=(B, nt + 1),
        in_specs=[pl.BlockSpec((J, tm, LANES), lambda b, t: (0, tile(b, t), 0)),
                  pl.BlockSpec((J, HEAD_ROWS, LANES), lambda b, t: (0, b, 0)),
                  _layer(pw.shape[1:], l), _layer(bblk.shape[1:], l), _layer(cblk.shape[1:], l),
                  _vrow("D_skip", l), _layer((SSM_WIDTH, SSM_WIDTH), l), _vrow("b_glu", l),
                  _vrow("g_out_ssm", l),
                  pl.BlockSpec((None, crows, cc), lambda b, t: (l, tile(b, t), 0))],
        out_specs=[pl.BlockSpec((tm, SSM_WIDTH), main), pl.BlockSpec((HEAD_ROWS, SSM_WIDTH), head),
                   pl.BlockSpec((None, SUBLANES, SSM_STATE_LANES), lambda b, t: (b, 0, 0)),
                   pl.BlockSpec((crows, cc), main)],
        out_shape=[jax.ShapeDtypeStruct((rm, SSM_WIDTH), BF16), jax.ShapeDtypeStruct((rh, SSM_WIDTH), BF16),
                   jax.ShapeDtypeStruct((B, SUBLANES, SSM_STATE_LANES), F32),
                   jax.ShapeDtypeStruct((cr, cc), BF16)],
        scratch_shapes=[pltpu.VMEM((tm, SSM_WIDTH), F32),
                        pltpu.VMEM((tm, SSM_STATE_LANES), F32),
                        pltpu.VMEM((SUBLANES, SSM_STATE_LANES), F32),
                        pltpu.VMEM((SUBLANES, SSM_STATE_LANES), F32),
                        pltpu.VMEM((J, tm, LANES), F32)],
        compiler_params=_params("arbitrary", "arbitrary"),
        name="ssm_prompt",
    )(um, uh, pw, bblk, cblk, d, wg, bg, g, cast)


def _state_from_lanes(s):
    s = s.reshape(s.shape[0], SSM_LANE_BLOCKS, 2, SSM_BLOCK_STATES)
    return (s[:, :, 0].reshape(-1, SSM_GROUPS, SSM_STATE), s[:, :, 1].reshape(-1, SSM_GROUPS, SSM_STATE))


def _pool_tail(d_groups, w_ref, sc_ref, g_ref):
    y = jnp.concatenate(
        [jnp.dot(d.astype(BF16), w_ref[gi], preferred_element_type=F32) for gi, d in enumerate(d_groups)], axis=1)
    return _rms(y * sc_ref[...], g_ref[...])


def _pool_rows(x, prev, pos0, w_ref, sc_ref, g_ref):
    n = x.shape[0]
    xe = jnp.concatenate([prev, x], axis=0)
    pos = pos0 + jax.lax.broadcasted_iota(jnp.int32, (n, 1), 0)
    ds = []
    for gi, w in enumerate(POOL_WINDOWS):
        gsl = slice(gi * POOL_GROUP, (gi + 1) * POOL_GROUP)
        s = xe[:, gsl]
        k = 1
        while k < w:
            s = s + pltpu.roll(s, k, 0)
            k *= 2
        cnt = jnp.clip(pos + 1, 1, w).astype(F32)
        ds.append(s[POOL_HALO:] / cnt - x[:, gsl])
    return _pool_tail(ds, w_ref, sc_ref, g_ref)


def _pool_kernel(xm_ref, halo_ref, xh_ref, w_ref, sc_ref, g_ref, cast_in_ref, om_ref, oh_ref, cast_out_ref):
    t = pl.program_id(1)
    tm = xm_ref.shape[0]

    @pl.when(t == 0)
    def _():
        hrow = jax.lax.broadcasted_iota(jnp.int32, (HEAD_ROWS, 1), 0)
        x = jnp.where(hrow >= SEQ0, xh_ref[...], 0.0)
        prev = jnp.zeros((POOL_HALO, POOL_WIDTH), F32)
        oh_ref[...] = _pool_rows(x, prev, -SEQ0, w_ref, sc_ref, g_ref).astype(oh_ref.dtype)

    @pl.when(t > 0)
    def _():
        prev = jnp.where(t == 1, xh_ref[HEAD_ROWS - POOL_HALO:, :], halo_ref[...])
        om_ref[...] = _pool_rows(xm_ref[...], prev, N_META + (t - 1) * tm, w_ref, sc_ref, g_ref).astype(om_ref.dtype)
        cast_out_ref[...] = cast_in_ref[...].astype(BF16)


def _pool(xm, xh, w, l, sc, g, B, tm, cast):
    nt = xm.shape[0] // (B * tm)
    r = tm // POOL_HALO
    main = lambda b, t: (b * nt + jnp.maximum(t - 1, 0), 0)
    halo = lambda b, t: (jnp.maximum((b * nt + t - 1) * r - 1, 0), 0)
    head = lambda b, t: (b, 0)
    _, cr, cc = cast.shape
    crows = cr // (B * nt)
    return pl.pallas_call(
        _pool_kernel,
        grid=(B, nt + 1),
        in_specs=[pl.BlockSpec((tm, POOL_WIDTH), main), pl.BlockSpec((POOL_HALO, POOL_WIDTH), halo),
                  pl.BlockSpec((HEAD_ROWS, POOL_WIDTH), head),
                  _layer(w.shape[1:], l), _vrow("pool_scale", l), _vrow("g_out_pool", l),
                  pl.BlockSpec((None, crows, cc), lambda b, t: (l, b * nt + jnp.maximum(t - 1, 0), 0))],
        out_specs=[pl.BlockSpec((tm, POOL_WIDTH), main), pl.BlockSpec((HEAD_ROWS, POOL_WIDTH), head),
                   pl.BlockSpec((crows, cc), main)],
        out_shape=[jax.ShapeDtypeStruct(xm.shape, BF16), jax.ShapeDtypeStruct(xh.shape, BF16),
                   jax.ShapeDtypeStruct((cr, cc), BF16)],
        compiler_params=_params("arbitrary", "arbitrary"),
        name="pool_prompt",
    )(xm, xm, xh, w, sc, g, cast)


def _shift_caches_kernel(k_ref, v_ref, nk_ref, nv_ref):
    for src, dst in ((k_ref, nk_ref), (v_ref, nv_ref)):
        for bb in range(DEC_STEP):
            dst[bb, 0:WINDOW - 1] = src[bb, 1:WINDOW]
            dst[bb, WINDOW - 1] = jnp.zeros((N_KV_HEADS, HEAD_DIM), F32)


def _shift_caches(cache_k, cache_v):
    depth, N = cache_k.shape[:2]
    blk = pl.BlockSpec((None, DEC_STEP, WINDOW, N_KV_HEADS, HEAD_DIM), lambda l, s: (l, s, 0, 0, 0))
    shape = jax.ShapeDtypeStruct(cache_k.shape, F32)
    return pl.pallas_call(
        _shift_caches_kernel,
        grid=(depth, N // DEC_STEP),
        in_specs=[blk, blk],
        out_specs=[blk, blk],
        out_shape=[shape, shape],
        compiler_params=_params("arbitrary", "arbitrary"),
        name="shift_caches",
    )(cache_k, cache_v)


def _attn_sample_kernel(q_ref, kn_ref, vn_ref, kc_ref, vc_ref, sink_ref, g_ref, a_in_ref, nk_in_ref, nv_in_ref,
                        a_ref, nk_ref, nv_ref, acc_scr):
    del a_in_ref, nk_in_ref, nv_in_ref
    step = pl.program_id(0)
    scale = HEAD_DIM ** -0.5
    for bb in range(DEC_STEP):
        outs = []
        for kh in range(N_KV_HEADS):
            ksl = slice(kh * HEAD_DIM, (kh + 1) * HEAD_DIM)
            qh = jnp.concatenate(
                [q_ref[bb:bb + 1, (kh * GQA_GROUP + g) * HEAD_DIM:(kh * GQA_GROUP + g + 1) * HEAD_DIM]
                 for g in range(GQA_GROUP)], axis=0)
            kn = kn_ref[bb:bb + 1, ksl]
            vn = vn_ref[bb:bb + 1, ksl]
            nk_ref[bb, 0, kh:kh + 1, :] = kn
            nv_ref[bb, 0, kh:kh + 1, :] = vn
            kc = kc_ref[bb, :, kh, :]
            vc = vc_ref[bb, :, kh, :]
            sc = jax.lax.dot_general(qh.astype(BF16), kc.astype(BF16), (((1,), (1,)), ((), ())),
                                     preferred_element_type=F32) * scale
            sn = jnp.sum(qh.astype(BF16).astype(F32) * kn.astype(BF16).astype(F32), axis=-1, keepdims=True) * scale
            sk = sink_ref[kh]
            m = jnp.maximum(jnp.maximum(jnp.max(sc, axis=-1, keepdims=True), sn), sk)
            pc = jnp.exp(sc - m)
            pn = jnp.exp(sn - m)
            denom = jnp.sum(pc, axis=-1, keepdims=True) + pn + jnp.exp(sk - m)
            o = jnp.dot(pc.astype(BF16), vc.astype(BF16), preferred_element_type=F32)
            o = (o + pn.astype(BF16).astype(F32) * vn.astype(BF16).astype(F32)) / denom
            outs.extend(o[g:g + 1] for g in range(GQA_GROUP))
        a = jnp.concatenate(outs, axis=1)
        acc_scr[pl.ds(step * DEC_STEP + bb, 1), :] = _rms(a, g_ref[...])

    @pl.when(step == pl.num_programs(0) - 1)
    def _():
        a_ref[...] = acc_scr[...].astype(a_ref.dtype)


def _attn_sample(qh, kh, vh, cache_k, cache_v, l, sinks, g, ah, nk, nv, N):
    rows = lambda w: pl.BlockSpec((DEC_STEP, w), lambda s: (s, 0))
    cache = pl.BlockSpec((None, DEC_STEP, WINDOW, N_KV_HEADS, HEAD_DIM), lambda s: (l, s, 0, 0, 0))
    last = pl.BlockSpec((None, DEC_STEP, 1, N_KV_HEADS, HEAD_DIM), lambda s: (l, s, WINDOW - 1, 0, 0))
    anyspec = pl.BlockSpec(memory_space=pl.ANY)
    return pl.pallas_call(
        _attn_sample_kernel,
        grid=(N // DEC_STEP,),
        in_specs=[rows(ATTN_WIDTH), rows(KV_WIDTH), rows(KV_WIDTH), cache, cache,
                  _layer((N_KV_HEADS, GQA_GROUP, 1), l), _vrow("g_out_attn", l),
                  anyspec, anyspec, anyspec],
        out_specs=[pl.BlockSpec((N, ATTN_WIDTH), lambda s: (0, 0)), last, last],
        out_shape=[jax.ShapeDtypeStruct(ah.shape, ah.dtype), jax.ShapeDtypeStruct(nk.shape, nk.dtype),
                   jax.ShapeDtypeStruct(nv.shape, nv.dtype)],
        scratch_shapes=[pltpu.VMEM((N, ATTN_WIDTH), F32)],
        input_output_aliases={7: 0, 8: 1, 9: 2},
        compiler_params=_params("arbitrary"),
        name="attn_sample",
    )(qh, kh, vh, cache_k, cache_v, sinks, g, ah, nk, nv)


def _mix_sample_kernel(u_ref, h0_ref, pw_ref, bblk_ref, cblk_ref, d_ref, wg_ref, bg_ref, gs_ref,
                       xp_ref, pb_ref, wp_ref, sc_ref, gp_ref, s_in_ref, p_in_ref, s_ref, st_ref, p_ref):
    del s_in_ref, p_in_ref
    S = SSM_BLOCK_STATES
    u = jnp.concatenate([u_ref[j] for j in range(SSM_LANE_BLOCKS)], axis=1)
    ub = u.astype(BF16)
    ys = []
    for j in range(SSM_LANE_BLOCKS):
        x = jnp.dot(ub[:, j * LANES:(j + 1) * LANES], bblk_ref[j], preferred_element_type=F32)
        base = j * 2 * S
        ar = pw_ref[0:1, base:base + S]
        ai = pw_ref[0:1, base + S:base + 2 * S]
        h0r = h0_ref[:, base:base + S]
        h0i = h0_ref[:, base + S:base + 2 * S]
        hr = x[:, 0:S] + ar * h0r - ai * h0i
        hi = x[:, S:] + ar * h0i + ai * h0r
        st_ref[:, base:base + S] = hr
        st_ref[:, base + S:base + 2 * S] = hi
        h = jnp.concatenate([hr, hi], axis=1).astype(BF16)
        ys.append(jnp.dot(h, cblk_ref[j], preferred_element_type=F32))
    s_ref[...] = _ssm_tail(jnp.concatenate(ys, axis=1), u, d_ref, wg_ref, bg_ref, gs_ref).astype(s_ref.dtype)

    xp = xp_ref[...]
    ds = []
    for gi, w in enumerate(POOL_WINDOWS):
        gsl = slice(gi * POOL_GROUP, (gi + 1) * POOL_GROUP)
        s = xp[:, gsl]
        for back in range(1, w):
            s = s + pb_ref[POOL_BUF - back][:, gsl]
        ds.append(s / float(w) - xp[:, gsl])
    p_ref[...] = _pool_tail(ds, wp_ref, sc_ref, gp_ref).astype(p_ref.dtype)


def _mix_sample(uh, h0, pw, bblk, cblk, d, wg, l, bg, gs, xph, pbuf, wp, _l, sc, gp, sh, ph):
    N = h0.shape[1]
    rows = lambda w: pl.BlockSpec((N, w), lambda i: (0, 0))
    anyspec = pl.BlockSpec(memory_space=pl.ANY)
    return pl.pallas_call(
        _mix_sample_kernel,
        grid=(1,),
        in_specs=[pl.BlockSpec((SSM_LANE_BLOCKS, N, LANES), lambda i: (0, 0, 0)),
                  _layer(h0.shape[1:], l), _layer(pw.shape[1:], l), _layer(bblk.shape[1:], l),
                  _layer(cblk.shape[1:], l),
                  _vrow("D_skip", l), _layer((SSM_WIDTH, SSM_WIDTH), l), _vrow("b_glu", l),
                  _vrow("g_out_ssm", l), rows(POOL_WIDTH), _layer(pbuf.shape[1:], l),
                  _layer(wp.shape[1:], l),
                  _vrow("pool_scale", l), _vrow("g_out_pool", l), anyspec, anyspec],
        out_specs=[rows(SSM_WIDTH), _full((N, SSM_STATE_LANES)), rows(POOL_WIDTH)],
        out_shape=[jax.ShapeDtypeStruct(sh.shape, sh.dtype), jax.ShapeDtypeStruct((N, SSM_STATE_LANES), F32),
                   jax.ShapeDtypeStruct(ph.shape, ph.dtype)],
        input_output_aliases={14: 0, 15: 2},
        compiler_params=_params("arbitrary"),
        name="mix_sample",
    )(uh, h0, pw, bblk, cblk, d, wg, bg, gs, xph, pbuf, wp, sc, gp, sh, ph)


def _outproj_kernel(x_ref, a_ref, s_ref, p_ref, w_ref, o_ref):
    o1 = ATTN_WIDTH
    o2 = o1 + SSM_WIDTH
    acc = x_ref[...]
    acc = acc + jnp.dot(a_ref[...], w_ref[0:o1, :], preferred_element_type=F32)
    acc = acc + jnp.dot(s_ref[...], w_ref[o1:o2, :], preferred_element_type=F32)
    acc = acc + jnp.dot(p_ref[...], w_ref[o2:, :], preferred_element_type=F32)
    o_ref[...] = acc


def _outproj(x, a, s, p, w, tm):
    R = x.shape[0]
    row = lambda i: (i, 0)
    return pl.pallas_call(
        _outproj_kernel,
        grid=(R // tm,),
        in_specs=[pl.BlockSpec((tm, D_MODEL), row), pl.BlockSpec((tm, ATTN_WIDTH), row),
                  pl.BlockSpec((tm, SSM_WIDTH), row), pl.BlockSpec((tm, POOL_WIDTH), row),
                  pl.BlockSpec((D_MODEL, D_MODEL), lambda i: (0, 0), pipeline_mode=pl.Buffered(1))],
        out_specs=pl.BlockSpec((tm, D_MODEL), row),
        out_shape=jax.ShapeDtypeStruct((R, D_MODEL), F32),
        compiler_params=_params("arbitrary"),
        name="outproj",
    )(x, a, s, p, w)


def _ffn_kernel(xm_ref, xh_ref, g_ref, w1_ref, w2_ref, om_ref, oh_ref, hm_scr, hh_scr):
    i = pl.program_id(0)
    f = pl.program_id(1)

    @pl.when(f == 0)
    def _():
        x = xm_ref[...]
        hm_scr[...] = _rms(x, g_ref[...]).astype(BF16)
        om_ref[...] = x

    @pl.when((f == 0) & (i == 0))
    def _():
        x = xh_ref[...]
        hh_scr[...] = _rms(x, g_ref[...]).astype(BF16)
        oh_ref[...] = x

    w1 = w1_ref[...]
    w2 = w2_ref[...]

    def mlp(h):
        h1 = jnp.dot(h, w1, preferred_element_type=F32)
        return jnp.dot(jnp.square(jnp.maximum(h1, 0.0)).astype(BF16), w2, preferred_element_type=F32)

    om_ref[...] += mlp(hm_scr[...])

    @pl.when(i == 0)
    def _():
        oh_ref[...] += mlp(hh_scr[...])


def _ffn(xm, xh, g, l, w1, w2, tm, tf):
    R, RH = xm.shape[0], xh.shape[0]
    return pl.pallas_call(
        _ffn_kernel,
        grid=(R // tm, D_FF // tf),
        in_specs=[pl.BlockSpec((tm, D_MODEL), lambda i, f: (i, 0)),
                  pl.BlockSpec((RH, D_MODEL), lambda i, f: (0, 0), pipeline_mode=pl.Buffered(1)),
                  _vrow("g_ffn", l),
                  pl.BlockSpec((D_MODEL, tf), lambda i, f: (0, f)),
                  pl.BlockSpec((tf, D_MODEL), lambda i, f: (f, 0))],
        out_specs=[pl.BlockSpec((tm, D_MODEL), lambda i, f: (i, 0)),
                   pl.BlockSpec((RH, D_MODEL), lambda i, f: (0, 0))],
        out_shape=[jax.ShapeDtypeStruct((R, D_MODEL), F32), jax.ShapeDtypeStruct((RH, D_MODEL), F32)],
        scratch_shapes=[pltpu.VMEM((tm, D_MODEL), BF16), pltpu.VMEM((RH, D_MODEL), BF16)],
        compiler_params=_params("arbitrary", "arbitrary"),
        name="ffn",
    )(xm, xh, g, w1, w2)


def _rope_tables(pos):
    half = ROT_DIM // 2
    inv = ROPE_THETA ** (-np.arange(0, ROT_DIM, 2, dtype=np.float64) / ROT_DIM)
    ang = np.asarray(pos, np.float64)[:, None] * inv
    cos, sin = np.cos(ang), np.sin(ang)
    n = ang.shape[0]
    z = np.zeros((n, HEAD_DIM - ROT_DIM))
    zh = np.zeros((n, half))
    rc = np.concatenate([cos, cos, z + 1.0], axis=1)
    rs1 = np.concatenate([zh, sin, z], axis=1)
    rs2 = np.concatenate([-sin, zh, z], axis=1)
    return tuple(t.astype(np.float32) for t in (rc, rs1, rs2))


def kernel(x_prompt, x_sample, cache_k, cache_v, state_ssm_re, state_ssm_im, state_pool, meta_tokens, g_mix, w_in, g_q, g_k, sinks, A_re, A_im, log_dt, B_re, B_im, C_re, C_im, D_skip, w_glu, b_glu, w_pool, pool_scale, g_out_attn, g_out_ssm, g_out_pool, w_out, g_ffn, w_ff1, w_ff2):
    B, T, _ = x_prompt.shape
    N = x_sample.shape[0]
    depth = w_in.shape[0]
    assert N <= SEQ0 and N % DEC_STEP == 0 and T % TM_FFN == 0
    meta = meta_tokens.astype(F32)
    head0 = jnp.concatenate([x_sample.reshape(N, D_MODEL), jnp.zeros((SEQ0 - N, D_MODEL), F32), meta], axis=0)
    head_rest = jnp.concatenate([jnp.zeros((SEQ0, D_MODEL), F32), meta], axis=0)
    xh = jnp.concatenate([head0] + [head_rest] * (B - 1), axis=0)
    xm = x_prompt.reshape(B * T, D_MODEL)

    rope_m = _rope_tables(N_META + np.arange(T))
    hr_ = np.arange(HEAD_ROWS)
    pos_h0 = np.where(hr_ < N, PAST_LEN, np.maximum(hr_ - SEQ0, 0))
    pos_h = np.concatenate([pos_h0] + [np.maximum(hr_ - SEQ0, 0)] * (B - 1))
    rope_h = _rope_tables(pos_h)

    wi = w_in[0].astype(BF16)
    wg_all, wp_all = w_glu.astype(BF16), w_pool.astype(BF16)
    ck = cache_k.astype(F32)
    cv = cache_v.astype(F32)
    RH = B * HEAD_ROWS

    named = dict(g_mix=g_mix, g_ffn=g_ffn, g_out_attn=g_out_attn, D_skip=D_skip, b_glu=b_glu,
                 g_out_ssm=g_out_ssm, pool_scale=pool_scale, g_out_pool=g_out_pool, g_q=g_q, g_k=g_k)
    vecs = jnp.concatenate([named[name].astype(F32) for name, _ in VEC_LAYOUT], axis=1)[:, None, :]
    pw, bblk, cblk = _ssm_params(A_re, A_im, log_dt, B_re, B_im, C_re, C_im)
    ssm_w = lambda l: (pw, bblk, cblk, vecs, wg_all, l, vecs, vecs)
    pool_w = lambda l: (wp_all, l, vecs, vecs)
    g_attn = vecs
    g_ffn_ = vecs
    sinks_flat = sinks.astype(F32).reshape(depth * N_HEADS)
    sinks_col = sinks.astype(F32).reshape(depth, N_KV_HEADS, GQA_GROUP, 1)
    bias = _attn_bias()
    h0 = jnp.concatenate([state_ssm_re.astype(F32).reshape(depth, N, SSM_LANE_BLOCKS, SSM_BLOCK_STATES),
                          state_ssm_im.astype(F32).reshape(depth, N, SSM_LANE_BLOCKS, SSM_BLOCK_STATES)],
                         axis=-1).reshape(depth, N, SSM_STATE_LANES)
    pbuf = state_pool.astype(F32).transpose(0, 2, 1, 3)

    nk, nv = _shift_caches(ck, cv)
    ks, vs, pls, sts, st_ss, phs = ([] for _ in range(6))
    for l in range(depth):
        in_w = (vecs, wi, l, vecs, vecs)
        qm, km, vm, um, pm, w1 = _inproj(xm, *in_w, rope_m, TM_PROJ, cast=w_ff1)
        qh, kh, vh, uh, ph = _inproj(xh, *in_w, rope_h, RH)

        if l + 1 < depth:
            am, ah, wi = _attn(sinks_flat, l, qm, qh, km, kh, vm, vh, g_attn, bias, B, cast=w_in, cast_layer=l + 1)
        else:
            am, ah = _attn(sinks_flat, l, qm, qh, km, kh, vm, vh, g_attn, bias, B)
        sm, sh, st, w2 = _ssm(um, uh, *ssm_w(l), B, TM_SEQ, w_ff2)
        plm, plh, wo = _pool(pm, ph, *pool_w(l), B, TM_SEQ, w_out)

        ah, nk, nv = _attn_sample(qh, kh, vh, ck, cv, l, sinks_col, g_attn, ah, nk, nv, N)
        sh, st_s, plh = _mix_sample(uh, h0, *ssm_w(l), ph, pbuf, *pool_w(l), sh, plh)

        xm = _outproj(xm, am, sm, plm, wo, TM_PROJ)
        xh = _outproj(xh, ah, sh, plh, wo, RH)
        xm, xh = _ffn(xm, xh, g_ffn_, l, w1, w2, TM_FFN, TF_FFN)

        ks.append(km.reshape(B, T, KV_WIDTH)[:, T - WINDOW:])
        vs.append(vm.reshape(B, T, KV_WIDTH)[:, T - WINDOW:])
        pls.append(pm.reshape(B, T, POOL_WIDTH)[:, T - POOL_BUF:])
        sts.append(st[:, 0])
        st_ss.append(st_s)
        phs.append(ph[:N])

    y_prompt = xm.reshape(B, T, D_MODEL)
    y_sample = xh[:N].reshape(N, 1, D_MODEL)
    heads = lambda t: jnp.stack(t).reshape(depth, -1, WINDOW, N_KV_HEADS, HEAD_DIM)
    p_re, p_im = _state_from_lanes(jnp.stack(sts).reshape(depth * B, SSM_STATE_LANES))
    s_re, s_im = _state_from_lanes(jnp.stack(st_ss).reshape(depth * N, SSM_STATE_LANES))
    st4 = lambda t, n: t.reshape(depth, n, SSM_GROUPS, SSM_STATE)
    s_pool = jnp.concatenate([state_pool.astype(F32)[:, :, 1:], jnp.stack(phs)[:, :, None]], axis=2)
    return (y_prompt, y_sample, heads(ks), heads(vs), st4(p_re, B), st4(p_im, B), jnp.stack(pls),
            nk, nv, st4(s_re, N), st4(s_im, N), s_pool)
```

```python
import functools
import math

import jax
import jax.numpy as jnp
import numpy as np
from jax.experimental import pallas as pl
from jax.experimental.pallas import tpu as pltpu

D_MODEL = 2048
N_META = 16
HEAD_DIM = 128
N_HEADS = 8
N_KV_HEADS = 2
GQA_GROUP = 4
ATTN_WIDTH = 1024
KV_WIDTH = 256
WINDOW = 128
BLOCK = 128
ROT_DIM = 32
ROPE_THETA = 500000.0
SSM_WIDTH = 512
SSM_GROUP_SIZE = 16
SSM_GROUPS = 32
SSM_STATE = 64
POOL_WIDTH = 512
POOL_WINDOWS = (2, 4, 8, 16)
POOL_GROUP = 128
POOL_BUF = 15
POOL_HALO = 16
IN_WIDTH = 2560
D_FF = 8192
EPS = 1e-6
PAST_LEN = 16384
LOG2E = math.log2(math.e)

HEAD_ROWS = BLOCK
SEQ0 = HEAD_ROWS - N_META
LANES = 128
SUBLANES = 8
SSM_LANE_BLOCKS = SSM_WIDTH // LANES
SSM_BLOCK_STATES = (LANES // SSM_GROUP_SIZE) * SSM_STATE
SSM_STATE_LANES = SSM_LANE_BLOCKS * 2 * SSM_BLOCK_STATES
VMEM_LIMIT = 56 * 1024 * 1024

TM_PROJ = 512
TM_FFN = 1024
TF_FFN = 512
TM_SEQ = 512
DEC_STEP = 8
SSM_POW_ROWS = (1, HEAD_ROWS // SUBLANES, TM_SEQ // SUBLANES)

BF16 = jnp.bfloat16
F32 = jnp.float32


def _params(*semantics):
    return pltpu.CompilerParams(dimension_semantics=semantics, vmem_limit_bytes=VMEM_LIMIT)


def _rms(x, g):
    return x * jax.lax.rsqrt(jnp.mean(x * x, axis=-1, keepdims=True) + EPS) * g


def _full(shape):
    n = len(shape)
    return pl.BlockSpec(shape, lambda *_: (0,) * n)


def _layer(shape, l):
    n = len(shape)
    return pl.BlockSpec((None, *shape), lambda *_: (l,) + (0,) * n)


VEC_LAYOUT = (("g_mix", D_MODEL), ("g_ffn", D_MODEL), ("g_out_attn", ATTN_WIDTH), ("D_skip", SSM_WIDTH),
              ("b_glu", SSM_WIDTH), ("g_out_ssm", SSM_WIDTH), ("pool_scale", POOL_WIDTH),
              ("g_out_pool", POOL_WIDTH), ("g_q", HEAD_DIM), ("g_k", HEAD_DIM))
VEC_WIDTH = dict(VEC_LAYOUT)
VEC_OFFSET = {name: sum(w for _, w in VEC_LAYOUT[:i]) for i, (name, _) in enumerate(VEC_LAYOUT)}
assert all(VEC_OFFSET[name] % w == 0 for name, w in VEC_LAYOUT)


def _vrow(name, l):
    w = VEC_WIDTH[name]
    return pl.BlockSpec((None, 1, w), lambda *_: (l, 0, VEC_OFFSET[name] // w))


def _inproj_kernel(x_ref, g_ref, w_ref, gq_ref, gk_ref, rc_ref, rs1_ref, rs2_ref, *rest):
    if len(rest) == 7:
        cast_in_ref, q_ref, k_ref, v_ref, u_ref, xp_ref, cast_out_ref = rest
        cast_out_ref[...] = cast_in_ref[...].astype(BF16)
    else:
        q_ref, k_ref, v_ref, u_ref, xp_ref = rest
    h = _rms(x_ref[...], g_ref[...]).astype(BF16)
    proj = jnp.dot(h, w_ref[...], preferred_element_type=F32)
    rc, rs1, rs2 = rc_ref[...], rs1_ref[...], rs2_ref[...]

    def head(t, g):
        t = _rms(t, g)
        return t * rc + pltpu.roll(t, 16, 1) * rs1 + pltpu.roll(t, LANES - 16, 1) * rs2

    for hd in range(N_HEADS):
        sl = slice(hd * HEAD_DIM, (hd + 1) * HEAD_DIM)
        q_ref[:, sl] = head(proj[:, sl], gq_ref[...])
    for hd in range(N_KV_HEADS):
        sl = slice(hd * HEAD_DIM, (hd + 1) * HEAD_DIM)
        k_ref[:, sl] = head(proj[:, ATTN_WIDTH + hd * HEAD_DIM:ATTN_WIDTH + (hd + 1) * HEAD_DIM], gk_ref[...])
    o2 = ATTN_WIDTH + KV_WIDTH
    o3 = o2 + KV_WIDTH
    o4 = o3 + SSM_WIDTH
    v_ref[...] = proj[:, o2:o3]
    for j in range(SSM_LANE_BLOCKS):
        u_ref[j] = proj[:, o3 + j * LANES:o3 + (j + 1) * LANES]
    xp_ref[...] = proj[:, o4:]


def _cast_side_job(w, l, steps):
    _, r, c = w.shape
    rows = r // steps
    return (pl.BlockSpec((None, rows, c), lambda i: (l, i, 0)), pl.BlockSpec((rows, c), lambda i: (i, 0)),
            jax.ShapeDtypeStruct((r, c), BF16))


def _inproj(x, g, w, l, gq, gk, rope, tm, cast=None):
    R = x.shape[0]
    tiles_per_rope = rope[0].shape[0] // tm
    row = lambda i: (i, 0)
    rrow = lambda i: (i % tiles_per_rope, 0)
    flat = lambda w_: (pl.BlockSpec((tm, w_), row), jax.ShapeDtypeStruct((R, w_), F32))
    u_out = (pl.BlockSpec((SSM_LANE_BLOCKS, tm, LANES), lambda i: (0, i, 0)),
             jax.ShapeDtypeStruct((SSM_LANE_BLOCKS, R, LANES), F32))
    outs = [flat(ATTN_WIDTH), flat(KV_WIDTH), flat(KV_WIDTH), u_out, flat(POOL_WIDTH)]
    w_spec = pl.BlockSpec((D_MODEL, IN_WIDTH), lambda i: (0, 0), pipeline_mode=pl.Buffered(1))
    in_specs = [pl.BlockSpec((tm, D_MODEL), row), _vrow("g_mix", l), w_spec,
                _vrow("g_q", l), _vrow("g_k", l),
                pl.BlockSpec((tm, LANES), rrow), pl.BlockSpec((tm, LANES), rrow),
                pl.BlockSpec((tm, LANES), rrow)]
    args = [x, g, w, gq, gk, *rope]
    if cast is not None:
        c_in, c_out, c_shape = _cast_side_job(cast, l, R // tm)
        in_specs.append(c_in)
        args.append(cast)
        outs.append((c_out, c_shape))
    return pl.pallas_call(
        _inproj_kernel,
        grid=(R // tm,),
        in_specs=in_specs,
        out_specs=[o[0] for o in outs],
        out_shape=[o[1] for o in outs],
        compiler_params=_params("arbitrary"),
        name="inproj",
    )(*args)


def _attn_bias():
    rows = GQA_GROUP * BLOCK
    i, r, c = np.meshgrid(np.arange(3), np.arange(rows) % BLOCK, np.arange(2 * BLOCK), indexing="ij")
    diff = BLOCK + r - c
    krow = (i - 1) * BLOCK + c
    mask = (diff >= 0) & (diff <= WINDOW) & (krow >= SEQ0)
    return np.where(mask, 0.0, -np.inf).astype(np.float32)


def _attn_block(q_blk, kp_blk, kc_blk, vp_blk, vc_blk, bias, sink_ref, l, g):
    rows = GQA_GROUP * BLOCK
    rgrp = jax.lax.broadcasted_iota(jnp.int32, (rows, 1), 0) // BLOCK
    outs = []
    for kh in range(N_KV_HEADS):
        ksl = slice(kh * HEAD_DIM, (kh + 1) * HEAD_DIM)
        qh = jnp.concatenate(
            [q_blk[:, (kh * GQA_GROUP + h) * HEAD_DIM:(kh * GQA_GROUP + h + 1) * HEAD_DIM]
             for h in range(GQA_GROUP)], axis=0).astype(BF16)
        kk = jnp.concatenate([kp_blk[:, ksl], kc_blk[:, ksl]], axis=0).astype(BF16)
        vv = jnp.concatenate([vp_blk[:, ksl], vc_blk[:, ksl]], axis=0).astype(BF16)
        s = jax.lax.dot_general(qh, kk, (((1,), (1,)), ((), ())),
                                preferred_element_type=F32) * (HEAD_DIM ** -0.5 * LOG2E) + bias
        sk = jnp.zeros((rows, 1), F32)
        for h in range(GQA_GROUP):
            sk = jnp.where(rgrp == h, sink_ref[l * N_HEADS + kh * GQA_GROUP + h] * LOG2E, sk)
        m = jnp.maximum(jnp.max(s, axis=-1, keepdims=True), sk)
        p = jnp.exp2(s - m)
        denom = jnp.sum(p, axis=-1, keepdims=True) + jnp.exp2(sk - m)
        o = jnp.dot(p.astype(BF16), vv, preferred_element_type=F32) / denom
        outs.extend(o[h * BLOCK:(h + 1) * BLOCK] for h in range(GQA_GROUP))
    return _rms(jnp.concatenate(outs, axis=1), g)


def _attn_kernel(sink_ref, qm_ref, qh_ref, kpm_ref, kcm_ref, kh_ref, vpm_ref, vcm_ref, vh_ref, g_ref,
                 bias_a_ref, bias_b_ref, *rest, l):
    s = pl.program_id(1)
    if len(rest) == 4:
        cast_in_ref, om_ref, oh_ref, cast_out_ref = rest

        @pl.when(s > 0)
        def _():
            cast_out_ref[...] = cast_in_ref[...].astype(BF16)
    else:
        om_ref, oh_ref = rest
    blk = functools.partial(_attn_block, sink_ref=sink_ref, l=l, g=g_ref[...])

    @pl.when(s == 0)
    def _():
        a = blk(qh_ref[...], kpm_ref[...], kh_ref[...], vpm_ref[...], vh_ref[...], bias_a_ref[...])
        oh_ref[...] = a.astype(oh_ref.dtype)

    @pl.when(s > 0)
    def _():
        k1, k2 = kcm_ref[0:BLOCK, :], kcm_ref[BLOCK:2 * BLOCK, :]
        v1, v2 = vcm_ref[0:BLOCK, :], vcm_ref[BLOCK:2 * BLOCK, :]
        kp = jnp.where(s == 1, kh_ref[...], kpm_ref[...])
        vp = jnp.where(s == 1, vh_ref[...], vpm_ref[...])
        a1 = blk(qm_ref[0:BLOCK, :], kp, k1, vp, v1, bias_a_ref[...])
        a2 = blk(qm_ref[BLOCK:2 * BLOCK, :], k1, k2, v1, v2, bias_b_ref[...])
        om_ref[0:BLOCK, :] = a1.astype(om_ref.dtype)
        om_ref[BLOCK:2 * BLOCK, :] = a2.astype(om_ref.dtype)


def _attn(sinks, l, qm, qh, km, kh, vm, vh, g, bias, B, cast=None, cast_layer=0):
    nb = qm.shape[0] // (B * BLOCK)
    npair = nb // 2
    pair = lambda b, s, _: (b * npair + jnp.maximum(s - 1, 0), 0)
    prev = lambda b, s, _: (b * nb + jnp.maximum(2 * s - 3, 0), 0)
    head = lambda b, s, _: (b, 0)
    one = lambda w, im: pl.BlockSpec((BLOCK, w), im)
    two = lambda w: pl.BlockSpec((2 * BLOCK, w), pair)
    bias_spec = lambda im: pl.BlockSpec((None,) + bias.shape[1:], im)
    in_specs = [two(ATTN_WIDTH), one(ATTN_WIDTH, head),
                one(KV_WIDTH, prev), two(KV_WIDTH), one(KV_WIDTH, head),
                one(KV_WIDTH, prev), two(KV_WIDTH), one(KV_WIDTH, head),
                _vrow("g_out_attn", l),
                bias_spec(lambda b, s, _: (jnp.minimum(s, 2), 0, 0)),
                bias_spec(lambda b, s, _: (2, 0, 0))]
    args = [sinks, qm, qh, km, km, kh, vm, vm, vh, g, bias, bias]
    out_specs = [two(ATTN_WIDTH), one(ATTN_WIDTH, head)]
    out_shape = [jax.ShapeDtypeStruct(qm.shape, BF16), jax.ShapeDtypeStruct(qh.shape, BF16)]
    if cast is not None:
        _, cr, cc = cast.shape
        crows = cr // (B * npair)
        in_specs.append(pl.BlockSpec((None, crows, cc),
                                     lambda b, s, _: (cast_layer, b * npair + jnp.maximum(s - 1, 0), 0)))
        args.append(cast)
        out_specs.append(pl.BlockSpec((crows, cc), pair))
        out_shape.append(jax.ShapeDtypeStruct((cr, cc), BF16))
    return pl.pallas_call(
        functools.partial(_attn_kernel, l=l),
        grid_spec=pltpu.PrefetchScalarGridSpec(
            num_scalar_prefetch=1, grid=(B, npair + 1), in_specs=in_specs, out_specs=out_specs),
        out_shape=out_shape,
        compiler_params=_params("arbitrary", "arbitrary"),
        name="attn_prompt",
    )(*args)


def _ssm_params_kernel(ar_ref, ai_ref, ldt_ref, kk_ref, br_ref, bi_ref, tr_ref, ti_ref, bbr_ref, bbi_ref):
    ar, ai = ar_ref[...], ai_ref[...]
    dt = jnp.exp(ldt_ref[...])
    kk = kk_ref[...]
    mag = jnp.exp(dt * ar * kk)
    ang = dt * ai * kk
    tr = mag * jnp.cos(ang)
    ti = mag * jnp.sin(ang)
    tr_ref[...] = tr
    ti_ref[...] = ti
    abr, abi = tr[0:1], ti[0:1]
    den = ar * ar + ai * ai
    fr = ((abr - 1.0) * ar + abi * ai) / den
    fi = (abi * ar - (abr - 1.0) * ai) / den
    br, bi = br_ref[...], bi_ref[...]
    bbr_ref[...] = fr * br - fi * bi
    bbi_ref[...] = fr * bi + fi * br


def _ssm_params(A_re, A_im, log_dt, B_re, B_im, C_re, C_im):
    depth = A_re.shape[0]
    n = depth * SSM_GROUPS * SSM_STATE
    row = lambda t: t.astype(F32).reshape(1, n)
    ldt = jnp.broadcast_to(log_dt.astype(F32)[:, :, None], (depth, SSM_GROUPS, SSM_STATE)).reshape(1, n)
    kk = jnp.array(SSM_POW_ROWS + (0,) * (SUBLANES - len(SSM_POW_ROWS)), F32).reshape(SUBLANES, 1)
    chan_first = lambda t: t.astype(F32).reshape(n, SSM_GROUP_SIZE).T
    shapes = [(SUBLANES, n), (SUBLANES, n), (SSM_GROUP_SIZE, n), (SSM_GROUP_SIZE, n)]
    tr, ti, bbr, bbi = pl.pallas_call(
        _ssm_params_kernel,
        out_shape=[jax.ShapeDtypeStruct(s, F32) for s in shapes],
        name="ssm_params",
    )(row(A_re), row(A_im), ldt, kk, chan_first(B_re), chan_first(B_im))
    J, G8 = SSM_LANE_BLOCKS, LANES // SSM_GROUP_SIZE

    def lanes(t):
        return t.reshape(SUBLANES, depth, J, SSM_BLOCK_STATES).transpose(1, 0, 2, 3)

    pw = jnp.concatenate([lanes(tr), lanes(ti)], axis=-1).reshape(depth, SUBLANES, SSM_STATE_LANES)
    eye = jnp.eye(G8, dtype=F32)

    def bdiag(t):
        t = t.reshape(SSM_GROUP_SIZE, depth, J, G8, SSM_STATE).transpose(1, 2, 3, 0, 4)
        t = t[:, :, :, :, None, :] * eye[None, None, :, None, :, None]
        return t.reshape(depth, J, LANES, SSM_BLOCK_STATES)

    bblk = jnp.concatenate([bdiag(bbr), bdiag(bbi)], axis=-1).astype(BF16)

    def cdiag(t):
        t = t.astype(F32).reshape(depth, J, G8, SSM_GROUP_SIZE, SSM_STATE).transpose(0, 1, 2, 4, 3)
        t = t[:, :, :, :, None, :] * eye[None, None, :, None, :, None]
        return t.reshape(depth, J, SSM_BLOCK_STATES, LANES)

    cblk = jnp.concatenate([cdiag(C_re), -cdiag(C_im)], axis=2).astype(BF16)
    return pw, bblk, cblk


def _ssm_tail(y, u, d_ref, wg_ref, bg_ref, g_ref):
    y = y + d_ref[...] * u
    z = jax.nn.gelu(y)
    gate = jax.nn.sigmoid(jnp.dot(z.astype(BF16), wg_ref[...], preferred_element_type=F32) + bg_ref[...])
    return _rms(z * gate, g_ref[...])


def _ssm_sweep(x_scr, n, a_tabs, init, store):
    S = SSM_BLOCK_STATES
    fins = []
    for j0 in range(0, SSM_LANE_BLOCKS, 2):
        js = (j0, j0 + 1)

        def body(k, carry, js=js):
            r0 = pl.multiple_of(k * SUBLANES, SUBLANES)
            out = []
            for idx, j in enumerate(js):
                hr, hi = carry[2 * idx], carry[2 * idx + 1]
                base = j * 2 * S
                ar, ai = a_tabs[j]
                nhr = ar * hr - ai * hi + x_scr[pl.ds(r0, SUBLANES), base:base + S]
                nhi = ar * hi + ai * hr + x_scr[pl.ds(r0, SUBLANES), base + S:base + 2 * S]
                if store:
                    x_scr[pl.ds(r0, SUBLANES), base:base + S] = nhr
                    x_scr[pl.ds(r0, SUBLANES), base + S:base + 2 * S] = nhi
                out += [nhr, nhi]
            return tuple(out)

        c0 = tuple(t for j in js for t in init[j])
        res = jax.lax.fori_loop(0, n // SUBLANES, body, c0, unroll=True)
        fins += [(res[0], res[1]), (res[2], res[3])]
    return fins


def _ssm_rows(u, pow_row, pw_ref, bblk_ref, cblk_ref, x_scr, s_scr, carry_scr):
    n = u.shape[0]
    S = SSM_BLOCK_STATES
    ub = u.astype(BF16)
    for j in range(SSM_LANE_BLOCKS):
        x_scr[0:n, j * 2 * S:(j + 1) * 2 * S] = jnp.dot(ub[:, j * LANES:(j + 1) * LANES], bblk_ref[j],
                                                       preferred_element_type=F32)
    bc = lambda t: jnp.broadcast_to(t, (SUBLANES, S))
    re = lambda ref, r0, r1, j: ref[r0:r1, j * 2 * S:j * 2 * S + S]
    im = lambda ref, r0, r1, j: ref[r0:r1, j * 2 * S + S:(j + 1) * 2 * S]
    a_tabs = [(bc(re(pw_ref, 0, 1, j)), bc(im(pw_ref, 0, 1, j))) for j in range(SSM_LANE_BLOCKS)]
    zero = jnp.zeros((SUBLANES, S), F32)
    fins = _ssm_sweep(x_scr, n, a_tabs, [(zero, zero)] * SSM_LANE_BLOCKS, store=False)
    for j in range(SSM_LANE_BLOCKS):
        base = j * 2 * S
        cr, ci = re(pw_ref, pow_row, pow_row + 1, j), im(pw_ref, pow_row, pow_row + 1, j)
        sr, si = re(carry_scr, 0, 1, j), im(carry_scr, 0, 1, j)
        fr, fi = fins[j]
        for c in range(SUBLANES):
            s_scr[c:c + 1, base:base + S] = sr
            s_scr[c:c + 1, base + S:base + 2 * S] = si
            sr, si = cr * sr - ci * si + fr[c:c + 1], cr * si + ci * sr + fi[c:c + 1]
        carry_scr[:, base:base + S] = bc(sr)
        carry_scr[:, base + S:base + 2 * S] = bc(si)
    init = [(re(s_scr, 0, SUBLANES, j), im(s_scr, 0, SUBLANES, j)) for j in range(SSM_LANE_BLOCKS)]
    _ssm_sweep(x_scr, n, a_tabs, init, store=True)
    ys = [jnp.dot(x_scr[0:n, j * 2 * S:(j + 1) * 2 * S].astype(BF16), cblk_ref[j], preferred_element_type=F32)
          for j in range(SSM_LANE_BLOCKS)]
    return jnp.concatenate(ys, axis=1)


def _ssm_tile(u_ref, n, pow_row, seq_start, refs, o_ref, scr):
    pw_ref, bblk_ref, cblk_ref, d_ref, wg_ref, bg_ref, g_ref = refs
    up_scr, x_scr, s_scr, carry_scr, o_scr = scr
    q = n // SUBLANES
    for j in range(SSM_LANE_BLOCKS):
        for k in range(q):
            up_scr[k * SUBLANES:(k + 1) * SUBLANES, j * LANES:(j + 1) * LANES] = \
                u_ref[j, pl.ds(k, SUBLANES, stride=q), :]
    u = up_scr[0:n, :]
    if seq_start:
        p = jax.lax.broadcasted_iota(jnp.int32, (n, 1), 0)
        u = jnp.where((p % SUBLANES) * q + p // SUBLANES >= seq_start, u, 0.0)
    y = _ssm_rows(u, pow_row, pw_ref, bblk_ref, cblk_ref, x_scr, s_scr, carry_scr)
    out = _ssm_tail(y, u, d_ref, wg_ref, bg_ref, g_ref)
    for j in range(SSM_LANE_BLOCKS):
        for k in range(q):
            o_scr[j, pl.ds(k, SUBLANES, stride=q), :] = out[k * SUBLANES:(k + 1) * SUBLANES,
                                                            j * LANES:(j + 1) * LANES]
    o_ref[...] = jnp.concatenate([o_scr[j, 0:n, :] for j in range(SSM_LANE_BLOCKS)], axis=1).astype(o_ref.dtype)


def _seq_kernel(um_ref, uh_ref, pw_ref, bblk_ref, cblk_ref, d_ref, wg_ref, bg_ref, gs_ref, cast_a_ref,
                xm_ref, halo_ref, xh_ref, wp_ref, sc_ref, gp_ref, cast_b_ref,
                som_ref, soh_ref, st_ref, cast_a_out_ref, pom_ref, poh_ref, cast_b_out_ref,
                up_scr, x_scr, s_scr, carry_scr, o_scr):
    t = pl.program_id(1)
    tm = xm_ref.shape[0]
    ssm_refs = (pw_ref, bblk_ref, cblk_ref, d_ref, wg_ref, bg_ref, gs_ref)
    scr = (up_scr, x_scr, s_scr, carry_scr, o_scr)

    @pl.when(t == 0)
    def _():
        carry_scr[...] = jnp.zeros_like(carry_scr)
        _ssm_tile(uh_ref, HEAD_ROWS, 1, SEQ0, ssm_refs, soh_ref, scr)
        hrow = jax.lax.broadcasted_iota(jnp.int32, (HEAD_ROWS, 1), 0)
        x = jnp.where(hrow >= SEQ0, xh_ref[...], 0.0)
        prev = jnp.zeros((POOL_HALO, POOL_WIDTH), F32)
        poh_ref[...] = _pool_rows(x, prev, -SEQ0, wp_ref, sc_ref, gp_ref).astype(poh_ref.dtype)

    @pl.when(t > 0)
    def _():
        cast_a_out_ref[...] = cast_a_ref[...].astype(BF16)
        cast_b_out_ref[...] = cast_b_ref[...].astype(BF16)
        _ssm_tile(um_ref, um_ref.shape[1], 2, 0, ssm_refs, som_ref, scr)
        st_ref[...] = carry_scr[...]
        prev = jnp.where(t == 1, xh_ref[HEAD_ROWS - POOL_HALO:, :], halo_ref[...])
        pom_ref[...] = _pool_rows(xm_ref[...], prev, N_META + (t - 1) * tm, wp_ref, sc_ref, gp_ref).astype(pom_ref.dtype)


def _seq_mixers(um, uh, xm, xh, vecs, pw, bblk, cblk, wg, wp, l, B, tm, cast_a, cast_b):
    assert (1, HEAD_ROWS // SUBLANES, tm // SUBLANES) == SSM_POW_ROWS
    J = SSM_LANE_BLOCKS
    rm, rh = um.shape[1], uh.shape[1]
    nt = rm // (B * tm)
    r = tm // POOL_HALO
    tile = lambda b, t: b * nt + jnp.maximum(t - 1, 0)
    main = lambda b, t: (tile(b, t), 0)
    head = lambda b, t: (b, 0)
    halo = lambda b, t: (jnp.maximum((b * nt + t - 1) * r - 1, 0), 0)

    def slab(w):
        _, cr, cc = w.shape
        rows = cr // (B * nt)
        return (pl.BlockSpec((None, rows, cc), lambda b, t: (l, tile(b, t), 0)), pl.BlockSpec((rows, cc), main),
                jax.ShapeDtypeStruct((cr, cc), BF16))

    (a_in, a_out, a_shape), (b_in, b_out, b_shape) = slab(cast_a), slab(cast_b)
    bf = lambda n, w: jax.ShapeDtypeStruct((n, w), BF16)
    return pl.pallas_call(
        _seq_kernel,
        grid=(B, nt + 1),
        in_specs=[pl.BlockSpec((J, tm, LANES), lambda b, t: (0, tile(b, t), 0)),
                  pl.BlockSpec((J, HEAD_ROWS, LANES), lambda b, t: (0, b, 0)),
                  _layer(pw.shape[1:], l), _layer(bblk.shape[1:], l), _layer(cblk.shape[1:], l),
                  _vrow("D_skip", l), _layer((SSM_WIDTH, SSM_WIDTH), l), _vrow("b_glu", l),
                  _vrow("g_out_ssm", l), a_in,
                  pl.BlockSpec((tm, POOL_WIDTH), main), pl.BlockSpec((POOL_HALO, POOL_WIDTH), halo),
                  pl.BlockSpec((HEAD_ROWS, POOL_WIDTH), head),
                  _layer(wp.shape[1:], l), _vrow("pool_scale", l), _vrow("g_out_pool", l), b_in],
        out_specs=[pl.BlockSpec((tm, SSM_WIDTH), main), pl.BlockSpec((HEAD_ROWS, SSM_WIDTH), head),
                   pl.BlockSpec((None, SUBLANES, SSM_STATE_LANES), lambda b, t: (b, 0, 0)), a_out,
                   pl.BlockSpec((tm, POOL_WIDTH), main), pl.BlockSpec((HEAD_ROWS, POOL_WIDTH), head), b_out],
        out_shape=[bf(rm, SSM_WIDTH), bf(rh, SSM_WIDTH),
                   jax.ShapeDtypeStruct((B, SUBLANES, SSM_STATE_LANES), F32), a_shape,
                   bf(rm, POOL_WIDTH), bf(rh, POOL_WIDTH), b_shape],
        scratch_shapes=[pltpu.VMEM((tm, SSM_WIDTH), F32),
                        pltpu.VMEM((tm, SSM_STATE_LANES), F32),
                        pltpu.VMEM((SUBLANES, SSM_STATE_LANES), F32),
                        pltpu.VMEM((SUBLANES, SSM_STATE_LANES), F32),
                        pltpu.VMEM((J, tm, LANES), F32)],
        compiler_params=_params("arbitrary", "arbitrary"),
        name="seq_mixers",
    )(um, uh, pw, bblk, cblk, vecs, wg, vecs, vecs, cast_a, xm, xm, xh, wp, vecs, vecs, cast_b)


def _state_from_lanes(s):
    s = s.reshape(s.shape[0], SSM_LANE_BLOCKS, 2, SSM_BLOCK_STATES)
    return (s[:, :, 0].reshape(-1, SSM_GROUPS, SSM_STATE), s[:, :, 1].reshape(-1, SSM_GROUPS, SSM_STATE))


def _pool_tail(d_groups, w_ref, sc_ref, g_ref):
    y = jnp.concatenate(
        [jnp.dot(d.astype(BF16), w_ref[gi], preferred_element_type=F32) for gi, d in enumerate(d_groups)], axis=1)
    return _rms(y * sc_ref[...], g_ref[...])


def _pool_rows(x, prev, pos0, w_ref, sc_ref, g_ref):
    n = x.shape[0]
    xe = jnp.concatenate([prev, x], axis=0)
    pos = pos0 + jax.lax.broadcasted_iota(jnp.int32, (n, 1), 0)
    ds = []
    for gi, w in enumerate(POOL_WINDOWS):
        gsl = slice(gi * POOL_GROUP, (gi + 1) * POOL_GROUP)
        s = xe[:, gsl]
        k = 1
        while k < w:
            s = s + pltpu.roll(s, k, 0)
            k *= 2
        cnt = jnp.clip(pos + 1, 1, w).astype(F32)
        ds.append(s[POOL_HALO:] / cnt - x[:, gsl])
    return _pool_tail(ds, w_ref, sc_ref, g_ref)


def _shift_caches_kernel(k_ref, v_ref, nk_ref, nv_ref):
    for src, dst in ((k_ref, nk_ref), (v_ref, nv_ref)):
        for bb in range(DEC_STEP):
            dst[bb, 0:WINDOW - 1] = src[bb, 1:WINDOW]
            dst[bb, WINDOW - 1] = jnp.zeros((N_KV_HEADS, HEAD_DIM), F32)


def _shift_caches(cache_k, cache_v):
    depth, N = cache_k.shape[:2]
    blk = pl.BlockSpec((None, DEC_STEP, WINDOW, N_KV_HEADS, HEAD_DIM), lambda l, s: (l, s, 0, 0, 0))
    shape = jax.ShapeDtypeStruct(cache_k.shape, F32)
    return pl.pallas_call(
        _shift_caches_kernel,
        grid=(depth, N // DEC_STEP),
        in_specs=[blk, blk],
        out_specs=[blk, blk],
        out_shape=[shape, shape],
        compiler_params=_params("arbitrary", "arbitrary"),
        name="shift_caches",
    )(cache_k, cache_v)


def _attn_sample_kernel(q_ref, kn_ref, vn_ref, kc_ref, vc_ref, sink_ref, g_ref, a_in_ref, nk_in_ref, nv_in_ref,
                        a_ref, nk_ref, nv_ref, acc_scr):
    del a_in_ref, nk_in_ref, nv_in_ref
    step = pl.program_id(0)
    scale = HEAD_DIM ** -0.5
    for bb in range(DEC_STEP):
        outs = []
        for kh in range(N_KV_HEADS):
            ksl = slice(kh * HEAD_DIM, (kh + 1) * HEAD_DIM)
            qh = jnp.concatenate(
                [q_ref[bb:bb + 1, (kh * GQA_GROUP + g) * HEAD_DIM:(kh * GQA_GROUP + g + 1) * HEAD_DIM]
                 for g in range(GQA_GROUP)], axis=0)
            kn = kn_ref[bb:bb + 1, ksl]
            vn = vn_ref[bb:bb + 1, ksl]
            nk_ref[bb, 0, kh:kh + 1, :] = kn
            nv_ref[bb, 0, kh:kh + 1, :] = vn
            kc = kc_ref[bb, :, kh, :]
            vc = vc_ref[bb, :, kh, :]
            sc = jax.lax.dot_general(qh.astype(BF16), kc.astype(BF16), (((1,), (1,)), ((), ())),
                                     preferred_element_type=F32) * scale
            sn = jnp.sum(qh.astype(BF16).astype(F32) * kn.astype(BF16).astype(F32), axis=-1, keepdims=True) * scale
            sk = sink_ref[kh]
            m = jnp.maximum(jnp.maximum(jnp.max(sc, axis=-1, keepdims=True), sn), sk)
            pc = jnp.exp(sc - m)
            pn = jnp.exp(sn - m)
            denom = jnp.sum(pc, axis=-1, keepdims=True) + pn + jnp.exp(sk - m)
            o = jnp.dot(pc.astype(BF16), vc.astype(BF16), preferred_element_type=F32)
            o = (o + pn.astype(BF16).astype(F32) * vn.astype(BF16).astype(F32)) / denom
            outs.extend(o[g:g + 1] for g in range(GQA_GROUP))
        a = jnp.concatenate(outs, axis=1)
        acc_scr[pl.ds(step * DEC_STEP + bb, 1), :] = _rms(a, g_ref[...])

    @pl.when(step == pl.num_programs(0) - 1)
    def _():
        a_ref[...] = acc_scr[...].astype(a_ref.dtype)


def _attn_sample(qh, kh, vh, cache_k, cache_v, l, sinks, g, ah, nk, nv, N):
    rows = lambda w: pl.BlockSpec((DEC_STEP, w), lambda s: (s, 0))
    cache = pl.BlockSpec((None, DEC_STEP, WINDOW, N_KV_HEADS, HEAD_DIM), lambda s: (l, s, 0, 0, 0))
    last = pl.BlockSpec((None, DEC_STEP, 1, N_KV_HEADS, HEAD_DIM), lambda s: (l, s, WINDOW - 1, 0, 0))
    anyspec = pl.BlockSpec(memory_space=pl.ANY)
    return pl.pallas_call(
        _attn_sample_kernel,
        grid=(N // DEC_STEP,),
        in_specs=[rows(ATTN_WIDTH), rows(KV_WIDTH), rows(KV_WIDTH), cache, cache,
                  _layer((N_KV_HEADS, GQA_GROUP, 1), l), _vrow("g_out_attn", l),
                  anyspec, anyspec, anyspec],
        out_specs=[pl.BlockSpec((N, ATTN_WIDTH), lambda s: (0, 0)), last, last],
        out_shape=[jax.ShapeDtypeStruct(ah.shape, ah.dtype), jax.ShapeDtypeStruct(nk.shape, nk.dtype),
                   jax.ShapeDtypeStruct(nv.shape, nv.dtype)],
        scratch_shapes=[pltpu.VMEM((N, ATTN_WIDTH), F32)],
        input_output_aliases={7: 0, 8: 1, 9: 2},
        compiler_params=_params("arbitrary"),
        name="attn_sample",
    )(qh, kh, vh, cache_k, cache_v, sinks, g, ah, nk, nv)


def _mix_sample_kernel(u_ref, h0_ref, pw_ref, bblk_ref, cblk_ref, d_ref, wg_ref, bg_ref, gs_ref,
                       xp_ref, pb_ref, wp_ref, sc_ref, gp_ref, s_in_ref, p_in_ref, s_ref, st_ref, p_ref):
    del s_in_ref, p_in_ref
    S = SSM_BLOCK_STATES
    u = jnp.concatenate([u_ref[j] for j in range(SSM_LANE_BLOCKS)], axis=1)
    ub = u.astype(BF16)
    ys = []
    for j in range(SSM_LANE_BLOCKS):
        x = jnp.dot(ub[:, j * LANES:(j + 1) * LANES], bblk_ref[j], preferred_element_type=F32)
        base = j * 2 * S
        ar = pw_ref[0:1, base:base + S]
        ai = pw_ref[0:1, base + S:base + 2 * S]
        h0r = h0_ref[:, base:base + S]
        h0i = h0_ref[:, base + S:base + 2 * S]
        hr = x[:, 0:S] + ar * h0r - ai * h0i
        hi = x[:, S:] + ar * h0i + ai * h0r
        st_ref[:, base:base + S] = hr
        st_ref[:, base + S:base + 2 * S] = hi
        h = jnp.concatenate([hr, hi], axis=1).astype(BF16)
        ys.append(jnp.dot(h, cblk_ref[j], preferred_element_type=F32))
    s_ref[...] = _ssm_tail(jnp.concatenate(ys, axis=1), u, d_ref, wg_ref, bg_ref, gs_ref).astype(s_ref.dtype)

    xp = xp_ref[...]
    ds = []
    for gi, w in enumerate(POOL_WINDOWS):
        gsl = slice(gi * POOL_GROUP, (gi + 1) * POOL_GROUP)
        s = xp[:, gsl]
        for back in range(1, w):
            s = s + pb_ref[POOL_BUF - back][:, gsl]
        ds.append(s / float(w) - xp[:, gsl])
    p_ref[...] = _pool_tail(ds, wp_ref, sc_ref, gp_ref).astype(p_ref.dtype)


def _mix_sample(uh, h0, pw, bblk, cblk, d, wg, l, bg, gs, xph, pbuf, wp, _l, sc, gp, sh, ph):
    N = h0.shape[1]
    rows = lambda w: pl.BlockSpec((N, w), lambda i: (0, 0))
    anyspec = pl.BlockSpec(memory_space=pl.ANY)
    return pl.pallas_call(
        _mix_sample_kernel,
        grid=(1,),
        in_specs=[pl.BlockSpec((SSM_LANE_BLOCKS, N, LANES), lambda i: (0, 0, 0)),
                  _layer(h0.shape[1:], l), _layer(pw.shape[1:], l), _layer(bblk.shape[1:], l),
                  _layer(cblk.shape[1:], l),
                  _vrow("D_skip", l), _layer((SSM_WIDTH, SSM_WIDTH), l), _vrow("b_glu", l),
                  _vrow("g_out_ssm", l), rows(POOL_WIDTH), _layer(pbuf.shape[1:], l),
                  _layer(wp.shape[1:], l),
                  _vrow("pool_scale", l), _vrow("g_out_pool", l), anyspec, anyspec],
        out_specs=[rows(SSM_WIDTH), _full((N, SSM_STATE_LANES)), rows(POOL_WIDTH)],
        out_shape=[jax.ShapeDtypeStruct(sh.shape, sh.dtype), jax.ShapeDtypeStruct((N, SSM_STATE_LANES), F32),
                   jax.ShapeDtypeStruct(ph.shape, ph.dtype)],
        input_output_aliases={14: 0, 15: 2},
        compiler_params=_params("arbitrary"),
        name="mix_sample",
    )(uh, h0, pw, bblk, cblk, d, wg, bg, gs, xph, pbuf, wp, sc, gp, sh, ph)


def _outproj_kernel(x_ref, a_ref, s_ref, p_ref, w_ref, o_ref):
    o1 = ATTN_WIDTH
    o2 = o1 + SSM_WIDTH
    acc = x_ref[...]
    acc = acc + jnp.dot(a_ref[...], w_ref[0:o1, :], preferred_element_type=F32)
    acc = acc + jnp.dot(s_ref[...], w_ref[o1:o2, :], preferred_element_type=F32)
    acc = acc + jnp.dot(p_ref[...], w_ref[o2:, :], preferred_element_type=F32)
    o_ref[...] = acc


def _outproj(x, a, s, p, w, tm):
    R = x.shape[0]
    row = lambda i: (i, 0)
    return pl.pallas_call(
        _outproj_kernel,
        grid=(R // tm,),
        in_specs=[pl.BlockSpec((tm, D_MODEL), row), pl.BlockSpec((tm, ATTN_WIDTH), row),
                  pl.BlockSpec((tm, SSM_WIDTH), row), pl.BlockSpec((tm, POOL_WIDTH), row),
                  pl.BlockSpec((D_MODEL, D_MODEL), lambda i: (0, 0), pipeline_mode=pl.Buffered(1))],
        out_specs=pl.BlockSpec((tm, D_MODEL), row),
        out_shape=jax.ShapeDtypeStruct((R, D_MODEL), F32),
        compiler_params=_params("arbitrary"),
        name="outproj",
    )(x, a, s, p, w)


def _ffn_kernel(xm_ref, xh_ref, g_ref, w1_ref, w2_ref, om_ref, oh_ref, hm_scr, hh_scr):
    i = pl.program_id(0)
    f = pl.program_id(1)

    @pl.when(f == 0)
    def _():
        x = xm_ref[...]
        hm_scr[...] = _rms(x, g_ref[...]).astype(BF16)
        om_ref[...] = x

    @pl.when((f == 0) & (i == 0))
    def _():
        x = xh_ref[...]
        hh_scr[...] = _rms(x, g_ref[...]).astype(BF16)
        oh_ref[...] = x

    w1 = w1_ref[...]
    w2 = w2_ref[...]

    def mlp(h):
        h1 = jnp.dot(h, w1, preferred_element_type=F32)
        return jnp.dot(jnp.square(jnp.maximum(h1, 0.0)).astype(BF16), w2, preferred_element_type=F32)

    om_ref[...] += mlp(hm_scr[...])

    @pl.when(i == 0)
    def _():
        oh_ref[...] += mlp(hh_scr[...])


def _ffn(xm, xh, g, l, w1, w2, tm, tf):
    R, RH = xm.shape[0], xh.shape[0]
    return pl.pallas_call(
        _ffn_kernel,
        grid=(R // tm, D_FF // tf),
        in_specs=[pl.BlockSpec((tm, D_MODEL), lambda i, f: (i, 0)),
                  pl.BlockSpec((RH, D_MODEL), lambda i, f: (0, 0), pipeline_mode=pl.Buffered(1)),
                  _vrow("g_ffn", l),
                  pl.BlockSpec((D_MODEL, tf), lambda i, f: (0, f)),
                  pl.BlockSpec((tf, D_MODEL), lambda i, f: (f, 0))],
        out_specs=[pl.BlockSpec((tm, D_MODEL), lambda i, f: (i, 0)),
                   pl.BlockSpec((RH, D_MODEL), lambda i, f: (0, 0))],
        out_shape=[jax.ShapeDtypeStruct((R, D_MODEL), F32), jax.ShapeDtypeStruct((RH, D_MODEL), F32)],
        scratch_shapes=[pltpu.VMEM((tm, D_MODEL), BF16), pltpu.VMEM((RH, D_MODEL), BF16)],
        compiler_params=_params("arbitrary", "arbitrary"),
        name="ffn",
    )(xm, xh, g, w1, w2)


def _rope_tables(pos):
    half = ROT_DIM // 2
    inv = ROPE_THETA ** (-np.arange(0, ROT_DIM, 2, dtype=np.float64) / ROT_DIM)
    ang = np.asarray(pos, np.float64)[:, None] * inv
    cos, sin = np.cos(ang), np.sin(ang)
    n = ang.shape[0]
    z = np.zeros((n, HEAD_DIM - ROT_DIM))
    zh = np.zeros((n, half))
    rc = np.concatenate([cos, cos, z + 1.0], axis=1)
    rs1 = np.concatenate([zh, sin, z], axis=1)
    rs2 = np.concatenate([-sin, zh, z], axis=1)
    return tuple(t.astype(np.float32) for t in (rc, rs1, rs2))


def kernel(x_prompt, x_sample, cache_k, cache_v, state_ssm_re, state_ssm_im, state_pool, meta_tokens, g_mix, w_in, g_q, g_k, sinks, A_re, A_im, log_dt, B_re, B_im, C_re, C_im, D_skip, w_glu, b_glu, w_pool, pool_scale, g_out_attn, g_out_ssm, g_out_pool, w_out, g_ffn, w_ff1, w_ff2):
    B, T, _ = x_prompt.shape
    N = x_sample.shape[0]
    depth = w_in.shape[0]
    assert N <= SEQ0 and N % DEC_STEP == 0 and T % TM_FFN == 0
    meta = meta_tokens.astype(F32)
    head0 = jnp.concatenate([x_sample.reshape(N, D_MODEL), jnp.zeros((SEQ0 - N, D_MODEL), F32), meta], axis=0)
    head_rest = jnp.concatenate([jnp.zeros((SEQ0, D_MODEL), F32), meta], axis=0)
    xh = jnp.concatenate([head0] + [head_rest] * (B - 1), axis=0)
    xm = x_prompt.reshape(B * T, D_MODEL)

    rope_m = _rope_tables(N_META + np.arange(T))
    hr_ = np.arange(HEAD_ROWS)
    pos_h0 = np.where(hr_ < N, PAST_LEN, np.maximum(hr_ - SEQ0, 0))
    pos_h = np.concatenate([pos_h0] + [np.maximum(hr_ - SEQ0, 0)] * (B - 1))
    rope_h = _rope_tables(pos_h)

    wi = w_in[0].astype(BF16)
    wg_all, wp_all = w_glu.astype(BF16), w_pool.astype(BF16)
    ck = cache_k.astype(F32)
    cv = cache_v.astype(F32)
    RH = B * HEAD_ROWS

    named = dict(g_mix=g_mix, g_ffn=g_ffn, g_out_attn=g_out_attn, D_skip=D_skip, b_glu=b_glu,
                 g_out_ssm=g_out_ssm, pool_scale=pool_scale, g_out_pool=g_out_pool, g_q=g_q, g_k=g_k)
    vecs = jnp.concatenate([named[name].astype(F32) for name, _ in VEC_LAYOUT], axis=1)[:, None, :]
    pw, bblk, cblk = _ssm_params(A_re, A_im, log_dt, B_re, B_im, C_re, C_im)
    ssm_w = lambda l: (pw, bblk, cblk, vecs, wg_all, l, vecs, vecs)
    pool_w = lambda l: (wp_all, l, vecs, vecs)
    g_attn = vecs
    g_ffn_ = vecs
    sinks_flat = sinks.astype(F32).reshape(depth * N_HEADS)
    sinks_col = sinks.astype(F32).reshape(depth, N_KV_HEADS, GQA_GROUP, 1)
    bias = _attn_bias()
    h0 = jnp.concatenate([state_ssm_re.astype(F32).reshape(depth, N, SSM_LANE_BLOCKS, SSM_BLOCK_STATES),
                          state_ssm_im.astype(F32).reshape(depth, N, SSM_LANE_BLOCKS, SSM_BLOCK_STATES)],
                         axis=-1).reshape(depth, N, SSM_STATE_LANES)
    pbuf = state_pool.astype(F32).transpose(0, 2, 1, 3)

    nk, nv = _shift_caches(ck, cv)
    ks, vs, pls, sts, st_ss, phs = ([] for _ in range(6))
    for l in range(depth):
        in_w = (vecs, wi, l, vecs, vecs)
        qm, km, vm, um, pm, w1 = _inproj(xm, *in_w, rope_m, TM_PROJ, cast=w_ff1)
        qh, kh, vh, uh, ph = _inproj(xh, *in_w, rope_h, RH)

        if l + 1 < depth:
            am, ah, wi = _attn(sinks_flat, l, qm, qh, km, kh, vm, vh, g_attn, bias, B, cast=w_in, cast_layer=l + 1)
        else:
            am, ah = _attn(sinks_flat, l, qm, qh, km, kh, vm, vh, g_attn, bias, B)
        sm, sh, st, w2, plm, plh, wo = _seq_mixers(um, uh, pm, ph, vecs, pw, bblk, cblk, wg_all, wp_all, l, B, TM_SEQ,
                                                   w_ff2, w_out)

        ah, nk, nv = _attn_sample(qh, kh, vh, ck, cv, l, sinks_col, g_attn, ah, nk, nv, N)
        sh, st_s, plh = _mix_sample(uh, h0, *ssm_w(l), ph, pbuf, *pool_w(l), sh, plh)

        xm = _outproj(xm, am, sm, plm, wo, TM_PROJ)
        xh = _outproj(xh, ah, sh, plh, wo, RH)
        xm, xh = _ffn(xm, xh, g_ffn_, l, w1, w2, TM_FFN, TF_FFN)

        ks.append(km.reshape(B, T, KV_WIDTH)[:, T - WINDOW:])
        vs.append(vm.reshape(B, T, KV_WIDTH)[:, T - WINDOW:])
        pls.append(pm.reshape(B, T, POOL_WIDTH)[:, T - POOL_BUF:])
        sts.append(st[:, 0])
        st_ss.append(st_s)
        phs.append(ph[:N])

    y_prompt = xm.reshape(B, T, D_MODEL)
    y_sample = xh[:N].reshape(N, 1, D_MODEL)
    heads = lambda t: jnp.stack(t).reshape(depth, -1, WINDOW, N_KV_HEADS, HEAD_DIM)
    p_re, p_im = _state_from_lanes(jnp.stack(sts).reshape(depth * B, SSM_STATE_LANES))
    s_re, s_im = _state_from_lanes(jnp.stack(st_ss).reshape(depth * N, SSM_STATE_LANES))
    st4 = lambda t, n: t.reshape(depth, n, SSM_GROUPS, SSM_STATE)
    s_pool = jnp.concatenate([state_pool.astype(F32)[:, :, 1:], jnp.stack(phs)[:, :, None]], axis=2)
    return (y_prompt, y_sample, heads(ks), heads(vs), st4(p_re, B), st4(p_im, B), jnp.stack(pls),
            nk, nv, st4(s_re, N), st4(s_im, N), s_pool)
```

```python
import functools
import math

import jax
import jax.numpy as jnp
import numpy as np
from jax.experimental import pallas as pl
from jax.experimental.pallas import tpu as pltpu

D_MODEL = 2048
N_META = 16
HEAD_DIM = 128
N_HEADS = 8
N_KV_HEADS = 2
GQA_GROUP = 4
ATTN_WIDTH = 1024
KV_WIDTH = 256
WINDOW = 128
BLOCK = 128
ROT_DIM = 32
ROPE_THETA = 500000.0
SSM_WIDTH = 512
SSM_GROUP_SIZE = 16
SSM_GROUPS = 32
SSM_STATE = 64
POOL_WIDTH = 512
POOL_WINDOWS = (2, 4, 8, 16)
POOL_GROUP = 128
POOL_BUF = 15
POOL_HALO = 16
IN_WIDTH = 2560
D_FF = 8192
EPS = 1e-6
PAST_LEN = 16384
LOG2E = math.log2(math.e)

HEAD_ROWS = BLOCK
SEQ0 = HEAD_ROWS - N_META
LANES = 128
SUBLANES = 8
SSM_LANE_BLOCKS = SSM_WIDTH // LANES
SSM_BLOCK_STATES = (LANES // SSM_GROUP_SIZE) * SSM_STATE
SSM_STATE_LANES = SSM_LANE_BLOCKS * 2 * SSM_BLOCK_STATES
VMEM_LIMIT = 56 * 1024 * 1024

TM_PROJ = 512
TM_FFN = 1024
TF_FFN = 512
TM_SEQ = 512
DEC_STEP = 8
SSM_POW_ROWS = (1, HEAD_ROWS // SUBLANES, TM_SEQ // SUBLANES)

BF16 = jnp.bfloat16
F32 = jnp.float32


def _params(*semantics):
    return pltpu.CompilerParams(dimension_semantics=semantics, vmem_limit_bytes=VMEM_LIMIT)


def _rms(x, g):
    return x * jax.lax.rsqrt(jnp.mean(x * x, axis=-1, keepdims=True) + EPS) * g


def _full(shape):
    n = len(shape)
    return pl.BlockSpec(shape, lambda *_: (0,) * n)


def _layer(shape, l):
    n = len(shape)
    return pl.BlockSpec((None, *shape), lambda *_: (l,) + (0,) * n)


VEC_LAYOUT = (("g_mix", D_MODEL), ("g_ffn", D_MODEL), ("g_out_attn", ATTN_WIDTH), ("D_skip", SSM_WIDTH),
              ("b_glu", SSM_WIDTH), ("g_out_ssm", SSM_WIDTH), ("pool_scale", POOL_WIDTH),
              ("g_out_pool", POOL_WIDTH), ("g_q", HEAD_DIM), ("g_k", HEAD_DIM))
VEC_WIDTH = dict(VEC_LAYOUT)
VEC_OFFSET = {name: sum(w for _, w in VEC_LAYOUT[:i]) for i, (name, _) in enumerate(VEC_LAYOUT)}
assert all(VEC_OFFSET[name] % w == 0 for name, w in VEC_LAYOUT)


def _vrow(name, l):
    w = VEC_WIDTH[name]
    return pl.BlockSpec((None, 1, w), lambda *_: (l, 0, VEC_OFFSET[name] // w))


def _inproj_rows(x, g_ref, w_ref, gq_ref, gk_ref, rope_refs, out_refs):
    q_ref, k_ref, v_ref, u_ref, xp_ref = out_refs
    h = _rms(x, g_ref[...]).astype(BF16)
    proj = jnp.dot(h, w_ref[...], preferred_element_type=F32)
    rc, rs1, rs2 = (r[...] for r in rope_refs)

    def head(t, g):
        t = _rms(t, g)
        return t * rc + pltpu.roll(t, 16, 1) * rs1 + pltpu.roll(t, LANES - 16, 1) * rs2

    for hd in range(N_HEADS):
        sl = slice(hd * HEAD_DIM, (hd + 1) * HEAD_DIM)
        q_ref[:, sl] = head(proj[:, sl], gq_ref[...])
    for hd in range(N_KV_HEADS):
        sl = slice(hd * HEAD_DIM, (hd + 1) * HEAD_DIM)
        k_ref[:, sl] = head(proj[:, ATTN_WIDTH + hd * HEAD_DIM:ATTN_WIDTH + (hd + 1) * HEAD_DIM], gk_ref[...])
    o2 = ATTN_WIDTH + KV_WIDTH
    o3 = o2 + KV_WIDTH
    o4 = o3 + SSM_WIDTH
    v_ref[...] = proj[:, o2:o3]
    for j in range(SSM_LANE_BLOCKS):
        u_ref[j] = proj[:, o3 + j * LANES:o3 + (j + 1) * LANES]
    xp_ref[...] = proj[:, o4:]


def _inproj_kernel(xm_ref, xh_ref, g_ref, w_ref, gq_ref, gk_ref, rcm_ref, rs1m_ref, rs2m_ref,
                   rch_ref, rs1h_ref, rs2h_ref, cast_in_ref, *outs):
    main_outs, head_outs, cast_out_ref = outs[0:5], outs[5:10], outs[10]
    i = pl.program_id(0)
    last = pl.num_programs(0) - 1

    @pl.when(i < last)
    def _():
        cast_out_ref[...] = cast_in_ref[...].astype(BF16)
        _inproj_rows(xm_ref[...], g_ref, w_ref, gq_ref, gk_ref, (rcm_ref, rs1m_ref, rs2m_ref), main_outs)

    @pl.when(i == last)
    def _():
        _inproj_rows(xh_ref[...], g_ref, w_ref, gq_ref, gk_ref, (rch_ref, rs1h_ref, rs2h_ref), head_outs)


def _inproj(xm, xh, vecs, w, l, rope_m, rope_h, tm, cast):
    R, RH = xm.shape[0], xh.shape[0]
    nm = R // tm
    tiles_per_rope = rope_m[0].shape[0] // tm
    tile = lambda i: jnp.minimum(i, nm - 1)
    row = lambda i: (tile(i), 0)
    rrow = lambda i: (tile(i) % tiles_per_rope, 0)
    once = pl.Buffered(1)
    full = lambda n, w_: pl.BlockSpec((n, w_), lambda i: (0, 0))

    def out_set(n, rows, im, uim):
        flat = lambda w_: (pl.BlockSpec((n, w_), im), jax.ShapeDtypeStruct((rows, w_), F32))
        u_out = (pl.BlockSpec((SSM_LANE_BLOCKS, n, LANES), uim),
                 jax.ShapeDtypeStruct((SSM_LANE_BLOCKS, rows, LANES), F32))
        return [flat(ATTN_WIDTH), flat(KV_WIDTH), flat(KV_WIDTH), u_out, flat(POOL_WIDTH)]

    outs = (out_set(tm, R, row, lambda i: (0, tile(i), 0))
            + out_set(RH, RH, lambda i: (0, 0), lambda i: (0, 0, 0)))
    _, cr, cc = cast.shape
    crows = cr // nm
    outs.append((pl.BlockSpec((crows, cc), row), jax.ShapeDtypeStruct((cr, cc), BF16)))
    in_specs = [pl.BlockSpec((tm, D_MODEL), row),
                pl.BlockSpec((RH, D_MODEL), lambda i: (0, 0), pipeline_mode=once),
                _vrow("g_mix", l),
                pl.BlockSpec((D_MODEL, IN_WIDTH), lambda i: (0, 0), pipeline_mode=once),
                _vrow("g_q", l), _vrow("g_k", l),
                pl.BlockSpec((tm, LANES), rrow), pl.BlockSpec((tm, LANES), rrow), pl.BlockSpec((tm, LANES), rrow),
                full(RH, LANES), full(RH, LANES), full(RH, LANES),
                pl.BlockSpec((None, crows, cc), lambda i: (l, tile(i), 0))]
    res = pl.pallas_call(
        _inproj_kernel,
        grid=(nm + 1,),
        in_specs=in_specs,
        out_specs=[o[0] for o in outs],
        out_shape=[o[1] for o in outs],
        compiler_params=_params("arbitrary"),
        name="inproj",
    )(xm, xh, vecs, w, vecs, vecs, *rope_m, *rope_h, cast)
    return res[0:5], res[5:10], res[10]


def _attn_bias():
    rows = GQA_GROUP * BLOCK
    i, r, c = np.meshgrid(np.arange(3), np.arange(rows) % BLOCK, np.arange(2 * BLOCK), indexing="ij")
    diff = BLOCK + r - c
    krow = (i - 1) * BLOCK + c
    mask = (diff >= 0) & (diff <= WINDOW) & (krow >= SEQ0)
    return np.where(mask, 0.0, -np.inf).astype(np.float32)


def _attn_block(q_blk, kp_blk, kc_blk, vp_blk, vc_blk, bias, sink_ref, l, g):
    rows = GQA_GROUP * BLOCK
    rgrp = jax.lax.broadcasted_iota(jnp.int32, (rows, 1), 0) // BLOCK
    outs = []
    for kh in range(N_KV_HEADS):
        ksl = slice(kh * HEAD_DIM, (kh + 1) * HEAD_DIM)
        qh = jnp.concatenate(
            [q_blk[:, (kh * GQA_GROUP + h) * HEAD_DIM:(kh * GQA_GROUP + h + 1) * HEAD_DIM]
             for h in range(GQA_GROUP)], axis=0).astype(BF16)
        kk = jnp.concatenate([kp_blk[:, ksl], kc_blk[:, ksl]], axis=0).astype(BF16)
        vv = jnp.concatenate([vp_blk[:, ksl], vc_blk[:, ksl]], axis=0).astype(BF16)
        s = jax.lax.dot_general(qh, kk, (((1,), (1,)), ((), ())),
                                preferred_element_type=F32) * (HEAD_DIM ** -0.5 * LOG2E) + bias
        sk = jnp.zeros((rows, 1), F32)
        for h in range(GQA_GROUP):
            sk = jnp.where(rgrp == h, sink_ref[l * N_HEADS + kh * GQA_GROUP + h] * LOG2E, sk)
        m = jnp.maximum(jnp.max(s, axis=-1, keepdims=True), sk)
        p = jnp.exp2(s - m)
        denom = jnp.sum(p, axis=-1, keepdims=True) + jnp.exp2(sk - m)
        o = jnp.dot(p.astype(BF16), vv, preferred_element_type=F32) / denom
        outs.extend(o[h * BLOCK:(h + 1) * BLOCK] for h in range(GQA_GROUP))
    return _rms(jnp.concatenate(outs, axis=1), g)


def _attn_kernel(sink_ref, qm_ref, qh_ref, kpm_ref, kcm_ref, kh_ref, vpm_ref, vcm_ref, vh_ref, g_ref,
                 bias_a_ref, bias_b_ref, *rest, l):
    s = pl.program_id(1)
    if len(rest) == 4:
        cast_in_ref, om_ref, oh_ref, cast_out_ref = rest

        @pl.when(s > 0)
        def _():
            cast_out_ref[...] = cast_in_ref[...].astype(BF16)
    else:
        om_ref, oh_ref = rest
    blk = functools.partial(_attn_block, sink_ref=sink_ref, l=l, g=g_ref[...])

    @pl.when(s == 0)
    def _():
        a = blk(qh_ref[...], kpm_ref[...], kh_ref[...], vpm_ref[...], vh_ref[...], bias_a_ref[...])
        oh_ref[...] = a.astype(oh_ref.dtype)

    @pl.when(s > 0)
    def _():
        k1, k2 = kcm_ref[0:BLOCK, :], kcm_ref[BLOCK:2 * BLOCK, :]
        v1, v2 = vcm_ref[0:BLOCK, :], vcm_ref[BLOCK:2 * BLOCK, :]
        kp = jnp.where(s == 1, kh_ref[...], kpm_ref[...])
        vp = jnp.where(s == 1, vh_ref[...], vpm_ref[...])
        a1 = blk(qm_ref[0:BLOCK, :], kp, k1, vp, v1, bias_a_ref[...])
        a2 = blk(qm_ref[BLOCK:2 * BLOCK, :], k1, k2, v1, v2, bias_b_ref[...])
        om_ref[0:BLOCK, :] = a1.astype(om_ref.dtype)
        om_ref[BLOCK:2 * BLOCK, :] = a2.astype(om_ref.dtype)


def _attn(sinks, l, qm, qh, km, kh, vm, vh, g, bias, B, cast=None, cast_layer=0):
    nb = qm.shape[0] // (B * BLOCK)
    npair = nb // 2
    pair = lambda b, s, _: (b * npair + jnp.maximum(s - 1, 0), 0)
    prev = lambda b, s, _: (b * nb + jnp.maximum(2 * s - 3, 0), 0)
    head = lambda b, s, _: (b, 0)
    one = lambda w, im: pl.BlockSpec((BLOCK, w), im)
    two = lambda w: pl.BlockSpec((2 * BLOCK, w), pair)
    bias_spec = lambda im: pl.BlockSpec((None,) + bias.shape[1:], im)
    in_specs = [two(ATTN_WIDTH), one(ATTN_WIDTH, head),
                one(KV_WIDTH, prev), two(KV_WIDTH), one(KV_WIDTH, head),
                one(KV_WIDTH, prev), two(KV_WIDTH), one(KV_WIDTH, head),
                _vrow("g_out_attn", l),
                bias_spec(lambda b, s, _: (jnp.minimum(s, 2), 0, 0)),
                bias_spec(lambda b, s, _: (2, 0, 0))]
    args = [sinks, qm, qh, km, km, kh, vm, vm, vh, g, bias, bias]
    out_specs = [two(ATTN_WIDTH), one(ATTN_WIDTH, head)]
    out_shape = [jax.ShapeDtypeStruct(qm.shape, BF16), jax.ShapeDtypeStruct(qh.shape, BF16)]
    if cast is not None:
        _, cr, cc = cast.shape
        crows = cr // (B * npair)
        in_specs.append(pl.BlockSpec((None, crows, cc),
                                     lambda b, s, _: (cast_layer, b * npair + jnp.maximum(s - 1, 0), 0)))
        args.append(cast)
        out_specs.append(pl.BlockSpec((crows, cc), pair))
        out_shape.append(jax.ShapeDtypeStruct((cr, cc), BF16))
    return pl.pallas_call(
        functools.partial(_attn_kernel, l=l),
        grid_spec=pltpu.PrefetchScalarGridSpec(
            num_scalar_prefetch=1, grid=(B, npair + 1), in_specs=in_specs, out_specs=out_specs),
        out_shape=out_shape,
        compiler_params=_params("arbitrary", "arbitrary"),
        name="attn_prompt",
    )(*args)


def _ssm_params_kernel(ar_ref, ai_ref, ldt_ref, kk_ref, br_ref, bi_ref, tr_ref, ti_ref, bbr_ref, bbi_ref):
    ar, ai = ar_ref[...], ai_ref[...]
    dt = jnp.exp(ldt_ref[...])
    kk = kk_ref[...]
    mag = jnp.exp(dt * ar * kk)
    ang = dt * ai * kk
    tr = mag * jnp.cos(ang)
    ti = mag * jnp.sin(ang)
    tr_ref[...] = tr
    ti_ref[...] = ti
    abr, abi = tr[0:1], ti[0:1]
    den = ar * ar + ai * ai
    fr = ((abr - 1.0) * ar + abi * ai) / den
    fi = (abi * ar - (abr - 1.0) * ai) / den
    br, bi = br_ref[...], bi_ref[...]
    bbr_ref[...] = fr * br - fi * bi
    bbi_ref[...] = fr * bi + fi * br


def _ssm_params(A_re, A_im, log_dt, B_re, B_im, C_re, C_im):
    depth = A_re.shape[0]
    n = depth * SSM_GROUPS * SSM_STATE
    row = lambda t: t.astype(F32).reshape(1, n)
    ldt = jnp.broadcast_to(log_dt.astype(F32)[:, :, None], (depth, SSM_GROUPS, SSM_STATE)).reshape(1, n)
    kk = jnp.array(SSM_POW_ROWS + (0,) * (SUBLANES - len(SSM_POW_ROWS)), F32).reshape(SUBLANES, 1)
    chan_first = lambda t: t.astype(F32).reshape(n, SSM_GROUP_SIZE).T
    shapes = [(SUBLANES, n), (SUBLANES, n), (SSM_GROUP_SIZE, n), (SSM_GROUP_SIZE, n)]
    tr, ti, bbr, bbi = pl.pallas_call(
        _ssm_params_kernel,
        out_shape=[jax.ShapeDtypeStruct(s, F32) for s in shapes],
        name="ssm_params",
    )(row(A_re), row(A_im), ldt, kk, chan_first(B_re), chan_first(B_im))
    J, G8 = SSM_LANE_BLOCKS, LANES // SSM_GROUP_SIZE

    def lanes(t):
        return t.reshape(SUBLANES, depth, J, SSM_BLOCK_STATES).transpose(1, 0, 2, 3)

    pw = jnp.concatenate([lanes(tr), lanes(ti)], axis=-1).reshape(depth, SUBLANES, SSM_STATE_LANES)
    eye = jnp.eye(G8, dtype=F32)

    def bdiag(t):
        t = t.reshape(SSM_GROUP_SIZE, depth, J, G8, SSM_STATE).transpose(1, 2, 3, 0, 4)
        t = t[:, :, :, :, None, :] * eye[None, None, :, None, :, None]
        return t.reshape(depth, J, LANES, SSM_BLOCK_STATES)

    bblk = jnp.concatenate([bdiag(bbr), bdiag(bbi)], axis=-1).astype(BF16)

    def cdiag(t):
        t = t.astype(F32).reshape(depth, J, G8, SSM_GROUP_SIZE, SSM_STATE).transpose(0, 1, 2, 4, 3)
        t = t[:, :, :, :, None, :] * eye[None, None, :, None, :, None]
        return t.reshape(depth, J, SSM_BLOCK_STATES, LANES)

    cblk = jnp.concatenate([cdiag(C_re), -cdiag(C_im)], axis=2).astype(BF16)
    return pw, bblk, cblk


def _ssm_tail(y, u, d_ref, wg_ref, bg_ref, g_ref):
    y = y + d_ref[...] * u
    z = jax.nn.gelu(y)
    gate = jax.nn.sigmoid(jnp.dot(z.astype(BF16), wg_ref[...], preferred_element_type=F32) + bg_ref[...])
    return _rms(z * gate, g_ref[...])


def _ssm_sweep(x_scr, n, a_tabs, init, store):
    S = SSM_BLOCK_STATES
    fins = []
    for j0 in range(0, SSM_LANE_BLOCKS, 2):
        js = (j0, j0 + 1)

        def body(k, carry, js=js):
            r0 = pl.multiple_of(k * SUBLANES, SUBLANES)
            out = []
            for idx, j in enumerate(js):
                hr, hi = carry[2 * idx], carry[2 * idx + 1]
                base = j * 2 * S
                ar, ai = a_tabs[j]
                nhr = ar * hr - ai * hi + x_scr[pl.ds(r0, SUBLANES), base:base + S]
                nhi = ar * hi + ai * hr + x_scr[pl.ds(r0, SUBLANES), base + S:base + 2 * S]
                if store:
                    x_scr[pl.ds(r0, SUBLANES), base:base + S] = nhr
                    x_scr[pl.ds(r0, SUBLANES), base + S:base + 2 * S] = nhi
                out += [nhr, nhi]
            return tuple(out)

        c0 = tuple(t for j in js for t in init[j])
        res = jax.lax.fori_loop(0, n // SUBLANES, body, c0, unroll=True)
        fins += [(res[0], res[1]), (res[2], res[3])]
    return fins


def _ssm_rows(u, pow_row, pw_ref, bblk_ref, cblk_ref, x_scr, s_scr, carry_scr):
    n = u.shape[0]
    S = SSM_BLOCK_STATES
    ub = u.astype(BF16)
    for j in range(SSM_LANE_BLOCKS):
        x_scr[0:n, j * 2 * S:(j + 1) * 2 * S] = jnp.dot(ub[:, j * LANES:(j + 1) * LANES], bblk_ref[j],
                                                       preferred_element_type=F32)
    bc = lambda t: jnp.broadcast_to(t, (SUBLANES, S))
    re = lambda ref, r0, r1, j: ref[r0:r1, j * 2 * S:j * 2 * S + S]
    im = lambda ref, r0, r1, j: ref[r0:r1, j * 2 * S + S:(j + 1) * 2 * S]
    a_tabs = [(bc(re(pw_ref, 0, 1, j)), bc(im(pw_ref, 0, 1, j))) for j in range(SSM_LANE_BLOCKS)]
    zero = jnp.zeros((SUBLANES, S), F32)
    fins = _ssm_sweep(x_scr, n, a_tabs, [(zero, zero)] * SSM_LANE_BLOCKS, store=False)
    for j in range(SSM_LANE_BLOCKS):
        base = j * 2 * S
        cr, ci = re(pw_ref, pow_row, pow_row + 1, j), im(pw_ref, pow_row, pow_row + 1, j)
        sr, si = re(carry_scr, 0, 1, j), im(carry_scr, 0, 1, j)
        fr, fi = fins[j]
        for c in range(SUBLANES):
            s_scr[c:c + 1, base:base + S] = sr
            s_scr[c:c + 1, base + S:base + 2 * S] = si
            sr, si = cr * sr - ci * si + fr[c:c + 1], cr * si + ci * sr + fi[c:c + 1]
        carry_scr[:, base:base + S] = bc(sr)
        carry_scr[:, base + S:base + 2 * S] = bc(si)
    init = [(re(s_scr, 0, SUBLANES, j), im(s_scr, 0, SUBLANES, j)) for j in range(SSM_LANE_BLOCKS)]
    _ssm_sweep(x_scr, n, a_tabs, init, store=True)
    ys = [jnp.dot(x_scr[0:n, j * 2 * S:(j + 1) * 2 * S].astype(BF16), cblk_ref[j], preferred_element_type=F32)
          for j in range(SSM_LANE_BLOCKS)]
    return jnp.concatenate(ys, axis=1)


def _ssm_tile(u_ref, n, pow_row, seq_start, refs, o_ref, scr):
    pw_ref, bblk_ref, cblk_ref, d_ref, wg_ref, bg_ref, g_ref = refs
    up_scr, x_scr, s_scr, carry_scr, o_scr = scr
    q = n // SUBLANES
    for j in range(SSM_LANE_BLOCKS):
        for k in range(q):
            up_scr[k * SUBLANES:(k + 1) * SUBLANES, j * LANES:(j + 1) * LANES] = \
                u_ref[j, pl.ds(k, SUBLANES, stride=q), :]
    u = up_scr[0:n, :]
    if seq_start:
        p = jax.lax.broadcasted_iota(jnp.int32, (n, 1), 0)
        u = jnp.where((p % SUBLANES) * q + p // SUBLANES >= seq_start, u, 0.0)
    y = _ssm_rows(u, pow_row, pw_ref, bblk_ref, cblk_ref, x_scr, s_scr, carry_scr)
    out = _ssm_tail(y, u, d_ref, wg_ref, bg_ref, g_ref)
    for j in range(SSM_LANE_BLOCKS):
        for k in range(q):
            o_scr[j, pl.ds(k, SUBLANES, stride=q), :] = out[k * SUBLANES:(k + 1) * SUBLANES,
                                                            j * LANES:(j + 1) * LANES]
    o_ref[...] = jnp.concatenate([o_scr[j, 0:n, :] for j in range(SSM_LANE_BLOCKS)], axis=1).astype(o_ref.dtype)


def _seq_kernel(um_ref, uh_ref, pw_ref, bblk_ref, cblk_ref, d_ref, wg_ref, bg_ref, gs_ref, cast_a_ref,
                xm_ref, halo_ref, xh_ref, wp_ref, sc_ref, gp_ref, cast_b_ref,
                som_ref, soh_ref, st_ref, cast_a_out_ref, pom_ref, poh_ref, cast_b_out_ref,
                up_scr, x_scr, s_scr, carry_scr, o_scr):
    t = pl.program_id(1)
    tm = xm_ref.shape[0]
    ssm_refs = (pw_ref, bblk_ref, cblk_ref, d_ref, wg_ref, bg_ref, gs_ref)
    scr = (up_scr, x_scr, s_scr, carry_scr, o_scr)

    @pl.when(t == 0)
    def _():
        carry_scr[...] = jnp.zeros_like(carry_scr)
        _ssm_tile(uh_ref, HEAD_ROWS, 1, SEQ0, ssm_refs, soh_ref, scr)
        hrow = jax.lax.broadcasted_iota(jnp.int32, (HEAD_ROWS, 1), 0)
        x = jnp.where(hrow >= SEQ0, xh_ref[...], 0.0)
        prev = jnp.zeros((POOL_HALO, POOL_WIDTH), F32)
        poh_ref[...] = _pool_rows(x, prev, -SEQ0, wp_ref, sc_ref, gp_ref).astype(poh_ref.dtype)

    @pl.when(t > 0)
    def _():
        cast_a_out_ref[...] = cast_a_ref[...].astype(BF16)
        cast_b_out_ref[...] = cast_b_ref[...].astype(BF16)
        _ssm_tile(um_ref, um_ref.shape[1], 2, 0, ssm_refs, som_ref, scr)
        st_ref[...] = carry_scr[...]
        prev = jnp.where(t == 1, xh_ref[HEAD_ROWS - POOL_HALO:, :], halo_ref[...])
        pom_ref[...] = _pool_rows(xm_ref[...], prev, N_META + (t - 1) * tm, wp_ref, sc_ref, gp_ref).astype(pom_ref.dtype)


def _seq_mixers(um, uh, xm, xh, vecs, pw, bblk, cblk, wg, wp, l, B, tm, cast_a, cast_b):
    assert (1, HEAD_ROWS // SUBLANES, tm // SUBLANES) == SSM_POW_ROWS
    J = SSM_LANE_BLOCKS
    rm, rh = um.shape[1], uh.shape[1]
    nt = rm // (B * tm)
    r = tm // POOL_HALO
    tile = lambda b, t: b * nt + jnp.maximum(t - 1, 0)
    main = lambda b, t: (tile(b, t), 0)
    head = lambda b, t: (b, 0)
    halo = lambda b, t: (jnp.maximum((b * nt + t - 1) * r - 1, 0), 0)

    def slab(w):
        _, cr, cc = w.shape
        rows = cr // (B * nt)
        return (pl.BlockSpec((None, rows, cc), lambda b, t: (l, tile(b, t), 0)), pl.BlockSpec((rows, cc), main),
                jax.ShapeDtypeStruct((cr, cc), BF16))

    (a_in, a_out, a_shape), (b_in, b_out, b_shape) = slab(cast_a), slab(cast_b)
    bf = lambda n, w: jax.ShapeDtypeStruct((n, w), BF16)
    return pl.pallas_call(
        _seq_kernel,
        grid=(B, nt + 1),
        in_specs=[pl.BlockSpec((J, tm, LANES), lambda b, t: (0, tile(b, t), 0)),
                  pl.BlockSpec((J, HEAD_ROWS, LANES), lambda b, t: (0, b, 0)),
                  _layer(pw.shape[1:], l), _layer(bblk.shape[1:], l), _layer(cblk.shape[1:], l),
                  _vrow("D_skip", l), _layer((SSM_WIDTH, SSM_WIDTH), l), _vrow("b_glu", l),
                  _vrow("g_out_ssm", l), a_in,
                  pl.BlockSpec((tm, POOL_WIDTH), main), pl.BlockSpec((POOL_HALO, POOL_WIDTH), halo),
                  pl.BlockSpec((HEAD_ROWS, POOL_WIDTH), head),
                  _layer(wp.shape[1:], l), _vrow("pool_scale", l), _vrow("g_out_pool", l), b_in],
        out_specs=[pl.BlockSpec((tm, SSM_WIDTH), main), pl.BlockSpec((HEAD_ROWS, SSM_WIDTH), head),
                   pl.BlockSpec((None, SUBLANES, SSM_STATE_LANES), lambda b, t: (b, 0, 0)), a_out,
                   pl.BlockSpec((tm, POOL_WIDTH), main), pl.BlockSpec((HEAD_ROWS, POOL_WIDTH), head), b_out],
        out_shape=[bf(rm, SSM_WIDTH), bf(rh, SSM_WIDTH),
                   jax.ShapeDtypeStruct((B, SUBLANES, SSM_STATE_LANES), F32), a_shape,
                   bf(rm, POOL_WIDTH), bf(rh, POOL_WIDTH), b_shape],
        scratch_shapes=[pltpu.VMEM((tm, SSM_WIDTH), F32),
                        pltpu.VMEM((tm, SSM_STATE_LANES), F32),
                        pltpu.VMEM((SUBLANES, SSM_STATE_LANES), F32),
                        pltpu.VMEM((SUBLANES, SSM_STATE_LANES), F32),
                        pltpu.VMEM((J, tm, LANES), F32)],
        compiler_params=_params("arbitrary", "arbitrary"),
        name="seq_mixers",
    )(um, uh, pw, bblk, cblk, vecs, wg, vecs, vecs, cast_a, xm, xm, xh, wp, vecs, vecs, cast_b)


def _state_from_lanes(s):
    s = s.reshape(s.shape[0], SSM_LANE_BLOCKS, 2, SSM_BLOCK_STATES)
    return (s[:, :, 0].reshape(-1, SSM_GROUPS, SSM_STATE), s[:, :, 1].reshape(-1, SSM_GROUPS, SSM_STATE))


def _pool_tail(d_groups, w_ref, sc_ref, g_ref):
    y = jnp.concatenate(
        [jnp.dot(d.astype(BF16), w_ref[gi], preferred_element_type=F32) for gi, d in enumerate(d_groups)], axis=1)
    return _rms(y * sc_ref[...], g_ref[...])


def _pool_rows(x, prev, pos0, w_ref, sc_ref, g_ref):
    n = x.shape[0]
    xe = jnp.concatenate([prev, x], axis=0)
    pos = pos0 + jax.lax.broadcasted_iota(jnp.int32, (n, 1), 0)
    ds = []
    for gi, w in enumerate(POOL_WINDOWS):
        gsl = slice(gi * POOL_GROUP, (gi + 1) * POOL_GROUP)
        s = xe[:, gsl]
        k = 1
        while k < w:
            s = s + pltpu.roll(s, k, 0)
            k *= 2
        cnt = jnp.clip(pos + 1, 1, w).astype(F32)
        ds.append(s[POOL_HALO:] / cnt - x[:, gsl])
    return _pool_tail(ds, w_ref, sc_ref, g_ref)


def _shift_caches_kernel(k_ref, v_ref, nk_ref, nv_ref):
    for src, dst in ((k_ref, nk_ref), (v_ref, nv_ref)):
        for bb in range(DEC_STEP):
            dst[bb, 0:WINDOW - 1] = src[bb, 1:WINDOW]
            dst[bb, WINDOW - 1] = jnp.zeros((N_KV_HEADS, HEAD_DIM), F32)


def _shift_caches(cache_k, cache_v):
    depth, N = cache_k.shape[:2]
    blk = pl.BlockSpec((None, DEC_STEP, WINDOW, N_KV_HEADS, HEAD_DIM), lambda l, s: (l, s, 0, 0, 0))
    shape = jax.ShapeDtypeStruct(cache_k.shape, F32)
    return pl.pallas_call(
        _shift_caches_kernel,
        grid=(depth, N // DEC_STEP),
        in_specs=[blk, blk],
        out_specs=[blk, blk],
        out_shape=[shape, shape],
        compiler_params=_params("arbitrary", "arbitrary"),
        name="shift_caches",
    )(cache_k, cache_v)


def _attn_sample_kernel(q_ref, kn_ref, vn_ref, kc_ref, vc_ref, sink_ref, g_ref, a_in_ref, nk_in_ref, nv_in_ref,
                        a_ref, nk_ref, nv_ref, acc_scr):
    del a_in_ref, nk_in_ref, nv_in_ref
    step = pl.program_id(0)
    scale = HEAD_DIM ** -0.5
    for bb in range(DEC_STEP):
        outs = []
        for kh in range(N_KV_HEADS):
            ksl = slice(kh * HEAD_DIM, (kh + 1) * HEAD_DIM)
            qh = jnp.concatenate(
                [q_ref[bb:bb + 1, (kh * GQA_GROUP + g) * HEAD_DIM:(kh * GQA_GROUP + g + 1) * HEAD_DIM]
                 for g in range(GQA_GROUP)], axis=0)
            kn = kn_ref[bb:bb + 1, ksl]
            vn = vn_ref[bb:bb + 1, ksl]
            nk_ref[bb, 0, kh:kh + 1, :] = kn
            nv_ref[bb, 0, kh:kh + 1, :] = vn
            kc = kc_ref[bb, :, kh, :]
            vc = vc_ref[bb, :, kh, :]
            sc = jax.lax.dot_general(qh.astype(BF16), kc.astype(BF16), (((1,), (1,)), ((), ())),
                                     preferred_element_type=F32) * scale
            sn = jnp.sum(qh.astype(BF16).astype(F32) * kn.astype(BF16).astype(F32), axis=-1, keepdims=True) * scale
            sk = sink_ref[kh]
            m = jnp.maximum(jnp.maximum(jnp.max(sc, axis=-1, keepdims=True), sn), sk)
            pc = jnp.exp(sc - m)
            pn = jnp.exp(sn - m)
            denom = jnp.sum(pc, axis=-1, keepdims=True) + pn + jnp.exp(sk - m)
            o = jnp.dot(pc.astype(BF16), vc.astype(BF16), preferred_element_type=F32)
            o = (o + pn.astype(BF16).astype(F32) * vn.astype(BF16).astype(F32)) / denom
            outs.extend(o[g:g + 1] for g in range(GQA_GROUP))
        a = jnp.concatenate(outs, axis=1)
        acc_scr[pl.ds(step * DEC_STEP + bb, 1), :] = _rms(a, g_ref[...])

    @pl.when(step == pl.num_programs(0) - 1)
    def _():
        a_ref[...] = acc_scr[...].astype(a_ref.dtype)


def _attn_sample(qh, kh, vh, cache_k, cache_v, l, sinks, g, ah, nk, nv, N):
    rows = lambda w: pl.BlockSpec((DEC_STEP, w), lambda s: (s, 0))
    cache = pl.BlockSpec((None, DEC_STEP, WINDOW, N_KV_HEADS, HEAD_DIM), lambda s: (l, s, 0, 0, 0))
    last = pl.BlockSpec((None, DEC_STEP, 1, N_KV_HEADS, HEAD_DIM), lambda s: (l, s, WINDOW - 1, 0, 0))
    anyspec = pl.BlockSpec(memory_space=pl.ANY)
    return pl.pallas_call(
        _attn_sample_kernel,
        grid=(N // DEC_STEP,),
        in_specs=[rows(ATTN_WIDTH), rows(KV_WIDTH), rows(KV_WIDTH), cache, cache,
                  _layer((N_KV_HEADS, GQA_GROUP, 1), l), _vrow("g_out_attn", l),
                  anyspec, anyspec, anyspec],
        out_specs=[pl.BlockSpec((N, ATTN_WIDTH), lambda s: (0, 0)), last, last],
        out_shape=[jax.ShapeDtypeStruct(ah.shape, ah.dtype), jax.ShapeDtypeStruct(nk.shape, nk.dtype),
                   jax.ShapeDtypeStruct(nv.shape, nv.dtype)],
        scratch_shapes=[pltpu.VMEM((N, ATTN_WIDTH), F32)],
        input_output_aliases={7: 0, 8: 1, 9: 2},
        compiler_params=_params("arbitrary"),
        name="attn_sample",
    )(qh, kh, vh, cache_k, cache_v, sinks, g, ah, nk, nv)


def _mix_sample_kernel(u_ref, h0_ref, pw_ref, bblk_ref, cblk_ref, d_ref, wg_ref, bg_ref, gs_ref,
                       xp_ref, pb_ref, wp_ref, sc_ref, gp_ref, s_in_ref, p_in_ref, s_ref, st_ref, p_ref):
    del s_in_ref, p_in_ref
    S = SSM_BLOCK_STATES
    u = jnp.concatenate([u_ref[j] for j in range(SSM_LANE_BLOCKS)], axis=1)
    ub = u.astype(BF16)
    ys = []
    for j in range(SSM_LANE_BLOCKS):
        x = jnp.dot(ub[:, j * LANES:(j + 1) * LANES], bblk_ref[j], preferred_element_type=F32)
        base = j * 2 * S
        ar = pw_ref[0:1, base:base + S]
        ai = pw_ref[0:1, base + S:base + 2 * S]
        h0r = h0_ref[:, base:base + S]
        h0i = h0_ref[:, base + S:base + 2 * S]
        hr = x[:, 0:S] + ar * h0r - ai * h0i
        hi = x[:, S:] + ar * h0i + ai * h0r
        st_ref[:, base:base + S] = hr
        st_ref[:, base + S:base + 2 * S] = hi
        h = jnp.concatenate([hr, hi], axis=1).astype(BF16)
        ys.append(jnp.dot(h, cblk_ref[j], preferred_element_type=F32))
    s_ref[...] = _ssm_tail(jnp.concatenate(ys, axis=1), u, d_ref, wg_ref, bg_ref, gs_ref).astype(s_ref.dtype)

    xp = xp_ref[...]
    ds = []
    for gi, w in enumerate(POOL_WINDOWS):
        gsl = slice(gi * POOL_GROUP, (gi + 1) * POOL_GROUP)
        s = xp[:, gsl]
        for back in range(1, w):
            s = s + pb_ref[POOL_BUF - back][:, gsl]
        ds.append(s / float(w) - xp[:, gsl])
    p_ref[...] = _pool_tail(ds, wp_ref, sc_ref, gp_ref).astype(p_ref.dtype)


def _mix_sample(uh, h0, pw, bblk, cblk, d, wg, l, bg, gs, xph, pbuf, wp, _l, sc, gp, sh, ph):
    N = h0.shape[1]
    rows = lambda w: pl.BlockSpec((N, w), lambda i: (0, 0))
    anyspec = pl.BlockSpec(memory_space=pl.ANY)
    return pl.pallas_call(
        _mix_sample_kernel,
        grid=(1,),
        in_specs=[pl.BlockSpec((SSM_LANE_BLOCKS, N, LANES), lambda i: (0, 0, 0)),
                  _layer(h0.shape[1:], l), _layer(pw.shape[1:], l), _layer(bblk.shape[1:], l),
                  _layer(cblk.shape[1:], l),
                  _vrow("D_skip", l), _layer((SSM_WIDTH, SSM_WIDTH), l), _vrow("b_glu", l),
                  _vrow("g_out_ssm", l), rows(POOL_WIDTH), _layer(pbuf.shape[1:], l),
                  _layer(wp.shape[1:], l),
                  _vrow("pool_scale", l), _vrow("g_out_pool", l), anyspec, anyspec],
        out_specs=[rows(SSM_WIDTH), _full((N, SSM_STATE_LANES)), rows(POOL_WIDTH)],
        out_shape=[jax.ShapeDtypeStruct(sh.shape, sh.dtype), jax.ShapeDtypeStruct((N, SSM_STATE_LANES), F32),
                   jax.ShapeDtypeStruct(ph.shape, ph.dtype)],
        input_output_aliases={14: 0, 15: 2},
        compiler_params=_params("arbitrary"),
        name="mix_sample",
    )(uh, h0, pw, bblk, cblk, d, wg, bg, gs, xph, pbuf, wp, sc, gp, sh, ph)


def _outproj_rows(x_ref, a_ref, s_ref, p_ref, w_ref, o_ref):
    o1 = ATTN_WIDTH
    o2 = o1 + SSM_WIDTH
    acc = x_ref[...]
    acc = acc + jnp.dot(a_ref[...], w_ref[0:o1, :], preferred_element_type=F32)
    acc = acc + jnp.dot(s_ref[...], w_ref[o1:o2, :], preferred_element_type=F32)
    acc = acc + jnp.dot(p_ref[...], w_ref[o2:, :], preferred_element_type=F32)
    o_ref[...] = acc


def _outproj_kernel(xm_ref, am_ref, sm_ref, pm_ref, xh_ref, ah_ref, sh_ref, ph_ref, w_ref, om_ref, oh_ref):
    i = pl.program_id(0)
    last = pl.num_programs(0) - 1

    @pl.when(i < last)
    def _():
        _outproj_rows(xm_ref, am_ref, sm_ref, pm_ref, w_ref, om_ref)

    @pl.when(i == last)
    def _():
        _outproj_rows(xh_ref, ah_ref, sh_ref, ph_ref, w_ref, oh_ref)


def _outproj(xm, am, sm, pm, xh, ah, sh, ph, w, tm):
    R, RH = xm.shape[0], xh.shape[0]
    nm = R // tm
    row = lambda i: (jnp.minimum(i, nm - 1), 0)
    once = pl.Buffered(1)
    head = lambda w_: pl.BlockSpec((RH, w_), lambda i: (0, 0), pipeline_mode=once)
    return pl.pallas_call(
        _outproj_kernel,
        grid=(nm + 1,),
        in_specs=[pl.BlockSpec((tm, D_MODEL), row), pl.BlockSpec((tm, ATTN_WIDTH), row),
                  pl.BlockSpec((tm, SSM_WIDTH), row), pl.BlockSpec((tm, POOL_WIDTH), row),
                  head(D_MODEL), head(ATTN_WIDTH), head(SSM_WIDTH), head(POOL_WIDTH),
                  pl.BlockSpec((D_MODEL, D_MODEL), lambda i: (0, 0), pipeline_mode=once)],
        out_specs=[pl.BlockSpec((tm, D_MODEL), row), pl.BlockSpec((RH, D_MODEL), lambda i: (0, 0))],
        out_shape=[jax.ShapeDtypeStruct((R, D_MODEL), F32), jax.ShapeDtypeStruct((RH, D_MODEL), F32)],
        compiler_params=_params("arbitrary"),
        name="outproj",
    )(xm, am, sm, pm, xh, ah, sh, ph, w)


def _ffn_kernel(xm_ref, xh_ref, g_ref, w1_ref, w2_ref, om_ref, oh_ref, hm_scr, hh_scr):
    i = pl.program_id(0)
    f = pl.program_id(1)

    @pl.when(f == 0)
    def _():
        x = xm_ref[...]
        hm_scr[...] = _rms(x, g_ref[...]).astype(BF16)
        om_ref[...] = x

    @pl.when((f == 0) & (i == 0))
    def _():
        x = xh_ref[...]
        hh_scr[...] = _rms(x, g_ref[...]).astype(BF16)
        oh_ref[...] = x

    w1 = w1_ref[...]
    w2 = w2_ref[...]

    def mlp(h):
        h1 = jnp.dot(h, w1, preferred_element_type=F32)
        return jnp.dot(jnp.square(jnp.maximum(h1, 0.0)).astype(BF16), w2, preferred_element_type=F32)

    om_ref[...] += mlp(hm_scr[...])

    @pl.when(i == 0)
    def _():
        oh_ref[...] += mlp(hh_scr[...])


def _ffn(xm, xh, g, l, w1, w2, tm, tf):
    R, RH = xm.shape[0], xh.shape[0]
    return pl.pallas_call(
        _ffn_kernel,
        grid=(R // tm, D_FF // tf),
        in_specs=[pl.BlockSpec((tm, D_MODEL), lambda i, f: (i, 0)),
                  pl.BlockSpec((RH, D_MODEL), lambda i, f: (0, 0), pipeline_mode=pl.Buffered(1)),
                  _vrow("g_ffn", l),
                  pl.BlockSpec((D_MODEL, tf), lambda i, f: (0, f)),
                  pl.BlockSpec((tf, D_MODEL), lambda i, f: (f, 0))],
        out_specs=[pl.BlockSpec((tm, D_MODEL), lambda i, f: (i, 0)),
                   pl.BlockSpec((RH, D_MODEL), lambda i, f: (0, 0))],
        out_shape=[jax.ShapeDtypeStruct((R, D_MODEL), F32), jax.ShapeDtypeStruct((RH, D_MODEL), F32)],
        scratch_shapes=[pltpu.VMEM((tm, D_MODEL), BF16), pltpu.VMEM((RH, D_MODEL), BF16)],
        compiler_params=_params("arbitrary", "arbitrary"),
        name="ffn",
    )(xm, xh, g, w1, w2)


def _rope_tables(pos):
    half = ROT_DIM // 2
    inv = ROPE_THETA ** (-np.arange(0, ROT_DIM, 2, dtype=np.float64) / ROT_DIM)
    ang = np.asarray(pos, np.float64)[:, None] * inv
    cos, sin = np.cos(ang), np.sin(ang)
    n = ang.shape[0]
    z = np.zeros((n, HEAD_DIM - ROT_DIM))
    zh = np.zeros((n, half))
    rc = np.concatenate([cos, cos, z + 1.0], axis=1)
    rs1 = np.concatenate([zh, sin, z], axis=1)
    rs2 = np.concatenate([-sin, zh, z], axis=1)
    return tuple(t.astype(np.float32) for t in (rc, rs1, rs2))


def kernel(x_prompt, x_sample, cache_k, cache_v, state_ssm_re, state_ssm_im, state_pool, meta_tokens, g_mix, w_in, g_q, g_k, sinks, A_re, A_im, log_dt, B_re, B_im, C_re, C_im, D_skip, w_glu, b_glu, w_pool, pool_scale, g_out_attn, g_out_ssm, g_out_pool, w_out, g_ffn, w_ff1, w_ff2):
    B, T, _ = x_prompt.shape
    N = x_sample.shape[0]
    depth = w_in.shape[0]
    assert N <= SEQ0 and N % DEC_STEP == 0 and T % TM_FFN == 0
    meta = meta_tokens.astype(F32)
    head0 = jnp.concatenate([x_sample.reshape(N, D_MODEL), jnp.zeros((SEQ0 - N, D_MODEL), F32), meta], axis=0)
    head_rest = jnp.concatenate([jnp.zeros((SEQ0, D_MODEL), F32), meta], axis=0)
    xh = jnp.concatenate([head0] + [head_rest] * (B - 1), axis=0)
    xm = x_prompt.reshape(B * T, D_MODEL)

    rope_m = _rope_tables(N_META + np.arange(T))
    hr_ = np.arange(HEAD_ROWS)
    pos_h0 = np.where(hr_ < N, PAST_LEN, np.maximum(hr_ - SEQ0, 0))
    pos_h = np.concatenate([pos_h0] + [np.maximum(hr_ - SEQ0, 0)] * (B - 1))
    rope_h = _rope_tables(pos_h)

    wi = w_in[0].astype(BF16)
    wg_all, wp_all = w_glu.astype(BF16), w_pool.astype(BF16)
    ck = cache_k.astype(F32)
    cv = cache_v.astype(F32)
    RH = B * HEAD_ROWS

    named = dict(g_mix=g_mix, g_ffn=g_ffn, g_out_attn=g_out_attn, D_skip=D_skip, b_glu=b_glu,
                 g_out_ssm=g_out_ssm, pool_scale=pool_scale, g_out_pool=g_out_pool, g_q=g_q, g_k=g_k)
    vecs = jnp.concatenate([named[name].astype(F32) for name, _ in VEC_LAYOUT], axis=1)[:, None, :]
    pw, bblk, cblk = _ssm_params(A_re, A_im, log_dt, B_re, B_im, C_re, C_im)
    ssm_w = lambda l: (pw, bblk, cblk, vecs, wg_all, l, vecs, vecs)
    pool_w = lambda l: (wp_all, l, vecs, vecs)
    g_attn = vecs
    g_ffn_ = vecs
    sinks_flat = sinks.astype(F32).reshape(depth * N_HEADS)
    sinks_col = sinks.astype(F32).reshape(depth, N_KV_HEADS, GQA_GROUP, 1)
    bias = _attn_bias()
    h0 = jnp.concatenate([state_ssm_re.astype(F32).reshape(depth, N, SSM_LANE_BLOCKS, SSM_BLOCK_STATES),
                          state_ssm_im.astype(F32).reshape(depth, N, SSM_LANE_BLOCKS, SSM_BLOCK_STATES)],
                         axis=-1).reshape(depth, N, SSM_STATE_LANES)
    pbuf = state_pool.astype(F32).transpose(0, 2, 1, 3)

    nk, nv = _shift_caches(ck, cv)
    ks, vs, pls, sts, st_ss, phs = ([] for _ in range(6))
    for l in range(depth):
        (qm, km, vm, um, pm), (qh, kh, vh, uh, ph), w1 = _inproj(xm, xh, vecs, wi, l, rope_m, rope_h, TM_PROJ, w_ff1)

        if l + 1 < depth:
            am, ah, wi = _attn(sinks_flat, l, qm, qh, km, kh, vm, vh, g_attn, bias, B, cast=w_in, cast_layer=l + 1)
        else:
            am, ah = _attn(sinks_flat, l, qm, qh, km, kh, vm, vh, g_attn, bias, B)
        sm, sh, st, w2, plm, plh, wo = _seq_mixers(um, uh, pm, ph, vecs, pw, bblk, cblk, wg_all, wp_all, l, B, TM_SEQ,
                                                   w_ff2, w_out)

        ah, nk, nv = _attn_sample(qh, kh, vh, ck, cv, l, sinks_col, g_attn, ah, nk, nv, N)
        sh, st_s, plh = _mix_sample(uh, h0, *ssm_w(l), ph, pbuf, *pool_w(l), sh, plh)

        xm, xh = _outproj(xm, am, sm, plm, xh, ah, sh, plh, wo, TM_PROJ)
        xm, xh = _ffn(xm, xh, g_ffn_, l, w1, w2, TM_FFN, TF_FFN)

        ks.append(km.reshape(B, T, KV_WIDTH)[:, T - WINDOW:])
        vs.append(vm.reshape(B, T, KV_WIDTH)[:, T - WINDOW:])
        pls.append(pm.reshape(B, T, POOL_WIDTH)[:, T - POOL_BUF:])
        sts.append(st[:, 0])
        st_ss.append(st_s)
        phs.append(ph[:N])

    y_prompt = xm.reshape(B, T, D_MODEL)
    y_sample = xh[:N].reshape(N, 1, D_MODEL)
    heads = lambda t: jnp.stack(t).reshape(depth, -1, WINDOW, N_KV_HEADS, HEAD_DIM)
    p_re, p_im = _state_from_lanes(jnp.stack(sts).reshape(depth * B, SSM_STATE_LANES))
    s_re, s_im = _state_from_lanes(jnp.stack(st_ss).reshape(depth * N, SSM_STATE_LANES))
    st4 = lambda t, n: t.reshape(depth, n, SSM_GROUPS, SSM_STATE)
    s_pool = jnp.concatenate([state_pool.astype(F32)[:, :, 1:], jnp.stack(phs)[:, :, None]], axis=2)
    return (y_prompt, y_sample, heads(ks), heads(vs), st4(p_re, B), st4(p_im, B), jnp.stack(pls),
            nk, nv, st4(s_re, N), st4(s_im, N), s_pool)
```

```python
import functools
import math

import jax
import jax.numpy as jnp
import numpy as np
from jax.experimental import pallas as pl
from jax.experimental.pallas import tpu as pltpu

D_MODEL = 2048
N_META = 16
HEAD_DIM = 128
N_HEADS = 8
N_KV_HEADS = 2
GQA_GROUP = 4
ATTN_WIDTH = 1024
KV_WIDTH = 256
WINDOW = 128
BLOCK = 128
ROT_DIM = 32
ROT_HALF = ROT_DIM // 2
ROPE_THETA = 500000.0
SSM_WIDTH = 512
SSM_GROUP_SIZE = 16
SSM_GROUPS = 32
SSM_STATE = 64
POOL_WIDTH = 512
POOL_WINDOWS = (2, 4, 8, 16)
POOL_GROUP = 128
POOL_BUF = 15
POOL_HALO = 16
IN_WIDTH = 2560
D_FF = 8192
EPS = 1e-6
PAST_LEN = 16384
LOG2E = math.log2(math.e)

HEAD_ROWS = BLOCK
SEQ0 = HEAD_ROWS - N_META
LANES = 128
SUBLANES = 8
SSM_LANE_BLOCKS = SSM_WIDTH // LANES
SSM_BLOCK_STATES = (LANES // SSM_GROUP_SIZE) * SSM_STATE
SSM_STATE_LANES = SSM_LANE_BLOCKS * 2 * SSM_BLOCK_STATES
VMEM_LIMIT = 60 * 1024 * 1024

TM_PROJ = 512
TM_FFN = 1024
TF_FFN = 512
TM_SEQ = 512
DEC_STEP = 8
SSM_POW_ROWS = (1, HEAD_ROWS // SUBLANES, TM_SEQ // SUBLANES)
POW_ROW_HEAD, POW_ROW_MAIN = 1, 2

BF16 = jnp.bfloat16
F32 = jnp.float32


def _params(*semantics):
    return pltpu.CompilerParams(dimension_semantics=semantics, vmem_limit_bytes=VMEM_LIMIT)


def _rms(x, g):
    return x * jax.lax.rsqrt(jnp.mean(x * x, axis=-1, keepdims=True) + EPS) * g


def _full(shape):
    n = len(shape)
    return pl.BlockSpec(shape, lambda *_: (0,) * n)


def _layer(shape, l):
    n = len(shape)
    return pl.BlockSpec((None, *shape), lambda *_: (l,) + (0,) * n)


VEC_LAYOUT = (("g_mix", D_MODEL), ("g_ffn", D_MODEL), ("g_out_attn", ATTN_WIDTH), ("D_skip", SSM_WIDTH),
              ("b_glu", SSM_WIDTH), ("g_out_ssm", SSM_WIDTH), ("pool_scale", POOL_WIDTH),
              ("g_out_pool", POOL_WIDTH), ("g_q", HEAD_DIM), ("g_k", HEAD_DIM))
VEC_WIDTH = dict(VEC_LAYOUT)
VEC_OFFSET = {name: sum(w for _, w in VEC_LAYOUT[:i]) for i, (name, _) in enumerate(VEC_LAYOUT)}
assert all(VEC_OFFSET[name] % w == 0 for name, w in VEC_LAYOUT)


def _vrow(name, l):
    w = VEC_WIDTH[name]
    return pl.BlockSpec((None, 1, w), lambda *_: (l, 0, VEC_OFFSET[name] // w))


def _inproj_rows(x, g_ref, w_ref, gq_ref, gk_ref, rope_refs, out_refs):
    q_ref, k_ref, v_ref, u_ref, xp_ref = out_refs
    h = _rms(x, g_ref[...]).astype(BF16)
    proj = jnp.dot(h, w_ref[...], preferred_element_type=F32)
    rc, rs1, rs2 = (r[...] for r in rope_refs)

    def head(t, g):
        t = _rms(t, g)
        return t * rc + pltpu.roll(t, ROT_HALF, 1) * rs1 + pltpu.roll(t, LANES - ROT_HALF, 1) * rs2

    for hd in range(N_HEADS):
        sl = slice(hd * HEAD_DIM, (hd + 1) * HEAD_DIM)
        q_ref[:, sl] = head(proj[:, sl], gq_ref[...])
    for hd in range(N_KV_HEADS):
        sl = slice(hd * HEAD_DIM, (hd + 1) * HEAD_DIM)
        k_ref[:, sl] = head(proj[:, ATTN_WIDTH + hd * HEAD_DIM:ATTN_WIDTH + (hd + 1) * HEAD_DIM], gk_ref[...])
    o2 = ATTN_WIDTH + KV_WIDTH
    o3 = o2 + KV_WIDTH
    o4 = o3 + SSM_WIDTH
    v_ref[...] = proj[:, o2:o3]
    for j in range(SSM_LANE_BLOCKS):
        u_ref[j] = proj[:, o3 + j * LANES:o3 + (j + 1) * LANES]
    xp_ref[...] = proj[:, o4:]


def _inproj_kernel(xm_ref, xh_ref, g_ref, w_ref, gq_ref, gk_ref, rcm_ref, rs1m_ref, rs2m_ref,
                   rch_ref, rs1h_ref, rs2h_ref, cast_in_ref, *outs):
    main_outs, head_outs, cast_out_ref = outs[0:5], outs[5:10], outs[10]
    i = pl.program_id(0)
    last = pl.num_programs(0) - 1

    @pl.when(i < last)
    def _():
        cast_out_ref[...] = cast_in_ref[...].astype(BF16)
        _inproj_rows(xm_ref[...], g_ref, w_ref, gq_ref, gk_ref, (rcm_ref, rs1m_ref, rs2m_ref), main_outs)

    @pl.when(i == last)
    def _():
        _inproj_rows(xh_ref[...], g_ref, w_ref, gq_ref, gk_ref, (rch_ref, rs1h_ref, rs2h_ref), head_outs)


def _inproj(xm, xh, vecs, w, l, rope_m, rope_h, tm, cast):
    R, RH = xm.shape[0], xh.shape[0]
    nm = R // tm
    tiles_per_rope = rope_m[0].shape[0] // tm
    tile = lambda i: jnp.minimum(i, nm - 1)
    row = lambda i: (tile(i), 0)
    rrow = lambda i: (tile(i) % tiles_per_rope, 0)
    once = pl.Buffered(1)
    full = lambda n, w_: pl.BlockSpec((n, w_), lambda i: (0, 0))

    def out_set(n, rows, im, uim):
        flat = lambda w_: (pl.BlockSpec((n, w_), im), jax.ShapeDtypeStruct((rows, w_), F32))
        u_out = (pl.BlockSpec((SSM_LANE_BLOCKS, n, LANES), uim),
                 jax.ShapeDtypeStruct((SSM_LANE_BLOCKS, rows, LANES), F32))
        return [flat(ATTN_WIDTH), flat(KV_WIDTH), flat(KV_WIDTH), u_out, flat(POOL_WIDTH)]

    outs = (out_set(tm, R, row, lambda i: (0, tile(i), 0))
            + out_set(RH, RH, lambda i: (0, 0), lambda i: (0, 0, 0)))
    _, cr, cc = cast.shape
    crows = cr // nm
    outs.append((pl.BlockSpec((crows, cc), row), jax.ShapeDtypeStruct((cr, cc), BF16)))
    in_specs = [pl.BlockSpec((tm, D_MODEL), row),
                pl.BlockSpec((RH, D_MODEL), lambda i: (0, 0), pipeline_mode=once),
                _vrow("g_mix", l),
                pl.BlockSpec((D_MODEL, IN_WIDTH), lambda i: (0, 0), pipeline_mode=once),
                _vrow("g_q", l), _vrow("g_k", l),
                pl.BlockSpec((tm, LANES), rrow), pl.BlockSpec((tm, LANES), rrow), pl.BlockSpec((tm, LANES), rrow),
                full(RH, LANES), full(RH, LANES), full(RH, LANES),
                pl.BlockSpec((None, crows, cc), lambda i: (l, tile(i), 0))]
    res = pl.pallas_call(
        _inproj_kernel,
        grid=(nm + 1,),
        in_specs=in_specs,
        out_specs=[o[0] for o in outs],
        out_shape=[o[1] for o in outs],
        compiler_params=_params("arbitrary"),
        name="inproj",
    )(xm, xh, vecs, w, vecs, vecs, *rope_m, *rope_h, cast)
    return res[0:5], res[5:10], res[10]


def _attn_bias():
    rows = GQA_GROUP * BLOCK
    i, r, c = np.meshgrid(np.arange(3), np.arange(rows) % BLOCK, np.arange(2 * BLOCK), indexing="ij")
    diff = BLOCK + r - c
    krow = (i - 1) * BLOCK + c
    mask = (diff >= 0) & (diff <= WINDOW) & (krow >= SEQ0)
    return np.where(mask, 0.0, -np.inf).astype(np.float32)


def _attn_block(q_blk, kp_blk, kc_blk, vp_blk, vc_blk, bias, sink_ref, l, g):
    rows = GQA_GROUP * BLOCK
    rgrp = jax.lax.broadcasted_iota(jnp.int32, (rows, 1), 0) // BLOCK
    outs = []
    for kh in range(N_KV_HEADS):
        ksl = slice(kh * HEAD_DIM, (kh + 1) * HEAD_DIM)
        qh = jnp.concatenate(
            [q_blk[:, (kh * GQA_GROUP + h) * HEAD_DIM:(kh * GQA_GROUP + h + 1) * HEAD_DIM]
             for h in range(GQA_GROUP)], axis=0).astype(BF16)
        kk = jnp.concatenate([kp_blk[:, ksl], kc_blk[:, ksl]], axis=0).astype(BF16)
        vv = jnp.concatenate([vp_blk[:, ksl], vc_blk[:, ksl]], axis=0).astype(BF16)
        s = jax.lax.dot_general(qh, kk, (((1,), (1,)), ((), ())),
                                preferred_element_type=F32) * (HEAD_DIM ** -0.5 * LOG2E) + bias
        sk = jnp.zeros((rows, 1), F32)
        for h in range(GQA_GROUP):
            sk = jnp.where(rgrp == h, sink_ref[l * N_HEADS + kh * GQA_GROUP + h] * LOG2E, sk)
        m = jnp.maximum(jnp.max(s, axis=-1, keepdims=True), sk)
        p = jnp.exp2(s - m)
        denom = jnp.sum(p, axis=-1, keepdims=True) + jnp.exp2(sk - m)
        o = jnp.dot(p.astype(BF16), vv, preferred_element_type=F32) / denom
        outs.extend(o[h * BLOCK:(h + 1) * BLOCK] for h in range(GQA_GROUP))
    return _rms(jnp.concatenate(outs, axis=1), g)


def _ssm_params_kernel(ar_ref, ai_ref, ldt_ref, kk_ref, br_ref, bi_ref, tr_ref, ti_ref, bbr_ref, bbi_ref):
    ar, ai = ar_ref[...], ai_ref[...]
    dt = jnp.exp(ldt_ref[...])
    kk = kk_ref[...]
    mag = jnp.exp(dt * ar * kk)
    ang = dt * ai * kk
    tr = mag * jnp.cos(ang)
    ti = mag * jnp.sin(ang)
    tr_ref[...] = tr
    ti_ref[...] = ti
    abr, abi = tr[0:1], ti[0:1]
    den = ar * ar + ai * ai
    fr = ((abr - 1.0) * ar + abi * ai) / den
    fi = (abi * ar - (abr - 1.0) * ai) / den
    br, bi = br_ref[...], bi_ref[...]
    bbr_ref[...] = fr * br - fi * bi
    bbi_ref[...] = fr * bi + fi * br


def _ssm_params(A_re, A_im, log_dt, B_re, B_im, C_re, C_im):
    depth = A_re.shape[0]
    n = depth * SSM_GROUPS * SSM_STATE
    row = lambda t: t.astype(F32).reshape(1, n)
    ldt = jnp.broadcast_to(log_dt.astype(F32)[:, :, None], (depth, SSM_GROUPS, SSM_STATE)).reshape(1, n)
    kk = jnp.array(SSM_POW_ROWS + (0,) * (SUBLANES - len(SSM_POW_ROWS)), F32).reshape(SUBLANES, 1)
    chan_first = lambda t: t.astype(F32).reshape(n, SSM_GROUP_SIZE).T
    shapes = [(SUBLANES, n), (SUBLANES, n), (SSM_GROUP_SIZE, n), (SSM_GROUP_SIZE, n)]
    tr, ti, bbr, bbi = pl.pallas_call(
        _ssm_params_kernel,
        out_shape=[jax.ShapeDtypeStruct(s, F32) for s in shapes],
        name="ssm_params",
    )(row(A_re), row(A_im), ldt, kk, chan_first(B_re), chan_first(B_im))
    J, G8 = SSM_LANE_BLOCKS, LANES // SSM_GROUP_SIZE

    def lanes(t):
        return t.reshape(SUBLANES, depth, J, SSM_BLOCK_STATES).transpose(1, 0, 2, 3)

    pw = jnp.concatenate([lanes(tr), lanes(ti)], axis=-1).reshape(depth, SUBLANES, SSM_STATE_LANES)
    eye = jnp.eye(G8, dtype=F32)

    def bdiag(t):
        t = t.reshape(SSM_GROUP_SIZE, depth, J, G8, SSM_STATE).transpose(1, 2, 3, 0, 4)
        t = t[:, :, :, :, None, :] * eye[None, None, :, None, :, None]
        return t.reshape(depth, J, LANES, SSM_BLOCK_STATES)

    bblk = jnp.concatenate([bdiag(bbr), bdiag(bbi)], axis=-1).astype(BF16)

    def cdiag(t):
        t = t.astype(F32).reshape(depth, J, G8, SSM_GROUP_SIZE, SSM_STATE).transpose(0, 1, 2, 4, 3)
        t = t[:, :, :, :, None, :] * eye[None, None, :, None, :, None]
        return t.reshape(depth, J, SSM_BLOCK_STATES, LANES)

    cblk = jnp.concatenate([cdiag(C_re), -cdiag(C_im)], axis=2).astype(BF16)
    return pw, bblk, cblk


def _ssm_tail(y, u, d_ref, wg_ref, bg_ref, g_ref):
    y = y + d_ref[...] * u
    z = jax.nn.gelu(y)
    gate = jax.nn.sigmoid(jnp.dot(z.astype(BF16), wg_ref[...], preferred_element_type=F32) + bg_ref[...])
    return _rms(z * gate, g_ref[...])


def _ssm_sweep(x_scr, n, a_tabs, init, store):
    S = SSM_BLOCK_STATES
    fins = []
    for j0 in range(0, SSM_LANE_BLOCKS, 2):
        js = (j0, j0 + 1)

        def body(k, carry, js=js):
            r0 = pl.multiple_of(k * SUBLANES, SUBLANES)
            out = []
            for idx, j in enumerate(js):
                hr, hi = carry[2 * idx], carry[2 * idx + 1]
                base = j * 2 * S
                ar, ai = a_tabs[j]
                nhr = ar * hr - ai * hi + x_scr[pl.ds(r0, SUBLANES), base:base + S]
                nhi = ar * hi + ai * hr + x_scr[pl.ds(r0, SUBLANES), base + S:base + 2 * S]
                if store:
                    x_scr[pl.ds(r0, SUBLANES), base:base + S] = nhr
                    x_scr[pl.ds(r0, SUBLANES), base + S:base + 2 * S] = nhi
                out += [nhr, nhi]
            return tuple(out)

        c0 = tuple(t for j in js for t in init[j])
        res = jax.lax.fori_loop(0, n // SUBLANES, body, c0, unroll=True)
        fins += [(res[0], res[1]), (res[2], res[3])]
    return fins


def _ssm_rows(u, pow_row, pw_ref, bblk_ref, cblk_ref, x_scr, s_scr, carry_scr):
    n = u.shape[0]
    S = SSM_BLOCK_STATES
    ub = u.astype(BF16)
    for j in range(SSM_LANE_BLOCKS):
        x_scr[0:n, j * 2 * S:(j + 1) * 2 * S] = jnp.dot(ub[:, j * LANES:(j + 1) * LANES], bblk_ref[j],
                                                       preferred_element_type=F32)
    bc = lambda t: jnp.broadcast_to(t, (SUBLANES, S))
    re = lambda ref, r0, r1, j: ref[r0:r1, j * 2 * S:j * 2 * S + S]
    im = lambda ref, r0, r1, j: ref[r0:r1, j * 2 * S + S:(j + 1) * 2 * S]
    a_tabs = [(bc(re(pw_ref, 0, 1, j)), bc(im(pw_ref, 0, 1, j))) for j in range(SSM_LANE_BLOCKS)]
    zero = jnp.zeros((SUBLANES, S), F32)
    fins = _ssm_sweep(x_scr, n, a_tabs, [(zero, zero)] * SSM_LANE_BLOCKS, store=False)
    for j in range(SSM_LANE_BLOCKS):
        base = j * 2 * S
        cr, ci = re(pw_ref, pow_row, pow_row + 1, j), im(pw_ref, pow_row, pow_row + 1, j)
        sr, si = re(carry_scr, 0, 1, j), im(carry_scr, 0, 1, j)
        fr, fi = fins[j]
        for c in range(SUBLANES):
            s_scr[c:c + 1, base:base + S] = sr
            s_scr[c:c + 1, base + S:base + 2 * S] = si
            sr, si = cr * sr - ci * si + fr[c:c + 1], cr * si + ci * sr + fi[c:c + 1]
        carry_scr[:, base:base + S] = bc(sr)
        carry_scr[:, base + S:base + 2 * S] = bc(si)
    init = [(re(s_scr, 0, SUBLANES, j), im(s_scr, 0, SUBLANES, j)) for j in range(SSM_LANE_BLOCKS)]
    _ssm_sweep(x_scr, n, a_tabs, init, store=True)
    ys = [jnp.dot(x_scr[0:n, j * 2 * S:(j + 1) * 2 * S].astype(BF16), cblk_ref[j], preferred_element_type=F32)
          for j in range(SSM_LANE_BLOCKS)]
    return jnp.concatenate(ys, axis=1)


def _ssm_tile(u_ref, n, pow_row, seq_start, refs, o_ref, scr):
    pw_ref, bblk_ref, cblk_ref, d_ref, wg_ref, bg_ref, g_ref = refs
    up_scr, x_scr, s_scr, carry_scr, o_scr = scr
    q = n // SUBLANES
    for j in range(SSM_LANE_BLOCKS):
        for k in range(q):
            up_scr[k * SUBLANES:(k + 1) * SUBLANES, j * LANES:(j + 1) * LANES] = \
                u_ref[j, pl.ds(k, SUBLANES, stride=q), :]
    u = up_scr[0:n, :]
    if seq_start:
        p = jax.lax.broadcasted_iota(jnp.int32, (n, 1), 0)
        u = jnp.where((p % SUBLANES) * q + p // SUBLANES >= seq_start, u, 0.0)
    y = _ssm_rows(u, pow_row, pw_ref, bblk_ref, cblk_ref, x_scr, s_scr, carry_scr)
    out = _ssm_tail(y, u, d_ref, wg_ref, bg_ref, g_ref)
    for j in range(SSM_LANE_BLOCKS):
        for k in range(q):
            o_scr[j, pl.ds(k, SUBLANES, stride=q), :] = out[k * SUBLANES:(k + 1) * SUBLANES,
                                                            j * LANES:(j + 1) * LANES]
    o_ref[...] = jnp.concatenate([o_scr[j, 0:n, :] for j in range(SSM_LANE_BLOCKS)], axis=1).astype(o_ref.dtype)


def _mixers_kernel(sink_ref,
                   qm_ref, qh_ref, kpm_ref, km_ref, kh_ref, vpm_ref, vm_ref, vh_ref, ga_ref, bias_ref,
                   um_ref, uh_ref, pw_ref, bblk_ref, cblk_ref, d_ref, wg_ref, bg_ref, gs_ref,
                   xm_ref, halo_ref, xh_ref, wp_ref, sc_ref, gp_ref, *rest, l, n_cast):
    cast_in = rest[0:n_cast]
    am_ref, ah_ref, som_ref, soh_ref, st_ref, pom_ref, poh_ref = rest[n_cast:n_cast + 7]
    cast_out = rest[n_cast + 7:2 * n_cast + 7]
    scr = rest[2 * n_cast + 7:]
    carry_scr = scr[3]
    t = pl.program_id(1)
    tm = xm_ref.shape[0]
    ssm_refs = (pw_ref, bblk_ref, cblk_ref, d_ref, wg_ref, bg_ref, gs_ref)
    blk = functools.partial(_attn_block, sink_ref=sink_ref, l=l, g=ga_ref[...])

    @pl.when(t == 0)
    def _():
        a = blk(qh_ref[...], kpm_ref[...], kh_ref[...], vpm_ref[...], vh_ref[...], bias_ref[0])
        ah_ref[...] = a.astype(ah_ref.dtype)
        carry_scr[...] = jnp.zeros_like(carry_scr)
        _ssm_tile(uh_ref, HEAD_ROWS, POW_ROW_HEAD, SEQ0, ssm_refs, soh_ref, scr)
        hrow = jax.lax.broadcasted_iota(jnp.int32, (HEAD_ROWS, 1), 0)
        x = jnp.where(hrow >= SEQ0, xh_ref[...], 0.0)
        prev = jnp.zeros((POOL_HALO, POOL_WIDTH), F32)
        poh_ref[...] = _pool_rows(x, prev, -SEQ0, wp_ref, sc_ref, gp_ref).astype(poh_ref.dtype)

    @pl.when(t > 0)
    def _():
        for src, dst in zip(cast_in, cast_out):
            dst[...] = src[...].astype(BF16)
        kp = jnp.where(t == 1, kh_ref[...], kpm_ref[...])
        vp = jnp.where(t == 1, vh_ref[...], vpm_ref[...])
        for n in range(tm // BLOCK):
            rows = slice(n * BLOCK, (n + 1) * BLOCK)
            kc, vc = km_ref[rows, :], vm_ref[rows, :]
            bias = bias_ref[jnp.minimum(t, 2)] if n == 0 else bias_ref[2]
            am_ref[rows, :] = blk(qm_ref[rows, :], kp, kc, vp, vc, bias).astype(am_ref.dtype)
            kp, vp = kc, vc
        _ssm_tile(um_ref, um_ref.shape[1], POW_ROW_MAIN, 0, ssm_refs, som_ref, scr)
        st_ref[...] = carry_scr[...]
        prev = jnp.where(t == 1, xh_ref[HEAD_ROWS - POOL_HALO:, :], halo_ref[...])
        pom_ref[...] = _pool_rows(xm_ref[...], prev, N_META + (t - 1) * tm, wp_ref, sc_ref, gp_ref).astype(pom_ref.dtype)


def _mixers(sinks, qkv_m, qkv_h, um, uh, xm, xh, vecs, bias, pw, bblk, cblk, wg, wp, l, B, tm, casts):
    assert (1, HEAD_ROWS // SUBLANES, tm // SUBLANES) == SSM_POW_ROWS
    J = SSM_LANE_BLOCKS
    rm, rh = um.shape[1], uh.shape[1]
    nt = rm // (B * tm)
    r = tm // POOL_HALO
    bpt = tm // BLOCK
    tile = lambda b, t: b * nt + jnp.maximum(t - 1, 0)
    main = lambda b, t, _: (tile(b, t), 0)
    head = lambda b, t, _: (b, 0)
    halo = lambda b, t, _: (jnp.maximum((b * nt + t - 1) * r - 1, 0), 0)
    prevb = lambda b, t, _: (jnp.maximum(tile(b, t) * bpt - 1, 0), 0)
    cast_specs = []
    for w, wl in casts:
        _, cr, cc = w.shape
        rows = cr // (B * nt)
        cast_specs.append((pl.BlockSpec((None, rows, cc), lambda b, t, _, wl=wl: (wl, tile(b, t), 0)),
                           pl.BlockSpec((rows, cc), main), jax.ShapeDtypeStruct((cr, cc), BF16)))
    bf = lambda n, w: jax.ShapeDtypeStruct((n, w), BF16)
    tile_spec = lambda w: pl.BlockSpec((tm, w), main)
    head_spec = lambda w: pl.BlockSpec((HEAD_ROWS, w), head)
    in_specs = [tile_spec(ATTN_WIDTH), head_spec(ATTN_WIDTH),
                pl.BlockSpec((BLOCK, KV_WIDTH), prevb), tile_spec(KV_WIDTH), head_spec(KV_WIDTH),
                pl.BlockSpec((BLOCK, KV_WIDTH), prevb), tile_spec(KV_WIDTH), head_spec(KV_WIDTH),
                _vrow("g_out_attn", l), _full(bias.shape),
                pl.BlockSpec((J, tm, LANES), lambda b, t, _: (0, tile(b, t), 0)),
                pl.BlockSpec((J, HEAD_ROWS, LANES), lambda b, t, _: (0, b, 0)),
                _layer(pw.shape[1:], l), _layer(bblk.shape[1:], l), _layer(cblk.shape[1:], l),
                _vrow("D_skip", l), _layer((SSM_WIDTH, SSM_WIDTH), l), _vrow("b_glu", l), _vrow("g_out_ssm", l),
                tile_spec(POOL_WIDTH), pl.BlockSpec((POOL_HALO, POOL_WIDTH), halo), head_spec(POOL_WIDTH),
                _layer(wp.shape[1:], l), _vrow("pool_scale", l), _vrow("g_out_pool", l)]
    in_specs += [c[0] for c in cast_specs]
    out_specs = [tile_spec(ATTN_WIDTH), head_spec(ATTN_WIDTH), tile_spec(SSM_WIDTH), head_spec(SSM_WIDTH),
                 pl.BlockSpec((None, SUBLANES, SSM_STATE_LANES), lambda b, t, _: (b, 0, 0)),
                 tile_spec(POOL_WIDTH), head_spec(POOL_WIDTH)] + [c[1] for c in cast_specs]
    out_shape = [bf(rm, ATTN_WIDTH), bf(rh, ATTN_WIDTH), bf(rm, SSM_WIDTH), bf(rh, SSM_WIDTH),
                 jax.ShapeDtypeStruct((B, SUBLANES, SSM_STATE_LANES), F32),
                 bf(rm, POOL_WIDTH), bf(rh, POOL_WIDTH)] + [c[2] for c in cast_specs]
    (qm, km, vm), (qh, kh, vh) = qkv_m, qkv_h
    res = pl.pallas_call(
        functools.partial(_mixers_kernel, l=l, n_cast=len(casts)),
        grid_spec=pltpu.PrefetchScalarGridSpec(
            num_scalar_prefetch=1, grid=(B, nt + 1), in_specs=in_specs, out_specs=out_specs,
            scratch_shapes=[pltpu.VMEM((tm, SSM_WIDTH), F32),
                            pltpu.VMEM((tm, SSM_STATE_LANES), F32),
                            pltpu.VMEM((SUBLANES, SSM_STATE_LANES), F32),
                            pltpu.VMEM((SUBLANES, SSM_STATE_LANES), F32),
                            pltpu.VMEM((J, tm, LANES), F32)]),
        out_shape=out_shape,
        compiler_params=_params("arbitrary", "arbitrary"),
        name="mixers",
    )(sinks, qm, qh, km, km, kh, vm, vm, vh, vecs, bias, um, uh, pw, bblk, cblk, vecs, wg, vecs, vecs,
      xm, xm, xh, wp, vecs, vecs, *[w for w, _ in casts])
    return res[0:7], res[7:]


def _state_from_lanes(s):
    s = s.reshape(s.shape[0], SSM_LANE_BLOCKS, 2, SSM_BLOCK_STATES)
    return (s[:, :, 0].reshape(-1, SSM_GROUPS, SSM_STATE), s[:, :, 1].reshape(-1, SSM_GROUPS, SSM_STATE))


def _pool_tail(d_groups, w_ref, sc_ref, g_ref):
    y = jnp.concatenate(
        [jnp.dot(d.astype(BF16), w_ref[gi], preferred_element_type=F32) for gi, d in enumerate(d_groups)], axis=1)
    return _rms(y * sc_ref[...], g_ref[...])


def _pool_rows(x, prev, pos0, w_ref, sc_ref, g_ref):
    n = x.shape[0]
    xe = jnp.concatenate([prev, x], axis=0)
    pos = pos0 + jax.lax.broadcasted_iota(jnp.int32, (n, 1), 0)
    ds = []
    for gi, w in enumerate(POOL_WINDOWS):
        gsl = slice(gi * POOL_GROUP, (gi + 1) * POOL_GROUP)
        s = xe[:, gsl]
        k = 1
        while k < w:
            s = s + pltpu.roll(s, k, 0)
            k *= 2
        cnt = jnp.clip(pos + 1, 1, w).astype(F32)
        ds.append(s[POOL_HALO:] / cnt - x[:, gsl])
    return _pool_tail(ds, w_ref, sc_ref, g_ref)


def _shift_caches_kernel(k_ref, v_ref, nk_ref, nv_ref):
    for src, dst in ((k_ref, nk_ref), (v_ref, nv_ref)):
        for bb in range(DEC_STEP):
            dst[bb, 0:WINDOW - 1] = src[bb, 1:WINDOW]
            dst[bb, WINDOW - 1] = jnp.zeros((N_KV_HEADS, HEAD_DIM), F32)


def _shift_caches(cache_k, cache_v):
    depth, N = cache_k.shape[:2]
    blk = pl.BlockSpec((None, DEC_STEP, WINDOW, N_KV_HEADS, HEAD_DIM), lambda l, s: (l, s, 0, 0, 0))
    shape = jax.ShapeDtypeStruct(cache_k.shape, F32)
    return pl.pallas_call(
        _shift_caches_kernel,
        grid=(depth, N // DEC_STEP),
        in_specs=[blk, blk],
        out_specs=[blk, blk],
        out_shape=[shape, shape],
        compiler_params=_params("arbitrary", "arbitrary"),
        name="shift_caches",
    )(cache_k, cache_v)


def _attn_sample_kernel(q_ref, kn_ref, vn_ref, kc_ref, vc_ref, sink_ref, g_ref, a_in_ref, nk_in_ref, nv_in_ref,
                        a_ref, nk_ref, nv_ref, acc_scr):
    del a_in_ref, nk_in_ref, nv_in_ref
    step = pl.program_id(0)
    scale = HEAD_DIM ** -0.5
    for bb in range(DEC_STEP):
        outs = []
        for kh in range(N_KV_HEADS):
            ksl = slice(kh * HEAD_DIM, (kh + 1) * HEAD_DIM)
            qh = jnp.concatenate(
                [q_ref[bb:bb + 1, (kh * GQA_GROUP + g) * HEAD_DIM:(kh * GQA_GROUP + g + 1) * HEAD_DIM]
                 for g in range(GQA_GROUP)], axis=0)
            kn = kn_ref[bb:bb + 1, ksl]
            vn = vn_ref[bb:bb + 1, ksl]
            nk_ref[bb, 0, kh:kh + 1, :] = kn
            nv_ref[bb, 0, kh:kh + 1, :] = vn
            kc = kc_ref[bb, :, kh, :]
            vc = vc_ref[bb, :, kh, :]
            sc = jax.lax.dot_general(qh.astype(BF16), kc.astype(BF16), (((1,), (1,)), ((), ())),
                                     preferred_element_type=F32) * scale
            sn = jnp.sum(qh * kn, axis=-1, keepdims=True) * scale
            sk = sink_ref[kh]
            m = jnp.maximum(jnp.maximum(jnp.max(sc, axis=-1, keepdims=True), sn), sk)
            pc = jnp.exp(sc - m)
            pn = jnp.exp(sn - m)
            denom = jnp.sum(pc, axis=-1, keepdims=True) + pn + jnp.exp(sk - m)
            o = jnp.dot(pc.astype(BF16), vc.astype(BF16), preferred_element_type=F32)
            o = (o + pn * vn) / denom
            outs.extend(o[g:g + 1] for g in range(GQA_GROUP))
        a = jnp.concatenate(outs, axis=1)
        acc_scr[pl.ds(step * DEC_STEP + bb, 1), :] = _rms(a, g_ref[...])

    @pl.when(step == pl.num_programs(0) - 1)
    def _():
        a_ref[...] = acc_scr[...].astype(a_ref.dtype)


def _attn_sample(qh, kh, vh, cache_k, cache_v, l, sinks, g, ah, nk, nv, N):
    rows = lambda w: pl.BlockSpec((DEC_STEP, w), lambda s: (s, 0))
    cache = pl.BlockSpec((None, DEC_STEP, WINDOW, N_KV_HEADS, HEAD_DIM), lambda s: (l, s, 0, 0, 0))
    last = pl.BlockSpec((None, DEC_STEP, 1, N_KV_HEADS, HEAD_DIM), lambda s: (l, s, WINDOW - 1, 0, 0))
    anyspec = pl.BlockSpec(memory_space=pl.ANY)
    return pl.pallas_call(
        _attn_sample_kernel,
        grid=(N // DEC_STEP,),
        in_specs=[rows(ATTN_WIDTH), rows(KV_WIDTH), rows(KV_WIDTH), cache, cache,
                  _layer((N_KV_HEADS, GQA_GROUP, 1), l), _vrow("g_out_attn", l),
                  anyspec, anyspec, anyspec],
        out_specs=[pl.BlockSpec((N, ATTN_WIDTH), lambda s: (0, 0)), last, last],
        out_shape=[jax.ShapeDtypeStruct(ah.shape, ah.dtype), jax.ShapeDtypeStruct(nk.shape, nk.dtype),
                   jax.ShapeDtypeStruct(nv.shape, nv.dtype)],
        scratch_shapes=[pltpu.VMEM((N, ATTN_WIDTH), F32)],
        input_output_aliases={7: 0, 8: 1, 9: 2},
        compiler_params=_params("arbitrary"),
        name="attn_sample",
    )(qh, kh, vh, cache_k, cache_v, sinks, g, ah, nk, nv)


def _mix_sample_kernel(u_ref, h0_ref, pw_ref, bblk_ref, cblk_ref, d_ref, wg_ref, bg_ref, gs_ref,
                       xp_ref, pb_ref, wp_ref, sc_ref, gp_ref, s_in_ref, p_in_ref, s_ref, st_ref, p_ref):
    del s_in_ref, p_in_ref
    S = SSM_BLOCK_STATES
    u = jnp.concatenate([u_ref[j] for j in range(SSM_LANE_BLOCKS)], axis=1)
    ub = u.astype(BF16)
    ys = []
    for j in range(SSM_LANE_BLOCKS):
        x = jnp.dot(ub[:, j * LANES:(j + 1) * LANES], bblk_ref[j], preferred_element_type=F32)
        base = j * 2 * S
        ar = pw_ref[0:1, base:base + S]
        ai = pw_ref[0:1, base + S:base + 2 * S]
        h0r = h0_ref[:, base:base + S]
        h0i = h0_ref[:, base + S:base + 2 * S]
        hr = x[:, 0:S] + ar * h0r - ai * h0i
        hi = x[:, S:] + ar * h0i + ai * h0r
        st_ref[:, base:base + S] = hr
        st_ref[:, base + S:base + 2 * S] = hi
        h = jnp.concatenate([hr, hi], axis=1).astype(BF16)
        ys.append(jnp.dot(h, cblk_ref[j], preferred_element_type=F32))
    s_ref[...] = _ssm_tail(jnp.concatenate(ys, axis=1), u, d_ref, wg_ref, bg_ref, gs_ref).astype(s_ref.dtype)

    xp = xp_ref[...]
    ds = []
    for gi, w in enumerate(POOL_WINDOWS):
        gsl = slice(gi * POOL_GROUP, (gi + 1) * POOL_GROUP)
        s = xp[:, gsl]
        for back in range(1, w):
            s = s + pb_ref[POOL_BUF - back][:, gsl]
        ds.append(s / float(w) - xp[:, gsl])
    p_ref[...] = _pool_tail(ds, wp_ref, sc_ref, gp_ref).astype(p_ref.dtype)


def _mix_sample(uh, xph, h0, pbuf, vecs, pw, bblk, cblk, wg, wp, l, sh, ph):
    N = h0.shape[1]
    rows = lambda w: pl.BlockSpec((N, w), lambda i: (0, 0))
    anyspec = pl.BlockSpec(memory_space=pl.ANY)
    return pl.pallas_call(
        _mix_sample_kernel,
        grid=(1,),
        in_specs=[pl.BlockSpec((SSM_LANE_BLOCKS, N, LANES), lambda i: (0, 0, 0)),
                  _layer(h0.shape[1:], l), _layer(pw.shape[1:], l), _layer(bblk.shape[1:], l),
                  _layer(cblk.shape[1:], l),
                  _vrow("D_skip", l), _layer((SSM_WIDTH, SSM_WIDTH), l), _vrow("b_glu", l),
                  _vrow("g_out_ssm", l), rows(POOL_WIDTH), _layer(pbuf.shape[1:], l),
                  _layer(wp.shape[1:], l),
                  _vrow("pool_scale", l), _vrow("g_out_pool", l), anyspec, anyspec],
        out_specs=[rows(SSM_WIDTH), _full((N, SSM_STATE_LANES)), rows(POOL_WIDTH)],
        out_shape=[jax.ShapeDtypeStruct(sh.shape, sh.dtype), jax.ShapeDtypeStruct((N, SSM_STATE_LANES), F32),
                   jax.ShapeDtypeStruct(ph.shape, ph.dtype)],
        input_output_aliases={14: 0, 15: 2},
        compiler_params=_params("arbitrary"),
        name="mix_sample",
    )(uh, h0, pw, bblk, cblk, vecs, wg, vecs, vecs, xph, pbuf, wp, vecs, vecs, sh, ph)


def _outproj_rows(x_ref, a_ref, s_ref, p_ref, w_ref, o_ref):
    o1 = ATTN_WIDTH
    o2 = o1 + SSM_WIDTH
    acc = x_ref[...]
    acc = acc + jnp.dot(a_ref[...], w_ref[0:o1, :], preferred_element_type=F32)
    acc = acc + jnp.dot(s_ref[...], w_ref[o1:o2, :], preferred_element_type=F32)
    acc = acc + jnp.dot(p_ref[...], w_ref[o2:, :], preferred_element_type=F32)
    o_ref[...] = acc


def _outproj_kernel(xm_ref, am_ref, sm_ref, pm_ref, xh_ref, ah_ref, sh_ref, ph_ref, w_ref, om_ref, oh_ref):
    i = pl.program_id(0)
    last = pl.num_programs(0) - 1

    @pl.when(i < last)
    def _():
        _outproj_rows(xm_ref, am_ref, sm_ref, pm_ref, w_ref, om_ref)

    @pl.when(i == last)
    def _():
        _outproj_rows(xh_ref, ah_ref, sh_ref, ph_ref, w_ref, oh_ref)


def _outproj(xm, am, sm, pm, xh, ah, sh, ph, w, tm):
    R, RH = xm.shape[0], xh.shape[0]
    nm = R // tm
    row = lambda i: (jnp.minimum(i, nm - 1), 0)
    once = pl.Buffered(1)
    head = lambda w_: pl.BlockSpec((RH, w_), lambda i: (0, 0), pipeline_mode=once)
    return pl.pallas_call(
        _outproj_kernel,
        grid=(nm + 1,),
        in_specs=[pl.BlockSpec((tm, D_MODEL), row), pl.BlockSpec((tm, ATTN_WIDTH), row),
                  pl.BlockSpec((tm, SSM_WIDTH), row), pl.BlockSpec((tm, POOL_WIDTH), row),
                  head(D_MODEL), head(ATTN_WIDTH), head(SSM_WIDTH), head(POOL_WIDTH),
                  pl.BlockSpec((D_MODEL, D_MODEL), lambda i: (0, 0), pipeline_mode=once)],
        out_specs=[pl.BlockSpec((tm, D_MODEL), row), pl.BlockSpec((RH, D_MODEL), lambda i: (0, 0))],
        out_shape=[jax.ShapeDtypeStruct((R, D_MODEL), F32), jax.ShapeDtypeStruct((RH, D_MODEL), F32)],
        compiler_params=_params("arbitrary"),
        name="outproj",
    )(xm, am, sm, pm, xh, ah, sh, ph, w)


def _ffn_kernel(xm_ref, xh_ref, g_ref, w1_ref, w2_ref, om_ref, oh_ref, hm_scr, hh_scr):
    i = pl.program_id(0)
    f = pl.program_id(1)

    @pl.when(f == 0)
    def _():
        x = xm_ref[...]
        hm_scr[...] = _rms(x, g_ref[...]).astype(BF16)
        om_ref[...] = x

    @pl.when((f == 0) & (i == 0))
    def _():
        x = xh_ref[...]
        hh_scr[...] = _rms(x, g_ref[...]).astype(BF16)
        oh_ref[...] = x

    w1 = w1_ref[...]
    w2 = w2_ref[...]

    def mlp(h):
        h1 = jnp.dot(h, w1, preferred_element_type=F32)
        return jnp.dot(jnp.square(jnp.maximum(h1, 0.0)).astype(BF16), w2, preferred_element_type=F32)

    om_ref[...] += mlp(hm_scr[...])

    @pl.when(i == 0)
    def _():
        oh_ref[...] += mlp(hh_scr[...])


def _ffn(xm, xh, g, l, w1, w2, tm, tf):
    R, RH = xm.shape[0], xh.shape[0]
    return pl.pallas_call(
        _ffn_kernel,
        grid=(R // tm, D_FF // tf),
        in_specs=[pl.BlockSpec((tm, D_MODEL), lambda i, f: (i, 0)),
                  pl.BlockSpec((RH, D_MODEL), lambda i, f: (0, 0), pipeline_mode=pl.Buffered(1)),
                  _vrow("g_ffn", l),
                  pl.BlockSpec((D_MODEL, tf), lambda i, f: (0, f)),
                  pl.BlockSpec((tf, D_MODEL), lambda i, f: (f, 0))],
        out_specs=[pl.BlockSpec((tm, D_MODEL), lambda i, f: (i, 0)),
                   pl.BlockSpec((RH, D_MODEL), lambda i, f: (0, 0))],
        out_shape=[jax.ShapeDtypeStruct((R, D_MODEL), F32), jax.ShapeDtypeStruct((RH, D_MODEL), F32)],
        scratch_shapes=[pltpu.VMEM((tm, D_MODEL), BF16), pltpu.VMEM((RH, D_MODEL), BF16)],
        compiler_params=_params("arbitrary", "arbitrary"),
        name="ffn",
    )(xm, xh, g, w1, w2)


def _rope_tables(pos):
    half = ROT_HALF
    inv = ROPE_THETA ** (-np.arange(0, ROT_DIM, 2, dtype=np.float64) / ROT_DIM)
    ang = np.asarray(pos, np.float64)[:, None] * inv
    cos, sin = np.cos(ang), np.sin(ang)
    n = ang.shape[0]
    z = np.zeros((n, HEAD_DIM - ROT_DIM))
    zh = np.zeros((n, half))
    rc = np.concatenate([cos, cos, z + 1.0], axis=1)
    rs1 = np.concatenate([zh, sin, z], axis=1)
    rs2 = np.concatenate([-sin, zh, z], axis=1)
    return tuple(t.astype(np.float32) for t in (rc, rs1, rs2))


def kernel(x_prompt, x_sample, cache_k, cache_v, state_ssm_re, state_ssm_im, state_pool, meta_tokens, g_mix, w_in, g_q, g_k, sinks, A_re, A_im, log_dt, B_re, B_im, C_re, C_im, D_skip, w_glu, b_glu, w_pool, pool_scale, g_out_attn, g_out_ssm, g_out_pool, w_out, g_ffn, w_ff1, w_ff2):
    B, T, _ = x_prompt.shape
    N = x_sample.shape[0]
    depth = w_in.shape[0]
    assert N <= SEQ0 and N % DEC_STEP == 0 and T % TM_FFN == 0
    meta = meta_tokens.astype(F32)
    head0 = jnp.concatenate([x_sample.reshape(N, D_MODEL), jnp.zeros((SEQ0 - N, D_MODEL), F32), meta], axis=0)
    head_rest = jnp.concatenate([jnp.zeros((SEQ0, D_MODEL), F32), meta], axis=0)
    xh = jnp.concatenate([head0] + [head_rest] * (B - 1), axis=0)
    xm = x_prompt.reshape(B * T, D_MODEL)

    rope_m = _rope_tables(N_META + np.arange(T))
    hr_ = np.arange(HEAD_ROWS)
    pos_h0 = np.where(hr_ < N, PAST_LEN, np.maximum(hr_ - SEQ0, 0))
    pos_h = np.concatenate([pos_h0] + [np.maximum(hr_ - SEQ0, 0)] * (B - 1))
    rope_h = _rope_tables(pos_h)

    wi = w_in[0].astype(BF16)
    wg_all, wp_all = w_glu.astype(BF16), w_pool.astype(BF16)
    ck = cache_k.astype(F32)
    cv = cache_v.astype(F32)
    RH = B * HEAD_ROWS

    named = dict(g_mix=g_mix, g_ffn=g_ffn, g_out_attn=g_out_attn, D_skip=D_skip, b_glu=b_glu,
                 g_out_ssm=g_out_ssm, pool_scale=pool_scale, g_out_pool=g_out_pool, g_q=g_q, g_k=g_k)
    vecs = jnp.concatenate([named[name].astype(F32) for name, _ in VEC_LAYOUT], axis=1)[:, None, :]
    pw, bblk, cblk = _ssm_params(A_re, A_im, log_dt, B_re, B_im, C_re, C_im)
    sinks_flat = sinks.astype(F32).reshape(depth * N_HEADS)
    sinks_col = sinks.astype(F32).reshape(depth, N_KV_HEADS, GQA_GROUP, 1)
    bias = _attn_bias()
    h0 = jnp.concatenate([state_ssm_re.astype(F32).reshape(depth, N, SSM_LANE_BLOCKS, SSM_BLOCK_STATES),
                          state_ssm_im.astype(F32).reshape(depth, N, SSM_LANE_BLOCKS, SSM_BLOCK_STATES)],
                         axis=-1).reshape(depth, N, SSM_STATE_LANES)
    pbuf = state_pool.astype(F32).transpose(0, 2, 1, 3)

    nk, nv = _shift_caches(ck, cv)
    ks, vs, pls, sts, st_ss, phs = ([] for _ in range(6))
    for l in range(depth):
        (qm, km, vm, um, pm), (qh, kh, vh, uh, ph), w1 = _inproj(xm, xh, vecs, wi, l, rope_m, rope_h, TM_PROJ, w_ff1)

        casts = [(w_ff2, l), (w_out, l)] + ([(w_in, l + 1)] if l + 1 < depth else [])
        (am, ah, sm, sh, st, plm, plh), wcast = _mixers(
            sinks_flat, (qm, km, vm), (qh, kh, vh), um, uh, pm, ph, vecs, bias, pw, bblk, cblk, wg_all, wp_all,
            l, B, TM_SEQ, casts)
        w2, wo = wcast[0], wcast[1]
        if l + 1 < depth:
            wi = wcast[2]

        ah, nk, nv = _attn_sample(qh, kh, vh, ck, cv, l, sinks_col, vecs, ah, nk, nv, N)
        sh, st_s, plh = _mix_sample(uh, ph, h0, pbuf, vecs, pw, bblk, cblk, wg_all, wp_all, l, sh, plh)

        xm, xh = _outproj(xm, am, sm, plm, xh, ah, sh, plh, wo, TM_PROJ)
        xm, xh = _ffn(xm, xh, vecs, l, w1, w2, TM_FFN, TF_FFN)

        ks.append(km.reshape(B, T, KV_WIDTH)[:, T - WINDOW:])
        vs.append(vm.reshape(B, T, KV_WIDTH)[:, T - WINDOW:])
        pls.append(pm.reshape(B, T, POOL_WIDTH)[:, T - POOL_BUF:])
        sts.append(st[:, 0])
        st_ss.append(st_s)
        phs.append(ph[:N])

    y_prompt = xm.reshape(B, T, D_MODEL)
    y_sample = xh[:N].reshape(N, 1, D_MODEL)
    heads = lambda t: jnp.stack(t).reshape(depth, -1, WINDOW, N_KV_HEADS, HEAD_DIM)
    p_re, p_im = _state_from_lanes(jnp.stack(sts).reshape(depth * B, SSM_STATE_LANES))
    s_re, s_im = _state_from_lanes(jnp.stack(st_ss).reshape(depth * N, SSM_STATE_LANES))
    st4 = lambda t, n: t.reshape(depth, n, SSM_GROUPS, SSM_STATE)
    s_pool = jnp.concatenate([state_pool.astype(F32)[:, :, 1:], jnp.stack(phs)[:, :, None]], axis=2)
    return (y_prompt, y_sample, heads(ks), heads(vs), st4(p_re, B), st4(p_im, B), jnp.stack(pls),
            nk, nv, st4(s_re, N), st4(s_im, N), s_pool)
```

```python
import functools
import math

import jax
import jax.numpy as jnp
import numpy as np
from jax.experimental import pallas as pl
from jax.experimental.pallas import tpu as pltpu

D_MODEL = 2048
N_META = 16
HEAD_DIM = 128
N_HEADS = 8
N_KV_HEADS = 2
GQA_GROUP = 4
ATTN_WIDTH = 1024
KV_WIDTH = 256
WINDOW = 128
BLOCK = 128
ROT_DIM = 32
ROT_HALF = ROT_DIM // 2
ROPE_THETA = 500000.0
SSM_WIDTH = 512
SSM_GROUP_SIZE = 16
SSM_GROUPS = 32
SSM_STATE = 64
POOL_WIDTH = 512
POOL_WINDOWS = (2, 4, 8, 16)
POOL_GROUP = 128
POOL_BUF = 15
POOL_HALO = 16
IN_WIDTH = 2560
D_FF = 8192
EPS = 1e-6
PAST_LEN = 16384
LOG2E = math.log2(math.e)

HEAD_ROWS = BLOCK
SEQ0 = HEAD_ROWS - N_META
LANES = 128
SUBLANES = 8
SSM_LANE_BLOCKS = SSM_WIDTH // LANES
SSM_BLOCK_STATES = (LANES // SSM_GROUP_SIZE) * SSM_STATE
SSM_STATE_LANES = SSM_LANE_BLOCKS * 2 * SSM_BLOCK_STATES
VMEM_LIMIT = 60 * 1024 * 1024

TM_PROJ = 512
TM_FFN = 1024
TF_FFN = 512
TM_SEQ = 512
DEC_STEP = 8
SSM_POW_ROWS = (1, HEAD_ROWS // SUBLANES, TM_SEQ // SUBLANES)
POW_ROW_HEAD, POW_ROW_MAIN = 1, 2

BF16 = jnp.bfloat16
F32 = jnp.float32


def _params(*semantics):
    return pltpu.CompilerParams(dimension_semantics=semantics, vmem_limit_bytes=VMEM_LIMIT)


def _rms(x, g):
    return x * jax.lax.rsqrt(jnp.mean(x * x, axis=-1, keepdims=True) + EPS) * g


def _full(shape):
    n = len(shape)
    return pl.BlockSpec(shape, lambda *_: (0,) * n)


def _layer(shape, l):
    n = len(shape)
    return pl.BlockSpec((None, *shape), lambda *_: (l,) + (0,) * n)


VEC_LAYOUT = (("g_mix", D_MODEL), ("g_ffn", D_MODEL), ("g_out_attn", ATTN_WIDTH), ("D_skip", SSM_WIDTH),
              ("b_glu", SSM_WIDTH), ("g_out_ssm", SSM_WIDTH), ("pool_scale", POOL_WIDTH),
              ("g_out_pool", POOL_WIDTH), ("g_q", HEAD_DIM), ("g_k", HEAD_DIM))
VEC_WIDTH = dict(VEC_LAYOUT)
VEC_OFFSET = {name: sum(w for _, w in VEC_LAYOUT[:i]) for i, (name, _) in enumerate(VEC_LAYOUT)}
assert all(VEC_OFFSET[name] % w == 0 for name, w in VEC_LAYOUT)


def _vrow(name, l):
    w = VEC_WIDTH[name]
    return pl.BlockSpec((None, 1, w), lambda *_: (l, 0, VEC_OFFSET[name] // w))


def _inproj_rows(x, g_ref, w_ref, gq_ref, gk_ref, rope_refs, out_refs):
    q_ref, k_ref, v_ref, u_ref, xp_ref = out_refs
    h = _rms(x, g_ref[...]).astype(BF16)
    proj = jnp.dot(h, w_ref[...], preferred_element_type=F32)
    rc, rs1, rs2 = (r[...] for r in rope_refs)

    def head(t, g):
        t = _rms(t, g)
        return t * rc + pltpu.roll(t, ROT_HALF, 1) * rs1 + pltpu.roll(t, LANES - ROT_HALF, 1) * rs2

    for hd in range(N_HEADS):
        sl = slice(hd * HEAD_DIM, (hd + 1) * HEAD_DIM)
        q_ref[:, sl] = head(proj[:, sl], gq_ref[...])
    for hd in range(N_KV_HEADS):
        sl = slice(hd * HEAD_DIM, (hd + 1) * HEAD_DIM)
        k_ref[:, sl] = head(proj[:, ATTN_WIDTH + hd * HEAD_DIM:ATTN_WIDTH + (hd + 1) * HEAD_DIM], gk_ref[...])
    o2 = ATTN_WIDTH + KV_WIDTH
    o3 = o2 + KV_WIDTH
    o4 = o3 + SSM_WIDTH
    v_ref[...] = proj[:, o2:o3]
    for j in range(SSM_LANE_BLOCKS):
        u_ref[j] = proj[:, o3 + j * LANES:o3 + (j + 1) * LANES]
    xp_ref[...] = proj[:, o4:]


def _inproj_kernel(xm_ref, xh_ref, g_ref, w_ref, gq_ref, gk_ref, rcm_ref, rs1m_ref, rs2m_ref,
                   rch_ref, rs1h_ref, rs2h_ref, cast_in_ref, *outs):
    main_outs, head_outs, cast_out_ref = outs[0:5], outs[5:10], outs[10]
    i = pl.program_id(0)
    last = pl.num_programs(0) - 1

    @pl.when(i < last)
    def _():
        cast_out_ref[...] = cast_in_ref[...].astype(BF16)
        _inproj_rows(xm_ref[...], g_ref, w_ref, gq_ref, gk_ref, (rcm_ref, rs1m_ref, rs2m_ref), main_outs)

    @pl.when(i == last)
    def _():
        _inproj_rows(xh_ref[...], g_ref, w_ref, gq_ref, gk_ref, (rch_ref, rs1h_ref, rs2h_ref), head_outs)


def _inproj(xm, xh, vecs, w, l, rope_m, rope_h, tm, cast):
    R, RH = xm.shape[0], xh.shape[0]
    nm = R // tm
    tiles_per_rope = rope_m[0].shape[0] // tm
    tile = lambda i: jnp.minimum(i, nm - 1)
    row = lambda i: (tile(i), 0)
    rrow = lambda i: (tile(i) % tiles_per_rope, 0)
    once = pl.Buffered(1)
    full = lambda n, w_: pl.BlockSpec((n, w_), lambda i: (0, 0))

    def out_set(n, rows, im, uim):
        flat = lambda w_: (pl.BlockSpec((n, w_), im), jax.ShapeDtypeStruct((rows, w_), F32))
        u_out = (pl.BlockSpec((SSM_LANE_BLOCKS, n, LANES), uim),
                 jax.ShapeDtypeStruct((SSM_LANE_BLOCKS, rows, LANES), F32))
        return [flat(ATTN_WIDTH), flat(KV_WIDTH), flat(KV_WIDTH), u_out, flat(POOL_WIDTH)]

    outs = (out_set(tm, R, row, lambda i: (0, tile(i), 0))
            + out_set(RH, RH, lambda i: (0, 0), lambda i: (0, 0, 0)))
    _, cr, cc = cast.shape
    crows = cr // nm
    outs.append((pl.BlockSpec((crows, cc), row), jax.ShapeDtypeStruct((cr, cc), BF16)))
    in_specs = [pl.BlockSpec((tm, D_MODEL), row),
                pl.BlockSpec((RH, D_MODEL), lambda i: (0, 0), pipeline_mode=once),
                _vrow("g_mix", l),
                pl.BlockSpec((D_MODEL, IN_WIDTH), lambda i: (0, 0), pipeline_mode=once),
                _vrow("g_q", l), _vrow("g_k", l),
                pl.BlockSpec((tm, LANES), rrow), pl.BlockSpec((tm, LANES), rrow), pl.BlockSpec((tm, LANES), rrow),
                full(RH, LANES), full(RH, LANES), full(RH, LANES),
                pl.BlockSpec((None, crows, cc), lambda i: (l, tile(i), 0))]
    res = pl.pallas_call(
        _inproj_kernel,
        grid=(nm + 1,),
        in_specs=in_specs,
        out_specs=[o[0] for o in outs],
        out_shape=[o[1] for o in outs],
        compiler_params=_params("arbitrary"),
        name="inproj",
    )(xm, xh, vecs, w, vecs, vecs, *rope_m, *rope_h, cast)
    return res[0:5], res[5:10], res[10]


def _attn_bias():
    rows = GQA_GROUP * BLOCK
    i, r, c = np.meshgrid(np.arange(3), np.arange(rows) % BLOCK, np.arange(2 * BLOCK), indexing="ij")
    diff = BLOCK + r - c
    krow = (i - 1) * BLOCK + c
    mask = (diff >= 0) & (diff <= WINDOW) & (krow >= SEQ0)
    return np.where(mask, 0.0, -np.inf).astype(np.float32)


def _attn_block(q_blk, kp_blk, kc_blk, vp_blk, vc_blk, bias, sink_ref, l, g):
    rows = GQA_GROUP * BLOCK
    rgrp = jax.lax.broadcasted_iota(jnp.int32, (rows, 1), 0) // BLOCK
    outs = []
    for kh in range(N_KV_HEADS):
        ksl = slice(kh * HEAD_DIM, (kh + 1) * HEAD_DIM)
        qh = jnp.concatenate(
            [q_blk[:, (kh * GQA_GROUP + h) * HEAD_DIM:(kh * GQA_GROUP + h + 1) * HEAD_DIM]
             for h in range(GQA_GROUP)], axis=0).astype(BF16)
        kk = jnp.concatenate([kp_blk[:, ksl], kc_blk[:, ksl]], axis=0).astype(BF16)
        vv = jnp.concatenate([vp_blk[:, ksl], vc_blk[:, ksl]], axis=0).astype(BF16)
        s = jax.lax.dot_general(qh, kk, (((1,), (1,)), ((), ())),
                                preferred_element_type=F32) * (HEAD_DIM ** -0.5 * LOG2E) + bias
        sk = jnp.zeros((rows, 1), F32)
        for h in range(GQA_GROUP):
            sk = jnp.where(rgrp == h, sink_ref[l * N_HEADS + kh * GQA_GROUP + h] * LOG2E, sk)
        m = jnp.maximum(jnp.max(s, axis=-1, keepdims=True), sk)
        p = jnp.exp2(s - m)
        denom = jnp.sum(p, axis=-1, keepdims=True) + jnp.exp2(sk - m)
        o = jnp.dot(p.astype(BF16), vv, preferred_element_type=F32) / denom
        outs.extend(o[h * BLOCK:(h + 1) * BLOCK] for h in range(GQA_GROUP))
    return _rms(jnp.concatenate(outs, axis=1), g)


def _ssm_params_kernel(ar_ref, ai_ref, ldt_ref, kk_ref, br_ref, bi_ref, tr_ref, ti_ref, bbr_ref, bbi_ref):
    ar, ai = ar_ref[...], ai_ref[...]
    dt = jnp.exp(ldt_ref[...])
    kk = kk_ref[...]
    mag = jnp.exp(dt * ar * kk)
    ang = dt * ai * kk
    tr = mag * jnp.cos(ang)
    ti = mag * jnp.sin(ang)
    tr_ref[...] = tr
    ti_ref[...] = ti
    abr, abi = tr[0:1], ti[0:1]
    den = ar * ar + ai * ai
    fr = ((abr - 1.0) * ar + abi * ai) / den
    fi = (abi * ar - (abr - 1.0) * ai) / den
    br, bi = br_ref[...], bi_ref[...]
    bbr_ref[...] = fr * br - fi * bi
    bbi_ref[...] = fr * bi + fi * br


def _ssm_params(A_re, A_im, log_dt, B_re, B_im, C_re, C_im):
    depth = A_re.shape[0]
    n = depth * SSM_GROUPS * SSM_STATE
    row = lambda t: t.astype(F32).reshape(1, n)
    ldt = jnp.broadcast_to(log_dt.astype(F32)[:, :, None], (depth, SSM_GROUPS, SSM_STATE)).reshape(1, n)
    kk = jnp.array(SSM_POW_ROWS + (0,) * (SUBLANES - len(SSM_POW_ROWS)), F32).reshape(SUBLANES, 1)
    chan_first = lambda t: t.astype(F32).reshape(n, SSM_GROUP_SIZE).T
    shapes = [(SUBLANES, n), (SUBLANES, n), (SSM_GROUP_SIZE, n), (SSM_GROUP_SIZE, n)]
    tr, ti, bbr, bbi = pl.pallas_call(
        _ssm_params_kernel,
        out_shape=[jax.ShapeDtypeStruct(s, F32) for s in shapes],
        name="ssm_params",
    )(row(A_re), row(A_im), ldt, kk, chan_first(B_re), chan_first(B_im))
    J, G8 = SSM_LANE_BLOCKS, LANES // SSM_GROUP_SIZE

    def lanes(t):
        return t.reshape(SUBLANES, depth, J, SSM_BLOCK_STATES).transpose(1, 0, 2, 3)

    pw = jnp.concatenate([lanes(tr), lanes(ti)], axis=-1).reshape(depth, SUBLANES, SSM_STATE_LANES)
    eye = jnp.eye(G8, dtype=F32)

    def bdiag(t):
        t = t.reshape(SSM_GROUP_SIZE, depth, J, G8, SSM_STATE).transpose(1, 2, 3, 0, 4)
        t = t[:, :, :, :, None, :] * eye[None, None, :, None, :, None]
        return t.reshape(depth, J, LANES, SSM_BLOCK_STATES)

    bblk = jnp.concatenate([bdiag(bbr), bdiag(bbi)], axis=-1).astype(BF16)

    def cdiag(t):
        t = t.astype(F32).reshape(depth, J, G8, SSM_GROUP_SIZE, SSM_STATE).transpose(0, 1, 2, 4, 3)
        t = t[:, :, :, :, None, :] * eye[None, None, :, None, :, None]
        return t.reshape(depth, J, SSM_BLOCK_STATES, LANES)

    cblk = jnp.concatenate([cdiag(C_re), -cdiag(C_im)], axis=2).astype(BF16)
    return pw, bblk, cblk


def _ssm_tail(y, u, d_ref, wg_ref, bg_ref, g_ref):
    y = y + d_ref[...] * u
    z = jax.nn.gelu(y)
    gate = jax.nn.sigmoid(jnp.dot(z.astype(BF16), wg_ref[...], preferred_element_type=F32) + bg_ref[...])
    return _rms(z * gate, g_ref[...])


def _ssm_sweep(x_scr, n, a_tabs, init, store):
    S = SSM_BLOCK_STATES
    fins = []
    for j0 in range(0, SSM_LANE_BLOCKS, 2):
        js = (j0, j0 + 1)

        def body(k, carry, js=js):
            r0 = pl.multiple_of(k * SUBLANES, SUBLANES)
            out = []
            for idx, j in enumerate(js):
                hr, hi = carry[2 * idx], carry[2 * idx + 1]
                base = j * 2 * S
                ar, ai = a_tabs[j]
                nhr = ar * hr - ai * hi + x_scr[pl.ds(r0, SUBLANES), base:base + S]
                nhi = ar * hi + ai * hr + x_scr[pl.ds(r0, SUBLANES), base + S:base + 2 * S]
                if store:
                    x_scr[pl.ds(r0, SUBLANES), base:base + S] = nhr
                    x_scr[pl.ds(r0, SUBLANES), base + S:base + 2 * S] = nhi
                out += [nhr, nhi]
            return tuple(out)

        c0 = tuple(t for j in js for t in init[j])
        res = jax.lax.fori_loop(0, n // SUBLANES, body, c0, unroll=True)
        fins += [(res[0], res[1]), (res[2], res[3])]
    return fins


def _ssm_rows(u, pow_row, pw_ref, bblk_ref, cblk_ref, x_scr, s_scr, carry_scr):
    n = u.shape[0]
    S = SSM_BLOCK_STATES
    ub = u.astype(BF16)
    for j in range(SSM_LANE_BLOCKS):
        x_scr[0:n, j * 2 * S:(j + 1) * 2 * S] = jnp.dot(ub[:, j * LANES:(j + 1) * LANES], bblk_ref[j],
                                                       preferred_element_type=F32)
    bc = lambda t: jnp.broadcast_to(t, (SUBLANES, S))
    re = lambda ref, r0, r1, j: ref[r0:r1, j * 2 * S:j * 2 * S + S]
    im = lambda ref, r0, r1, j: ref[r0:r1, j * 2 * S + S:(j + 1) * 2 * S]
    a_tabs = [(bc(re(pw_ref, 0, 1, j)), bc(im(pw_ref, 0, 1, j))) for j in range(SSM_LANE_BLOCKS)]
    zero = jnp.zeros((SUBLANES, S), F32)
    fins = _ssm_sweep(x_scr, n, a_tabs, [(zero, zero)] * SSM_LANE_BLOCKS, store=False)
    for j in range(SSM_LANE_BLOCKS):
        base = j * 2 * S
        cr, ci = re(pw_ref, pow_row, pow_row + 1, j), im(pw_ref, pow_row, pow_row + 1, j)
        sr, si = re(carry_scr, 0, 1, j), im(carry_scr, 0, 1, j)
        fr, fi = fins[j]
        for c in range(SUBLANES):
            s_scr[c:c + 1, base:base + S] = sr
            s_scr[c:c + 1, base + S:base + 2 * S] = si
            sr, si = cr * sr - ci * si + fr[c:c + 1], cr * si + ci * sr + fi[c:c + 1]
        carry_scr[:, base:base + S] = bc(sr)
        carry_scr[:, base + S:base + 2 * S] = bc(si)
    init = [(re(s_scr, 0, SUBLANES, j), im(s_scr, 0, SUBLANES, j)) for j in range(SSM_LANE_BLOCKS)]
    _ssm_sweep(x_scr, n, a_tabs, init, store=True)
    ys = [jnp.dot(x_scr[0:n, j * 2 * S:(j + 1) * 2 * S].astype(BF16), cblk_ref[j], preferred_element_type=F32)
          for j in range(SSM_LANE_BLOCKS)]
    return jnp.concatenate(ys, axis=1)


def _ssm_tile(u_ref, n, pow_row, seq_start, refs, o_ref, scr):
    pw_ref, bblk_ref, cblk_ref, d_ref, wg_ref, bg_ref, g_ref = refs
    up_scr, x_scr, s_scr, carry_scr, o_scr = scr
    q = n // SUBLANES
    for j in range(SSM_LANE_BLOCKS):
        for k in range(q):
            up_scr[k * SUBLANES:(k + 1) * SUBLANES, j * LANES:(j + 1) * LANES] = \
                u_ref[j, pl.ds(k, SUBLANES, stride=q), :]
    u = up_scr[0:n, :]
    if seq_start:
        p = jax.lax.broadcasted_iota(jnp.int32, (n, 1), 0)
        u = jnp.where((p % SUBLANES) * q + p // SUBLANES >= seq_start, u, 0.0)
    y = _ssm_rows(u, pow_row, pw_ref, bblk_ref, cblk_ref, x_scr, s_scr, carry_scr)
    out = _ssm_tail(y, u, d_ref, wg_ref, bg_ref, g_ref)
    for j in range(SSM_LANE_BLOCKS):
        for k in range(q):
            o_scr[j, pl.ds(k, SUBLANES, stride=q), :] = out[k * SUBLANES:(k + 1) * SUBLANES,
                                                            j * LANES:(j + 1) * LANES]
    o_ref[...] = jnp.concatenate([o_scr[j, 0:n, :] for j in range(SSM_LANE_BLOCKS)], axis=1).astype(o_ref.dtype)


def _mixers_kernel(sink_ref,
                   qm_ref, qh_ref, kpm_ref, km_ref, kh_ref, vpm_ref, vm_ref, vh_ref, ga_ref, bias_ref,
                   um_ref, uh_ref, pw_ref, bblk_ref, cblk_ref, d_ref, wg_ref, bg_ref, gs_ref,
                   xm_ref, halo_ref, xh_ref, wp_ref, sc_ref, gp_ref, *rest, l, n_cast):
    cast_in = rest[0:n_cast]
    am_ref, ah_ref, som_ref, soh_ref, st_ref, pom_ref, poh_ref = rest[n_cast:n_cast + 7]
    cast_out = rest[n_cast + 7:2 * n_cast + 7]
    scr = rest[2 * n_cast + 7:]
    carry_scr = scr[3]
    t = pl.program_id(1)
    tm = xm_ref.shape[0]
    ssm_refs = (pw_ref, bblk_ref, cblk_ref, d_ref, wg_ref, bg_ref, gs_ref)
    blk = functools.partial(_attn_block, sink_ref=sink_ref, l=l, g=ga_ref[...])

    @pl.when(t == 0)
    def _():
        a = blk(qh_ref[...], kpm_ref[...], kh_ref[...], vpm_ref[...], vh_ref[...], bias_ref[0])
        ah_ref[...] = a.astype(ah_ref.dtype)
        carry_scr[...] = jnp.zeros_like(carry_scr)
        _ssm_tile(uh_ref, HEAD_ROWS, POW_ROW_HEAD, SEQ0, ssm_refs, soh_ref, scr)
        hrow = jax.lax.broadcasted_iota(jnp.int32, (HEAD_ROWS, 1), 0)
        x = jnp.where(hrow >= SEQ0, xh_ref[...], 0.0)
        prev = jnp.zeros((POOL_HALO, POOL_WIDTH), F32)
        poh_ref[...] = _pool_rows(x, prev, -SEQ0, wp_ref, sc_ref, gp_ref).astype(poh_ref.dtype)

    @pl.when(t > 0)
    def _():
        for src, dst in zip(cast_in, cast_out):
            dst[...] = src[...].astype(BF16)
        kp = jnp.where(t == 1, kh_ref[...], kpm_ref[...])
        vp = jnp.where(t == 1, vh_ref[...], vpm_ref[...])
        for n in range(tm // BLOCK):
            rows = slice(n * BLOCK, (n + 1) * BLOCK)
            kc, vc = km_ref[rows, :], vm_ref[rows, :]
            bias = bias_ref[jnp.minimum(t, 2)] if n == 0 else bias_ref[2]
            am_ref[rows, :] = blk(qm_ref[rows, :], kp, kc, vp, vc, bias).astype(am_ref.dtype)
            kp, vp = kc, vc
        _ssm_tile(um_ref, um_ref.shape[1], POW_ROW_MAIN, 0, ssm_refs, som_ref, scr)
        st_ref[...] = carry_scr[...]
        prev = jnp.where(t == 1, xh_ref[HEAD_ROWS - POOL_HALO:, :], halo_ref[...])
        pom_ref[...] = _pool_rows(xm_ref[...], prev, N_META + (t - 1) * tm, wp_ref, sc_ref, gp_ref).astype(pom_ref.dtype)


def _mixers(sinks, qkv_m, qkv_h, um, uh, xm, xh, vecs, bias, pw, bblk, cblk, wg, wp, l, B, tm, casts):
    assert (1, HEAD_ROWS // SUBLANES, tm // SUBLANES) == SSM_POW_ROWS
    J = SSM_LANE_BLOCKS
    rm, rh = um.shape[1], uh.shape[1]
    nt = rm // (B * tm)
    r = tm // POOL_HALO
    bpt = tm // BLOCK
    tile = lambda b, t: b * nt + jnp.maximum(t - 1, 0)
    main = lambda b, t, _: (tile(b, t), 0)
    head = lambda b, t, _: (b, 0)
    halo = lambda b, t, _: (jnp.maximum((b * nt + t - 1) * r - 1, 0), 0)
    prevb = lambda b, t, _: (jnp.maximum(tile(b, t) * bpt - 1, 0), 0)
    cast_specs = []
    for w, wl in casts:
        _, cr, cc = w.shape
        rows = cr // (B * nt)
        cast_specs.append((pl.BlockSpec((None, rows, cc), lambda b, t, _, wl=wl: (wl, tile(b, t), 0)),
                           pl.BlockSpec((rows, cc), main), jax.ShapeDtypeStruct((cr, cc), BF16)))
    bf = lambda n, w: jax.ShapeDtypeStruct((n, w), BF16)
    tile_spec = lambda w: pl.BlockSpec((tm, w), main)
    head_spec = lambda w: pl.BlockSpec((HEAD_ROWS, w), head)
    in_specs = [tile_spec(ATTN_WIDTH), head_spec(ATTN_WIDTH),
                pl.BlockSpec((BLOCK, KV_WIDTH), prevb), tile_spec(KV_WIDTH), head_spec(KV_WIDTH),
                pl.BlockSpec((BLOCK, KV_WIDTH), prevb), tile_spec(KV_WIDTH), head_spec(KV_WIDTH),
                _vrow("g_out_attn", l), _full(bias.shape),
                pl.BlockSpec((J, tm, LANES), lambda b, t, _: (0, tile(b, t), 0)),
                pl.BlockSpec((J, HEAD_ROWS, LANES), lambda b, t, _: (0, b, 0)),
                _layer(pw.shape[1:], l), _layer(bblk.shape[1:], l), _layer(cblk.shape[1:], l),
                _vrow("D_skip", l), _layer((SSM_WIDTH, SSM_WIDTH), l), _vrow("b_glu", l), _vrow("g_out_ssm", l),
                tile_spec(POOL_WIDTH), pl.BlockSpec((POOL_HALO, POOL_WIDTH), halo), head_spec(POOL_WIDTH),
                _layer(wp.shape[1:], l), _vrow("pool_scale", l), _vrow("g_out_pool", l)]
    in_specs += [c[0] for c in cast_specs]
    out_specs = [tile_spec(ATTN_WIDTH), head_spec(ATTN_WIDTH), tile_spec(SSM_WIDTH), head_spec(SSM_WIDTH),
                 pl.BlockSpec((None, SUBLANES, SSM_STATE_LANES), lambda b, t, _: (b, 0, 0)),
                 tile_spec(POOL_WIDTH), head_spec(POOL_WIDTH)] + [c[1] for c in cast_specs]
    out_shape = [bf(rm, ATTN_WIDTH), bf(rh, ATTN_WIDTH), bf(rm, SSM_WIDTH), bf(rh, SSM_WIDTH),
                 jax.ShapeDtypeStruct((B, SUBLANES, SSM_STATE_LANES), F32),
                 bf(rm, POOL_WIDTH), bf(rh, POOL_WIDTH)] + [c[2] for c in cast_specs]
    (qm, km, vm), (qh, kh, vh) = qkv_m, qkv_h
    res = pl.pallas_call(
        functools.partial(_mixers_kernel, l=l, n_cast=len(casts)),
        grid_spec=pltpu.PrefetchScalarGridSpec(
            num_scalar_prefetch=1, grid=(B, nt + 1), in_specs=in_specs, out_specs=out_specs,
            scratch_shapes=[pltpu.VMEM((tm, SSM_WIDTH), F32),
                            pltpu.VMEM((tm, SSM_STATE_LANES), F32),
                            pltpu.VMEM((SUBLANES, SSM_STATE_LANES), F32),
                            pltpu.VMEM((SUBLANES, SSM_STATE_LANES), F32),
                            pltpu.VMEM((J, tm, LANES), F32)]),
        out_shape=out_shape,
        compiler_params=_params("arbitrary", "arbitrary"),
        name="mixers",
    )(sinks, qm, qh, km, km, kh, vm, vm, vh, vecs, bias, um, uh, pw, bblk, cblk, vecs, wg, vecs, vecs,
      xm, xm, xh, wp, vecs, vecs, *[w for w, _ in casts])
    return res[0:7], res[7:]


def _state_from_lanes(s):
    s = s.reshape(s.shape[0], SSM_LANE_BLOCKS, 2, SSM_BLOCK_STATES)
    return (s[:, :, 0].reshape(-1, SSM_GROUPS, SSM_STATE), s[:, :, 1].reshape(-1, SSM_GROUPS, SSM_STATE))


def _pool_tail(d_groups, w_ref, sc_ref, g_ref):
    y = jnp.concatenate(
        [jnp.dot(d.astype(BF16), w_ref[gi], preferred_element_type=F32) for gi, d in enumerate(d_groups)], axis=1)
    return _rms(y * sc_ref[...], g_ref[...])


def _pool_rows(x, prev, pos0, w_ref, sc_ref, g_ref):
    n = x.shape[0]
    xe = jnp.concatenate([prev, x], axis=0)
    pos = pos0 + jax.lax.broadcasted_iota(jnp.int32, (n, 1), 0)
    ds = []
    for gi, w in enumerate(POOL_WINDOWS):
        gsl = slice(gi * POOL_GROUP, (gi + 1) * POOL_GROUP)
        s = xe[:, gsl]
        k = 1
        while k < w:
            s = s + pltpu.roll(s, k, 0)
            k *= 2
        cnt = jnp.clip(pos + 1, 1, w).astype(F32)
        ds.append(s[POOL_HALO:] / cnt - x[:, gsl])
    return _pool_tail(ds, w_ref, sc_ref, g_ref)


def _shift_caches_kernel(k_ref, v_ref, nk_ref, nv_ref):
    for src, dst in ((k_ref, nk_ref), (v_ref, nv_ref)):
        for bb in range(DEC_STEP):
            dst[bb, 0:WINDOW - 1] = src[bb, 1:WINDOW]
            dst[bb, WINDOW - 1] = jnp.zeros((N_KV_HEADS, HEAD_DIM), F32)


def _shift_caches(cache_k, cache_v):
    depth, N = cache_k.shape[:2]
    blk = pl.BlockSpec((None, DEC_STEP, WINDOW, N_KV_HEADS, HEAD_DIM), lambda l, s: (l, s, 0, 0, 0))
    shape = jax.ShapeDtypeStruct(cache_k.shape, F32)
    return pl.pallas_call(
        _shift_caches_kernel,
        grid=(depth, N // DEC_STEP),
        in_specs=[blk, blk],
        out_specs=[blk, blk],
        out_shape=[shape, shape],
        compiler_params=_params("arbitrary", "arbitrary"),
        name="shift_caches",
    )(cache_k, cache_v)


def _attn_sample_kernel(q_ref, kn_ref, vn_ref, kc_ref, vc_ref, sink_ref, g_ref, a_in_ref, nk_in_ref, nv_in_ref,
                        a_ref, nk_ref, nv_ref, acc_scr):
    del a_in_ref, nk_in_ref, nv_in_ref
    step = pl.program_id(0)
    scale = HEAD_DIM ** -0.5
    for bb in range(DEC_STEP):
        outs = []
        for kh in range(N_KV_HEADS):
            ksl = slice(kh * HEAD_DIM, (kh + 1) * HEAD_DIM)
            qh = jnp.concatenate(
                [q_ref[bb:bb + 1, (kh * GQA_GROUP + g) * HEAD_DIM:(kh * GQA_GROUP + g + 1) * HEAD_DIM]
                 for g in range(GQA_GROUP)], axis=0)
            kn = kn_ref[bb:bb + 1, ksl]
            vn = vn_ref[bb:bb + 1, ksl]
            nk_ref[bb, 0, kh:kh + 1, :] = kn
            nv_ref[bb, 0, kh:kh + 1, :] = vn
            kc = kc_ref[bb, :, kh, :]
            vc = vc_ref[bb, :, kh, :]
            sc = jax.lax.dot_general(qh.astype(BF16), kc.astype(BF16), (((1,), (1,)), ((), ())),
                                     preferred_element_type=F32) * scale
            sn = jnp.sum(qh * kn, axis=-1, keepdims=True) * scale
            sk = sink_ref[kh]
            m = jnp.maximum(jnp.maximum(jnp.max(sc, axis=-1, keepdims=True), sn), sk)
            pc = jnp.exp(sc - m)
            pn = jnp.exp(sn - m)
            denom = jnp.sum(pc, axis=-1, keepdims=True) + pn + jnp.exp(sk - m)
            o = jnp.dot(pc.astype(BF16), vc.astype(BF16), preferred_element_type=F32)
            o = (o + pn * vn) / denom
            outs.extend(o[g:g + 1] for g in range(GQA_GROUP))
        a = jnp.concatenate(outs, axis=1)
        acc_scr[pl.ds(step * DEC_STEP + bb, 1), :] = _rms(a, g_ref[...])

    @pl.when(step == pl.num_programs(0) - 1)
    def _():
        a_ref[...] = acc_scr[...].astype(a_ref.dtype)


def _attn_sample(qh, kh, vh, cache_k, cache_v, l, sinks, g, ah, nk, nv, N):
    rows = lambda w: pl.BlockSpec((DEC_STEP, w), lambda s: (s, 0))
    cache = pl.BlockSpec((None, DEC_STEP, WINDOW, N_KV_HEADS, HEAD_DIM), lambda s: (l, s, 0, 0, 0))
    last = pl.BlockSpec((None, DEC_STEP, 1, N_KV_HEADS, HEAD_DIM), lambda s: (l, s, WINDOW - 1, 0, 0))
    anyspec = pl.BlockSpec(memory_space=pl.ANY)
    return pl.pallas_call(
        _attn_sample_kernel,
        grid=(N // DEC_STEP,),
        in_specs=[rows(ATTN_WIDTH), rows(KV_WIDTH), rows(KV_WIDTH), cache, cache,
                  _layer((N_KV_HEADS, GQA_GROUP, 1), l), _vrow("g_out_attn", l),
                  anyspec, anyspec, anyspec],
        out_specs=[pl.BlockSpec((N, ATTN_WIDTH), lambda s: (0, 0)), last, last],
        out_shape=[jax.ShapeDtypeStruct(ah.shape, ah.dtype), jax.ShapeDtypeStruct(nk.shape, nk.dtype),
                   jax.ShapeDtypeStruct(nv.shape, nv.dtype)],
        scratch_shapes=[pltpu.VMEM((N, ATTN_WIDTH), F32)],
        input_output_aliases={7: 0, 8: 1, 9: 2},
        compiler_params=_params("arbitrary"),
        name="attn_sample",
    )(qh, kh, vh, cache_k, cache_v, sinks, g, ah, nk, nv)


def _mix_sample_kernel(u_ref, h0_ref, pw_ref, bblk_ref, cblk_ref, d_ref, wg_ref, bg_ref, gs_ref,
                       xp_ref, pb_ref, wp_ref, sc_ref, gp_ref, s_in_ref, p_in_ref, s_ref, st_ref, p_ref):
    del s_in_ref, p_in_ref
    S = SSM_BLOCK_STATES
    u = jnp.concatenate([u_ref[j] for j in range(SSM_LANE_BLOCKS)], axis=1)
    ub = u.astype(BF16)
    ys = []
    for j in range(SSM_LANE_BLOCKS):
        x = jnp.dot(ub[:, j * LANES:(j + 1) * LANES], bblk_ref[j], preferred_element_type=F32)
        base = j * 2 * S
        ar = pw_ref[0:1, base:base + S]
        ai = pw_ref[0:1, base + S:base + 2 * S]
        h0r = h0_ref[:, base:base + S]
        h0i = h0_ref[:, base + S:base + 2 * S]
        hr = x[:, 0:S] + ar * h0r - ai * h0i
        hi = x[:, S:] + ar * h0i + ai * h0r
        st_ref[:, base:base + S] = hr
        st_ref[:, base + S:base + 2 * S] = hi
        h = jnp.concatenate([hr, hi], axis=1).astype(BF16)
        ys.append(jnp.dot(h, cblk_ref[j], preferred_element_type=F32))
    s_ref[...] = _ssm_tail(jnp.concatenate(ys, axis=1), u, d_ref, wg_ref, bg_ref, gs_ref).astype(s_ref.dtype)

    xp = xp_ref[...]
    ds = []
    for gi, w in enumerate(POOL_WINDOWS):
        gsl = slice(gi * POOL_GROUP, (gi + 1) * POOL_GROUP)
        s = xp[:, gsl]
        for back in range(1, w):
            s = s + pb_ref[POOL_BUF - back][:, gsl]
        ds.append(s / float(w) - xp[:, gsl])
    p_ref[...] = _pool_tail(ds, wp_ref, sc_ref, gp_ref).astype(p_ref.dtype)


def _mix_sample(uh, xph, h0, pbuf, vecs, pw, bblk, cblk, wg, wp, l, sh, ph):
    N = h0.shape[1]
    rows = lambda w: pl.BlockSpec((N, w), lambda i: (0, 0))
    anyspec = pl.BlockSpec(memory_space=pl.ANY)
    return pl.pallas_call(
        _mix_sample_kernel,
        grid=(1,),
        in_specs=[pl.BlockSpec((SSM_LANE_BLOCKS, N, LANES), lambda i: (0, 0, 0)),
                  _layer(h0.shape[1:], l), _layer(pw.shape[1:], l), _layer(bblk.shape[1:], l),
                  _layer(cblk.shape[1:], l),
                  _vrow("D_skip", l), _layer((SSM_WIDTH, SSM_WIDTH), l), _vrow("b_glu", l),
                  _vrow("g_out_ssm", l), rows(POOL_WIDTH), _layer(pbuf.shape[1:], l),
                  _layer(wp.shape[1:], l),
                  _vrow("pool_scale", l), _vrow("g_out_pool", l), anyspec, anyspec],
        out_specs=[rows(SSM_WIDTH), _full((N, SSM_STATE_LANES)), rows(POOL_WIDTH)],
        out_shape=[jax.ShapeDtypeStruct(sh.shape, sh.dtype), jax.ShapeDtypeStruct((N, SSM_STATE_LANES), F32),
                   jax.ShapeDtypeStruct(ph.shape, ph.dtype)],
        input_output_aliases={14: 0, 15: 2},
        compiler_params=_params("arbitrary"),
        name="mix_sample",
    )(uh, h0, pw, bblk, cblk, vecs, wg, vecs, vecs, xph, pbuf, wp, vecs, vecs, sh, ph)


def _outproj_rows(x_ref, a_ref, s_ref, p_ref, w_ref, o_ref):
    o1 = ATTN_WIDTH
    o2 = o1 + SSM_WIDTH
    acc = x_ref[...]
    acc = acc + jnp.dot(a_ref[...], w_ref[0:o1, :], preferred_element_type=F32)
    acc = acc + jnp.dot(s_ref[...], w_ref[o1:o2, :], preferred_element_type=F32)
    acc = acc + jnp.dot(p_ref[...], w_ref[o2:, :], preferred_element_type=F32)
    o_ref[...] = acc


def _outproj_kernel(xm_ref, am_ref, sm_ref, pm_ref, xh_ref, ah_ref, sh_ref, ph_ref, w_ref, om_ref, oh_ref):
    i = pl.program_id(0)
    last = pl.num_programs(0) - 1

    @pl.when(i < last)
    def _():
        _outproj_rows(xm_ref, am_ref, sm_ref, pm_ref, w_ref, om_ref)

    @pl.when(i == last)
    def _():
        _outproj_rows(xh_ref, ah_ref, sh_ref, ph_ref, w_ref, oh_ref)


def _outproj(xm, am, sm, pm, xh, ah, sh, ph, w, tm):
    R, RH = xm.shape[0], xh.shape[0]
    nm = R // tm
    row = lambda i: (jnp.minimum(i, nm - 1), 0)
    once = pl.Buffered(1)
    head = lambda w_: pl.BlockSpec((RH, w_), lambda i: (0, 0), pipeline_mode=once)
    return pl.pallas_call(
        _outproj_kernel,
        grid=(nm + 1,),
        in_specs=[pl.BlockSpec((tm, D_MODEL), row), pl.BlockSpec((tm, ATTN_WIDTH), row),
                  pl.BlockSpec((tm, SSM_WIDTH), row), pl.BlockSpec((tm, POOL_WIDTH), row),
                  head(D_MODEL), head(ATTN_WIDTH), head(SSM_WIDTH), head(POOL_WIDTH),
                  pl.BlockSpec((D_MODEL, D_MODEL), lambda i: (0, 0), pipeline_mode=once)],
        out_specs=[pl.BlockSpec((tm, D_MODEL), row), pl.BlockSpec((RH, D_MODEL), lambda i: (0, 0))],
        out_shape=[jax.ShapeDtypeStruct((R, D_MODEL), F32), jax.ShapeDtypeStruct((RH, D_MODEL), F32)],
        compiler_params=_params("arbitrary"),
        name="outproj",
    )(xm, am, sm, pm, xh, ah, sh, ph, w)


def _ffn_kernel(xm_ref, xh_ref, g_ref, w1_ref, w2_ref, om_ref, oh_ref, hm_scr, hh_scr):
    i = pl.program_id(0)
    f = pl.program_id(1)

    def mlp(h):
        h1 = jnp.dot(h, w1_ref[...], preferred_element_type=F32)
        return jnp.dot(jnp.square(jnp.maximum(h1, 0.0)).astype(BF16), w2_ref[...], preferred_element_type=F32)

    def first(x_ref, h_scr, o_ref):
        x = x_ref[...]
        h = _rms(x, g_ref[...]).astype(BF16)
        h_scr[...] = h
        o_ref[...] = x + mlp(h)

    @pl.when(f == 0)
    def _():
        first(xm_ref, hm_scr, om_ref)

    @pl.when(f > 0)
    def _():
        om_ref[...] += mlp(hm_scr[...])

    @pl.when((i == 0) & (f == 0))
    def _():
        first(xh_ref, hh_scr, oh_ref)

    @pl.when((i == 0) & (f > 0))
    def _():
        oh_ref[...] += mlp(hh_scr[...])


def _ffn(xm, xh, g, l, w1, w2, tm, tf):
    R, RH = xm.shape[0], xh.shape[0]
    return pl.pallas_call(
        _ffn_kernel,
        grid=(R // tm, D_FF // tf),
        in_specs=[pl.BlockSpec((tm, D_MODEL), lambda i, f: (i, 0)),
                  pl.BlockSpec((RH, D_MODEL), lambda i, f: (0, 0), pipeline_mode=pl.Buffered(1)),
                  _vrow("g_ffn", l),
                  pl.BlockSpec((D_MODEL, tf), lambda i, f: (0, f)),
                  pl.BlockSpec((tf, D_MODEL), lambda i, f: (f, 0))],
        out_specs=[pl.BlockSpec((tm, D_MODEL), lambda i, f: (i, 0)),
                   pl.BlockSpec((RH, D_MODEL), lambda i, f: (0, 0))],
        out_shape=[jax.ShapeDtypeStruct((R, D_MODEL), F32), jax.ShapeDtypeStruct((RH, D_MODEL), F32)],
        scratch_shapes=[pltpu.VMEM((tm, D_MODEL), BF16), pltpu.VMEM((RH, D_MODEL), BF16)],
        compiler_params=_params("arbitrary", "arbitrary"),
        name="ffn",
    )(xm, xh, g, w1, w2)


def _rope_tables(pos):
    half = ROT_HALF
    inv = ROPE_THETA ** (-np.arange(0, ROT_DIM, 2, dtype=np.float64) / ROT_DIM)
    ang = np.asarray(pos, np.float64)[:, None] * inv
    cos, sin = np.cos(ang), np.sin(ang)
    n = ang.shape[0]
    z = np.zeros((n, HEAD_DIM - ROT_DIM))
    zh = np.zeros((n, half))
    rc = np.concatenate([cos, cos, z + 1.0], axis=1)
    rs1 = np.concatenate([zh, sin, z], axis=1)
    rs2 = np.concatenate([-sin, zh, z], axis=1)
    return tuple(t.astype(np.float32) for t in (rc, rs1, rs2))


def kernel(x_prompt, x_sample, cache_k, cache_v, state_ssm_re, state_ssm_im, state_pool, meta_tokens, g_mix, w_in, g_q, g_k, sinks, A_re, A_im, log_dt, B_re, B_im, C_re, C_im, D_skip, w_glu, b_glu, w_pool, pool_scale, g_out_attn, g_out_ssm, g_out_pool, w_out, g_ffn, w_ff1, w_ff2):
    B, T, _ = x_prompt.shape
    N = x_sample.shape[0]
    depth = w_in.shape[0]
    assert N <= SEQ0 and N % DEC_STEP == 0 and T % TM_FFN == 0
    meta = meta_tokens.astype(F32)
    head0 = jnp.concatenate([x_sample.reshape(N, D_MODEL), jnp.zeros((SEQ0 - N, D_MODEL), F32), meta], axis=0)
    head_rest = jnp.concatenate([jnp.zeros((SEQ0, D_MODEL), F32), meta], axis=0)
    xh = jnp.concatenate([head0] + [head_rest] * (B - 1), axis=0)
    xm = x_prompt.reshape(B * T, D_MODEL)

    rope_m = _rope_tables(N_META + np.arange(T))
    hr_ = np.arange(HEAD_ROWS)
    pos_h0 = np.where(hr_ < N, PAST_LEN, np.maximum(hr_ - SEQ0, 0))
    pos_h = np.concatenate([pos_h0] + [np.maximum(hr_ - SEQ0, 0)] * (B - 1))
    rope_h = _rope_tables(pos_h)

    wi = w_in[0].astype(BF16)
    wg_all, wp_all = w_glu.astype(BF16), w_pool.astype(BF16)
    ck = cache_k.astype(F32)
    cv = cache_v.astype(F32)
    RH = B * HEAD_ROWS

    named = dict(g_mix=g_mix, g_ffn=g_ffn, g_out_attn=g_out_attn, D_skip=D_skip, b_glu=b_glu,
                 g_out_ssm=g_out_ssm, pool_scale=pool_scale, g_out_pool=g_out_pool, g_q=g_q, g_k=g_k)
    vecs = jnp.concatenate([named[name].astype(F32) for name, _ in VEC_LAYOUT], axis=1)[:, None, :]
    pw, bblk, cblk = _ssm_params(A_re, A_im, log_dt, B_re, B_im, C_re, C_im)
    sinks_flat = sinks.astype(F32).reshape(depth * N_HEADS)
    sinks_col = sinks.astype(F32).reshape(depth, N_KV_HEADS, GQA_GROUP, 1)
    bias = _attn_bias()
    h0 = jnp.concatenate([state_ssm_re.astype(F32).reshape(depth, N, SSM_LANE_BLOCKS, SSM_BLOCK_STATES),
                          state_ssm_im.astype(F32).reshape(depth, N, SSM_LANE_BLOCKS, SSM_BLOCK_STATES)],
                         axis=-1).reshape(depth, N, SSM_STATE_LANES)
    pbuf = state_pool.astype(F32).transpose(0, 2, 1, 3)

    nk, nv = _shift_caches(ck, cv)
    ks, vs, pls, sts, st_ss, phs = ([] for _ in range(6))
    for l in range(depth):
        (qm, km, vm, um, pm), (qh, kh, vh, uh, ph), w1 = _inproj(xm, xh, vecs, wi, l, rope_m, rope_h, TM_PROJ, w_ff1)

        casts = [(w_ff2, l), (w_out, l)] + ([(w_in, l + 1)] if l + 1 < depth else [])
        (am, ah, sm, sh, st, plm, plh), wcast = _mixers(
            sinks_flat, (qm, km, vm), (qh, kh, vh), um, uh, pm, ph, vecs, bias, pw, bblk, cblk, wg_all, wp_all,
            l, B, TM_SEQ, casts)
        w2, wo = wcast[0], wcast[1]
        if l + 1 < depth:
            wi = wcast[2]

        ah, nk, nv = _attn_sample(qh, kh, vh, ck, cv, l, sinks_col, vecs, ah, nk, nv, N)
        sh, st_s, plh = _mix_sample(uh, ph, h0, pbuf, vecs, pw, bblk, cblk, wg_all, wp_all, l, sh, plh)

        xm, xh = _outproj(xm, am, sm, plm, xh, ah, sh, plh, wo, TM_PROJ)
        xm, xh = _ffn(xm, xh, vecs, l, w1, w2, TM_FFN, TF_FFN)

        ks.append(km.reshape(B, T, KV_WIDTH)[:, T - WINDOW:])
        vs.append(vm.reshape(B, T, KV_WIDTH)[:, T - WINDOW:])
        pls.append(pm.reshape(B, T, POOL_WIDTH)[:, T - POOL_BUF:])
        sts.append(st[:, 0])
        st_ss.append(st_s)
        phs.append(ph[:N])

    y_prompt = xm.reshape(B, T, D_MODEL)
    y_sample = xh[:N].reshape(N, 1, D_MODEL)
    heads = lambda t: jnp.stack(t).reshape(depth, -1, WINDOW, N_KV_HEADS, HEAD_DIM)
    p_re, p_im = _state_from_lanes(jnp.stack(sts).reshape(depth * B, SSM_STATE_LANES))
    s_re, s_im = _state_from_lanes(jnp.stack(st_ss).reshape(depth * N, SSM_STATE_LANES))
    st4 = lambda t, n: t.reshape(depth, n, SSM_GROUPS, SSM_STATE)
    s_pool = jnp.concatenate([state_pool.astype(F32)[:, :, 1:], jnp.stack(phs)[:, :, None]], axis=2)
    return (y_prompt, y_sample, heads(ks), heads(vs), st4(p_re, B), st4(p_im, B), jnp.stack(pls),
            nk, nv, st4(s_re, N), st4(s_im, N), s_pool)
```

```python
import functools
import math

import jax
import jax.numpy as jnp
import numpy as np
from jax.experimental import pallas as pl
from jax.experimental.pallas import tpu as pltpu

D_MODEL = 2048
N_META = 16
HEAD_DIM = 128
N_HEADS = 8
N_KV_HEADS = 2
GQA_GROUP = 4
ATTN_WIDTH = 1024
KV_WIDTH = 256
WINDOW = 128
BLOCK = 128
ROT_DIM = 32
ROT_HALF = ROT_DIM // 2
ROPE_THETA = 500000.0
SSM_WIDTH = 512
SSM_GROUP_SIZE = 16
SSM_GROUPS = 32
SSM_STATE = 64
POOL_WIDTH = 512
POOL_WINDOWS = (2, 4, 8, 16)
POOL_GROUP = 128
POOL_BUF = 15
POOL_HALO = 16
IN_WIDTH = 2560
D_FF = 8192
EPS = 1e-6
PAST_LEN = 16384
LOG2E = math.log2(math.e)

HEAD_ROWS = BLOCK
SEQ0 = HEAD_ROWS - N_META
LANES = 128
SUBLANES = 8
SSM_LANE_BLOCKS = SSM_WIDTH // LANES
SSM_BLOCK_STATES = (LANES // SSM_GROUP_SIZE) * SSM_STATE
SSM_STATE_LANES = SSM_LANE_BLOCKS * 2 * SSM_BLOCK_STATES
VMEM_LIMIT = 60 * 1024 * 1024

TM_PROJ = 512
TM_FFN = 1024
TF_FFN = 512
TM_SEQ = 512
DEC_STEP = 8
SSM_POW_ROWS = (1, HEAD_ROWS // SUBLANES, TM_SEQ // SUBLANES)
POW_ROW_HEAD, POW_ROW_MAIN = 1, 2

BF16 = jnp.bfloat16
F32 = jnp.float32


def _params(*semantics):
    return pltpu.CompilerParams(dimension_semantics=semantics, vmem_limit_bytes=VMEM_LIMIT)


def _rms(x, g):
    return x * jax.lax.rsqrt(jnp.mean(x * x, axis=-1, keepdims=True) + EPS) * g


def _full(shape):
    n = len(shape)
    return pl.BlockSpec(shape, lambda *_: (0,) * n)


def _layer(shape, l):
    n = len(shape)
    return pl.BlockSpec((None, *shape), lambda *_: (l,) + (0,) * n)


VEC_LAYOUT = (("g_mix", D_MODEL), ("g_ffn", D_MODEL), ("g_out_attn", ATTN_WIDTH), ("D_skip", SSM_WIDTH),
              ("b_glu", SSM_WIDTH), ("g_out_ssm", SSM_WIDTH), ("pool_scale", POOL_WIDTH),
              ("g_out_pool", POOL_WIDTH), ("g_q", HEAD_DIM), ("g_k", HEAD_DIM))
VEC_WIDTH = dict(VEC_LAYOUT)
VEC_OFFSET = {name: sum(w for _, w in VEC_LAYOUT[:i]) for i, (name, _) in enumerate(VEC_LAYOUT)}
assert all(VEC_OFFSET[name] % w == 0 for name, w in VEC_LAYOUT)


def _vrow(name, l):
    w = VEC_WIDTH[name]
    return pl.BlockSpec((None, 1, w), lambda *_: (l, 0, VEC_OFFSET[name] // w))


def _inproj_rows(x, g_ref, w_ref, gq_ref, gk_ref, rope_refs, out_refs):
    q_ref, k_ref, v_ref, u_ref, xp_ref = out_refs
    h = _rms(x, g_ref[...]).astype(BF16)
    proj = jnp.dot(h, w_ref[...], preferred_element_type=F32)
    rc, rs1, rs2 = (r[...] for r in rope_refs)

    def head(t, g):
        t = _rms(t, g)
        return t * rc + pltpu.roll(t, ROT_HALF, 1) * rs1 + pltpu.roll(t, LANES - ROT_HALF, 1) * rs2

    for hd in range(N_HEADS):
        sl = slice(hd * HEAD_DIM, (hd + 1) * HEAD_DIM)
        q_ref[:, sl] = head(proj[:, sl], gq_ref[...])
    for hd in range(N_KV_HEADS):
        sl = slice(hd * HEAD_DIM, (hd + 1) * HEAD_DIM)
        k_ref[:, sl] = head(proj[:, ATTN_WIDTH + hd * HEAD_DIM:ATTN_WIDTH + (hd + 1) * HEAD_DIM], gk_ref[...])
    o2 = ATTN_WIDTH + KV_WIDTH
    o3 = o2 + KV_WIDTH
    o4 = o3 + SSM_WIDTH
    v_ref[...] = proj[:, o2:o3]
    for j in range(SSM_LANE_BLOCKS):
        u_ref[j] = proj[:, o3 + j * LANES:o3 + (j + 1) * LANES]
    xp_ref[...] = proj[:, o4:]


def _inproj_kernel(xm_ref, xh_ref, g_ref, w_ref, gq_ref, gk_ref, rcm_ref, rs1m_ref, rs2m_ref,
                   rch_ref, rs1h_ref, rs2h_ref, *rest):
    cast = len(rest) == 12
    outs = rest[1:] if cast else rest
    main_outs, head_outs = outs[0:5], outs[5:10]
    i = pl.program_id(0)
    last = pl.num_programs(0) - 1

    @pl.when(i < last)
    def _():
        if cast:
            outs[10][...] = rest[0][...].astype(BF16)
        _inproj_rows(xm_ref[...], g_ref, w_ref, gq_ref, gk_ref, (rcm_ref, rs1m_ref, rs2m_ref), main_outs)

    @pl.when(i == last)
    def _():
        _inproj_rows(xh_ref[...], g_ref, w_ref, gq_ref, gk_ref, (rch_ref, rs1h_ref, rs2h_ref), head_outs)


def _inproj(xm, xh, vecs, w, l, rope_m, rope_h, tm, cast=None):
    R, RH = xm.shape[0], xh.shape[0]
    nm = R // tm
    tiles_per_rope = rope_m[0].shape[0] // tm
    tile = lambda i: jnp.minimum(i, nm - 1)
    row = lambda i: (tile(i), 0)
    rrow = lambda i: (tile(i) % tiles_per_rope, 0)
    once = pl.Buffered(1)
    full = lambda n, w_: pl.BlockSpec((n, w_), lambda i: (0, 0))

    def out_set(n, rows, im, uim):
        flat = lambda w_: (pl.BlockSpec((n, w_), im), jax.ShapeDtypeStruct((rows, w_), F32))
        u_out = (pl.BlockSpec((SSM_LANE_BLOCKS, n, LANES), uim),
                 jax.ShapeDtypeStruct((SSM_LANE_BLOCKS, rows, LANES), F32))
        return [flat(ATTN_WIDTH), flat(KV_WIDTH), flat(KV_WIDTH), u_out, flat(POOL_WIDTH)]

    outs = (out_set(tm, R, row, lambda i: (0, tile(i), 0))
            + out_set(RH, RH, lambda i: (0, 0), lambda i: (0, 0, 0)))
    in_specs = [pl.BlockSpec((tm, D_MODEL), row),
                pl.BlockSpec((RH, D_MODEL), lambda i: (0, 0), pipeline_mode=once),
                _vrow("g_mix", l),
                pl.BlockSpec((D_MODEL, IN_WIDTH), lambda i: (0, 0), pipeline_mode=once),
                _vrow("g_q", l), _vrow("g_k", l),
                pl.BlockSpec((tm, LANES), rrow), pl.BlockSpec((tm, LANES), rrow), pl.BlockSpec((tm, LANES), rrow),
                full(RH, LANES), full(RH, LANES), full(RH, LANES)]
    args = [xm, xh, vecs, w, vecs, vecs, *rope_m, *rope_h]
    if cast is not None:
        _, cr, cc = cast.shape
        crows = cr // nm
        in_specs.append(pl.BlockSpec((None, crows, cc), lambda i: (l, tile(i), 0)))
        args.append(cast)
        outs.append((pl.BlockSpec((crows, cc), row), jax.ShapeDtypeStruct((cr, cc), BF16)))
    res = pl.pallas_call(
        _inproj_kernel,
        grid=(nm + 1,),
        in_specs=in_specs,
        out_specs=[o[0] for o in outs],
        out_shape=[o[1] for o in outs],
        compiler_params=_params("arbitrary"),
        name="inproj",
    )(*args)
    return res[0:5], res[5:10], (res[10] if cast is not None else None)


def _attn_bias():
    rows = GQA_GROUP * BLOCK
    i, r, c = np.meshgrid(np.arange(3), np.arange(rows) % BLOCK, np.arange(2 * BLOCK), indexing="ij")
    diff = BLOCK + r - c
    krow = (i - 1) * BLOCK + c
    mask = (diff >= 0) & (diff <= WINDOW) & (krow >= SEQ0)
    return np.where(mask, 0.0, -np.inf).astype(np.float32)


def _attn_block(q_blk, kp_blk, kc_blk, vp_blk, vc_blk, bias, sink_ref, l, g):
    rows = GQA_GROUP * BLOCK
    rgrp = jax.lax.broadcasted_iota(jnp.int32, (rows, 1), 0) // BLOCK
    outs = []
    for kh in range(N_KV_HEADS):
        ksl = slice(kh * HEAD_DIM, (kh + 1) * HEAD_DIM)
        qh = jnp.concatenate(
            [q_blk[:, (kh * GQA_GROUP + h) * HEAD_DIM:(kh * GQA_GROUP + h + 1) * HEAD_DIM]
             for h in range(GQA_GROUP)], axis=0).astype(BF16)
        kk = jnp.concatenate([kp_blk[:, ksl], kc_blk[:, ksl]], axis=0).astype(BF16)
        vv = jnp.concatenate([vp_blk[:, ksl], vc_blk[:, ksl]], axis=0).astype(BF16)
        s = jax.lax.dot_general(qh, kk, (((1,), (1,)), ((), ())),
                                preferred_element_type=F32) * (HEAD_DIM ** -0.5 * LOG2E) + bias
        sk = jnp.zeros((rows, 1), F32)
        for h in range(GQA_GROUP):
            sk = jnp.where(rgrp == h, sink_ref[l * N_HEADS + kh * GQA_GROUP + h] * LOG2E, sk)
        m = jnp.maximum(jnp.max(s, axis=-1, keepdims=True), sk)
        p = jnp.exp2(s - m)
        denom = jnp.sum(p, axis=-1, keepdims=True) + jnp.exp2(sk - m)
        o = jnp.dot(p.astype(BF16), vv, preferred_element_type=F32) / denom
        outs.extend(o[h * BLOCK:(h + 1) * BLOCK] for h in range(GQA_GROUP))
    return _rms(jnp.concatenate(outs, axis=1), g)


def _ssm_params_kernel(ar_ref, ai_ref, ldt_ref, kk_ref, br_ref, bi_ref, tr_ref, ti_ref, bbr_ref, bbi_ref):
    ar, ai = ar_ref[...], ai_ref[...]
    dt = jnp.exp(ldt_ref[...])
    kk = kk_ref[...]
    mag = jnp.exp(dt * ar * kk)
    ang = dt * ai * kk
    tr = mag * jnp.cos(ang)
    ti = mag * jnp.sin(ang)
    tr_ref[...] = tr
    ti_ref[...] = ti
    abr, abi = tr[0:1], ti[0:1]
    den = ar * ar + ai * ai
    fr = ((abr - 1.0) * ar + abi * ai) / den
    fi = (abi * ar - (abr - 1.0) * ai) / den
    br, bi = br_ref[...], bi_ref[...]
    bbr_ref[...] = fr * br - fi * bi
    bbi_ref[...] = fr * bi + fi * br


def _ssm_params(A_re, A_im, log_dt, B_re, B_im, C_re, C_im):
    depth = A_re.shape[0]
    n = depth * SSM_GROUPS * SSM_STATE
    row = lambda t: t.astype(F32).reshape(1, n)
    ldt = jnp.broadcast_to(log_dt.astype(F32)[:, :, None], (depth, SSM_GROUPS, SSM_STATE)).reshape(1, n)
    kk = jnp.array(SSM_POW_ROWS + (0,) * (SUBLANES - len(SSM_POW_ROWS)), F32).reshape(SUBLANES, 1)
    chan_first = lambda t: t.astype(F32).reshape(n, SSM_GROUP_SIZE).T
    shapes = [(SUBLANES, n), (SUBLANES, n), (SSM_GROUP_SIZE, n), (SSM_GROUP_SIZE, n)]
    tr, ti, bbr, bbi = pl.pallas_call(
        _ssm_params_kernel,
        out_shape=[jax.ShapeDtypeStruct(s, F32) for s in shapes],
        name="ssm_params",
    )(row(A_re), row(A_im), ldt, kk, chan_first(B_re), chan_first(B_im))
    J, G8 = SSM_LANE_BLOCKS, LANES // SSM_GROUP_SIZE

    def lanes(t):
        return t.reshape(SUBLANES, depth, J, SSM_BLOCK_STATES).transpose(1, 0, 2, 3)

    pw = jnp.concatenate([lanes(tr), lanes(ti)], axis=-1).reshape(depth, SUBLANES, SSM_STATE_LANES)
    eye = jnp.eye(G8, dtype=F32)

    def bdiag(t):
        t = t.reshape(SSM_GROUP_SIZE, depth, J, G8, SSM_STATE).transpose(1, 2, 3, 0, 4)
        t = t[:, :, :, :, None, :] * eye[None, None, :, None, :, None]
        return t.reshape(depth, J, LANES, SSM_BLOCK_STATES)

    bblk = jnp.concatenate([bdiag(bbr), bdiag(bbi)], axis=-1).astype(BF16)

    def cdiag(t):
        t = t.astype(F32).reshape(depth, J, G8, SSM_GROUP_SIZE, SSM_STATE).transpose(0, 1, 2, 4, 3)
        t = t[:, :, :, :, None, :] * eye[None, None, :, None, :, None]
        return t.reshape(depth, J, SSM_BLOCK_STATES, LANES)

    cblk = jnp.concatenate([cdiag(C_re), -cdiag(C_im)], axis=2).astype(BF16)
    return pw, bblk, cblk


def _ssm_tail(y, u, d_ref, wg_ref, bg_ref, g_ref):
    y = y + d_ref[...] * u
    z = jax.nn.gelu(y)
    gate = jax.nn.sigmoid(jnp.dot(z.astype(BF16), wg_ref[...], preferred_element_type=F32) + bg_ref[...])
    return _rms(z * gate, g_ref[...])


def _ssm_sweep(x_scr, n, a_tabs, init, store):
    S = SSM_BLOCK_STATES
    fins = []
    for j0 in range(0, SSM_LANE_BLOCKS, 2):
        js = (j0, j0 + 1)

        def body(k, carry, js=js):
            r0 = pl.multiple_of(k * SUBLANES, SUBLANES)
            out = []
            for idx, j in enumerate(js):
                hr, hi = carry[2 * idx], carry[2 * idx + 1]
                base = j * 2 * S
                ar, ai = a_tabs[j]
                nhr = ar * hr - ai * hi + x_scr[pl.ds(r0, SUBLANES), base:base + S]
                nhi = ar * hi + ai * hr + x_scr[pl.ds(r0, SUBLANES), base + S:base + 2 * S]
                if store:
                    x_scr[pl.ds(r0, SUBLANES), base:base + S] = nhr
                    x_scr[pl.ds(r0, SUBLANES), base + S:base + 2 * S] = nhi
                out += [nhr, nhi]
            return tuple(out)

        c0 = tuple(t for j in js for t in init[j])
        res = jax.lax.fori_loop(0, n // SUBLANES, body, c0, unroll=True)
        fins += [(res[0], res[1]), (res[2], res[3])]
    return fins


def _ssm_rows(u, pow_row, pw_ref, bblk_ref, cblk_ref, x_scr, s_scr, carry_scr):
    n = u.shape[0]
    S = SSM_BLOCK_STATES
    ub = u.astype(BF16)
    for j in range(SSM_LANE_BLOCKS):
        x_scr[0:n, j * 2 * S:(j + 1) * 2 * S] = jnp.dot(ub[:, j * LANES:(j + 1) * LANES], bblk_ref[j],
                                                       preferred_element_type=F32)
    bc = lambda t: jnp.broadcast_to(t, (SUBLANES, S))
    re = lambda ref, r0, r1, j: ref[r0:r1, j * 2 * S:j * 2 * S + S]
    im = lambda ref, r0, r1, j: ref[r0:r1, j * 2 * S + S:(j + 1) * 2 * S]
    a_tabs = [(bc(re(pw_ref, 0, 1, j)), bc(im(pw_ref, 0, 1, j))) for j in range(SSM_LANE_BLOCKS)]
    zero = jnp.zeros((SUBLANES, S), F32)
    fins = _ssm_sweep(x_scr, n, a_tabs, [(zero, zero)] * SSM_LANE_BLOCKS, store=False)
    for j in range(SSM_LANE_BLOCKS):
        base = j * 2 * S
        cr, ci = re(pw_ref, pow_row, pow_row + 1, j), im(pw_ref, pow_row, pow_row + 1, j)
        sr, si = re(carry_scr, 0, 1, j), im(carry_scr, 0, 1, j)
        fr, fi = fins[j]
        for c in range(SUBLANES):
            s_scr[c:c + 1, base:base + S] = sr
            s_scr[c:c + 1, base + S:base + 2 * S] = si
            sr, si = cr * sr - ci * si + fr[c:c + 1], cr * si + ci * sr + fi[c:c + 1]
        carry_scr[:, base:base + S] = bc(sr)
        carry_scr[:, base + S:base + 2 * S] = bc(si)
    init = [(re(s_scr, 0, SUBLANES, j), im(s_scr, 0, SUBLANES, j)) for j in range(SSM_LANE_BLOCKS)]
    _ssm_sweep(x_scr, n, a_tabs, init, store=True)
    ys = [jnp.dot(x_scr[0:n, j * 2 * S:(j + 1) * 2 * S].astype(BF16), cblk_ref[j], preferred_element_type=F32)
          for j in range(SSM_LANE_BLOCKS)]
    return jnp.concatenate(ys, axis=1)


def _ssm_tile(u_ref, n, pow_row, seq_start, refs, o_ref, scr):
    pw_ref, bblk_ref, cblk_ref, d_ref, wg_ref, bg_ref, g_ref = refs
    up_scr, x_scr, s_scr, carry_scr, o_scr = scr
    q = n // SUBLANES
    for j in range(SSM_LANE_BLOCKS):
        for k in range(q):
            up_scr[k * SUBLANES:(k + 1) * SUBLANES, j * LANES:(j + 1) * LANES] = \
                u_ref[j, pl.ds(k, SUBLANES, stride=q), :]
    u = up_scr[0:n, :]
    if seq_start:
        p = jax.lax.broadcasted_iota(jnp.int32, (n, 1), 0)
        u = jnp.where((p % SUBLANES) * q + p // SUBLANES >= seq_start, u, 0.0)
    y = _ssm_rows(u, pow_row, pw_ref, bblk_ref, cblk_ref, x_scr, s_scr, carry_scr)
    out = _ssm_tail(y, u, d_ref, wg_ref, bg_ref, g_ref)
    for j in range(SSM_LANE_BLOCKS):
        for k in range(q):
            o_scr[j, pl.ds(k, SUBLANES, stride=q), :] = out[k * SUBLANES:(k + 1) * SUBLANES,
                                                            j * LANES:(j + 1) * LANES]
    o_ref[...] = jnp.concatenate([o_scr[j, 0:n, :] for j in range(SSM_LANE_BLOCKS)], axis=1).astype(o_ref.dtype)


def _mixers_kernel(sink_ref,
                   qm_ref, qh_ref, kpm_ref, km_ref, kh_ref, vpm_ref, vm_ref, vh_ref, ga_ref, bias_ref,
                   um_ref, uh_ref, pw_ref, bblk_ref, cblk_ref, d_ref, wg_ref, bg_ref, gs_ref,
                   xm_ref, halo_ref, xh_ref, wp_ref, sc_ref, gp_ref, *rest, l, n_cast):
    cast_in = rest[0:n_cast]
    am_ref, ah_ref, som_ref, soh_ref, st_ref, pom_ref, poh_ref = rest[n_cast:n_cast + 7]
    cast_out = rest[n_cast + 7:2 * n_cast + 7]
    scr = rest[2 * n_cast + 7:]
    carry_scr = scr[3]
    t = pl.program_id(1)
    tm = xm_ref.shape[0]
    ssm_refs = (pw_ref, bblk_ref, cblk_ref, d_ref, wg_ref, bg_ref, gs_ref)
    blk = functools.partial(_attn_block, sink_ref=sink_ref, l=l, g=ga_ref[...])

    @pl.when(t == 0)
    def _():
        a = blk(qh_ref[...], kpm_ref[...], kh_ref[...], vpm_ref[...], vh_ref[...], bias_ref[0])
        ah_ref[...] = a.astype(ah_ref.dtype)
        carry_scr[...] = jnp.zeros_like(carry_scr)
        _ssm_tile(uh_ref, HEAD_ROWS, POW_ROW_HEAD, SEQ0, ssm_refs, soh_ref, scr)
        hrow = jax.lax.broadcasted_iota(jnp.int32, (HEAD_ROWS, 1), 0)
        x = jnp.where(hrow >= SEQ0, xh_ref[...], 0.0)
        prev = jnp.zeros((POOL_HALO, POOL_WIDTH), F32)
        poh_ref[...] = _pool_rows(x, prev, -SEQ0, wp_ref, sc_ref, gp_ref).astype(poh_ref.dtype)

    @pl.when(t > 0)
    def _():
        for src, dst in zip(cast_in, cast_out):
            dst[...] = src[...].astype(BF16)
        kp = jnp.where(t == 1, kh_ref[...], kpm_ref[...])
        vp = jnp.where(t == 1, vh_ref[...], vpm_ref[...])
        for n in range(tm // BLOCK):
            rows = slice(n * BLOCK, (n + 1) * BLOCK)
            kc, vc = km_ref[rows, :], vm_ref[rows, :]
            bias = bias_ref[jnp.minimum(t, 2)] if n == 0 else bias_ref[2]
            am_ref[rows, :] = blk(qm_ref[rows, :], kp, kc, vp, vc, bias).astype(am_ref.dtype)
            kp, vp = kc, vc
        _ssm_tile(um_ref, um_ref.shape[1], POW_ROW_MAIN, 0, ssm_refs, som_ref, scr)
        st_ref[...] = carry_scr[...]
        prev = jnp.where(t == 1, xh_ref[HEAD_ROWS - POOL_HALO:, :], halo_ref[...])
        pom_ref[...] = _pool_rows(xm_ref[...], prev, N_META + (t - 1) * tm, wp_ref, sc_ref, gp_ref).astype(pom_ref.dtype)


def _mixers(sinks, qkv_m, qkv_h, um, uh, xm, xh, vecs, bias, pw, bblk, cblk, wg, wp, l, B, tm, casts):
    assert (1, HEAD_ROWS // SUBLANES, tm // SUBLANES) == SSM_POW_ROWS
    J = SSM_LANE_BLOCKS
    rm, rh = um.shape[1], uh.shape[1]
    nt = rm // (B * tm)
    r = tm // POOL_HALO
    bpt = tm // BLOCK
    tile = lambda b, t: b * nt + jnp.maximum(t - 1, 0)
    main = lambda b, t, _: (tile(b, t), 0)
    head = lambda b, t, _: (b, 0)
    halo = lambda b, t, _: (jnp.maximum((b * nt + t - 1) * r - 1, 0), 0)
    prevb = lambda b, t, _: (jnp.maximum(tile(b, t) * bpt - 1, 0), 0)
    cast_specs = []
    for w, wl in casts:
        _, cr, cc = w.shape
        rows = cr // (B * nt)
        cast_specs.append((pl.BlockSpec((None, rows, cc), lambda b, t, _, wl=wl: (wl, tile(b, t), 0)),
                           pl.BlockSpec((rows, cc), main), jax.ShapeDtypeStruct((cr, cc), BF16)))
    bf = lambda n, w: jax.ShapeDtypeStruct((n, w), BF16)
    tile_spec = lambda w: pl.BlockSpec((tm, w), main)
    head_spec = lambda w: pl.BlockSpec((HEAD_ROWS, w), head)
    in_specs = [tile_spec(ATTN_WIDTH), head_spec(ATTN_WIDTH),
                pl.BlockSpec((BLOCK, KV_WIDTH), prevb), tile_spec(KV_WIDTH), head_spec(KV_WIDTH),
                pl.BlockSpec((BLOCK, KV_WIDTH), prevb), tile_spec(KV_WIDTH), head_spec(KV_WIDTH),
                _vrow("g_out_attn", l), _full(bias.shape),
                pl.BlockSpec((J, tm, LANES), lambda b, t, _: (0, tile(b, t), 0)),
                pl.BlockSpec((J, HEAD_ROWS, LANES), lambda b, t, _: (0, b, 0)),
                _layer(pw.shape[1:], l), _layer(bblk.shape[1:], l), _layer(cblk.shape[1:], l),
                _vrow("D_skip", l), _layer((SSM_WIDTH, SSM_WIDTH), l), _vrow("b_glu", l), _vrow("g_out_ssm", l),
                tile_spec(POOL_WIDTH), pl.BlockSpec((POOL_HALO, POOL_WIDTH), halo), head_spec(POOL_WIDTH),
                _layer(wp.shape[1:], l), _vrow("pool_scale", l), _vrow("g_out_pool", l)]
    in_specs += [c[0] for c in cast_specs]
    out_specs = [tile_spec(ATTN_WIDTH), head_spec(ATTN_WIDTH), tile_spec(SSM_WIDTH), head_spec(SSM_WIDTH),
                 pl.BlockSpec((None, SUBLANES, SSM_STATE_LANES), lambda b, t, _: (b, 0, 0)),
                 tile_spec(POOL_WIDTH), head_spec(POOL_WIDTH)] + [c[1] for c in cast_specs]
    out_shape = [bf(rm, ATTN_WIDTH), bf(rh, ATTN_WIDTH), bf(rm, SSM_WIDTH), bf(rh, SSM_WIDTH),
                 jax.ShapeDtypeStruct((B, SUBLANES, SSM_STATE_LANES), F32),
                 bf(rm, POOL_WIDTH), bf(rh, POOL_WIDTH)] + [c[2] for c in cast_specs]
    (qm, km, vm), (qh, kh, vh) = qkv_m, qkv_h
    res = pl.pallas_call(
        functools.partial(_mixers_kernel, l=l, n_cast=len(casts)),
        grid_spec=pltpu.PrefetchScalarGridSpec(
            num_scalar_prefetch=1, grid=(B, nt + 1), in_specs=in_specs, out_specs=out_specs,
            scratch_shapes=[pltpu.VMEM((tm, SSM_WIDTH), F32),
                            pltpu.VMEM((tm, SSM_STATE_LANES), F32),
                            pltpu.VMEM((SUBLANES, SSM_STATE_LANES), F32),
                            pltpu.VMEM((SUBLANES, SSM_STATE_LANES), F32),
                            pltpu.VMEM((J, tm, LANES), F32)]),
        out_shape=out_shape,
        compiler_params=_params("arbitrary", "arbitrary"),
        name="mixers",
    )(sinks, qm, qh, km, km, kh, vm, vm, vh, vecs, bias, um, uh, pw, bblk, cblk, vecs, wg, vecs, vecs,
      xm, xm, xh, wp, vecs, vecs, *[w for w, _ in casts])
    return res[0:7], res[7:]


def _state_from_lanes(s):
    s = s.reshape(s.shape[0], SSM_LANE_BLOCKS, 2, SSM_BLOCK_STATES)
    return (s[:, :, 0].reshape(-1, SSM_GROUPS, SSM_STATE), s[:, :, 1].reshape(-1, SSM_GROUPS, SSM_STATE))


def _pool_tail(d_groups, w_ref, sc_ref, g_ref):
    y = jnp.concatenate(
        [jnp.dot(d.astype(BF16), w_ref[gi], preferred_element_type=F32) for gi, d in enumerate(d_groups)], axis=1)
    return _rms(y * sc_ref[...], g_ref[...])


def _pool_rows(x, prev, pos0, w_ref, sc_ref, g_ref):
    n = x.shape[0]
    xe = jnp.concatenate([prev, x], axis=0)
    pos = pos0 + jax.lax.broadcasted_iota(jnp.int32, (n, 1), 0)
    ds = []
    for gi, w in enumerate(POOL_WINDOWS):
        gsl = slice(gi * POOL_GROUP, (gi + 1) * POOL_GROUP)
        s = xe[:, gsl]
        k = 1
        while k < w:
            s = s + pltpu.roll(s, k, 0)
            k *= 2
        cnt = jnp.clip(pos + 1, 1, w).astype(F32)
        ds.append(s[POOL_HALO:] / cnt - x[:, gsl])
    return _pool_tail(ds, w_ref, sc_ref, g_ref)


def _shift_caches_kernel(k_ref, v_ref, nk_ref, nv_ref):
    for src, dst in ((k_ref, nk_ref), (v_ref, nv_ref)):
        for bb in range(DEC_STEP):
            dst[bb, 0:WINDOW - 1] = src[bb, 1:WINDOW]
            dst[bb, WINDOW - 1] = jnp.zeros((N_KV_HEADS, HEAD_DIM), F32)


def _shift_caches(cache_k, cache_v):
    depth, N = cache_k.shape[:2]
    blk = pl.BlockSpec((None, DEC_STEP, WINDOW, N_KV_HEADS, HEAD_DIM), lambda l, s: (l, s, 0, 0, 0))
    shape = jax.ShapeDtypeStruct(cache_k.shape, F32)
    return pl.pallas_call(
        _shift_caches_kernel,
        grid=(depth, N // DEC_STEP),
        in_specs=[blk, blk],
        out_specs=[blk, blk],
        out_shape=[shape, shape],
        compiler_params=_params("arbitrary", "arbitrary"),
        name="shift_caches",
    )(cache_k, cache_v)


def _attn_sample_kernel(q_ref, kn_ref, vn_ref, kc_ref, vc_ref, sink_ref, g_ref, a_in_ref, nk_in_ref, nv_in_ref,
                        a_ref, nk_ref, nv_ref, acc_scr):
    del a_in_ref, nk_in_ref, nv_in_ref
    step = pl.program_id(0)
    scale = HEAD_DIM ** -0.5
    for bb in range(DEC_STEP):
        outs = []
        for kh in range(N_KV_HEADS):
            ksl = slice(kh * HEAD_DIM, (kh + 1) * HEAD_DIM)
            qh = jnp.concatenate(
                [q_ref[bb:bb + 1, (kh * GQA_GROUP + g) * HEAD_DIM:(kh * GQA_GROUP + g + 1) * HEAD_DIM]
                 for g in range(GQA_GROUP)], axis=0)
            kn = kn_ref[bb:bb + 1, ksl]
            vn = vn_ref[bb:bb + 1, ksl]
            nk_ref[bb, 0, kh:kh + 1, :] = kn
            nv_ref[bb, 0, kh:kh + 1, :] = vn
            kc = kc_ref[bb, :, kh, :]
            vc = vc_ref[bb, :, kh, :]
            sc = jax.lax.dot_general(qh.astype(BF16), kc.astype(BF16), (((1,), (1,)), ((), ())),
                                     preferred_element_type=F32) * scale
            sn = jnp.sum(qh * kn, axis=-1, keepdims=True) * scale
            sk = sink_ref[kh]
            m = jnp.maximum(jnp.maximum(jnp.max(sc, axis=-1, keepdims=True), sn), sk)
            pc = jnp.exp(sc - m)
            pn = jnp.exp(sn - m)
            denom = jnp.sum(pc, axis=-1, keepdims=True) + pn + jnp.exp(sk - m)
            o = jnp.dot(pc.astype(BF16), vc.astype(BF16), preferred_element_type=F32)
            o = (o + pn * vn) / denom
            outs.extend(o[g:g + 1] for g in range(GQA_GROUP))
        a = jnp.concatenate(outs, axis=1)
        acc_scr[pl.ds(step * DEC_STEP + bb, 1), :] = _rms(a, g_ref[...])

    @pl.when(step == pl.num_programs(0) - 1)
    def _():
        a_ref[...] = acc_scr[...].astype(a_ref.dtype)


def _attn_sample(qh, kh, vh, cache_k, cache_v, l, sinks, g, ah, nk, nv, N):
    rows = lambda w: pl.BlockSpec((DEC_STEP, w), lambda s: (s, 0))
    cache = pl.BlockSpec((None, DEC_STEP, WINDOW, N_KV_HEADS, HEAD_DIM), lambda s: (l, s, 0, 0, 0))
    last = pl.BlockSpec((None, DEC_STEP, 1, N_KV_HEADS, HEAD_DIM), lambda s: (l, s, WINDOW - 1, 0, 0))
    anyspec = pl.BlockSpec(memory_space=pl.ANY)
    return pl.pallas_call(
        _attn_sample_kernel,
        grid=(N // DEC_STEP,),
        in_specs=[rows(ATTN_WIDTH), rows(KV_WIDTH), rows(KV_WIDTH), cache, cache,
                  _layer((N_KV_HEADS, GQA_GROUP, 1), l), _vrow("g_out_attn", l),
                  anyspec, anyspec, anyspec],
        out_specs=[pl.BlockSpec((N, ATTN_WIDTH), lambda s: (0, 0)), last, last],
        out_shape=[jax.ShapeDtypeStruct(ah.shape, ah.dtype), jax.ShapeDtypeStruct(nk.shape, nk.dtype),
                   jax.ShapeDtypeStruct(nv.shape, nv.dtype)],
        scratch_shapes=[pltpu.VMEM((N, ATTN_WIDTH), F32)],
        input_output_aliases={7: 0, 8: 1, 9: 2},
        compiler_params=_params("arbitrary"),
        name="attn_sample",
    )(qh, kh, vh, cache_k, cache_v, sinks, g, ah, nk, nv)


def _mix_sample_kernel(u_ref, h0_ref, pw_ref, bblk_ref, cblk_ref, d_ref, wg_ref, bg_ref, gs_ref,
                       xp_ref, pb_ref, wp_ref, sc_ref, gp_ref, s_in_ref, p_in_ref, s_ref, st_ref, p_ref):
    del s_in_ref, p_in_ref
    S = SSM_BLOCK_STATES
    u = jnp.concatenate([u_ref[j] for j in range(SSM_LANE_BLOCKS)], axis=1)
    ub = u.astype(BF16)
    ys = []
    for j in range(SSM_LANE_BLOCKS):
        x = jnp.dot(ub[:, j * LANES:(j + 1) * LANES], bblk_ref[j], preferred_element_type=F32)
        base = j * 2 * S
        ar = pw_ref[0:1, base:base + S]
        ai = pw_ref[0:1, base + S:base + 2 * S]
        h0r = h0_ref[:, base:base + S]
        h0i = h0_ref[:, base + S:base + 2 * S]
        hr = x[:, 0:S] + ar * h0r - ai * h0i
        hi = x[:, S:] + ar * h0i + ai * h0r
        st_ref[:, base:base + S] = hr
        st_ref[:, base + S:base + 2 * S] = hi
        h = jnp.concatenate([hr, hi], axis=1).astype(BF16)
        ys.append(jnp.dot(h, cblk_ref[j], preferred_element_type=F32))
    s_ref[...] = _ssm_tail(jnp.concatenate(ys, axis=1), u, d_ref, wg_ref, bg_ref, gs_ref).astype(s_ref.dtype)

    xp = xp_ref[...]
    ds = []
    for gi, w in enumerate(POOL_WINDOWS):
        gsl = slice(gi * POOL_GROUP, (gi + 1) * POOL_GROUP)
        s = xp[:, gsl]
        for back in range(1, w):
            s = s + pb_ref[POOL_BUF - back][:, gsl]
        ds.append(s / float(w) - xp[:, gsl])
    p_ref[...] = _pool_tail(ds, wp_ref, sc_ref, gp_ref).astype(p_ref.dtype)


def _mix_sample(uh, xph, h0, pbuf, vecs, pw, bblk, cblk, wg, wp, l, sh, ph):
    N = h0.shape[1]
    rows = lambda w: pl.BlockSpec((N, w), lambda i: (0, 0))
    anyspec = pl.BlockSpec(memory_space=pl.ANY)
    return pl.pallas_call(
        _mix_sample_kernel,
        grid=(1,),
        in_specs=[pl.BlockSpec((SSM_LANE_BLOCKS, N, LANES), lambda i: (0, 0, 0)),
                  _layer(h0.shape[1:], l), _layer(pw.shape[1:], l), _layer(bblk.shape[1:], l),
                  _layer(cblk.shape[1:], l),
                  _vrow("D_skip", l), _layer((SSM_WIDTH, SSM_WIDTH), l), _vrow("b_glu", l),
                  _vrow("g_out_ssm", l), rows(POOL_WIDTH), _layer(pbuf.shape[1:], l),
                  _layer(wp.shape[1:], l),
                  _vrow("pool_scale", l), _vrow("g_out_pool", l), anyspec, anyspec],
        out_specs=[rows(SSM_WIDTH), _full((N, SSM_STATE_LANES)), rows(POOL_WIDTH)],
        out_shape=[jax.ShapeDtypeStruct(sh.shape, sh.dtype), jax.ShapeDtypeStruct((N, SSM_STATE_LANES), F32),
                   jax.ShapeDtypeStruct(ph.shape, ph.dtype)],
        input_output_aliases={14: 0, 15: 2},
        compiler_params=_params("arbitrary"),
        name="mix_sample",
    )(uh, h0, pw, bblk, cblk, vecs, wg, vecs, vecs, xph, pbuf, wp, vecs, vecs, sh, ph)


def _outproj_rows(x_ref, a_ref, s_ref, p_ref, w_ref, o_ref):
    o1 = ATTN_WIDTH
    o2 = o1 + SSM_WIDTH
    acc = x_ref[...]
    acc = acc + jnp.dot(a_ref[...], w_ref[0:o1, :], preferred_element_type=F32)
    acc = acc + jnp.dot(s_ref[...], w_ref[o1:o2, :], preferred_element_type=F32)
    acc = acc + jnp.dot(p_ref[...], w_ref[o2:, :], preferred_element_type=F32)
    o_ref[...] = acc


def _outproj_kernel(xm_ref, am_ref, sm_ref, pm_ref, xh_ref, ah_ref, sh_ref, ph_ref, w_ref, om_ref, oh_ref):
    i = pl.program_id(0)
    last = pl.num_programs(0) - 1

    @pl.when(i < last)
    def _():
        _outproj_rows(xm_ref, am_ref, sm_ref, pm_ref, w_ref, om_ref)

    @pl.when(i == last)
    def _():
        _outproj_rows(xh_ref, ah_ref, sh_ref, ph_ref, w_ref, oh_ref)


def _outproj(xm, am, sm, pm, xh, ah, sh, ph, w, tm):
    R, RH = xm.shape[0], xh.shape[0]
    nm = R // tm
    row = lambda i: (jnp.minimum(i, nm - 1), 0)
    once = pl.Buffered(1)
    head = lambda w_: pl.BlockSpec((RH, w_), lambda i: (0, 0), pipeline_mode=once)
    return pl.pallas_call(
        _outproj_kernel,
        grid=(nm + 1,),
        in_specs=[pl.BlockSpec((tm, D_MODEL), row), pl.BlockSpec((tm, ATTN_WIDTH), row),
                  pl.BlockSpec((tm, SSM_WIDTH), row), pl.BlockSpec((tm, POOL_WIDTH), row),
                  head(D_MODEL), head(ATTN_WIDTH), head(SSM_WIDTH), head(POOL_WIDTH),
                  pl.BlockSpec((D_MODEL, D_MODEL), lambda i: (0, 0), pipeline_mode=once)],
        out_specs=[pl.BlockSpec((tm, D_MODEL), row), pl.BlockSpec((RH, D_MODEL), lambda i: (0, 0))],
        out_shape=[jax.ShapeDtypeStruct((R, D_MODEL), F32), jax.ShapeDtypeStruct((RH, D_MODEL), F32)],
        compiler_params=_params("arbitrary"),
        name="outproj",
    )(xm, am, sm, pm, xh, ah, sh, ph, w)


def _ffn_kernel(xm_ref, xh_ref, g_ref, w1_ref, w2_ref, *rest, n_cast):
    cast_in = rest[0:n_cast]
    om_ref, oh_ref = rest[n_cast:n_cast + 2]
    cast_out = rest[n_cast + 2:2 * n_cast + 2]
    hm_scr, hh_scr = rest[2 * n_cast + 2:]
    i = pl.program_id(0)
    f = pl.program_id(1)
    for src, dst in zip(cast_in, cast_out):
        dst[...] = src[...].astype(BF16)

    def mlp(h):
        h1 = jnp.dot(h, w1_ref[...], preferred_element_type=F32)
        return jnp.dot(jnp.square(jnp.maximum(h1, 0.0)).astype(BF16), w2_ref[...], preferred_element_type=F32)

    def first(x_ref, h_scr, o_ref):
        x = x_ref[...]
        h = _rms(x, g_ref[...]).astype(BF16)
        h_scr[...] = h
        o_ref[...] = x + mlp(h)

    @pl.when(f == 0)
    def _():
        first(xm_ref, hm_scr, om_ref)

    @pl.when(f > 0)
    def _():
        om_ref[...] += mlp(hm_scr[...])

    @pl.when((i == 0) & (f == 0))
    def _():
        first(xh_ref, hh_scr, oh_ref)

    @pl.when((i == 0) & (f > 0))
    def _():
        oh_ref[...] += mlp(hh_scr[...])


def _ffn(xm, xh, g, l, w1, w2, tm, tf, casts=()):
    R, RH = xm.shape[0], xh.shape[0]
    nf = D_FF // tf
    steps = (R // tm) * nf
    step = lambda i, f: (i * nf + f, 0)
    cast_specs = []
    for w, wl in casts:
        _, cr, cc = w.shape
        rows = cr // steps
        cast_specs.append((pl.BlockSpec((None, rows, cc), lambda i, f, wl=wl: (wl, i * nf + f, 0)),
                           pl.BlockSpec((rows, cc), step), jax.ShapeDtypeStruct((cr, cc), BF16)))
    res = pl.pallas_call(
        functools.partial(_ffn_kernel, n_cast=len(casts)),
        grid=(R // tm, nf),
        in_specs=[pl.BlockSpec((tm, D_MODEL), lambda i, f: (i, 0)),
                  pl.BlockSpec((RH, D_MODEL), lambda i, f: (0, 0), pipeline_mode=pl.Buffered(1)),
                  _vrow("g_ffn", l),
                  pl.BlockSpec((D_MODEL, tf), lambda i, f: (0, f)),
                  pl.BlockSpec((tf, D_MODEL), lambda i, f: (f, 0))] + [c[0] for c in cast_specs],
        out_specs=[pl.BlockSpec((tm, D_MODEL), lambda i, f: (i, 0)),
                   pl.BlockSpec((RH, D_MODEL), lambda i, f: (0, 0))] + [c[1] for c in cast_specs],
        out_shape=[jax.ShapeDtypeStruct((R, D_MODEL), F32), jax.ShapeDtypeStruct((RH, D_MODEL), F32)]
        + [c[2] for c in cast_specs],
        scratch_shapes=[pltpu.VMEM((tm, D_MODEL), BF16), pltpu.VMEM((RH, D_MODEL), BF16)],
        compiler_params=_params("arbitrary", "arbitrary"),
        name="ffn",
    )(xm, xh, g, w1, w2, *[w for w, _ in casts])
    return res[0], res[1], res[2:]


def _rope_tables(pos):
    half = ROT_HALF
    inv = ROPE_THETA ** (-np.arange(0, ROT_DIM, 2, dtype=np.float64) / ROT_DIM)
    ang = np.asarray(pos, np.float64)[:, None] * inv
    cos, sin = np.cos(ang), np.sin(ang)
    n = ang.shape[0]
    z = np.zeros((n, HEAD_DIM - ROT_DIM))
    zh = np.zeros((n, half))
    rc = np.concatenate([cos, cos, z + 1.0], axis=1)
    rs1 = np.concatenate([zh, sin, z], axis=1)
    rs2 = np.concatenate([-sin, zh, z], axis=1)
    return tuple(t.astype(np.float32) for t in (rc, rs1, rs2))


def kernel(x_prompt, x_sample, cache_k, cache_v, state_ssm_re, state_ssm_im, state_pool, meta_tokens, g_mix, w_in, g_q, g_k, sinks, A_re, A_im, log_dt, B_re, B_im, C_re, C_im, D_skip, w_glu, b_glu, w_pool, pool_scale, g_out_attn, g_out_ssm, g_out_pool, w_out, g_ffn, w_ff1, w_ff2):
    B, T, _ = x_prompt.shape
    N = x_sample.shape[0]
    depth = w_in.shape[0]
    assert N <= SEQ0 and N % DEC_STEP == 0 and T % TM_FFN == 0
    meta = meta_tokens.astype(F32)
    head0 = jnp.concatenate([x_sample.reshape(N, D_MODEL), jnp.zeros((SEQ0 - N, D_MODEL), F32), meta], axis=0)
    head_rest = jnp.concatenate([jnp.zeros((SEQ0, D_MODEL), F32), meta], axis=0)
    xh = jnp.concatenate([head0] + [head_rest] * (B - 1), axis=0)
    xm = x_prompt.reshape(B * T, D_MODEL)

    rope_m = _rope_tables(N_META + np.arange(T))
    hr_ = np.arange(HEAD_ROWS)
    pos_h0 = np.where(hr_ < N, PAST_LEN, np.maximum(hr_ - SEQ0, 0))
    pos_h = np.concatenate([pos_h0] + [np.maximum(hr_ - SEQ0, 0)] * (B - 1))
    rope_h = _rope_tables(pos_h)

    wi = w_in[0].astype(BF16)
    wg_all, wp_all = w_glu.astype(BF16), w_pool.astype(BF16)
    ck = cache_k.astype(F32)
    cv = cache_v.astype(F32)
    RH = B * HEAD_ROWS

    named = dict(g_mix=g_mix, g_ffn=g_ffn, g_out_attn=g_out_attn, D_skip=D_skip, b_glu=b_glu,
                 g_out_ssm=g_out_ssm, pool_scale=pool_scale, g_out_pool=g_out_pool, g_q=g_q, g_k=g_k)
    vecs = jnp.concatenate([named[name].astype(F32) for name, _ in VEC_LAYOUT], axis=1)[:, None, :]
    pw, bblk, cblk = _ssm_params(A_re, A_im, log_dt, B_re, B_im, C_re, C_im)
    sinks_flat = sinks.astype(F32).reshape(depth * N_HEADS)
    sinks_col = sinks.astype(F32).reshape(depth, N_KV_HEADS, GQA_GROUP, 1)
    bias = _attn_bias()
    h0 = jnp.concatenate([state_ssm_re.astype(F32).reshape(depth, N, SSM_LANE_BLOCKS, SSM_BLOCK_STATES),
                          state_ssm_im.astype(F32).reshape(depth, N, SSM_LANE_BLOCKS, SSM_BLOCK_STATES)],
                         axis=-1).reshape(depth, N, SSM_STATE_LANES)
    pbuf = state_pool.astype(F32).transpose(0, 2, 1, 3)

    nk, nv = _shift_caches(ck, cv)
    ks, vs, pls, sts, st_ss, phs = ([] for _ in range(6))
    for l in range(depth):
        (qm, km, vm, um, pm), (qh, kh, vh, uh, ph), w1_new = _inproj(
            xm, xh, vecs, wi, l, rope_m, rope_h, TM_PROJ, w_ff1 if l == 0 else None)
        if l == 0:
            w1 = w1_new

        (am, ah, sm, sh, st, plm, plh), wcast = _mixers(
            sinks_flat, (qm, km, vm), (qh, kh, vh), um, uh, pm, ph, vecs, bias, pw, bblk, cblk, wg_all, wp_all,
            l, B, TM_SEQ, [(w_ff2, l), (w_out, l)] if l == 0 else [])
        if l == 0:
            w2, wo = wcast

        ah, nk, nv = _attn_sample(qh, kh, vh, ck, cv, l, sinks_col, vecs, ah, nk, nv, N)
        sh, st_s, plh = _mix_sample(uh, ph, h0, pbuf, vecs, pw, bblk, cblk, wg_all, wp_all, l, sh, plh)

        xm, xh = _outproj(xm, am, sm, plm, xh, ah, sh, plh, wo, TM_PROJ)
        nxt = [(w, l + 1) for w in (w_in, w_ff1, w_ff2, w_out)] if l + 1 < depth else []
        xm, xh, wnext = _ffn(xm, xh, vecs, l, w1, w2, TM_FFN, TF_FFN, nxt)
        if nxt:
            wi, w1, w2, wo = wnext

        ks.append(km.reshape(B, T, KV_WIDTH)[:, T - WINDOW:])
        vs.append(vm.reshape(B, T, KV_WIDTH)[:, T - WINDOW:])
        pls.append(pm.reshape(B, T, POOL_WIDTH)[:, T - POOL_BUF:])
        sts.append(st[:, 0])
        st_ss.append(st_s)
        phs.append(ph[:N])

    y_prompt = xm.reshape(B, T, D_MODEL)
    y_sample = xh[:N].reshape(N, 1, D_MODEL)
    heads = lambda t: jnp.stack(t).reshape(depth, -1, WINDOW, N_KV_HEADS, HEAD_DIM)
    p_re, p_im = _state_from_lanes(jnp.stack(sts).reshape(depth * B, SSM_STATE_LANES))
    s_re, s_im = _state_from_lanes(jnp.stack(st_ss).reshape(depth * N, SSM_STATE_LANES))
    st4 = lambda t, n: t.reshape(depth, n, SSM_GROUPS, SSM_STATE)
    s_pool = jnp.concatenate([state_pool.astype(F32)[:, :, 1:], jnp.stack(phs)[:, :, None]], axis=2)
    return (y_prompt, y_sample, heads(ks), heads(vs), st4(p_re, B), st4(p_im, B), jnp.stack(pls),
            nk, nv, st4(s_re, N), st4(s_im, N), s_pool)
```

```python
import functools
import math

import jax
import jax.numpy as jnp
import numpy as np
from jax.experimental import pallas as pl
from jax.experimental.pallas import tpu as pltpu

D_MODEL = 2048
N_META = 16
HEAD_DIM = 128
N_HEADS = 8
N_KV_HEADS = 2
GQA_GROUP = 4
ATTN_WIDTH = 1024
KV_WIDTH = 256
WINDOW = 128
BLOCK = 128
ROT_DIM = 32
ROT_HALF = ROT_DIM // 2
ROPE_THETA = 500000.0
SSM_WIDTH = 512
SSM_GROUP_SIZE = 16
SSM_GROUPS = 32
SSM_STATE = 64
POOL_WIDTH = 512
POOL_WINDOWS = (2, 4, 8, 16)
POOL_GROUP = 128
POOL_BUF = 15
POOL_HALO = 16
IN_WIDTH = 2560
D_FF = 8192
EPS = 1e-6
PAST_LEN = 16384
LOG2E = math.log2(math.e)

HEAD_ROWS = BLOCK
SEQ0 = HEAD_ROWS - N_META
LANES = 128
SUBLANES = 8
SSM_LANE_BLOCKS = SSM_WIDTH // LANES
SSM_BLOCK_STATES = (LANES // SSM_GROUP_SIZE) * SSM_STATE
SSM_STATE_LANES = SSM_LANE_BLOCKS * 2 * SSM_BLOCK_STATES
VMEM_LIMIT = 60 * 1024 * 1024

TM_PROJ = 512
TM_FFN = 1024
TF_FFN = 512
TM_SEQ = 512
DEC_STEP = 8
SSM_POW_ROWS = (1, HEAD_ROWS // SUBLANES, TM_SEQ // SUBLANES)
POW_ROW_HEAD, POW_ROW_MAIN = 1, 2

BF16 = jnp.bfloat16
F32 = jnp.float32


def _params(*semantics):
    return pltpu.CompilerParams(dimension_semantics=semantics, vmem_limit_bytes=VMEM_LIMIT)


def _rms(x, g):
    return x * jax.lax.rsqrt(jnp.mean(x * x, axis=-1, keepdims=True) + EPS) * g


def _full(shape):
    n = len(shape)
    return pl.BlockSpec(shape, lambda *_: (0,) * n)


def _layer(shape, l):
    n = len(shape)
    return pl.BlockSpec((None, *shape), lambda *_: (l,) + (0,) * n)


VEC_LAYOUT = (("g_mix", D_MODEL), ("g_ffn", D_MODEL), ("g_out_attn", ATTN_WIDTH), ("D_skip", SSM_WIDTH),
              ("b_glu", SSM_WIDTH), ("g_out_ssm", SSM_WIDTH), ("pool_scale", POOL_WIDTH),
              ("g_out_pool", POOL_WIDTH), ("g_q", HEAD_DIM), ("g_k", HEAD_DIM))
VEC_WIDTH = dict(VEC_LAYOUT)
VEC_OFFSET = {name: sum(w for _, w in VEC_LAYOUT[:i]) for i, (name, _) in enumerate(VEC_LAYOUT)}
assert all(VEC_OFFSET[name] % w == 0 for name, w in VEC_LAYOUT)


def _vrow(name, l):
    w = VEC_WIDTH[name]
    return pl.BlockSpec((None, 1, w), lambda *_: (l, 0, VEC_OFFSET[name] // w))


def _inproj_rows(x, g_ref, w_ref, gq_ref, gk_ref, rope_refs, out_refs):
    q_ref, k_ref, v_ref, u_ref, xp_ref = out_refs
    h = _rms(x, g_ref[...]).astype(BF16)
    proj = jnp.dot(h, w_ref[...], preferred_element_type=F32)
    rc, rs1, rs2 = (r[...] for r in rope_refs)

    def head(t, g):
        t = _rms(t, g)
        return t * rc + pltpu.roll(t, ROT_HALF, 1) * rs1 + pltpu.roll(t, LANES - ROT_HALF, 1) * rs2

    for hd in range(N_HEADS):
        sl = slice(hd * HEAD_DIM, (hd + 1) * HEAD_DIM)
        q_ref[:, sl] = head(proj[:, sl], gq_ref[...])
    for hd in range(N_KV_HEADS):
        sl = slice(hd * HEAD_DIM, (hd + 1) * HEAD_DIM)
        k_ref[:, sl] = head(proj[:, ATTN_WIDTH + hd * HEAD_DIM:ATTN_WIDTH + (hd + 1) * HEAD_DIM], gk_ref[...])
    o2 = ATTN_WIDTH + KV_WIDTH
    o3 = o2 + KV_WIDTH
    o4 = o3 + SSM_WIDTH
    v_ref[...] = proj[:, o2:o3]
    for j in range(SSM_LANE_BLOCKS):
        u_ref[j] = proj[:, o3 + j * LANES:o3 + (j + 1) * LANES]
    xp_ref[...] = proj[:, o4:]


def _inproj_kernel(xm_ref, xh_ref, g_ref, w_ref, gq_ref, gk_ref, rcm_ref, rs1m_ref, rs2m_ref,
                   rch_ref, rs1h_ref, rs2h_ref, *rest):
    cast = len(rest) == 12
    outs = rest[1:] if cast else rest
    main_outs, head_outs = outs[0:5], outs[5:10]
    i = pl.program_id(0)
    last = pl.num_programs(0) - 1

    @pl.when(i < last)
    def _():
        if cast:
            outs[10][...] = rest[0][...].astype(BF16)
        _inproj_rows(xm_ref[...], g_ref, w_ref, gq_ref, gk_ref, (rcm_ref, rs1m_ref, rs2m_ref), main_outs)

    @pl.when(i == last)
    def _():
        _inproj_rows(xh_ref[...], g_ref, w_ref, gq_ref, gk_ref, (rch_ref, rs1h_ref, rs2h_ref), head_outs)


def _inproj(xm, xh, vecs, w, l, rope_m, rope_h, tm, cast=None):
    R, RH = xm.shape[0], xh.shape[0]
    nm = R // tm
    tiles_per_rope = rope_m[0].shape[0] // tm
    tile = lambda i: jnp.minimum(i, nm - 1)
    row = lambda i: (tile(i), 0)
    rrow = lambda i: (tile(i) % tiles_per_rope, 0)
    once = pl.Buffered(1)
    full = lambda n, w_: pl.BlockSpec((n, w_), lambda i: (0, 0))

    def out_set(n, rows, im, uim):
        flat = lambda w_: (pl.BlockSpec((n, w_), im), jax.ShapeDtypeStruct((rows, w_), F32))
        u_out = (pl.BlockSpec((SSM_LANE_BLOCKS, n, LANES), uim),
                 jax.ShapeDtypeStruct((SSM_LANE_BLOCKS, rows, LANES), F32))
        return [flat(ATTN_WIDTH), flat(KV_WIDTH), flat(KV_WIDTH), u_out, flat(POOL_WIDTH)]

    outs = (out_set(tm, R, row, lambda i: (0, tile(i), 0))
            + out_set(RH, RH, lambda i: (0, 0), lambda i: (0, 0, 0)))
    in_specs = [pl.BlockSpec((tm, D_MODEL), row),
                pl.BlockSpec((RH, D_MODEL), lambda i: (0, 0), pipeline_mode=once),
                _vrow("g_mix", l),
                pl.BlockSpec((D_MODEL, IN_WIDTH), lambda i: (0, 0), pipeline_mode=once),
                _vrow("g_q", l), _vrow("g_k", l),
                pl.BlockSpec((tm, LANES), rrow), pl.BlockSpec((tm, LANES), rrow), pl.BlockSpec((tm, LANES), rrow),
                full(RH, LANES), full(RH, LANES), full(RH, LANES)]
    args = [xm, xh, vecs, w, vecs, vecs, *rope_m, *rope_h]
    if cast is not None:
        _, cr, cc = cast.shape
        crows = cr // nm
        in_specs.append(pl.BlockSpec((None, crows, cc), lambda i: (l, tile(i), 0)))
        args.append(cast)
        outs.append((pl.BlockSpec((crows, cc), row), jax.ShapeDtypeStruct((cr, cc), BF16)))
    res = pl.pallas_call(
        _inproj_kernel,
        grid=(nm + 1,),
        in_specs=in_specs,
        out_specs=[o[0] for o in outs],
        out_shape=[o[1] for o in outs],
        compiler_params=_params("arbitrary"),
        name="inproj",
    )(*args)
    return res[0:5], res[5:10], (res[10] if cast is not None else None)


def _attn_bias():
    rows = GQA_GROUP * BLOCK
    i, r, c = np.meshgrid(np.arange(3), np.arange(rows) % BLOCK, np.arange(2 * BLOCK), indexing="ij")
    diff = BLOCK + r - c
    krow = (i - 1) * BLOCK + c
    mask = (diff >= 0) & (diff <= WINDOW) & (krow >= SEQ0)
    return np.where(mask, 0.0, -np.inf).astype(np.float32)


def _attn_block(q_blk, kp_blk, kc_blk, vp_blk, vc_blk, bias, sink_ref, l, g):
    rows = GQA_GROUP * BLOCK
    rgrp = jax.lax.broadcasted_iota(jnp.int32, (rows, 1), 0) // BLOCK
    outs = []
    for kh in range(N_KV_HEADS):
        ksl = slice(kh * HEAD_DIM, (kh + 1) * HEAD_DIM)
        qh = jnp.concatenate(
            [q_blk[:, (kh * GQA_GROUP + h) * HEAD_DIM:(kh * GQA_GROUP + h + 1) * HEAD_DIM]
             for h in range(GQA_GROUP)], axis=0).astype(BF16)
        kk = jnp.concatenate([kp_blk[:, ksl], kc_blk[:, ksl]], axis=0).astype(BF16)
        vv = jnp.concatenate([vp_blk[:, ksl], vc_blk[:, ksl]], axis=0).astype(BF16)
        s = jax.lax.dot_general(qh, kk, (((1,), (1,)), ((), ())),
                                preferred_element_type=F32) * (HEAD_DIM ** -0.5 * LOG2E) + bias
        sk = jnp.zeros((rows, 1), F32)
        for h in range(GQA_GROUP):
            sk = jnp.where(rgrp == h, sink_ref[l * N_HEADS + kh * GQA_GROUP + h] * LOG2E, sk)
        m = jnp.maximum(jnp.max(s, axis=-1, keepdims=True), sk)
        p = jnp.exp2(s - m)
        denom = jnp.sum(p, axis=-1, keepdims=True) + jnp.exp2(sk - m)
        o = jnp.dot(p.astype(BF16), vv, preferred_element_type=F32) / denom
        outs.extend(o[h * BLOCK:(h + 1) * BLOCK] for h in range(GQA_GROUP))
    return _rms(jnp.concatenate(outs, axis=1), g)


def _ssm_params_kernel(ar_ref, ai_ref, ldt_ref, kk_ref, br_ref, bi_ref, tr_ref, ti_ref, bbr_ref, bbi_ref):
    ar, ai = ar_ref[...], ai_ref[...]
    dt = jnp.exp(ldt_ref[...])
    kk = kk_ref[...]
    mag = jnp.exp(dt * ar * kk)
    ang = dt * ai * kk
    tr = mag * jnp.cos(ang)
    ti = mag * jnp.sin(ang)
    tr_ref[...] = tr
    ti_ref[...] = ti
    abr, abi = tr[0:1], ti[0:1]
    den = ar * ar + ai * ai
    fr = ((abr - 1.0) * ar + abi * ai) / den
    fi = (abi * ar - (abr - 1.0) * ai) / den
    br, bi = br_ref[...], bi_ref[...]
    bbr_ref[...] = fr * br - fi * bi
    bbi_ref[...] = fr * bi + fi * br


def _ssm_params(A_re, A_im, log_dt, B_re, B_im, C_re, C_im):
    depth = A_re.shape[0]
    n = depth * SSM_GROUPS * SSM_STATE
    row = lambda t: t.astype(F32).reshape(1, n)
    ldt = jnp.broadcast_to(log_dt.astype(F32)[:, :, None], (depth, SSM_GROUPS, SSM_STATE)).reshape(1, n)
    kk = jnp.array(SSM_POW_ROWS + (0,) * (SUBLANES - len(SSM_POW_ROWS)), F32).reshape(SUBLANES, 1)
    chan_first = lambda t: t.astype(F32).reshape(n, SSM_GROUP_SIZE).T
    shapes = [(SUBLANES, n), (SUBLANES, n), (SSM_GROUP_SIZE, n), (SSM_GROUP_SIZE, n)]
    tr, ti, bbr, bbi = pl.pallas_call(
        _ssm_params_kernel,
        out_shape=[jax.ShapeDtypeStruct(s, F32) for s in shapes],
        name="ssm_params",
    )(row(A_re), row(A_im), ldt, kk, chan_first(B_re), chan_first(B_im))
    J, G8 = SSM_LANE_BLOCKS, LANES // SSM_GROUP_SIZE

    def lanes(t):
        return t.reshape(SUBLANES, depth, J, SSM_BLOCK_STATES).transpose(1, 0, 2, 3)

    pw = jnp.concatenate([lanes(tr), lanes(ti)], axis=-1).reshape(depth, SUBLANES, SSM_STATE_LANES)
    eye = jnp.eye(G8, dtype=F32)

    def bdiag(t):
        t = t.reshape(SSM_GROUP_SIZE, depth, J, G8, SSM_STATE).transpose(1, 2, 3, 0, 4)
        t = t[:, :, :, :, None, :] * eye[None, None, :, None, :, None]
        return t.reshape(depth, J, LANES, SSM_BLOCK_STATES)

    bblk = jnp.concatenate([bdiag(bbr), bdiag(bbi)], axis=-1).astype(BF16)

    def cdiag(t):
        t = t.astype(F32).reshape(depth, J, G8, SSM_GROUP_SIZE, SSM_STATE).transpose(0, 1, 2, 4, 3)
        t = t[:, :, :, :, None, :] * eye[None, None, :, None, :, None]
        return t.reshape(depth, J, SSM_BLOCK_STATES, LANES)

    cblk = jnp.concatenate([cdiag(C_re), -cdiag(C_im)], axis=2).astype(BF16)
    return pw, bblk, cblk


def _ssm_tail(y, u, d_ref, wg_ref, bg_ref, g_ref):
    y = y + d_ref[...] * u
    z = jax.nn.gelu(y)
    gate = jax.nn.sigmoid(jnp.dot(z.astype(BF16), wg_ref[...], preferred_element_type=F32) + bg_ref[...])
    return _rms(z * gate, g_ref[...])


def _ssm_sweep(x_scr, n, a_tabs, init, store):
    S = SSM_BLOCK_STATES
    fins = []
    for j0 in range(0, SSM_LANE_BLOCKS, 2):
        js = (j0, j0 + 1)

        def body(k, carry, js=js):
            r0 = pl.multiple_of(k * SUBLANES, SUBLANES)
            out = []
            for idx, j in enumerate(js):
                hr, hi = carry[2 * idx], carry[2 * idx + 1]
                base = j * 2 * S
                ar, ai = a_tabs[j]
                nhr = ar * hr - ai * hi + x_scr[pl.ds(r0, SUBLANES), base:base + S]
                nhi = ar * hi + ai * hr + x_scr[pl.ds(r0, SUBLANES), base + S:base + 2 * S]
                if store:
                    x_scr[pl.ds(r0, SUBLANES), base:base + S] = nhr
                    x_scr[pl.ds(r0, SUBLANES), base + S:base + 2 * S] = nhi
                out += [nhr, nhi]
            return tuple(out)

        c0 = tuple(t for j in js for t in init[j])
        res = jax.lax.fori_loop(0, n // SUBLANES, body, c0, unroll=True)
        fins += [(res[0], res[1]), (res[2], res[3])]
    return fins


def _ssm_rows(u, pow_row, pw_ref, bblk_ref, cblk_ref, x_scr, s_scr, carry_scr):
    n = u.shape[0]
    S = SSM_BLOCK_STATES
    ub = u.astype(BF16)
    for j in range(SSM_LANE_BLOCKS):
        x_scr[0:n, j * 2 * S:(j + 1) * 2 * S] = jnp.dot(ub[:, j * LANES:(j + 1) * LANES], bblk_ref[j],
                                                       preferred_element_type=F32)
    bc = lambda t: jnp.broadcast_to(t, (SUBLANES, S))
    re = lambda ref, r0, r1, j: ref[r0:r1, j * 2 * S:j * 2 * S + S]
    im = lambda ref, r0, r1, j: ref[r0:r1, j * 2 * S + S:(j + 1) * 2 * S]
    a_tabs = [(bc(re(pw_ref, 0, 1, j)), bc(im(pw_ref, 0, 1, j))) for j in range(SSM_LANE_BLOCKS)]
    zero = jnp.zeros((SUBLANES, S), F32)
    fins = _ssm_sweep(x_scr, n, a_tabs, [(zero, zero)] * SSM_LANE_BLOCKS, store=False)
    for j in range(SSM_LANE_BLOCKS):
        base = j * 2 * S
        cr, ci = re(pw_ref, pow_row, pow_row + 1, j), im(pw_ref, pow_row, pow_row + 1, j)
        sr, si = re(carry_scr, 0, 1, j), im(carry_scr, 0, 1, j)
        fr, fi = fins[j]
        for c in range(SUBLANES):
            s_scr[c:c + 1, base:base + S] = sr
            s_scr[c:c + 1, base + S:base + 2 * S] = si
            sr, si = cr * sr - ci * si + fr[c:c + 1], cr * si + ci * sr + fi[c:c + 1]
        carry_scr[:, base:base + S] = bc(sr)
        carry_scr[:, base + S:base + 2 * S] = bc(si)
    init = [(re(s_scr, 0, SUBLANES, j), im(s_scr, 0, SUBLANES, j)) for j in range(SSM_LANE_BLOCKS)]
    _ssm_sweep(x_scr, n, a_tabs, init, store=True)
    ys = [jnp.dot(x_scr[0:n, j * 2 * S:(j + 1) * 2 * S].astype(BF16), cblk_ref[j], preferred_element_type=F32)
          for j in range(SSM_LANE_BLOCKS)]
    return jnp.concatenate(ys, axis=1)


def _ssm_tile(u_ref, n, pow_row, seq_start, refs, o_ref, scr):
    pw_ref, bblk_ref, cblk_ref, d_ref, wg_ref, bg_ref, g_ref = refs
    up_scr, x_scr, s_scr, carry_scr, o_scr = scr
    q = n // SUBLANES
    for j in range(SSM_LANE_BLOCKS):
        for k in range(q):
            up_scr[k * SUBLANES:(k + 1) * SUBLANES, j * LANES:(j + 1) * LANES] = \
                u_ref[j, pl.ds(k, SUBLANES, stride=q), :]
    u = up_scr[0:n, :]
    if seq_start:
        p = jax.lax.broadcasted_iota(jnp.int32, (n, 1), 0)
        u = jnp.where((p % SUBLANES) * q + p // SUBLANES >= seq_start, u, 0.0)
    y = _ssm_rows(u, pow_row, pw_ref, bblk_ref, cblk_ref, x_scr, s_scr, carry_scr)
    out = _ssm_tail(y, u, d_ref, wg_ref, bg_ref, g_ref)
    for j in range(SSM_LANE_BLOCKS):
        for k in range(q):
            o_scr[j, pl.ds(k, SUBLANES, stride=q), :] = out[k * SUBLANES:(k + 1) * SUBLANES,
                                                            j * LANES:(j + 1) * LANES]
    o_ref[...] = jnp.concatenate([o_scr[j, 0:n, :] for j in range(SSM_LANE_BLOCKS)], axis=1).astype(o_ref.dtype)


def _mixers_kernel(sink_ref,
                   qm_ref, qh_ref, kpm_ref, km_ref, kh_ref, vpm_ref, vm_ref, vh_ref, ga_ref, bias_ref,
                   um_ref, uh_ref, pw_ref, bblk_ref, cblk_ref, d_ref, wg_ref, bg_ref, gs_ref,
                   xm_ref, halo_ref, xh_ref, wp_ref, sc_ref, gp_ref, *rest, l, n_cast):
    cast_in = rest[0:n_cast]
    am_ref, ah_ref, som_ref, soh_ref, st_ref, pom_ref, poh_ref = rest[n_cast:n_cast + 7]
    cast_out = rest[n_cast + 7:2 * n_cast + 7]
    scr = rest[2 * n_cast + 7:]
    carry_scr = scr[3]
    t = pl.program_id(1)
    tm = xm_ref.shape[0]
    ssm_refs = (pw_ref, bblk_ref, cblk_ref, d_ref, wg_ref, bg_ref, gs_ref)
    blk = functools.partial(_attn_block, sink_ref=sink_ref, l=l, g=ga_ref[...])

    hrow = jax.lax.broadcasted_iota(jnp.int32, (HEAD_ROWS, 1), 0)
    head_k = lambda: jnp.where(hrow >= SEQ0, kh_ref[...], 0.0)
    head_v = lambda: jnp.where(hrow >= SEQ0, vh_ref[...], 0.0)

    @pl.when(t == 0)
    def _():
        a = blk(qh_ref[...], kpm_ref[...], head_k(), vpm_ref[...], head_v(), bias_ref[0])
        ah_ref[...] = a.astype(ah_ref.dtype)
        carry_scr[...] = jnp.zeros_like(carry_scr)
        _ssm_tile(uh_ref, HEAD_ROWS, POW_ROW_HEAD, SEQ0, ssm_refs, soh_ref, scr)
        x = jnp.where(hrow >= SEQ0, xh_ref[...], 0.0)
        prev = jnp.zeros((POOL_HALO, POOL_WIDTH), F32)
        poh_ref[...] = _pool_rows(x, prev, -SEQ0, wp_ref, sc_ref, gp_ref).astype(poh_ref.dtype)

    @pl.when(t > 0)
    def _():
        for src, dst in zip(cast_in, cast_out):
            dst[...] = src[...].astype(BF16)
        kp = jnp.where(t == 1, head_k(), kpm_ref[...])
        vp = jnp.where(t == 1, head_v(), vpm_ref[...])
        for n in range(tm // BLOCK):
            rows = slice(n * BLOCK, (n + 1) * BLOCK)
            kc, vc = km_ref[rows, :], vm_ref[rows, :]
            bias = bias_ref[jnp.minimum(t, 2)] if n == 0 else bias_ref[2]
            am_ref[rows, :] = blk(qm_ref[rows, :], kp, kc, vp, vc, bias).astype(am_ref.dtype)
            kp, vp = kc, vc
        _ssm_tile(um_ref, um_ref.shape[1], POW_ROW_MAIN, 0, ssm_refs, som_ref, scr)
        st_ref[...] = carry_scr[...]
        prev = jnp.where(t == 1, xh_ref[HEAD_ROWS - POOL_HALO:, :], halo_ref[...])
        pom_ref[...] = _pool_rows(xm_ref[...], prev, N_META + (t - 1) * tm, wp_ref, sc_ref, gp_ref).astype(pom_ref.dtype)


def _mixers(sinks, qkv_m, qkv_h, um, uh, xm, xh, vecs, bias, pw, bblk, cblk, wg, wp, l, B, tm, casts):
    assert (1, HEAD_ROWS // SUBLANES, tm // SUBLANES) == SSM_POW_ROWS
    J = SSM_LANE_BLOCKS
    rm, rh = um.shape[1], uh.shape[1]
    nt = rm // (B * tm)
    r = tm // POOL_HALO
    bpt = tm // BLOCK
    tile = lambda b, t: b * nt + jnp.maximum(t - 1, 0)
    main = lambda b, t, _: (tile(b, t), 0)
    head = lambda b, t, _: (b, 0)
    halo = lambda b, t, _: (jnp.maximum((b * nt + t - 1) * r - 1, 0), 0)
    prevb = lambda b, t, _: (jnp.maximum(tile(b, t) * bpt - 1, 0), 0)
    cast_specs = []
    for w, wl in casts:
        _, cr, cc = w.shape
        rows = cr // (B * nt)
        cast_specs.append((pl.BlockSpec((None, rows, cc), lambda b, t, _, wl=wl: (wl, tile(b, t), 0)),
                           pl.BlockSpec((rows, cc), main), jax.ShapeDtypeStruct((cr, cc), BF16)))
    bf = lambda n, w: jax.ShapeDtypeStruct((n, w), BF16)
    tile_spec = lambda w: pl.BlockSpec((tm, w), main)
    head_spec = lambda w: pl.BlockSpec((HEAD_ROWS, w), head)
    in_specs = [tile_spec(ATTN_WIDTH), head_spec(ATTN_WIDTH),
                pl.BlockSpec((BLOCK, KV_WIDTH), prevb), tile_spec(KV_WIDTH), head_spec(KV_WIDTH),
                pl.BlockSpec((BLOCK, KV_WIDTH), prevb), tile_spec(KV_WIDTH), head_spec(KV_WIDTH),
                _vrow("g_out_attn", l), _full(bias.shape),
                pl.BlockSpec((J, tm, LANES), lambda b, t, _: (0, tile(b, t), 0)),
                pl.BlockSpec((J, HEAD_ROWS, LANES), lambda b, t, _: (0, b, 0)),
                _layer(pw.shape[1:], l), _layer(bblk.shape[1:], l), _layer(cblk.shape[1:], l),
                _vrow("D_skip", l), _layer((SSM_WIDTH, SSM_WIDTH), l), _vrow("b_glu", l), _vrow("g_out_ssm", l),
                tile_spec(POOL_WIDTH), pl.BlockSpec((POOL_HALO, POOL_WIDTH), halo), head_spec(POOL_WIDTH),
                _layer(wp.shape[1:], l), _vrow("pool_scale", l), _vrow("g_out_pool", l)]
    in_specs += [c[0] for c in cast_specs]
    out_specs = [tile_spec(ATTN_WIDTH), head_spec(ATTN_WIDTH), tile_spec(SSM_WIDTH), head_spec(SSM_WIDTH),
                 pl.BlockSpec((None, SUBLANES, SSM_STATE_LANES), lambda b, t, _: (b, 0, 0)),
                 tile_spec(POOL_WIDTH), head_spec(POOL_WIDTH)] + [c[1] for c in cast_specs]
    out_shape = [bf(rm, ATTN_WIDTH), bf(rh, ATTN_WIDTH), bf(rm, SSM_WIDTH), bf(rh, SSM_WIDTH),
                 jax.ShapeDtypeStruct((B, SUBLANES, SSM_STATE_LANES), F32),
                 bf(rm, POOL_WIDTH), bf(rh, POOL_WIDTH)] + [c[2] for c in cast_specs]
    (qm, km, vm), (qh, kh, vh) = qkv_m, qkv_h
    res = pl.pallas_call(
        functools.partial(_mixers_kernel, l=l, n_cast=len(casts)),
        grid_spec=pltpu.PrefetchScalarGridSpec(
            num_scalar_prefetch=1, grid=(B, nt + 1), in_specs=in_specs, out_specs=out_specs,
            scratch_shapes=[pltpu.VMEM((tm, SSM_WIDTH), F32),
                            pltpu.VMEM((tm, SSM_STATE_LANES), F32),
                            pltpu.VMEM((SUBLANES, SSM_STATE_LANES), F32),
                            pltpu.VMEM((SUBLANES, SSM_STATE_LANES), F32),
                            pltpu.VMEM((J, tm, LANES), F32)]),
        out_shape=out_shape,
        compiler_params=_params("arbitrary", "arbitrary"),
        name="mixers",
    )(sinks, qm, qh, km, km, kh, vm, vm, vh, vecs, bias, um, uh, pw, bblk, cblk, vecs, wg, vecs, vecs,
      xm, xm, xh, wp, vecs, vecs, *[w for w, _ in casts])
    return res[0:7], res[7:]


def _state_from_lanes(s):
    s = s.reshape(s.shape[0], SSM_LANE_BLOCKS, 2, SSM_BLOCK_STATES)
    return (s[:, :, 0].reshape(-1, SSM_GROUPS, SSM_STATE), s[:, :, 1].reshape(-1, SSM_GROUPS, SSM_STATE))


def _pool_tail(d_groups, w_ref, sc_ref, g_ref):
    y = jnp.concatenate(
        [jnp.dot(d.astype(BF16), w_ref[gi], preferred_element_type=F32) for gi, d in enumerate(d_groups)], axis=1)
    return _rms(y * sc_ref[...], g_ref[...])


def _pool_rows(x, prev, pos0, w_ref, sc_ref, g_ref):
    n = x.shape[0]
    xe = jnp.concatenate([prev, x], axis=0)
    pos = pos0 + jax.lax.broadcasted_iota(jnp.int32, (n, 1), 0)
    ds = []
    for gi, w in enumerate(POOL_WINDOWS):
        gsl = slice(gi * POOL_GROUP, (gi + 1) * POOL_GROUP)
        s = xe[:, gsl]
        k = 1
        while k < w:
            s = s + pltpu.roll(s, k, 0)
            k *= 2
        cnt = jnp.clip(pos + 1, 1, w).astype(F32)
        ds.append(s[POOL_HALO:] / cnt - x[:, gsl])
    return _pool_tail(ds, w_ref, sc_ref, g_ref)


def _shift_caches_kernel(k_ref, v_ref, nk_ref, nv_ref):
    for src, dst in ((k_ref, nk_ref), (v_ref, nv_ref)):
        for bb in range(DEC_STEP):
            dst[bb, 0:WINDOW - 1] = src[bb, 1:WINDOW]
            dst[bb, WINDOW - 1] = jnp.zeros((N_KV_HEADS, HEAD_DIM), F32)


def _shift_caches(cache_k, cache_v):
    depth, N = cache_k.shape[:2]
    blk = pl.BlockSpec((None, DEC_STEP, WINDOW, N_KV_HEADS, HEAD_DIM), lambda l, s: (l, s, 0, 0, 0))
    shape = jax.ShapeDtypeStruct(cache_k.shape, F32)
    return pl.pallas_call(
        _shift_caches_kernel,
        grid=(depth, N // DEC_STEP),
        in_specs=[blk, blk],
        out_specs=[blk, blk],
        out_shape=[shape, shape],
        compiler_params=_params("arbitrary", "arbitrary"),
        name="shift_caches",
    )(cache_k, cache_v)


def _attn_sample_kernel(q_ref, kn_ref, vn_ref, kc_ref, vc_ref, sink_ref, g_ref, a_in_ref, nk_in_ref, nv_in_ref,
                        a_ref, nk_ref, nv_ref, acc_scr):
    del a_in_ref, nk_in_ref, nv_in_ref
    step = pl.program_id(0)
    scale = HEAD_DIM ** -0.5
    for bb in range(DEC_STEP):
        outs = []
        for kh in range(N_KV_HEADS):
            ksl = slice(kh * HEAD_DIM, (kh + 1) * HEAD_DIM)
            qh = jnp.concatenate(
                [q_ref[bb:bb + 1, (kh * GQA_GROUP + g) * HEAD_DIM:(kh * GQA_GROUP + g + 1) * HEAD_DIM]
                 for g in range(GQA_GROUP)], axis=0)
            kn = kn_ref[bb:bb + 1, ksl]
            vn = vn_ref[bb:bb + 1, ksl]
            nk_ref[bb, 0, kh:kh + 1, :] = kn
            nv_ref[bb, 0, kh:kh + 1, :] = vn
            kc = kc_ref[bb, :, kh, :]
            vc = vc_ref[bb, :, kh, :]
            sc = jax.lax.dot_general(qh.astype(BF16), kc.astype(BF16), (((1,), (1,)), ((), ())),
                                     preferred_element_type=F32) * scale
            sn = jnp.sum(qh * kn, axis=-1, keepdims=True) * scale
            sk = sink_ref[kh]
            m = jnp.maximum(jnp.maximum(jnp.max(sc, axis=-1, keepdims=True), sn), sk)
            pc = jnp.exp(sc - m)
            pn = jnp.exp(sn - m)
            denom = jnp.sum(pc, axis=-1, keepdims=True) + pn + jnp.exp(sk - m)
            o = jnp.dot(pc.astype(BF16), vc.astype(BF16), preferred_element_type=F32)
            o = (o + pn * vn) / denom
            outs.extend(o[g:g + 1] for g in range(GQA_GROUP))
        a = jnp.concatenate(outs, axis=1)
        acc_scr[pl.ds(step * DEC_STEP + bb, 1), :] = _rms(a, g_ref[...])

    @pl.when(step == pl.num_programs(0) - 1)
    def _():
        a_ref[...] = acc_scr[...].astype(a_ref.dtype)


def _attn_sample(qh, kh, vh, cache_k, cache_v, l, sinks, g, ah, nk, nv, N):
    rows = lambda w: pl.BlockSpec((DEC_STEP, w), lambda s: (s, 0))
    cache = pl.BlockSpec((None, DEC_STEP, WINDOW, N_KV_HEADS, HEAD_DIM), lambda s: (l, s, 0, 0, 0))
    last = pl.BlockSpec((None, DEC_STEP, 1, N_KV_HEADS, HEAD_DIM), lambda s: (l, s, WINDOW - 1, 0, 0))
    anyspec = pl.BlockSpec(memory_space=pl.ANY)
    return pl.pallas_call(
        _attn_sample_kernel,
        grid=(N // DEC_STEP,),
        in_specs=[rows(ATTN_WIDTH), rows(KV_WIDTH), rows(KV_WIDTH), cache, cache,
                  _layer((N_KV_HEADS, GQA_GROUP, 1), l), _vrow("g_out_attn", l),
                  anyspec, anyspec, anyspec],
        out_specs=[pl.BlockSpec((N, ATTN_WIDTH), lambda s: (0, 0)), last, last],
        out_shape=[jax.ShapeDtypeStruct(ah.shape, ah.dtype), jax.ShapeDtypeStruct(nk.shape, nk.dtype),
                   jax.ShapeDtypeStruct(nv.shape, nv.dtype)],
        scratch_shapes=[pltpu.VMEM((N, ATTN_WIDTH), F32)],
        input_output_aliases={7: 0, 8: 1, 9: 2},
        compiler_params=_params("arbitrary"),
        name="attn_sample",
    )(qh, kh, vh, cache_k, cache_v, sinks, g, ah, nk, nv)


def _mix_sample_kernel(u_ref, h0_ref, pw_ref, bblk_ref, cblk_ref, d_ref, wg_ref, bg_ref, gs_ref,
                       xp_ref, pb_ref, wp_ref, sc_ref, gp_ref, s_in_ref, p_in_ref, s_ref, st_ref, p_ref):
    del s_in_ref, p_in_ref
    S = SSM_BLOCK_STATES
    u = jnp.concatenate([u_ref[j] for j in range(SSM_LANE_BLOCKS)], axis=1)
    ub = u.astype(BF16)
    ys = []
    for j in range(SSM_LANE_BLOCKS):
        x = jnp.dot(ub[:, j * LANES:(j + 1) * LANES], bblk_ref[j], preferred_element_type=F32)
        base = j * 2 * S
        ar = pw_ref[0:1, base:base + S]
        ai = pw_ref[0:1, base + S:base + 2 * S]
        h0r = h0_ref[:, base:base + S]
        h0i = h0_ref[:, base + S:base + 2 * S]
        hr = x[:, 0:S] + ar * h0r - ai * h0i
        hi = x[:, S:] + ar * h0i + ai * h0r
        st_ref[:, base:base + S] = hr
        st_ref[:, base + S:base + 2 * S] = hi
        h = jnp.concatenate([hr, hi], axis=1).astype(BF16)
        ys.append(jnp.dot(h, cblk_ref[j], preferred_element_type=F32))
    s_ref[...] = _ssm_tail(jnp.concatenate(ys, axis=1), u, d_ref, wg_ref, bg_ref, gs_ref).astype(s_ref.dtype)

    xp = xp_ref[...]
    ds = []
    for gi, w in enumerate(POOL_WINDOWS):
        gsl = slice(gi * POOL_GROUP, (gi + 1) * POOL_GROUP)
        s = xp[:, gsl]
        for back in range(1, w):
            s = s + pb_ref[POOL_BUF - back][:, gsl]
        ds.append(s / float(w) - xp[:, gsl])
    p_ref[...] = _pool_tail(ds, wp_ref, sc_ref, gp_ref).astype(p_ref.dtype)


def _mix_sample(uh, xph, h0, pbuf, vecs, pw, bblk, cblk, wg, wp, l, sh, ph):
    N = h0.shape[1]
    rows = lambda w: pl.BlockSpec((N, w), lambda i: (0, 0))
    anyspec = pl.BlockSpec(memory_space=pl.ANY)
    return pl.pallas_call(
        _mix_sample_kernel,
        grid=(1,),
        in_specs=[pl.BlockSpec((SSM_LANE_BLOCKS, N, LANES), lambda i: (0, 0, 0)),
                  _layer(h0.shape[1:], l), _layer(pw.shape[1:], l), _layer(bblk.shape[1:], l),
                  _layer(cblk.shape[1:], l),
                  _vrow("D_skip", l), _layer((SSM_WIDTH, SSM_WIDTH), l), _vrow("b_glu", l),
                  _vrow("g_out_ssm", l), rows(POOL_WIDTH), _layer(pbuf.shape[1:], l),
                  _layer(wp.shape[1:], l),
                  _vrow("pool_scale", l), _vrow("g_out_pool", l), anyspec, anyspec],
        out_specs=[rows(SSM_WIDTH), _full((N, SSM_STATE_LANES)), rows(POOL_WIDTH)],
        out_shape=[jax.ShapeDtypeStruct(sh.shape, sh.dtype), jax.ShapeDtypeStruct((N, SSM_STATE_LANES), F32),
                   jax.ShapeDtypeStruct(ph.shape, ph.dtype)],
        input_output_aliases={14: 0, 15: 2},
        compiler_params=_params("arbitrary"),
        name="mix_sample",
    )(uh, h0, pw, bblk, cblk, vecs, wg, vecs, vecs, xph, pbuf, wp, vecs, vecs, sh, ph)


def _outproj_rows(x_ref, a_ref, s_ref, p_ref, w_ref, o_ref):
    o1 = ATTN_WIDTH
    o2 = o1 + SSM_WIDTH
    acc = x_ref[...]
    acc = acc + jnp.dot(a_ref[...], w_ref[0:o1, :], preferred_element_type=F32)
    acc = acc + jnp.dot(s_ref[...], w_ref[o1:o2, :], preferred_element_type=F32)
    acc = acc + jnp.dot(p_ref[...], w_ref[o2:, :], preferred_element_type=F32)
    o_ref[...] = acc


def _outproj_kernel(xm_ref, am_ref, sm_ref, pm_ref, xh_ref, ah_ref, sh_ref, ph_ref, w_ref, om_ref, oh_ref):
    i = pl.program_id(0)
    last = pl.num_programs(0) - 1

    @pl.when(i < last)
    def _():
        _outproj_rows(xm_ref, am_ref, sm_ref, pm_ref, w_ref, om_ref)

    @pl.when(i == last)
    def _():
        _outproj_rows(xh_ref, ah_ref, sh_ref, ph_ref, w_ref, oh_ref)


def _outproj(xm, am, sm, pm, xh, ah, sh, ph, w, tm):
    R, RH = xm.shape[0], xh.shape[0]
    nm = R // tm
    row = lambda i: (jnp.minimum(i, nm - 1), 0)
    once = pl.Buffered(1)
    head = lambda w_: pl.BlockSpec((RH, w_), lambda i: (0, 0), pipeline_mode=once)
    return pl.pallas_call(
        _outproj_kernel,
        grid=(nm + 1,),
        in_specs=[pl.BlockSpec((tm, D_MODEL), row), pl.BlockSpec((tm, ATTN_WIDTH), row),
                  pl.BlockSpec((tm, SSM_WIDTH), row), pl.BlockSpec((tm, POOL_WIDTH), row),
                  head(D_MODEL), head(ATTN_WIDTH), head(SSM_WIDTH), head(POOL_WIDTH),
                  pl.BlockSpec((D_MODEL, D_MODEL), lambda i: (0, 0), pipeline_mode=once)],
        out_specs=[pl.BlockSpec((tm, D_MODEL), row), pl.BlockSpec((RH, D_MODEL), lambda i: (0, 0))],
        out_shape=[jax.ShapeDtypeStruct((R, D_MODEL), F32), jax.ShapeDtypeStruct((RH, D_MODEL), F32)],
        compiler_params=_params("arbitrary"),
        name="outproj",
    )(xm, am, sm, pm, xh, ah, sh, ph, w)


def _ffn_kernel(xm_ref, xh_ref, g_ref, w1_ref, w2_ref, *rest, n_cast):
    cast_in = rest[0:n_cast]
    om_ref, oh_ref = rest[n_cast:n_cast + 2]
    cast_out = rest[n_cast + 2:2 * n_cast + 2]
    hm_scr, hh_scr = rest[2 * n_cast + 2:]
    i = pl.program_id(0)
    f = pl.program_id(1)
    for src, dst in zip(cast_in, cast_out):
        dst[...] = src[...].astype(BF16)

    def mlp(h):
        h1 = jnp.dot(h, w1_ref[...], preferred_element_type=F32)
        return jnp.dot(jnp.square(jnp.maximum(h1, 0.0)).astype(BF16), w2_ref[...], preferred_element_type=F32)

    def first(x_ref, h_scr, o_ref):
        x = x_ref[...]
        h = _rms(x, g_ref[...]).astype(BF16)
        h_scr[...] = h
        o_ref[...] = x + mlp(h)

    @pl.when(f == 0)
    def _():
        first(xm_ref, hm_scr, om_ref)

    @pl.when(f > 0)
    def _():
        om_ref[...] += mlp(hm_scr[...])

    @pl.when((i == 0) & (f == 0))
    def _():
        first(xh_ref, hh_scr, oh_ref)

    @pl.when((i == 0) & (f > 0))
    def _():
        oh_ref[...] += mlp(hh_scr[...])


def _ffn(xm, xh, g, l, w1, w2, tm, tf, casts=()):
    R, RH = xm.shape[0], xh.shape[0]
    nf = D_FF // tf
    steps = (R // tm) * nf
    step = lambda i, f: (i * nf + f, 0)
    cast_specs = []
    for w, wl in casts:
        _, cr, cc = w.shape
        rows = cr // steps
        cast_specs.append((pl.BlockSpec((None, rows, cc), lambda i, f, wl=wl: (wl, i * nf + f, 0)),
                           pl.BlockSpec((rows, cc), step), jax.ShapeDtypeStruct((cr, cc), BF16)))
    res = pl.pallas_call(
        functools.partial(_ffn_kernel, n_cast=len(casts)),
        grid=(R // tm, nf),
        in_specs=[pl.BlockSpec((tm, D_MODEL), lambda i, f: (i, 0)),
                  pl.BlockSpec((RH, D_MODEL), lambda i, f: (0, 0), pipeline_mode=pl.Buffered(1)),
                  _vrow("g_ffn", l),
                  pl.BlockSpec((D_MODEL, tf), lambda i, f: (0, f)),
                  pl.BlockSpec((tf, D_MODEL), lambda i, f: (f, 0))] + [c[0] for c in cast_specs],
        out_specs=[pl.BlockSpec((tm, D_MODEL), lambda i, f: (i, 0)),
                   pl.BlockSpec((RH, D_MODEL), lambda i, f: (0, 0))] + [c[1] for c in cast_specs],
        out_shape=[jax.ShapeDtypeStruct((R, D_MODEL), F32), jax.ShapeDtypeStruct((RH, D_MODEL), F32)]
        + [c[2] for c in cast_specs],
        scratch_shapes=[pltpu.VMEM((tm, D_MODEL), BF16), pltpu.VMEM((RH, D_MODEL), BF16)],
        compiler_params=_params("arbitrary", "arbitrary"),
        name="ffn",
    )(xm, xh, g, w1, w2, *[w for w, _ in casts])
    return res[0], res[1], res[2:]


def _rope_tables(pos):
    half = ROT_HALF
    inv = ROPE_THETA ** (-np.arange(0, ROT_DIM, 2, dtype=np.float64) / ROT_DIM)
    ang = np.asarray(pos, np.float64)[:, None] * inv
    cos, sin = np.cos(ang), np.sin(ang)
    n = ang.shape[0]
    z = np.zeros((n, HEAD_DIM - ROT_DIM))
    zh = np.zeros((n, half))
    rc = np.concatenate([cos, cos, z + 1.0], axis=1)
    rs1 = np.concatenate([zh, sin, z], axis=1)
    rs2 = np.concatenate([-sin, zh, z], axis=1)
    return tuple(t.astype(np.float32) for t in (rc, rs1, rs2))


def kernel(x_prompt, x_sample, cache_k, cache_v, state_ssm_re, state_ssm_im, state_pool, meta_tokens, g_mix, w_in, g_q, g_k, sinks, A_re, A_im, log_dt, B_re, B_im, C_re, C_im, D_skip, w_glu, b_glu, w_pool, pool_scale, g_out_attn, g_out_ssm, g_out_pool, w_out, g_ffn, w_ff1, w_ff2):
    B, T, _ = x_prompt.shape
    N = x_sample.shape[0]
    depth = w_in.shape[0]
    assert N <= SEQ0 and N % DEC_STEP == 0 and T % TM_FFN == 0
    meta = meta_tokens.astype(F32)
    head0 = jnp.concatenate([x_sample.reshape(N, D_MODEL), jnp.zeros((SEQ0 - N, D_MODEL), F32), meta], axis=0)
    head_rest = jnp.concatenate([jnp.zeros((SEQ0, D_MODEL), F32), meta], axis=0)
    xh = jnp.concatenate([head0] + [head_rest] * (B - 1), axis=0)
    xm = x_prompt.reshape(B * T, D_MODEL)

    rope_m = _rope_tables(N_META + np.arange(T))
    hr_ = np.arange(HEAD_ROWS)
    pos_h0 = np.where(hr_ < N, PAST_LEN, np.maximum(hr_ - SEQ0, 0))
    pos_h = np.concatenate([pos_h0] + [np.maximum(hr_ - SEQ0, 0)] * (B - 1))
    rope_h = _rope_tables(pos_h)

    wi = w_in[0].astype(BF16)
    wg_all, wp_all = w_glu.astype(BF16), w_pool.astype(BF16)
    ck = cache_k.astype(F32)
    cv = cache_v.astype(F32)
    RH = B * HEAD_ROWS

    named = dict(g_mix=g_mix, g_ffn=g_ffn, g_out_attn=g_out_attn, D_skip=D_skip, b_glu=b_glu,
                 g_out_ssm=g_out_ssm, pool_scale=pool_scale, g_out_pool=g_out_pool, g_q=g_q, g_k=g_k)
    vecs = jnp.concatenate([named[name].astype(F32) for name, _ in VEC_LAYOUT], axis=1)[:, None, :]
    pw, bblk, cblk = _ssm_params(A_re, A_im, log_dt, B_re, B_im, C_re, C_im)
    sinks_flat = sinks.astype(F32).reshape(depth * N_HEADS)
    sinks_col = sinks.astype(F32).reshape(depth, N_KV_HEADS, GQA_GROUP, 1)
    bias = _attn_bias()
    h0 = jnp.concatenate([state_ssm_re.astype(F32).reshape(depth, N, SSM_LANE_BLOCKS, SSM_BLOCK_STATES),
                          state_ssm_im.astype(F32).reshape(depth, N, SSM_LANE_BLOCKS, SSM_BLOCK_STATES)],
                         axis=-1).reshape(depth, N, SSM_STATE_LANES)
    pbuf = state_pool.astype(F32).transpose(0, 2, 1, 3)

    nk, nv = _shift_caches(ck, cv)
    ks, vs, pls, sts, st_ss, phs = ([] for _ in range(6))
    for l in range(depth):
        (qm, km, vm, um, pm), (qh, kh, vh, uh, ph), w1_new = _inproj(
            xm, xh, vecs, wi, l, rope_m, rope_h, TM_PROJ, w_ff1 if l == 0 else None)
        if l == 0:
            w1 = w1_new

        (am, ah, sm, sh, st, plm, plh), wcast = _mixers(
            sinks_flat, (qm, km, vm), (qh, kh, vh), um, uh, pm, ph, vecs, bias, pw, bblk, cblk, wg_all, wp_all,
            l, B, TM_SEQ, [(w_ff2, l), (w_out, l)] if l == 0 else [])
        if l == 0:
            w2, wo = wcast

        ah, nk, nv = _attn_sample(qh, kh, vh, ck, cv, l, sinks_col, vecs, ah, nk, nv, N)
        sh, st_s, plh = _mix_sample(uh, ph, h0, pbuf, vecs, pw, bblk, cblk, wg_all, wp_all, l, sh, plh)

        xm, xh = _outproj(xm, am, sm, plm, xh, ah, sh, plh, wo, TM_PROJ)
        nxt = [(w, l + 1) for w in (w_in, w_ff1, w_ff2, w_out)] if l + 1 < depth else []
        xm, xh, wnext = _ffn(xm, xh, vecs, l, w1, w2, TM_FFN, TF_FFN, nxt)
        if nxt:
            wi, w1, w2, wo = wnext

        ks.append(km.reshape(B, T, KV_WIDTH)[:, T - WINDOW:])
        vs.append(vm.reshape(B, T, KV_WIDTH)[:, T - WINDOW:])
        pls.append(pm.reshape(B, T, POOL_WIDTH)[:, T - POOL_BUF:])
        sts.append(st[:, 0])
        st_ss.append(st_s)
        phs.append(ph[:N])

    y_prompt = xm.reshape(B, T, D_MODEL)
    y_sample = xh[:N].reshape(N, 1, D_MODEL)
    heads = lambda t: jnp.stack(t).reshape(depth, -1, WINDOW, N_KV_HEADS, HEAD_DIM)
    p_re, p_im = _state_from_lanes(jnp.stack(sts).reshape(depth * B, SSM_STATE_LANES))
    s_re, s_im = _state_from_lanes(jnp.stack(st_ss).reshape(depth * N, SSM_STATE_LANES))
    st4 = lambda t, n: t.reshape(depth, n, SSM_GROUPS, SSM_STATE)
    s_pool = jnp.concatenate([state_pool.astype(F32)[:, :, 1:], jnp.stack(phs)[:, :, None]], axis=2)
    return (y_prompt, y_sample, heads(ks), heads(vs), st4(p_re, B), st4(p_im, B), jnp.stack(pls),
            nk, nv, st4(s_re, N), st4(s_im, N), s_pool)
```

```python
import functools
import math

import jax
import jax.numpy as jnp
import numpy as np
from jax.experimental import pallas as pl
from jax.experimental.pallas import tpu as pltpu

D_MODEL = 2048
N_META = 16
HEAD_DIM = 128
N_HEADS = 8
N_KV_HEADS = 2
GQA_GROUP = 4
ATTN_WIDTH = 1024
KV_WIDTH = 256
WINDOW = 128
BLOCK = 128
ROT_DIM = 32
ROT_HALF = ROT_DIM // 2
ROPE_THETA = 500000.0
SSM_WIDTH = 512
SSM_GROUP_SIZE = 16
SSM_GROUPS = 32
SSM_STATE = 64
POOL_WIDTH = 512
POOL_WINDOWS = (2, 4, 8, 16)
POOL_GROUP = 128
POOL_BUF = 15
POOL_HALO = 16
IN_WIDTH = 2560
D_FF = 8192
EPS = 1e-6
PAST_LEN = 16384
LOG2E = math.log2(math.e)

SEQ0 = BLOCK - N_META
LANES = 128
SUBLANES = 8
SSM_LANE_BLOCKS = SSM_WIDTH // LANES
SSM_BLOCK_STATES = (LANES // SSM_GROUP_SIZE) * SSM_STATE
SSM_STATE_LANES = SSM_LANE_BLOCKS * 2 * SSM_BLOCK_STATES
VMEM_LIMIT = 60 * 1024 * 1024

TM_PROJ = 512
TM_FFN = 1024
TF_FFN = 512
TM_SEQ = 512
DEC_STEP = 8
SSM_POW_ROWS = (1, N_META // SUBLANES, TM_SEQ // SUBLANES)
POW_ROW_HEAD, POW_ROW_MAIN = 1, 2

BF16 = jnp.bfloat16
F32 = jnp.float32


def _params(*semantics):
    return pltpu.CompilerParams(dimension_semantics=semantics, vmem_limit_bytes=VMEM_LIMIT)


def _rms(x, g):
    return x * jax.lax.rsqrt(jnp.mean(x * x, axis=-1, keepdims=True) + EPS) * g


def _full(shape):
    n = len(shape)
    return pl.BlockSpec(shape, lambda *_: (0,) * n)


def _layer(shape, l):
    n = len(shape)
    return pl.BlockSpec((None, *shape), lambda *_: (l,) + (0,) * n)


VEC_LAYOUT = (("g_mix", D_MODEL), ("g_ffn", D_MODEL), ("g_out_attn", ATTN_WIDTH), ("D_skip", SSM_WIDTH),
              ("b_glu", SSM_WIDTH), ("g_out_ssm", SSM_WIDTH), ("pool_scale", POOL_WIDTH),
              ("g_out_pool", POOL_WIDTH), ("g_q", HEAD_DIM), ("g_k", HEAD_DIM))
VEC_WIDTH = dict(VEC_LAYOUT)
VEC_OFFSET = {name: sum(w for _, w in VEC_LAYOUT[:i]) for i, (name, _) in enumerate(VEC_LAYOUT)}
assert all(VEC_OFFSET[name] % w == 0 for name, w in VEC_LAYOUT)


def _vrow(name, l):
    w = VEC_WIDTH[name]
    return pl.BlockSpec((None, 1, w), lambda *_: (l, 0, VEC_OFFSET[name] // w))


def _inproj_rows(x, g_ref, w_ref, gq_ref, gk_ref, rope_refs, out_refs):
    q_ref, k_ref, v_ref, u_ref, xp_ref = out_refs
    h = _rms(x, g_ref[...]).astype(BF16)
    proj = jnp.dot(h, w_ref[...], preferred_element_type=F32)
    rc, rs1, rs2 = (r[...] for r in rope_refs)

    def head(t, g):
        t = _rms(t, g)
        return t * rc + pltpu.roll(t, ROT_HALF, 1) * rs1 + pltpu.roll(t, LANES - ROT_HALF, 1) * rs2

    for hd in range(N_HEADS):
        sl = slice(hd * HEAD_DIM, (hd + 1) * HEAD_DIM)
        q_ref[:, sl] = head(proj[:, sl], gq_ref[...])
    for hd in range(N_KV_HEADS):
        sl = slice(hd * HEAD_DIM, (hd + 1) * HEAD_DIM)
        k_ref[:, sl] = head(proj[:, ATTN_WIDTH + hd * HEAD_DIM:ATTN_WIDTH + (hd + 1) * HEAD_DIM], gk_ref[...])
    o2 = ATTN_WIDTH + KV_WIDTH
    o3 = o2 + KV_WIDTH
    o4 = o3 + SSM_WIDTH
    v_ref[...] = proj[:, o2:o3]
    for j in range(SSM_LANE_BLOCKS):
        u_ref[j] = proj[:, o3 + j * LANES:o3 + (j + 1) * LANES]
    xp_ref[...] = proj[:, o4:]


def _inproj_kernel(xm_ref, xh_ref, g_ref, w_ref, gq_ref, gk_ref, rcm_ref, rs1m_ref, rs2m_ref,
                   rch_ref, rs1h_ref, rs2h_ref, *rest):
    cast = len(rest) == 12
    outs = rest[1:] if cast else rest
    main_outs, head_outs = outs[0:5], outs[5:10]
    i = pl.program_id(0)
    last = pl.num_programs(0) - 1

    @pl.when(i < last)
    def _():
        if cast:
            outs[10][...] = rest[0][...].astype(BF16)
        _inproj_rows(xm_ref[...], g_ref, w_ref, gq_ref, gk_ref, (rcm_ref, rs1m_ref, rs2m_ref), main_outs)

    @pl.when(i == last)
    def _():
        _inproj_rows(xh_ref[...], g_ref, w_ref, gq_ref, gk_ref, (rch_ref, rs1h_ref, rs2h_ref), head_outs)


def _inproj(xm, xh, vecs, w, l, rope_m, rope_h, tm, cast=None):
    R, RH = xm.shape[0], xh.shape[0]
    nm = R // tm
    tiles_per_rope = rope_m[0].shape[0] // tm
    tile = lambda i: jnp.minimum(i, nm - 1)
    row = lambda i: (tile(i), 0)
    rrow = lambda i: (tile(i) % tiles_per_rope, 0)
    once = pl.Buffered(1)
    full = lambda n, w_: pl.BlockSpec((n, w_), lambda i: (0, 0))

    def out_set(n, rows, im, uim):
        flat = lambda w_: (pl.BlockSpec((n, w_), im), jax.ShapeDtypeStruct((rows, w_), F32))
        u_out = (pl.BlockSpec((SSM_LANE_BLOCKS, n, LANES), uim),
                 jax.ShapeDtypeStruct((SSM_LANE_BLOCKS, rows, LANES), F32))
        return [flat(ATTN_WIDTH), flat(KV_WIDTH), flat(KV_WIDTH), u_out, flat(POOL_WIDTH)]

    outs = (out_set(tm, R, row, lambda i: (0, tile(i), 0))
            + out_set(RH, RH, lambda i: (0, 0), lambda i: (0, 0, 0)))
    in_specs = [pl.BlockSpec((tm, D_MODEL), row),
                pl.BlockSpec((RH, D_MODEL), lambda i: (0, 0), pipeline_mode=once),
                _vrow("g_mix", l),
                pl.BlockSpec((D_MODEL, IN_WIDTH), lambda i: (0, 0), pipeline_mode=once),
                _vrow("g_q", l), _vrow("g_k", l),
                pl.BlockSpec((tm, LANES), rrow), pl.BlockSpec((tm, LANES), rrow), pl.BlockSpec((tm, LANES), rrow),
                full(RH, LANES), full(RH, LANES), full(RH, LANES)]
    args = [xm, xh, vecs, w, vecs, vecs, *rope_m, *rope_h]
    if cast is not None:
        _, cr, cc = cast.shape
        crows = cr // nm
        in_specs.append(pl.BlockSpec((None, crows, cc), lambda i: (l, tile(i), 0)))
        args.append(cast)
        outs.append((pl.BlockSpec((crows, cc), row), jax.ShapeDtypeStruct((cr, cc), BF16)))
    res = pl.pallas_call(
        _inproj_kernel,
        grid=(nm + 1,),
        in_specs=in_specs,
        out_specs=[o[0] for o in outs],
        out_shape=[o[1] for o in outs],
        compiler_params=_params("arbitrary"),
        name="inproj",
    )(*args)
    return res[0:5], res[5:10], (res[10] if cast is not None else None)


def _attn_bias():
    rows = GQA_GROUP * BLOCK
    i, r, c = np.meshgrid(np.arange(3), np.arange(rows) % BLOCK, np.arange(2 * BLOCK), indexing="ij")
    diff = BLOCK + r - c
    krow = (i - 1) * BLOCK + c
    mask = (diff >= 0) & (diff <= WINDOW) & (krow >= SEQ0)
    return np.where(mask, 0.0, -np.inf).astype(np.float32)


def _attn_block(q_blk, kp_blk, kc_blk, vp_blk, vc_blk, bias, sink_ref, l, g):
    rows = GQA_GROUP * BLOCK
    rgrp = jax.lax.broadcasted_iota(jnp.int32, (rows, 1), 0) // BLOCK
    outs = []
    for kh in range(N_KV_HEADS):
        ksl = slice(kh * HEAD_DIM, (kh + 1) * HEAD_DIM)
        qh = jnp.concatenate(
            [q_blk[:, (kh * GQA_GROUP + h) * HEAD_DIM:(kh * GQA_GROUP + h + 1) * HEAD_DIM]
             for h in range(GQA_GROUP)], axis=0).astype(BF16)
        kk = jnp.concatenate([kp_blk[:, ksl], kc_blk[:, ksl]], axis=0).astype(BF16)
        vv = jnp.concatenate([vp_blk[:, ksl], vc_blk[:, ksl]], axis=0).astype(BF16)
        s = jax.lax.dot_general(qh, kk, (((1,), (1,)), ((), ())),
                                preferred_element_type=F32) * (HEAD_DIM ** -0.5 * LOG2E) + bias
        sk = jnp.zeros((rows, 1), F32)
        for h in range(GQA_GROUP):
            sk = jnp.where(rgrp == h, sink_ref[l * N_HEADS + kh * GQA_GROUP + h] * LOG2E, sk)
        m = jnp.maximum(jnp.max(s, axis=-1, keepdims=True), sk)
        p = jnp.exp2(s - m)
        denom = jnp.sum(p, axis=-1, keepdims=True) + jnp.exp2(sk - m)
        o = jnp.dot(p.astype(BF16), vv, preferred_element_type=F32) / denom
        outs.extend(o[h * BLOCK:(h + 1) * BLOCK] for h in range(GQA_GROUP))
    return _rms(jnp.concatenate(outs, axis=1), g)


def _ssm_params_kernel(ar_ref, ai_ref, ldt_ref, kk_ref, br_ref, bi_ref, tr_ref, ti_ref, bbr_ref, bbi_ref):
    ar, ai = ar_ref[...], ai_ref[...]
    dt = jnp.exp(ldt_ref[...])
    kk = kk_ref[...]
    mag = jnp.exp(dt * ar * kk)
    ang = dt * ai * kk
    tr = mag * jnp.cos(ang)
    ti = mag * jnp.sin(ang)
    tr_ref[...] = tr
    ti_ref[...] = ti
    abr, abi = tr[0:1], ti[0:1]
    den = ar * ar + ai * ai
    fr = ((abr - 1.0) * ar + abi * ai) / den
    fi = (abi * ar - (abr - 1.0) * ai) / den
    br, bi = br_ref[...], bi_ref[...]
    bbr_ref[...] = fr * br - fi * bi
    bbi_ref[...] = fr * bi + fi * br


def _ssm_params(A_re, A_im, log_dt, B_re, B_im, C_re, C_im):
    depth = A_re.shape[0]
    n = depth * SSM_GROUPS * SSM_STATE
    row = lambda t: t.astype(F32).reshape(1, n)
    ldt = jnp.broadcast_to(log_dt.astype(F32)[:, :, None], (depth, SSM_GROUPS, SSM_STATE)).reshape(1, n)
    kk = jnp.array(SSM_POW_ROWS + (0,) * (SUBLANES - len(SSM_POW_ROWS)), F32).reshape(SUBLANES, 1)
    chan_first = lambda t: t.astype(F32).reshape(n, SSM_GROUP_SIZE).T
    shapes = [(SUBLANES, n), (SUBLANES, n), (SSM_GROUP_SIZE, n), (SSM_GROUP_SIZE, n)]
    tr, ti, bbr, bbi = pl.pallas_call(
        _ssm_params_kernel,
        out_shape=[jax.ShapeDtypeStruct(s, F32) for s in shapes],
        name="ssm_params",
    )(row(A_re), row(A_im), ldt, kk, chan_first(B_re), chan_first(B_im))
    J, G8 = SSM_LANE_BLOCKS, LANES // SSM_GROUP_SIZE

    def lanes(t):
        return t.reshape(SUBLANES, depth, J, SSM_BLOCK_STATES).transpose(1, 0, 2, 3)

    pw = jnp.concatenate([lanes(tr), lanes(ti)], axis=-1).reshape(depth, SUBLANES, SSM_STATE_LANES)
    eye = jnp.eye(G8, dtype=F32)

    def bdiag(t):
        t = t.reshape(SSM_GROUP_SIZE, depth, J, G8, SSM_STATE).transpose(1, 2, 3, 0, 4)
        t = t[:, :, :, :, None, :] * eye[None, None, :, None, :, None]
        return t.reshape(depth, J, LANES, SSM_BLOCK_STATES)

    bblk = jnp.concatenate([bdiag(bbr), bdiag(bbi)], axis=-1).astype(BF16)

    def cdiag(t):
        t = t.astype(F32).reshape(depth, J, G8, SSM_GROUP_SIZE, SSM_STATE).transpose(0, 1, 2, 4, 3)
        t = t[:, :, :, :, None, :] * eye[None, None, :, None, :, None]
        return t.reshape(depth, J, SSM_BLOCK_STATES, LANES)

    cblk = jnp.concatenate([cdiag(C_re), -cdiag(C_im)], axis=2).astype(BF16)
    return pw, bblk, cblk


def _ssm_tail(y, u, d_ref, wg_ref, bg_ref, g_ref):
    y = y + d_ref[...] * u
    z = jax.nn.gelu(y)
    gate = jax.nn.sigmoid(jnp.dot(z.astype(BF16), wg_ref[...], preferred_element_type=F32) + bg_ref[...])
    return _rms(z * gate, g_ref[...])


def _ssm_sweep(x_scr, n, a_tabs, init, store):
    S = SSM_BLOCK_STATES
    fins = []
    for j0 in range(0, SSM_LANE_BLOCKS, 2):
        js = (j0, j0 + 1)

        def body(k, carry, js=js):
            r0 = pl.multiple_of(k * SUBLANES, SUBLANES)
            out = []
            for idx, j in enumerate(js):
                hr, hi = carry[2 * idx], carry[2 * idx + 1]
                base = j * 2 * S
                ar, ai = a_tabs[j]
                nhr = ar * hr - ai * hi + x_scr[pl.ds(r0, SUBLANES), base:base + S]
                nhi = ar * hi + ai * hr + x_scr[pl.ds(r0, SUBLANES), base + S:base + 2 * S]
                if store:
                    x_scr[pl.ds(r0, SUBLANES), base:base + S] = nhr
                    x_scr[pl.ds(r0, SUBLANES), base + S:base + 2 * S] = nhi
                out += [nhr, nhi]
            return tuple(out)

        c0 = tuple(t for j in js for t in init[j])
        res = jax.lax.fori_loop(0, n // SUBLANES, body, c0, unroll=True)
        fins += [(res[0], res[1]), (res[2], res[3])]
    return fins


def _ssm_rows(u, pow_row, pw_ref, bblk_ref, cblk_ref, x_scr, s_scr, carry_scr):
    n = u.shape[0]
    S = SSM_BLOCK_STATES
    ub = u.astype(BF16)
    for j in range(SSM_LANE_BLOCKS):
        x_scr[0:n, j * 2 * S:(j + 1) * 2 * S] = jnp.dot(ub[:, j * LANES:(j + 1) * LANES], bblk_ref[j],
                                                       preferred_element_type=F32)
    bc = lambda t: jnp.broadcast_to(t, (SUBLANES, S))
    re = lambda ref, r0, r1, j: ref[r0:r1, j * 2 * S:j * 2 * S + S]
    im = lambda ref, r0, r1, j: ref[r0:r1, j * 2 * S + S:(j + 1) * 2 * S]
    a_tabs = [(bc(re(pw_ref, 0, 1, j)), bc(im(pw_ref, 0, 1, j))) for j in range(SSM_LANE_BLOCKS)]
    zero = jnp.zeros((SUBLANES, S), F32)
    fins = _ssm_sweep(x_scr, n, a_tabs, [(zero, zero)] * SSM_LANE_BLOCKS, store=False)
    for j in range(SSM_LANE_BLOCKS):
        base = j * 2 * S
        cr, ci = re(pw_ref, pow_row, pow_row + 1, j), im(pw_ref, pow_row, pow_row + 1, j)
        sr, si = re(carry_scr, 0, 1, j), im(carry_scr, 0, 1, j)
        fr, fi = fins[j]
        for c in range(SUBLANES):
            s_scr[c:c + 1, base:base + S] = sr
            s_scr[c:c + 1, base + S:base + 2 * S] = si
            sr, si = cr * sr - ci * si + fr[c:c + 1], cr * si + ci * sr + fi[c:c + 1]
        carry_scr[:, base:base + S] = bc(sr)
        carry_scr[:, base + S:base + 2 * S] = bc(si)
    init = [(re(s_scr, 0, SUBLANES, j), im(s_scr, 0, SUBLANES, j)) for j in range(SSM_LANE_BLOCKS)]
    _ssm_sweep(x_scr, n, a_tabs, init, store=True)
    ys = [jnp.dot(x_scr[0:n, j * 2 * S:(j + 1) * 2 * S].astype(BF16), cblk_ref[j], preferred_element_type=F32)
          for j in range(SSM_LANE_BLOCKS)]
    return jnp.concatenate(ys, axis=1)


def _ssm_tile(u_ref, n, pow_row, refs, o_ref, scr):
    pw_ref, bblk_ref, cblk_ref, d_ref, wg_ref, bg_ref, g_ref = refs
    up_scr, x_scr, s_scr, carry_scr, o_scr = scr
    q = n // SUBLANES
    for j in range(SSM_LANE_BLOCKS):
        for k in range(q):
            up_scr[k * SUBLANES:(k + 1) * SUBLANES, j * LANES:(j + 1) * LANES] = \
                u_ref[j, pl.ds(k, SUBLANES, stride=q), :]
    u = up_scr[0:n, :]
    y = _ssm_rows(u, pow_row, pw_ref, bblk_ref, cblk_ref, x_scr, s_scr, carry_scr)
    out = _ssm_tail(y, u, d_ref, wg_ref, bg_ref, g_ref)
    for j in range(SSM_LANE_BLOCKS):
        for k in range(q):
            o_scr[j, pl.ds(k, SUBLANES, stride=q), :] = out[k * SUBLANES:(k + 1) * SUBLANES,
                                                            j * LANES:(j + 1) * LANES]
    o_ref[...] = jnp.concatenate([o_scr[j, 0:n, :] for j in range(SSM_LANE_BLOCKS)], axis=1).astype(o_ref.dtype)


def _mixers_kernel(sink_ref,
                   qm_ref, qh_ref, kpm_ref, km_ref, kh_ref, vpm_ref, vm_ref, vh_ref, ga_ref, bias_ref,
                   um_ref, uh_ref, pw_ref, bblk_ref, cblk_ref, d_ref, wg_ref, bg_ref, gs_ref,
                   xm_ref, halo_ref, xh_ref, wp_ref, sc_ref, gp_ref, *rest, l, n_cast):
    cast_in = rest[0:n_cast]
    am_ref, ah_ref, som_ref, soh_ref, st_ref, pom_ref, poh_ref = rest[n_cast:n_cast + 7]
    cast_out = rest[n_cast + 7:2 * n_cast + 7]
    scr = rest[2 * n_cast + 7:]
    carry_scr = scr[3]
    t = pl.program_id(1)
    tm = xm_ref.shape[0]
    ssm_refs = (pw_ref, bblk_ref, cblk_ref, d_ref, wg_ref, bg_ref, gs_ref)
    blk = functools.partial(_attn_block, sink_ref=sink_ref, l=l, g=ga_ref[...])

    block = lambda ref: jnp.concatenate([jnp.zeros((SEQ0, ref.shape[1]), F32), ref[...]], axis=0)

    @pl.when(t == 0)
    def _():
        a = blk(block(qh_ref), kpm_ref[...], block(kh_ref), vpm_ref[...], block(vh_ref), bias_ref[0])
        ah_ref[...] = a[SEQ0:].astype(ah_ref.dtype)
        carry_scr[...] = jnp.zeros_like(carry_scr)
        _ssm_tile(uh_ref, N_META, POW_ROW_HEAD, ssm_refs, soh_ref, scr)
        prev = jnp.zeros((POOL_HALO, POOL_WIDTH), F32)
        poh_ref[...] = _pool_rows(xh_ref[...], prev, 0, wp_ref, sc_ref, gp_ref).astype(poh_ref.dtype)

    @pl.when(t > 0)
    def _():
        for src, dst in zip(cast_in, cast_out):
            dst[...] = src[...].astype(BF16)
        kp = jnp.where(t == 1, block(kh_ref), kpm_ref[...])
        vp = jnp.where(t == 1, block(vh_ref), vpm_ref[...])
        for n in range(tm // BLOCK):
            rows = slice(n * BLOCK, (n + 1) * BLOCK)
            kc, vc = km_ref[rows, :], vm_ref[rows, :]
            bias = bias_ref[jnp.minimum(t, 2)] if n == 0 else bias_ref[2]
            am_ref[rows, :] = blk(qm_ref[rows, :], kp, kc, vp, vc, bias).astype(am_ref.dtype)
            kp, vp = kc, vc
        _ssm_tile(um_ref, um_ref.shape[1], POW_ROW_MAIN, ssm_refs, som_ref, scr)
        st_ref[...] = carry_scr[...]
        prev = jnp.where(t == 1, xh_ref[...], halo_ref[...])
        pom_ref[...] = _pool_rows(xm_ref[...], prev, N_META + (t - 1) * tm, wp_ref, sc_ref, gp_ref).astype(pom_ref.dtype)


def _mixers(sinks, qkv_m, qkv_h, um, uh, xm, xh, vecs, bias, pw, bblk, cblk, wg, wp, l, B, tm, casts):
    assert (1, N_META // SUBLANES, tm // SUBLANES) == SSM_POW_ROWS and N_META == POOL_HALO
    J = SSM_LANE_BLOCKS
    rm, rh = um.shape[1], uh.shape[1]
    nt = rm // (B * tm)
    r = tm // POOL_HALO
    bpt = tm // BLOCK
    tile = lambda b, t: b * nt + jnp.maximum(t - 1, 0)
    main = lambda b, t, _: (tile(b, t), 0)
    n_dec = uh.shape[1] - B * N_META
    head = lambda b, t, _: (n_dec // N_META + b, 0)
    halo = lambda b, t, _: (jnp.maximum((b * nt + t - 1) * r - 1, 0), 0)
    prevb = lambda b, t, _: (jnp.maximum(tile(b, t) * bpt - 1, 0), 0)
    cast_specs = []
    for w, wl in casts:
        _, cr, cc = w.shape
        rows = cr // (B * nt)
        cast_specs.append((pl.BlockSpec((None, rows, cc), lambda b, t, _, wl=wl: (wl, tile(b, t), 0)),
                           pl.BlockSpec((rows, cc), main), jax.ShapeDtypeStruct((cr, cc), BF16)))
    bf = lambda n, w: jax.ShapeDtypeStruct((n, w), BF16)
    tile_spec = lambda w: pl.BlockSpec((tm, w), main)
    head_spec = lambda w: pl.BlockSpec((N_META, w), head)
    in_specs = [tile_spec(ATTN_WIDTH), head_spec(ATTN_WIDTH),
                pl.BlockSpec((BLOCK, KV_WIDTH), prevb), tile_spec(KV_WIDTH), head_spec(KV_WIDTH),
                pl.BlockSpec((BLOCK, KV_WIDTH), prevb), tile_spec(KV_WIDTH), head_spec(KV_WIDTH),
                _vrow("g_out_attn", l), _full(bias.shape),
                pl.BlockSpec((J, tm, LANES), lambda b, t, _: (0, tile(b, t), 0)),
                pl.BlockSpec((J, N_META, LANES), lambda b, t, _: (0, n_dec // N_META + b, 0)),
                _layer(pw.shape[1:], l), _layer(bblk.shape[1:], l), _layer(cblk.shape[1:], l),
                _vrow("D_skip", l), _layer((SSM_WIDTH, SSM_WIDTH), l), _vrow("b_glu", l), _vrow("g_out_ssm", l),
                tile_spec(POOL_WIDTH), pl.BlockSpec((POOL_HALO, POOL_WIDTH), halo), head_spec(POOL_WIDTH),
                _layer(wp.shape[1:], l), _vrow("pool_scale", l), _vrow("g_out_pool", l)]
    in_specs += [c[0] for c in cast_specs]
    out_specs = [tile_spec(ATTN_WIDTH), head_spec(ATTN_WIDTH), tile_spec(SSM_WIDTH), head_spec(SSM_WIDTH),
                 pl.BlockSpec((None, SUBLANES, SSM_STATE_LANES), lambda b, t, _: (b, 0, 0)),
                 tile_spec(POOL_WIDTH), head_spec(POOL_WIDTH)] + [c[1] for c in cast_specs]
    out_shape = [bf(rm, ATTN_WIDTH), bf(rh, ATTN_WIDTH), bf(rm, SSM_WIDTH), bf(rh, SSM_WIDTH),
                 jax.ShapeDtypeStruct((B, SUBLANES, SSM_STATE_LANES), F32),
                 bf(rm, POOL_WIDTH), bf(rh, POOL_WIDTH)] + [c[2] for c in cast_specs]
    (qm, km, vm), (qh, kh, vh) = qkv_m, qkv_h
    res = pl.pallas_call(
        functools.partial(_mixers_kernel, l=l, n_cast=len(casts)),
        grid_spec=pltpu.PrefetchScalarGridSpec(
            num_scalar_prefetch=1, grid=(B, nt + 1), in_specs=in_specs, out_specs=out_specs,
            scratch_shapes=[pltpu.VMEM((tm, SSM_WIDTH), F32),
                            pltpu.VMEM((tm, SSM_STATE_LANES), F32),
                            pltpu.VMEM((SUBLANES, SSM_STATE_LANES), F32),
                            pltpu.VMEM((SUBLANES, SSM_STATE_LANES), F32),
                            pltpu.VMEM((J, tm, LANES), F32)]),
        out_shape=out_shape,
        compiler_params=_params("arbitrary", "arbitrary"),
        name="mixers",
    )(sinks, qm, qh, km, km, kh, vm, vm, vh, vecs, bias, um, uh, pw, bblk, cblk, vecs, wg, vecs, vecs,
      xm, xm, xh, wp, vecs, vecs, *[w for w, _ in casts])
    return res[0:7], res[7:]


def _state_from_lanes(s):
    s = s.reshape(s.shape[0], SSM_LANE_BLOCKS, 2, SSM_BLOCK_STATES)
    return (s[:, :, 0].reshape(-1, SSM_GROUPS, SSM_STATE), s[:, :, 1].reshape(-1, SSM_GROUPS, SSM_STATE))


def _pool_tail(d_groups, w_ref, sc_ref, g_ref):
    y = jnp.concatenate(
        [jnp.dot(d.astype(BF16), w_ref[gi], preferred_element_type=F32) for gi, d in enumerate(d_groups)], axis=1)
    return _rms(y * sc_ref[...], g_ref[...])


def _pool_rows(x, prev, pos0, w_ref, sc_ref, g_ref):
    n = x.shape[0]
    xe = jnp.concatenate([prev, x], axis=0)
    pos = pos0 + jax.lax.broadcasted_iota(jnp.int32, (n, 1), 0)
    ds = []
    for gi, w in enumerate(POOL_WINDOWS):
        gsl = slice(gi * POOL_GROUP, (gi + 1) * POOL_GROUP)
        s = xe[:, gsl]
        k = 1
        while k < w:
            s = s + pltpu.roll(s, k, 0)
            k *= 2
        cnt = jnp.clip(pos + 1, 1, w).astype(F32)
        ds.append(s[POOL_HALO:] / cnt - x[:, gsl])
    return _pool_tail(ds, w_ref, sc_ref, g_ref)


def _shift_caches_kernel(k_ref, v_ref, nk_ref, nv_ref):
    for src, dst in ((k_ref, nk_ref), (v_ref, nv_ref)):
        for bb in range(DEC_STEP):
            dst[bb, 0:WINDOW - 1] = src[bb, 1:WINDOW]
            dst[bb, WINDOW - 1] = jnp.zeros((N_KV_HEADS, HEAD_DIM), F32)


def _shift_caches(cache_k, cache_v):
    depth, N = cache_k.shape[:2]
    blk = pl.BlockSpec((None, DEC_STEP, WINDOW, N_KV_HEADS, HEAD_DIM), lambda l, s: (l, s, 0, 0, 0))
    shape = jax.ShapeDtypeStruct(cache_k.shape, F32)
    return pl.pallas_call(
        _shift_caches_kernel,
        grid=(depth, N // DEC_STEP),
        in_specs=[blk, blk],
        out_specs=[blk, blk],
        out_shape=[shape, shape],
        compiler_params=_params("arbitrary", "arbitrary"),
        name="shift_caches",
    )(cache_k, cache_v)


def _attn_sample_kernel(q_ref, kn_ref, vn_ref, kc_ref, vc_ref, sink_ref, g_ref, a_in_ref, nk_in_ref, nv_in_ref,
                        a_ref, nk_ref, nv_ref, acc_scr):
    del a_in_ref, nk_in_ref, nv_in_ref
    step = pl.program_id(0)
    scale = HEAD_DIM ** -0.5
    for bb in range(DEC_STEP):
        outs = []
        for kh in range(N_KV_HEADS):
            ksl = slice(kh * HEAD_DIM, (kh + 1) * HEAD_DIM)
            qh = jnp.concatenate(
                [q_ref[bb:bb + 1, (kh * GQA_GROUP + g) * HEAD_DIM:(kh * GQA_GROUP + g + 1) * HEAD_DIM]
                 for g in range(GQA_GROUP)], axis=0)
            kn = kn_ref[bb:bb + 1, ksl]
            vn = vn_ref[bb:bb + 1, ksl]
            nk_ref[bb, 0, kh:kh + 1, :] = kn
            nv_ref[bb, 0, kh:kh + 1, :] = vn
            kc = kc_ref[bb, :, kh, :]
            vc = vc_ref[bb, :, kh, :]
            sc = jax.lax.dot_general(qh.astype(BF16), kc.astype(BF16), (((1,), (1,)), ((), ())),
                                     preferred_element_type=F32) * scale
            sn = jnp.sum(qh * kn, axis=-1, keepdims=True) * scale
            sk = sink_ref[kh]
            m = jnp.maximum(jnp.maximum(jnp.max(sc, axis=-1, keepdims=True), sn), sk)
            pc = jnp.exp(sc - m)
            pn = jnp.exp(sn - m)
            denom = jnp.sum(pc, axis=-1, keepdims=True) + pn + jnp.exp(sk - m)
            o = jnp.dot(pc.astype(BF16), vc.astype(BF16), preferred_element_type=F32)
            o = (o + pn * vn) / denom
            outs.extend(o[g:g + 1] for g in range(GQA_GROUP))
        a = jnp.concatenate(outs, axis=1)
        acc_scr[pl.ds(step * DEC_STEP + bb, 1), :] = _rms(a, g_ref[...])

    @pl.when(step == pl.num_programs(0) - 1)
    def _():
        a_ref[...] = acc_scr[...].astype(a_ref.dtype)


def _attn_sample(qh, kh, vh, cache_k, cache_v, l, sinks, g, ah, nk, nv, N):
    rows = lambda w: pl.BlockSpec((DEC_STEP, w), lambda s: (s, 0))
    cache = pl.BlockSpec((None, DEC_STEP, WINDOW, N_KV_HEADS, HEAD_DIM), lambda s: (l, s, 0, 0, 0))
    last = pl.BlockSpec((None, DEC_STEP, 1, N_KV_HEADS, HEAD_DIM), lambda s: (l, s, WINDOW - 1, 0, 0))
    anyspec = pl.BlockSpec(memory_space=pl.ANY)
    return pl.pallas_call(
        _attn_sample_kernel,
        grid=(N // DEC_STEP,),
        in_specs=[rows(ATTN_WIDTH), rows(KV_WIDTH), rows(KV_WIDTH), cache, cache,
                  _layer((N_KV_HEADS, GQA_GROUP, 1), l), _vrow("g_out_attn", l),
                  anyspec, anyspec, anyspec],
        out_specs=[pl.BlockSpec((N, ATTN_WIDTH), lambda s: (0, 0)), last, last],
        out_shape=[jax.ShapeDtypeStruct(ah.shape, ah.dtype), jax.ShapeDtypeStruct(nk.shape, nk.dtype),
                   jax.ShapeDtypeStruct(nv.shape, nv.dtype)],
        scratch_shapes=[pltpu.VMEM((N, ATTN_WIDTH), F32)],
        input_output_aliases={7: 0, 8: 1, 9: 2},
        compiler_params=_params("arbitrary"),
        name="attn_sample",
    )(qh, kh, vh, cache_k, cache_v, sinks, g, ah, nk, nv)


def _mix_sample_kernel(u_ref, h0_ref, pw_ref, bblk_ref, cblk_ref, d_ref, wg_ref, bg_ref, gs_ref,
                       xp_ref, pb_ref, wp_ref, sc_ref, gp_ref, s_in_ref, p_in_ref, s_ref, st_ref, p_ref):
    del s_in_ref, p_in_ref
    S = SSM_BLOCK_STATES
    u = jnp.concatenate([u_ref[j] for j in range(SSM_LANE_BLOCKS)], axis=1)
    ub = u.astype(BF16)
    ys = []
    for j in range(SSM_LANE_BLOCKS):
        x = jnp.dot(ub[:, j * LANES:(j + 1) * LANES], bblk_ref[j], preferred_element_type=F32)
        base = j * 2 * S
        ar = pw_ref[0:1, base:base + S]
        ai = pw_ref[0:1, base + S:base + 2 * S]
        h0r = h0_ref[:, base:base + S]
        h0i = h0_ref[:, base + S:base + 2 * S]
        hr = x[:, 0:S] + ar * h0r - ai * h0i
        hi = x[:, S:] + ar * h0i + ai * h0r
        st_ref[:, base:base + S] = hr
        st_ref[:, base + S:base + 2 * S] = hi
        h = jnp.concatenate([hr, hi], axis=1).astype(BF16)
        ys.append(jnp.dot(h, cblk_ref[j], preferred_element_type=F32))
    s_ref[...] = _ssm_tail(jnp.concatenate(ys, axis=1), u, d_ref, wg_ref, bg_ref, gs_ref).astype(s_ref.dtype)

    xp = xp_ref[...]
    ds = []
    for gi, w in enumerate(POOL_WINDOWS):
        gsl = slice(gi * POOL_GROUP, (gi + 1) * POOL_GROUP)
        s = xp[:, gsl]
        for back in range(1, w):
            s = s + pb_ref[POOL_BUF - back][:, gsl]
        ds.append(s / float(w) - xp[:, gsl])
    p_ref[...] = _pool_tail(ds, wp_ref, sc_ref, gp_ref).astype(p_ref.dtype)


def _mix_sample(uh, xph, h0, pbuf, vecs, pw, bblk, cblk, wg, wp, l, sh, ph):
    N = h0.shape[1]
    rows = lambda w: pl.BlockSpec((N, w), lambda i: (0, 0))
    anyspec = pl.BlockSpec(memory_space=pl.ANY)
    return pl.pallas_call(
        _mix_sample_kernel,
        grid=(1,),
        in_specs=[pl.BlockSpec((SSM_LANE_BLOCKS, N, LANES), lambda i: (0, 0, 0)),
                  _layer(h0.shape[1:], l), _layer(pw.shape[1:], l), _layer(bblk.shape[1:], l),
                  _layer(cblk.shape[1:], l),
                  _vrow("D_skip", l), _layer((SSM_WIDTH, SSM_WIDTH), l), _vrow("b_glu", l),
                  _vrow("g_out_ssm", l), rows(POOL_WIDTH), _layer(pbuf.shape[1:], l),
                  _layer(wp.shape[1:], l),
                  _vrow("pool_scale", l), _vrow("g_out_pool", l), anyspec, anyspec],
        out_specs=[rows(SSM_WIDTH), _full((N, SSM_STATE_LANES)), rows(POOL_WIDTH)],
        out_shape=[jax.ShapeDtypeStruct(sh.shape, sh.dtype), jax.ShapeDtypeStruct((N, SSM_STATE_LANES), F32),
                   jax.ShapeDtypeStruct(ph.shape, ph.dtype)],
        input_output_aliases={14: 0, 15: 2},
        compiler_params=_params("arbitrary"),
        name="mix_sample",
    )(uh, h0, pw, bblk, cblk, vecs, wg, vecs, vecs, xph, pbuf, wp, vecs, vecs, sh, ph)


def _outproj_rows(x_ref, a_ref, s_ref, p_ref, w_ref, o_ref):
    o1 = ATTN_WIDTH
    o2 = o1 + SSM_WIDTH
    acc = x_ref[...]
    acc = acc + jnp.dot(a_ref[...], w_ref[0:o1, :], preferred_element_type=F32)
    acc = acc + jnp.dot(s_ref[...], w_ref[o1:o2, :], preferred_element_type=F32)
    acc = acc + jnp.dot(p_ref[...], w_ref[o2:, :], preferred_element_type=F32)
    o_ref[...] = acc


def _outproj_kernel(xm_ref, am_ref, sm_ref, pm_ref, xh_ref, ah_ref, sh_ref, ph_ref, w_ref, om_ref, oh_ref):
    i = pl.program_id(0)
    last = pl.num_programs(0) - 1

    @pl.when(i < last)
    def _():
        _outproj_rows(xm_ref, am_ref, sm_ref, pm_ref, w_ref, om_ref)

    @pl.when(i == last)
    def _():
        _outproj_rows(xh_ref, ah_ref, sh_ref, ph_ref, w_ref, oh_ref)


def _outproj(xm, am, sm, pm, xh, ah, sh, ph, w, tm):
    R, RH = xm.shape[0], xh.shape[0]
    nm = R // tm
    row = lambda i: (jnp.minimum(i, nm - 1), 0)
    once = pl.Buffered(1)
    head = lambda w_: pl.BlockSpec((RH, w_), lambda i: (0, 0), pipeline_mode=once)
    return pl.pallas_call(
        _outproj_kernel,
        grid=(nm + 1,),
        in_specs=[pl.BlockSpec((tm, D_MODEL), row), pl.BlockSpec((tm, ATTN_WIDTH), row),
                  pl.BlockSpec((tm, SSM_WIDTH), row), pl.BlockSpec((tm, POOL_WIDTH), row),
                  head(D_MODEL), head(ATTN_WIDTH), head(SSM_WIDTH), head(POOL_WIDTH),
                  pl.BlockSpec((D_MODEL, D_MODEL), lambda i: (0, 0), pipeline_mode=once)],
        out_specs=[pl.BlockSpec((tm, D_MODEL), row), pl.BlockSpec((RH, D_MODEL), lambda i: (0, 0))],
        out_shape=[jax.ShapeDtypeStruct((R, D_MODEL), F32), jax.ShapeDtypeStruct((RH, D_MODEL), F32)],
        compiler_params=_params("arbitrary"),
        name="outproj",
    )(xm, am, sm, pm, xh, ah, sh, ph, w)


def _ffn_kernel(xm_ref, xh_ref, g_ref, w1_ref, w2_ref, *rest, n_cast):
    cast_in = rest[0:n_cast]
    om_ref, oh_ref = rest[n_cast:n_cast + 2]
    cast_out = rest[n_cast + 2:2 * n_cast + 2]
    hm_scr, hh_scr = rest[2 * n_cast + 2:]
    i = pl.program_id(0)
    f = pl.program_id(1)
    for src, dst in zip(cast_in, cast_out):
        dst[...] = src[...].astype(BF16)

    def mlp(h):
        h1 = jnp.dot(h, w1_ref[...], preferred_element_type=F32)
        return jnp.dot(jnp.square(jnp.maximum(h1, 0.0)).astype(BF16), w2_ref[...], preferred_element_type=F32)

    def first(x_ref, h_scr, o_ref):
        x = x_ref[...]
        h = _rms(x, g_ref[...]).astype(BF16)
        h_scr[...] = h
        o_ref[...] = x + mlp(h)

    @pl.when(f == 0)
    def _():
        first(xm_ref, hm_scr, om_ref)

    @pl.when(f > 0)
    def _():
        om_ref[...] += mlp(hm_scr[...])

    @pl.when((i == 0) & (f == 0))
    def _():
        first(xh_ref, hh_scr, oh_ref)

    @pl.when((i == 0) & (f > 0))
    def _():
        oh_ref[...] += mlp(hh_scr[...])


def _ffn(xm, xh, g, l, w1, w2, tm, tf, casts=()):
    R, RH = xm.shape[0], xh.shape[0]
    nf = D_FF // tf
    steps = (R // tm) * nf
    step = lambda i, f: (i * nf + f, 0)
    cast_specs = []
    for w, wl in casts:
        _, cr, cc = w.shape
        rows = cr // steps
        cast_specs.append((pl.BlockSpec((None, rows, cc), lambda i, f, wl=wl: (wl, i * nf + f, 0)),
                           pl.BlockSpec((rows, cc), step), jax.ShapeDtypeStruct((cr, cc), BF16)))
    res = pl.pallas_call(
        functools.partial(_ffn_kernel, n_cast=len(casts)),
        grid=(R // tm, nf),
        in_specs=[pl.BlockSpec((tm, D_MODEL), lambda i, f: (i, 0)),
                  pl.BlockSpec((RH, D_MODEL), lambda i, f: (0, 0), pipeline_mode=pl.Buffered(1)),
                  _vrow("g_ffn", l),
                  pl.BlockSpec((D_MODEL, tf), lambda i, f: (0, f)),
                  pl.BlockSpec((tf, D_MODEL), lambda i, f: (f, 0))] + [c[0] for c in cast_specs],
        out_specs=[pl.BlockSpec((tm, D_MODEL), lambda i, f: (i, 0)),
                   pl.BlockSpec((RH, D_MODEL), lambda i, f: (0, 0))] + [c[1] for c in cast_specs],
        out_shape=[jax.ShapeDtypeStruct((R, D_MODEL), F32), jax.ShapeDtypeStruct((RH, D_MODEL), F32)]
        + [c[2] for c in cast_specs],
        scratch_shapes=[pltpu.VMEM((tm, D_MODEL), BF16), pltpu.VMEM((RH, D_MODEL), BF16)],
        compiler_params=_params("arbitrary", "arbitrary"),
        name="ffn",
    )(xm, xh, g, w1, w2, *[w for w, _ in casts])
    return res[0], res[1], res[2:]


def _rope_tables(pos):
    half = ROT_HALF
    inv = ROPE_THETA ** (-np.arange(0, ROT_DIM, 2, dtype=np.float64) / ROT_DIM)
    ang = np.asarray(pos, np.float64)[:, None] * inv
    cos, sin = np.cos(ang), np.sin(ang)
    n = ang.shape[0]
    z = np.zeros((n, HEAD_DIM - ROT_DIM))
    zh = np.zeros((n, half))
    rc = np.concatenate([cos, cos, z + 1.0], axis=1)
    rs1 = np.concatenate([zh, sin, z], axis=1)
    rs2 = np.concatenate([-sin, zh, z], axis=1)
    return tuple(t.astype(np.float32) for t in (rc, rs1, rs2))


def kernel(x_prompt, x_sample, cache_k, cache_v, state_ssm_re, state_ssm_im, state_pool, meta_tokens, g_mix, w_in, g_q, g_k, sinks, A_re, A_im, log_dt, B_re, B_im, C_re, C_im, D_skip, w_glu, b_glu, w_pool, pool_scale, g_out_attn, g_out_ssm, g_out_pool, w_out, g_ffn, w_ff1, w_ff2):
    B, T, _ = x_prompt.shape
    N = x_sample.shape[0]
    depth = w_in.shape[0]
    assert N % N_META == 0 and N % DEC_STEP == 0 and T % TM_FFN == 0
    xh = jnp.concatenate([x_sample.reshape(N, D_MODEL)] + [meta_tokens.astype(F32)] * B, axis=0)
    xm = x_prompt.reshape(B * T, D_MODEL)

    rope_m = _rope_tables(N_META + np.arange(T))
    rope_h = _rope_tables(np.concatenate([np.full(N, PAST_LEN)] + [np.arange(N_META)] * B))

    wi = w_in[0].astype(BF16)
    wg_all, wp_all = w_glu.astype(BF16), w_pool.astype(BF16)
    ck = cache_k.astype(F32)
    cv = cache_v.astype(F32)

    named = dict(g_mix=g_mix, g_ffn=g_ffn, g_out_attn=g_out_attn, D_skip=D_skip, b_glu=b_glu,
                 g_out_ssm=g_out_ssm, pool_scale=pool_scale, g_out_pool=g_out_pool, g_q=g_q, g_k=g_k)
    vecs = jnp.concatenate([named[name].astype(F32) for name, _ in VEC_LAYOUT], axis=1)[:, None, :]
    pw, bblk, cblk = _ssm_params(A_re, A_im, log_dt, B_re, B_im, C_re, C_im)
    sinks_flat = sinks.astype(F32).reshape(depth * N_HEADS)
    sinks_col = sinks.astype(F32).reshape(depth, N_KV_HEADS, GQA_GROUP, 1)
    bias = _attn_bias()
    h0 = jnp.concatenate([state_ssm_re.astype(F32).reshape(depth, N, SSM_LANE_BLOCKS, SSM_BLOCK_STATES),
                          state_ssm_im.astype(F32).reshape(depth, N, SSM_LANE_BLOCKS, SSM_BLOCK_STATES)],
                         axis=-1).reshape(depth, N, SSM_STATE_LANES)
    pbuf = state_pool.astype(F32).transpose(0, 2, 1, 3)

    nk, nv = _shift_caches(ck, cv)
    ks, vs, pls, sts, st_ss, phs = ([] for _ in range(6))
    for l in range(depth):
        (qm, km, vm, um, pm), (qh, kh, vh, uh, ph), w1_new = _inproj(
            xm, xh, vecs, wi, l, rope_m, rope_h, TM_PROJ, w_ff1 if l == 0 else None)
        if l == 0:
            w1 = w1_new

        (am, ah, sm, sh, st, plm, plh), wcast = _mixers(
            sinks_flat, (qm, km, vm), (qh, kh, vh), um, uh, pm, ph, vecs, bias, pw, bblk, cblk, wg_all, wp_all,
            l, B, TM_SEQ, [(w_ff2, l), (w_out, l)] if l == 0 else [])
        if l == 0:
            w2, wo = wcast

        ah, nk, nv = _attn_sample(qh, kh, vh, ck, cv, l, sinks_col, vecs, ah, nk, nv, N)
        sh, st_s, plh = _mix_sample(uh, ph, h0, pbuf, vecs, pw, bblk, cblk, wg_all, wp_all, l, sh, plh)

        xm, xh = _outproj(xm, am, sm, plm, xh, ah, sh, plh, wo, TM_PROJ)
        nxt = [(w, l + 1) for w in (w_in, w_ff1, w_ff2, w_out)] if l + 1 < depth else []
        xm, xh, wnext = _ffn(xm, xh, vecs, l, w1, w2, TM_FFN, TF_FFN, nxt)
        if nxt:
            wi, w1, w2, wo = wnext

        ks.append(km.reshape(B, T, KV_WIDTH)[:, T - WINDOW:])
        vs.append(vm.reshape(B, T, KV_WIDTH)[:, T - WINDOW:])
        pls.append(pm.reshape(B, T, POOL_WIDTH)[:, T - POOL_BUF:])
        sts.append(st[:, 0])
        st_ss.append(st_s)
        phs.append(ph[:N])

    y_prompt = xm.reshape(B, T, D_MODEL)
    y_sample = xh[:N].reshape(N, 1, D_MODEL)
    heads = lambda t: jnp.stack(t).reshape(depth, -1, WINDOW, N_KV_HEADS, HEAD_DIM)
    p_re, p_im = _state_from_lanes(jnp.stack(sts).reshape(depth * B, SSM_STATE_LANES))
    s_re, s_im = _state_from_lanes(jnp.stack(st_ss).reshape(depth * N, SSM_STATE_LANES))
    st4 = lambda t, n: t.reshape(depth, n, SSM_GROUPS, SSM_STATE)
    s_pool = jnp.concatenate([state_pool.astype(F32)[:, :, 1:], jnp.stack(phs)[:, :, None]], axis=2)
    return (y_prompt, y_sample, heads(ks), heads(vs), st4(p_re, B), st4(p_im, B), jnp.stack(pls),
            nk, nv, st4(s_re, N), st4(s_im, N), s_pool)
```

```python
import functools
import math

import jax
import jax.numpy as jnp
import numpy as np
from jax.experimental import pallas as pl
from jax.experimental.pallas import tpu as pltpu

D_MODEL = 2048
N_META = 16
HEAD_DIM = 128
N_HEADS = 8
N_KV_HEADS = 2
GQA_GROUP = 4
ATTN_WIDTH = 1024
KV_WIDTH = 256
WINDOW = 128
BLOCK = 128
ROT_DIM = 32
ROT_HALF = ROT_DIM // 2
ROPE_THETA = 500000.0
SSM_WIDTH = 512
SSM_GROUP_SIZE = 16
SSM_GROUPS = 32
SSM_STATE = 64
POOL_WIDTH = 512
POOL_WINDOWS = (2, 4, 8, 16)
POOL_GROUP = 128
POOL_BUF = 15
POOL_HALO = 16
IN_WIDTH = 2560
D_FF = 8192
EPS = 1e-6
PAST_LEN = 16384
LOG2E = math.log2(math.e)

SEQ0 = BLOCK - N_META
LANES = 128
SUBLANES = 8
SSM_LANE_BLOCKS = SSM_WIDTH // LANES
SSM_BLOCK_STATES = (LANES // SSM_GROUP_SIZE) * SSM_STATE
SSM_STATE_LANES = SSM_LANE_BLOCKS * 2 * SSM_BLOCK_STATES
VMEM_LIMIT = 60 * 1024 * 1024

TM_PROJ = 512
TM_FFN = 1024
TF_FFN = 512
TM_SEQ = 512
DEC_STEP = 8
SSM_POW_ROWS = (1, N_META // SUBLANES, TM_SEQ // SUBLANES)
POW_ROW_HEAD, POW_ROW_MAIN = 1, 2

BF16 = jnp.bfloat16
F32 = jnp.float32


def _params(*semantics):
    return pltpu.CompilerParams(dimension_semantics=semantics, vmem_limit_bytes=VMEM_LIMIT)


def _rms(x, g):
    return x * jax.lax.rsqrt(jnp.mean(x * x, axis=-1, keepdims=True) + EPS) * g


def _full(shape):
    n = len(shape)
    return pl.BlockSpec(shape, lambda *_: (0,) * n)


def _layer(shape, l):
    n = len(shape)
    return pl.BlockSpec((None, *shape), lambda *_: (l,) + (0,) * n)


VEC_LAYOUT = (("g_mix", D_MODEL), ("g_ffn", D_MODEL), ("g_out_attn", ATTN_WIDTH), ("D_skip", SSM_WIDTH),
              ("b_glu", SSM_WIDTH), ("g_out_ssm", SSM_WIDTH), ("pool_scale", POOL_WIDTH),
              ("g_out_pool", POOL_WIDTH), ("g_q", HEAD_DIM), ("g_k", HEAD_DIM))
VEC_WIDTH = dict(VEC_LAYOUT)
VEC_OFFSET = {name: sum(w for _, w in VEC_LAYOUT[:i]) for i, (name, _) in enumerate(VEC_LAYOUT)}
assert all(VEC_OFFSET[name] % w == 0 for name, w in VEC_LAYOUT)


def _vrow(name, l):
    w = VEC_WIDTH[name]
    return pl.BlockSpec((None, 1, w), lambda *_: (l, 0, VEC_OFFSET[name] // w))


def _inproj_rows(x, g_ref, w_ref, gq_ref, gk_ref, rope_refs, out_refs):
    q_ref, k_ref, v_ref, u_ref, xp_ref = out_refs
    h = _rms(x, g_ref[...]).astype(BF16)
    proj = jnp.dot(h, w_ref[...], preferred_element_type=F32)
    rc, rs1, rs2 = (r[...] for r in rope_refs)

    def head(t, g):
        t = _rms(t, g)
        return t * rc + pltpu.roll(t, ROT_HALF, 1) * rs1 + pltpu.roll(t, LANES - ROT_HALF, 1) * rs2

    for hd in range(N_HEADS):
        sl = slice(hd * HEAD_DIM, (hd + 1) * HEAD_DIM)
        q_ref[:, sl] = head(proj[:, sl], gq_ref[...])
    for hd in range(N_KV_HEADS):
        sl = slice(hd * HEAD_DIM, (hd + 1) * HEAD_DIM)
        k_ref[:, sl] = head(proj[:, ATTN_WIDTH + hd * HEAD_DIM:ATTN_WIDTH + (hd + 1) * HEAD_DIM], gk_ref[...])
    o2 = ATTN_WIDTH + KV_WIDTH
    o3 = o2 + KV_WIDTH
    o4 = o3 + SSM_WIDTH
    v_ref[...] = proj[:, o2:o3]
    for j in range(SSM_LANE_BLOCKS):
        u_ref[j] = proj[:, o3 + j * LANES:o3 + (j + 1) * LANES]
    xp_ref[...] = proj[:, o4:]


def _inproj_kernel(xm_ref, xh_ref, g_ref, w_ref, gq_ref, gk_ref, rcm_ref, rs1m_ref, rs2m_ref,
                   rch_ref, rs1h_ref, rs2h_ref, *rest):
    cast = len(rest) == 12
    outs = rest[1:] if cast else rest
    main_outs, head_outs = outs[0:5], outs[5:10]
    i = pl.program_id(0)
    last = pl.num_programs(0) - 1

    @pl.when(i < last)
    def _():
        if cast:
            outs[10][...] = rest[0][...].astype(BF16)
        _inproj_rows(xm_ref[...], g_ref, w_ref, gq_ref, gk_ref, (rcm_ref, rs1m_ref, rs2m_ref), main_outs)

    @pl.when(i == last)
    def _():
        _inproj_rows(xh_ref[...], g_ref, w_ref, gq_ref, gk_ref, (rch_ref, rs1h_ref, rs2h_ref), head_outs)


def _inproj(xm, xh, vecs, w, l, rope_m, rope_h, tm, cast=None):
    R, RH = xm.shape[0], xh.shape[0]
    nm = R // tm
    tiles_per_rope = rope_m[0].shape[0] // tm
    tile = lambda i: jnp.minimum(i, nm - 1)
    row = lambda i: (tile(i), 0)
    rrow = lambda i: (tile(i) % tiles_per_rope, 0)
    once = pl.Buffered(1)
    full = lambda n, w_: pl.BlockSpec((n, w_), lambda i: (0, 0))

    def out_set(n, rows, im, uim):
        flat = lambda w_: (pl.BlockSpec((n, w_), im), jax.ShapeDtypeStruct((rows, w_), F32))
        u_out = (pl.BlockSpec((SSM_LANE_BLOCKS, n, LANES), uim),
                 jax.ShapeDtypeStruct((SSM_LANE_BLOCKS, rows, LANES), F32))
        return [flat(ATTN_WIDTH), flat(KV_WIDTH), flat(KV_WIDTH), u_out, flat(POOL_WIDTH)]

    outs = (out_set(tm, R, row, lambda i: (0, tile(i), 0))
            + out_set(RH, RH, lambda i: (0, 0), lambda i: (0, 0, 0)))
    in_specs = [pl.BlockSpec((tm, D_MODEL), row),
                pl.BlockSpec((RH, D_MODEL), lambda i: (0, 0), pipeline_mode=once),
                _vrow("g_mix", l),
                pl.BlockSpec((D_MODEL, IN_WIDTH), lambda i: (0, 0), pipeline_mode=once),
                _vrow("g_q", l), _vrow("g_k", l),
                pl.BlockSpec((tm, LANES), rrow), pl.BlockSpec((tm, LANES), rrow), pl.BlockSpec((tm, LANES), rrow),
                full(RH, LANES), full(RH, LANES), full(RH, LANES)]
    args = [xm, xh, vecs, w, vecs, vecs, *rope_m, *rope_h]
    if cast is not None:
        _, cr, cc = cast.shape
        crows = cr // nm
        in_specs.append(pl.BlockSpec((None, crows, cc), lambda i: (l, tile(i), 0)))
        args.append(cast)
        outs.append((pl.BlockSpec((crows, cc), row), jax.ShapeDtypeStruct((cr, cc), BF16)))
    res = pl.pallas_call(
        _inproj_kernel,
        grid=(nm + 1,),
        in_specs=in_specs,
        out_specs=[o[0] for o in outs],
        out_shape=[o[1] for o in outs],
        compiler_params=_params("arbitrary"),
        name="inproj",
    )(*args)
    return res[0:5], res[5:10], (res[10] if cast is not None else None)


def _attn_bias():
    rows = GQA_GROUP * BLOCK
    i, r, c = np.meshgrid(np.arange(3), np.arange(rows) % BLOCK, np.arange(2 * BLOCK), indexing="ij")
    diff = BLOCK + r - c
    krow = (i - 1) * BLOCK + c
    mask = (diff >= 0) & (diff <= WINDOW) & (krow >= SEQ0)
    return np.where(mask, 0.0, -np.inf).astype(np.float32)


def _attn_block(q_blk, kp_blk, kc_blk, vp_blk, vc_blk, bias, sink_ref, l, g):
    rows = GQA_GROUP * BLOCK
    rgrp = jax.lax.broadcasted_iota(jnp.int32, (rows, 1), 0) // BLOCK
    outs = []
    for kh in range(N_KV_HEADS):
        ksl = slice(kh * HEAD_DIM, (kh + 1) * HEAD_DIM)
        qh = jnp.concatenate(
            [q_blk[:, (kh * GQA_GROUP + h) * HEAD_DIM:(kh * GQA_GROUP + h + 1) * HEAD_DIM]
             for h in range(GQA_GROUP)], axis=0).astype(BF16)
        kk = jnp.concatenate([kp_blk[:, ksl], kc_blk[:, ksl]], axis=0).astype(BF16)
        vv = jnp.concatenate([vp_blk[:, ksl], vc_blk[:, ksl]], axis=0).astype(BF16)
        s = jax.lax.dot_general(qh, kk, (((1,), (1,)), ((), ())),
                                preferred_element_type=F32) * (HEAD_DIM ** -0.5 * LOG2E) + bias
        sk = jnp.zeros((rows, 1), F32)
        for h in range(GQA_GROUP):
            sk = jnp.where(rgrp == h, sink_ref[l * N_HEADS + kh * GQA_GROUP + h] * LOG2E, sk)
        m = jnp.maximum(jnp.max(s, axis=-1, keepdims=True), sk)
        p = jnp.exp2(s - m)
        denom = jnp.sum(p, axis=-1, keepdims=True) + jnp.exp2(sk - m)
        o = jnp.dot(p.astype(BF16), vv, preferred_element_type=F32) / denom
        outs.extend(o[h * BLOCK:(h + 1) * BLOCK] for h in range(GQA_GROUP))
    return _rms(jnp.concatenate(outs, axis=1), g)


def _ssm_params_kernel(ar_ref, ai_ref, ldt_ref, kk_ref, br_ref, bi_ref, tr_ref, ti_ref, bbr_ref, bbi_ref):
    ar, ai = ar_ref[...], ai_ref[...]
    dt = jnp.exp(ldt_ref[...])
    kk = kk_ref[...]
    mag = jnp.exp(dt * ar * kk)
    ang = dt * ai * kk
    tr = mag * jnp.cos(ang)
    ti = mag * jnp.sin(ang)
    tr_ref[...] = tr
    ti_ref[...] = ti
    abr, abi = tr[0:1], ti[0:1]
    den = ar * ar + ai * ai
    fr = ((abr - 1.0) * ar + abi * ai) / den
    fi = (abi * ar - (abr - 1.0) * ai) / den
    br, bi = br_ref[...], bi_ref[...]
    bbr_ref[...] = fr * br - fi * bi
    bbi_ref[...] = fr * bi + fi * br


def _ssm_params(A_re, A_im, log_dt, B_re, B_im, C_re, C_im):
    depth = A_re.shape[0]
    n = depth * SSM_GROUPS * SSM_STATE
    row = lambda t: t.astype(F32).reshape(1, n)
    ldt = jnp.broadcast_to(log_dt.astype(F32)[:, :, None], (depth, SSM_GROUPS, SSM_STATE)).reshape(1, n)
    kk = jnp.array(SSM_POW_ROWS + (0,) * (SUBLANES - len(SSM_POW_ROWS)), F32).reshape(SUBLANES, 1)
    chan_first = lambda t: t.astype(F32).reshape(n, SSM_GROUP_SIZE).T
    shapes = [(SUBLANES, n), (SUBLANES, n), (SSM_GROUP_SIZE, n), (SSM_GROUP_SIZE, n)]
    tr, ti, bbr, bbi = pl.pallas_call(
        _ssm_params_kernel,
        out_shape=[jax.ShapeDtypeStruct(s, F32) for s in shapes],
        name="ssm_params",
    )(row(A_re), row(A_im), ldt, kk, chan_first(B_re), chan_first(B_im))
    J, G8 = SSM_LANE_BLOCKS, LANES // SSM_GROUP_SIZE

    def lanes(t):
        return t.reshape(SUBLANES, depth, J, SSM_BLOCK_STATES).transpose(1, 0, 2, 3)

    pw = jnp.concatenate([lanes(tr), lanes(ti)], axis=-1).reshape(depth, SUBLANES, SSM_STATE_LANES)
    eye = jnp.eye(G8, dtype=F32)

    def bdiag(t):
        t = t.reshape(SSM_GROUP_SIZE, depth, J, G8, SSM_STATE).transpose(1, 2, 3, 0, 4)
        t = t[:, :, :, :, None, :] * eye[None, None, :, None, :, None]
        return t.reshape(depth, J, LANES, SSM_BLOCK_STATES)

    bblk = jnp.concatenate([bdiag(bbr), bdiag(bbi)], axis=-1).astype(BF16)

    def cdiag(t):
        t = t.astype(F32).reshape(depth, J, G8, SSM_GROUP_SIZE, SSM_STATE).transpose(0, 1, 2, 4, 3)
        t = t[:, :, :, :, None, :] * eye[None, None, :, None, :, None]
        return t.reshape(depth, J, SSM_BLOCK_STATES, LANES)

    cblk = jnp.concatenate([cdiag(C_re), -cdiag(C_im)], axis=2).astype(BF16)
    return pw, bblk, cblk


def _ssm_tail(y, u, d_ref, wg_ref, bg_ref, g_ref):
    y = y + d_ref[...] * u
    z = jax.nn.gelu(y)
    gate = jax.nn.sigmoid(jnp.dot(z.astype(BF16), wg_ref[...], preferred_element_type=F32) + bg_ref[...])
    return _rms(z * gate, g_ref[...])


def _ssm_sweep(x_scr, n, a_tabs, init, store):
    S = SSM_BLOCK_STATES
    fins = []
    for j0 in range(0, SSM_LANE_BLOCKS, 2):
        js = (j0, j0 + 1)

        def body(k, carry, js=js):
            r0 = pl.multiple_of(k * SUBLANES, SUBLANES)
            out = []
            for idx, j in enumerate(js):
                hr, hi = carry[2 * idx], carry[2 * idx + 1]
                base = j * 2 * S
                ar, ai = a_tabs[j]
                nhr = ar * hr - ai * hi + x_scr[pl.ds(r0, SUBLANES), base:base + S]
                nhi = ar * hi + ai * hr + x_scr[pl.ds(r0, SUBLANES), base + S:base + 2 * S]
                if store:
                    x_scr[pl.ds(r0, SUBLANES), base:base + S] = nhr
                    x_scr[pl.ds(r0, SUBLANES), base + S:base + 2 * S] = nhi
                out += [nhr, nhi]
            return tuple(out)

        c0 = tuple(t for j in js for t in init[j])
        res = jax.lax.fori_loop(0, n // SUBLANES, body, c0, unroll=True)
        fins += [(res[0], res[1]), (res[2], res[3])]
    return fins


def _ssm_rows(u, pow_row, pw_ref, bblk_ref, cblk_ref, x_scr, s_scr, carry_scr):
    n = u.shape[0]
    S = SSM_BLOCK_STATES
    ub = u.astype(BF16)
    for j in range(SSM_LANE_BLOCKS):
        x_scr[0:n, j * 2 * S:(j + 1) * 2 * S] = jnp.dot(ub[:, j * LANES:(j + 1) * LANES], bblk_ref[j],
                                                       preferred_element_type=F32)
    bc = lambda t: jnp.broadcast_to(t, (SUBLANES, S))
    re = lambda ref, r0, r1, j: ref[r0:r1, j * 2 * S:j * 2 * S + S]
    im = lambda ref, r0, r1, j: ref[r0:r1, j * 2 * S + S:(j + 1) * 2 * S]
    a_tabs = [(bc(re(pw_ref, 0, 1, j)), bc(im(pw_ref, 0, 1, j))) for j in range(SSM_LANE_BLOCKS)]
    zero = jnp.zeros((SUBLANES, S), F32)
    fins = _ssm_sweep(x_scr, n, a_tabs, [(zero, zero)] * SSM_LANE_BLOCKS, store=False)
    for j in range(SSM_LANE_BLOCKS):
        base = j * 2 * S
        cr, ci = re(pw_ref, pow_row, pow_row + 1, j), im(pw_ref, pow_row, pow_row + 1, j)
        sr, si = re(carry_scr, 0, 1, j), im(carry_scr, 0, 1, j)
        fr, fi = fins[j]
        for c in range(SUBLANES):
            s_scr[c:c + 1, base:base + S] = sr
            s_scr[c:c + 1, base + S:base + 2 * S] = si
            sr, si = cr * sr - ci * si + fr[c:c + 1], cr * si + ci * sr + fi[c:c + 1]
        carry_scr[:, base:base + S] = bc(sr)
        carry_scr[:, base + S:base + 2 * S] = bc(si)
    init = [(re(s_scr, 0, SUBLANES, j), im(s_scr, 0, SUBLANES, j)) for j in range(SSM_LANE_BLOCKS)]
    _ssm_sweep(x_scr, n, a_tabs, init, store=True)
    ys = [jnp.dot(x_scr[0:n, j * 2 * S:(j + 1) * 2 * S].astype(BF16), cblk_ref[j], preferred_element_type=F32)
          for j in range(SSM_LANE_BLOCKS)]
    return jnp.concatenate(ys, axis=1)


def _ssm_tile(u_ref, n, pow_row, refs, o_ref, scr):
    pw_ref, bblk_ref, cblk_ref, d_ref, wg_ref, bg_ref, g_ref = refs
    up_scr, x_scr, s_scr, carry_scr, o_scr = scr
    q = n // SUBLANES
    for j in range(SSM_LANE_BLOCKS):
        for k in range(q):
            up_scr[k * SUBLANES:(k + 1) * SUBLANES, j * LANES:(j + 1) * LANES] = \
                u_ref[j, pl.ds(k, SUBLANES, stride=q), :]
    u = up_scr[0:n, :]
    y = _ssm_rows(u, pow_row, pw_ref, bblk_ref, cblk_ref, x_scr, s_scr, carry_scr)
    out = _ssm_tail(y, u, d_ref, wg_ref, bg_ref, g_ref)
    for j in range(SSM_LANE_BLOCKS):
        for k in range(q):
            o_scr[j, pl.ds(k, SUBLANES, stride=q), :] = out[k * SUBLANES:(k + 1) * SUBLANES,
                                                            j * LANES:(j + 1) * LANES]
    o_ref[...] = jnp.concatenate([o_scr[j, 0:n, :] for j in range(SSM_LANE_BLOCKS)], axis=1).astype(o_ref.dtype)


def _mixers_kernel(sink_ref,
                   qm_ref, qh_ref, kpm_ref, km_ref, kh_ref, vpm_ref, vm_ref, vh_ref, ga_ref, bias_ref,
                   um_ref, uh_ref, pw_ref, bblk_ref, cblk_ref, d_ref, wg_ref, bg_ref, gs_ref,
                   xm_ref, halo_ref, xh_ref, wp_ref, sc_ref, gp_ref, *rest, l, n_cast):
    cast_in = rest[0:n_cast]
    am_ref, ah_ref, som_ref, soh_ref, st_ref, pom_ref, poh_ref = rest[n_cast:n_cast + 7]
    cast_out = rest[n_cast + 7:2 * n_cast + 7]
    scr = rest[2 * n_cast + 7:]
    carry_scr = scr[3]
    t = pl.program_id(1)
    tm = xm_ref.shape[0]
    ssm_refs = (pw_ref, bblk_ref, cblk_ref, d_ref, wg_ref, bg_ref, gs_ref)
    blk = functools.partial(_attn_block, sink_ref=sink_ref, l=l, g=ga_ref[...])

    block = lambda ref: jnp.concatenate([jnp.zeros((SEQ0, ref.shape[1]), F32), ref[...]], axis=0)

    @pl.when(t == 0)
    def _():
        a = blk(block(qh_ref), kpm_ref[...], block(kh_ref), vpm_ref[...], block(vh_ref), bias_ref[0])
        ah_ref[...] = a[SEQ0:].astype(ah_ref.dtype)
        carry_scr[...] = jnp.zeros_like(carry_scr)
        _ssm_tile(uh_ref, N_META, POW_ROW_HEAD, ssm_refs, soh_ref, scr)
        prev = jnp.zeros((POOL_HALO, POOL_WIDTH), F32)
        poh_ref[...] = _pool_rows(xh_ref[...], prev, 0, wp_ref, sc_ref, gp_ref).astype(poh_ref.dtype)

    @pl.when(t > 0)
    def _():
        for src, dst in zip(cast_in, cast_out):
            dst[...] = src[...].astype(BF16)
        kp = jnp.where(t == 1, block(kh_ref), kpm_ref[...])
        vp = jnp.where(t == 1, block(vh_ref), vpm_ref[...])
        for n in range(tm // BLOCK):
            rows = slice(n * BLOCK, (n + 1) * BLOCK)
            kc, vc = km_ref[rows, :], vm_ref[rows, :]
            bias = bias_ref[jnp.minimum(t, 2)] if n == 0 else bias_ref[2]
            am_ref[rows, :] = blk(qm_ref[rows, :], kp, kc, vp, vc, bias).astype(am_ref.dtype)
            kp, vp = kc, vc
        _ssm_tile(um_ref, um_ref.shape[1], POW_ROW_MAIN, ssm_refs, som_ref, scr)
        st_ref[...] = carry_scr[...]
        prev = jnp.where(t == 1, xh_ref[...], halo_ref[...])
        pom_ref[...] = _pool_rows(xm_ref[...], prev, N_META + (t - 1) * tm, wp_ref, sc_ref, gp_ref).astype(pom_ref.dtype)


def _mixers(sinks, qkv_m, qkv_h, um, uh, xm, xh, vecs, bias, pw, bblk, cblk, wg, wp, l, B, tm, casts):
    assert (1, N_META // SUBLANES, tm // SUBLANES) == SSM_POW_ROWS and N_META == POOL_HALO
    J = SSM_LANE_BLOCKS
    rm, rh = um.shape[1], uh.shape[1]
    nt = rm // (B * tm)
    r = tm // POOL_HALO
    bpt = tm // BLOCK
    tile = lambda b, t: b * nt + jnp.maximum(t - 1, 0)
    main = lambda b, t, _: (tile(b, t), 0)
    n_dec = uh.shape[1] - B * N_META
    head = lambda b, t, _: (n_dec // N_META + b, 0)
    halo = lambda b, t, _: (jnp.maximum((b * nt + t - 1) * r - 1, 0), 0)
    prevb = lambda b, t, _: (jnp.maximum(tile(b, t) * bpt - 1, 0), 0)
    cast_specs = []
    for w, wl in casts:
        _, cr, cc = w.shape
        rows = cr // (B * nt)
        cast_specs.append((pl.BlockSpec((None, rows, cc), lambda b, t, _, wl=wl: (wl, tile(b, t), 0)),
                           pl.BlockSpec((rows, cc), main), jax.ShapeDtypeStruct((cr, cc), BF16)))
    bf = lambda n, w: jax.ShapeDtypeStruct((n, w), BF16)
    tile_spec = lambda w: pl.BlockSpec((tm, w), main)
    head_spec = lambda w: pl.BlockSpec((N_META, w), head)
    in_specs = [tile_spec(ATTN_WIDTH), head_spec(ATTN_WIDTH),
                pl.BlockSpec((BLOCK, KV_WIDTH), prevb), tile_spec(KV_WIDTH), head_spec(KV_WIDTH),
                pl.BlockSpec((BLOCK, KV_WIDTH), prevb), tile_spec(KV_WIDTH), head_spec(KV_WIDTH),
                _vrow("g_out_attn", l), _full(bias.shape),
                pl.BlockSpec((J, tm, LANES), lambda b, t, _: (0, tile(b, t), 0)),
                pl.BlockSpec((J, N_META, LANES), lambda b, t, _: (0, n_dec // N_META + b, 0)),
                _layer(pw.shape[1:], l), _layer(bblk.shape[1:], l), _layer(cblk.shape[1:], l),
                _vrow("D_skip", l), _layer((SSM_WIDTH, SSM_WIDTH), l), _vrow("b_glu", l), _vrow("g_out_ssm", l),
                tile_spec(POOL_WIDTH), pl.BlockSpec((POOL_HALO, POOL_WIDTH), halo), head_spec(POOL_WIDTH),
                _layer(wp.shape[1:], l), _vrow("pool_scale", l), _vrow("g_out_pool", l)]
    in_specs += [c[0] for c in cast_specs]
    out_specs = [tile_spec(ATTN_WIDTH), head_spec(ATTN_WIDTH), tile_spec(SSM_WIDTH), head_spec(SSM_WIDTH),
                 pl.BlockSpec((None, SUBLANES, SSM_STATE_LANES), lambda b, t, _: (b, 0, 0)),
                 tile_spec(POOL_WIDTH), head_spec(POOL_WIDTH)] + [c[1] for c in cast_specs]
    out_shape = [bf(rm, ATTN_WIDTH), bf(rh, ATTN_WIDTH), bf(rm, SSM_WIDTH), bf(rh, SSM_WIDTH),
                 jax.ShapeDtypeStruct((B, SUBLANES, SSM_STATE_LANES), F32),
                 bf(rm, POOL_WIDTH), bf(rh, POOL_WIDTH)] + [c[2] for c in cast_specs]
    (qm, km, vm), (qh, kh, vh) = qkv_m, qkv_h
    res = pl.pallas_call(
        functools.partial(_mixers_kernel, l=l, n_cast=len(casts)),
        grid_spec=pltpu.PrefetchScalarGridSpec(
            num_scalar_prefetch=1, grid=(B, nt + 1), in_specs=in_specs, out_specs=out_specs,
            scratch_shapes=[pltpu.VMEM((tm, SSM_WIDTH), F32),
                            pltpu.VMEM((tm, SSM_STATE_LANES), F32),
                            pltpu.VMEM((SUBLANES, SSM_STATE_LANES), F32),
                            pltpu.VMEM((SUBLANES, SSM_STATE_LANES), F32),
                            pltpu.VMEM((J, tm, LANES), F32)]),
        out_shape=out_shape,
        compiler_params=_params("arbitrary", "arbitrary"),
        name="mixers",
    )(sinks, qm, qh, km, km, kh, vm, vm, vh, vecs, bias, um, uh, pw, bblk, cblk, vecs, wg, vecs, vecs,
      xm, xm, xh, wp, vecs, vecs, *[w for w, _ in casts])
    return res[0:7], res[7:]


def _state_from_lanes(s):
    s = s.reshape(s.shape[0], SSM_LANE_BLOCKS, 2, SSM_BLOCK_STATES)
    return (s[:, :, 0].reshape(-1, SSM_GROUPS, SSM_STATE), s[:, :, 1].reshape(-1, SSM_GROUPS, SSM_STATE))


def _pool_tail(d_groups, w_ref, sc_ref, g_ref):
    y = jnp.concatenate(
        [jnp.dot(d.astype(BF16), w_ref[gi], preferred_element_type=F32) for gi, d in enumerate(d_groups)], axis=1)
    return _rms(y * sc_ref[...], g_ref[...])


def _pool_rows(x, prev, pos0, w_ref, sc_ref, g_ref):
    n = x.shape[0]
    xe = jnp.concatenate([prev, x], axis=0)
    pos = pos0 + jax.lax.broadcasted_iota(jnp.int32, (n, 1), 0)
    ds = []
    for gi, w in enumerate(POOL_WINDOWS):
        gsl = slice(gi * POOL_GROUP, (gi + 1) * POOL_GROUP)
        s = xe[:, gsl]
        k = 1
        while k < w:
            s = s + pltpu.roll(s, k, 0)
            k *= 2
        cnt = jnp.clip(pos + 1, 1, w).astype(F32)
        ds.append(s[POOL_HALO:] / cnt - x[:, gsl])
    return _pool_tail(ds, w_ref, sc_ref, g_ref)


def _attn_sample_kernel(q_ref, kn_ref, vn_ref, kc_ref, vc_ref, sink_ref, g_ref, *rest):
    a_ref, nk_ref, nv_ref, acc_scr = rest[-4:]
    step = pl.program_id(0)
    scale = HEAD_DIM ** -0.5
    for bb in range(DEC_STEP):
        nk_ref[bb, 0:WINDOW - 1] = kc_ref[bb, 1:WINDOW]
        nv_ref[bb, 0:WINDOW - 1] = vc_ref[bb, 1:WINDOW]
        outs = []
        for kh in range(N_KV_HEADS):
            ksl = slice(kh * HEAD_DIM, (kh + 1) * HEAD_DIM)
            qh = jnp.concatenate(
                [q_ref[bb:bb + 1, (kh * GQA_GROUP + g) * HEAD_DIM:(kh * GQA_GROUP + g + 1) * HEAD_DIM]
                 for g in range(GQA_GROUP)], axis=0)
            kn = kn_ref[bb:bb + 1, ksl]
            vn = vn_ref[bb:bb + 1, ksl]
            nk_ref[bb, WINDOW - 1, kh:kh + 1, :] = kn
            nv_ref[bb, WINDOW - 1, kh:kh + 1, :] = vn
            kc = kc_ref[bb, :, kh, :]
            vc = vc_ref[bb, :, kh, :]
            sc = jax.lax.dot_general(qh.astype(BF16), kc.astype(BF16), (((1,), (1,)), ((), ())),
                                     preferred_element_type=F32) * scale
            sn = jnp.sum(qh * kn, axis=-1, keepdims=True) * scale
            sk = sink_ref[kh]
            m = jnp.maximum(jnp.maximum(jnp.max(sc, axis=-1, keepdims=True), sn), sk)
            pc = jnp.exp(sc - m)
            pn = jnp.exp(sn - m)
            denom = jnp.sum(pc, axis=-1, keepdims=True) + pn + jnp.exp(sk - m)
            o = jnp.dot(pc.astype(BF16), vc.astype(BF16), preferred_element_type=F32)
            o = (o + pn * vn) / denom
            outs.extend(o[g:g + 1] for g in range(GQA_GROUP))
        a = jnp.concatenate(outs, axis=1)
        acc_scr[pl.ds(step * DEC_STEP + bb, 1), :] = _rms(a, g_ref[...])

    @pl.when(step == pl.num_programs(0) - 1)
    def _():
        a_ref[...] = acc_scr[...].astype(a_ref.dtype)


def _attn_sample(qh, kh, vh, cache_k, cache_v, l, sinks, g, ah, nk, nv, N):
    rows = lambda w: pl.BlockSpec((DEC_STEP, w), lambda s: (s, 0))
    cache = pl.BlockSpec((None, DEC_STEP, WINDOW, N_KV_HEADS, HEAD_DIM), lambda s: (l, s, 0, 0, 0))
    anyspec = pl.BlockSpec(memory_space=pl.ANY)
    carried = [] if nk is None else [nk, nv]
    cshape = jax.ShapeDtypeStruct(cache_k.shape, F32)
    return pl.pallas_call(
        _attn_sample_kernel,
        grid=(N // DEC_STEP,),
        in_specs=[rows(ATTN_WIDTH), rows(KV_WIDTH), rows(KV_WIDTH), cache, cache,
                  _layer((N_KV_HEADS, GQA_GROUP, 1), l), _vrow("g_out_attn", l)] + [anyspec] * (1 + len(carried)),
        out_specs=[pl.BlockSpec((N, ATTN_WIDTH), lambda s: (0, 0)), cache, cache],
        out_shape=[jax.ShapeDtypeStruct(ah.shape, ah.dtype), cshape, cshape],
        scratch_shapes=[pltpu.VMEM((N, ATTN_WIDTH), F32)],
        input_output_aliases={7 + i: i for i in range(1 + len(carried))},
        compiler_params=_params("arbitrary"),
        name="attn_sample",
    )(qh, kh, vh, cache_k, cache_v, sinks, g, ah, *carried)


def _mix_sample_kernel(u_ref, h0_ref, pw_ref, bblk_ref, cblk_ref, d_ref, wg_ref, bg_ref, gs_ref,
                       xp_ref, pb_ref, wp_ref, sc_ref, gp_ref, s_in_ref, p_in_ref, s_ref, st_ref, p_ref):
    del s_in_ref, p_in_ref
    S = SSM_BLOCK_STATES
    u = jnp.concatenate([u_ref[j] for j in range(SSM_LANE_BLOCKS)], axis=1)
    ub = u.astype(BF16)
    ys = []
    for j in range(SSM_LANE_BLOCKS):
        x = jnp.dot(ub[:, j * LANES:(j + 1) * LANES], bblk_ref[j], preferred_element_type=F32)
        base = j * 2 * S
        ar = pw_ref[0:1, base:base + S]
        ai = pw_ref[0:1, base + S:base + 2 * S]
        h0r = h0_ref[:, base:base + S]
        h0i = h0_ref[:, base + S:base + 2 * S]
        hr = x[:, 0:S] + ar * h0r - ai * h0i
        hi = x[:, S:] + ar * h0i + ai * h0r
        st_ref[:, base:base + S] = hr
        st_ref[:, base + S:base + 2 * S] = hi
        h = jnp.concatenate([hr, hi], axis=1).astype(BF16)
        ys.append(jnp.dot(h, cblk_ref[j], preferred_element_type=F32))
    s_ref[...] = _ssm_tail(jnp.concatenate(ys, axis=1), u, d_ref, wg_ref, bg_ref, gs_ref).astype(s_ref.dtype)

    xp = xp_ref[...]
    ds = []
    for gi, w in enumerate(POOL_WINDOWS):
        gsl = slice(gi * POOL_GROUP, (gi + 1) * POOL_GROUP)
        s = xp[:, gsl]
        for back in range(1, w):
            s = s + pb_ref[POOL_BUF - back][:, gsl]
        ds.append(s / float(w) - xp[:, gsl])
    p_ref[...] = _pool_tail(ds, wp_ref, sc_ref, gp_ref).astype(p_ref.dtype)


def _mix_sample(uh, xph, h0, pbuf, vecs, pw, bblk, cblk, wg, wp, l, sh, ph):
    N = h0.shape[1]
    rows = lambda w: pl.BlockSpec((N, w), lambda i: (0, 0))
    anyspec = pl.BlockSpec(memory_space=pl.ANY)
    return pl.pallas_call(
        _mix_sample_kernel,
        grid=(1,),
        in_specs=[pl.BlockSpec((SSM_LANE_BLOCKS, N, LANES), lambda i: (0, 0, 0)),
                  _layer(h0.shape[1:], l), _layer(pw.shape[1:], l), _layer(bblk.shape[1:], l),
                  _layer(cblk.shape[1:], l),
                  _vrow("D_skip", l), _layer((SSM_WIDTH, SSM_WIDTH), l), _vrow("b_glu", l),
                  _vrow("g_out_ssm", l), rows(POOL_WIDTH), _layer(pbuf.shape[1:], l),
                  _layer(wp.shape[1:], l),
                  _vrow("pool_scale", l), _vrow("g_out_pool", l), anyspec, anyspec],
        out_specs=[rows(SSM_WIDTH), _full((N, SSM_STATE_LANES)), rows(POOL_WIDTH)],
        out_shape=[jax.ShapeDtypeStruct(sh.shape, sh.dtype), jax.ShapeDtypeStruct((N, SSM_STATE_LANES), F32),
                   jax.ShapeDtypeStruct(ph.shape, ph.dtype)],
        input_output_aliases={14: 0, 15: 2},
        compiler_params=_params("arbitrary"),
        name="mix_sample",
    )(uh, h0, pw, bblk, cblk, vecs, wg, vecs, vecs, xph, pbuf, wp, vecs, vecs, sh, ph)


def _outproj_rows(x_ref, a_ref, s_ref, p_ref, w_ref, o_ref):
    o1 = ATTN_WIDTH
    o2 = o1 + SSM_WIDTH
    acc = x_ref[...]
    acc = acc + jnp.dot(a_ref[...], w_ref[0:o1, :], preferred_element_type=F32)
    acc = acc + jnp.dot(s_ref[...], w_ref[o1:o2, :], preferred_element_type=F32)
    acc = acc + jnp.dot(p_ref[...], w_ref[o2:, :], preferred_element_type=F32)
    o_ref[...] = acc


def _outproj_kernel(xm_ref, am_ref, sm_ref, pm_ref, xh_ref, ah_ref, sh_ref, ph_ref, w_ref, om_ref, oh_ref):
    i = pl.program_id(0)
    last = pl.num_programs(0) - 1

    @pl.when(i < last)
    def _():
        _outproj_rows(xm_ref, am_ref, sm_ref, pm_ref, w_ref, om_ref)

    @pl.when(i == last)
    def _():
        _outproj_rows(xh_ref, ah_ref, sh_ref, ph_ref, w_ref, oh_ref)


def _outproj(xm, am, sm, pm, xh, ah, sh, ph, w, tm):
    R, RH = xm.shape[0], xh.shape[0]
    nm = R // tm
    row = lambda i: (jnp.minimum(i, nm - 1), 0)
    once = pl.Buffered(1)
    head = lambda w_: pl.BlockSpec((RH, w_), lambda i: (0, 0), pipeline_mode=once)
    return pl.pallas_call(
        _outproj_kernel,
        grid=(nm + 1,),
        in_specs=[pl.BlockSpec((tm, D_MODEL), row), pl.BlockSpec((tm, ATTN_WIDTH), row),
                  pl.BlockSpec((tm, SSM_WIDTH), row), pl.BlockSpec((tm, POOL_WIDTH), row),
                  head(D_MODEL), head(ATTN_WIDTH), head(SSM_WIDTH), head(POOL_WIDTH),
                  pl.BlockSpec((D_MODEL, D_MODEL), lambda i: (0, 0), pipeline_mode=once)],
        out_specs=[pl.BlockSpec((tm, D_MODEL), row), pl.BlockSpec((RH, D_MODEL), lambda i: (0, 0))],
        out_shape=[jax.ShapeDtypeStruct((R, D_MODEL), F32), jax.ShapeDtypeStruct((RH, D_MODEL), F32)],
        compiler_params=_params("arbitrary"),
        name="outproj",
    )(xm, am, sm, pm, xh, ah, sh, ph, w)


def _ffn_kernel(xm_ref, xh_ref, g_ref, w1_ref, w2_ref, *rest, n_cast):
    cast_in = rest[0:n_cast]
    om_ref, oh_ref = rest[n_cast:n_cast + 2]
    cast_out = rest[n_cast + 2:2 * n_cast + 2]
    hm_scr, hh_scr = rest[2 * n_cast + 2:]
    i = pl.program_id(0)
    f = pl.program_id(1)
    for src, dst in zip(cast_in, cast_out):
        dst[...] = src[...].astype(BF16)

    def mlp(h):
        h1 = jnp.dot(h, w1_ref[...], preferred_element_type=F32)
        return jnp.dot(jnp.square(jnp.maximum(h1, 0.0)).astype(BF16), w2_ref[...], preferred_element_type=F32)

    def first(x_ref, h_scr, o_ref):
        x = x_ref[...]
        h = _rms(x, g_ref[...]).astype(BF16)
        h_scr[...] = h
        o_ref[...] = x + mlp(h)

    @pl.when(f == 0)
    def _():
        first(xm_ref, hm_scr, om_ref)

    @pl.when(f > 0)
    def _():
        om_ref[...] += mlp(hm_scr[...])

    @pl.when((i == 0) & (f == 0))
    def _():
        first(xh_ref, hh_scr, oh_ref)

    @pl.when((i == 0) & (f > 0))
    def _():
        oh_ref[...] += mlp(hh_scr[...])


def _ffn(xm, xh, g, l, w1, w2, tm, tf, casts=()):
    R, RH = xm.shape[0], xh.shape[0]
    nf = D_FF // tf
    steps = (R // tm) * nf
    step = lambda i, f: (i * nf + f, 0)
    cast_specs = []
    for w, wl in casts:
        _, cr, cc = w.shape
        rows = cr // steps
        cast_specs.append((pl.BlockSpec((None, rows, cc), lambda i, f, wl=wl: (wl, i * nf + f, 0)),
                           pl.BlockSpec((rows, cc), step), jax.ShapeDtypeStruct((cr, cc), BF16)))
    res = pl.pallas_call(
        functools.partial(_ffn_kernel, n_cast=len(casts)),
        grid=(R // tm, nf),
        in_specs=[pl.BlockSpec((tm, D_MODEL), lambda i, f: (i, 0)),
                  pl.BlockSpec((RH, D_MODEL), lambda i, f: (0, 0), pipeline_mode=pl.Buffered(1)),
                  _vrow("g_ffn", l),
                  pl.BlockSpec((D_MODEL, tf), lambda i, f: (0, f)),
                  pl.BlockSpec((tf, D_MODEL), lambda i, f: (f, 0))] + [c[0] for c in cast_specs],
        out_specs=[pl.BlockSpec((tm, D_MODEL), lambda i, f: (i, 0)),
                   pl.BlockSpec((RH, D_MODEL), lambda i, f: (0, 0))] + [c[1] for c in cast_specs],
        out_shape=[jax.ShapeDtypeStruct((R, D_MODEL), F32), jax.ShapeDtypeStruct((RH, D_MODEL), F32)]
        + [c[2] for c in cast_specs],
        scratch_shapes=[pltpu.VMEM((tm, D_MODEL), BF16), pltpu.VMEM((RH, D_MODEL), BF16)],
        compiler_params=_params("arbitrary", "arbitrary"),
        name="ffn",
    )(xm, xh, g, w1, w2, *[w for w, _ in casts])
    return res[0], res[1], res[2:]


def _rope_tables(pos):
    half = ROT_HALF
    inv = ROPE_THETA ** (-np.arange(0, ROT_DIM, 2, dtype=np.float64) / ROT_DIM)
    ang = np.asarray(pos, np.float64)[:, None] * inv
    cos, sin = np.cos(ang), np.sin(ang)
    n = ang.shape[0]
    z = np.zeros((n, HEAD_DIM - ROT_DIM))
    zh = np.zeros((n, half))
    rc = np.concatenate([cos, cos, z + 1.0], axis=1)
    rs1 = np.concatenate([zh, sin, z], axis=1)
    rs2 = np.concatenate([-sin, zh, z], axis=1)
    return tuple(t.astype(np.float32) for t in (rc, rs1, rs2))


def kernel(x_prompt, x_sample, cache_k, cache_v, state_ssm_re, state_ssm_im, state_pool, meta_tokens, g_mix, w_in, g_q, g_k, sinks, A_re, A_im, log_dt, B_re, B_im, C_re, C_im, D_skip, w_glu, b_glu, w_pool, pool_scale, g_out_attn, g_out_ssm, g_out_pool, w_out, g_ffn, w_ff1, w_ff2):
    B, T, _ = x_prompt.shape
    N = x_sample.shape[0]
    depth = w_in.shape[0]
    assert N % N_META == 0 and N % DEC_STEP == 0 and T % TM_FFN == 0
    xh = jnp.concatenate([x_sample.reshape(N, D_MODEL)] + [meta_tokens.astype(F32)] * B, axis=0)
    xm = x_prompt.reshape(B * T, D_MODEL)

    rope_m = _rope_tables(N_META + np.arange(T))
    rope_h = _rope_tables(np.concatenate([np.full(N, PAST_LEN)] + [np.arange(N_META)] * B))

    wi = w_in[0].astype(BF16)
    wg_all, wp_all = w_glu.astype(BF16), w_pool.astype(BF16)
    ck = cache_k.astype(F32)
    cv = cache_v.astype(F32)

    named = dict(g_mix=g_mix, g_ffn=g_ffn, g_out_attn=g_out_attn, D_skip=D_skip, b_glu=b_glu,
                 g_out_ssm=g_out_ssm, pool_scale=pool_scale, g_out_pool=g_out_pool, g_q=g_q, g_k=g_k)
    vecs = jnp.concatenate([named[name].astype(F32) for name, _ in VEC_LAYOUT], axis=1)[:, None, :]
    pw, bblk, cblk = _ssm_params(A_re, A_im, log_dt, B_re, B_im, C_re, C_im)
    sinks_flat = sinks.astype(F32).reshape(depth * N_HEADS)
    sinks_col = sinks.astype(F32).reshape(depth, N_KV_HEADS, GQA_GROUP, 1)
    bias = _attn_bias()
    h0 = jnp.concatenate([state_ssm_re.astype(F32).reshape(depth, N, SSM_LANE_BLOCKS, SSM_BLOCK_STATES),
                          state_ssm_im.astype(F32).reshape(depth, N, SSM_LANE_BLOCKS, SSM_BLOCK_STATES)],
                         axis=-1).reshape(depth, N, SSM_STATE_LANES)
    pbuf = state_pool.astype(F32).transpose(0, 2, 1, 3)

    nk = nv = None
    ks, vs, pls, sts, st_ss, phs = ([] for _ in range(6))
    for l in range(depth):
        (qm, km, vm, um, pm), (qh, kh, vh, uh, ph), w1_new = _inproj(
            xm, xh, vecs, wi, l, rope_m, rope_h, TM_PROJ, w_ff1 if l == 0 else None)
        if l == 0:
            w1 = w1_new

        (am, ah, sm, sh, st, plm, plh), wcast = _mixers(
            sinks_flat, (qm, km, vm), (qh, kh, vh), um, uh, pm, ph, vecs, bias, pw, bblk, cblk, wg_all, wp_all,
            l, B, TM_SEQ, [(w_ff2, l), (w_out, l)] if l == 0 else [])
        if l == 0:
            w2, wo = wcast

        ah, nk, nv = _attn_sample(qh, kh, vh, ck, cv, l, sinks_col, vecs, ah, nk, nv, N)
        sh, st_s, plh = _mix_sample(uh, ph, h0, pbuf, vecs, pw, bblk, cblk, wg_all, wp_all, l, sh, plh)

        xm, xh = _outproj(xm, am, sm, plm, xh, ah, sh, plh, wo, TM_PROJ)
        nxt = [(w, l + 1) for w in (w_in, w_ff1, w_ff2, w_out)] if l + 1 < depth else []
        xm, xh, wnext = _ffn(xm, xh, vecs, l, w1, w2, TM_FFN, TF_FFN, nxt)
        if nxt:
            wi, w1, w2, wo = wnext

        ks.append(km.reshape(B, T, KV_WIDTH)[:, T - WINDOW:])
        vs.append(vm.reshape(B, T, KV_WIDTH)[:, T - WINDOW:])
        pls.append(pm.reshape(B, T, POOL_WIDTH)[:, T - POOL_BUF:])
        sts.append(st[:, 0])
        st_ss.append(st_s)
        phs.append(ph[:N])

    y_prompt = xm.reshape(B, T, D_MODEL)
    y_sample = xh[:N].reshape(N, 1, D_MODEL)
    heads = lambda t: jnp.stack(t).reshape(depth, -1, WINDOW, N_KV_HEADS, HEAD_DIM)
    p_re, p_im = _state_from_lanes(jnp.stack(sts).reshape(depth * B, SSM_STATE_LANES))
    s_re, s_im = _state_from_lanes(jnp.stack(st_ss).reshape(depth * N, SSM_STATE_LANES))
    st4 = lambda t, n: t.reshape(depth, n, SSM_GROUPS, SSM_STATE)
    s_pool = jnp.concatenate([state_pool.astype(F32)[:, :, 1:], jnp.stack(phs)[:, :, None]], axis=2)
    return (y_prompt, y_sample, heads(ks), heads(vs), st4(p_re, B), st4(p_im, B), jnp.stack(pls),
            nk, nv, st4(s_re, N), st4(s_im, N), s_pool)
```

```python
import functools
import math

import jax
import jax.numpy as jnp
import numpy as np
from jax.experimental import pallas as pl
from jax.experimental.pallas import tpu as pltpu

D_MODEL = 2048
N_META = 16
HEAD_DIM = 128
N_HEADS = 8
N_KV_HEADS = 2
GQA_GROUP = 4
ATTN_WIDTH = 1024
KV_WIDTH = 256
WINDOW = 128
BLOCK = 128
ROT_DIM = 32
ROT_HALF = ROT_DIM // 2
ROPE_THETA = 500000.0
SSM_WIDTH = 512
SSM_GROUP_SIZE = 16
SSM_GROUPS = 32
SSM_STATE = 64
POOL_WIDTH = 512
POOL_WINDOWS = (2, 4, 8, 16)
POOL_GROUP = 128
POOL_BUF = 15
POOL_HALO = 16
IN_WIDTH = 2560
D_FF = 8192
EPS = 1e-6
PAST_LEN = 16384
LOG2E = math.log2(math.e)

SEQ0 = BLOCK - N_META
LANES = 128
SUBLANES = 8
SSM_LANE_BLOCKS = SSM_WIDTH // LANES
SSM_BLOCK_STATES = (LANES // SSM_GROUP_SIZE) * SSM_STATE
SSM_STATE_LANES = SSM_LANE_BLOCKS * 2 * SSM_BLOCK_STATES
VMEM_LIMIT = 60 * 1024 * 1024

TM_PROJ = 512
TM_FFN = 1024
TF_FFN = 512
TM_SEQ = 512
DEC_STEP = 8
SSM_POW_ROWS = (1, N_META // SUBLANES, TM_SEQ // SUBLANES)
POW_ROW_HEAD, POW_ROW_MAIN = 1, 2

BF16 = jnp.bfloat16
F32 = jnp.float32


def _params(*semantics):
    return pltpu.CompilerParams(dimension_semantics=semantics, vmem_limit_bytes=VMEM_LIMIT)


def _rms(x, g):
    return x * jax.lax.rsqrt(jnp.mean(x * x, axis=-1, keepdims=True) + EPS) * g


def _full(shape):
    n = len(shape)
    return pl.BlockSpec(shape, lambda *_: (0,) * n)


def _layer(shape, l):
    n = len(shape)
    return pl.BlockSpec((None, *shape), lambda *_: (l,) + (0,) * n)


VEC_LAYOUT = (("g_mix", D_MODEL), ("g_ffn", D_MODEL), ("g_out_attn", ATTN_WIDTH), ("D_skip", SSM_WIDTH),
              ("b_glu", SSM_WIDTH), ("g_out_ssm", SSM_WIDTH), ("pool_scale", POOL_WIDTH),
              ("g_out_pool", POOL_WIDTH), ("g_q", HEAD_DIM), ("g_k", HEAD_DIM))
VEC_WIDTH = dict(VEC_LAYOUT)
VEC_OFFSET = {name: sum(w for _, w in VEC_LAYOUT[:i]) for i, (name, _) in enumerate(VEC_LAYOUT)}
assert all(VEC_OFFSET[name] % w == 0 for name, w in VEC_LAYOUT)


def _vrow(name, l):
    w = VEC_WIDTH[name]
    return pl.BlockSpec((None, 1, w), lambda *_: (l, 0, VEC_OFFSET[name] // w))


def _inproj_rows(x, g_ref, w_ref, gq_ref, gk_ref, rope_refs, out_refs):
    q_ref, k_ref, v_ref, u_ref, xp_ref = out_refs
    h = _rms(x, g_ref[...]).astype(BF16)
    proj = jnp.dot(h, w_ref[...], preferred_element_type=F32)
    rc, rs1, rs2 = (r[...] for r in rope_refs)

    def head(t, g):
        t = _rms(t, g)
        return t * rc + pltpu.roll(t, ROT_HALF, 1) * rs1 + pltpu.roll(t, LANES - ROT_HALF, 1) * rs2

    for hd in range(N_HEADS):
        sl = slice(hd * HEAD_DIM, (hd + 1) * HEAD_DIM)
        q_ref[:, sl] = head(proj[:, sl], gq_ref[...])
    for hd in range(N_KV_HEADS):
        sl = slice(hd * HEAD_DIM, (hd + 1) * HEAD_DIM)
        k_ref[:, sl] = head(proj[:, ATTN_WIDTH + hd * HEAD_DIM:ATTN_WIDTH + (hd + 1) * HEAD_DIM], gk_ref[...])
    o2 = ATTN_WIDTH + KV_WIDTH
    o3 = o2 + KV_WIDTH
    o4 = o3 + SSM_WIDTH
    v_ref[...] = proj[:, o2:o3]
    for j in range(SSM_LANE_BLOCKS):
        u_ref[j] = proj[:, o3 + j * LANES:o3 + (j + 1) * LANES]
    xp_ref[...] = proj[:, o4:]


def _inproj_kernel(xm_ref, xh_ref, g_ref, w_ref, gq_ref, gk_ref, rcm_ref, rs1m_ref, rs2m_ref,
                   rch_ref, rs1h_ref, rs2h_ref, *rest):
    cast = len(rest) == 12
    outs = rest[1:] if cast else rest
    main_outs, head_outs = outs[0:5], outs[5:10]
    i = pl.program_id(0)
    last = pl.num_programs(0) - 1

    @pl.when(i < last)
    def _():
        if cast:
            outs[10][...] = rest[0][...].astype(BF16)
        _inproj_rows(xm_ref[...], g_ref, w_ref, gq_ref, gk_ref, (rcm_ref, rs1m_ref, rs2m_ref), main_outs)

    @pl.when(i == last)
    def _():
        _inproj_rows(xh_ref[...], g_ref, w_ref, gq_ref, gk_ref, (rch_ref, rs1h_ref, rs2h_ref), head_outs)


def _inproj(xm, xh, vecs, w, l, rope_m, rope_h, tm, cast=None):
    R, RH = xm.shape[0], xh.shape[0]
    nm = R // tm
    tiles_per_rope = rope_m[0].shape[0] // tm
    tile = lambda i: jnp.minimum(i, nm - 1)
    row = lambda i: (tile(i), 0)
    rrow = lambda i: (tile(i) % tiles_per_rope, 0)
    once = pl.Buffered(1)
    full = lambda n, w_: pl.BlockSpec((n, w_), lambda i: (0, 0))

    def out_set(n, rows, im, uim):
        flat = lambda w_: (pl.BlockSpec((n, w_), im), jax.ShapeDtypeStruct((rows, w_), F32))
        u_out = (pl.BlockSpec((SSM_LANE_BLOCKS, n, LANES), uim),
                 jax.ShapeDtypeStruct((SSM_LANE_BLOCKS, rows, LANES), F32))
        return [flat(ATTN_WIDTH), flat(KV_WIDTH), flat(KV_WIDTH), u_out, flat(POOL_WIDTH)]

    outs = (out_set(tm, R, row, lambda i: (0, tile(i), 0))
            + out_set(RH, RH, lambda i: (0, 0), lambda i: (0, 0, 0)))
    in_specs = [pl.BlockSpec((tm, D_MODEL), row),
                pl.BlockSpec((RH, D_MODEL), lambda i: (0, 0), pipeline_mode=once),
                _vrow("g_mix", l),
                pl.BlockSpec((D_MODEL, IN_WIDTH), lambda i: (0, 0), pipeline_mode=once),
                _vrow("g_q", l), _vrow("g_k", l),
                pl.BlockSpec((tm, LANES), rrow), pl.BlockSpec((tm, LANES), rrow), pl.BlockSpec((tm, LANES), rrow),
                full(RH, LANES), full(RH, LANES), full(RH, LANES)]
    args = [xm, xh, vecs, w, vecs, vecs, *rope_m, *rope_h]
    if cast is not None:
        _, cr, cc = cast.shape
        crows = cr // nm
        in_specs.append(pl.BlockSpec((None, crows, cc), lambda i: (l, tile(i), 0)))
        args.append(cast)
        outs.append((pl.BlockSpec((crows, cc), row), jax.ShapeDtypeStruct((cr, cc), BF16)))
    res = pl.pallas_call(
        _inproj_kernel,
        grid=(nm + 1,),
        in_specs=in_specs,
        out_specs=[o[0] for o in outs],
        out_shape=[o[1] for o in outs],
        compiler_params=_params("arbitrary"),
        name="inproj",
    )(*args)
    return res[0:5], res[5:10], (res[10] if cast is not None else None)


def _attn_bias():
    rows = GQA_GROUP * BLOCK
    i, r, c = np.meshgrid(np.arange(3), np.arange(rows) % BLOCK, np.arange(2 * BLOCK), indexing="ij")
    diff = BLOCK + r - c
    krow = (i - 1) * BLOCK + c
    mask = (diff >= 0) & (diff <= WINDOW) & (krow >= SEQ0)
    return np.where(mask, 0.0, -np.inf).astype(np.float32)


def _attn_block(q_blk, kp_blk, kc_blk, vp_blk, vc_blk, bias, sink_ref, l, g):
    rows = GQA_GROUP * BLOCK
    rgrp = jax.lax.broadcasted_iota(jnp.int32, (rows, 1), 0) // BLOCK
    outs = []
    for kh in range(N_KV_HEADS):
        ksl = slice(kh * HEAD_DIM, (kh + 1) * HEAD_DIM)
        qh = jnp.concatenate(
            [q_blk[:, (kh * GQA_GROUP + h) * HEAD_DIM:(kh * GQA_GROUP + h + 1) * HEAD_DIM]
             for h in range(GQA_GROUP)], axis=0).astype(BF16)
        kk = jnp.concatenate([kp_blk[:, ksl], kc_blk[:, ksl]], axis=0).astype(BF16)
        vv = jnp.concatenate([vp_blk[:, ksl], vc_blk[:, ksl]], axis=0).astype(BF16)
        s = jax.lax.dot_general(qh, kk, (((1,), (1,)), ((), ())),
                                preferred_element_type=F32) * (HEAD_DIM ** -0.5 * LOG2E) + bias
        sk = jnp.zeros((rows, 1), F32)
        for h in range(GQA_GROUP):
            sk = jnp.where(rgrp == h, sink_ref[l * N_HEADS + kh * GQA_GROUP + h] * LOG2E, sk)
        m = jnp.maximum(jnp.max(s, axis=-1, keepdims=True), sk)
        p = jnp.exp2(s - m)
        denom = jnp.sum(p, axis=-1, keepdims=True) + jnp.exp2(sk - m)
        o = jnp.dot(p.astype(BF16), vv, preferred_element_type=F32) / denom
        outs.extend(o[h * BLOCK:(h + 1) * BLOCK] for h in range(GQA_GROUP))
    return _rms(jnp.concatenate(outs, axis=1), g)


def _ssm_params_kernel(ar_ref, ai_ref, ldt_ref, kk_ref, br_ref, bi_ref, tr_ref, ti_ref, bbr_ref, bbi_ref):
    ar, ai = ar_ref[...], ai_ref[...]
    dt = jnp.exp(ldt_ref[...])
    kk = kk_ref[...]
    mag = jnp.exp(dt * ar * kk)
    ang = dt * ai * kk
    tr = mag * jnp.cos(ang)
    ti = mag * jnp.sin(ang)
    tr_ref[...] = tr
    ti_ref[...] = ti
    abr, abi = tr[0:1], ti[0:1]
    den = ar * ar + ai * ai
    fr = ((abr - 1.0) * ar + abi * ai) / den
    fi = (abi * ar - (abr - 1.0) * ai) / den
    br, bi = br_ref[...], bi_ref[...]
    bbr_ref[...] = fr * br - fi * bi
    bbi_ref[...] = fr * bi + fi * br


def _ssm_params(A_re, A_im, log_dt, B_re, B_im, C_re, C_im):
    depth = A_re.shape[0]
    n = depth * SSM_GROUPS * SSM_STATE
    row = lambda t: t.astype(F32).reshape(1, n)
    ldt = jnp.broadcast_to(log_dt.astype(F32)[:, :, None], (depth, SSM_GROUPS, SSM_STATE)).reshape(1, n)
    kk = jnp.array(SSM_POW_ROWS + (0,) * (SUBLANES - len(SSM_POW_ROWS)), F32).reshape(SUBLANES, 1)
    chan_first = lambda t: t.astype(F32).reshape(n, SSM_GROUP_SIZE).T
    shapes = [(SUBLANES, n), (SUBLANES, n), (SSM_GROUP_SIZE, n), (SSM_GROUP_SIZE, n)]
    tr, ti, bbr, bbi = pl.pallas_call(
        _ssm_params_kernel,
        out_shape=[jax.ShapeDtypeStruct(s, F32) for s in shapes],
        name="ssm_params",
    )(row(A_re), row(A_im), ldt, kk, chan_first(B_re), chan_first(B_im))
    J, G8 = SSM_LANE_BLOCKS, LANES // SSM_GROUP_SIZE

    def lanes(t):
        return t.reshape(SUBLANES, depth, J, SSM_BLOCK_STATES).transpose(1, 0, 2, 3)

    pw = jnp.concatenate([lanes(tr), lanes(ti)], axis=-1).reshape(depth, SUBLANES, SSM_STATE_LANES)
    eye = jnp.eye(G8, dtype=F32)

    def bdiag(t):
        t = t.reshape(SSM_GROUP_SIZE, depth, J, G8, SSM_STATE).transpose(1, 2, 3, 0, 4)
        t = t[:, :, :, :, None, :] * eye[None, None, :, None, :, None]
        return t.reshape(depth, J, LANES, SSM_BLOCK_STATES)

    bblk = jnp.concatenate([bdiag(bbr), bdiag(bbi)], axis=-1).astype(BF16)

    def cdiag(t):
        t = t.astype(F32).reshape(depth, J, G8, SSM_GROUP_SIZE, SSM_STATE).transpose(0, 1, 2, 4, 3)
        t = t[:, :, :, :, None, :] * eye[None, None, :, None, :, None]
        return t.reshape(depth, J, SSM_BLOCK_STATES, LANES)

    cblk = jnp.concatenate([cdiag(C_re), -cdiag(C_im)], axis=2).astype(BF16)
    return pw, bblk, cblk


def _ssm_tail(y, u, d_ref, wg_ref, bg_ref, g_ref):
    y = y + d_ref[...] * u
    z = jax.nn.gelu(y)
    gate = jax.nn.sigmoid(jnp.dot(z.astype(BF16), wg_ref[...], preferred_element_type=F32) + bg_ref[...])
    return _rms(z * gate, g_ref[...])


def _ssm_sweep(x_scr, n, a_tabs, init, store):
    S = SSM_BLOCK_STATES
    fins = []
    for j0 in range(0, SSM_LANE_BLOCKS, 2):
        js = (j0, j0 + 1)

        def body(k, carry, js=js):
            r0 = pl.multiple_of(k * SUBLANES, SUBLANES)
            out = []
            for idx, j in enumerate(js):
                hr, hi = carry[2 * idx], carry[2 * idx + 1]
                base = j * 2 * S
                ar, ai = a_tabs[j]
                nhr = ar * hr - ai * hi + x_scr[pl.ds(r0, SUBLANES), base:base + S]
                nhi = ar * hi + ai * hr + x_scr[pl.ds(r0, SUBLANES), base + S:base + 2 * S]
                if store:
                    x_scr[pl.ds(r0, SUBLANES), base:base + S] = nhr
                    x_scr[pl.ds(r0, SUBLANES), base + S:base + 2 * S] = nhi
                out += [nhr, nhi]
            return tuple(out)

        c0 = tuple(t for j in js for t in init[j])
        res = jax.lax.fori_loop(0, n // SUBLANES, body, c0, unroll=True)
        fins += [(res[0], res[1]), (res[2], res[3])]
    return fins


def _ssm_rows(u, pow_row, pw_ref, bblk_ref, cblk_ref, x_scr, s_scr, carry_scr):
    n = u.shape[0]
    S = SSM_BLOCK_STATES
    ub = u.astype(BF16)
    for j in range(SSM_LANE_BLOCKS):
        x_scr[0:n, j * 2 * S:(j + 1) * 2 * S] = jnp.dot(ub[:, j * LANES:(j + 1) * LANES], bblk_ref[j],
                                                       preferred_element_type=F32)
    bc = lambda t: jnp.broadcast_to(t, (SUBLANES, S))
    re = lambda ref, r0, r1, j: ref[r0:r1, j * 2 * S:j * 2 * S + S]
    im = lambda ref, r0, r1, j: ref[r0:r1, j * 2 * S + S:(j + 1) * 2 * S]
    a_tabs = [(bc(re(pw_ref, 0, 1, j)), bc(im(pw_ref, 0, 1, j))) for j in range(SSM_LANE_BLOCKS)]
    zero = jnp.zeros((SUBLANES, S), F32)
    fins = _ssm_sweep(x_scr, n, a_tabs, [(zero, zero)] * SSM_LANE_BLOCKS, store=False)
    for j in range(SSM_LANE_BLOCKS):
        base = j * 2 * S
        cr, ci = re(pw_ref, pow_row, pow_row + 1, j), im(pw_ref, pow_row, pow_row + 1, j)
        sr, si = re(carry_scr, 0, 1, j), im(carry_scr, 0, 1, j)
        fr, fi = fins[j]
        for c in range(SUBLANES):
            s_scr[c:c + 1, base:base + S] = sr
            s_scr[c:c + 1, base + S:base + 2 * S] = si
            sr, si = cr * sr - ci * si + fr[c:c + 1], cr * si + ci * sr + fi[c:c + 1]
        carry_scr[:, base:base + S] = bc(sr)
        carry_scr[:, base + S:base + 2 * S] = bc(si)
    init = [(re(s_scr, 0, SUBLANES, j), im(s_scr, 0, SUBLANES, j)) for j in range(SSM_LANE_BLOCKS)]
    _ssm_sweep(x_scr, n, a_tabs, init, store=True)
    ys = [jnp.dot(x_scr[0:n, j * 2 * S:(j + 1) * 2 * S].astype(BF16), cblk_ref[j], preferred_element_type=F32)
          for j in range(SSM_LANE_BLOCKS)]
    return jnp.concatenate(ys, axis=1)


def _ssm_tile(u_ref, n, pow_row, refs, o_ref, scr):
    pw_ref, bblk_ref, cblk_ref, d_ref, wg_ref, bg_ref, g_ref = refs
    up_scr, x_scr, s_scr, carry_scr, o_scr = scr
    q = n // SUBLANES
    for j in range(SSM_LANE_BLOCKS):
        for k in range(q):
            up_scr[k * SUBLANES:(k + 1) * SUBLANES, j * LANES:(j + 1) * LANES] = \
                u_ref[j, pl.ds(k, SUBLANES, stride=q), :]
    u = up_scr[0:n, :]
    y = _ssm_rows(u, pow_row, pw_ref, bblk_ref, cblk_ref, x_scr, s_scr, carry_scr)
    out = _ssm_tail(y, u, d_ref, wg_ref, bg_ref, g_ref)
    for j in range(SSM_LANE_BLOCKS):
        for k in range(q):
            o_scr[j, pl.ds(k, SUBLANES, stride=q), :] = out[k * SUBLANES:(k + 1) * SUBLANES,
                                                            j * LANES:(j + 1) * LANES]
    o_ref[...] = jnp.concatenate([o_scr[j, 0:n, :] for j in range(SSM_LANE_BLOCKS)], axis=1).astype(o_ref.dtype)


def _mixers_kernel(sink_ref,
                   qm_ref, qh_ref, kpm_ref, km_ref, kh_ref, vpm_ref, vm_ref, vh_ref, ga_ref, bias_ref,
                   um_ref, uh_ref, pw_ref, bblk_ref, cblk_ref, d_ref, wg_ref, bg_ref, gs_ref,
                   xm_ref, halo_ref, xh_ref, wp_ref, sc_ref, gp_ref, *rest, l, n_cast, n_dec):
    cast_in = rest[0:n_cast]
    am_ref, ah_ref, som_ref, soh_ref, st_ref, pom_ref, poh_ref = rest[n_cast:n_cast + 7]
    cast_out = rest[n_cast + 7:2 * n_cast + 7]
    scr = rest[2 * n_cast + 7:]
    carry_scr = scr[3]
    b = pl.program_id(0)
    t = pl.program_id(1)
    tm = xm_ref.shape[0]
    ssm_refs = (pw_ref, bblk_ref, cblk_ref, d_ref, wg_ref, bg_ref, gs_ref)
    blk = functools.partial(_attn_block, sink_ref=sink_ref, l=l, g=ga_ref[...])
    meta_rows = pl.ds(pl.multiple_of(n_dec + b * N_META, N_META), N_META)

    @pl.when((b == 0) & (t == 0))
    def _():
        for ref in (ah_ref, soh_ref, poh_ref):
            ref[0:n_dec, :] = jnp.zeros((n_dec, ref.shape[1]), ref.dtype)

    block = lambda ref: jnp.concatenate([jnp.zeros((SEQ0, ref.shape[1]), F32), ref[...]], axis=0)

    @pl.when(t == 0)
    def _():
        a = blk(block(qh_ref), kpm_ref[...], block(kh_ref), vpm_ref[...], block(vh_ref), bias_ref[0])
        ah_ref[meta_rows, :] = a[SEQ0:].astype(ah_ref.dtype)
        carry_scr[...] = jnp.zeros_like(carry_scr)
        _ssm_tile(uh_ref, N_META, POW_ROW_HEAD, ssm_refs, soh_ref.at[meta_rows, :], scr)
        prev = jnp.zeros((POOL_HALO, POOL_WIDTH), F32)
        poh_ref[meta_rows, :] = _pool_rows(xh_ref[...], prev, 0, wp_ref, sc_ref, gp_ref).astype(poh_ref.dtype)

    @pl.when(t > 0)
    def _():
        for src, dst in zip(cast_in, cast_out):
            dst[...] = src[...].astype(BF16)
        kp = jnp.where(t == 1, block(kh_ref), kpm_ref[...])
        vp = jnp.where(t == 1, block(vh_ref), vpm_ref[...])
        for n in range(tm // BLOCK):
            rows = slice(n * BLOCK, (n + 1) * BLOCK)
            kc, vc = km_ref[rows, :], vm_ref[rows, :]
            bias = bias_ref[jnp.minimum(t, 2)] if n == 0 else bias_ref[2]
            am_ref[rows, :] = blk(qm_ref[rows, :], kp, kc, vp, vc, bias).astype(am_ref.dtype)
            kp, vp = kc, vc
        _ssm_tile(um_ref, um_ref.shape[1], POW_ROW_MAIN, ssm_refs, som_ref, scr)
        st_ref[...] = carry_scr[...]
        prev = jnp.where(t == 1, xh_ref[...], halo_ref[...])
        pom_ref[...] = _pool_rows(xm_ref[...], prev, N_META + (t - 1) * tm, wp_ref, sc_ref, gp_ref).astype(pom_ref.dtype)


def _mixers(sinks, qkv_m, qkv_h, um, uh, xm, xh, vecs, bias, pw, bblk, cblk, wg, wp, l, B, tm, casts):
    assert (1, N_META // SUBLANES, tm // SUBLANES) == SSM_POW_ROWS and N_META == POOL_HALO
    J = SSM_LANE_BLOCKS
    rm, rh = um.shape[1], uh.shape[1]
    nt = rm // (B * tm)
    r = tm // POOL_HALO
    bpt = tm // BLOCK
    tile = lambda b, t: b * nt + jnp.maximum(t - 1, 0)
    main = lambda b, t, _: (tile(b, t), 0)
    n_dec = uh.shape[1] - B * N_META
    head = lambda b, t, _: (n_dec // N_META + b, 0)
    halo = lambda b, t, _: (jnp.maximum((b * nt + t - 1) * r - 1, 0), 0)
    prevb = lambda b, t, _: (jnp.maximum(tile(b, t) * bpt - 1, 0), 0)
    cast_specs = []
    for w, wl in casts:
        _, cr, cc = w.shape
        rows = cr // (B * nt)
        cast_specs.append((pl.BlockSpec((None, rows, cc), lambda b, t, _, wl=wl: (wl, tile(b, t), 0)),
                           pl.BlockSpec((rows, cc), main), jax.ShapeDtypeStruct((cr, cc), BF16)))
    bf = lambda n, w: jax.ShapeDtypeStruct((n, w), BF16)
    tile_spec = lambda w: pl.BlockSpec((tm, w), main)
    head_spec = lambda w: pl.BlockSpec((N_META, w), head)
    in_specs = [tile_spec(ATTN_WIDTH), head_spec(ATTN_WIDTH),
                pl.BlockSpec((BLOCK, KV_WIDTH), prevb), tile_spec(KV_WIDTH), head_spec(KV_WIDTH),
                pl.BlockSpec((BLOCK, KV_WIDTH), prevb), tile_spec(KV_WIDTH), head_spec(KV_WIDTH),
                _vrow("g_out_attn", l), _full(bias.shape),
                pl.BlockSpec((J, tm, LANES), lambda b, t, _: (0, tile(b, t), 0)),
                pl.BlockSpec((J, N_META, LANES), lambda b, t, _: (0, n_dec // N_META + b, 0)),
                _layer(pw.shape[1:], l), _layer(bblk.shape[1:], l), _layer(cblk.shape[1:], l),
                _vrow("D_skip", l), _layer((SSM_WIDTH, SSM_WIDTH), l), _vrow("b_glu", l), _vrow("g_out_ssm", l),
                tile_spec(POOL_WIDTH), pl.BlockSpec((POOL_HALO, POOL_WIDTH), halo), head_spec(POOL_WIDTH),
                _layer(wp.shape[1:], l), _vrow("pool_scale", l), _vrow("g_out_pool", l)]
    in_specs += [c[0] for c in cast_specs]
    head_out = lambda w: pl.BlockSpec((rh, w), lambda b, t, _: (0, 0))
    out_specs = [tile_spec(ATTN_WIDTH), head_out(ATTN_WIDTH), tile_spec(SSM_WIDTH), head_out(SSM_WIDTH),
                 pl.BlockSpec((None, SUBLANES, SSM_STATE_LANES), lambda b, t, _: (b, 0, 0)),
                 tile_spec(POOL_WIDTH), head_out(POOL_WIDTH)] + [c[1] for c in cast_specs]
    out_shape = [bf(rm, ATTN_WIDTH), bf(rh, ATTN_WIDTH), bf(rm, SSM_WIDTH), bf(rh, SSM_WIDTH),
                 jax.ShapeDtypeStruct((B, SUBLANES, SSM_STATE_LANES), F32),
                 bf(rm, POOL_WIDTH), bf(rh, POOL_WIDTH)] + [c[2] for c in cast_specs]
    (qm, km, vm), (qh, kh, vh) = qkv_m, qkv_h
    res = pl.pallas_call(
        functools.partial(_mixers_kernel, l=l, n_cast=len(casts), n_dec=n_dec),
        grid_spec=pltpu.PrefetchScalarGridSpec(
            num_scalar_prefetch=1, grid=(B, nt + 1), in_specs=in_specs, out_specs=out_specs,
            scratch_shapes=[pltpu.VMEM((tm, SSM_WIDTH), F32),
                            pltpu.VMEM((tm, SSM_STATE_LANES), F32),
                            pltpu.VMEM((SUBLANES, SSM_STATE_LANES), F32),
                            pltpu.VMEM((SUBLANES, SSM_STATE_LANES), F32),
                            pltpu.VMEM((J, tm, LANES), F32)]),
        out_shape=out_shape,
        compiler_params=_params("arbitrary", "arbitrary"),
        name="mixers",
    )(sinks, qm, qh, km, km, kh, vm, vm, vh, vecs, bias, um, uh, pw, bblk, cblk, vecs, wg, vecs, vecs,
      xm, xm, xh, wp, vecs, vecs, *[w for w, _ in casts])
    return res[0:7], res[7:]


def _state_from_lanes(s):
    s = s.reshape(s.shape[0], SSM_LANE_BLOCKS, 2, SSM_BLOCK_STATES)
    return (s[:, :, 0].reshape(-1, SSM_GROUPS, SSM_STATE), s[:, :, 1].reshape(-1, SSM_GROUPS, SSM_STATE))


def _pool_tail(d_groups, w_ref, sc_ref, g_ref):
    y = jnp.concatenate(
        [jnp.dot(d.astype(BF16), w_ref[gi], preferred_element_type=F32) for gi, d in enumerate(d_groups)], axis=1)
    return _rms(y * sc_ref[...], g_ref[...])


def _pool_rows(x, prev, pos0, w_ref, sc_ref, g_ref):
    n = x.shape[0]
    xe = jnp.concatenate([prev, x], axis=0)
    pos = pos0 + jax.lax.broadcasted_iota(jnp.int32, (n, 1), 0)
    ds = []
    for gi, w in enumerate(POOL_WINDOWS):
        gsl = slice(gi * POOL_GROUP, (gi + 1) * POOL_GROUP)
        s = xe[:, gsl]
        k = 1
        while k < w:
            s = s + pltpu.roll(s, k, 0)
            k *= 2
        cnt = jnp.clip(pos + 1, 1, w).astype(F32)
        ds.append(s[POOL_HALO:] / cnt - x[:, gsl])
    return _pool_tail(ds, w_ref, sc_ref, g_ref)


def _attn_sample_kernel(q_ref, kn_ref, vn_ref, kc_ref, vc_ref, sink_ref, g_ref, *rest, first):
    a_ref, nk_ref, nv_ref, acc_scr = rest[-4:]
    step = pl.program_id(1 if first else 0)
    nsteps = pl.num_programs(1 if first else 0)
    slot = WINDOW - 1 if first else 0
    scale = HEAD_DIM ** -0.5
    if first:
        for bb in range(DEC_STEP):
            nk_ref[bb, 0:WINDOW - 1] = kc_ref[bb, 1:WINDOW]
            nv_ref[bb, 0:WINDOW - 1] = vc_ref[bb, 1:WINDOW]
            nk_ref[bb, WINDOW - 1] = jnp.zeros((N_KV_HEADS, HEAD_DIM), F32)
            nv_ref[bb, WINDOW - 1] = jnp.zeros((N_KV_HEADS, HEAD_DIM), F32)

    def attend():
        _attn_sample_rows(q_ref, kn_ref, vn_ref, kc_ref, vc_ref, sink_ref, g_ref, nk_ref, nv_ref, acc_scr,
                          step, slot, scale)

        @pl.when(step == nsteps - 1)
        def _():
            a_ref[...] = acc_scr[...].astype(a_ref.dtype)

    if first:
        pl.when(pl.program_id(0) == 0)(attend)
    else:
        attend()


def _attn_sample_rows(q_ref, kn_ref, vn_ref, kc_ref, vc_ref, sink_ref, g_ref, nk_ref, nv_ref, acc_scr,
                      step, slot, scale):
    for bb in range(DEC_STEP):
        outs = []
        for kh in range(N_KV_HEADS):
            ksl = slice(kh * HEAD_DIM, (kh + 1) * HEAD_DIM)
            qh = jnp.concatenate(
                [q_ref[bb:bb + 1, (kh * GQA_GROUP + g) * HEAD_DIM:(kh * GQA_GROUP + g + 1) * HEAD_DIM]
                 for g in range(GQA_GROUP)], axis=0)
            kn = kn_ref[bb:bb + 1, ksl]
            vn = vn_ref[bb:bb + 1, ksl]
            nk_ref[bb, slot, kh:kh + 1, :] = kn
            nv_ref[bb, slot, kh:kh + 1, :] = vn
            kc = kc_ref[bb, :, kh, :]
            vc = vc_ref[bb, :, kh, :]
            sc = jax.lax.dot_general(qh.astype(BF16), kc.astype(BF16), (((1,), (1,)), ((), ())),
                                     preferred_element_type=F32) * scale
            sn = jnp.sum(qh * kn, axis=-1, keepdims=True) * scale
            sk = sink_ref[kh]
            m = jnp.maximum(jnp.maximum(jnp.max(sc, axis=-1, keepdims=True), sn), sk)
            pc = jnp.exp(sc - m)
            pn = jnp.exp(sn - m)
            denom = jnp.sum(pc, axis=-1, keepdims=True) + pn + jnp.exp(sk - m)
            o = jnp.dot(pc.astype(BF16), vc.astype(BF16), preferred_element_type=F32)
            o = (o + pn * vn) / denom
            outs.extend(o[g:g + 1] for g in range(GQA_GROUP))
        a = jnp.concatenate(outs, axis=1)
        acc_scr[pl.ds(step * DEC_STEP + bb, 1), :] = _rms(a, g_ref[...])


def _attn_sample(qh, kh, vh, cache_k, cache_v, l, sinks, g, ah, nk, nv, N):
    depth = cache_k.shape[0]
    first = nk is None
    anyspec = pl.BlockSpec(memory_space=pl.ANY)
    cshape = jax.ShapeDtypeStruct(cache_k.shape, F32)
    slab = (None, DEC_STEP, WINDOW, N_KV_HEADS, HEAD_DIM)
    if first:
        assert l == 0
        grid = (depth, N // DEC_STEP)
        rows = lambda w: pl.BlockSpec((DEC_STEP, w), lambda ll, s: (s, 0))
        cache = pl.BlockSpec(slab, lambda ll, s: (ll, s, 0, 0, 0))
        new = cache
        a_spec = pl.BlockSpec((N, ATTN_WIDTH), lambda ll, s: (0, 0))
        carried = []
    else:
        grid = (N // DEC_STEP,)
        rows = lambda w: pl.BlockSpec((DEC_STEP, w), lambda s: (s, 0))
        cache = pl.BlockSpec(slab, lambda s: (l, s, 0, 0, 0))
        new = pl.BlockSpec((None, DEC_STEP, 1, N_KV_HEADS, HEAD_DIM), lambda s: (l, s, WINDOW - 1, 0, 0))
        a_spec = pl.BlockSpec((N, ATTN_WIDTH), lambda s: (0, 0))
        carried = [nk, nv]
    return pl.pallas_call(
        functools.partial(_attn_sample_kernel, first=first),
        grid=grid,
        in_specs=[rows(ATTN_WIDTH), rows(KV_WIDTH), rows(KV_WIDTH), cache, cache,
                  _layer((N_KV_HEADS, GQA_GROUP, 1), l), _vrow("g_out_attn", l)] + [anyspec] * (1 + len(carried)),
        out_specs=[a_spec, new, new],
        out_shape=[jax.ShapeDtypeStruct(ah.shape, ah.dtype), cshape, cshape],
        scratch_shapes=[pltpu.VMEM((N, ATTN_WIDTH), F32)],
        input_output_aliases={7 + i: i for i in range(1 + len(carried))},
        compiler_params=_params(*(["arbitrary"] * len(grid))),
        name="attn_sample",
    )(qh, kh, vh, cache_k, cache_v, sinks, g, ah, *carried)


def _mix_sample_kernel(u_ref, h0_ref, pw_ref, bblk_ref, cblk_ref, d_ref, wg_ref, bg_ref, gs_ref,
                       xp_ref, pb_ref, wp_ref, sc_ref, gp_ref, s_in_ref, p_in_ref, s_ref, st_ref, p_ref):
    del s_in_ref, p_in_ref
    S = SSM_BLOCK_STATES
    u = jnp.concatenate([u_ref[j] for j in range(SSM_LANE_BLOCKS)], axis=1)
    ub = u.astype(BF16)
    ys = []
    for j in range(SSM_LANE_BLOCKS):
        x = jnp.dot(ub[:, j * LANES:(j + 1) * LANES], bblk_ref[j], preferred_element_type=F32)
        base = j * 2 * S
        ar = pw_ref[0:1, base:base + S]
        ai = pw_ref[0:1, base + S:base + 2 * S]
        h0r = h0_ref[:, base:base + S]
        h0i = h0_ref[:, base + S:base + 2 * S]
        hr = x[:, 0:S] + ar * h0r - ai * h0i
        hi = x[:, S:] + ar * h0i + ai * h0r
        st_ref[:, base:base + S] = hr
        st_ref[:, base + S:base + 2 * S] = hi
        h = jnp.concatenate([hr, hi], axis=1).astype(BF16)
        ys.append(jnp.dot(h, cblk_ref[j], preferred_element_type=F32))
    s_ref[...] = _ssm_tail(jnp.concatenate(ys, axis=1), u, d_ref, wg_ref, bg_ref, gs_ref).astype(s_ref.dtype)

    xp = xp_ref[...]
    ds = []
    for gi, w in enumerate(POOL_WINDOWS):
        gsl = slice(gi * POOL_GROUP, (gi + 1) * POOL_GROUP)
        s = xp[:, gsl]
        for back in range(1, w):
            s = s + pb_ref[POOL_BUF - back][:, gsl]
        ds.append(s / float(w) - xp[:, gsl])
    p_ref[...] = _pool_tail(ds, wp_ref, sc_ref, gp_ref).astype(p_ref.dtype)


def _mix_sample(uh, xph, h0, pbuf, vecs, pw, bblk, cblk, wg, wp, l, sh, ph):
    N = h0.shape[1]
    rows = lambda w: pl.BlockSpec((N, w), lambda i: (0, 0))
    anyspec = pl.BlockSpec(memory_space=pl.ANY)
    return pl.pallas_call(
        _mix_sample_kernel,
        grid=(1,),
        in_specs=[pl.BlockSpec((SSM_LANE_BLOCKS, N, LANES), lambda i: (0, 0, 0)),
                  _layer(h0.shape[1:], l), _layer(pw.shape[1:], l), _layer(bblk.shape[1:], l),
                  _layer(cblk.shape[1:], l),
                  _vrow("D_skip", l), _layer((SSM_WIDTH, SSM_WIDTH), l), _vrow("b_glu", l),
                  _vrow("g_out_ssm", l), rows(POOL_WIDTH), _layer(pbuf.shape[1:], l),
                  _layer(wp.shape[1:], l),
                  _vrow("pool_scale", l), _vrow("g_out_pool", l), anyspec, anyspec],
        out_specs=[rows(SSM_WIDTH), _full((N, SSM_STATE_LANES)), rows(POOL_WIDTH)],
        out_shape=[jax.ShapeDtypeStruct(sh.shape, sh.dtype), jax.ShapeDtypeStruct((N, SSM_STATE_LANES), F32),
                   jax.ShapeDtypeStruct(ph.shape, ph.dtype)],
        input_output_aliases={14: 0, 15: 2},
        compiler_params=_params("arbitrary"),
        name="mix_sample",
    )(uh, h0, pw, bblk, cblk, vecs, wg, vecs, vecs, xph, pbuf, wp, vecs, vecs, sh, ph)


def _outproj_rows(x_ref, a_ref, s_ref, p_ref, w_ref, o_ref):
    o1 = ATTN_WIDTH
    o2 = o1 + SSM_WIDTH
    acc = x_ref[...]
    acc = acc + jnp.dot(a_ref[...], w_ref[0:o1, :], preferred_element_type=F32)
    acc = acc + jnp.dot(s_ref[...], w_ref[o1:o2, :], preferred_element_type=F32)
    acc = acc + jnp.dot(p_ref[...], w_ref[o2:, :], preferred_element_type=F32)
    o_ref[...] = acc


def _outproj_kernel(xm_ref, am_ref, sm_ref, pm_ref, xh_ref, ah_ref, sh_ref, ph_ref, w_ref, om_ref, oh_ref):
    i = pl.program_id(0)
    last = pl.num_programs(0) - 1

    @pl.when(i < last)
    def _():
        _outproj_rows(xm_ref, am_ref, sm_ref, pm_ref, w_ref, om_ref)

    @pl.when(i == last)
    def _():
        _outproj_rows(xh_ref, ah_ref, sh_ref, ph_ref, w_ref, oh_ref)


def _outproj(xm, am, sm, pm, xh, ah, sh, ph, w, tm):
    R, RH = xm.shape[0], xh.shape[0]
    nm = R // tm
    row = lambda i: (jnp.minimum(i, nm - 1), 0)
    once = pl.Buffered(1)
    head = lambda w_: pl.BlockSpec((RH, w_), lambda i: (0, 0), pipeline_mode=once)
    return pl.pallas_call(
        _outproj_kernel,
        grid=(nm + 1,),
        in_specs=[pl.BlockSpec((tm, D_MODEL), row), pl.BlockSpec((tm, ATTN_WIDTH), row),
                  pl.BlockSpec((tm, SSM_WIDTH), row), pl.BlockSpec((tm, POOL_WIDTH), row),
                  head(D_MODEL), head(ATTN_WIDTH), head(SSM_WIDTH), head(POOL_WIDTH),
                  pl.BlockSpec((D_MODEL, D_MODEL), lambda i: (0, 0), pipeline_mode=once)],
        out_specs=[pl.BlockSpec((tm, D_MODEL), row), pl.BlockSpec((RH, D_MODEL), lambda i: (0, 0))],
        out_shape=[jax.ShapeDtypeStruct((R, D_MODEL), F32), jax.ShapeDtypeStruct((RH, D_MODEL), F32)],
        compiler_params=_params("arbitrary"),
        name="outproj",
    )(xm, am, sm, pm, xh, ah, sh, ph, w)


def _ffn_kernel(xm_ref, xh_ref, g_ref, w1_hbm, w2_hbm, om_ref, oh_ref, hm_scr, hh_scr, w1_buf, w2_buf, sem, *, tf):
    i = pl.program_id(0)
    last_tile = pl.num_programs(0) - 1
    nf = D_FF // tf

    def copies(f, slot):
        col = pl.multiple_of(f * tf, tf)
        return (pltpu.make_async_copy(w1_hbm.at[:, pl.ds(col, tf)], w1_buf.at[slot], sem.at[0, slot]),
                pltpu.make_async_copy(w2_hbm.at[pl.ds(col, tf), :], w2_buf.at[slot], sem.at[1, slot]))

    def start(f, slot):
        for c in copies(f, slot):
            c.start()

    def wait(f, slot):
        for c in copies(f, slot):
            c.wait()

    def mlp(h, slot):
        h1 = jnp.dot(h, w1_buf[slot], preferred_element_type=F32)
        return jnp.dot(jnp.square(jnp.maximum(h1, 0.0)).astype(BF16), w2_buf[slot], preferred_element_type=F32)

    def first(x_ref, h_scr, o_ref):
        x = x_ref[...]
        h = _rms(x, g_ref[...]).astype(BF16)
        h_scr[...] = h
        o_ref[...] = x + mlp(h, 0)

    @pl.when(i == 0)
    def _():
        start(0, 0)

    wait(0, 0)
    start(1, 1)
    first(xm_ref, hm_scr, om_ref)

    @pl.when(i == 0)
    def _():
        first(xh_ref, hh_scr, oh_ref)

    def body(f, carry):
        slot = f % 2
        wait(f, slot)

        @pl.when(f + 1 < nf)
        def _():
            start(f + 1, 1 - slot)

        @pl.when((f + 1 == nf) & (i < last_tile))
        def _():
            start(0, 0)

        om_ref[...] += mlp(hm_scr[...], slot)

        @pl.when(i == 0)
        def _():
            oh_ref[...] += mlp(hh_scr[...], slot)

        return carry

    jax.lax.fori_loop(1, nf, body, 0)


def _ffn(xm, xh, g, l, w1, w2, tm, tf):
    R, RH = xm.shape[0], xh.shape[0]
    assert (D_FF // tf) % 2 == 0
    anyspec = pl.BlockSpec(memory_space=pl.ANY)
    return pl.pallas_call(
        functools.partial(_ffn_kernel, tf=tf),
        grid=(R // tm,),
        in_specs=[pl.BlockSpec((tm, D_MODEL), lambda i: (i, 0)),
                  pl.BlockSpec((RH, D_MODEL), lambda i: (0, 0), pipeline_mode=pl.Buffered(1)),
                  _vrow("g_ffn", l), anyspec, anyspec],
        out_specs=[pl.BlockSpec((tm, D_MODEL), lambda i: (i, 0)), pl.BlockSpec((RH, D_MODEL), lambda i: (0, 0))],
        out_shape=[jax.ShapeDtypeStruct((R, D_MODEL), F32), jax.ShapeDtypeStruct((RH, D_MODEL), F32)],
        scratch_shapes=[pltpu.VMEM((tm, D_MODEL), BF16), pltpu.VMEM((RH, D_MODEL), BF16),
                        pltpu.VMEM((2, D_MODEL, tf), BF16), pltpu.VMEM((2, tf, D_MODEL), BF16),
                        pltpu.SemaphoreType.DMA((2, 2))],
        compiler_params=_params("arbitrary"),
        name="ffn",
    )(xm, xh, g, w1, w2)


def _rope_tables(pos):
    half = ROT_HALF
    inv = ROPE_THETA ** (-np.arange(0, ROT_DIM, 2, dtype=np.float64) / ROT_DIM)
    ang = np.asarray(pos, np.float64)[:, None] * inv
    cos, sin = np.cos(ang), np.sin(ang)
    n = ang.shape[0]
    z = np.zeros((n, HEAD_DIM - ROT_DIM))
    zh = np.zeros((n, half))
    rc = np.concatenate([cos, cos, z + 1.0], axis=1)
    rs1 = np.concatenate([zh, sin, z], axis=1)
    rs2 = np.concatenate([-sin, zh, z], axis=1)
    return tuple(t.astype(np.float32) for t in (rc, rs1, rs2))


def kernel(x_prompt, x_sample, cache_k, cache_v, state_ssm_re, state_ssm_im, state_pool, meta_tokens, g_mix, w_in, g_q, g_k, sinks, A_re, A_im, log_dt, B_re, B_im, C_re, C_im, D_skip, w_glu, b_glu, w_pool, pool_scale, g_out_attn, g_out_ssm, g_out_pool, w_out, g_ffn, w_ff1, w_ff2):
    B, T, _ = x_prompt.shape
    N = x_sample.shape[0]
    depth = w_in.shape[0]
    assert N % N_META == 0 and N % DEC_STEP == 0 and T % TM_FFN == 0
    xh = jnp.concatenate([x_sample.reshape(N, D_MODEL)] + [meta_tokens.astype(F32)] * B, axis=0)
    xm = x_prompt.reshape(B * T, D_MODEL)

    rope_m = _rope_tables(N_META + np.arange(T))
    rope_h = _rope_tables(np.concatenate([np.full(N, PAST_LEN)] + [np.arange(N_META)] * B))

    wi = w_in[0].astype(BF16)
    wg_all, wp_all = w_glu.astype(BF16), w_pool.astype(BF16)
    ck = cache_k.astype(F32)
    cv = cache_v.astype(F32)

    named = dict(g_mix=g_mix, g_ffn=g_ffn, g_out_attn=g_out_attn, D_skip=D_skip, b_glu=b_glu,
                 g_out_ssm=g_out_ssm, pool_scale=pool_scale, g_out_pool=g_out_pool, g_q=g_q, g_k=g_k)
    vecs = jnp.concatenate([named[name].astype(F32) for name, _ in VEC_LAYOUT], axis=1)[:, None, :]
    pw, bblk, cblk = _ssm_params(A_re, A_im, log_dt, B_re, B_im, C_re, C_im)
    sinks_flat = sinks.astype(F32).reshape(depth * N_HEADS)
    sinks_col = sinks.astype(F32).reshape(depth, N_KV_HEADS, GQA_GROUP, 1)
    bias = _attn_bias()
    h0 = jnp.concatenate([state_ssm_re.astype(F32).reshape(depth, N, SSM_LANE_BLOCKS, SSM_BLOCK_STATES),
                          state_ssm_im.astype(F32).reshape(depth, N, SSM_LANE_BLOCKS, SSM_BLOCK_STATES)],
                         axis=-1).reshape(depth, N, SSM_STATE_LANES)
    pbuf = state_pool.astype(F32).transpose(0, 2, 1, 3)

    nk = nv = None
    ks, vs, pls, sts, st_ss, phs = ([] for _ in range(6))
    for l in range(depth):
        (qm, km, vm, um, pm), (qh, kh, vh, uh, ph), w1 = _inproj(xm, xh, vecs, wi, l, rope_m, rope_h, TM_PROJ, w_ff1)

        (am, ah, sm, sh, st, plm, plh), wcast = _mixers(
            sinks_flat, (qm, km, vm), (qh, kh, vh), um, uh, pm, ph, vecs, bias, pw, bblk, cblk, wg_all, wp_all,
            l, B, TM_SEQ, [(w_ff2, l), (w_out, l)] + ([(w_in, l + 1)] if l + 1 < depth else []))
        w2, wo = wcast[0], wcast[1]

        ah, nk, nv = _attn_sample(qh, kh, vh, ck, cv, l, sinks_col, vecs, ah, nk, nv, N)
        sh, st_s, plh = _mix_sample(uh, ph, h0, pbuf, vecs, pw, bblk, cblk, wg_all, wp_all, l, sh, plh)

        xm, xh = _outproj(xm, am, sm, plm, xh, ah, sh, plh, wo, TM_PROJ)
        xm, xh = _ffn(xm, xh, vecs, l, w1, w2, TM_FFN, TF_FFN)
        if l + 1 < depth:
            wi = wcast[2]

        ks.append(km.reshape(B, T, KV_WIDTH)[:, T - WINDOW:])
        vs.append(vm.reshape(B, T, KV_WIDTH)[:, T - WINDOW:])
        pls.append(pm.reshape(B, T, POOL_WIDTH)[:, T - POOL_BUF:])
        sts.append(st[:, 0])
        st_ss.append(st_s)
        phs.append(ph[:N])

    y_prompt = xm.reshape(B, T, D_MODEL)
    y_sample = xh[:N].reshape(N, 1, D_MODEL)
    heads = lambda t: jnp.stack(t).reshape(depth, -1, WINDOW, N_KV_HEADS, HEAD_DIM)
    p_re, p_im = _state_from_lanes(jnp.stack(sts).reshape(depth * B, SSM_STATE_LANES))
    s_re, s_im = _state_from_lanes(jnp.stack(st_ss).reshape(depth * N, SSM_STATE_LANES))
    st4 = lambda t, n: t.reshape(depth, n, SSM_GROUPS, SSM_STATE)
    s_pool = jnp.concatenate([state_pool.astype(F32)[:, :, 1:], jnp.stack(phs)[:, :, None]], axis=2)
    return (y_prompt, y_sample, heads(ks), heads(vs), st4(p_re, B), st4(p_im, B), jnp.stack(pls),
            nk, nv, st4(s_re, N), st4(s_im, N), s_pool)
```

```python
import functools
import math

import jax
import jax.numpy as jnp
import numpy as np
from jax.experimental import pallas as pl
from jax.experimental.pallas import tpu as pltpu

D_MODEL = 2048
N_META = 16
HEAD_DIM = 128
N_HEADS = 8
N_KV_HEADS = 2
GQA_GROUP = 4
ATTN_WIDTH = 1024
KV_WIDTH = 256
WINDOW = 128
BLOCK = 128
ROT_DIM = 32
ROT_HALF = ROT_DIM // 2
ROPE_THETA = 500000.0
SSM_WIDTH = 512
SSM_GROUP_SIZE = 16
SSM_GROUPS = 32
SSM_STATE = 64
POOL_WIDTH = 512
POOL_WINDOWS = (2, 4, 8, 16)
POOL_GROUP = 128
POOL_BUF = 15
POOL_HALO = 16
IN_WIDTH = 2560
D_FF = 8192
EPS = 1e-6
PAST_LEN = 16384
LOG2E = math.log2(math.e)

SEQ0 = BLOCK - N_META
LANES = 128
SUBLANES = 8
SSM_LANE_BLOCKS = SSM_WIDTH // LANES
SSM_BLOCK_STATES = (LANES // SSM_GROUP_SIZE) * SSM_STATE
SSM_STATE_LANES = SSM_LANE_BLOCKS * 2 * SSM_BLOCK_STATES
VMEM_LIMIT = 60 * 1024 * 1024

TM_PROJ = 512
TM_FFN = 1024
TF_FFN = 512
TM_SEQ = 512
DEC_STEP = 8
SSM_POW_ROWS = (1, N_META // SUBLANES, TM_SEQ // SUBLANES)
POW_ROW_HEAD, POW_ROW_MAIN = 1, 2

BF16 = jnp.bfloat16
F32 = jnp.float32


def _params(*semantics):
    return pltpu.CompilerParams(dimension_semantics=semantics, vmem_limit_bytes=VMEM_LIMIT)


def _rms(x, g):
    return x * jax.lax.rsqrt(jnp.mean(x * x, axis=-1, keepdims=True) + EPS) * g


def _full(shape):
    n = len(shape)
    return pl.BlockSpec(shape, lambda *_: (0,) * n)


def _layer(shape, l):
    n = len(shape)
    return pl.BlockSpec((None, *shape), lambda *_: (l,) + (0,) * n)


VEC_LAYOUT = (("g_mix", D_MODEL), ("g_ffn", D_MODEL), ("g_out_attn", ATTN_WIDTH), ("D_skip", SSM_WIDTH),
              ("b_glu", SSM_WIDTH), ("g_out_ssm", SSM_WIDTH), ("pool_scale", POOL_WIDTH),
              ("g_out_pool", POOL_WIDTH), ("g_q", HEAD_DIM), ("g_k", HEAD_DIM))
VEC_WIDTH = dict(VEC_LAYOUT)
VEC_OFFSET = {name: sum(w for _, w in VEC_LAYOUT[:i]) for i, (name, _) in enumerate(VEC_LAYOUT)}
assert all(VEC_OFFSET[name] % w == 0 for name, w in VEC_LAYOUT)


def _vrow(name, l):
    w = VEC_WIDTH[name]
    return pl.BlockSpec((None, 1, w), lambda *_: (l, 0, VEC_OFFSET[name] // w))


def _inproj_rows(x, g_ref, w_ref, gq_ref, gk_ref, rope_refs, out_refs):
    q_ref, k_ref, v_ref, u_ref, xp_ref = out_refs
    h = _rms(x, g_ref[...]).astype(BF16)
    proj = jnp.dot(h, w_ref[...], preferred_element_type=F32)
    rc, rs1, rs2 = (r[...] for r in rope_refs)

    def head(t, g):
        t = _rms(t, g)
        return t * rc + pltpu.roll(t, ROT_HALF, 1) * rs1 + pltpu.roll(t, LANES - ROT_HALF, 1) * rs2

    for hd in range(N_HEADS):
        sl = slice(hd * HEAD_DIM, (hd + 1) * HEAD_DIM)
        q_ref[:, sl] = head(proj[:, sl], gq_ref[...])
    for hd in range(N_KV_HEADS):
        sl = slice(hd * HEAD_DIM, (hd + 1) * HEAD_DIM)
        k_ref[:, sl] = head(proj[:, ATTN_WIDTH + hd * HEAD_DIM:ATTN_WIDTH + (hd + 1) * HEAD_DIM], gk_ref[...])
    o2 = ATTN_WIDTH + KV_WIDTH
    o3 = o2 + KV_WIDTH
    o4 = o3 + SSM_WIDTH
    v_ref[...] = proj[:, o2:o3]
    for j in range(SSM_LANE_BLOCKS):
        u_ref[j] = proj[:, o3 + j * LANES:o3 + (j + 1) * LANES]
    xp_ref[...] = proj[:, o4:]


def _inproj_kernel(xm_ref, xh_ref, g_ref, w_ref, gq_ref, gk_ref, rcm_ref, rs1m_ref, rs2m_ref,
                   rch_ref, rs1h_ref, rs2h_ref, *rest):
    cast = len(rest) == 12
    outs = rest[1:] if cast else rest
    main_outs, head_outs = outs[0:5], outs[5:10]
    i = pl.program_id(0)
    last = pl.num_programs(0) - 1

    @pl.when(i < last)
    def _():
        if cast:
            src, dst = rest[0], outs[10]
            tf = dst.shape[2]
            for j in range(dst.shape[0]):
                dst[j] = src[:, j * tf:(j + 1) * tf].astype(BF16)
        _inproj_rows(xm_ref[...], g_ref, w_ref, gq_ref, gk_ref, (rcm_ref, rs1m_ref, rs2m_ref), main_outs)

    @pl.when(i == last)
    def _():
        _inproj_rows(xh_ref[...], g_ref, w_ref, gq_ref, gk_ref, (rch_ref, rs1h_ref, rs2h_ref), head_outs)


def _inproj(xm, xh, vecs, w, l, rope_m, rope_h, tm, cast=None, cast_tf=None):
    R, RH = xm.shape[0], xh.shape[0]
    nm = R // tm
    tiles_per_rope = rope_m[0].shape[0] // tm
    tile = lambda i: jnp.minimum(i, nm - 1)
    row = lambda i: (tile(i), 0)
    rrow = lambda i: (tile(i) % tiles_per_rope, 0)
    once = pl.Buffered(1)
    full = lambda n, w_: pl.BlockSpec((n, w_), lambda i: (0, 0))

    def out_set(n, rows, im, uim):
        flat = lambda w_: (pl.BlockSpec((n, w_), im), jax.ShapeDtypeStruct((rows, w_), F32))
        u_out = (pl.BlockSpec((SSM_LANE_BLOCKS, n, LANES), uim),
                 jax.ShapeDtypeStruct((SSM_LANE_BLOCKS, rows, LANES), F32))
        return [flat(ATTN_WIDTH), flat(KV_WIDTH), flat(KV_WIDTH), u_out, flat(POOL_WIDTH)]

    outs = (out_set(tm, R, row, lambda i: (0, tile(i), 0))
            + out_set(RH, RH, lambda i: (0, 0), lambda i: (0, 0, 0)))
    in_specs = [pl.BlockSpec((tm, D_MODEL), row),
                pl.BlockSpec((RH, D_MODEL), lambda i: (0, 0), pipeline_mode=once),
                _vrow("g_mix", l),
                pl.BlockSpec((D_MODEL, IN_WIDTH), lambda i: (0, 0), pipeline_mode=once),
                _vrow("g_q", l), _vrow("g_k", l),
                pl.BlockSpec((tm, LANES), rrow), pl.BlockSpec((tm, LANES), rrow), pl.BlockSpec((tm, LANES), rrow),
                full(RH, LANES), full(RH, LANES), full(RH, LANES)]
    args = [xm, xh, vecs, w, vecs, vecs, *rope_m, *rope_h]
    if cast is not None:
        _, cr, cc = cast.shape
        crows = cr // nm
        in_specs.append(pl.BlockSpec((None, crows, cc), lambda i: (l, tile(i), 0)))
        args.append(cast)
        outs.append((pl.BlockSpec((cc // cast_tf, crows, cast_tf), lambda i: (0, tile(i), 0)),
                     jax.ShapeDtypeStruct((cc // cast_tf, cr, cast_tf), BF16)))
    res = pl.pallas_call(
        _inproj_kernel,
        grid=(nm + 1,),
        in_specs=in_specs,
        out_specs=[o[0] for o in outs],
        out_shape=[o[1] for o in outs],
        compiler_params=_params("arbitrary"),
        name="inproj",
    )(*args)
    return res[0:5], res[5:10], (res[10] if cast is not None else None)


def _attn_bias():
    rows = GQA_GROUP * BLOCK
    i, r, c = np.meshgrid(np.arange(3), np.arange(rows) % BLOCK, np.arange(2 * BLOCK), indexing="ij")
    diff = BLOCK + r - c
    krow = (i - 1) * BLOCK + c
    mask = (diff >= 0) & (diff <= WINDOW) & (krow >= SEQ0)
    return np.where(mask, 0.0, -np.inf).astype(np.float32)


def _attn_block(q_blk, kp_blk, kc_blk, vp_blk, vc_blk, bias, sink_ref, l, g):
    rows = GQA_GROUP * BLOCK
    rgrp = jax.lax.broadcasted_iota(jnp.int32, (rows, 1), 0) // BLOCK
    outs = []
    for kh in range(N_KV_HEADS):
        ksl = slice(kh * HEAD_DIM, (kh + 1) * HEAD_DIM)
        qh = jnp.concatenate(
            [q_blk[:, (kh * GQA_GROUP + h) * HEAD_DIM:(kh * GQA_GROUP + h + 1) * HEAD_DIM]
             for h in range(GQA_GROUP)], axis=0).astype(BF16)
        kk = jnp.concatenate([kp_blk[:, ksl], kc_blk[:, ksl]], axis=0).astype(BF16)
        vv = jnp.concatenate([vp_blk[:, ksl], vc_blk[:, ksl]], axis=0).astype(BF16)
        s = jax.lax.dot_general(qh, kk, (((1,), (1,)), ((), ())),
                                preferred_element_type=F32) * (HEAD_DIM ** -0.5 * LOG2E) + bias
        sk = jnp.zeros((rows, 1), F32)
        for h in range(GQA_GROUP):
            sk = jnp.where(rgrp == h, sink_ref[l * N_HEADS + kh * GQA_GROUP + h] * LOG2E, sk)
        m = jnp.maximum(jnp.max(s, axis=-1, keepdims=True), sk)
        p = jnp.exp2(s - m)
        denom = jnp.sum(p, axis=-1, keepdims=True) + jnp.exp2(sk - m)
        o = jnp.dot(p.astype(BF16), vv, preferred_element_type=F32) / denom
        outs.extend(o[h * BLOCK:(h + 1) * BLOCK] for h in range(GQA_GROUP))
    return _rms(jnp.concatenate(outs, axis=1), g)


def _ssm_params_kernel(ar_ref, ai_ref, ldt_ref, kk_ref, br_ref, bi_ref, tr_ref, ti_ref, bbr_ref, bbi_ref):
    ar, ai = ar_ref[...], ai_ref[...]
    dt = jnp.exp(ldt_ref[...])
    kk = kk_ref[...]
    mag = jnp.exp(dt * ar * kk)
    ang = dt * ai * kk
    tr = mag * jnp.cos(ang)
    ti = mag * jnp.sin(ang)
    tr_ref[...] = tr
    ti_ref[...] = ti
    abr, abi = tr[0:1], ti[0:1]
    den = ar * ar + ai * ai
    fr = ((abr - 1.0) * ar + abi * ai) / den
    fi = (abi * ar - (abr - 1.0) * ai) / den
    br, bi = br_ref[...], bi_ref[...]
    bbr_ref[...] = fr * br - fi * bi
    bbi_ref[...] = fr * bi + fi * br


def _ssm_params(A_re, A_im, log_dt, B_re, B_im, C_re, C_im):
    depth = A_re.shape[0]
    n = depth * SSM_GROUPS * SSM_STATE
    row = lambda t: t.astype(F32).reshape(1, n)
    ldt = jnp.broadcast_to(log_dt.astype(F32)[:, :, None], (depth, SSM_GROUPS, SSM_STATE)).reshape(1, n)
    kk = jnp.array(SSM_POW_ROWS + (0,) * (SUBLANES - len(SSM_POW_ROWS)), F32).reshape(SUBLANES, 1)
    chan_first = lambda t: t.astype(F32).reshape(n, SSM_GROUP_SIZE).T
    shapes = [(SUBLANES, n), (SUBLANES, n), (SSM_GROUP_SIZE, n), (SSM_GROUP_SIZE, n)]
    tr, ti, bbr, bbi = pl.pallas_call(
        _ssm_params_kernel,
        out_shape=[jax.ShapeDtypeStruct(s, F32) for s in shapes],
        name="ssm_params",
    )(row(A_re), row(A_im), ldt, kk, chan_first(B_re), chan_first(B_im))
    J, G8 = SSM_LANE_BLOCKS, LANES // SSM_GROUP_SIZE

    def lanes(t):
        return t.reshape(SUBLANES, depth, J, SSM_BLOCK_STATES).transpose(1, 0, 2, 3)

    pw = jnp.concatenate([lanes(tr), lanes(ti)], axis=-1).reshape(depth, SUBLANES, SSM_STATE_LANES)
    eye = jnp.eye(G8, dtype=F32)

    def bdiag(t):
        t = t.reshape(SSM_GROUP_SIZE, depth, J, G8, SSM_STATE).transpose(1, 2, 3, 0, 4)
        t = t[:, :, :, :, None, :] * eye[None, None, :, None, :, None]
        return t.reshape(depth, J, LANES, SSM_BLOCK_STATES)

    bblk = jnp.concatenate([bdiag(bbr), bdiag(bbi)], axis=-1).astype(BF16)

    def cdiag(t):
        t = t.astype(F32).reshape(depth, J, G8, SSM_GROUP_SIZE, SSM_STATE).transpose(0, 1, 2, 4, 3)
        t = t[:, :, :, :, None, :] * eye[None, None, :, None, :, None]
        return t.reshape(depth, J, SSM_BLOCK_STATES, LANES)

    cblk = jnp.concatenate([cdiag(C_re), -cdiag(C_im)], axis=2).astype(BF16)
    return pw, bblk, cblk


def _ssm_tail(y, u, d_ref, wg_ref, bg_ref, g_ref):
    y = y + d_ref[...] * u
    z = jax.nn.gelu(y)
    gate = jax.nn.sigmoid(jnp.dot(z.astype(BF16), wg_ref[...], preferred_element_type=F32) + bg_ref[...])
    return _rms(z * gate, g_ref[...])


def _ssm_sweep(x_scr, n, a_tabs, init, store):
    S = SSM_BLOCK_STATES
    fins = []
    for j0 in range(0, SSM_LANE_BLOCKS, 2):
        js = (j0, j0 + 1)

        def body(k, carry, js=js):
            r0 = pl.multiple_of(k * SUBLANES, SUBLANES)
            out = []
            for idx, j in enumerate(js):
                hr, hi = carry[2 * idx], carry[2 * idx + 1]
                base = j * 2 * S
                ar, ai = a_tabs[j]
                nhr = ar * hr - ai * hi + x_scr[pl.ds(r0, SUBLANES), base:base + S]
                nhi = ar * hi + ai * hr + x_scr[pl.ds(r0, SUBLANES), base + S:base + 2 * S]
                if store:
                    x_scr[pl.ds(r0, SUBLANES), base:base + S] = nhr
                    x_scr[pl.ds(r0, SUBLANES), base + S:base + 2 * S] = nhi
                out += [nhr, nhi]
            return tuple(out)

        c0 = tuple(t for j in js for t in init[j])
        res = jax.lax.fori_loop(0, n // SUBLANES, body, c0, unroll=True)
        fins += [(res[0], res[1]), (res[2], res[3])]
    return fins


def _ssm_rows(u, pow_row, pw_ref, bblk_ref, cblk_ref, x_scr, s_scr, carry_scr):
    n = u.shape[0]
    S = SSM_BLOCK_STATES
    ub = u.astype(BF16)
    for j in range(SSM_LANE_BLOCKS):
        x_scr[0:n, j * 2 * S:(j + 1) * 2 * S] = jnp.dot(ub[:, j * LANES:(j + 1) * LANES], bblk_ref[j],
                                                       preferred_element_type=F32)
    bc = lambda t: jnp.broadcast_to(t, (SUBLANES, S))
    re = lambda ref, r0, r1, j: ref[r0:r1, j * 2 * S:j * 2 * S + S]
    im = lambda ref, r0, r1, j: ref[r0:r1, j * 2 * S + S:(j + 1) * 2 * S]
    a_tabs = [(bc(re(pw_ref, 0, 1, j)), bc(im(pw_ref, 0, 1, j))) for j in range(SSM_LANE_BLOCKS)]
    zero = jnp.zeros((SUBLANES, S), F32)
    fins = _ssm_sweep(x_scr, n, a_tabs, [(zero, zero)] * SSM_LANE_BLOCKS, store=False)
    for j in range(SSM_LANE_BLOCKS):
        base = j * 2 * S
        cr, ci = re(pw_ref, pow_row, pow_row + 1, j), im(pw_ref, pow_row, pow_row + 1, j)
        sr, si = re(carry_scr, 0, 1, j), im(carry_scr, 0, 1, j)
        fr, fi = fins[j]
        for c in range(SUBLANES):
            s_scr[c:c + 1, base:base + S] = sr
            s_scr[c:c + 1, base + S:base + 2 * S] = si
            sr, si = cr * sr - ci * si + fr[c:c + 1], cr * si + ci * sr + fi[c:c + 1]
        carry_scr[:, base:base + S] = bc(sr)
        carry_scr[:, base + S:base + 2 * S] = bc(si)
    init = [(re(s_scr, 0, SUBLANES, j), im(s_scr, 0, SUBLANES, j)) for j in range(SSM_LANE_BLOCKS)]
    _ssm_sweep(x_scr, n, a_tabs, init, store=True)
    ys = [jnp.dot(x_scr[0:n, j * 2 * S:(j + 1) * 2 * S].astype(BF16), cblk_ref[j], preferred_element_type=F32)
          for j in range(SSM_LANE_BLOCKS)]
    return jnp.concatenate(ys, axis=1)


def _ssm_tile(u_ref, n, pow_row, refs, o_ref, scr):
    pw_ref, bblk_ref, cblk_ref, d_ref, wg_ref, bg_ref, g_ref = refs
    up_scr, x_scr, s_scr, carry_scr, o_scr = scr
    q = n // SUBLANES
    for j in range(SSM_LANE_BLOCKS):
        for k in range(q):
            up_scr[k * SUBLANES:(k + 1) * SUBLANES, j * LANES:(j + 1) * LANES] = \
                u_ref[j, pl.ds(k, SUBLANES, stride=q), :]
    u = up_scr[0:n, :]
    y = _ssm_rows(u, pow_row, pw_ref, bblk_ref, cblk_ref, x_scr, s_scr, carry_scr)
    out = _ssm_tail(y, u, d_ref, wg_ref, bg_ref, g_ref)
    for j in range(SSM_LANE_BLOCKS):
        for k in range(q):
            o_scr[j, pl.ds(k, SUBLANES, stride=q), :] = out[k * SUBLANES:(k + 1) * SUBLANES,
                                                            j * LANES:(j + 1) * LANES]
    o_ref[...] = jnp.concatenate([o_scr[j, 0:n, :] for j in range(SSM_LANE_BLOCKS)], axis=1).astype(o_ref.dtype)


def _mixers_kernel(sink_ref,
                   qm_ref, qh_ref, kpm_ref, km_ref, kh_ref, vpm_ref, vm_ref, vh_ref, ga_ref, bias_ref,
                   um_ref, uh_ref, pw_ref, bblk_ref, cblk_ref, d_ref, wg_ref, bg_ref, gs_ref,
                   xm_ref, halo_ref, xh_ref, wp_ref, sc_ref, gp_ref, *rest, l, n_cast, n_dec):
    cast_in = rest[0:n_cast]
    am_ref, ah_ref, som_ref, soh_ref, st_ref, pom_ref, poh_ref = rest[n_cast:n_cast + 7]
    cast_out = rest[n_cast + 7:2 * n_cast + 7]
    scr = rest[2 * n_cast + 7:]
    carry_scr = scr[3]
    b = pl.program_id(0)
    t = pl.program_id(1)
    tm = xm_ref.shape[0]
    ssm_refs = (pw_ref, bblk_ref, cblk_ref, d_ref, wg_ref, bg_ref, gs_ref)
    blk = functools.partial(_attn_block, sink_ref=sink_ref, l=l, g=ga_ref[...])
    meta_rows = pl.ds(pl.multiple_of(n_dec + b * N_META, N_META), N_META)

    @pl.when((b == 0) & (t == 0))
    def _():
        for ref in (ah_ref, soh_ref, poh_ref):
            ref[0:n_dec, :] = jnp.zeros((n_dec, ref.shape[1]), ref.dtype)

    block = lambda ref: jnp.concatenate([jnp.zeros((SEQ0, ref.shape[1]), F32), ref[...]], axis=0)

    @pl.when(t == 0)
    def _():
        a = blk(block(qh_ref), kpm_ref[...], block(kh_ref), vpm_ref[...], block(vh_ref), bias_ref[0])
        ah_ref[meta_rows, :] = a[SEQ0:].astype(ah_ref.dtype)
        carry_scr[...] = jnp.zeros_like(carry_scr)
        _ssm_tile(uh_ref, N_META, POW_ROW_HEAD, ssm_refs, soh_ref.at[meta_rows, :], scr)
        prev = jnp.zeros((POOL_HALO, POOL_WIDTH), F32)
        poh_ref[meta_rows, :] = _pool_rows(xh_ref[...], prev, 0, wp_ref, sc_ref, gp_ref).astype(poh_ref.dtype)

    @pl.when(t > 0)
    def _():
        for src, dst in zip(cast_in, cast_out):
            dst[...] = src[...].astype(BF16)
        kp = jnp.where(t == 1, block(kh_ref), kpm_ref[...])
        vp = jnp.where(t == 1, block(vh_ref), vpm_ref[...])
        for n in range(tm // BLOCK):
            rows = slice(n * BLOCK, (n + 1) * BLOCK)
            kc, vc = km_ref[rows, :], vm_ref[rows, :]
            bias = bias_ref[jnp.minimum(t, 2)] if n == 0 else bias_ref[2]
            am_ref[rows, :] = blk(qm_ref[rows, :], kp, kc, vp, vc, bias).astype(am_ref.dtype)
            kp, vp = kc, vc
        _ssm_tile(um_ref, um_ref.shape[1], POW_ROW_MAIN, ssm_refs, som_ref, scr)
        st_ref[...] = carry_scr[...]
        prev = jnp.where(t == 1, xh_ref[...], halo_ref[...])
        pom_ref[...] = _pool_rows(xm_ref[...], prev, N_META + (t - 1) * tm, wp_ref, sc_ref, gp_ref).astype(pom_ref.dtype)


def _mixers(sinks, qkv_m, qkv_h, um, uh, xm, xh, vecs, bias, pw, bblk, cblk, wg, wp, l, B, tm, casts):
    assert (1, N_META // SUBLANES, tm // SUBLANES) == SSM_POW_ROWS and N_META == POOL_HALO
    J = SSM_LANE_BLOCKS
    rm, rh = um.shape[1], uh.shape[1]
    nt = rm // (B * tm)
    r = tm // POOL_HALO
    bpt = tm // BLOCK
    tile = lambda b, t: b * nt + jnp.maximum(t - 1, 0)
    main = lambda b, t, _: (tile(b, t), 0)
    n_dec = uh.shape[1] - B * N_META
    head = lambda b, t, _: (n_dec // N_META + b, 0)
    halo = lambda b, t, _: (jnp.maximum((b * nt + t - 1) * r - 1, 0), 0)
    prevb = lambda b, t, _: (jnp.maximum(tile(b, t) * bpt - 1, 0), 0)
    cast_specs = []
    for w, wl in casts:
        _, cr, cc = w.shape
        rows = cr // (B * nt)
        cast_specs.append((pl.BlockSpec((None, rows, cc), lambda b, t, _, wl=wl: (wl, tile(b, t), 0)),
                           pl.BlockSpec((rows, cc), main), jax.ShapeDtypeStruct((cr, cc), BF16)))
    bf = lambda n, w: jax.ShapeDtypeStruct((n, w), BF16)
    tile_spec = lambda w: pl.BlockSpec((tm, w), main)
    head_spec = lambda w: pl.BlockSpec((N_META, w), head)
    in_specs = [tile_spec(ATTN_WIDTH), head_spec(ATTN_WIDTH),
                pl.BlockSpec((BLOCK, KV_WIDTH), prevb), tile_spec(KV_WIDTH), head_spec(KV_WIDTH),
                pl.BlockSpec((BLOCK, KV_WIDTH), prevb), tile_spec(KV_WIDTH), head_spec(KV_WIDTH),
                _vrow("g_out_attn", l), _full(bias.shape),
                pl.BlockSpec((J, tm, LANES), lambda b, t, _: (0, tile(b, t), 0)),
                pl.BlockSpec((J, N_META, LANES), lambda b, t, _: (0, n_dec // N_META + b, 0)),
                _layer(pw.shape[1:], l), _layer(bblk.shape[1:], l), _layer(cblk.shape[1:], l),
                _vrow("D_skip", l), _layer((SSM_WIDTH, SSM_WIDTH), l), _vrow("b_glu", l), _vrow("g_out_ssm", l),
                tile_spec(POOL_WIDTH), pl.BlockSpec((POOL_HALO, POOL_WIDTH), halo), head_spec(POOL_WIDTH),
                _layer(wp.shape[1:], l), _vrow("pool_scale", l), _vrow("g_out_pool", l)]
    in_specs += [c[0] for c in cast_specs]
    head_out = lambda w: pl.BlockSpec((rh, w), lambda b, t, _: (0, 0))
    out_specs = [tile_spec(ATTN_WIDTH), head_out(ATTN_WIDTH), tile_spec(SSM_WIDTH), head_out(SSM_WIDTH),
                 pl.BlockSpec((None, SUBLANES, SSM_STATE_LANES), lambda b, t, _: (b, 0, 0)),
                 tile_spec(POOL_WIDTH), head_out(POOL_WIDTH)] + [c[1] for c in cast_specs]
    out_shape = [bf(rm, ATTN_WIDTH), bf(rh, ATTN_WIDTH), bf(rm, SSM_WIDTH), bf(rh, SSM_WIDTH),
                 jax.ShapeDtypeStruct((B, SUBLANES, SSM_STATE_LANES), F32),
                 bf(rm, POOL_WIDTH), bf(rh, POOL_WIDTH)] + [c[2] for c in cast_specs]
    (qm, km, vm), (qh, kh, vh) = qkv_m, qkv_h
    res = pl.pallas_call(
        functools.partial(_mixers_kernel, l=l, n_cast=len(casts), n_dec=n_dec),
        grid_spec=pltpu.PrefetchScalarGridSpec(
            num_scalar_prefetch=1, grid=(B, nt + 1), in_specs=in_specs, out_specs=out_specs,
            scratch_shapes=[pltpu.VMEM((tm, SSM_WIDTH), F32),
                            pltpu.VMEM((tm, SSM_STATE_LANES), F32),
                            pltpu.VMEM((SUBLANES, SSM_STATE_LANES), F32),
                            pltpu.VMEM((SUBLANES, SSM_STATE_LANES), F32),
                            pltpu.VMEM((J, tm, LANES), F32)]),
        out_shape=out_shape,
        compiler_params=_params("arbitrary", "arbitrary"),
        name="mixers",
    )(sinks, qm, qh, km, km, kh, vm, vm, vh, vecs, bias, um, uh, pw, bblk, cblk, vecs, wg, vecs, vecs,
      xm, xm, xh, wp, vecs, vecs, *[w for w, _ in casts])
    return res[0:7], res[7:]


def _state_from_lanes(s):
    s = s.reshape(s.shape[0], SSM_LANE_BLOCKS, 2, SSM_BLOCK_STATES)
    return (s[:, :, 0].reshape(-1, SSM_GROUPS, SSM_STATE), s[:, :, 1].reshape(-1, SSM_GROUPS, SSM_STATE))


def _pool_tail(d_groups, w_ref, sc_ref, g_ref):
    y = jnp.concatenate(
        [jnp.dot(d.astype(BF16), w_ref[gi], preferred_element_type=F32) for gi, d in enumerate(d_groups)], axis=1)
    return _rms(y * sc_ref[...], g_ref[...])


def _pool_rows(x, prev, pos0, w_ref, sc_ref, g_ref):
    n = x.shape[0]
    xe = jnp.concatenate([prev, x], axis=0)
    pos = pos0 + jax.lax.broadcasted_iota(jnp.int32, (n, 1), 0)
    ds = []
    for gi, w in enumerate(POOL_WINDOWS):
        gsl = slice(gi * POOL_GROUP, (gi + 1) * POOL_GROUP)
        s = xe[:, gsl]
        k = 1
        while k < w:
            s = s + pltpu.roll(s, k, 0)
            k *= 2
        cnt = jnp.clip(pos + 1, 1, w).astype(F32)
        ds.append(s[POOL_HALO:] / cnt - x[:, gsl])
    return _pool_tail(ds, w_ref, sc_ref, g_ref)


def _attn_sample_kernel(q_ref, kn_ref, vn_ref, kc_ref, vc_ref, sink_ref, g_ref, *rest, first):
    a_ref, nk_ref, nv_ref, acc_scr = rest[-4:]
    step = pl.program_id(1 if first else 0)
    nsteps = pl.num_programs(1 if first else 0)
    slot = WINDOW - 1 if first else 0
    scale = HEAD_DIM ** -0.5
    if first:
        for bb in range(DEC_STEP):
            nk_ref[bb, 0:WINDOW - 1] = kc_ref[bb, 1:WINDOW]
            nv_ref[bb, 0:WINDOW - 1] = vc_ref[bb, 1:WINDOW]
            nk_ref[bb, WINDOW - 1] = jnp.zeros((N_KV_HEADS, HEAD_DIM), F32)
            nv_ref[bb, WINDOW - 1] = jnp.zeros((N_KV_HEADS, HEAD_DIM), F32)

    def attend():
        _attn_sample_rows(q_ref, kn_ref, vn_ref, kc_ref, vc_ref, sink_ref, g_ref, nk_ref, nv_ref, acc_scr,
                          step, slot, scale)

        @pl.when(step == nsteps - 1)
        def _():
            a_ref[...] = acc_scr[...].astype(a_ref.dtype)

    if first:
        pl.when(pl.program_id(0) == 0)(attend)
    else:
        attend()


def _attn_sample_rows(q_ref, kn_ref, vn_ref, kc_ref, vc_ref, sink_ref, g_ref, nk_ref, nv_ref, acc_scr,
                      step, slot, scale):
    for bb in range(DEC_STEP):
        outs = []
        for kh in range(N_KV_HEADS):
            ksl = slice(kh * HEAD_DIM, (kh + 1) * HEAD_DIM)
            qh = jnp.concatenate(
                [q_ref[bb:bb + 1, (kh * GQA_GROUP + g) * HEAD_DIM:(kh * GQA_GROUP + g + 1) * HEAD_DIM]
                 for g in range(GQA_GROUP)], axis=0)
            kn = kn_ref[bb:bb + 1, ksl]
            vn = vn_ref[bb:bb + 1, ksl]
            nk_ref[bb, slot, kh:kh + 1, :] = kn
            nv_ref[bb, slot, kh:kh + 1, :] = vn
            kc = kc_ref[bb, :, kh, :]
            vc = vc_ref[bb, :, kh, :]
            sc = jax.lax.dot_general(qh.astype(BF16), kc.astype(BF16), (((1,), (1,)), ((), ())),
                                     preferred_element_type=F32) * scale
            sn = jnp.sum(qh * kn, axis=-1, keepdims=True) * scale
            sk = sink_ref[kh]
            m = jnp.maximum(jnp.maximum(jnp.max(sc, axis=-1, keepdims=True), sn), sk)
            pc = jnp.exp(sc - m)
            pn = jnp.exp(sn - m)
            denom = jnp.sum(pc, axis=-1, keepdims=True) + pn + jnp.exp(sk - m)
            o = jnp.dot(pc.astype(BF16), vc.astype(BF16), preferred_element_type=F32)
            o = (o + pn * vn) / denom
            outs.extend(o[g:g + 1] for g in range(GQA_GROUP))
        a = jnp.concatenate(outs, axis=1)
        acc_scr[pl.ds(step * DEC_STEP + bb, 1), :] = _rms(a, g_ref[...])


def _attn_sample(qh, kh, vh, cache_k, cache_v, l, sinks, g, ah, nk, nv, N):
    depth = cache_k.shape[0]
    first = nk is None
    anyspec = pl.BlockSpec(memory_space=pl.ANY)
    cshape = jax.ShapeDtypeStruct(cache_k.shape, F32)
    slab = (None, DEC_STEP, WINDOW, N_KV_HEADS, HEAD_DIM)
    if first:
        assert l == 0
        grid = (depth, N // DEC_STEP)
        rows = lambda w: pl.BlockSpec((DEC_STEP, w), lambda ll, s: (s, 0))
        cache = pl.BlockSpec(slab, lambda ll, s: (ll, s, 0, 0, 0))
        new = cache
        a_spec = pl.BlockSpec((N, ATTN_WIDTH), lambda ll, s: (0, 0))
        carried = []
    else:
        grid = (N // DEC_STEP,)
        rows = lambda w: pl.BlockSpec((DEC_STEP, w), lambda s: (s, 0))
        cache = pl.BlockSpec(slab, lambda s: (l, s, 0, 0, 0))
        new = pl.BlockSpec((None, DEC_STEP, 1, N_KV_HEADS, HEAD_DIM), lambda s: (l, s, WINDOW - 1, 0, 0))
        a_spec = pl.BlockSpec((N, ATTN_WIDTH), lambda s: (0, 0))
        carried = [nk, nv]
    return pl.pallas_call(
        functools.partial(_attn_sample_kernel, first=first),
        grid=grid,
        in_specs=[rows(ATTN_WIDTH), rows(KV_WIDTH), rows(KV_WIDTH), cache, cache,
                  _layer((N_KV_HEADS, GQA_GROUP, 1), l), _vrow("g_out_attn", l)] + [anyspec] * (1 + len(carried)),
        out_specs=[a_spec, new, new],
        out_shape=[jax.ShapeDtypeStruct(ah.shape, ah.dtype), cshape, cshape],
        scratch_shapes=[pltpu.VMEM((N, ATTN_WIDTH), F32)],
        input_output_aliases={7 + i: i for i in range(1 + len(carried))},
        compiler_params=_params(*(["arbitrary"] * len(grid))),
        name="attn_sample",
    )(qh, kh, vh, cache_k, cache_v, sinks, g, ah, *carried)


def _mix_sample_kernel(u_ref, h0_ref, pw_ref, bblk_ref, cblk_ref, d_ref, wg_ref, bg_ref, gs_ref,
                       xp_ref, pb_ref, wp_ref, sc_ref, gp_ref, s_in_ref, p_in_ref, s_ref, st_ref, p_ref):
    del s_in_ref, p_in_ref
    S = SSM_BLOCK_STATES
    u = jnp.concatenate([u_ref[j] for j in range(SSM_LANE_BLOCKS)], axis=1)
    ub = u.astype(BF16)
    ys = []
    for j in range(SSM_LANE_BLOCKS):
        x = jnp.dot(ub[:, j * LANES:(j + 1) * LANES], bblk_ref[j], preferred_element_type=F32)
        base = j * 2 * S
        ar = pw_ref[0:1, base:base + S]
        ai = pw_ref[0:1, base + S:base + 2 * S]
        h0r = h0_ref[:, base:base + S]
        h0i = h0_ref[:, base + S:base + 2 * S]
        hr = x[:, 0:S] + ar * h0r - ai * h0i
        hi = x[:, S:] + ar * h0i + ai * h0r
        st_ref[:, base:base + S] = hr
        st_ref[:, base + S:base + 2 * S] = hi
        h = jnp.concatenate([hr, hi], axis=1).astype(BF16)
        ys.append(jnp.dot(h, cblk_ref[j], preferred_element_type=F32))
    s_ref[...] = _ssm_tail(jnp.concatenate(ys, axis=1), u, d_ref, wg_ref, bg_ref, gs_ref).astype(s_ref.dtype)

    xp = xp_ref[...]
    ds = []
    for gi, w in enumerate(POOL_WINDOWS):
        gsl = slice(gi * POOL_GROUP, (gi + 1) * POOL_GROUP)
        s = xp[:, gsl]
        for back in range(1, w):
            s = s + pb_ref[POOL_BUF - back][:, gsl]
        ds.append(s / float(w) - xp[:, gsl])
    p_ref[...] = _pool_tail(ds, wp_ref, sc_ref, gp_ref).astype(p_ref.dtype)


def _mix_sample(uh, xph, h0, pbuf, vecs, pw, bblk, cblk, wg, wp, l, sh, ph):
    N = h0.shape[1]
    rows = lambda w: pl.BlockSpec((N, w), lambda i: (0, 0))
    anyspec = pl.BlockSpec(memory_space=pl.ANY)
    return pl.pallas_call(
        _mix_sample_kernel,
        grid=(1,),
        in_specs=[pl.BlockSpec((SSM_LANE_BLOCKS, N, LANES), lambda i: (0, 0, 0)),
                  _layer(h0.shape[1:], l), _layer(pw.shape[1:], l), _layer(bblk.shape[1:], l),
                  _layer(cblk.shape[1:], l),
                  _vrow("D_skip", l), _layer((SSM_WIDTH, SSM_WIDTH), l), _vrow("b_glu", l),
                  _vrow("g_out_ssm", l), rows(POOL_WIDTH), _layer(pbuf.shape[1:], l),
                  _layer(wp.shape[1:], l),
                  _vrow("pool_scale", l), _vrow("g_out_pool", l), anyspec, anyspec],
        out_specs=[rows(SSM_WIDTH), _full((N, SSM_STATE_LANES)), rows(POOL_WIDTH)],
        out_shape=[jax.ShapeDtypeStruct(sh.shape, sh.dtype), jax.ShapeDtypeStruct((N, SSM_STATE_LANES), F32),
                   jax.ShapeDtypeStruct(ph.shape, ph.dtype)],
        input_output_aliases={14: 0, 15: 2},
        compiler_params=_params("arbitrary"),
        name="mix_sample",
    )(uh, h0, pw, bblk, cblk, vecs, wg, vecs, vecs, xph, pbuf, wp, vecs, vecs, sh, ph)


def _outproj_rows(x_ref, a_ref, s_ref, p_ref, w_ref, o_ref):
    o1 = ATTN_WIDTH
    o2 = o1 + SSM_WIDTH
    acc = x_ref[...]
    acc = acc + jnp.dot(a_ref[...], w_ref[0:o1, :], preferred_element_type=F32)
    acc = acc + jnp.dot(s_ref[...], w_ref[o1:o2, :], preferred_element_type=F32)
    acc = acc + jnp.dot(p_ref[...], w_ref[o2:, :], preferred_element_type=F32)
    o_ref[...] = acc


def _outproj_kernel(xm_ref, am_ref, sm_ref, pm_ref, xh_ref, ah_ref, sh_ref, ph_ref, w_ref, om_ref, oh_ref):
    i = pl.program_id(0)
    last = pl.num_programs(0) - 1

    @pl.when(i < last)
    def _():
        _outproj_rows(xm_ref, am_ref, sm_ref, pm_ref, w_ref, om_ref)

    @pl.when(i == last)
    def _():
        _outproj_rows(xh_ref, ah_ref, sh_ref, ph_ref, w_ref, oh_ref)


def _outproj(xm, am, sm, pm, xh, ah, sh, ph, w, tm):
    R, RH = xm.shape[0], xh.shape[0]
    nm = R // tm
    row = lambda i: (jnp.minimum(i, nm - 1), 0)
    once = pl.Buffered(1)
    head = lambda w_: pl.BlockSpec((RH, w_), lambda i: (0, 0), pipeline_mode=once)
    return pl.pallas_call(
        _outproj_kernel,
        grid=(nm + 1,),
        in_specs=[pl.BlockSpec((tm, D_MODEL), row), pl.BlockSpec((tm, ATTN_WIDTH), row),
                  pl.BlockSpec((tm, SSM_WIDTH), row), pl.BlockSpec((tm, POOL_WIDTH), row),
                  head(D_MODEL), head(ATTN_WIDTH), head(SSM_WIDTH), head(POOL_WIDTH),
                  pl.BlockSpec((D_MODEL, D_MODEL), lambda i: (0, 0), pipeline_mode=once)],
        out_specs=[pl.BlockSpec((tm, D_MODEL), row), pl.BlockSpec((RH, D_MODEL), lambda i: (0, 0))],
        out_shape=[jax.ShapeDtypeStruct((R, D_MODEL), F32), jax.ShapeDtypeStruct((RH, D_MODEL), F32)],
        compiler_params=_params("arbitrary"),
        name="outproj",
    )(xm, am, sm, pm, xh, ah, sh, ph, w)


def _ffn_kernel(xm_ref, xh_ref, g_ref, w1_hbm, w2_hbm, om_ref, oh_ref, hm_scr, hh_scr, w1_buf, w2_buf, sem, *, tf):
    i = pl.program_id(0)
    last_tile = pl.num_programs(0) - 1
    nf = D_FF // tf

    def copies(f, slot):
        row = pl.multiple_of(f * tf, tf)
        return (pltpu.make_async_copy(w1_hbm.at[f], w1_buf.at[slot], sem.at[0, slot]),
                pltpu.make_async_copy(w2_hbm.at[pl.ds(row, tf), :], w2_buf.at[slot], sem.at[1, slot]))

    def start(f, slot):
        for c in copies(f, slot):
            c.start()

    def wait(f, slot):
        for c in copies(f, slot):
            c.wait()

    def mlp(h, slot):
        h1 = jnp.dot(h, w1_buf[slot], preferred_element_type=F32)
        return jnp.dot(jnp.square(jnp.maximum(h1, 0.0)).astype(BF16), w2_buf[slot], preferred_element_type=F32)

    def first(x_ref, h_scr, o_ref):
        x = x_ref[...]
        h = _rms(x, g_ref[...]).astype(BF16)
        h_scr[...] = h
        o_ref[...] = x + mlp(h, 0)

    @pl.when(i == 0)
    def _():
        start(0, 0)

    wait(0, 0)
    start(1, 1)
    first(xm_ref, hm_scr, om_ref)

    @pl.when(i == 0)
    def _():
        first(xh_ref, hh_scr, oh_ref)

    def body(f, carry):
        slot = f % 2
        wait(f, slot)

        @pl.when(f + 1 < nf)
        def _():
            start(f + 1, 1 - slot)

        @pl.when((f + 1 == nf) & (i < last_tile))
        def _():
            start(0, 0)

        om_ref[...] += mlp(hm_scr[...], slot)

        @pl.when(i == 0)
        def _():
            oh_ref[...] += mlp(hh_scr[...], slot)

        return carry

    jax.lax.fori_loop(1, nf, body, 0)


def _ffn(xm, xh, g, l, w1, w2, tm, tf):
    R, RH = xm.shape[0], xh.shape[0]
    assert w1.shape == (D_FF // tf, D_MODEL, tf) and (D_FF // tf) % 2 == 0
    anyspec = pl.BlockSpec(memory_space=pl.ANY)
    return pl.pallas_call(
        functools.partial(_ffn_kernel, tf=tf),
        grid=(R // tm,),
        in_specs=[pl.BlockSpec((tm, D_MODEL), lambda i: (i, 0)),
                  pl.BlockSpec((RH, D_MODEL), lambda i: (0, 0), pipeline_mode=pl.Buffered(1)),
                  _vrow("g_ffn", l), anyspec, anyspec],
        out_specs=[pl.BlockSpec((tm, D_MODEL), lambda i: (i, 0)), pl.BlockSpec((RH, D_MODEL), lambda i: (0, 0))],
        out_shape=[jax.ShapeDtypeStruct((R, D_MODEL), F32), jax.ShapeDtypeStruct((RH, D_MODEL), F32)],
        scratch_shapes=[pltpu.VMEM((tm, D_MODEL), BF16), pltpu.VMEM((RH, D_MODEL), BF16),
                        pltpu.VMEM((2, D_MODEL, tf), BF16), pltpu.VMEM((2, tf, D_MODEL), BF16),
                        pltpu.SemaphoreType.DMA((2, 2))],
        compiler_params=_params("arbitrary"),
        name="ffn",
    )(xm, xh, g, w1, w2)


def _rope_tables(pos):
    half = ROT_HALF
    inv = ROPE_THETA ** (-np.arange(0, ROT_DIM, 2, dtype=np.float64) / ROT_DIM)
    ang = np.asarray(pos, np.float64)[:, None] * inv
    cos, sin = np.cos(ang), np.sin(ang)
    n = ang.shape[0]
    z = np.zeros((n, HEAD_DIM - ROT_DIM))
    zh = np.zeros((n, half))
    rc = np.concatenate([cos, cos, z + 1.0], axis=1)
    rs1 = np.concatenate([zh, sin, z], axis=1)
    rs2 = np.concatenate([-sin, zh, z], axis=1)
    return tuple(t.astype(np.float32) for t in (rc, rs1, rs2))


def kernel(x_prompt, x_sample, cache_k, cache_v, state_ssm_re, state_ssm_im, state_pool, meta_tokens, g_mix, w_in, g_q, g_k, sinks, A_re, A_im, log_dt, B_re, B_im, C_re, C_im, D_skip, w_glu, b_glu, w_pool, pool_scale, g_out_attn, g_out_ssm, g_out_pool, w_out, g_ffn, w_ff1, w_ff2):
    B, T, _ = x_prompt.shape
    N = x_sample.shape[0]
    depth = w_in.shape[0]
    assert N % N_META == 0 and N % DEC_STEP == 0 and T % TM_FFN == 0
    xh = jnp.concatenate([x_sample.reshape(N, D_MODEL)] + [meta_tokens.astype(F32)] * B, axis=0)
    xm = x_prompt.reshape(B * T, D_MODEL)

    rope_m = _rope_tables(N_META + np.arange(T))
    rope_h = _rope_tables(np.concatenate([np.full(N, PAST_LEN)] + [np.arange(N_META)] * B))

    wi = w_in[0].astype(BF16)
    wg_all, wp_all = w_glu.astype(BF16), w_pool.astype(BF16)
    ck = cache_k.astype(F32)
    cv = cache_v.astype(F32)

    named = dict(g_mix=g_mix, g_ffn=g_ffn, g_out_attn=g_out_attn, D_skip=D_skip, b_glu=b_glu,
                 g_out_ssm=g_out_ssm, pool_scale=pool_scale, g_out_pool=g_out_pool, g_q=g_q, g_k=g_k)
    vecs = jnp.concatenate([named[name].astype(F32) for name, _ in VEC_LAYOUT], axis=1)[:, None, :]
    pw, bblk, cblk = _ssm_params(A_re, A_im, log_dt, B_re, B_im, C_re, C_im)
    sinks_flat = sinks.astype(F32).reshape(depth * N_HEADS)
    sinks_col = sinks.astype(F32).reshape(depth, N_KV_HEADS, GQA_GROUP, 1)
    bias = _attn_bias()
    h0 = jnp.concatenate([state_ssm_re.astype(F32).reshape(depth, N, SSM_LANE_BLOCKS, SSM_BLOCK_STATES),
                          state_ssm_im.astype(F32).reshape(depth, N, SSM_LANE_BLOCKS, SSM_BLOCK_STATES)],
                         axis=-1).reshape(depth, N, SSM_STATE_LANES)
    pbuf = state_pool.astype(F32).transpose(0, 2, 1, 3)

    nk = nv = None
    ks, vs, pls, sts, st_ss, phs = ([] for _ in range(6))
    for l in range(depth):
        (qm, km, vm, um, pm), (qh, kh, vh, uh, ph), w1 = _inproj(xm, xh, vecs, wi, l, rope_m, rope_h, TM_PROJ,
                                                                 w_ff1, TF_FFN)

        (am, ah, sm, sh, st, plm, plh), wcast = _mixers(
            sinks_flat, (qm, km, vm), (qh, kh, vh), um, uh, pm, ph, vecs, bias, pw, bblk, cblk, wg_all, wp_all,
            l, B, TM_SEQ, [(w_ff2, l), (w_out, l)] + ([(w_in, l + 1)] if l + 1 < depth else []))
        w2, wo = wcast[0], wcast[1]

        ah, nk, nv = _attn_sample(qh, kh, vh, ck, cv, l, sinks_col, vecs, ah, nk, nv, N)
        sh, st_s, plh = _mix_sample(uh, ph, h0, pbuf, vecs, pw, bblk, cblk, wg_all, wp_all, l, sh, plh)

        xm, xh = _outproj(xm, am, sm, plm, xh, ah, sh, plh, wo, TM_PROJ)
        xm, xh = _ffn(xm, xh, vecs, l, w1, w2, TM_FFN, TF_FFN)
        if l + 1 < depth:
            wi = wcast[2]

        ks.append(km.reshape(B, T, KV_WIDTH)[:, T - WINDOW:])
        vs.append(vm.reshape(B, T, KV_WIDTH)[:, T - WINDOW:])
        pls.append(pm.reshape(B, T, POOL_WIDTH)[:, T - POOL_BUF:])
        sts.append(st[:, 0])
        st_ss.append(st_s)
        phs.append(ph[:N])

    y_prompt = xm.reshape(B, T, D_MODEL)
    y_sample = xh[:N].reshape(N, 1, D_MODEL)
    heads = lambda t: jnp.stack(t).reshape(depth, -1, WINDOW, N_KV_HEADS, HEAD_DIM)
    p_re, p_im = _state_from_lanes(jnp.stack(sts).reshape(depth * B, SSM_STATE_LANES))
    s_re, s_im = _state_from_lanes(jnp.stack(st_ss).reshape(depth * N, SSM_STATE_LANES))
    st4 = lambda t, n: t.reshape(depth, n, SSM_GROUPS, SSM_STATE)
    s_pool = jnp.concatenate([state_pool.astype(F32)[:, :, 1:], jnp.stack(phs)[:, :, None]], axis=2)
    return (y_prompt, y_sample, heads(ks), heads(vs), st4(p_re, B), st4(p_im, B), jnp.stack(pls),
            nk, nv, st4(s_re, N), st4(s_im, N), s_pool)
```

```python
import functools
import math

import jax
import jax.numpy as jnp
import numpy as np
from jax.experimental import pallas as pl
from jax.experimental.pallas import tpu as pltpu

D_MODEL = 2048
N_META = 16
HEAD_DIM = 128
N_HEADS = 8
N_KV_HEADS = 2
GQA_GROUP = 4
ATTN_WIDTH = 1024
KV_WIDTH = 256
WINDOW = 128
BLOCK = 128
ROT_DIM = 32
ROT_HALF = ROT_DIM // 2
ROPE_THETA = 500000.0
SSM_WIDTH = 512
SSM_GROUP_SIZE = 16
SSM_GROUPS = 32
SSM_STATE = 64
POOL_WIDTH = 512
POOL_WINDOWS = (2, 4, 8, 16)
POOL_GROUP = 128
POOL_BUF = 15
POOL_HALO = 16
IN_WIDTH = 2560
D_FF = 8192
EPS = 1e-6
PAST_LEN = 16384
LOG2E = math.log2(math.e)

SEQ0 = BLOCK - N_META
LANES = 128
SUBLANES = 8
SSM_LANE_BLOCKS = SSM_WIDTH // LANES
SSM_BLOCK_STATES = (LANES // SSM_GROUP_SIZE) * SSM_STATE
SSM_STATE_LANES = SSM_LANE_BLOCKS * 2 * SSM_BLOCK_STATES
VMEM_LIMIT = 60 * 1024 * 1024

TM_PROJ = 512
TM_FFN = 512
TF_FFN = 1024
TM_SEQ = 512
DEC_STEP = 8
SSM_POW_ROWS = (1, N_META // SUBLANES, TM_SEQ // SUBLANES)
POW_ROW_HEAD, POW_ROW_MAIN = 1, 2

BF16 = jnp.bfloat16
F32 = jnp.float32


def _params(*semantics):
    return pltpu.CompilerParams(dimension_semantics=semantics, vmem_limit_bytes=VMEM_LIMIT)


def _rms(x, g):
    return x * jax.lax.rsqrt(jnp.mean(x * x, axis=-1, keepdims=True) + EPS) * g


def _full(shape):
    n = len(shape)
    return pl.BlockSpec(shape, lambda *_: (0,) * n)


def _layer(shape, l):
    n = len(shape)
    return pl.BlockSpec((None, *shape), lambda *_: (l,) + (0,) * n)


VEC_LAYOUT = (("g_mix", D_MODEL), ("g_ffn", D_MODEL), ("g_out_attn", ATTN_WIDTH), ("D_skip", SSM_WIDTH),
              ("b_glu", SSM_WIDTH), ("g_out_ssm", SSM_WIDTH), ("pool_scale", POOL_WIDTH),
              ("g_out_pool", POOL_WIDTH), ("g_q", HEAD_DIM), ("g_k", HEAD_DIM))
VEC_WIDTH = dict(VEC_LAYOUT)
VEC_OFFSET = {name: sum(w for _, w in VEC_LAYOUT[:i]) for i, (name, _) in enumerate(VEC_LAYOUT)}
assert all(VEC_OFFSET[name] % w == 0 for name, w in VEC_LAYOUT)


def _vrow(name, l):
    w = VEC_WIDTH[name]
    return pl.BlockSpec((None, 1, w), lambda *_: (l, 0, VEC_OFFSET[name] // w))


def _inproj_rows(x, g_ref, w_ref, gq_ref, gk_ref, rope_refs, out_refs):
    q_ref, k_ref, v_ref, u_ref, xp_ref = out_refs
    h = _rms(x, g_ref[...]).astype(BF16)
    proj = jnp.dot(h, w_ref[...], preferred_element_type=F32)
    rc, rs1, rs2 = (r[...] for r in rope_refs)

    def head(t, g):
        t = _rms(t, g)
        return t * rc + pltpu.roll(t, ROT_HALF, 1) * rs1 + pltpu.roll(t, LANES - ROT_HALF, 1) * rs2

    for hd in range(N_HEADS):
        sl = slice(hd * HEAD_DIM, (hd + 1) * HEAD_DIM)
        q_ref[:, sl] = head(proj[:, sl], gq_ref[...])
    for hd in range(N_KV_HEADS):
        sl = slice(hd * HEAD_DIM, (hd + 1) * HEAD_DIM)
        k_ref[:, sl] = head(proj[:, ATTN_WIDTH + hd * HEAD_DIM:ATTN_WIDTH + (hd + 1) * HEAD_DIM], gk_ref[...])
    o2 = ATTN_WIDTH + KV_WIDTH
    o3 = o2 + KV_WIDTH
    o4 = o3 + SSM_WIDTH
    v_ref[...] = proj[:, o2:o3]
    for j in range(SSM_LANE_BLOCKS):
        u_ref[j] = proj[:, o3 + j * LANES:o3 + (j + 1) * LANES]
    xp_ref[...] = proj[:, o4:]


def _inproj_kernel(xm_ref, xh_ref, g_ref, w_ref, gq_ref, gk_ref, rcm_ref, rs1m_ref, rs2m_ref,
                   rch_ref, rs1h_ref, rs2h_ref, *rest):
    cast = len(rest) == 12
    outs = rest[1:] if cast else rest
    main_outs, head_outs = outs[0:5], outs[5:10]
    i = pl.program_id(0)
    last = pl.num_programs(0) - 1

    @pl.when(i < last)
    def _():
        if cast:
            src, dst = rest[0], outs[10]
            tf = dst.shape[2]
            for j in range(dst.shape[0]):
                dst[j] = src[:, j * tf:(j + 1) * tf].astype(BF16)
        _inproj_rows(xm_ref[...], g_ref, w_ref, gq_ref, gk_ref, (rcm_ref, rs1m_ref, rs2m_ref), main_outs)

    @pl.when(i == last)
    def _():
        _inproj_rows(xh_ref[...], g_ref, w_ref, gq_ref, gk_ref, (rch_ref, rs1h_ref, rs2h_ref), head_outs)


def _inproj(xm, xh, vecs, w, l, rope_m, rope_h, tm, cast=None, cast_tf=None):
    R, RH = xm.shape[0], xh.shape[0]
    nm = R // tm
    tiles_per_rope = rope_m[0].shape[0] // tm
    tile = lambda i: jnp.minimum(i, nm - 1)
    row = lambda i: (tile(i), 0)
    rrow = lambda i: (tile(i) % tiles_per_rope, 0)
    once = pl.Buffered(1)
    full = lambda n, w_: pl.BlockSpec((n, w_), lambda i: (0, 0))

    def out_set(n, rows, im, uim):
        flat = lambda w_: (pl.BlockSpec((n, w_), im), jax.ShapeDtypeStruct((rows, w_), F32))
        u_out = (pl.BlockSpec((SSM_LANE_BLOCKS, n, LANES), uim),
                 jax.ShapeDtypeStruct((SSM_LANE_BLOCKS, rows, LANES), F32))
        return [flat(ATTN_WIDTH), flat(KV_WIDTH), flat(KV_WIDTH), u_out, flat(POOL_WIDTH)]

    outs = (out_set(tm, R, row, lambda i: (0, tile(i), 0))
            + out_set(RH, RH, lambda i: (0, 0), lambda i: (0, 0, 0)))
    in_specs = [pl.BlockSpec((tm, D_MODEL), row),
                pl.BlockSpec((RH, D_MODEL), lambda i: (0, 0), pipeline_mode=once),
                _vrow("g_mix", l),
                pl.BlockSpec((D_MODEL, IN_WIDTH), lambda i: (0, 0), pipeline_mode=once),
                _vrow("g_q", l), _vrow("g_k", l),
                pl.BlockSpec((tm, LANES), rrow), pl.BlockSpec((tm, LANES), rrow), pl.BlockSpec((tm, LANES), rrow),
                full(RH, LANES), full(RH, LANES), full(RH, LANES)]
    args = [xm, xh, vecs, w, vecs, vecs, *rope_m, *rope_h]
    if cast is not None:
        _, cr, cc = cast.shape
        crows = cr // nm
        in_specs.append(pl.BlockSpec((None, crows, cc), lambda i: (l, tile(i), 0)))
        args.append(cast)
        outs.append((pl.BlockSpec((cc // cast_tf, crows, cast_tf), lambda i: (0, tile(i), 0)),
                     jax.ShapeDtypeStruct((cc // cast_tf, cr, cast_tf), BF16)))
    res = pl.pallas_call(
        _inproj_kernel,
        grid=(nm + 1,),
        in_specs=in_specs,
        out_specs=[o[0] for o in outs],
        out_shape=[o[1] for o in outs],
        compiler_params=_params("arbitrary"),
        name="inproj",
    )(*args)
    return res[0:5], res[5:10], (res[10] if cast is not None else None)


def _attn_bias():
    rows = GQA_GROUP * BLOCK
    i, r, c = np.meshgrid(np.arange(3), np.arange(rows) % BLOCK, np.arange(2 * BLOCK), indexing="ij")
    diff = BLOCK + r - c
    krow = (i - 1) * BLOCK + c
    mask = (diff >= 0) & (diff <= WINDOW) & (krow >= SEQ0)
    return np.where(mask, 0.0, -np.inf).astype(np.float32)


def _attn_block(q_blk, kp_blk, kc_blk, vp_blk, vc_blk, bias, sink_ref, l, g):
    rows = GQA_GROUP * BLOCK
    rgrp = jax.lax.broadcasted_iota(jnp.int32, (rows, 1), 0) // BLOCK
    outs = []
    for kh in range(N_KV_HEADS):
        ksl = slice(kh * HEAD_DIM, (kh + 1) * HEAD_DIM)
        qh = jnp.concatenate(
            [q_blk[:, (kh * GQA_GROUP + h) * HEAD_DIM:(kh * GQA_GROUP + h + 1) * HEAD_DIM]
             for h in range(GQA_GROUP)], axis=0).astype(BF16)
        kk = jnp.concatenate([kp_blk[:, ksl], kc_blk[:, ksl]], axis=0).astype(BF16)
        vv = jnp.concatenate([vp_blk[:, ksl], vc_blk[:, ksl]], axis=0).astype(BF16)
        s = jax.lax.dot_general(qh, kk, (((1,), (1,)), ((), ())),
                                preferred_element_type=F32) * (HEAD_DIM ** -0.5 * LOG2E) + bias
        sk = jnp.zeros((rows, 1), F32)
        for h in range(GQA_GROUP):
            sk = jnp.where(rgrp == h, sink_ref[l * N_HEADS + kh * GQA_GROUP + h] * LOG2E, sk)
        m = jnp.maximum(jnp.max(s, axis=-1, keepdims=True), sk)
        p = jnp.exp2(s - m)
        denom = jnp.sum(p, axis=-1, keepdims=True) + jnp.exp2(sk - m)
        o = jnp.dot(p.astype(BF16), vv, preferred_element_type=F32) / denom
        outs.extend(o[h * BLOCK:(h + 1) * BLOCK] for h in range(GQA_GROUP))
    return _rms(jnp.concatenate(outs, axis=1), g)


def _ssm_params_kernel(ar_ref, ai_ref, ldt_ref, kk_ref, br_ref, bi_ref, tr_ref, ti_ref, bbr_ref, bbi_ref):
    ar, ai = ar_ref[...], ai_ref[...]
    dt = jnp.exp(ldt_ref[...])
    kk = kk_ref[...]
    mag = jnp.exp(dt * ar * kk)
    ang = dt * ai * kk
    tr = mag * jnp.cos(ang)
    ti = mag * jnp.sin(ang)
    tr_ref[...] = tr
    ti_ref[...] = ti
    abr, abi = tr[0:1], ti[0:1]
    den = ar * ar + ai * ai
    fr = ((abr - 1.0) * ar + abi * ai) / den
    fi = (abi * ar - (abr - 1.0) * ai) / den
    br, bi = br_ref[...], bi_ref[...]
    bbr_ref[...] = fr * br - fi * bi
    bbi_ref[...] = fr * bi + fi * br


def _ssm_params(A_re, A_im, log_dt, B_re, B_im, C_re, C_im):
    depth = A_re.shape[0]
    n = depth * SSM_GROUPS * SSM_STATE
    row = lambda t: t.astype(F32).reshape(1, n)
    ldt = jnp.broadcast_to(log_dt.astype(F32)[:, :, None], (depth, SSM_GROUPS, SSM_STATE)).reshape(1, n)
    kk = jnp.array(SSM_POW_ROWS + (0,) * (SUBLANES - len(SSM_POW_ROWS)), F32).reshape(SUBLANES, 1)
    chan_first = lambda t: t.astype(F32).reshape(n, SSM_GROUP_SIZE).T
    shapes = [(SUBLANES, n), (SUBLANES, n), (SSM_GROUP_SIZE, n), (SSM_GROUP_SIZE, n)]
    tr, ti, bbr, bbi = pl.pallas_call(
        _ssm_params_kernel,
        out_shape=[jax.ShapeDtypeStruct(s, F32) for s in shapes],
        name="ssm_params",
    )(row(A_re), row(A_im), ldt, kk, chan_first(B_re), chan_first(B_im))
    J, G8 = SSM_LANE_BLOCKS, LANES // SSM_GROUP_SIZE

    def lanes(t):
        return t.reshape(SUBLANES, depth, J, SSM_BLOCK_STATES).transpose(1, 0, 2, 3)

    pw = jnp.concatenate([lanes(tr), lanes(ti)], axis=-1).reshape(depth, SUBLANES, SSM_STATE_LANES)
    eye = jnp.eye(G8, dtype=F32)

    def bdiag(t):
        t = t.reshape(SSM_GROUP_SIZE, depth, J, G8, SSM_STATE).transpose(1, 2, 3, 0, 4)
        t = t[:, :, :, :, None, :] * eye[None, None, :, None, :, None]
        return t.reshape(depth, J, LANES, SSM_BLOCK_STATES)

    bblk = jnp.concatenate([bdiag(bbr), bdiag(bbi)], axis=-1).astype(BF16)

    def cdiag(t):
        t = t.astype(F32).reshape(depth, J, G8, SSM_GROUP_SIZE, SSM_STATE).transpose(0, 1, 2, 4, 3)
        t = t[:, :, :, :, None, :] * eye[None, None, :, None, :, None]
        return t.reshape(depth, J, SSM_BLOCK_STATES, LANES)

    cblk = jnp.concatenate([cdiag(C_re), -cdiag(C_im)], axis=2).astype(BF16)
    return pw, bblk, cblk


def _ssm_tail(y, u, d_ref, wg_ref, bg_ref, g_ref):
    y = y + d_ref[...] * u
    z = jax.nn.gelu(y)
    gate = jax.nn.sigmoid(jnp.dot(z.astype(BF16), wg_ref[...], preferred_element_type=F32) + bg_ref[...])
    return _rms(z * gate, g_ref[...])


def _ssm_sweep(x_scr, n, a_tabs, init, store):
    S = SSM_BLOCK_STATES
    fins = []
    for j0 in range(0, SSM_LANE_BLOCKS, 2):
        js = (j0, j0 + 1)

        def body(k, carry, js=js):
            r0 = pl.multiple_of(k * SUBLANES, SUBLANES)
            out = []
            for idx, j in enumerate(js):
                hr, hi = carry[2 * idx], carry[2 * idx + 1]
                base = j * 2 * S
                ar, ai = a_tabs[j]
                nhr = ar * hr - ai * hi + x_scr[pl.ds(r0, SUBLANES), base:base + S]
                nhi = ar * hi + ai * hr + x_scr[pl.ds(r0, SUBLANES), base + S:base + 2 * S]
                if store:
                    x_scr[pl.ds(r0, SUBLANES), base:base + S] = nhr
                    x_scr[pl.ds(r0, SUBLANES), base + S:base + 2 * S] = nhi
                out += [nhr, nhi]
            return tuple(out)

        c0 = tuple(t for j in js for t in init[j])
        res = jax.lax.fori_loop(0, n // SUBLANES, body, c0, unroll=True)
        fins += [(res[0], res[1]), (res[2], res[3])]
    return fins


def _ssm_rows(u, pow_row, pw_ref, bblk_ref, cblk_ref, x_scr, s_scr, carry_scr):
    n = u.shape[0]
    S = SSM_BLOCK_STATES
    ub = u.astype(BF16)
    for j in range(SSM_LANE_BLOCKS):
        x_scr[0:n, j * 2 * S:(j + 1) * 2 * S] = jnp.dot(ub[:, j * LANES:(j + 1) * LANES], bblk_ref[j],
                                                       preferred_element_type=F32)
    bc = lambda t: jnp.broadcast_to(t, (SUBLANES, S))
    re = lambda ref, r0, r1, j: ref[r0:r1, j * 2 * S:j * 2 * S + S]
    im = lambda ref, r0, r1, j: ref[r0:r1, j * 2 * S + S:(j + 1) * 2 * S]
    a_tabs = [(bc(re(pw_ref, 0, 1, j)), bc(im(pw_ref, 0, 1, j))) for j in range(SSM_LANE_BLOCKS)]
    zero = jnp.zeros((SUBLANES, S), F32)
    fins = _ssm_sweep(x_scr, n, a_tabs, [(zero, zero)] * SSM_LANE_BLOCKS, store=False)
    for j in range(SSM_LANE_BLOCKS):
        base = j * 2 * S
        cr, ci = re(pw_ref, pow_row, pow_row + 1, j), im(pw_ref, pow_row, pow_row + 1, j)
        sr, si = re(carry_scr, 0, 1, j), im(carry_scr, 0, 1, j)
        fr, fi = fins[j]
        for c in range(SUBLANES):
            s_scr[c:c + 1, base:base + S] = sr
            s_scr[c:c + 1, base + S:base + 2 * S] = si
            sr, si = cr * sr - ci * si + fr[c:c + 1], cr * si + ci * sr + fi[c:c + 1]
        carry_scr[:, base:base + S] = bc(sr)
        carry_scr[:, base + S:base + 2 * S] = bc(si)
    init = [(re(s_scr, 0, SUBLANES, j), im(s_scr, 0, SUBLANES, j)) for j in range(SSM_LANE_BLOCKS)]
    _ssm_sweep(x_scr, n, a_tabs, init, store=True)
    ys = [jnp.dot(x_scr[0:n, j * 2 * S:(j + 1) * 2 * S].astype(BF16), cblk_ref[j], preferred_element_type=F32)
          for j in range(SSM_LANE_BLOCKS)]
    return jnp.concatenate(ys, axis=1)


def _ssm_tile(u_ref, n, pow_row, refs, o_ref, scr):
    pw_ref, bblk_ref, cblk_ref, d_ref, wg_ref, bg_ref, g_ref = refs
    up_scr, x_scr, s_scr, carry_scr, o_scr = scr
    q = n // SUBLANES
    for j in range(SSM_LANE_BLOCKS):
        for k in range(q):
            up_scr[k * SUBLANES:(k + 1) * SUBLANES, j * LANES:(j + 1) * LANES] = \
                u_ref[j, pl.ds(k, SUBLANES, stride=q), :]
    u = up_scr[0:n, :]
    y = _ssm_rows(u, pow_row, pw_ref, bblk_ref, cblk_ref, x_scr, s_scr, carry_scr)
    out = _ssm_tail(y, u, d_ref, wg_ref, bg_ref, g_ref)
    for j in range(SSM_LANE_BLOCKS):
        for k in range(q):
            o_scr[j, pl.ds(k, SUBLANES, stride=q), :] = out[k * SUBLANES:(k + 1) * SUBLANES,
                                                            j * LANES:(j + 1) * LANES]
    o_ref[...] = jnp.concatenate([o_scr[j, 0:n, :] for j in range(SSM_LANE_BLOCKS)], axis=1).astype(o_ref.dtype)


def _mixers_kernel(sink_ref,
                   qm_ref, qh_ref, kpm_ref, km_ref, kh_ref, vpm_ref, vm_ref, vh_ref, ga_ref, bias_ref,
                   um_ref, uh_ref, pw_ref, bblk_ref, cblk_ref, d_ref, wg_ref, bg_ref, gs_ref,
                   xm_ref, halo_ref, xh_ref, wp_ref, sc_ref, gp_ref, *rest, l, n_cast, n_dec):
    cast_in = rest[0:n_cast]
    am_ref, ah_ref, som_ref, soh_ref, st_ref, pom_ref, poh_ref = rest[n_cast:n_cast + 7]
    cast_out = rest[n_cast + 7:2 * n_cast + 7]
    scr = rest[2 * n_cast + 7:]
    carry_scr = scr[3]
    b = pl.program_id(0)
    t = pl.program_id(1)
    tm = xm_ref.shape[0]
    ssm_refs = (pw_ref, bblk_ref, cblk_ref, d_ref, wg_ref, bg_ref, gs_ref)
    blk = functools.partial(_attn_block, sink_ref=sink_ref, l=l, g=ga_ref[...])
    meta_rows = pl.ds(pl.multiple_of(n_dec + b * N_META, N_META), N_META)

    @pl.when((b == 0) & (t == 0))
    def _():
        for ref in (ah_ref, soh_ref, poh_ref):
            ref[0:n_dec, :] = jnp.zeros((n_dec, ref.shape[1]), ref.dtype)

    block = lambda ref: jnp.concatenate([jnp.zeros((SEQ0, ref.shape[1]), F32), ref[...]], axis=0)

    @pl.when(t == 0)
    def _():
        a = blk(block(qh_ref), kpm_ref[...], block(kh_ref), vpm_ref[...], block(vh_ref), bias_ref[0])
        ah_ref[meta_rows, :] = a[SEQ0:].astype(ah_ref.dtype)
        carry_scr[...] = jnp.zeros_like(carry_scr)
        _ssm_tile(uh_ref, N_META, POW_ROW_HEAD, ssm_refs, soh_ref.at[meta_rows, :], scr)
        prev = jnp.zeros((POOL_HALO, POOL_WIDTH), F32)
        poh_ref[meta_rows, :] = _pool_rows(xh_ref[...], prev, 0, wp_ref, sc_ref, gp_ref).astype(poh_ref.dtype)

    @pl.when(t > 0)
    def _():
        for src, dst in zip(cast_in, cast_out):
            dst[...] = src[...].astype(BF16)
        kp = jnp.where(t == 1, block(kh_ref), kpm_ref[...])
        vp = jnp.where(t == 1, block(vh_ref), vpm_ref[...])
        for n in range(tm // BLOCK):
            rows = slice(n * BLOCK, (n + 1) * BLOCK)
            kc, vc = km_ref[rows, :], vm_ref[rows, :]
            bias = bias_ref[jnp.minimum(t, 2)] if n == 0 else bias_ref[2]
            am_ref[rows, :] = blk(qm_ref[rows, :], kp, kc, vp, vc, bias).astype(am_ref.dtype)
            kp, vp = kc, vc
        _ssm_tile(um_ref, um_ref.shape[1], POW_ROW_MAIN, ssm_refs, som_ref, scr)
        st_ref[...] = carry_scr[...]
        prev = jnp.where(t == 1, xh_ref[...], halo_ref[...])
        pom_ref[...] = _pool_rows(xm_ref[...], prev, N_META + (t - 1) * tm, wp_ref, sc_ref, gp_ref).astype(pom_ref.dtype)


def _mixers(sinks, qkv_m, qkv_h, um, uh, xm, xh, vecs, bias, pw, bblk, cblk, wg, wp, l, B, tm, casts):
    assert (1, N_META // SUBLANES, tm // SUBLANES) == SSM_POW_ROWS and N_META == POOL_HALO
    J = SSM_LANE_BLOCKS
    rm, rh = um.shape[1], uh.shape[1]
    nt = rm // (B * tm)
    r = tm // POOL_HALO
    bpt = tm // BLOCK
    tile = lambda b, t: b * nt + jnp.maximum(t - 1, 0)
    main = lambda b, t, _: (tile(b, t), 0)
    n_dec = uh.shape[1] - B * N_META
    head = lambda b, t, _: (n_dec // N_META + b, 0)
    halo = lambda b, t, _: (jnp.maximum((b * nt + t - 1) * r - 1, 0), 0)
    prevb = lambda b, t, _: (jnp.maximum(tile(b, t) * bpt - 1, 0), 0)
    cast_specs = []
    for w, wl in casts:
        _, cr, cc = w.shape
        rows = cr // (B * nt)
        cast_specs.append((pl.BlockSpec((None, rows, cc), lambda b, t, _, wl=wl: (wl, tile(b, t), 0)),
                           pl.BlockSpec((rows, cc), main), jax.ShapeDtypeStruct((cr, cc), BF16)))
    bf = lambda n, w: jax.ShapeDtypeStruct((n, w), BF16)
    tile_spec = lambda w: pl.BlockSpec((tm, w), main)
    head_spec = lambda w: pl.BlockSpec((N_META, w), head)
    in_specs = [tile_spec(ATTN_WIDTH), head_spec(ATTN_WIDTH),
                pl.BlockSpec((BLOCK, KV_WIDTH), prevb), tile_spec(KV_WIDTH), head_spec(KV_WIDTH),
                pl.BlockSpec((BLOCK, KV_WIDTH), prevb), tile_spec(KV_WIDTH), head_spec(KV_WIDTH),
                _vrow("g_out_attn", l), _full(bias.shape),
                pl.BlockSpec((J, tm, LANES), lambda b, t, _: (0, tile(b, t), 0)),
                pl.BlockSpec((J, N_META, LANES), lambda b, t, _: (0, n_dec // N_META + b, 0)),
                _layer(pw.shape[1:], l), _layer(bblk.shape[1:], l), _layer(cblk.shape[1:], l),
                _vrow("D_skip", l), _layer((SSM_WIDTH, SSM_WIDTH), l), _vrow("b_glu", l), _vrow("g_out_ssm", l),
                tile_spec(POOL_WIDTH), pl.BlockSpec((POOL_HALO, POOL_WIDTH), halo), head_spec(POOL_WIDTH),
                _layer(wp.shape[1:], l), _vrow("pool_scale", l), _vrow("g_out_pool", l)]
    in_specs += [c[0] for c in cast_specs]
    head_out = lambda w: pl.BlockSpec((rh, w), lambda b, t, _: (0, 0))
    out_specs = [tile_spec(ATTN_WIDTH), head_out(ATTN_WIDTH), tile_spec(SSM_WIDTH), head_out(SSM_WIDTH),
                 pl.BlockSpec((None, SUBLANES, SSM_STATE_LANES), lambda b, t, _: (b, 0, 0)),
                 tile_spec(POOL_WIDTH), head_out(POOL_WIDTH)] + [c[1] for c in cast_specs]
    out_shape = [bf(rm, ATTN_WIDTH), bf(rh, ATTN_WIDTH), bf(rm, SSM_WIDTH), bf(rh, SSM_WIDTH),
                 jax.ShapeDtypeStruct((B, SUBLANES, SSM_STATE_LANES), F32),
                 bf(rm, POOL_WIDTH), bf(rh, POOL_WIDTH)] + [c[2] for c in cast_specs]
    (qm, km, vm), (qh, kh, vh) = qkv_m, qkv_h
    res = pl.pallas_call(
        functools.partial(_mixers_kernel, l=l, n_cast=len(casts), n_dec=n_dec),
        grid_spec=pltpu.PrefetchScalarGridSpec(
            num_scalar_prefetch=1, grid=(B, nt + 1), in_specs=in_specs, out_specs=out_specs,
            scratch_shapes=[pltpu.VMEM((tm, SSM_WIDTH), F32),
                            pltpu.VMEM((tm, SSM_STATE_LANES), F32),
                            pltpu.VMEM((SUBLANES, SSM_STATE_LANES), F32),
                            pltpu.VMEM((SUBLANES, SSM_STATE_LANES), F32),
                            pltpu.VMEM((J, tm, LANES), F32)]),
        out_shape=out_shape,
        compiler_params=_params("arbitrary", "arbitrary"),
        name="mixers",
    )(sinks, qm, qh, km, km, kh, vm, vm, vh, vecs, bias, um, uh, pw, bblk, cblk, vecs, wg, vecs, vecs,
      xm, xm, xh, wp, vecs, vecs, *[w for w, _ in casts])
    return res[0:7], res[7:]


def _state_from_lanes(s):
    s = s.reshape(s.shape[0], SSM_LANE_BLOCKS, 2, SSM_BLOCK_STATES)
    return (s[:, :, 0].reshape(-1, SSM_GROUPS, SSM_STATE), s[:, :, 1].reshape(-1, SSM_GROUPS, SSM_STATE))


def _pool_tail(d_groups, w_ref, sc_ref, g_ref):
    y = jnp.concatenate(
        [jnp.dot(d.astype(BF16), w_ref[gi], preferred_element_type=F32) for gi, d in enumerate(d_groups)], axis=1)
    return _rms(y * sc_ref[...], g_ref[...])


def _pool_rows(x, prev, pos0, w_ref, sc_ref, g_ref):
    n = x.shape[0]
    xe = jnp.concatenate([prev, x], axis=0)
    pos = pos0 + jax.lax.broadcasted_iota(jnp.int32, (n, 1), 0)
    ds = []
    for gi, w in enumerate(POOL_WINDOWS):
        gsl = slice(gi * POOL_GROUP, (gi + 1) * POOL_GROUP)
        s = xe[:, gsl]
        k = 1
        while k < w:
            s = s + pltpu.roll(s, k, 0)
            k *= 2
        cnt = jnp.clip(pos + 1, 1, w).astype(F32)
        ds.append(s[POOL_HALO:] / cnt - x[:, gsl])
    return _pool_tail(ds, w_ref, sc_ref, g_ref)


def _attn_sample_kernel(q_ref, kn_ref, vn_ref, kc_ref, vc_ref, sink_ref, g_ref, *rest, first):
    a_ref, nk_ref, nv_ref, acc_scr = rest[-4:]
    step = pl.program_id(1 if first else 0)
    nsteps = pl.num_programs(1 if first else 0)
    slot = WINDOW - 1 if first else 0
    scale = HEAD_DIM ** -0.5
    if first:
        for bb in range(DEC_STEP):
            nk_ref[bb, 0:WINDOW - 1] = kc_ref[bb, 1:WINDOW]
            nv_ref[bb, 0:WINDOW - 1] = vc_ref[bb, 1:WINDOW]
            nk_ref[bb, WINDOW - 1] = jnp.zeros((N_KV_HEADS, HEAD_DIM), F32)
            nv_ref[bb, WINDOW - 1] = jnp.zeros((N_KV_HEADS, HEAD_DIM), F32)

    def attend():
        _attn_sample_rows(q_ref, kn_ref, vn_ref, kc_ref, vc_ref, sink_ref, g_ref, nk_ref, nv_ref, acc_scr,
                          step, slot, scale)

        @pl.when(step == nsteps - 1)
        def _():
            a_ref[...] = acc_scr[...].astype(a_ref.dtype)

    if first:
        pl.when(pl.program_id(0) == 0)(attend)
    else:
        attend()


def _attn_sample_rows(q_ref, kn_ref, vn_ref, kc_ref, vc_ref, sink_ref, g_ref, nk_ref, nv_ref, acc_scr,
                      step, slot, scale):
    for bb in range(DEC_STEP):
        outs = []
        for kh in range(N_KV_HEADS):
            ksl = slice(kh * HEAD_DIM, (kh + 1) * HEAD_DIM)
            qh = jnp.concatenate(
                [q_ref[bb:bb + 1, (kh * GQA_GROUP + g) * HEAD_DIM:(kh * GQA_GROUP + g + 1) * HEAD_DIM]
                 for g in range(GQA_GROUP)], axis=0)
            kn = kn_ref[bb:bb + 1, ksl]
            vn = vn_ref[bb:bb + 1, ksl]
            nk_ref[bb, slot, kh:kh + 1, :] = kn
            nv_ref[bb, slot, kh:kh + 1, :] = vn
            kc = kc_ref[bb, :, kh, :]
            vc = vc_ref[bb, :, kh, :]
            sc = jax.lax.dot_general(qh.astype(BF16), kc.astype(BF16), (((1,), (1,)), ((), ())),
                                     preferred_element_type=F32) * scale
            sn = jnp.sum(qh * kn, axis=-1, keepdims=True) * scale
            sk = sink_ref[kh]
            m = jnp.maximum(jnp.maximum(jnp.max(sc, axis=-1, keepdims=True), sn), sk)
            pc = jnp.exp(sc - m)
            pn = jnp.exp(sn - m)
            denom = jnp.sum(pc, axis=-1, keepdims=True) + pn + jnp.exp(sk - m)
            o = jnp.dot(pc.astype(BF16), vc.astype(BF16), preferred_element_type=F32)
            o = (o + pn * vn) / denom
            outs.extend(o[g:g + 1] for g in range(GQA_GROUP))
        a = jnp.concatenate(outs, axis=1)
        acc_scr[pl.ds(step * DEC_STEP + bb, 1), :] = _rms(a, g_ref[...])


def _attn_sample(qh, kh, vh, cache_k, cache_v, l, sinks, g, ah, nk, nv, N):
    depth = cache_k.shape[0]
    first = nk is None
    anyspec = pl.BlockSpec(memory_space=pl.ANY)
    cshape = jax.ShapeDtypeStruct(cache_k.shape, F32)
    slab = (None, DEC_STEP, WINDOW, N_KV_HEADS, HEAD_DIM)
    if first:
        assert l == 0
        grid = (depth, N // DEC_STEP)
        rows = lambda w: pl.BlockSpec((DEC_STEP, w), lambda ll, s: (s, 0))
        cache = pl.BlockSpec(slab, lambda ll, s: (ll, s, 0, 0, 0))
        new = cache
        a_spec = pl.BlockSpec((N, ATTN_WIDTH), lambda ll, s: (0, 0))
        carried = []
    else:
        grid = (N // DEC_STEP,)
        rows = lambda w: pl.BlockSpec((DEC_STEP, w), lambda s: (s, 0))
        cache = pl.BlockSpec(slab, lambda s: (l, s, 0, 0, 0))
        new = pl.BlockSpec((None, DEC_STEP, 1, N_KV_HEADS, HEAD_DIM), lambda s: (l, s, WINDOW - 1, 0, 0))
        a_spec = pl.BlockSpec((N, ATTN_WIDTH), lambda s: (0, 0))
        carried = [nk, nv]
    return pl.pallas_call(
        functools.partial(_attn_sample_kernel, first=first),
        grid=grid,
        in_specs=[rows(ATTN_WIDTH), rows(KV_WIDTH), rows(KV_WIDTH), cache, cache,
                  _layer((N_KV_HEADS, GQA_GROUP, 1), l), _vrow("g_out_attn", l)] + [anyspec] * (1 + len(carried)),
        out_specs=[a_spec, new, new],
        out_shape=[jax.ShapeDtypeStruct(ah.shape, ah.dtype), cshape, cshape],
        scratch_shapes=[pltpu.VMEM((N, ATTN_WIDTH), F32)],
        input_output_aliases={7 + i: i for i in range(1 + len(carried))},
        compiler_params=_params(*(["arbitrary"] * len(grid))),
        name="attn_sample",
    )(qh, kh, vh, cache_k, cache_v, sinks, g, ah, *carried)


def _mix_sample_kernel(u_ref, h0_ref, pw_ref, bblk_ref, cblk_ref, d_ref, wg_ref, bg_ref, gs_ref,
                       xp_ref, pb_ref, wp_ref, sc_ref, gp_ref, s_in_ref, p_in_ref, s_ref, st_ref, p_ref):
    del s_in_ref, p_in_ref
    S = SSM_BLOCK_STATES
    u = jnp.concatenate([u_ref[j] for j in range(SSM_LANE_BLOCKS)], axis=1)
    ub = u.astype(BF16)
    ys = []
    for j in range(SSM_LANE_BLOCKS):
        x = jnp.dot(ub[:, j * LANES:(j + 1) * LANES], bblk_ref[j], preferred_element_type=F32)
        base = j * 2 * S
        ar = pw_ref[0:1, base:base + S]
        ai = pw_ref[0:1, base + S:base + 2 * S]
        h0r = h0_ref[:, base:base + S]
        h0i = h0_ref[:, base + S:base + 2 * S]
        hr = x[:, 0:S] + ar * h0r - ai * h0i
        hi = x[:, S:] + ar * h0i + ai * h0r
        st_ref[:, base:base + S] = hr
        st_ref[:, base + S:base + 2 * S] = hi
        h = jnp.concatenate([hr, hi], axis=1).astype(BF16)
        ys.append(jnp.dot(h, cblk_ref[j], preferred_element_type=F32))
    s_ref[...] = _ssm_tail(jnp.concatenate(ys, axis=1), u, d_ref, wg_ref, bg_ref, gs_ref).astype(s_ref.dtype)

    xp = xp_ref[...]
    ds = []
    for gi, w in enumerate(POOL_WINDOWS):
        gsl = slice(gi * POOL_GROUP, (gi + 1) * POOL_GROUP)
        s = xp[:, gsl]
        for back in range(1, w):
            s = s + pb_ref[POOL_BUF - back][:, gsl]
        ds.append(s / float(w) - xp[:, gsl])
    p_ref[...] = _pool_tail(ds, wp_ref, sc_ref, gp_ref).astype(p_ref.dtype)


def _mix_sample(uh, xph, h0, pbuf, vecs, pw, bblk, cblk, wg, wp, l, sh, ph):
    N = h0.shape[1]
    rows = lambda w: pl.BlockSpec((N, w), lambda i: (0, 0))
    anyspec = pl.BlockSpec(memory_space=pl.ANY)
    return pl.pallas_call(
        _mix_sample_kernel,
        grid=(1,),
        in_specs=[pl.BlockSpec((SSM_LANE_BLOCKS, N, LANES), lambda i: (0, 0, 0)),
                  _layer(h0.shape[1:], l), _layer(pw.shape[1:], l), _layer(bblk.shape[1:], l),
                  _layer(cblk.shape[1:], l),
                  _vrow("D_skip", l), _layer((SSM_WIDTH, SSM_WIDTH), l), _vrow("b_glu", l),
                  _vrow("g_out_ssm", l), rows(POOL_WIDTH), _layer(pbuf.shape[1:], l),
                  _layer(wp.shape[1:], l),
                  _vrow("pool_scale", l), _vrow("g_out_pool", l), anyspec, anyspec],
        out_specs=[rows(SSM_WIDTH), _full((N, SSM_STATE_LANES)), rows(POOL_WIDTH)],
        out_shape=[jax.ShapeDtypeStruct(sh.shape, sh.dtype), jax.ShapeDtypeStruct((N, SSM_STATE_LANES), F32),
                   jax.ShapeDtypeStruct(ph.shape, ph.dtype)],
        input_output_aliases={14: 0, 15: 2},
        compiler_params=_params("arbitrary"),
        name="mix_sample",
    )(uh, h0, pw, bblk, cblk, vecs, wg, vecs, vecs, xph, pbuf, wp, vecs, vecs, sh, ph)


def _outproj_rows(x_ref, a_ref, s_ref, p_ref, w_ref, o_ref):
    o1 = ATTN_WIDTH
    o2 = o1 + SSM_WIDTH
    acc = x_ref[...]
    acc = acc + jnp.dot(a_ref[...], w_ref[0:o1, :], preferred_element_type=F32)
    acc = acc + jnp.dot(s_ref[...], w_ref[o1:o2, :], preferred_element_type=F32)
    acc = acc + jnp.dot(p_ref[...], w_ref[o2:, :], preferred_element_type=F32)
    o_ref[...] = acc


def _outproj_kernel(xm_ref, am_ref, sm_ref, pm_ref, xh_ref, ah_ref, sh_ref, ph_ref, w_ref, om_ref, oh_ref):
    i = pl.program_id(0)
    last = pl.num_programs(0) - 1

    @pl.when(i < last)
    def _():
        _outproj_rows(xm_ref, am_ref, sm_ref, pm_ref, w_ref, om_ref)

    @pl.when(i == last)
    def _():
        _outproj_rows(xh_ref, ah_ref, sh_ref, ph_ref, w_ref, oh_ref)


def _outproj(xm, am, sm, pm, xh, ah, sh, ph, w, tm):
    R, RH = xm.shape[0], xh.shape[0]
    nm = R // tm
    row = lambda i: (jnp.minimum(i, nm - 1), 0)
    once = pl.Buffered(1)
    head = lambda w_: pl.BlockSpec((RH, w_), lambda i: (0, 0), pipeline_mode=once)
    return pl.pallas_call(
        _outproj_kernel,
        grid=(nm + 1,),
        in_specs=[pl.BlockSpec((tm, D_MODEL), row), pl.BlockSpec((tm, ATTN_WIDTH), row),
                  pl.BlockSpec((tm, SSM_WIDTH), row), pl.BlockSpec((tm, POOL_WIDTH), row),
                  head(D_MODEL), head(ATTN_WIDTH), head(SSM_WIDTH), head(POOL_WIDTH),
                  pl.BlockSpec((D_MODEL, D_MODEL), lambda i: (0, 0), pipeline_mode=once)],
        out_specs=[pl.BlockSpec((tm, D_MODEL), row), pl.BlockSpec((RH, D_MODEL), lambda i: (0, 0))],
        out_shape=[jax.ShapeDtypeStruct((R, D_MODEL), F32), jax.ShapeDtypeStruct((RH, D_MODEL), F32)],
        compiler_params=_params("arbitrary"),
        name="outproj",
    )(xm, am, sm, pm, xh, ah, sh, ph, w)


def _ffn_kernel(xm_ref, xh_ref, g_ref, w1_hbm, w2_hbm, om_ref, oh_ref, hm_scr, hh_scr, w1_buf, w2_buf, sem, *, tf):
    i = pl.program_id(0)
    last_tile = pl.num_programs(0) - 1
    nf = D_FF // tf

    def copies(f, slot):
        row = pl.multiple_of(f * tf, tf)
        return (pltpu.make_async_copy(w1_hbm.at[f], w1_buf.at[slot], sem.at[0, slot]),
                pltpu.make_async_copy(w2_hbm.at[pl.ds(row, tf), :], w2_buf.at[slot], sem.at[1, slot]))

    def start(f, slot):
        for c in copies(f, slot):
            c.start()

    def wait(f, slot):
        for c in copies(f, slot):
            c.wait()

    def mlp(h, slot):
        h1 = jnp.dot(h, w1_buf[slot], preferred_element_type=F32)
        return jnp.dot(jnp.square(jnp.maximum(h1, 0.0)).astype(BF16), w2_buf[slot], preferred_element_type=F32)

    def first(x_ref, h_scr, o_ref):
        x = x_ref[...]
        h = _rms(x, g_ref[...]).astype(BF16)
        h_scr[...] = h
        o_ref[...] = x + mlp(h, 0)

    @pl.when(i == 0)
    def _():
        start(0, 0)

    wait(0, 0)
    start(1, 1)
    first(xm_ref, hm_scr, om_ref)

    @pl.when(i == 0)
    def _():
        first(xh_ref, hh_scr, oh_ref)

    def body(f, carry):
        slot = f % 2
        wait(f, slot)

        @pl.when(f + 1 < nf)
        def _():
            start(f + 1, 1 - slot)

        @pl.when((f + 1 == nf) & (i < last_tile))
        def _():
            start(0, 0)

        om_ref[...] += mlp(hm_scr[...], slot)

        @pl.when(i == 0)
        def _():
            oh_ref[...] += mlp(hh_scr[...], slot)

        return carry

    jax.lax.fori_loop(1, nf, body, 0)


def _ffn(xm, xh, g, l, w1, w2, tm, tf):
    R, RH = xm.shape[0], xh.shape[0]
    assert w1.shape == (D_FF // tf, D_MODEL, tf) and (D_FF // tf) % 2 == 0
    anyspec = pl.BlockSpec(memory_space=pl.ANY)
    return pl.pallas_call(
        functools.partial(_ffn_kernel, tf=tf),
        grid=(R // tm,),
        in_specs=[pl.BlockSpec((tm, D_MODEL), lambda i: (i, 0)),
                  pl.BlockSpec((RH, D_MODEL), lambda i: (0, 0), pipeline_mode=pl.Buffered(1)),
                  _vrow("g_ffn", l), anyspec, anyspec],
        out_specs=[pl.BlockSpec((tm, D_MODEL), lambda i: (i, 0)), pl.BlockSpec((RH, D_MODEL), lambda i: (0, 0))],
        out_shape=[jax.ShapeDtypeStruct((R, D_MODEL), F32), jax.ShapeDtypeStruct((RH, D_MODEL), F32)],
        scratch_shapes=[pltpu.VMEM((tm, D_MODEL), BF16), pltpu.VMEM((RH, D_MODEL), BF16),
                        pltpu.VMEM((2, D_MODEL, tf), BF16), pltpu.VMEM((2, tf, D_MODEL), BF16),
                        pltpu.SemaphoreType.DMA((2, 2))],
        compiler_params=_params("arbitrary"),
        name="ffn",
    )(xm, xh, g, w1, w2)


def _rope_tables(pos):
    half = ROT_HALF
    inv = ROPE_THETA ** (-np.arange(0, ROT_DIM, 2, dtype=np.float64) / ROT_DIM)
    ang = np.asarray(pos, np.float64)[:, None] * inv
    cos, sin = np.cos(ang), np.sin(ang)
    n = ang.shape[0]
    z = np.zeros((n, HEAD_DIM - ROT_DIM))
    zh = np.zeros((n, half))
    rc = np.concatenate([cos, cos, z + 1.0], axis=1)
    rs1 = np.concatenate([zh, sin, z], axis=1)
    rs2 = np.concatenate([-sin, zh, z], axis=1)
    return tuple(t.astype(np.float32) for t in (rc, rs1, rs2))


def kernel(x_prompt, x_sample, cache_k, cache_v, state_ssm_re, state_ssm_im, state_pool, meta_tokens, g_mix, w_in, g_q, g_k, sinks, A_re, A_im, log_dt, B_re, B_im, C_re, C_im, D_skip, w_glu, b_glu, w_pool, pool_scale, g_out_attn, g_out_ssm, g_out_pool, w_out, g_ffn, w_ff1, w_ff2):
    B, T, _ = x_prompt.shape
    N = x_sample.shape[0]
    depth = w_in.shape[0]
    assert N % N_META == 0 and N % DEC_STEP == 0 and T % TM_FFN == 0
    xh = jnp.concatenate([x_sample.reshape(N, D_MODEL)] + [meta_tokens.astype(F32)] * B, axis=0)
    xm = x_prompt.reshape(B * T, D_MODEL)

    rope_m = _rope_tables(N_META + np.arange(T))
    rope_h = _rope_tables(np.concatenate([np.full(N, PAST_LEN)] + [np.arange(N_META)] * B))

    wi = w_in[0].astype(BF16)
    wg_all, wp_all = w_glu.astype(BF16), w_pool.astype(BF16)
    ck = cache_k.astype(F32)
    cv = cache_v.astype(F32)

    named = dict(g_mix=g_mix, g_ffn=g_ffn, g_out_attn=g_out_attn, D_skip=D_skip, b_glu=b_glu,
                 g_out_ssm=g_out_ssm, pool_scale=pool_scale, g_out_pool=g_out_pool, g_q=g_q, g_k=g_k)
    vecs = jnp.concatenate([named[name].astype(F32) for name, _ in VEC_LAYOUT], axis=1)[:, None, :]
    pw, bblk, cblk = _ssm_params(A_re, A_im, log_dt, B_re, B_im, C_re, C_im)
    sinks_flat = sinks.astype(F32).reshape(depth * N_HEADS)
    sinks_col = sinks.astype(F32).reshape(depth, N_KV_HEADS, GQA_GROUP, 1)
    bias = _attn_bias()
    h0 = jnp.concatenate([state_ssm_re.astype(F32).reshape(depth, N, SSM_LANE_BLOCKS, SSM_BLOCK_STATES),
                          state_ssm_im.astype(F32).reshape(depth, N, SSM_LANE_BLOCKS, SSM_BLOCK_STATES)],
                         axis=-1).reshape(depth, N, SSM_STATE_LANES)
    pbuf = state_pool.astype(F32).transpose(0, 2, 1, 3)

    nk = nv = None
    ks, vs, pls, sts, st_ss, phs = ([] for _ in range(6))
    for l in range(depth):
        (qm, km, vm, um, pm), (qh, kh, vh, uh, ph), w1 = _inproj(xm, xh, vecs, wi, l, rope_m, rope_h, TM_PROJ,
                                                                 w_ff1, TF_FFN)

        (am, ah, sm, sh, st, plm, plh), wcast = _mixers(
            sinks_flat, (qm, km, vm), (qh, kh, vh), um, uh, pm, ph, vecs, bias, pw, bblk, cblk, wg_all, wp_all,
            l, B, TM_SEQ, [(w_ff2, l), (w_out, l)] + ([(w_in, l + 1)] if l + 1 < depth else []))
        w2, wo = wcast[0], wcast[1]

        ah, nk, nv = _attn_sample(qh, kh, vh, ck, cv, l, sinks_col, vecs, ah, nk, nv, N)
        sh, st_s, plh = _mix_sample(uh, ph, h0, pbuf, vecs, pw, bblk, cblk, wg_all, wp_all, l, sh, plh)

        xm, xh = _outproj(xm, am, sm, plm, xh, ah, sh, plh, wo, TM_PROJ)
        xm, xh = _ffn(xm, xh, vecs, l, w1, w2, TM_FFN, TF_FFN)
        if l + 1 < depth:
            wi = wcast[2]

        ks.append(km.reshape(B, T, KV_WIDTH)[:, T - WINDOW:])
        vs.append(vm.reshape(B, T, KV_WIDTH)[:, T - WINDOW:])
        pls.append(pm.reshape(B, T, POOL_WIDTH)[:, T - POOL_BUF:])
        sts.append(st[:, 0])
        st_ss.append(st_s)
        phs.append(ph[:N])

    y_prompt = xm.reshape(B, T, D_MODEL)
    y_sample = xh[:N].reshape(N, 1, D_MODEL)
    heads = lambda t: jnp.stack(t).reshape(depth, -1, WINDOW, N_KV_HEADS, HEAD_DIM)
    p_re, p_im = _state_from_lanes(jnp.stack(sts).reshape(depth * B, SSM_STATE_LANES))
    s_re, s_im = _state_from_lanes(jnp.stack(st_ss).reshape(depth * N, SSM_STATE_LANES))
    st4 = lambda t, n: t.reshape(depth, n, SSM_GROUPS, SSM_STATE)
    s_pool = jnp.concatenate([state_pool.astype(F32)[:, :, 1:], jnp.stack(phs)[:, :, None]], axis=2)
    return (y_prompt, y_sample, heads(ks), heads(vs), st4(p_re, B), st4(p_im, B), jnp.stack(pls),
            nk, nv, st4(s_re, N), st4(s_im, N), s_pool)
```

```python
import functools
import math

import jax
import jax.numpy as jnp
import numpy as np
from jax.experimental import pallas as pl
from jax.experimental.pallas import tpu as pltpu

D_MODEL = 2048
N_META = 16
HEAD_DIM = 128
N_HEADS = 8
N_KV_HEADS = 2
GQA_GROUP = 4
ATTN_WIDTH = 1024
KV_WIDTH = 256
WINDOW = 128
BLOCK = 128
ROT_DIM = 32
ROT_HALF = ROT_DIM // 2
ROPE_THETA = 500000.0
SSM_WIDTH = 512
SSM_GROUP_SIZE = 16
SSM_GROUPS = 32
SSM_STATE = 64
POOL_WIDTH = 512
POOL_WINDOWS = (2, 4, 8, 16)
POOL_GROUP = 128
POOL_BUF = 15
POOL_HALO = 16
IN_WIDTH = 2560
D_FF = 8192
EPS = 1e-6
PAST_LEN = 16384
LOG2E = math.log2(math.e)

SEQ0 = BLOCK - N_META
LANES = 128
SUBLANES = 8
SSM_LANE_BLOCKS = SSM_WIDTH // LANES
SSM_BLOCK_STATES = (LANES // SSM_GROUP_SIZE) * SSM_STATE
SSM_STATE_LANES = SSM_LANE_BLOCKS * 2 * SSM_BLOCK_STATES
VMEM_LIMIT = 60 * 1024 * 1024

TM_PROJ = 512
TM_FFN = 1024
TF_FFN = 512
TM_SEQ = 512
DEC_STEP = 8
SSM_POW_ROWS = (1, N_META // SUBLANES, TM_SEQ // SUBLANES)
POW_ROW_HEAD, POW_ROW_MAIN = 1, 2

BF16 = jnp.bfloat16
F32 = jnp.float32


def _params(*semantics):
    return pltpu.CompilerParams(dimension_semantics=semantics, vmem_limit_bytes=VMEM_LIMIT)


def _rms(x, g):
    return x * jax.lax.rsqrt(jnp.mean(x * x, axis=-1, keepdims=True) + EPS) * g


def _full(shape):
    n = len(shape)
    return pl.BlockSpec(shape, lambda *_: (0,) * n)


def _layer(shape, l):
    n = len(shape)
    return pl.BlockSpec((None, *shape), lambda *_: (l,) + (0,) * n)


VEC_LAYOUT = (("g_mix", D_MODEL), ("g_ffn", D_MODEL), ("g_out_attn", ATTN_WIDTH), ("D_skip", SSM_WIDTH),
              ("b_glu", SSM_WIDTH), ("g_out_ssm", SSM_WIDTH), ("pool_scale", POOL_WIDTH),
              ("g_out_pool", POOL_WIDTH), ("g_q", HEAD_DIM), ("g_k", HEAD_DIM))
VEC_WIDTH = dict(VEC_LAYOUT)
VEC_OFFSET = {name: sum(w for _, w in VEC_LAYOUT[:i]) for i, (name, _) in enumerate(VEC_LAYOUT)}
assert all(VEC_OFFSET[name] % w == 0 for name, w in VEC_LAYOUT)


def _vrow(name, l):
    w = VEC_WIDTH[name]
    return pl.BlockSpec((None, 1, w), lambda *_: (l, 0, VEC_OFFSET[name] // w))


def _inproj_rows(x, g_ref, w_ref, gq_ref, gk_ref, rope_refs, out_refs):
    q_ref, k_ref, v_ref, u_ref, xp_ref = out_refs
    h = _rms(x, g_ref[...]).astype(BF16)
    proj = jnp.dot(h, w_ref[...], preferred_element_type=F32)
    rc, rs1, rs2 = (r[...] for r in rope_refs)

    def head(t, g):
        t = _rms(t, g)
        return t * rc + pltpu.roll(t, ROT_HALF, 1) * rs1 + pltpu.roll(t, LANES - ROT_HALF, 1) * rs2

    for hd in range(N_HEADS):
        sl = slice(hd * HEAD_DIM, (hd + 1) * HEAD_DIM)
        q_ref[:, sl] = head(proj[:, sl], gq_ref[...])
    for hd in range(N_KV_HEADS):
        sl = slice(hd * HEAD_DIM, (hd + 1) * HEAD_DIM)
        k_ref[:, sl] = head(proj[:, ATTN_WIDTH + hd * HEAD_DIM:ATTN_WIDTH + (hd + 1) * HEAD_DIM], gk_ref[...])
    o2 = ATTN_WIDTH + KV_WIDTH
    o3 = o2 + KV_WIDTH
    o4 = o3 + SSM_WIDTH
    v_ref[...] = proj[:, o2:o3]
    for j in range(SSM_LANE_BLOCKS):
        u_ref[j] = proj[:, o3 + j * LANES:o3 + (j + 1) * LANES]
    xp_ref[...] = proj[:, o4:]


def _inproj_kernel(xm_ref, xh_ref, g_ref, w_ref, gq_ref, gk_ref, rcm_ref, rs1m_ref, rs2m_ref,
                   rch_ref, rs1h_ref, rs2h_ref, *rest):
    cast = len(rest) == 12
    outs = rest[1:] if cast else rest
    main_outs, head_outs = outs[0:5], outs[5:10]
    i = pl.program_id(0)
    last = pl.num_programs(0) - 1

    @pl.when(i < last)
    def _():
        if cast:
            outs[10][...] = rest[0][...].astype(BF16)
        _inproj_rows(xm_ref[...], g_ref, w_ref, gq_ref, gk_ref, (rcm_ref, rs1m_ref, rs2m_ref), main_outs)

    @pl.when(i == last)
    def _():
        _inproj_rows(xh_ref[...], g_ref, w_ref, gq_ref, gk_ref, (rch_ref, rs1h_ref, rs2h_ref), head_outs)


def _inproj(xm, xh, vecs, w, l, rope_m, rope_h, tm, cast=None):
    R, RH = xm.shape[0], xh.shape[0]
    nm = R // tm
    tiles_per_rope = rope_m[0].shape[0] // tm
    tile = lambda i: jnp.minimum(i, nm - 1)
    row = lambda i: (tile(i), 0)
    rrow = lambda i: (tile(i) % tiles_per_rope, 0)
    once = pl.Buffered(1)
    full = lambda n, w_: pl.BlockSpec((n, w_), lambda i: (0, 0))

    def out_set(n, rows, im, uim):
        flat = lambda w_: (pl.BlockSpec((n, w_), im), jax.ShapeDtypeStruct((rows, w_), F32))
        u_out = (pl.BlockSpec((SSM_LANE_BLOCKS, n, LANES), uim),
                 jax.ShapeDtypeStruct((SSM_LANE_BLOCKS, rows, LANES), F32))
        return [flat(ATTN_WIDTH), flat(KV_WIDTH), flat(KV_WIDTH), u_out, flat(POOL_WIDTH)]

    outs = (out_set(tm, R, row, lambda i: (0, tile(i), 0))
            + out_set(RH, RH, lambda i: (0, 0), lambda i: (0, 0, 0)))
    in_specs = [pl.BlockSpec((tm, D_MODEL), row),
                pl.BlockSpec((RH, D_MODEL), lambda i: (0, 0), pipeline_mode=once),
                _vrow("g_mix", l),
                pl.BlockSpec((D_MODEL, IN_WIDTH), lambda i: (0, 0), pipeline_mode=once),
                _vrow("g_q", l), _vrow("g_k", l),
                pl.BlockSpec((tm, LANES), rrow), pl.BlockSpec((tm, LANES), rrow), pl.BlockSpec((tm, LANES), rrow),
                full(RH, LANES), full(RH, LANES), full(RH, LANES)]
    args = [xm, xh, vecs, w, vecs, vecs, *rope_m, *rope_h]
    if cast is not None:
        _, cr, cc = cast.shape
        crows = cr // nm
        in_specs.append(pl.BlockSpec((None, crows, cc), lambda i: (l, tile(i), 0)))
        args.append(cast)
        outs.append((pl.BlockSpec((crows, cc), row), jax.ShapeDtypeStruct((cr, cc), BF16)))
    res = pl.pallas_call(
        _inproj_kernel,
        grid=(nm + 1,),
        in_specs=in_specs,
        out_specs=[o[0] for o in outs],
        out_shape=[o[1] for o in outs],
        compiler_params=_params("arbitrary"),
        name="inproj",
    )(*args)
    return res[0:5], res[5:10], (res[10] if cast is not None else None)


def _attn_bias():
    rows = GQA_GROUP * BLOCK
    i, r, c = np.meshgrid(np.arange(3), np.arange(rows) % BLOCK, np.arange(2 * BLOCK), indexing="ij")
    diff = BLOCK + r - c
    krow = (i - 1) * BLOCK + c
    mask = (diff >= 0) & (diff <= WINDOW) & (krow >= SEQ0)
    return np.where(mask, 0.0, -np.inf).astype(np.float32)


def _attn_block(q_blk, kp_blk, kc_blk, vp_blk, vc_blk, bias, sink_ref, l, g):
    rows = GQA_GROUP * BLOCK
    rgrp = jax.lax.broadcasted_iota(jnp.int32, (rows, 1), 0) // BLOCK
    outs = []
    for kh in range(N_KV_HEADS):
        ksl = slice(kh * HEAD_DIM, (kh + 1) * HEAD_DIM)
        qh = jnp.concatenate(
            [q_blk[:, (kh * GQA_GROUP + h) * HEAD_DIM:(kh * GQA_GROUP + h + 1) * HEAD_DIM]
             for h in range(GQA_GROUP)], axis=0).astype(BF16)
        kk = jnp.concatenate([kp_blk[:, ksl], kc_blk[:, ksl]], axis=0).astype(BF16)
        vv = jnp.concatenate([vp_blk[:, ksl], vc_blk[:, ksl]], axis=0).astype(BF16)
        s = jax.lax.dot_general(qh, kk, (((1,), (1,)), ((), ())),
                                preferred_element_type=F32) * (HEAD_DIM ** -0.5 * LOG2E) + bias
        sk = jnp.zeros((rows, 1), F32)
        for h in range(GQA_GROUP):
            sk = jnp.where(rgrp == h, sink_ref[l * N_HEADS + kh * GQA_GROUP + h] * LOG2E, sk)
        m = jnp.maximum(jnp.max(s, axis=-1, keepdims=True), sk)
        p = jnp.exp2(s - m)
        denom = jnp.sum(p, axis=-1, keepdims=True) + jnp.exp2(sk - m)
        o = jnp.dot(p.astype(BF16), vv, preferred_element_type=F32) / denom
        outs.extend(o[h * BLOCK:(h + 1) * BLOCK] for h in range(GQA_GROUP))
    return _rms(jnp.concatenate(outs, axis=1), g)


def _ssm_params_kernel(ar_ref, ai_ref, ldt_ref, kk_ref, br_ref, bi_ref, tr_ref, ti_ref, bbr_ref, bbi_ref):
    ar, ai = ar_ref[...], ai_ref[...]
    dt = jnp.exp(ldt_ref[...])
    kk = kk_ref[...]
    mag = jnp.exp(dt * ar * kk)
    ang = dt * ai * kk
    tr = mag * jnp.cos(ang)
    ti = mag * jnp.sin(ang)
    tr_ref[...] = tr
    ti_ref[...] = ti
    abr, abi = tr[0:1], ti[0:1]
    den = ar * ar + ai * ai
    fr = ((abr - 1.0) * ar + abi * ai) / den
    fi = (abi * ar - (abr - 1.0) * ai) / den
    br, bi = br_ref[...], bi_ref[...]
    bbr_ref[...] = fr * br - fi * bi
    bbi_ref[...] = fr * bi + fi * br


def _ssm_params(A_re, A_im, log_dt, B_re, B_im, C_re, C_im):
    depth = A_re.shape[0]
    n = depth * SSM_GROUPS * SSM_STATE
    row = lambda t: t.astype(F32).reshape(1, n)
    ldt = jnp.broadcast_to(log_dt.astype(F32)[:, :, None], (depth, SSM_GROUPS, SSM_STATE)).reshape(1, n)
    kk = jnp.array(SSM_POW_ROWS + (0,) * (SUBLANES - len(SSM_POW_ROWS)), F32).reshape(SUBLANES, 1)
    chan_first = lambda t: t.astype(F32).reshape(n, SSM_GROUP_SIZE).T
    shapes = [(SUBLANES, n), (SUBLANES, n), (SSM_GROUP_SIZE, n), (SSM_GROUP_SIZE, n)]
    tr, ti, bbr, bbi = pl.pallas_call(
        _ssm_params_kernel,
        out_shape=[jax.ShapeDtypeStruct(s, F32) for s in shapes],
        name="ssm_params",
    )(row(A_re), row(A_im), ldt, kk, chan_first(B_re), chan_first(B_im))
    J, G8 = SSM_LANE_BLOCKS, LANES // SSM_GROUP_SIZE

    def lanes(t):
        return t.reshape(SUBLANES, depth, J, SSM_BLOCK_STATES).transpose(1, 0, 2, 3)

    pw = jnp.concatenate([lanes(tr), lanes(ti)], axis=-1).reshape(depth, SUBLANES, SSM_STATE_LANES)
    eye = jnp.eye(G8, dtype=F32)

    def bdiag(t):
        t = t.reshape(SSM_GROUP_SIZE, depth, J, G8, SSM_STATE).transpose(1, 2, 3, 0, 4)
        t = t[:, :, :, :, None, :] * eye[None, None, :, None, :, None]
        return t.reshape(depth, J, LANES, SSM_BLOCK_STATES)

    bblk = jnp.concatenate([bdiag(bbr), bdiag(bbi)], axis=-1).astype(BF16)

    def cdiag(t):
        t = t.astype(F32).reshape(depth, J, G8, SSM_GROUP_SIZE, SSM_STATE).transpose(0, 1, 2, 4, 3)
        t = t[:, :, :, :, None, :] * eye[None, None, :, None, :, None]
        return t.reshape(depth, J, SSM_BLOCK_STATES, LANES)

    cblk = jnp.concatenate([cdiag(C_re), -cdiag(C_im)], axis=2).astype(BF16)
    return pw, bblk, cblk


def _ssm_tail(y, u, d_ref, wg_ref, bg_ref, g_ref):
    y = y + d_ref[...] * u
    z = jax.nn.gelu(y)
    gate = jax.nn.sigmoid(jnp.dot(z.astype(BF16), wg_ref[...], preferred_element_type=F32) + bg_ref[...])
    return _rms(z * gate, g_ref[...])


def _ssm_sweep(x_scr, n, a_tabs, init, store):
    S = SSM_BLOCK_STATES
    fins = []
    for j0 in range(0, SSM_LANE_BLOCKS, 2):
        js = (j0, j0 + 1)

        def body(k, carry, js=js):
            r0 = pl.multiple_of(k * SUBLANES, SUBLANES)
            out = []
            for idx, j in enumerate(js):
                hr, hi = carry[2 * idx], carry[2 * idx + 1]
                base = j * 2 * S
                ar, ai = a_tabs[j]
                nhr = ar * hr - ai * hi + x_scr[pl.ds(r0, SUBLANES), base:base + S]
                nhi = ar * hi + ai * hr + x_scr[pl.ds(r0, SUBLANES), base + S:base + 2 * S]
                if store:
                    x_scr[pl.ds(r0, SUBLANES), base:base + S] = nhr
                    x_scr[pl.ds(r0, SUBLANES), base + S:base + 2 * S] = nhi
                out += [nhr, nhi]
            return tuple(out)

        c0 = tuple(t for j in js for t in init[j])
        res = jax.lax.fori_loop(0, n // SUBLANES, body, c0, unroll=True)
        fins += [(res[0], res[1]), (res[2], res[3])]
    return fins


def _ssm_rows(u, pow_row, pw_ref, bblk_ref, cblk_ref, x_scr, s_scr, carry_scr):
    n = u.shape[0]
    S = SSM_BLOCK_STATES
    ub = u.astype(BF16)
    for j in range(SSM_LANE_BLOCKS):
        x_scr[0:n, j * 2 * S:(j + 1) * 2 * S] = jnp.dot(ub[:, j * LANES:(j + 1) * LANES], bblk_ref[j],
                                                       preferred_element_type=F32)
    bc = lambda t: jnp.broadcast_to(t, (SUBLANES, S))
    re = lambda ref, r0, r1, j: ref[r0:r1, j * 2 * S:j * 2 * S + S]
    im = lambda ref, r0, r1, j: ref[r0:r1, j * 2 * S + S:(j + 1) * 2 * S]
    a_tabs = [(bc(re(pw_ref, 0, 1, j)), bc(im(pw_ref, 0, 1, j))) for j in range(SSM_LANE_BLOCKS)]
    zero = jnp.zeros((SUBLANES, S), F32)
    fins = _ssm_sweep(x_scr, n, a_tabs, [(zero, zero)] * SSM_LANE_BLOCKS, store=False)
    for j in range(SSM_LANE_BLOCKS):
        base = j * 2 * S
        cr, ci = re(pw_ref, pow_row, pow_row + 1, j), im(pw_ref, pow_row, pow_row + 1, j)
        sr, si = re(carry_scr, 0, 1, j), im(carry_scr, 0, 1, j)
        fr, fi = fins[j]
        for c in range(SUBLANES):
            s_scr[c:c + 1, base:base + S] = sr
            s_scr[c:c + 1, base + S:base + 2 * S] = si
            sr, si = cr * sr - ci * si + fr[c:c + 1], cr * si + ci * sr + fi[c:c + 1]
        carry_scr[:, base:base + S] = bc(sr)
        carry_scr[:, base + S:base + 2 * S] = bc(si)
    init = [(re(s_scr, 0, SUBLANES, j), im(s_scr, 0, SUBLANES, j)) for j in range(SSM_LANE_BLOCKS)]
    _ssm_sweep(x_scr, n, a_tabs, init, store=True)
    ys = [jnp.dot(x_scr[0:n, j * 2 * S:(j + 1) * 2 * S].astype(BF16), cblk_ref[j], preferred_element_type=F32)
          for j in range(SSM_LANE_BLOCKS)]
    return jnp.concatenate(ys, axis=1)


def _ssm_tile(u_ref, n, pow_row, refs, o_ref, scr):
    pw_ref, bblk_ref, cblk_ref, d_ref, wg_ref, bg_ref, g_ref = refs
    up_scr, x_scr, s_scr, carry_scr, o_scr = scr
    q = n // SUBLANES
    for j in range(SSM_LANE_BLOCKS):
        for k in range(q):
            up_scr[k * SUBLANES:(k + 1) * SUBLANES, j * LANES:(j + 1) * LANES] = \
                u_ref[j, pl.ds(k, SUBLANES, stride=q), :]
    u = up_scr[0:n, :]
    y = _ssm_rows(u, pow_row, pw_ref, bblk_ref, cblk_ref, x_scr, s_scr, carry_scr)
    out = _ssm_tail(y, u, d_ref, wg_ref, bg_ref, g_ref)
    for j in range(SSM_LANE_BLOCKS):
        for k in range(q):
            o_scr[j, pl.ds(k, SUBLANES, stride=q), :] = out[k * SUBLANES:(k + 1) * SUBLANES,
                                                            j * LANES:(j + 1) * LANES]
    o_ref[...] = jnp.concatenate([o_scr[j, 0:n, :] for j in range(SSM_LANE_BLOCKS)], axis=1).astype(o_ref.dtype)


def _mixers_kernel(sink_ref,
                   qm_ref, qh_ref, kpm_ref, km_ref, kh_ref, vpm_ref, vm_ref, vh_ref, ga_ref, bias_ref,
                   um_ref, uh_ref, pw_ref, bblk_ref, cblk_ref, d_ref, wg_ref, bg_ref, gs_ref,
                   xm_ref, halo_ref, xh_ref, wp_ref, sc_ref, gp_ref, *rest, l, n_cast, n_dec):
    cast_in = rest[0:n_cast]
    am_ref, ah_ref, som_ref, soh_ref, st_ref, pom_ref, poh_ref = rest[n_cast:n_cast + 7]
    cast_out = rest[n_cast + 7:2 * n_cast + 7]
    scr = rest[2 * n_cast + 7:]
    carry_scr = scr[3]
    b = pl.program_id(0)
    t = pl.program_id(1)
    tm = xm_ref.shape[0]
    ssm_refs = (pw_ref, bblk_ref, cblk_ref, d_ref, wg_ref, bg_ref, gs_ref)
    blk = functools.partial(_attn_block, sink_ref=sink_ref, l=l, g=ga_ref[...])
    meta_rows = pl.ds(pl.multiple_of(n_dec + b * N_META, N_META), N_META)

    @pl.when((b == 0) & (t == 0))
    def _():
        for ref in (ah_ref, soh_ref, poh_ref):
            ref[0:n_dec, :] = jnp.zeros((n_dec, ref.shape[1]), ref.dtype)

    block = lambda ref: jnp.concatenate([jnp.zeros((SEQ0, ref.shape[1]), F32), ref[...]], axis=0)

    @pl.when(t == 0)
    def _():
        a = blk(block(qh_ref), kpm_ref[...], block(kh_ref), vpm_ref[...], block(vh_ref), bias_ref[0])
        ah_ref[meta_rows, :] = a[SEQ0:].astype(ah_ref.dtype)
        carry_scr[...] = jnp.zeros_like(carry_scr)
        _ssm_tile(uh_ref, N_META, POW_ROW_HEAD, ssm_refs, soh_ref.at[meta_rows, :], scr)
        prev = jnp.zeros((POOL_HALO, POOL_WIDTH), F32)
        poh_ref[meta_rows, :] = _pool_rows(xh_ref[...], prev, 0, wp_ref, sc_ref, gp_ref).astype(poh_ref.dtype)

    @pl.when(t > 0)
    def _():
        for src, dst in zip(cast_in, cast_out):
            dst[...] = src[...].astype(BF16)
        kp = jnp.where(t == 1, block(kh_ref), kpm_ref[...])
        vp = jnp.where(t == 1, block(vh_ref), vpm_ref[...])
        for n in range(tm // BLOCK):
            rows = slice(n * BLOCK, (n + 1) * BLOCK)
            kc, vc = km_ref[rows, :], vm_ref[rows, :]
            bias = bias_ref[jnp.minimum(t, 2)] if n == 0 else bias_ref[2]
            am_ref[rows, :] = blk(qm_ref[rows, :], kp, kc, vp, vc, bias).astype(am_ref.dtype)
            kp, vp = kc, vc
        _ssm_tile(um_ref, um_ref.shape[1], POW_ROW_MAIN, ssm_refs, som_ref, scr)
        st_ref[...] = carry_scr[...]
        prev = jnp.where(t == 1, xh_ref[...], halo_ref[...])
        pom_ref[...] = _pool_rows(xm_ref[...], prev, N_META + (t - 1) * tm, wp_ref, sc_ref, gp_ref).astype(pom_ref.dtype)


def _mixers(sinks, qkv_m, qkv_h, um, uh, xm, xh, vecs, bias, pw, bblk, cblk, wg, wp, l, B, tm, casts):
    assert (1, N_META // SUBLANES, tm // SUBLANES) == SSM_POW_ROWS and N_META == POOL_HALO
    J = SSM_LANE_BLOCKS
    rm, rh = um.shape[1], uh.shape[1]
    nt = rm // (B * tm)
    r = tm // POOL_HALO
    bpt = tm // BLOCK
    tile = lambda b, t: b * nt + jnp.maximum(t - 1, 0)
    main = lambda b, t, _: (tile(b, t), 0)
    n_dec = uh.shape[1] - B * N_META
    head = lambda b, t, _: (n_dec // N_META + b, 0)
    halo = lambda b, t, _: (jnp.maximum((b * nt + t - 1) * r - 1, 0), 0)
    prevb = lambda b, t, _: (jnp.maximum(tile(b, t) * bpt - 1, 0), 0)
    cast_specs = []
    for w, wl in casts:
        _, cr, cc = w.shape
        rows = cr // (B * nt)
        cast_specs.append((pl.BlockSpec((None, rows, cc), lambda b, t, _, wl=wl: (wl, tile(b, t), 0)),
                           pl.BlockSpec((rows, cc), main), jax.ShapeDtypeStruct((cr, cc), BF16)))
    bf = lambda n, w: jax.ShapeDtypeStruct((n, w), BF16)
    tile_spec = lambda w: pl.BlockSpec((tm, w), main)
    head_spec = lambda w: pl.BlockSpec((N_META, w), head)
    in_specs = [tile_spec(ATTN_WIDTH), head_spec(ATTN_WIDTH),
                pl.BlockSpec((BLOCK, KV_WIDTH), prevb), tile_spec(KV_WIDTH), head_spec(KV_WIDTH),
                pl.BlockSpec((BLOCK, KV_WIDTH), prevb), tile_spec(KV_WIDTH), head_spec(KV_WIDTH),
                _vrow("g_out_attn", l), _full(bias.shape),
                pl.BlockSpec((J, tm, LANES), lambda b, t, _: (0, tile(b, t), 0)),
                pl.BlockSpec((J, N_META, LANES), lambda b, t, _: (0, n_dec // N_META + b, 0)),
                _layer(pw.shape[1:], l), _layer(bblk.shape[1:], l), _layer(cblk.shape[1:], l),
                _vrow("D_skip", l), _layer((SSM_WIDTH, SSM_WIDTH), l), _vrow("b_glu", l), _vrow("g_out_ssm", l),
                tile_spec(POOL_WIDTH), pl.BlockSpec((POOL_HALO, POOL_WIDTH), halo), head_spec(POOL_WIDTH),
                _layer(wp.shape[1:], l), _vrow("pool_scale", l), _vrow("g_out_pool", l)]
    in_specs += [c[0] for c in cast_specs]
    head_out = lambda w: pl.BlockSpec((rh, w), lambda b, t, _: (0, 0))
    out_specs = [tile_spec(ATTN_WIDTH), head_out(ATTN_WIDTH), tile_spec(SSM_WIDTH), head_out(SSM_WIDTH),
                 pl.BlockSpec((None, SUBLANES, SSM_STATE_LANES), lambda b, t, _: (b, 0, 0)),
                 tile_spec(POOL_WIDTH), head_out(POOL_WIDTH)] + [c[1] for c in cast_specs]
    out_shape = [bf(rm, ATTN_WIDTH), bf(rh, ATTN_WIDTH), bf(rm, SSM_WIDTH), bf(rh, SSM_WIDTH),
                 jax.ShapeDtypeStruct((B, SUBLANES, SSM_STATE_LANES), F32),
                 bf(rm, POOL_WIDTH), bf(rh, POOL_WIDTH)] + [c[2] for c in cast_specs]
    (qm, km, vm), (qh, kh, vh) = qkv_m, qkv_h
    res = pl.pallas_call(
        functools.partial(_mixers_kernel, l=l, n_cast=len(casts), n_dec=n_dec),
        grid_spec=pltpu.PrefetchScalarGridSpec(
            num_scalar_prefetch=1, grid=(B, nt + 1), in_specs=in_specs, out_specs=out_specs,
            scratch_shapes=[pltpu.VMEM((tm, SSM_WIDTH), F32),
                            pltpu.VMEM((tm, SSM_STATE_LANES), F32),
                            pltpu.VMEM((SUBLANES, SSM_STATE_LANES), F32),
                            pltpu.VMEM((SUBLANES, SSM_STATE_LANES), F32),
                            pltpu.VMEM((J, tm, LANES), F32)]),
        out_shape=out_shape,
        compiler_params=_params("arbitrary", "arbitrary"),
        name="mixers",
    )(sinks, qm, qh, km, km, kh, vm, vm, vh, vecs, bias, um, uh, pw, bblk, cblk, vecs, wg, vecs, vecs,
      xm, xm, xh, wp, vecs, vecs, *[w for w, _ in casts])
    return res[0:7], res[7:]


def _state_from_lanes(s):
    s = s.reshape(s.shape[0], SSM_LANE_BLOCKS, 2, SSM_BLOCK_STATES)
    return (s[:, :, 0].reshape(-1, SSM_GROUPS, SSM_STATE), s[:, :, 1].reshape(-1, SSM_GROUPS, SSM_STATE))


def _pool_tail(d_groups, w_ref, sc_ref, g_ref):
    y = jnp.concatenate(
        [jnp.dot(d.astype(BF16), w_ref[gi], preferred_element_type=F32) for gi, d in enumerate(d_groups)], axis=1)
    return _rms(y * sc_ref[...], g_ref[...])


def _pool_rows(x, prev, pos0, w_ref, sc_ref, g_ref):
    n = x.shape[0]
    xe = jnp.concatenate([prev, x], axis=0)
    pos = pos0 + jax.lax.broadcasted_iota(jnp.int32, (n, 1), 0)
    ds = []
    for gi, w in enumerate(POOL_WINDOWS):
        gsl = slice(gi * POOL_GROUP, (gi + 1) * POOL_GROUP)
        s = xe[:, gsl]
        k = 1
        while k < w:
            s = s + pltpu.roll(s, k, 0)
            k *= 2
        cnt = jnp.clip(pos + 1, 1, w).astype(F32)
        ds.append(s[POOL_HALO:] / cnt - x[:, gsl])
    return _pool_tail(ds, w_ref, sc_ref, g_ref)


def _attn_sample_kernel(q_ref, kn_ref, vn_ref, kc_ref, vc_ref, sink_ref, g_ref, *rest, first):
    a_ref, nk_ref, nv_ref, acc_scr = rest[-4:]
    step = pl.program_id(1 if first else 0)
    nsteps = pl.num_programs(1 if first else 0)
    slot = WINDOW - 1 if first else 0
    scale = HEAD_DIM ** -0.5
    if first:
        for bb in range(DEC_STEP):
            nk_ref[bb, 0:WINDOW - 1] = kc_ref[bb, 1:WINDOW]
            nv_ref[bb, 0:WINDOW - 1] = vc_ref[bb, 1:WINDOW]
            nk_ref[bb, WINDOW - 1] = jnp.zeros((N_KV_HEADS, HEAD_DIM), F32)
            nv_ref[bb, WINDOW - 1] = jnp.zeros((N_KV_HEADS, HEAD_DIM), F32)

    def attend():
        _attn_sample_rows(q_ref, kn_ref, vn_ref, kc_ref, vc_ref, sink_ref, g_ref, nk_ref, nv_ref, acc_scr,
                          step, slot, scale)

        @pl.when(step == nsteps - 1)
        def _():
            a_ref[...] = acc_scr[...].astype(a_ref.dtype)

    if first:
        pl.when(pl.program_id(0) == 0)(attend)
    else:
        attend()


def _attn_sample_rows(q_ref, kn_ref, vn_ref, kc_ref, vc_ref, sink_ref, g_ref, nk_ref, nv_ref, acc_scr,
                      step, slot, scale):
    for bb in range(DEC_STEP):
        outs = []
        for kh in range(N_KV_HEADS):
            ksl = slice(kh * HEAD_DIM, (kh + 1) * HEAD_DIM)
            qh = jnp.concatenate(
                [q_ref[bb:bb + 1, (kh * GQA_GROUP + g) * HEAD_DIM:(kh * GQA_GROUP + g + 1) * HEAD_DIM]
                 for g in range(GQA_GROUP)], axis=0)
            kn = kn_ref[bb:bb + 1, ksl]
            vn = vn_ref[bb:bb + 1, ksl]
            nk_ref[bb, slot, kh:kh + 1, :] = kn
            nv_ref[bb, slot, kh:kh + 1, :] = vn
            kc = kc_ref[bb, :, kh, :]
            vc = vc_ref[bb, :, kh, :]
            sc = jax.lax.dot_general(qh.astype(BF16), kc.astype(BF16), (((1,), (1,)), ((), ())),
                                     preferred_element_type=F32) * scale
            sn = jnp.sum(qh * kn, axis=-1, keepdims=True) * scale
            sk = sink_ref[kh]
            m = jnp.maximum(jnp.maximum(jnp.max(sc, axis=-1, keepdims=True), sn), sk)
            pc = jnp.exp(sc - m)
            pn = jnp.exp(sn - m)
            denom = jnp.sum(pc, axis=-1, keepdims=True) + pn + jnp.exp(sk - m)
            o = jnp.dot(pc.astype(BF16), vc.astype(BF16), preferred_element_type=F32)
            o = (o + pn * vn) / denom
            outs.extend(o[g:g + 1] for g in range(GQA_GROUP))
        a = jnp.concatenate(outs, axis=1)
        acc_scr[pl.ds(step * DEC_STEP + bb, 1), :] = _rms(a, g_ref[...])


def _attn_sample(qh, kh, vh, cache_k, cache_v, l, sinks, g, ah, nk, nv, N):
    depth = cache_k.shape[0]
    first = nk is None
    anyspec = pl.BlockSpec(memory_space=pl.ANY)
    cshape = jax.ShapeDtypeStruct(cache_k.shape, F32)
    slab = (None, DEC_STEP, WINDOW, N_KV_HEADS, HEAD_DIM)
    if first:
        assert l == 0
        grid = (depth, N // DEC_STEP)
        rows = lambda w: pl.BlockSpec((DEC_STEP, w), lambda ll, s: (s, 0))
        cache = pl.BlockSpec(slab, lambda ll, s: (ll, s, 0, 0, 0))
        new = cache
        a_spec = pl.BlockSpec((N, ATTN_WIDTH), lambda ll, s: (0, 0))
        carried = []
    else:
        grid = (N // DEC_STEP,)
        rows = lambda w: pl.BlockSpec((DEC_STEP, w), lambda s: (s, 0))
        cache = pl.BlockSpec(slab, lambda s: (l, s, 0, 0, 0))
        new = pl.BlockSpec((None, DEC_STEP, 1, N_KV_HEADS, HEAD_DIM), lambda s: (l, s, WINDOW - 1, 0, 0))
        a_spec = pl.BlockSpec((N, ATTN_WIDTH), lambda s: (0, 0))
        carried = [nk, nv]
    return pl.pallas_call(
        functools.partial(_attn_sample_kernel, first=first),
        grid=grid,
        in_specs=[rows(ATTN_WIDTH), rows(KV_WIDTH), rows(KV_WIDTH), cache, cache,
                  _layer((N_KV_HEADS, GQA_GROUP, 1), l), _vrow("g_out_attn", l)] + [anyspec] * (1 + len(carried)),
        out_specs=[a_spec, new, new],
        out_shape=[jax.ShapeDtypeStruct(ah.shape, ah.dtype), cshape, cshape],
        scratch_shapes=[pltpu.VMEM((N, ATTN_WIDTH), F32)],
        input_output_aliases={7 + i: i for i in range(1 + len(carried))},
        compiler_params=_params(*(["arbitrary"] * len(grid))),
        name="attn_sample",
    )(qh, kh, vh, cache_k, cache_v, sinks, g, ah, *carried)


def _mix_sample_kernel(u_ref, h0_ref, pw_ref, bblk_ref, cblk_ref, d_ref, wg_ref, bg_ref, gs_ref,
                       xp_ref, pb_ref, wp_ref, sc_ref, gp_ref, s_in_ref, p_in_ref, s_ref, st_ref, p_ref):
    del s_in_ref, p_in_ref
    S = SSM_BLOCK_STATES
    u = jnp.concatenate([u_ref[j] for j in range(SSM_LANE_BLOCKS)], axis=1)
    ub = u.astype(BF16)
    ys = []
    for j in range(SSM_LANE_BLOCKS):
        x = jnp.dot(ub[:, j * LANES:(j + 1) * LANES], bblk_ref[j], preferred_element_type=F32)
        base = j * 2 * S
        ar = pw_ref[0:1, base:base + S]
        ai = pw_ref[0:1, base + S:base + 2 * S]
        h0r = h0_ref[:, base:base + S]
        h0i = h0_ref[:, base + S:base + 2 * S]
        hr = x[:, 0:S] + ar * h0r - ai * h0i
        hi = x[:, S:] + ar * h0i + ai * h0r
        st_ref[:, base:base + S] = hr
        st_ref[:, base + S:base + 2 * S] = hi
        h = jnp.concatenate([hr, hi], axis=1).astype(BF16)
        ys.append(jnp.dot(h, cblk_ref[j], preferred_element_type=F32))
    s_ref[...] = _ssm_tail(jnp.concatenate(ys, axis=1), u, d_ref, wg_ref, bg_ref, gs_ref).astype(s_ref.dtype)

    xp = xp_ref[...]
    ds = []
    for gi, w in enumerate(POOL_WINDOWS):
        gsl = slice(gi * POOL_GROUP, (gi + 1) * POOL_GROUP)
        s = xp[:, gsl]
        for back in range(1, w):
            s = s + pb_ref[POOL_BUF - back][:, gsl]
        ds.append(s / float(w) - xp[:, gsl])
    p_ref[...] = _pool_tail(ds, wp_ref, sc_ref, gp_ref).astype(p_ref.dtype)


def _mix_sample(uh, xph, h0, pbuf, vecs, pw, bblk, cblk, wg, wp, l, sh, ph):
    N = h0.shape[1]
    rows = lambda w: pl.BlockSpec((N, w), lambda i: (0, 0))
    anyspec = pl.BlockSpec(memory_space=pl.ANY)
    return pl.pallas_call(
        _mix_sample_kernel,
        grid=(1,),
        in_specs=[pl.BlockSpec((SSM_LANE_BLOCKS, N, LANES), lambda i: (0, 0, 0)),
                  _layer(h0.shape[1:], l), _layer(pw.shape[1:], l), _layer(bblk.shape[1:], l),
                  _layer(cblk.shape[1:], l),
                  _vrow("D_skip", l), _layer((SSM_WIDTH, SSM_WIDTH), l), _vrow("b_glu", l),
                  _vrow("g_out_ssm", l), rows(POOL_WIDTH), _layer(pbuf.shape[1:], l),
                  _layer(wp.shape[1:], l),
                  _vrow("pool_scale", l), _vrow("g_out_pool", l), anyspec, anyspec],
        out_specs=[rows(SSM_WIDTH), _full((N, SSM_STATE_LANES)), rows(POOL_WIDTH)],
        out_shape=[jax.ShapeDtypeStruct(sh.shape, sh.dtype), jax.ShapeDtypeStruct((N, SSM_STATE_LANES), F32),
                   jax.ShapeDtypeStruct(ph.shape, ph.dtype)],
        input_output_aliases={14: 0, 15: 2},
        compiler_params=_params("arbitrary"),
        name="mix_sample",
    )(uh, h0, pw, bblk, cblk, vecs, wg, vecs, vecs, xph, pbuf, wp, vecs, vecs, sh, ph)


def _outproj_rows(x_ref, a_ref, s_ref, p_ref, w_ref, o_ref):
    o1 = ATTN_WIDTH
    o2 = o1 + SSM_WIDTH
    acc = x_ref[...]
    acc = acc + jnp.dot(a_ref[...], w_ref[0:o1, :], preferred_element_type=F32)
    acc = acc + jnp.dot(s_ref[...], w_ref[o1:o2, :], preferred_element_type=F32)
    acc = acc + jnp.dot(p_ref[...], w_ref[o2:, :], preferred_element_type=F32)
    o_ref[...] = acc


def _outproj_kernel(xm_ref, am_ref, sm_ref, pm_ref, xh_ref, ah_ref, sh_ref, ph_ref, w_ref, om_ref, oh_ref):
    i = pl.program_id(0)
    last = pl.num_programs(0) - 1

    @pl.when(i < last)
    def _():
        _outproj_rows(xm_ref, am_ref, sm_ref, pm_ref, w_ref, om_ref)

    @pl.when(i == last)
    def _():
        _outproj_rows(xh_ref, ah_ref, sh_ref, ph_ref, w_ref, oh_ref)


def _outproj(xm, am, sm, pm, xh, ah, sh, ph, w, tm):
    R, RH = xm.shape[0], xh.shape[0]
    nm = R // tm
    row = lambda i: (jnp.minimum(i, nm - 1), 0)
    once = pl.Buffered(1)
    head = lambda w_: pl.BlockSpec((RH, w_), lambda i: (0, 0), pipeline_mode=once)
    return pl.pallas_call(
        _outproj_kernel,
        grid=(nm + 1,),
        in_specs=[pl.BlockSpec((tm, D_MODEL), row), pl.BlockSpec((tm, ATTN_WIDTH), row),
                  pl.BlockSpec((tm, SSM_WIDTH), row), pl.BlockSpec((tm, POOL_WIDTH), row),
                  head(D_MODEL), head(ATTN_WIDTH), head(SSM_WIDTH), head(POOL_WIDTH),
                  pl.BlockSpec((D_MODEL, D_MODEL), lambda i: (0, 0), pipeline_mode=once)],
        out_specs=[pl.BlockSpec((tm, D_MODEL), row), pl.BlockSpec((RH, D_MODEL), lambda i: (0, 0))],
        out_shape=[jax.ShapeDtypeStruct((R, D_MODEL), F32), jax.ShapeDtypeStruct((RH, D_MODEL), F32)],
        compiler_params=_params("arbitrary"),
        name="outproj",
    )(xm, am, sm, pm, xh, ah, sh, ph, w)


def _ffn_kernel(xm_ref, xh_ref, g_ref, w1_ref, w2_ref, om_ref, oh_ref, hm_scr, hh_scr):
    i = pl.program_id(0)
    f = pl.program_id(1)

    def mlp(h):
        h1 = jnp.dot(h, w1_ref[...], preferred_element_type=F32)
        return jnp.dot(jnp.square(jnp.maximum(h1, 0.0)).astype(BF16), w2_ref[...], preferred_element_type=F32)

    def first(x_ref, h_scr, o_ref):
        x = x_ref[...]
        h = _rms(x, g_ref[...]).astype(BF16)
        h_scr[...] = h
        o_ref[...] = x + mlp(h)

    @pl.when(f == 0)
    def _():
        first(xm_ref, hm_scr, om_ref)

    @pl.when(f > 0)
    def _():
        om_ref[...] += mlp(hm_scr[...])

    @pl.when((i == 0) & (f == 0))
    def _():
        first(xh_ref, hh_scr, oh_ref)

    @pl.when((i == 0) & (f > 0))
    def _():
        oh_ref[...] += mlp(hh_scr[...])


def _ffn(xm, xh, g, l, w1, w2, tm, tf):
    R, RH = xm.shape[0], xh.shape[0]
    return pl.pallas_call(
        _ffn_kernel,
        grid=(R // tm, D_FF // tf),
        in_specs=[pl.BlockSpec((tm, D_MODEL), lambda i, f: (i, 0)),
                  pl.BlockSpec((RH, D_MODEL), lambda i, f: (0, 0), pipeline_mode=pl.Buffered(1)),
                  _vrow("g_ffn", l),
                  pl.BlockSpec((D_MODEL, tf), lambda i, f: (0, f)),
                  pl.BlockSpec((tf, D_MODEL), lambda i, f: (f, 0))],
        out_specs=[pl.BlockSpec((tm, D_MODEL), lambda i, f: (i, 0)),
                   pl.BlockSpec((RH, D_MODEL), lambda i, f: (0, 0))],
        out_shape=[jax.ShapeDtypeStruct((R, D_MODEL), F32), jax.ShapeDtypeStruct((RH, D_MODEL), F32)],
        scratch_shapes=[pltpu.VMEM((tm, D_MODEL), BF16), pltpu.VMEM((RH, D_MODEL), BF16)],
        compiler_params=_params("arbitrary", "arbitrary"),
        name="ffn",
    )(xm, xh, g, w1, w2)


def _rope_tables(pos):
    half = ROT_HALF
    inv = ROPE_THETA ** (-np.arange(0, ROT_DIM, 2, dtype=np.float64) / ROT_DIM)
    ang = np.asarray(pos, np.float64)[:, None] * inv
    cos, sin = np.cos(ang), np.sin(ang)
    n = ang.shape[0]
    z = np.zeros((n, HEAD_DIM - ROT_DIM))
    zh = np.zeros((n, half))
    rc = np.concatenate([cos, cos, z + 1.0], axis=1)
    rs1 = np.concatenate([zh, sin, z], axis=1)
    rs2 = np.concatenate([-sin, zh, z], axis=1)
    return tuple(t.astype(np.float32) for t in (rc, rs1, rs2))


def kernel(x_prompt, x_sample, cache_k, cache_v, state_ssm_re, state_ssm_im, state_pool, meta_tokens, g_mix, w_in, g_q, g_k, sinks, A_re, A_im, log_dt, B_re, B_im, C_re, C_im, D_skip, w_glu, b_glu, w_pool, pool_scale, g_out_attn, g_out_ssm, g_out_pool, w_out, g_ffn, w_ff1, w_ff2):
    B, T, _ = x_prompt.shape
    N = x_sample.shape[0]
    depth = w_in.shape[0]
    assert N % N_META == 0 and N % DEC_STEP == 0 and T % TM_FFN == 0
    xh = jnp.concatenate([x_sample.reshape(N, D_MODEL)] + [meta_tokens.astype(F32)] * B, axis=0)
    xm = x_prompt.reshape(B * T, D_MODEL)

    rope_m = _rope_tables(N_META + np.arange(T))
    rope_h = _rope_tables(np.concatenate([np.full(N, PAST_LEN)] + [np.arange(N_META)] * B))

    wi = w_in[0].astype(BF16)
    wg_all, wp_all = w_glu.astype(BF16), w_pool.astype(BF16)
    ck = cache_k.astype(F32)
    cv = cache_v.astype(F32)

    named = dict(g_mix=g_mix, g_ffn=g_ffn, g_out_attn=g_out_attn, D_skip=D_skip, b_glu=b_glu,
                 g_out_ssm=g_out_ssm, pool_scale=pool_scale, g_out_pool=g_out_pool, g_q=g_q, g_k=g_k)
    vecs = jnp.concatenate([named[name].astype(F32) for name, _ in VEC_LAYOUT], axis=1)[:, None, :]
    pw, bblk, cblk = _ssm_params(A_re, A_im, log_dt, B_re, B_im, C_re, C_im)
    sinks_flat = sinks.astype(F32).reshape(depth * N_HEADS)
    sinks_col = sinks.astype(F32).reshape(depth, N_KV_HEADS, GQA_GROUP, 1)
    bias = _attn_bias()
    h0 = jnp.concatenate([state_ssm_re.astype(F32).reshape(depth, N, SSM_LANE_BLOCKS, SSM_BLOCK_STATES),
                          state_ssm_im.astype(F32).reshape(depth, N, SSM_LANE_BLOCKS, SSM_BLOCK_STATES)],
                         axis=-1).reshape(depth, N, SSM_STATE_LANES)
    pbuf = state_pool.astype(F32).transpose(0, 2, 1, 3)

    nk = nv = None
    ks, vs, pls, sts, st_ss, phs = ([] for _ in range(6))
    for l in range(depth):
        (qm, km, vm, um, pm), (qh, kh, vh, uh, ph), w1 = _inproj(xm, xh, vecs, wi, l, rope_m, rope_h, TM_PROJ, w_ff1)

        (am, ah, sm, sh, st, plm, plh), wcast = _mixers(
            sinks_flat, (qm, km, vm), (qh, kh, vh), um, uh, pm, ph, vecs, bias, pw, bblk, cblk, wg_all, wp_all,
            l, B, TM_SEQ, [(w_ff2, l), (w_out, l)] + ([(w_in, l + 1)] if l + 1 < depth else []))
        w2, wo = wcast[0], wcast[1]

        ah, nk, nv = _attn_sample(qh, kh, vh, ck, cv, l, sinks_col, vecs, ah, nk, nv, N)
        sh, st_s, plh = _mix_sample(uh, ph, h0, pbuf, vecs, pw, bblk, cblk, wg_all, wp_all, l, sh, plh)

        xm, xh = _outproj(xm, am, sm, plm, xh, ah, sh, plh, wo, TM_PROJ)
        xm, xh = _ffn(xm, xh, vecs, l, w1, w2, TM_FFN, TF_FFN)
        if l + 1 < depth:
            wi = wcast[2]

        ks.append(km.reshape(B, T, KV_WIDTH)[:, T - WINDOW:])
        vs.append(vm.reshape(B, T, KV_WIDTH)[:, T - WINDOW:])
        pls.append(pm.reshape(B, T, POOL_WIDTH)[:, T - POOL_BUF:])
        sts.append(st[:, 0])
        st_ss.append(st_s)
        phs.append(ph[:N])

    y_prompt = xm.reshape(B, T, D_MODEL)
    y_sample = xh[:N].reshape(N, 1, D_MODEL)
    heads = lambda t: jnp.stack(t).reshape(depth, -1, WINDOW, N_KV_HEADS, HEAD_DIM)
    p_re, p_im = _state_from_lanes(jnp.stack(sts).reshape(depth * B, SSM_STATE_LANES))
    s_re, s_im = _state_from_lanes(jnp.stack(st_ss).reshape(depth * N, SSM_STATE_LANES))
    st4 = lambda t, n: t.reshape(depth, n, SSM_GROUPS, SSM_STATE)
    s_pool = jnp.concatenate([state_pool.astype(F32)[:, :, 1:], jnp.stack(phs)[:, :, None]], axis=2)
    return (y_prompt, y_sample, heads(ks), heads(vs), st4(p_re, B), st4(p_im, B), jnp.stack(pls),
            nk, nv, st4(s_re, N), st4(s_im, N), s_pool)
```

```python
import functools
import math

import jax
import jax.numpy as jnp
import numpy as np
from jax.experimental import pallas as pl
from jax.experimental.pallas import tpu as pltpu

D_MODEL = 2048
N_META = 16
HEAD_DIM = 128
N_HEADS = 8
N_KV_HEADS = 2
GQA_GROUP = 4
ATTN_WIDTH = 1024
KV_WIDTH = 256
WINDOW = 128
BLOCK = 128
ROT_DIM = 32
ROT_HALF = ROT_DIM // 2
ROPE_THETA = 500000.0
SSM_WIDTH = 512
SSM_GROUP_SIZE = 16
SSM_GROUPS = 32
SSM_STATE = 64
POOL_WIDTH = 512
POOL_WINDOWS = (2, 4, 8, 16)
POOL_GROUP = 128
POOL_BUF = 15
POOL_HALO = 16
IN_WIDTH = 2560
D_FF = 8192
EPS = 1e-6
PAST_LEN = 16384
LOG2E = math.log2(math.e)

SEQ0 = BLOCK - N_META
LANES = 128
SUBLANES = 8
SSM_LANE_BLOCKS = SSM_WIDTH // LANES
SSM_BLOCK_STATES = (LANES // SSM_GROUP_SIZE) * SSM_STATE
SSM_STATE_LANES = SSM_LANE_BLOCKS * 2 * SSM_BLOCK_STATES
VMEM_LIMIT = 60 * 1024 * 1024

TM_PROJ = 512
TM_FFN = 1024
TF_FFN = 512
TM_SEQ = 512
DEC_STEP = 8
SSM_POW_ROWS = (1, N_META // SUBLANES, TM_SEQ // SUBLANES)
POW_ROW_HEAD, POW_ROW_MAIN = 1, 2

BF16 = jnp.bfloat16
F32 = jnp.float32


def _params(*semantics):
    return pltpu.CompilerParams(dimension_semantics=semantics, vmem_limit_bytes=VMEM_LIMIT)


def _rms(x, g):
    return x * jax.lax.rsqrt(jnp.mean(x * x, axis=-1, keepdims=True) + EPS) * g


def _full(shape):
    n = len(shape)
    return pl.BlockSpec(shape, lambda *_: (0,) * n)


def _layer(shape, l):
    n = len(shape)
    return pl.BlockSpec((None, *shape), lambda *_: (l,) + (0,) * n)


VEC_LAYOUT = (("g_mix", D_MODEL), ("g_ffn", D_MODEL), ("g_out_attn", ATTN_WIDTH), ("D_skip", SSM_WIDTH),
              ("b_glu", SSM_WIDTH), ("g_out_ssm", SSM_WIDTH), ("pool_scale", POOL_WIDTH),
              ("g_out_pool", POOL_WIDTH), ("g_q", HEAD_DIM), ("g_k", HEAD_DIM))
VEC_WIDTH = dict(VEC_LAYOUT)
VEC_OFFSET = {name: sum(w for _, w in VEC_LAYOUT[:i]) for i, (name, _) in enumerate(VEC_LAYOUT)}
assert all(VEC_OFFSET[name] % w == 0 for name, w in VEC_LAYOUT)


def _vrow(name, l):
    w = VEC_WIDTH[name]
    return pl.BlockSpec((None, 1, w), lambda *_: (l, 0, VEC_OFFSET[name] // w))


def _inproj_rows(x, g_ref, w_ref, gq_ref, gk_ref, rope_refs, out_refs):
    q_ref, k_ref, v_ref, u_ref, xp_ref = out_refs
    h = _rms(x, g_ref[...]).astype(BF16)
    proj = jnp.dot(h, w_ref[...], preferred_element_type=F32)
    rc, rs1, rs2 = (r[...] for r in rope_refs)

    def head(t, g):
        t = _rms(t, g)
        return t * rc + pltpu.roll(t, ROT_HALF, 1) * rs1 + pltpu.roll(t, LANES - ROT_HALF, 1) * rs2

    for hd in range(N_HEADS):
        sl = slice(hd * HEAD_DIM, (hd + 1) * HEAD_DIM)
        q_ref[:, sl] = head(proj[:, sl], gq_ref[...])
    for hd in range(N_KV_HEADS):
        sl = slice(hd * HEAD_DIM, (hd + 1) * HEAD_DIM)
        k_ref[:, sl] = head(proj[:, ATTN_WIDTH + hd * HEAD_DIM:ATTN_WIDTH + (hd + 1) * HEAD_DIM], gk_ref[...])
    o2 = ATTN_WIDTH + KV_WIDTH
    o3 = o2 + KV_WIDTH
    o4 = o3 + SSM_WIDTH
    v_ref[...] = proj[:, o2:o3]
    for j in range(SSM_LANE_BLOCKS):
        u_ref[j] = proj[:, o3 + j * LANES:o3 + (j + 1) * LANES]
    xp_ref[...] = proj[:, o4:]


def _inproj_kernel(xm_ref, xh_ref, g_ref, w_ref, gq_ref, gk_ref, rcm_ref, rs1m_ref, rs2m_ref,
                   rch_ref, rs1h_ref, rs2h_ref, *rest):
    cast = len(rest) == 12
    outs = rest[1:] if cast else rest
    main_outs, head_outs = outs[0:5], outs[5:10]
    i = pl.program_id(0)
    last = pl.num_programs(0) - 1

    @pl.when(i < last)
    def _():
        if cast:
            outs[10][...] = rest[0][...].astype(BF16)
        _inproj_rows(xm_ref[...], g_ref, w_ref, gq_ref, gk_ref, (rcm_ref, rs1m_ref, rs2m_ref), main_outs)

    @pl.when(i == last)
    def _():
        _inproj_rows(xh_ref[...], g_ref, w_ref, gq_ref, gk_ref, (rch_ref, rs1h_ref, rs2h_ref), head_outs)


def _inproj(xm, xh, vecs, w, l, rope_m, rope_h, tm, cast=None):
    R, RH = xm.shape[0], xh.shape[0]
    nm = R // tm
    tiles_per_rope = rope_m[0].shape[0] // tm
    tile = lambda i: jnp.minimum(i, nm - 1)
    row = lambda i: (tile(i), 0)
    rrow = lambda i: (tile(i) % tiles_per_rope, 0)
    once = pl.Buffered(1)
    full = lambda n, w_: pl.BlockSpec((n, w_), lambda i: (0, 0))

    def out_set(n, rows, im, uim):
        flat = lambda w_: (pl.BlockSpec((n, w_), im), jax.ShapeDtypeStruct((rows, w_), F32))
        u_out = (pl.BlockSpec((SSM_LANE_BLOCKS, n, LANES), uim),
                 jax.ShapeDtypeStruct((SSM_LANE_BLOCKS, rows, LANES), F32))
        return [flat(ATTN_WIDTH), flat(KV_WIDTH), flat(KV_WIDTH), u_out, flat(POOL_WIDTH)]

    outs = (out_set(tm, R, row, lambda i: (0, tile(i), 0))
            + out_set(RH, RH, lambda i: (0, 0), lambda i: (0, 0, 0)))
    in_specs = [pl.BlockSpec((tm, D_MODEL), row),
                pl.BlockSpec((RH, D_MODEL), lambda i: (0, 0), pipeline_mode=once),
                _vrow("g_mix", l),
                pl.BlockSpec((D_MODEL, IN_WIDTH), lambda i: (0, 0), pipeline_mode=once),
                _vrow("g_q", l), _vrow("g_k", l),
                pl.BlockSpec((tm, LANES), rrow), pl.BlockSpec((tm, LANES), rrow), pl.BlockSpec((tm, LANES), rrow),
                full(RH, LANES), full(RH, LANES), full(RH, LANES)]
    args = [xm, xh, vecs, w, vecs, vecs, *rope_m, *rope_h]
    if cast is not None:
        _, cr, cc = cast.shape
        crows = cr // nm
        in_specs.append(pl.BlockSpec((None, crows, cc), lambda i: (l, tile(i), 0)))
        args.append(cast)
        outs.append((pl.BlockSpec((crows, cc), row), jax.ShapeDtypeStruct((cr, cc), BF16)))
    res = pl.pallas_call(
        _inproj_kernel,
        grid=(nm + 1,),
        in_specs=in_specs,
        out_specs=[o[0] for o in outs],
        out_shape=[o[1] for o in outs],
        compiler_params=_params("arbitrary"),
        name="inproj",
    )(*args)
    return res[0:5], res[5:10], (res[10] if cast is not None else None)


def _attn_bias():
    rows = GQA_GROUP * BLOCK
    i, r, c = np.meshgrid(np.arange(3), np.arange(rows) % BLOCK, np.arange(2 * BLOCK), indexing="ij")
    diff = BLOCK + r - c
    krow = (i - 1) * BLOCK + c
    mask = (diff >= 0) & (diff <= WINDOW) & (krow >= SEQ0)
    return np.where(mask, 0.0, -np.inf).astype(np.float32)


def _attn_block(q_blk, kp_blk, kc_blk, vp_blk, vc_blk, bias, sink_ref, l, g):
    rows = GQA_GROUP * BLOCK
    rgrp = jax.lax.broadcasted_iota(jnp.int32, (rows, 1), 0) // BLOCK
    outs = []
    for kh in range(N_KV_HEADS):
        ksl = slice(kh * HEAD_DIM, (kh + 1) * HEAD_DIM)
        qh = jnp.concatenate(
            [q_blk[:, (kh * GQA_GROUP + h) * HEAD_DIM:(kh * GQA_GROUP + h + 1) * HEAD_DIM]
             for h in range(GQA_GROUP)], axis=0).astype(BF16)
        kk = jnp.concatenate([kp_blk[:, ksl], kc_blk[:, ksl]], axis=0).astype(BF16)
        vv = jnp.concatenate([vp_blk[:, ksl], vc_blk[:, ksl]], axis=0).astype(BF16)
        s = jax.lax.dot_general(qh, kk, (((1,), (1,)), ((), ())),
                                preferred_element_type=F32) * (HEAD_DIM ** -0.5 * LOG2E) + bias
        sk = jnp.zeros((rows, 1), F32)
        for h in range(GQA_GROUP):
            sk = jnp.where(rgrp == h, sink_ref[l * N_HEADS + kh * GQA_GROUP + h] * LOG2E, sk)
        m = jnp.maximum(jnp.max(s, axis=-1, keepdims=True), sk)
        p = jnp.exp2(s - m)
        denom = jnp.sum(p, axis=-1, keepdims=True) + jnp.exp2(sk - m)
        o = jnp.dot(p.astype(BF16), vv, preferred_element_type=F32) / denom
        outs.extend(o[h * BLOCK:(h + 1) * BLOCK] for h in range(GQA_GROUP))
    return _rms(jnp.concatenate(outs, axis=1), g)


def _ssm_params_kernel(ar_ref, ai_ref, ldt_ref, kk_ref, br_ref, bi_ref, tr_ref, ti_ref, bbr_ref, bbi_ref):
    ar, ai = ar_ref[...], ai_ref[...]
    dt = jnp.exp(ldt_ref[...])
    kk = kk_ref[...]
    mag = jnp.exp(dt * ar * kk)
    ang = dt * ai * kk
    tr = mag * jnp.cos(ang)
    ti = mag * jnp.sin(ang)
    tr_ref[...] = tr
    ti_ref[...] = ti
    abr, abi = tr[0:1], ti[0:1]
    den = ar * ar + ai * ai
    fr = ((abr - 1.0) * ar + abi * ai) / den
    fi = (abi * ar - (abr - 1.0) * ai) / den
    br, bi = br_ref[...], bi_ref[...]
    bbr_ref[...] = fr * br - fi * bi
    bbi_ref[...] = fr * bi + fi * br


def _ssm_params(A_re, A_im, log_dt, B_re, B_im, C_re, C_im):
    depth = A_re.shape[0]
    n = depth * SSM_GROUPS * SSM_STATE
    row = lambda t: t.astype(F32).reshape(1, n)
    ldt = jnp.broadcast_to(log_dt.astype(F32)[:, :, None], (depth, SSM_GROUPS, SSM_STATE)).reshape(1, n)
    kk = jnp.array(SSM_POW_ROWS + (0,) * (SUBLANES - len(SSM_POW_ROWS)), F32).reshape(SUBLANES, 1)
    chan_first = lambda t: t.astype(F32).reshape(n, SSM_GROUP_SIZE).T
    shapes = [(SUBLANES, n), (SUBLANES, n), (SSM_GROUP_SIZE, n), (SSM_GROUP_SIZE, n)]
    tr, ti, bbr, bbi = pl.pallas_call(
        _ssm_params_kernel,
        out_shape=[jax.ShapeDtypeStruct(s, F32) for s in shapes],
        name="ssm_params",
    )(row(A_re), row(A_im), ldt, kk, chan_first(B_re), chan_first(B_im))
    J, G8 = SSM_LANE_BLOCKS, LANES // SSM_GROUP_SIZE

    def lanes(t):
        return t.reshape(SUBLANES, depth, J, SSM_BLOCK_STATES).transpose(1, 0, 2, 3)

    pw = jnp.concatenate([lanes(tr), lanes(ti)], axis=-1).reshape(depth, SUBLANES, SSM_STATE_LANES)
    eye = jnp.eye(G8, dtype=F32)

    def bdiag(t):
        t = t.reshape(SSM_GROUP_SIZE, depth, J, G8, SSM_STATE).transpose(1, 2, 3, 0, 4)
        t = t[:, :, :, :, None, :] * eye[None, None, :, None, :, None]
        return t.reshape(depth, J, LANES, SSM_BLOCK_STATES)

    bblk = jnp.concatenate([bdiag(bbr), bdiag(bbi)], axis=-1).astype(BF16)

    def cdiag(t):
        t = t.astype(F32).reshape(depth, J, G8, SSM_GROUP_SIZE, SSM_STATE).transpose(0, 1, 2, 4, 3)
        t = t[:, :, :, :, None, :] * eye[None, None, :, None, :, None]
        return t.reshape(depth, J, SSM_BLOCK_STATES, LANES)

    cblk = jnp.concatenate([cdiag(C_re), -cdiag(C_im)], axis=2).astype(BF16)
    return pw, bblk, cblk


def _ssm_tail(y, u, d_ref, wg_ref, bg_ref, g_ref):
    y = y + d_ref[...] * u
    z = jax.nn.gelu(y)
    gate = jax.nn.sigmoid(jnp.dot(z.astype(BF16), wg_ref[...], preferred_element_type=F32) + bg_ref[...])
    return _rms(z * gate, g_ref[...])


def _ssm_sweep(x_scr, n, a_tabs, init, store):
    S = SSM_BLOCK_STATES
    fins = []
    for j0 in range(0, SSM_LANE_BLOCKS, 2):
        js = (j0, j0 + 1)

        def body(k, carry, js=js):
            r0 = pl.multiple_of(k * SUBLANES, SUBLANES)
            out = []
            for idx, j in enumerate(js):
                hr, hi = carry[2 * idx], carry[2 * idx + 1]
                base = j * 2 * S
                ar, ai = a_tabs[j]
                nhr = ar * hr - ai * hi + x_scr[pl.ds(r0, SUBLANES), base:base + S]
                nhi = ar * hi + ai * hr + x_scr[pl.ds(r0, SUBLANES), base + S:base + 2 * S]
                if store:
                    x_scr[pl.ds(r0, SUBLANES), base:base + S] = nhr
                    x_scr[pl.ds(r0, SUBLANES), base + S:base + 2 * S] = nhi
                out += [nhr, nhi]
            return tuple(out)

        c0 = tuple(t for j in js for t in init[j])
        res = jax.lax.fori_loop(0, n // SUBLANES, body, c0, unroll=True)
        fins += [(res[0], res[1]), (res[2], res[3])]
    return fins


def _ssm_rows(u, pow_row, pw_ref, bblk_ref, cblk_ref, x_scr, s_scr, carry_scr):
    n = u.shape[0]
    S = SSM_BLOCK_STATES
    ub = u.astype(BF16)
    for j in range(SSM_LANE_BLOCKS):
        x_scr[0:n, j * 2 * S:(j + 1) * 2 * S] = jnp.dot(ub[:, j * LANES:(j + 1) * LANES], bblk_ref[j],
                                                       preferred_element_type=F32)
    bc = lambda t: jnp.broadcast_to(t, (SUBLANES, S))
    re = lambda ref, r0, r1, j: ref[r0:r1, j * 2 * S:j * 2 * S + S]
    im = lambda ref, r0, r1, j: ref[r0:r1, j * 2 * S + S:(j + 1) * 2 * S]
    a_tabs = [(bc(re(pw_ref, 0, 1, j)), bc(im(pw_ref, 0, 1, j))) for j in range(SSM_LANE_BLOCKS)]
    zero = jnp.zeros((SUBLANES, S), F32)
    fins = _ssm_sweep(x_scr, n, a_tabs, [(zero, zero)] * SSM_LANE_BLOCKS, store=False)
    for j in range(SSM_LANE_BLOCKS):
        base = j * 2 * S
        cr, ci = re(pw_ref, pow_row, pow_row + 1, j), im(pw_ref, pow_row, pow_row + 1, j)
        sr, si = re(carry_scr, 0, 1, j), im(carry_scr, 0, 1, j)
        fr, fi = fins[j]
        for c in range(SUBLANES):
            s_scr[c:c + 1, base:base + S] = sr
            s_scr[c:c + 1, base + S:base + 2 * S] = si
            sr, si = cr * sr - ci * si + fr[c:c + 1], cr * si + ci * sr + fi[c:c + 1]
        carry_scr[:, base:base + S] = bc(sr)
        carry_scr[:, base + S:base + 2 * S] = bc(si)
    init = [(re(s_scr, 0, SUBLANES, j), im(s_scr, 0, SUBLANES, j)) for j in range(SSM_LANE_BLOCKS)]
    _ssm_sweep(x_scr, n, a_tabs, init, store=True)
    ys = [jnp.dot(x_scr[0:n, j * 2 * S:(j + 1) * 2 * S].astype(BF16), cblk_ref[j], preferred_element_type=F32)
          for j in range(SSM_LANE_BLOCKS)]
    return jnp.concatenate(ys, axis=1)


def _ssm_tile(u_ref, n, pow_row, refs, o_ref, scr):
    pw_ref, bblk_ref, cblk_ref, d_ref, wg_ref, bg_ref, g_ref = refs
    up_scr, x_scr, s_scr, carry_scr, o_scr = scr
    q = n // SUBLANES
    for j in range(SSM_LANE_BLOCKS):
        for k in range(q):
            up_scr[k * SUBLANES:(k + 1) * SUBLANES, j * LANES:(j + 1) * LANES] = \
                u_ref[j, pl.ds(k, SUBLANES, stride=q), :]
    u = up_scr[0:n, :]
    y = _ssm_rows(u, pow_row, pw_ref, bblk_ref, cblk_ref, x_scr, s_scr, carry_scr)
    out = _ssm_tail(y, u, d_ref, wg_ref, bg_ref, g_ref)
    for j in range(SSM_LANE_BLOCKS):
        for k in range(q):
            o_scr[j, pl.ds(k, SUBLANES, stride=q), :] = out[k * SUBLANES:(k + 1) * SUBLANES,
                                                            j * LANES:(j + 1) * LANES]
    o_ref[...] = jnp.concatenate([o_scr[j, 0:n, :] for j in range(SSM_LANE_BLOCKS)], axis=1).astype(o_ref.dtype)


def _mixers_kernel(sink_ref,
                   qm_ref, qh_ref, kpm_ref, km_ref, kh_ref, vpm_ref, vm_ref, vh_ref, ga_ref, bias_ref,
                   um_ref, uh_ref, pw_ref, bblk_ref, cblk_ref, d_ref, wg_ref, bg_ref, gs_ref,
                   xm_ref, halo_ref, xh_ref, wp_ref, sc_ref, gp_ref, *rest, l, n_cast, n_dec):
    cast_in = rest[0:n_cast]
    am_ref, ah_ref, som_ref, soh_ref, st_ref, pom_ref, poh_ref = rest[n_cast:n_cast + 7]
    cast_out = rest[n_cast + 7:2 * n_cast + 7]
    scr = rest[2 * n_cast + 7:]
    carry_scr = scr[3]
    b = pl.program_id(0)
    t = pl.program_id(1)
    tm = xm_ref.shape[0]
    ssm_refs = (pw_ref, bblk_ref, cblk_ref, d_ref, wg_ref, bg_ref, gs_ref)
    blk = functools.partial(_attn_block, sink_ref=sink_ref, l=l, g=ga_ref[...])
    meta_rows = pl.ds(pl.multiple_of(n_dec + b * N_META, N_META), N_META)

    @pl.when((b == 0) & (t == 0))
    def _():
        for ref in (ah_ref, soh_ref, poh_ref):
            ref[0:n_dec, :] = jnp.zeros((n_dec, ref.shape[1]), ref.dtype)

    block = lambda ref: jnp.concatenate([jnp.zeros((SEQ0, ref.shape[1]), F32), ref[...]], axis=0)

    @pl.when(t == 0)
    def _():
        a = blk(block(qh_ref), kpm_ref[...], block(kh_ref), vpm_ref[...], block(vh_ref), bias_ref[0])
        ah_ref[meta_rows, :] = a[SEQ0:].astype(ah_ref.dtype)
        carry_scr[...] = jnp.zeros_like(carry_scr)
        _ssm_tile(uh_ref, N_META, POW_ROW_HEAD, ssm_refs, soh_ref.at[meta_rows, :], scr)
        prev = jnp.zeros((POOL_HALO, POOL_WIDTH), F32)
        poh_ref[meta_rows, :] = _pool_rows(xh_ref[...], prev, 0, wp_ref, sc_ref, gp_ref).astype(poh_ref.dtype)

    @pl.when(t > 0)
    def _():
        for src, dst in zip(cast_in, cast_out):
            dst[...] = src[...].astype(BF16)
        kp = jnp.where(t == 1, block(kh_ref), kpm_ref[...])
        vp = jnp.where(t == 1, block(vh_ref), vpm_ref[...])
        for n in range(tm // BLOCK):
            rows = slice(n * BLOCK, (n + 1) * BLOCK)
            kc, vc = km_ref[rows, :], vm_ref[rows, :]
            bias = bias_ref[jnp.minimum(t, 2)] if n == 0 else bias_ref[2]
            am_ref[rows, :] = blk(qm_ref[rows, :], kp, kc, vp, vc, bias).astype(am_ref.dtype)
            kp, vp = kc, vc
        _ssm_tile(um_ref, um_ref.shape[1], POW_ROW_MAIN, ssm_refs, som_ref, scr)
        st_ref[...] = carry_scr[...]
        prev = jnp.where(t == 1, xh_ref[...], halo_ref[...])
        pom_ref[...] = _pool_rows(xm_ref[...], prev, N_META + (t - 1) * tm, wp_ref, sc_ref, gp_ref).astype(pom_ref.dtype)


def _mixers(sinks, qkv_m, qkv_h, um, uh, xm, xh, vecs, bias, pw, bblk, cblk, wg, wp, l, B, tm, casts):
    assert (1, N_META // SUBLANES, tm // SUBLANES) == SSM_POW_ROWS and N_META == POOL_HALO
    J = SSM_LANE_BLOCKS
    rm, rh = um.shape[1], uh.shape[1]
    nt = rm // (B * tm)
    r = tm // POOL_HALO
    bpt = tm // BLOCK
    tile = lambda b, t: b * nt + jnp.maximum(t - 1, 0)
    main = lambda b, t, _: (tile(b, t), 0)
    n_dec = uh.shape[1] - B * N_META
    head = lambda b, t, _: (n_dec // N_META + b, 0)
    halo = lambda b, t, _: (jnp.maximum((b * nt + t - 1) * r - 1, 0), 0)
    prevb = lambda b, t, _: (jnp.maximum(tile(b, t) * bpt - 1, 0), 0)
    cast_specs = []
    for w, wl in casts:
        _, cr, cc = w.shape
        rows = cr // (B * nt)
        cast_specs.append((pl.BlockSpec((None, rows, cc), lambda b, t, _, wl=wl: (wl, tile(b, t), 0)),
                           pl.BlockSpec((rows, cc), main), jax.ShapeDtypeStruct((cr, cc), BF16)))
    bf = lambda n, w: jax.ShapeDtypeStruct((n, w), BF16)
    tile_spec = lambda w: pl.BlockSpec((tm, w), main)
    head_spec = lambda w: pl.BlockSpec((N_META, w), head)
    in_specs = [tile_spec(ATTN_WIDTH), head_spec(ATTN_WIDTH),
                pl.BlockSpec((BLOCK, KV_WIDTH), prevb), tile_spec(KV_WIDTH), head_spec(KV_WIDTH),
                pl.BlockSpec((BLOCK, KV_WIDTH), prevb), tile_spec(KV_WIDTH), head_spec(KV_WIDTH),
                _vrow("g_out_attn", l), _full(bias.shape),
                pl.BlockSpec((J, tm, LANES), lambda b, t, _: (0, tile(b, t), 0)),
                pl.BlockSpec((J, N_META, LANES), lambda b, t, _: (0, n_dec // N_META + b, 0)),
                _layer(pw.shape[1:], l), _layer(bblk.shape[1:], l), _layer(cblk.shape[1:], l),
                _vrow("D_skip", l), _layer((SSM_WIDTH, SSM_WIDTH), l), _vrow("b_glu", l), _vrow("g_out_ssm", l),
                tile_spec(POOL_WIDTH), pl.BlockSpec((POOL_HALO, POOL_WIDTH), halo), head_spec(POOL_WIDTH),
                _layer(wp.shape[1:], l), _vrow("pool_scale", l), _vrow("g_out_pool", l)]
    in_specs += [c[0] for c in cast_specs]
    head_out = lambda w: pl.BlockSpec((rh, w), lambda b, t, _: (0, 0))
    out_specs = [tile_spec(ATTN_WIDTH), head_out(ATTN_WIDTH), tile_spec(SSM_WIDTH), head_out(SSM_WIDTH),
                 pl.BlockSpec((None, SUBLANES, SSM_STATE_LANES), lambda b, t, _: (b, 0, 0)),
                 tile_spec(POOL_WIDTH), head_out(POOL_WIDTH)] + [c[1] for c in cast_specs]
    out_shape = [bf(rm, ATTN_WIDTH), bf(rh, ATTN_WIDTH), bf(rm, SSM_WIDTH), bf(rh, SSM_WIDTH),
                 jax.ShapeDtypeStruct((B, SUBLANES, SSM_STATE_LANES), F32),
                 bf(rm, POOL_WIDTH), bf(rh, POOL_WIDTH)] + [c[2] for c in cast_specs]
    (qm, km, vm), (qh, kh, vh) = qkv_m, qkv_h
    res = pl.pallas_call(
        functools.partial(_mixers_kernel, l=l, n_cast=len(casts), n_dec=n_dec),
        grid_spec=pltpu.PrefetchScalarGridSpec(
            num_scalar_prefetch=1, grid=(B, nt + 1), in_specs=in_specs, out_specs=out_specs,
            scratch_shapes=[pltpu.VMEM((tm, SSM_WIDTH), F32),
                            pltpu.VMEM((tm, SSM_STATE_LANES), F32),
                            pltpu.VMEM((SUBLANES, SSM_STATE_LANES), F32),
                            pltpu.VMEM((SUBLANES, SSM_STATE_LANES), F32),
                            pltpu.VMEM((J, tm, LANES), F32)]),
        out_shape=out_shape,
        compiler_params=_params("arbitrary", "arbitrary"),
        name="mixers",
    )(sinks, qm, qh, km, km, kh, vm, vm, vh, vecs, bias, um, uh, pw, bblk, cblk, vecs, wg, vecs, vecs,
      xm, xm, xh, wp, vecs, vecs, *[w for w, _ in casts])
    return res[0:7], res[7:]


def _state_from_lanes(s):
    s = s.reshape(s.shape[0], SSM_LANE_BLOCKS, 2, SSM_BLOCK_STATES)
    return (s[:, :, 0].reshape(-1, SSM_GROUPS, SSM_STATE), s[:, :, 1].reshape(-1, SSM_GROUPS, SSM_STATE))


def _pool_tail(d_groups, w_ref, sc_ref, g_ref):
    y = jnp.concatenate(
        [jnp.dot(d.astype(BF16), w_ref[gi], preferred_element_type=F32) for gi, d in enumerate(d_groups)], axis=1)
    return _rms(y * sc_ref[...], g_ref[...])


def _pool_rows(x, prev, pos0, w_ref, sc_ref, g_ref):
    n = x.shape[0]
    xe = jnp.concatenate([prev, x], axis=0)
    pos = pos0 + jax.lax.broadcasted_iota(jnp.int32, (n, 1), 0)
    ds = []
    for gi, w in enumerate(POOL_WINDOWS):
        gsl = slice(gi * POOL_GROUP, (gi + 1) * POOL_GROUP)
        s = xe[:, gsl]
        k = 1
        while k < w:
            s = s + pltpu.roll(s, k, 0)
            k *= 2
        cnt = jnp.clip(pos + 1, 1, w).astype(F32)
        ds.append(s[POOL_HALO:] / cnt - x[:, gsl])
    return _pool_tail(ds, w_ref, sc_ref, g_ref)


def _attn_sample_kernel(q_ref, kn_ref, vn_ref, kc_ref, vc_ref, kd_ref, vd_ref, sink_ref, g_ref, *rest, first):
    a_ref, nk_ref, nv_ref, acc_scr = rest[-4:]
    step = pl.program_id(1 if first else 0)
    nsteps = pl.num_programs(1 if first else 0)
    slot = WINDOW - 1 if first else 0
    scale = HEAD_DIM ** -0.5
    if first:
        for bb in range(DEC_STEP):
            nk_ref[bb, 0:WINDOW - 1] = kc_ref[bb, 1:WINDOW]
            nv_ref[bb, 0:WINDOW - 1] = vc_ref[bb, 1:WINDOW]
            nk_ref[bb, WINDOW - 1] = jnp.zeros((N_KV_HEADS, HEAD_DIM), F32)
            nv_ref[bb, WINDOW - 1] = jnp.zeros((N_KV_HEADS, HEAD_DIM), F32)

    def attend():
        _attn_sample_rows(q_ref, kn_ref, vn_ref, kd_ref, vd_ref, sink_ref, g_ref, nk_ref, nv_ref, acc_scr,
                          step, slot, scale)

        @pl.when(step == nsteps - 1)
        def _():
            a_ref[...] = acc_scr[...].astype(a_ref.dtype)

    if first:
        pl.when(pl.program_id(0) == 0)(attend)
    else:
        attend()


def _attn_sample_rows(q_ref, kn_ref, vn_ref, kc_ref, vc_ref, sink_ref, g_ref, nk_ref, nv_ref, acc_scr,
                      step, slot, scale):
    ncols = WINDOW * N_KV_HEADS
    head = jax.lax.broadcasted_iota(jnp.int32, (N_HEADS, ncols), 0)
    col = jax.lax.broadcasted_iota(jnp.int32, (N_HEADS, ncols), 1)
    own = (col % N_KV_HEADS) == (head // GQA_GROUP)
    sk = jnp.concatenate([sink_ref[kh] for kh in range(N_KV_HEADS)], axis=0)
    per_head = lambda rows: jnp.concatenate(
        [jnp.broadcast_to(r, (GQA_GROUP, HEAD_DIM)) for r in rows], axis=0)
    for bb in range(DEC_STEP):
        q = jnp.concatenate([q_ref[bb:bb + 1, h * HEAD_DIM:(h + 1) * HEAD_DIM] for h in range(N_HEADS)], axis=0)
        kn = [kn_ref[bb:bb + 1, kh * HEAD_DIM:(kh + 1) * HEAD_DIM] for kh in range(N_KV_HEADS)]
        vn = [vn_ref[bb:bb + 1, kh * HEAD_DIM:(kh + 1) * HEAD_DIM] for kh in range(N_KV_HEADS)]
        for kh in range(N_KV_HEADS):
            nk_ref[bb, slot, kh:kh + 1, :] = kn[kh]
            nv_ref[bb, slot, kh:kh + 1, :] = vn[kh]
        sc = jax.lax.dot_general(q.astype(BF16), kc_ref[bb].astype(BF16), (((1,), (1,)), ((), ())),
                                 preferred_element_type=F32) * scale
        sc = jnp.where(own, sc, -jnp.inf)
        sn = jnp.sum(q * per_head(kn), axis=-1, keepdims=True) * scale
        m = jnp.maximum(jnp.maximum(jnp.max(sc, axis=-1, keepdims=True), sn), sk)
        pc = jnp.exp(sc - m)
        pn = jnp.exp(sn - m)
        denom = jnp.sum(pc, axis=-1, keepdims=True) + pn + jnp.exp(sk - m)
        o = jnp.dot(pc.astype(BF16), vc_ref[bb].astype(BF16), preferred_element_type=F32)
        o = (o + pn * per_head(vn)) / denom
        a = jnp.concatenate([o[h:h + 1] for h in range(N_HEADS)], axis=1)
        acc_scr[pl.ds(step * DEC_STEP + bb, 1), :] = _rms(a, g_ref[...])


def _attn_sample(qh, kh, vh, cache_k, cache_v, l, sinks, g, ah, nk, nv, N):
    depth = cache_k.shape[0]
    first = nk is None
    anyspec = pl.BlockSpec(memory_space=pl.ANY)
    cshape = jax.ShapeDtypeStruct(cache_k.shape, F32)
    slab = (None, DEC_STEP, WINDOW, N_KV_HEADS, HEAD_DIM)
    dense_slab = (None, DEC_STEP, WINDOW * N_KV_HEADS, HEAD_DIM)
    dense_view = lambda c: c.reshape(depth, N, WINDOW * N_KV_HEADS, HEAD_DIM)
    nsteps = N // DEC_STEP
    if first:
        assert l == 0
        grid = (depth, nsteps)
        rows = lambda w: pl.BlockSpec((DEC_STEP, w), lambda ll, s: (s, 0))
        cache = pl.BlockSpec(slab, lambda ll, s: (ll, s, 0, 0, 0))
        dense = pl.BlockSpec(dense_slab, lambda ll, s: (0, jnp.where(ll == 0, s, nsteps - 1), 0, 0))
        new = cache
        a_spec = pl.BlockSpec((N, ATTN_WIDTH), lambda ll, s: (0, 0))
        carried = []
    else:
        grid = (nsteps,)
        rows = lambda w: pl.BlockSpec((DEC_STEP, w), lambda s: (s, 0))
        cache = anyspec
        dense = pl.BlockSpec(dense_slab, lambda s: (l, s, 0, 0))
        new = pl.BlockSpec((None, DEC_STEP, 1, N_KV_HEADS, HEAD_DIM), lambda s: (l, s, WINDOW - 1, 0, 0))
        a_spec = pl.BlockSpec((N, ATTN_WIDTH), lambda s: (0, 0))
        carried = [nk, nv]
    n_in = 9
    return pl.pallas_call(
        functools.partial(_attn_sample_kernel, first=first),
        grid=grid,
        in_specs=[rows(ATTN_WIDTH), rows(KV_WIDTH), rows(KV_WIDTH), cache, cache, dense, dense,
                  _layer((N_KV_HEADS, GQA_GROUP, 1), l), _vrow("g_out_attn", l)] + [anyspec] * (1 + len(carried)),
        out_specs=[a_spec, new, new],
        out_shape=[jax.ShapeDtypeStruct(ah.shape, ah.dtype), cshape, cshape],
        scratch_shapes=[pltpu.VMEM((N, ATTN_WIDTH), F32)],
        input_output_aliases={n_in + i: i for i in range(1 + len(carried))},
        compiler_params=_params(*(["arbitrary"] * len(grid))),
        name="attn_sample",
    )(qh, kh, vh, cache_k, cache_v, dense_view(cache_k), dense_view(cache_v), sinks, g, ah, *carried)


def _mix_sample_kernel(u_ref, h0_ref, pw_ref, bblk_ref, cblk_ref, d_ref, wg_ref, bg_ref, gs_ref,
                       xp_ref, pb_ref, wp_ref, sc_ref, gp_ref, s_in_ref, p_in_ref, s_ref, st_ref, p_ref):
    del s_in_ref, p_in_ref
    S = SSM_BLOCK_STATES
    u = jnp.concatenate([u_ref[j] for j in range(SSM_LANE_BLOCKS)], axis=1)
    ub = u.astype(BF16)
    ys = []
    for j in range(SSM_LANE_BLOCKS):
        x = jnp.dot(ub[:, j * LANES:(j + 1) * LANES], bblk_ref[j], preferred_element_type=F32)
        base = j * 2 * S
        ar = pw_ref[0:1, base:base + S]
        ai = pw_ref[0:1, base + S:base + 2 * S]
        h0r = h0_ref[:, base:base + S]
        h0i = h0_ref[:, base + S:base + 2 * S]
        hr = x[:, 0:S] + ar * h0r - ai * h0i
        hi = x[:, S:] + ar * h0i + ai * h0r
        st_ref[:, base:base + S] = hr
        st_ref[:, base + S:base + 2 * S] = hi
        h = jnp.concatenate([hr, hi], axis=1).astype(BF16)
        ys.append(jnp.dot(h, cblk_ref[j], preferred_element_type=F32))
    s_ref[...] = _ssm_tail(jnp.concatenate(ys, axis=1), u, d_ref, wg_ref, bg_ref, gs_ref).astype(s_ref.dtype)

    xp = xp_ref[...]
    ds = []
    for gi, w in enumerate(POOL_WINDOWS):
        gsl = slice(gi * POOL_GROUP, (gi + 1) * POOL_GROUP)
        s = xp[:, gsl]
        for back in range(1, w):
            s = s + pb_ref[POOL_BUF - back][:, gsl]
        ds.append(s / float(w) - xp[:, gsl])
    p_ref[...] = _pool_tail(ds, wp_ref, sc_ref, gp_ref).astype(p_ref.dtype)


def _mix_sample(uh, xph, h0, pbuf, vecs, pw, bblk, cblk, wg, wp, l, sh, ph):
    N = h0.shape[1]
    rows = lambda w: pl.BlockSpec((N, w), lambda i: (0, 0))
    anyspec = pl.BlockSpec(memory_space=pl.ANY)
    return pl.pallas_call(
        _mix_sample_kernel,
        grid=(1,),
        in_specs=[pl.BlockSpec((SSM_LANE_BLOCKS, N, LANES), lambda i: (0, 0, 0)),
                  _layer(h0.shape[1:], l), _layer(pw.shape[1:], l), _layer(bblk.shape[1:], l),
                  _layer(cblk.shape[1:], l),
                  _vrow("D_skip", l), _layer((SSM_WIDTH, SSM_WIDTH), l), _vrow("b_glu", l),
                  _vrow("g_out_ssm", l), rows(POOL_WIDTH), _layer(pbuf.shape[1:], l),
                  _layer(wp.shape[1:], l),
                  _vrow("pool_scale", l), _vrow("g_out_pool", l), anyspec, anyspec],
        out_specs=[rows(SSM_WIDTH), _full((N, SSM_STATE_LANES)), rows(POOL_WIDTH)],
        out_shape=[jax.ShapeDtypeStruct(sh.shape, sh.dtype), jax.ShapeDtypeStruct((N, SSM_STATE_LANES), F32),
                   jax.ShapeDtypeStruct(ph.shape, ph.dtype)],
        input_output_aliases={14: 0, 15: 2},
        compiler_params=_params("arbitrary"),
        name="mix_sample",
    )(uh, h0, pw, bblk, cblk, vecs, wg, vecs, vecs, xph, pbuf, wp, vecs, vecs, sh, ph)


def _outproj_rows(x_ref, a_ref, s_ref, p_ref, w_ref, o_ref):
    o1 = ATTN_WIDTH
    o2 = o1 + SSM_WIDTH
    acc = x_ref[...]
    acc = acc + jnp.dot(a_ref[...], w_ref[0:o1, :], preferred_element_type=F32)
    acc = acc + jnp.dot(s_ref[...], w_ref[o1:o2, :], preferred_element_type=F32)
    acc = acc + jnp.dot(p_ref[...], w_ref[o2:, :], preferred_element_type=F32)
    o_ref[...] = acc


def _outproj_kernel(xm_ref, am_ref, sm_ref, pm_ref, xh_ref, ah_ref, sh_ref, ph_ref, w_ref, om_ref, oh_ref):
    i = pl.program_id(0)
    last = pl.num_programs(0) - 1

    @pl.when(i < last)
    def _():
        _outproj_rows(xm_ref, am_ref, sm_ref, pm_ref, w_ref, om_ref)

    @pl.when(i == last)
    def _():
        _outproj_rows(xh_ref, ah_ref, sh_ref, ph_ref, w_ref, oh_ref)


def _outproj(xm, am, sm, pm, xh, ah, sh, ph, w, tm):
    R, RH = xm.shape[0], xh.shape[0]
    nm = R // tm
    row = lambda i: (jnp.minimum(i, nm - 1), 0)
    once = pl.Buffered(1)
    head = lambda w_: pl.BlockSpec((RH, w_), lambda i: (0, 0), pipeline_mode=once)
    return pl.pallas_call(
        _outproj_kernel,
        grid=(nm + 1,),
        in_specs=[pl.BlockSpec((tm, D_MODEL), row), pl.BlockSpec((tm, ATTN_WIDTH), row),
                  pl.BlockSpec((tm, SSM_WIDTH), row), pl.BlockSpec((tm, POOL_WIDTH), row),
                  head(D_MODEL), head(ATTN_WIDTH), head(SSM_WIDTH), head(POOL_WIDTH),
                  pl.BlockSpec((D_MODEL, D_MODEL), lambda i: (0, 0), pipeline_mode=once)],
        out_specs=[pl.BlockSpec((tm, D_MODEL), row), pl.BlockSpec((RH, D_MODEL), lambda i: (0, 0))],
        out_shape=[jax.ShapeDtypeStruct((R, D_MODEL), F32), jax.ShapeDtypeStruct((RH, D_MODEL), F32)],
        compiler_params=_params("arbitrary"),
        name="outproj",
    )(xm, am, sm, pm, xh, ah, sh, ph, w)


def _ffn_kernel(xm_ref, xh_ref, g_ref, w1_ref, w2_ref, om_ref, oh_ref, hm_scr, hh_scr):
    i = pl.program_id(0)
    f = pl.program_id(1)

    def mlp(h):
        h1 = jnp.dot(h, w1_ref[...], preferred_element_type=F32)
        return jnp.dot(jnp.square(jnp.maximum(h1, 0.0)).astype(BF16), w2_ref[...], preferred_element_type=F32)

    def first(x_ref, h_scr, o_ref):
        x = x_ref[...]
        h = _rms(x, g_ref[...]).astype(BF16)
        h_scr[...] = h
        o_ref[...] = x + mlp(h)

    @pl.when(f == 0)
    def _():
        first(xm_ref, hm_scr, om_ref)

    @pl.when(f > 0)
    def _():
        om_ref[...] += mlp(hm_scr[...])

    @pl.when((i == 0) & (f == 0))
    def _():
        first(xh_ref, hh_scr, oh_ref)

    @pl.when((i == 0) & (f > 0))
    def _():
        oh_ref[...] += mlp(hh_scr[...])


def _ffn(xm, xh, g, l, w1, w2, tm, tf):
    R, RH = xm.shape[0], xh.shape[0]
    return pl.pallas_call(
        _ffn_kernel,
        grid=(R // tm, D_FF // tf),
        in_specs=[pl.BlockSpec((tm, D_MODEL), lambda i, f: (i, 0)),
                  pl.BlockSpec((RH, D_MODEL), lambda i, f: (0, 0), pipeline_mode=pl.Buffered(1)),
                  _vrow("g_ffn", l),
                  pl.BlockSpec((D_MODEL, tf), lambda i, f: (0, f)),
                  pl.BlockSpec((tf, D_MODEL), lambda i, f: (f, 0))],
        out_specs=[pl.BlockSpec((tm, D_MODEL), lambda i, f: (i, 0)),
                   pl.BlockSpec((RH, D_MODEL), lambda i, f: (0, 0))],
        out_shape=[jax.ShapeDtypeStruct((R, D_MODEL), F32), jax.ShapeDtypeStruct((RH, D_MODEL), F32)],
        scratch_shapes=[pltpu.VMEM((tm, D_MODEL), BF16), pltpu.VMEM((RH, D_MODEL), BF16)],
        compiler_params=_params("arbitrary", "arbitrary"),
        name="ffn",
    )(xm, xh, g, w1, w2)


def _rope_tables(pos):
    half = ROT_HALF
    inv = ROPE_THETA ** (-np.arange(0, ROT_DIM, 2, dtype=np.float64) / ROT_DIM)
    ang = np.asarray(pos, np.float64)[:, None] * inv
    cos, sin = np.cos(ang), np.sin(ang)
    n = ang.shape[0]
    z = np.zeros((n, HEAD_DIM - ROT_DIM))
    zh = np.zeros((n, half))
    rc = np.concatenate([cos, cos, z + 1.0], axis=1)
    rs1 = np.concatenate([zh, sin, z], axis=1)
    rs2 = np.concatenate([-sin, zh, z], axis=1)
    return tuple(t.astype(np.float32) for t in (rc, rs1, rs2))


def kernel(x_prompt, x_sample, cache_k, cache_v, state_ssm_re, state_ssm_im, state_pool, meta_tokens, g_mix, w_in, g_q, g_k, sinks, A_re, A_im, log_dt, B_re, B_im, C_re, C_im, D_skip, w_glu, b_glu, w_pool, pool_scale, g_out_attn, g_out_ssm, g_out_pool, w_out, g_ffn, w_ff1, w_ff2):
    B, T, _ = x_prompt.shape
    N = x_sample.shape[0]
    depth = w_in.shape[0]
    assert N % N_META == 0 and N % DEC_STEP == 0 and T % TM_FFN == 0
    xh = jnp.concatenate([x_sample.reshape(N, D_MODEL)] + [meta_tokens.astype(F32)] * B, axis=0)
    xm = x_prompt.reshape(B * T, D_MODEL)

    rope_m = _rope_tables(N_META + np.arange(T))
    rope_h = _rope_tables(np.concatenate([np.full(N, PAST_LEN)] + [np.arange(N_META)] * B))

    wi = w_in[0].astype(BF16)
    wg_all, wp_all = w_glu.astype(BF16), w_pool.astype(BF16)
    ck = cache_k.astype(F32)
    cv = cache_v.astype(F32)

    named = dict(g_mix=g_mix, g_ffn=g_ffn, g_out_attn=g_out_attn, D_skip=D_skip, b_glu=b_glu,
                 g_out_ssm=g_out_ssm, pool_scale=pool_scale, g_out_pool=g_out_pool, g_q=g_q, g_k=g_k)
    vecs = jnp.concatenate([named[name].astype(F32) for name, _ in VEC_LAYOUT], axis=1)[:, None, :]
    pw, bblk, cblk = _ssm_params(A_re, A_im, log_dt, B_re, B_im, C_re, C_im)
    sinks_flat = sinks.astype(F32).reshape(depth * N_HEADS)
    sinks_col = sinks.astype(F32).reshape(depth, N_KV_HEADS, GQA_GROUP, 1)
    bias = _attn_bias()
    h0 = jnp.concatenate([state_ssm_re.astype(F32).reshape(depth, N, SSM_LANE_BLOCKS, SSM_BLOCK_STATES),
                          state_ssm_im.astype(F32).reshape(depth, N, SSM_LANE_BLOCKS, SSM_BLOCK_STATES)],
                         axis=-1).reshape(depth, N, SSM_STATE_LANES)
    pbuf = state_pool.astype(F32).transpose(0, 2, 1, 3)

    nk = nv = None
    ks, vs, pls, sts, st_ss, phs = ([] for _ in range(6))
    for l in range(depth):
        (qm, km, vm, um, pm), (qh, kh, vh, uh, ph), w1 = _inproj(xm, xh, vecs, wi, l, rope_m, rope_h, TM_PROJ, w_ff1)

        (am, ah, sm, sh, st, plm, plh), wcast = _mixers(
            sinks_flat, (qm, km, vm), (qh, kh, vh), um, uh, pm, ph, vecs, bias, pw, bblk, cblk, wg_all, wp_all,
            l, B, TM_SEQ, [(w_ff2, l), (w_out, l)] + ([(w_in, l + 1)] if l + 1 < depth else []))
        w2, wo = wcast[0], wcast[1]

        ah, nk, nv = _attn_sample(qh, kh, vh, ck, cv, l, sinks_col, vecs, ah, nk, nv, N)
        sh, st_s, plh = _mix_sample(uh, ph, h0, pbuf, vecs, pw, bblk, cblk, wg_all, wp_all, l, sh, plh)

        xm, xh = _outproj(xm, am, sm, plm, xh, ah, sh, plh, wo, TM_PROJ)
        xm, xh = _ffn(xm, xh, vecs, l, w1, w2, TM_FFN, TF_FFN)
        if l + 1 < depth:
            wi = wcast[2]

        ks.append(km.reshape(B, T, KV_WIDTH)[:, T - WINDOW:])
        vs.append(vm.reshape(B, T, KV_WIDTH)[:, T - WINDOW:])
        pls.append(pm.reshape(B, T, POOL_WIDTH)[:, T - POOL_BUF:])
        sts.append(st[:, 0])
        st_ss.append(st_s)
        phs.append(ph[:N])

    y_prompt = xm.reshape(B, T, D_MODEL)
    y_sample = xh[:N].reshape(N, 1, D_MODEL)
    heads = lambda t: jnp.stack(t).reshape(depth, -1, WINDOW, N_KV_HEADS, HEAD_DIM)
    p_re, p_im = _state_from_lanes(jnp.stack(sts).reshape(depth * B, SSM_STATE_LANES))
    s_re, s_im = _state_from_lanes(jnp.stack(st_ss).reshape(depth * N, SSM_STATE_LANES))
    st4 = lambda t, n: t.reshape(depth, n, SSM_GROUPS, SSM_STATE)
    s_pool = jnp.concatenate([state_pool.astype(F32)[:, :, 1:], jnp.stack(phs)[:, :, None]], axis=2)
    return (y_prompt, y_sample, heads(ks), heads(vs), st4(p_re, B), st4(p_im, B), jnp.stack(pls),
            nk, nv, st4(s_re, N), st4(s_im, N), s_pool)
```

```python
import functools
import math

import jax
import jax.numpy as jnp
import numpy as np
from jax.experimental import pallas as pl
from jax.experimental.pallas import tpu as pltpu

D_MODEL = 2048
N_META = 16
HEAD_DIM = 128
N_HEADS = 8
N_KV_HEADS = 2
GQA_GROUP = 4
ATTN_WIDTH = 1024
KV_WIDTH = 256
WINDOW = 128
BLOCK = 128
ROT_DIM = 32
ROT_HALF = ROT_DIM // 2
ROPE_THETA = 500000.0
SSM_WIDTH = 512
SSM_GROUP_SIZE = 16
SSM_GROUPS = 32
SSM_STATE = 64
POOL_WIDTH = 512
POOL_WINDOWS = (2, 4, 8, 16)
POOL_GROUP = 128
POOL_BUF = 15
POOL_HALO = 16
IN_WIDTH = 2560
D_FF = 8192
EPS = 1e-6
PAST_LEN = 16384
LOG2E = math.log2(math.e)

SEQ0 = BLOCK - N_META
LANES = 128
SUBLANES = 8
SSM_LANE_BLOCKS = SSM_WIDTH // LANES
SSM_BLOCK_STATES = (LANES // SSM_GROUP_SIZE) * SSM_STATE
SSM_STATE_LANES = SSM_LANE_BLOCKS * 2 * SSM_BLOCK_STATES
VMEM_LIMIT = 60 * 1024 * 1024

TM_PROJ = 512
TM_FFN = 1024
TF_FFN = 512
TM_SEQ = 512
DEC_STEP = 8
SSM_POW_ROWS = (1, N_META // SUBLANES, TM_SEQ // SUBLANES)
POW_ROW_HEAD, POW_ROW_MAIN = 1, 2

BF16 = jnp.bfloat16
F32 = jnp.float32


def _params(*semantics):
    return pltpu.CompilerParams(dimension_semantics=semantics, vmem_limit_bytes=VMEM_LIMIT)


def _rms(x, g):
    return x * jax.lax.rsqrt(jnp.mean(x * x, axis=-1, keepdims=True) + EPS) * g


def _full(shape):
    n = len(shape)
    return pl.BlockSpec(shape, lambda *_: (0,) * n)


def _layer(shape, l):
    n = len(shape)
    return pl.BlockSpec((None, *shape), lambda *_: (l,) + (0,) * n)


VEC_LAYOUT = (("g_mix", D_MODEL), ("g_ffn", D_MODEL), ("g_out_attn", ATTN_WIDTH), ("D_skip", SSM_WIDTH),
              ("b_glu", SSM_WIDTH), ("g_out_ssm", SSM_WIDTH), ("pool_scale", POOL_WIDTH),
              ("g_out_pool", POOL_WIDTH), ("g_q", HEAD_DIM), ("g_k", HEAD_DIM))
VEC_WIDTH = dict(VEC_LAYOUT)
VEC_OFFSET = {name: sum(w for _, w in VEC_LAYOUT[:i]) for i, (name, _) in enumerate(VEC_LAYOUT)}
assert all(VEC_OFFSET[name] % w == 0 for name, w in VEC_LAYOUT)


def _vrow(name, l):
    w = VEC_WIDTH[name]
    return pl.BlockSpec((None, 1, w), lambda *_: (l, 0, VEC_OFFSET[name] // w))


def _inproj_rows(x, g_ref, w_ref, gq_ref, gk_ref, rope_refs, out_refs):
    q_ref, k_ref, v_ref, u_ref, xp_ref = out_refs
    h = _rms(x, g_ref[...]).astype(BF16)
    proj = jnp.dot(h, w_ref[...], preferred_element_type=F32)
    rc, rs1, rs2 = (r[...] for r in rope_refs)

    def head(t, g):
        t = _rms(t, g)
        return t * rc + pltpu.roll(t, ROT_HALF, 1) * rs1 + pltpu.roll(t, LANES - ROT_HALF, 1) * rs2

    for hd in range(N_HEADS):
        sl = slice(hd * HEAD_DIM, (hd + 1) * HEAD_DIM)
        q_ref[:, sl] = head(proj[:, sl], gq_ref[...])
    for hd in range(N_KV_HEADS):
        sl = slice(hd * HEAD_DIM, (hd + 1) * HEAD_DIM)
        k_ref[:, sl] = head(proj[:, ATTN_WIDTH + hd * HEAD_DIM:ATTN_WIDTH + (hd + 1) * HEAD_DIM], gk_ref[...])
    o2 = ATTN_WIDTH + KV_WIDTH
    o3 = o2 + KV_WIDTH
    o4 = o3 + SSM_WIDTH
    v_ref[...] = proj[:, o2:o3]
    for j in range(SSM_LANE_BLOCKS):
        u_ref[j] = proj[:, o3 + j * LANES:o3 + (j + 1) * LANES]
    xp_ref[...] = proj[:, o4:]


def _inproj_kernel(xm_ref, xh_ref, g_ref, w_ref, gq_ref, gk_ref, rcm_ref, rs1m_ref, rs2m_ref,
                   rch_ref, rs1h_ref, rs2h_ref, *rest):
    cast = len(rest) == 12
    outs = rest[1:] if cast else rest
    main_outs, head_outs = outs[0:5], outs[5:10]
    i = pl.program_id(0)
    last = pl.num_programs(0) - 1

    @pl.when(i < last)
    def _():
        if cast:
            outs[10][...] = rest[0][...].astype(BF16)
        _inproj_rows(xm_ref[...], g_ref, w_ref, gq_ref, gk_ref, (rcm_ref, rs1m_ref, rs2m_ref), main_outs)

    @pl.when(i == last)
    def _():
        _inproj_rows(xh_ref[...], g_ref, w_ref, gq_ref, gk_ref, (rch_ref, rs1h_ref, rs2h_ref), head_outs)


def _inproj(xm, xh, vecs, w, l, rope_m, rope_h, tm, cast=None):
    R, RH = xm.shape[0], xh.shape[0]
    nm = R // tm
    tiles_per_rope = rope_m[0].shape[0] // tm
    tile = lambda i: jnp.minimum(i, nm - 1)
    row = lambda i: (tile(i), 0)
    rrow = lambda i: (tile(i) % tiles_per_rope, 0)
    once = pl.Buffered(1)
    full = lambda n, w_: pl.BlockSpec((n, w_), lambda i: (0, 0))

    def out_set(n, rows, im, uim):
        flat = lambda w_: (pl.BlockSpec((n, w_), im), jax.ShapeDtypeStruct((rows, w_), F32))
        u_out = (pl.BlockSpec((SSM_LANE_BLOCKS, n, LANES), uim),
                 jax.ShapeDtypeStruct((SSM_LANE_BLOCKS, rows, LANES), F32))
        return [flat(ATTN_WIDTH), flat(KV_WIDTH), flat(KV_WIDTH), u_out, flat(POOL_WIDTH)]

    outs = (out_set(tm, R, row, lambda i: (0, tile(i), 0))
            + out_set(RH, RH, lambda i: (0, 0), lambda i: (0, 0, 0)))
    in_specs = [pl.BlockSpec((tm, D_MODEL), row),
                pl.BlockSpec((RH, D_MODEL), lambda i: (0, 0), pipeline_mode=once),
                _vrow("g_mix", l),
                pl.BlockSpec((D_MODEL, IN_WIDTH), lambda i: (0, 0), pipeline_mode=once),
                _vrow("g_q", l), _vrow("g_k", l),
                pl.BlockSpec((tm, LANES), rrow), pl.BlockSpec((tm, LANES), rrow), pl.BlockSpec((tm, LANES), rrow),
                full(RH, LANES), full(RH, LANES), full(RH, LANES)]
    args = [xm, xh, vecs, w, vecs, vecs, *rope_m, *rope_h]
    if cast is not None:
        _, cr, cc = cast.shape
        crows = cr // nm
        in_specs.append(pl.BlockSpec((None, crows, cc), lambda i: (l, tile(i), 0)))
        args.append(cast)
        outs.append((pl.BlockSpec((crows, cc), row), jax.ShapeDtypeStruct((cr, cc), BF16)))
    res = pl.pallas_call(
        _inproj_kernel,
        grid=(nm + 1,),
        in_specs=in_specs,
        out_specs=[o[0] for o in outs],
        out_shape=[o[1] for o in outs],
        compiler_params=_params("arbitrary"),
        name="inproj",
    )(*args)
    return res[0:5], res[5:10], (res[10] if cast is not None else None)


def _attn_bias():
    rows = GQA_GROUP * BLOCK
    i, r, c = np.meshgrid(np.arange(3), np.arange(rows) % BLOCK, np.arange(2 * BLOCK), indexing="ij")
    diff = BLOCK + r - c
    krow = (i - 1) * BLOCK + c
    mask = (diff >= 0) & (diff <= WINDOW) & (krow >= SEQ0)
    return np.where(mask, 0.0, -np.inf).astype(np.float32)


def _attn_block(q_blk, kp_blk, kc_blk, vp_blk, vc_blk, bias, sink_ref, l, g):
    rows = GQA_GROUP * BLOCK
    rgrp = jax.lax.broadcasted_iota(jnp.int32, (rows, 1), 0) // BLOCK
    outs = []
    for kh in range(N_KV_HEADS):
        ksl = slice(kh * HEAD_DIM, (kh + 1) * HEAD_DIM)
        qh = jnp.concatenate(
            [q_blk[:, (kh * GQA_GROUP + h) * HEAD_DIM:(kh * GQA_GROUP + h + 1) * HEAD_DIM]
             for h in range(GQA_GROUP)], axis=0).astype(BF16)
        kk = jnp.concatenate([kp_blk[:, ksl], kc_blk[:, ksl]], axis=0).astype(BF16)
        vv = jnp.concatenate([vp_blk[:, ksl], vc_blk[:, ksl]], axis=0).astype(BF16)
        s = jax.lax.dot_general(qh, kk, (((1,), (1,)), ((), ())),
                                preferred_element_type=F32) * (HEAD_DIM ** -0.5 * LOG2E) + bias
        sk = jnp.zeros((rows, 1), F32)
        for h in range(GQA_GROUP):
            sk = jnp.where(rgrp == h, sink_ref[l * N_HEADS + kh * GQA_GROUP + h] * LOG2E, sk)
        m = jnp.maximum(jnp.max(s, axis=-1, keepdims=True), sk)
        p = jnp.exp2(s - m)
        denom = jnp.sum(p, axis=-1, keepdims=True) + jnp.exp2(sk - m)
        o = jnp.dot(p.astype(BF16), vv, preferred_element_type=F32) / denom
        outs.extend(o[h * BLOCK:(h + 1) * BLOCK] for h in range(GQA_GROUP))
    return _rms(jnp.concatenate(outs, axis=1), g)


def _ssm_params_kernel(ar_ref, ai_ref, ldt_ref, kk_ref, br_ref, bi_ref, tr_ref, ti_ref, bbr_ref, bbi_ref):
    ar, ai = ar_ref[...], ai_ref[...]
    dt = jnp.exp(ldt_ref[...])
    kk = kk_ref[...]
    mag = jnp.exp(dt * ar * kk)
    ang = dt * ai * kk
    tr = mag * jnp.cos(ang)
    ti = mag * jnp.sin(ang)
    tr_ref[...] = tr
    ti_ref[...] = ti
    abr, abi = tr[0:1], ti[0:1]
    den = ar * ar + ai * ai
    fr = ((abr - 1.0) * ar + abi * ai) / den
    fi = (abi * ar - (abr - 1.0) * ai) / den
    br, bi = br_ref[...], bi_ref[...]
    bbr_ref[...] = fr * br - fi * bi
    bbi_ref[...] = fr * bi + fi * br


def _ssm_params(A_re, A_im, log_dt, B_re, B_im, C_re, C_im):
    depth = A_re.shape[0]
    n = depth * SSM_GROUPS * SSM_STATE
    row = lambda t: t.astype(F32).reshape(1, n)
    ldt = jnp.broadcast_to(log_dt.astype(F32)[:, :, None], (depth, SSM_GROUPS, SSM_STATE)).reshape(1, n)
    kk = jnp.array(SSM_POW_ROWS + (0,) * (SUBLANES - len(SSM_POW_ROWS)), F32).reshape(SUBLANES, 1)
    chan_first = lambda t: t.astype(F32).reshape(n, SSM_GROUP_SIZE).T
    shapes = [(SUBLANES, n), (SUBLANES, n), (SSM_GROUP_SIZE, n), (SSM_GROUP_SIZE, n)]
    tr, ti, bbr, bbi = pl.pallas_call(
        _ssm_params_kernel,
        out_shape=[jax.ShapeDtypeStruct(s, F32) for s in shapes],
        name="ssm_params",
    )(row(A_re), row(A_im), ldt, kk, chan_first(B_re), chan_first(B_im))
    J, G8 = SSM_LANE_BLOCKS, LANES // SSM_GROUP_SIZE

    def lanes(t):
        return t.reshape(SUBLANES, depth, J, SSM_BLOCK_STATES).transpose(1, 0, 2, 3)

    pw = jnp.concatenate([lanes(tr), lanes(ti)], axis=-1).reshape(depth, SUBLANES, SSM_STATE_LANES)
    eye = jnp.eye(G8, dtype=F32)

    def bdiag(t):
        t = t.reshape(SSM_GROUP_SIZE, depth, J, G8, SSM_STATE).transpose(1, 2, 3, 0, 4)
        t = t[:, :, :, :, None, :] * eye[None, None, :, None, :, None]
        return t.reshape(depth, J, LANES, SSM_BLOCK_STATES)

    bblk = jnp.concatenate([bdiag(bbr), bdiag(bbi)], axis=-1).astype(BF16)

    def cdiag(t):
        t = t.astype(F32).reshape(depth, J, G8, SSM_GROUP_SIZE, SSM_STATE).transpose(0, 1, 2, 4, 3)
        t = t[:, :, :, :, None, :] * eye[None, None, :, None, :, None]
        return t.reshape(depth, J, SSM_BLOCK_STATES, LANES)

    cblk = jnp.concatenate([cdiag(C_re), -cdiag(C_im)], axis=2).astype(BF16)
    return pw, bblk, cblk


def _ssm_tail(y, u, d_ref, wg_ref, bg_ref, g_ref):
    y = y + d_ref[...] * u
    z = jax.nn.gelu(y)
    gate = jax.nn.sigmoid(jnp.dot(z.astype(BF16), wg_ref[...], preferred_element_type=F32) + bg_ref[...])
    return _rms(z * gate, g_ref[...])


def _ssm_sweep(x_scr, n, a_tabs, init, store):
    S = SSM_BLOCK_STATES
    fins = []
    for j0 in range(0, SSM_LANE_BLOCKS, 2):
        js = (j0, j0 + 1)

        def body(k, carry, js=js):
            r0 = pl.multiple_of(k * SUBLANES, SUBLANES)
            out = []
            for idx, j in enumerate(js):
                hr, hi = carry[2 * idx], carry[2 * idx + 1]
                base = j * 2 * S
                ar, ai = a_tabs[j]
                nhr = ar * hr - ai * hi + x_scr[pl.ds(r0, SUBLANES), base:base + S]
                nhi = ar * hi + ai * hr + x_scr[pl.ds(r0, SUBLANES), base + S:base + 2 * S]
                if store:
                    x_scr[pl.ds(r0, SUBLANES), base:base + S] = nhr
                    x_scr[pl.ds(r0, SUBLANES), base + S:base + 2 * S] = nhi
                out += [nhr, nhi]
            return tuple(out)

        c0 = tuple(t for j in js for t in init[j])
        res = jax.lax.fori_loop(0, n // SUBLANES, body, c0, unroll=True)
        fins += [(res[0], res[1]), (res[2], res[3])]
    return fins


def _ssm_rows(u, pow_row, pw_ref, bblk_ref, cblk_ref, x_scr, s_scr, carry_scr):
    n = u.shape[0]
    S = SSM_BLOCK_STATES
    ub = u.astype(BF16)
    for j in range(SSM_LANE_BLOCKS):
        x_scr[0:n, j * 2 * S:(j + 1) * 2 * S] = jnp.dot(ub[:, j * LANES:(j + 1) * LANES], bblk_ref[j],
                                                       preferred_element_type=F32)
    bc = lambda t: jnp.broadcast_to(t, (SUBLANES, S))
    re = lambda ref, r0, r1, j: ref[r0:r1, j * 2 * S:j * 2 * S + S]
    im = lambda ref, r0, r1, j: ref[r0:r1, j * 2 * S + S:(j + 1) * 2 * S]
    a_tabs = [(bc(re(pw_ref, 0, 1, j)), bc(im(pw_ref, 0, 1, j))) for j in range(SSM_LANE_BLOCKS)]
    zero = jnp.zeros((SUBLANES, S), F32)
    fins = _ssm_sweep(x_scr, n, a_tabs, [(zero, zero)] * SSM_LANE_BLOCKS, store=False)
    for j in range(SSM_LANE_BLOCKS):
        base = j * 2 * S
        cr, ci = re(pw_ref, pow_row, pow_row + 1, j), im(pw_ref, pow_row, pow_row + 1, j)
        sr, si = re(carry_scr, 0, 1, j), im(carry_scr, 0, 1, j)
        fr, fi = fins[j]
        for c in range(SUBLANES):
            s_scr[c:c + 1, base:base + S] = sr
            s_scr[c:c + 1, base + S:base + 2 * S] = si
            sr, si = cr * sr - ci * si + fr[c:c + 1], cr * si + ci * sr + fi[c:c + 1]
        carry_scr[:, base:base + S] = bc(sr)
        carry_scr[:, base + S:base + 2 * S] = bc(si)
    init = [(re(s_scr, 0, SUBLANES, j), im(s_scr, 0, SUBLANES, j)) for j in range(SSM_LANE_BLOCKS)]
    _ssm_sweep(x_scr, n, a_tabs, init, store=True)
    ys = [jnp.dot(x_scr[0:n, j * 2 * S:(j + 1) * 2 * S].astype(BF16), cblk_ref[j], preferred_element_type=F32)
          for j in range(SSM_LANE_BLOCKS)]
    return jnp.concatenate(ys, axis=1)


def _ssm_tile(u_ref, n, pow_row, refs, o_ref, scr):
    pw_ref, bblk_ref, cblk_ref, d_ref, wg_ref, bg_ref, g_ref = refs
    up_scr, x_scr, s_scr, carry_scr, o_scr = scr
    q = n // SUBLANES
    for j in range(SSM_LANE_BLOCKS):
        for k in range(q):
            up_scr[k * SUBLANES:(k + 1) * SUBLANES, j * LANES:(j + 1) * LANES] = \
                u_ref[j, pl.ds(k, SUBLANES, stride=q), :]
    u = up_scr[0:n, :]
    y = _ssm_rows(u, pow_row, pw_ref, bblk_ref, cblk_ref, x_scr, s_scr, carry_scr)
    out = _ssm_tail(y, u, d_ref, wg_ref, bg_ref, g_ref)
    for j in range(SSM_LANE_BLOCKS):
        for k in range(q):
            o_scr[j, pl.ds(k, SUBLANES, stride=q), :] = out[k * SUBLANES:(k + 1) * SUBLANES,
                                                            j * LANES:(j + 1) * LANES]
    o_ref[...] = jnp.concatenate([o_scr[j, 0:n, :] for j in range(SSM_LANE_BLOCKS)], axis=1).astype(o_ref.dtype)


def _mixers_kernel(sink_ref,
                   qm_ref, qh_ref, kpm_ref, km_ref, kh_ref, vpm_ref, vm_ref, vh_ref, ga_ref, bias_ref,
                   um_ref, uh_ref, pw_ref, bblk_ref, cblk_ref, d_ref, wg_ref, bg_ref, gs_ref,
                   xm_ref, halo_ref, xh_ref, wp_ref, sc_ref, gp_ref, *rest, l, n_cast, n_dec):
    cast_in = rest[0:n_cast]
    am_ref, ah_ref, som_ref, soh_ref, st_ref, pom_ref, poh_ref = rest[n_cast:n_cast + 7]
    cast_out = rest[n_cast + 7:2 * n_cast + 7]
    scr = rest[2 * n_cast + 7:]
    carry_scr = scr[3]
    b = pl.program_id(0)
    t = pl.program_id(1)
    tm = xm_ref.shape[0]
    ssm_refs = (pw_ref, bblk_ref, cblk_ref, d_ref, wg_ref, bg_ref, gs_ref)
    blk = functools.partial(_attn_block, sink_ref=sink_ref, l=l, g=ga_ref[...])
    meta_rows = pl.ds(pl.multiple_of(n_dec + b * N_META, N_META), N_META)

    @pl.when((b == 0) & (t == 0))
    def _():
        for ref in (ah_ref, soh_ref, poh_ref):
            ref[0:n_dec, :] = jnp.zeros((n_dec, ref.shape[1]), ref.dtype)

    block = lambda ref: jnp.concatenate([jnp.zeros((SEQ0, ref.shape[1]), F32), ref[...]], axis=0)

    @pl.when(t == 0)
    def _():
        a = blk(block(qh_ref), kpm_ref[...], block(kh_ref), vpm_ref[...], block(vh_ref), bias_ref[0])
        ah_ref[meta_rows, :] = a[SEQ0:].astype(ah_ref.dtype)
        carry_scr[...] = jnp.zeros_like(carry_scr)
        _ssm_tile(uh_ref, N_META, POW_ROW_HEAD, ssm_refs, soh_ref.at[meta_rows, :], scr)
        prev = jnp.zeros((POOL_HALO, POOL_WIDTH), F32)
        poh_ref[meta_rows, :] = _pool_rows(xh_ref[...], prev, 0, wp_ref, sc_ref, gp_ref).astype(poh_ref.dtype)

    @pl.when(t > 0)
    def _():
        for src, dst in zip(cast_in, cast_out):
            dst[...] = src[...].astype(BF16)
        kp = jnp.where(t == 1, block(kh_ref), kpm_ref[...])
        vp = jnp.where(t == 1, block(vh_ref), vpm_ref[...])
        for n in range(tm // BLOCK):
            rows = slice(n * BLOCK, (n + 1) * BLOCK)
            kc, vc = km_ref[rows, :], vm_ref[rows, :]
            bias = bias_ref[jnp.minimum(t, 2)] if n == 0 else bias_ref[2]
            am_ref[rows, :] = blk(qm_ref[rows, :], kp, kc, vp, vc, bias).astype(am_ref.dtype)
            kp, vp = kc, vc
        _ssm_tile(um_ref, um_ref.shape[1], POW_ROW_MAIN, ssm_refs, som_ref, scr)
        st_ref[...] = carry_scr[...]
        prev = jnp.where(t == 1, xh_ref[...], halo_ref[...])
        pom_ref[...] = _pool_rows(xm_ref[...], prev, N_META + (t - 1) * tm, wp_ref, sc_ref, gp_ref).astype(pom_ref.dtype)


def _mixers(sinks, qkv_m, qkv_h, um, uh, xm, xh, vecs, bias, pw, bblk, cblk, wg, wp, l, B, tm, casts):
    assert (1, N_META // SUBLANES, tm // SUBLANES) == SSM_POW_ROWS and N_META == POOL_HALO
    J = SSM_LANE_BLOCKS
    rm, rh = um.shape[1], uh.shape[1]
    nt = rm // (B * tm)
    r = tm // POOL_HALO
    bpt = tm // BLOCK
    tile = lambda b, t: b * nt + jnp.maximum(t - 1, 0)
    main = lambda b, t, _: (tile(b, t), 0)
    n_dec = uh.shape[1] - B * N_META
    head = lambda b, t, _: (n_dec // N_META + b, 0)
    halo = lambda b, t, _: (jnp.maximum((b * nt + t - 1) * r - 1, 0), 0)
    prevb = lambda b, t, _: (jnp.maximum(tile(b, t) * bpt - 1, 0), 0)
    cast_specs = []
    for w, wl in casts:
        _, cr, cc = w.shape
        rows = cr // (B * nt)
        cast_specs.append((pl.BlockSpec((None, rows, cc), lambda b, t, _, wl=wl: (wl, tile(b, t), 0)),
                           pl.BlockSpec((rows, cc), main), jax.ShapeDtypeStruct((cr, cc), BF16)))
    bf = lambda n, w: jax.ShapeDtypeStruct((n, w), BF16)
    tile_spec = lambda w: pl.BlockSpec((tm, w), main)
    head_spec = lambda w: pl.BlockSpec((N_META, w), head)
    in_specs = [tile_spec(ATTN_WIDTH), head_spec(ATTN_WIDTH),
                pl.BlockSpec((BLOCK, KV_WIDTH), prevb), tile_spec(KV_WIDTH), head_spec(KV_WIDTH),
                pl.BlockSpec((BLOCK, KV_WIDTH), prevb), tile_spec(KV_WIDTH), head_spec(KV_WIDTH),
                _vrow("g_out_attn", l), _full(bias.shape),
                pl.BlockSpec((J, tm, LANES), lambda b, t, _: (0, tile(b, t), 0)),
                pl.BlockSpec((J, N_META, LANES), lambda b, t, _: (0, n_dec // N_META + b, 0)),
                _layer(pw.shape[1:], l), _layer(bblk.shape[1:], l), _layer(cblk.shape[1:], l),
                _vrow("D_skip", l), _layer((SSM_WIDTH, SSM_WIDTH), l), _vrow("b_glu", l), _vrow("g_out_ssm", l),
                tile_spec(POOL_WIDTH), pl.BlockSpec((POOL_HALO, POOL_WIDTH), halo), head_spec(POOL_WIDTH),
                _layer(wp.shape[1:], l), _vrow("pool_scale", l), _vrow("g_out_pool", l)]
    in_specs += [c[0] for c in cast_specs]
    head_out = lambda w: pl.BlockSpec((rh, w), lambda b, t, _: (0, 0))
    out_specs = [tile_spec(ATTN_WIDTH), head_out(ATTN_WIDTH), tile_spec(SSM_WIDTH), head_out(SSM_WIDTH),
                 pl.BlockSpec((None, SUBLANES, SSM_STATE_LANES), lambda b, t, _: (b, 0, 0)),
                 tile_spec(POOL_WIDTH), head_out(POOL_WIDTH)] + [c[1] for c in cast_specs]
    out_shape = [bf(rm, ATTN_WIDTH), bf(rh, ATTN_WIDTH), bf(rm, SSM_WIDTH), bf(rh, SSM_WIDTH),
                 jax.ShapeDtypeStruct((B, SUBLANES, SSM_STATE_LANES), F32),
                 bf(rm, POOL_WIDTH), bf(rh, POOL_WIDTH)] + [c[2] for c in cast_specs]
    (qm, km, vm), (qh, kh, vh) = qkv_m, qkv_h
    res = pl.pallas_call(
        functools.partial(_mixers_kernel, l=l, n_cast=len(casts), n_dec=n_dec),
        grid_spec=pltpu.PrefetchScalarGridSpec(
            num_scalar_prefetch=1, grid=(B, nt + 1), in_specs=in_specs, out_specs=out_specs,
            scratch_shapes=[pltpu.VMEM((tm, SSM_WIDTH), F32),
                            pltpu.VMEM((tm, SSM_STATE_LANES), F32),
                            pltpu.VMEM((SUBLANES, SSM_STATE_LANES), F32),
                            pltpu.VMEM((SUBLANES, SSM_STATE_LANES), F32),
                            pltpu.VMEM((J, tm, LANES), F32)]),
        out_shape=out_shape,
        compiler_params=_params("arbitrary", "arbitrary"),
        name="mixers",
    )(sinks, qm, qh, km, km, kh, vm, vm, vh, vecs, bias, um, uh, pw, bblk, cblk, vecs, wg, vecs, vecs,
      xm, xm, xh, wp, vecs, vecs, *[w for w, _ in casts])
    return res[0:7], res[7:]


def _state_from_lanes(s):
    s = s.reshape(s.shape[0], SSM_LANE_BLOCKS, 2, SSM_BLOCK_STATES)
    return (s[:, :, 0].reshape(-1, SSM_GROUPS, SSM_STATE), s[:, :, 1].reshape(-1, SSM_GROUPS, SSM_STATE))


def _pool_tail(d_groups, w_ref, sc_ref, g_ref):
    y = jnp.concatenate(
        [jnp.dot(d.astype(BF16), w_ref[gi], preferred_element_type=F32) for gi, d in enumerate(d_groups)], axis=1)
    return _rms(y * sc_ref[...], g_ref[...])


def _pool_rows(x, prev, pos0, w_ref, sc_ref, g_ref):
    n = x.shape[0]
    xe = jnp.concatenate([prev, x], axis=0)
    pos = pos0 + jax.lax.broadcasted_iota(jnp.int32, (n, 1), 0)
    ds = []
    for gi, w in enumerate(POOL_WINDOWS):
        gsl = slice(gi * POOL_GROUP, (gi + 1) * POOL_GROUP)
        s = xe[:, gsl]
        k = 1
        while k < w:
            s = s + pltpu.roll(s, k, 0)
            k *= 2
        cnt = jnp.clip(pos + 1, 1, w).astype(F32)
        ds.append(s[POOL_HALO:] / cnt - x[:, gsl])
    return _pool_tail(ds, w_ref, sc_ref, g_ref)


def _attn_sample_kernel(q_ref, kn_ref, vn_ref, kc_ref, vc_ref, kd_ref, vd_ref, sink_ref, g_ref, *rest, first):
    a_ref, nk_ref, nv_ref, acc_scr = rest[-4:]
    step = pl.program_id(1 if first else 0)
    nsteps = pl.num_programs(1 if first else 0)
    slot = WINDOW - 1 if first else 0
    scale = HEAD_DIM ** -0.5
    if first:
        for bb in range(DEC_STEP):
            nk_ref[bb, 0:WINDOW - 1] = kc_ref[bb, 1:WINDOW]
            nv_ref[bb, 0:WINDOW - 1] = vc_ref[bb, 1:WINDOW]
            nk_ref[bb, WINDOW - 1] = jnp.zeros((N_KV_HEADS, HEAD_DIM), F32)
            nv_ref[bb, WINDOW - 1] = jnp.zeros((N_KV_HEADS, HEAD_DIM), F32)

    def attend():
        _attn_sample_rows(q_ref, kn_ref, vn_ref, kd_ref, vd_ref, sink_ref, g_ref, nk_ref, nv_ref, acc_scr,
                          step, slot, scale)

        @pl.when(step == nsteps - 1)
        def _():
            a_ref[...] = acc_scr[...].astype(a_ref.dtype)

    if first:
        pl.when(pl.program_id(0) == 0)(attend)
    else:
        attend()


def _attn_sample_rows(q_ref, kn_ref, vn_ref, kc_ref, vc_ref, sink_ref, g_ref, nk_ref, nv_ref, acc_scr,
                      step, slot, scale):
    ncols = WINDOW * N_KV_HEADS
    nrows = DEC_STEP * N_HEADS
    head = jax.lax.broadcasted_iota(jnp.int32, (nrows, ncols), 0) % N_HEADS
    col = jax.lax.broadcasted_iota(jnp.int32, (nrows, ncols), 1)
    own = (col % N_KV_HEADS) == (head // GQA_GROUP)
    sk = jnp.concatenate([sink_ref[kh] for kh in range(N_KV_HEADS)] * DEC_STEP, axis=0)
    per_head = lambda rows: [jnp.broadcast_to(r, (GQA_GROUP, HEAD_DIM)) for r in rows]
    qs, knh, vnh = [], [], []
    for bb in range(DEC_STEP):
        qs.append(jnp.concatenate(
            [q_ref[bb:bb + 1, h * HEAD_DIM:(h + 1) * HEAD_DIM] for h in range(N_HEADS)], axis=0))
        kn = [kn_ref[bb:bb + 1, kh * HEAD_DIM:(kh + 1) * HEAD_DIM] for kh in range(N_KV_HEADS)]
        vn = [vn_ref[bb:bb + 1, kh * HEAD_DIM:(kh + 1) * HEAD_DIM] for kh in range(N_KV_HEADS)]
        for kh in range(N_KV_HEADS):
            nk_ref[bb, slot, kh:kh + 1, :] = kn[kh]
            nv_ref[bb, slot, kh:kh + 1, :] = vn[kh]
        knh += per_head(kn)
        vnh += per_head(vn)
    sc = jnp.concatenate(
        [jax.lax.dot_general(qs[bb].astype(BF16), kc_ref[bb].astype(BF16), (((1,), (1,)), ((), ())),
                             preferred_element_type=F32) for bb in range(DEC_STEP)], axis=0) * scale
    sc = jnp.where(own, sc, -jnp.inf)
    q = jnp.concatenate(qs, axis=0)
    sn = jnp.sum(q * jnp.concatenate(knh, axis=0), axis=-1, keepdims=True) * scale
    m = jnp.maximum(jnp.maximum(jnp.max(sc, axis=-1, keepdims=True), sn), sk)
    pc = jnp.exp(sc - m)
    pn = jnp.exp(sn - m)
    denom = jnp.sum(pc, axis=-1, keepdims=True) + pn + jnp.exp(sk - m)
    o = jnp.concatenate(
        [jnp.dot(pc[bb * N_HEADS:(bb + 1) * N_HEADS].astype(BF16), vc_ref[bb].astype(BF16),
                 preferred_element_type=F32) for bb in range(DEC_STEP)], axis=0)
    o = (o + pn * jnp.concatenate(vnh, axis=0)) / denom
    a = jnp.concatenate(
        [jnp.concatenate([o[bb * N_HEADS + h:bb * N_HEADS + h + 1] for h in range(N_HEADS)], axis=1)
         for bb in range(DEC_STEP)], axis=0)
    acc_scr[pl.ds(pl.multiple_of(step * DEC_STEP, DEC_STEP), DEC_STEP), :] = _rms(a, g_ref[...])


def _attn_sample(qh, kh, vh, cache_k, cache_v, l, sinks, g, ah, nk, nv, N):
    depth = cache_k.shape[0]
    first = nk is None
    anyspec = pl.BlockSpec(memory_space=pl.ANY)
    cshape = jax.ShapeDtypeStruct(cache_k.shape, F32)
    slab = (None, DEC_STEP, WINDOW, N_KV_HEADS, HEAD_DIM)
    dense_slab = (None, DEC_STEP, WINDOW * N_KV_HEADS, HEAD_DIM)
    dense_view = lambda c: c.reshape(depth, N, WINDOW * N_KV_HEADS, HEAD_DIM)
    nsteps = N // DEC_STEP
    if first:
        assert l == 0
        grid = (depth, nsteps)
        rows = lambda w: pl.BlockSpec((DEC_STEP, w), lambda ll, s: (s, 0))
        cache = pl.BlockSpec(slab, lambda ll, s: (ll, s, 0, 0, 0))
        dense = pl.BlockSpec(dense_slab, lambda ll, s: (0, jnp.where(ll == 0, s, nsteps - 1), 0, 0))
        new = cache
        a_spec = pl.BlockSpec((N, ATTN_WIDTH), lambda ll, s: (0, 0))
        carried = []
    else:
        grid = (nsteps,)
        rows = lambda w: pl.BlockSpec((DEC_STEP, w), lambda s: (s, 0))
        cache = anyspec
        dense = pl.BlockSpec(dense_slab, lambda s: (l, s, 0, 0))
        new = pl.BlockSpec((None, DEC_STEP, 1, N_KV_HEADS, HEAD_DIM), lambda s: (l, s, WINDOW - 1, 0, 0))
        a_spec = pl.BlockSpec((N, ATTN_WIDTH), lambda s: (0, 0))
        carried = [nk, nv]
    n_in = 9
    return pl.pallas_call(
        functools.partial(_attn_sample_kernel, first=first),
        grid=grid,
        in_specs=[rows(ATTN_WIDTH), rows(KV_WIDTH), rows(KV_WIDTH), cache, cache, dense, dense,
                  _layer((N_KV_HEADS, GQA_GROUP, 1), l), _vrow("g_out_attn", l)] + [anyspec] * (1 + len(carried)),
        out_specs=[a_spec, new, new],
        out_shape=[jax.ShapeDtypeStruct(ah.shape, ah.dtype), cshape, cshape],
        scratch_shapes=[pltpu.VMEM((N, ATTN_WIDTH), F32)],
        input_output_aliases={n_in + i: i for i in range(1 + len(carried))},
        compiler_params=_params(*(["arbitrary"] * len(grid))),
        name="attn_sample",
    )(qh, kh, vh, cache_k, cache_v, dense_view(cache_k), dense_view(cache_v), sinks, g, ah, *carried)


def _mix_sample_kernel(u_ref, h0_ref, pw_ref, bblk_ref, cblk_ref, d_ref, wg_ref, bg_ref, gs_ref,
                       xp_ref, pb_ref, wp_ref, sc_ref, gp_ref, s_in_ref, p_in_ref, s_ref, st_ref, p_ref):
    del s_in_ref, p_in_ref
    S = SSM_BLOCK_STATES
    u = jnp.concatenate([u_ref[j] for j in range(SSM_LANE_BLOCKS)], axis=1)
    ub = u.astype(BF16)
    ys = []
    for j in range(SSM_LANE_BLOCKS):
        x = jnp.dot(ub[:, j * LANES:(j + 1) * LANES], bblk_ref[j], preferred_element_type=F32)
        base = j * 2 * S
        ar = pw_ref[0:1, base:base + S]
        ai = pw_ref[0:1, base + S:base + 2 * S]
        h0r = h0_ref[:, base:base + S]
        h0i = h0_ref[:, base + S:base + 2 * S]
        hr = x[:, 0:S] + ar * h0r - ai * h0i
        hi = x[:, S:] + ar * h0i + ai * h0r
        st_ref[:, base:base + S] = hr
        st_ref[:, base + S:base + 2 * S] = hi
        h = jnp.concatenate([hr, hi], axis=1).astype(BF16)
        ys.append(jnp.dot(h, cblk_ref[j], preferred_element_type=F32))
    s_ref[...] = _ssm_tail(jnp.concatenate(ys, axis=1), u, d_ref, wg_ref, bg_ref, gs_ref).astype(s_ref.dtype)

    xp = xp_ref[...]
    ds = []
    for gi, w in enumerate(POOL_WINDOWS):
        gsl = slice(gi * POOL_GROUP, (gi + 1) * POOL_GROUP)
        s = xp[:, gsl]
        for back in range(1, w):
            s = s + pb_ref[POOL_BUF - back][:, gsl]
        ds.append(s / float(w) - xp[:, gsl])
    p_ref[...] = _pool_tail(ds, wp_ref, sc_ref, gp_ref).astype(p_ref.dtype)


def _mix_sample(uh, xph, h0, pbuf, vecs, pw, bblk, cblk, wg, wp, l, sh, ph):
    N = h0.shape[1]
    rows = lambda w: pl.BlockSpec((N, w), lambda i: (0, 0))
    anyspec = pl.BlockSpec(memory_space=pl.ANY)
    return pl.pallas_call(
        _mix_sample_kernel,
        grid=(1,),
        in_specs=[pl.BlockSpec((SSM_LANE_BLOCKS, N, LANES), lambda i: (0, 0, 0)),
                  _layer(h0.shape[1:], l), _layer(pw.shape[1:], l), _layer(bblk.shape[1:], l),
                  _layer(cblk.shape[1:], l),
                  _vrow("D_skip", l), _layer((SSM_WIDTH, SSM_WIDTH), l), _vrow("b_glu", l),
                  _vrow("g_out_ssm", l), rows(POOL_WIDTH), _layer(pbuf.shape[1:], l),
                  _layer(wp.shape[1:], l),
                  _vrow("pool_scale", l), _vrow("g_out_pool", l), anyspec, anyspec],
        out_specs=[rows(SSM_WIDTH), _full((N, SSM_STATE_LANES)), rows(POOL_WIDTH)],
        out_shape=[jax.ShapeDtypeStruct(sh.shape, sh.dtype), jax.ShapeDtypeStruct((N, SSM_STATE_LANES), F32),
                   jax.ShapeDtypeStruct(ph.shape, ph.dtype)],
        input_output_aliases={14: 0, 15: 2},
        compiler_params=_params("arbitrary"),
        name="mix_sample",
    )(uh, h0, pw, bblk, cblk, vecs, wg, vecs, vecs, xph, pbuf, wp, vecs, vecs, sh, ph)


def _outproj_rows(x_ref, a_ref, s_ref, p_ref, w_ref, o_ref):
    o1 = ATTN_WIDTH
    o2 = o1 + SSM_WIDTH
    acc = x_ref[...]
    acc = acc + jnp.dot(a_ref[...], w_ref[0:o1, :], preferred_element_type=F32)
    acc = acc + jnp.dot(s_ref[...], w_ref[o1:o2, :], preferred_element_type=F32)
    acc = acc + jnp.dot(p_ref[...], w_ref[o2:, :], preferred_element_type=F32)
    o_ref[...] = acc


def _outproj_kernel(xm_ref, am_ref, sm_ref, pm_ref, xh_ref, ah_ref, sh_ref, ph_ref, w_ref, om_ref, oh_ref):
    i = pl.program_id(0)
    last = pl.num_programs(0) - 1

    @pl.when(i < last)
    def _():
        _outproj_rows(xm_ref, am_ref, sm_ref, pm_ref, w_ref, om_ref)

    @pl.when(i == last)
    def _():
        _outproj_rows(xh_ref, ah_ref, sh_ref, ph_ref, w_ref, oh_ref)


def _outproj(xm, am, sm, pm, xh, ah, sh, ph, w, tm):
    R, RH = xm.shape[0], xh.shape[0]
    nm = R // tm
    row = lambda i: (jnp.minimum(i, nm - 1), 0)
    once = pl.Buffered(1)
    head = lambda w_: pl.BlockSpec((RH, w_), lambda i: (0, 0), pipeline_mode=once)
    return pl.pallas_call(
        _outproj_kernel,
        grid=(nm + 1,),
        in_specs=[pl.BlockSpec((tm, D_MODEL), row), pl.BlockSpec((tm, ATTN_WIDTH), row),
                  pl.BlockSpec((tm, SSM_WIDTH), row), pl.BlockSpec((tm, POOL_WIDTH), row),
                  head(D_MODEL), head(ATTN_WIDTH), head(SSM_WIDTH), head(POOL_WIDTH),
                  pl.BlockSpec((D_MODEL, D_MODEL), lambda i: (0, 0), pipeline_mode=once)],
        out_specs=[pl.BlockSpec((tm, D_MODEL), row), pl.BlockSpec((RH, D_MODEL), lambda i: (0, 0))],
        out_shape=[jax.ShapeDtypeStruct((R, D_MODEL), F32), jax.ShapeDtypeStruct((RH, D_MODEL), F32)],
        compiler_params=_params("arbitrary"),
        name="outproj",
    )(xm, am, sm, pm, xh, ah, sh, ph, w)


def _ffn_kernel(xm_ref, xh_ref, g_ref, w1_ref, w2_ref, om_ref, oh_ref, hm_scr, hh_scr):
    i = pl.program_id(0)
    f = pl.program_id(1)

    def mlp(h):
        h1 = jnp.dot(h, w1_ref[...], preferred_element_type=F32)
        return jnp.dot(jnp.square(jnp.maximum(h1, 0.0)).astype(BF16), w2_ref[...], preferred_element_type=F32)

    def first(x_ref, h_scr, o_ref):
        x = x_ref[...]
        h = _rms(x, g_ref[...]).astype(BF16)
        h_scr[...] = h
        o_ref[...] = x + mlp(h)

    @pl.when(f == 0)
    def _():
        first(xm_ref, hm_scr, om_ref)

    @pl.when(f > 0)
    def _():
        om_ref[...] += mlp(hm_scr[...])

    @pl.when((i == 0) & (f == 0))
    def _():
        first(xh_ref, hh_scr, oh_ref)

    @pl.when((i == 0) & (f > 0))
    def _():
        oh_ref[...] += mlp(hh_scr[...])


def _ffn(xm, xh, g, l, w1, w2, tm, tf):
    R, RH = xm.shape[0], xh.shape[0]
    return pl.pallas_call(
        _ffn_kernel,
        grid=(R // tm, D_FF // tf),
        in_specs=[pl.BlockSpec((tm, D_MODEL), lambda i, f: (i, 0)),
                  pl.BlockSpec((RH, D_MODEL), lambda i, f: (0, 0), pipeline_mode=pl.Buffered(1)),
                  _vrow("g_ffn", l),
                  pl.BlockSpec((D_MODEL, tf), lambda i, f: (0, f)),
                  pl.BlockSpec((tf, D_MODEL), lambda i, f: (f, 0))],
        out_specs=[pl.BlockSpec((tm, D_MODEL), lambda i, f: (i, 0)),
                   pl.BlockSpec((RH, D_MODEL), lambda i, f: (0, 0))],
        out_shape=[jax.ShapeDtypeStruct((R, D_MODEL), F32), jax.ShapeDtypeStruct((RH, D_MODEL), F32)],
        scratch_shapes=[pltpu.VMEM((tm, D_MODEL), BF16), pltpu.VMEM((RH, D_MODEL), BF16)],
        compiler_params=_params("arbitrary", "arbitrary"),
        name="ffn",
    )(xm, xh, g, w1, w2)


def _rope_tables(pos):
    half = ROT_HALF
    inv = ROPE_THETA ** (-np.arange(0, ROT_DIM, 2, dtype=np.float64) / ROT_DIM)
    ang = np.asarray(pos, np.float64)[:, None] * inv
    cos, sin = np.cos(ang), np.sin(ang)
    n = ang.shape[0]
    z = np.zeros((n, HEAD_DIM - ROT_DIM))
    zh = np.zeros((n, half))
    rc = np.concatenate([cos, cos, z + 1.0], axis=1)
    rs1 = np.concatenate([zh, sin, z], axis=1)
    rs2 = np.concatenate([-sin, zh, z], axis=1)
    return tuple(t.astype(np.float32) for t in (rc, rs1, rs2))


def kernel(x_prompt, x_sample, cache_k, cache_v, state_ssm_re, state_ssm_im, state_pool, meta_tokens, g_mix, w_in, g_q, g_k, sinks, A_re, A_im, log_dt, B_re, B_im, C_re, C_im, D_skip, w_glu, b_glu, w_pool, pool_scale, g_out_attn, g_out_ssm, g_out_pool, w_out, g_ffn, w_ff1, w_ff2):
    B, T, _ = x_prompt.shape
    N = x_sample.shape[0]
    depth = w_in.shape[0]
    assert N % N_META == 0 and N % DEC_STEP == 0 and T % TM_FFN == 0
    xh = jnp.concatenate([x_sample.reshape(N, D_MODEL)] + [meta_tokens.astype(F32)] * B, axis=0)
    xm = x_prompt.reshape(B * T, D_MODEL)

    rope_m = _rope_tables(N_META + np.arange(T))
    rope_h = _rope_tables(np.concatenate([np.full(N, PAST_LEN)] + [np.arange(N_META)] * B))

    wi = w_in[0].astype(BF16)
    wg_all, wp_all = w_glu.astype(BF16), w_pool.astype(BF16)
    ck = cache_k.astype(F32)
    cv = cache_v.astype(F32)

    named = dict(g_mix=g_mix, g_ffn=g_ffn, g_out_attn=g_out_attn, D_skip=D_skip, b_glu=b_glu,
                 g_out_ssm=g_out_ssm, pool_scale=pool_scale, g_out_pool=g_out_pool, g_q=g_q, g_k=g_k)
    vecs = jnp.concatenate([named[name].astype(F32) for name, _ in VEC_LAYOUT], axis=1)[:, None, :]
    pw, bblk, cblk = _ssm_params(A_re, A_im, log_dt, B_re, B_im, C_re, C_im)
    sinks_flat = sinks.astype(F32).reshape(depth * N_HEADS)
    sinks_col = sinks.astype(F32).reshape(depth, N_KV_HEADS, GQA_GROUP, 1)
    bias = _attn_bias()
    h0 = jnp.concatenate([state_ssm_re.astype(F32).reshape(depth, N, SSM_LANE_BLOCKS, SSM_BLOCK_STATES),
                          state_ssm_im.astype(F32).reshape(depth, N, SSM_LANE_BLOCKS, SSM_BLOCK_STATES)],
                         axis=-1).reshape(depth, N, SSM_STATE_LANES)
    pbuf = state_pool.astype(F32).transpose(0, 2, 1, 3)

    nk = nv = None
    ks, vs, pls, sts, st_ss, phs = ([] for _ in range(6))
    for l in range(depth):
        (qm, km, vm, um, pm), (qh, kh, vh, uh, ph), w1 = _inproj(xm, xh, vecs, wi, l, rope_m, rope_h, TM_PROJ, w_ff1)

        (am, ah, sm, sh, st, plm, plh), wcast = _mixers(
            sinks_flat, (qm, km, vm), (qh, kh, vh), um, uh, pm, ph, vecs, bias, pw, bblk, cblk, wg_all, wp_all,
            l, B, TM_SEQ, [(w_ff2, l), (w_out, l)] + ([(w_in, l + 1)] if l + 1 < depth else []))
        w2, wo = wcast[0], wcast[1]

        ah, nk, nv = _attn_sample(qh, kh, vh, ck, cv, l, sinks_col, vecs, ah, nk, nv, N)
        sh, st_s, plh = _mix_sample(uh, ph, h0, pbuf, vecs, pw, bblk, cblk, wg_all, wp_all, l, sh, plh)

        xm, xh = _outproj(xm, am, sm, plm, xh, ah, sh, plh, wo, TM_PROJ)
        xm, xh = _ffn(xm, xh, vecs, l, w1, w2, TM_FFN, TF_FFN)
        if l + 1 < depth:
            wi = wcast[2]

        ks.append(km.reshape(B, T, KV_WIDTH)[:, T - WINDOW:])
        vs.append(vm.reshape(B, T, KV_WIDTH)[:, T - WINDOW:])
        pls.append(pm.reshape(B, T, POOL_WIDTH)[:, T - POOL_BUF:])
        sts.append(st[:, 0])
        st_ss.append(st_s)
        phs.append(ph[:N])

    y_prompt = xm.reshape(B, T, D_MODEL)
    y_sample = xh[:N].reshape(N, 1, D_MODEL)
    heads = lambda t: jnp.stack(t).reshape(depth, -1, WINDOW, N_KV_HEADS, HEAD_DIM)
    p_re, p_im = _state_from_lanes(jnp.stack(sts).reshape(depth * B, SSM_STATE_LANES))
    s_re, s_im = _state_from_lanes(jnp.stack(st_ss).reshape(depth * N, SSM_STATE_LANES))
    st4 = lambda t, n: t.reshape(depth, n, SSM_GROUPS, SSM_STATE)
    s_pool = jnp.concatenate([state_pool.astype(F32)[:, :, 1:], jnp.stack(phs)[:, :, None]], axis=2)
    return (y_prompt, y_sample, heads(ks), heads(vs), st4(p_re, B), st4(p_im, B), jnp.stack(pls),
            nk, nv, st4(s_re, N), st4(s_im, N), s_pool)
```

```python
import functools
import math

import jax
import jax.numpy as jnp
import numpy as np
from jax.experimental import pallas as pl
from jax.experimental.pallas import tpu as pltpu

D_MODEL = 2048
N_META = 16
HEAD_DIM = 128
N_HEADS = 8
N_KV_HEADS = 2
GQA_GROUP = 4
ATTN_WIDTH = 1024
KV_WIDTH = 256
WINDOW = 128
BLOCK = 128
ROT_DIM = 32
ROT_HALF = ROT_DIM // 2
ROPE_THETA = 500000.0
SSM_WIDTH = 512
SSM_GROUP_SIZE = 16
SSM_GROUPS = 32
SSM_STATE = 64
POOL_WIDTH = 512
POOL_WINDOWS = (2, 4, 8, 16)
POOL_GROUP = 128
POOL_BUF = 15
POOL_HALO = 16
IN_WIDTH = 2560
D_FF = 8192
EPS = 1e-6
PAST_LEN = 16384
LOG2E = math.log2(math.e)

SEQ0 = BLOCK - N_META
LANES = 128
SUBLANES = 8
SSM_LANE_BLOCKS = SSM_WIDTH // LANES
SSM_BLOCK_STATES = (LANES // SSM_GROUP_SIZE) * SSM_STATE
SSM_STATE_LANES = SSM_LANE_BLOCKS * 2 * SSM_BLOCK_STATES
VMEM_LIMIT = 60 * 1024 * 1024

TM_PROJ = 512
TM_FFN = 1024
TF_FFN = 512
TM_SEQ = 512
DEC_STEP = 8
SSM_POW_ROWS = (1, N_META // SUBLANES, TM_SEQ // SUBLANES)
POW_ROW_HEAD, POW_ROW_MAIN = 1, 2

BF16 = jnp.bfloat16
F32 = jnp.float32


def _params(*semantics):
    return pltpu.CompilerParams(dimension_semantics=semantics, vmem_limit_bytes=VMEM_LIMIT)


def _rms(x, g):
    return x * jax.lax.rsqrt(jnp.mean(x * x, axis=-1, keepdims=True) + EPS) * g


def _full(shape):
    n = len(shape)
    return pl.BlockSpec(shape, lambda *_: (0,) * n)


def _layer(shape, l):
    n = len(shape)
    return pl.BlockSpec((None, *shape), lambda *_: (l,) + (0,) * n)


VEC_LAYOUT = (("g_mix", D_MODEL), ("g_ffn", D_MODEL), ("g_out_attn", ATTN_WIDTH), ("D_skip", SSM_WIDTH),
              ("b_glu", SSM_WIDTH), ("g_out_ssm", SSM_WIDTH), ("pool_scale", POOL_WIDTH),
              ("g_out_pool", POOL_WIDTH), ("g_q", HEAD_DIM), ("g_k", HEAD_DIM))
VEC_WIDTH = dict(VEC_LAYOUT)
VEC_OFFSET = {name: sum(w for _, w in VEC_LAYOUT[:i]) for i, (name, _) in enumerate(VEC_LAYOUT)}
assert all(VEC_OFFSET[name] % w == 0 for name, w in VEC_LAYOUT)


def _vrow(name, l):
    w = VEC_WIDTH[name]
    return pl.BlockSpec((None, 1, w), lambda *_: (l, 0, VEC_OFFSET[name] // w))


def _inproj_rows(x, g_ref, w_ref, gq_ref, gk_ref, rope_refs, out_refs):
    q_ref, k_ref, v_ref, u_ref, xp_ref = out_refs
    h = _rms(x, g_ref[...]).astype(BF16)
    proj = jnp.dot(h, w_ref[...], preferred_element_type=F32)
    rc, rs1, rs2 = (r[...] for r in rope_refs)

    def head(t, g):
        t = _rms(t, g)
        return t * rc + pltpu.roll(t, ROT_HALF, 1) * rs1 + pltpu.roll(t, LANES - ROT_HALF, 1) * rs2

    for hd in range(N_HEADS):
        sl = slice(hd * HEAD_DIM, (hd + 1) * HEAD_DIM)
        q_ref[:, sl] = head(proj[:, sl], gq_ref[...])
    for hd in range(N_KV_HEADS):
        sl = slice(hd * HEAD_DIM, (hd + 1) * HEAD_DIM)
        k_ref[:, sl] = head(proj[:, ATTN_WIDTH + hd * HEAD_DIM:ATTN_WIDTH + (hd + 1) * HEAD_DIM], gk_ref[...])
    o2 = ATTN_WIDTH + KV_WIDTH
    o3 = o2 + KV_WIDTH
    o4 = o3 + SSM_WIDTH
    v_ref[...] = proj[:, o2:o3]
    for j in range(SSM_LANE_BLOCKS):
        u_ref[j] = proj[:, o3 + j * LANES:o3 + (j + 1) * LANES]
    xp_ref[...] = proj[:, o4:]


def _inproj_kernel(xm_ref, xh_ref, g_ref, w_ref, gq_ref, gk_ref, rcm_ref, rs1m_ref, rs2m_ref,
                   rch_ref, rs1h_ref, rs2h_ref, *rest):
    cast = len(rest) == 12
    outs = rest[1:] if cast else rest
    main_outs, head_outs = outs[0:5], outs[5:10]
    i = pl.program_id(0)
    last = pl.num_programs(0) - 1

    @pl.when(i < last)
    def _():
        if cast:
            outs[10][...] = rest[0][...].astype(BF16)
        _inproj_rows(xm_ref[...], g_ref, w_ref, gq_ref, gk_ref, (rcm_ref, rs1m_ref, rs2m_ref), main_outs)

    @pl.when(i == last)
    def _():
        _inproj_rows(xh_ref[...], g_ref, w_ref, gq_ref, gk_ref, (rch_ref, rs1h_ref, rs2h_ref), head_outs)


def _inproj(xm, xh, vecs, w, l, rope_m, rope_h, tm, cast=None):
    R, RH = xm.shape[0], xh.shape[0]
    nm = R // tm
    tiles_per_rope = rope_m[0].shape[0] // tm
    tile = lambda i: jnp.minimum(i, nm - 1)
    row = lambda i: (tile(i), 0)
    rrow = lambda i: (tile(i) % tiles_per_rope, 0)
    once = pl.Buffered(1)
    full = lambda n, w_: pl.BlockSpec((n, w_), lambda i: (0, 0))

    def out_set(n, rows, im, uim):
        flat = lambda w_: (pl.BlockSpec((n, w_), im), jax.ShapeDtypeStruct((rows, w_), F32))
        u_out = (pl.BlockSpec((SSM_LANE_BLOCKS, n, LANES), uim),
                 jax.ShapeDtypeStruct((SSM_LANE_BLOCKS, rows, LANES), F32))
        return [flat(ATTN_WIDTH), flat(KV_WIDTH), flat(KV_WIDTH), u_out, flat(POOL_WIDTH)]

    outs = (out_set(tm, R, row, lambda i: (0, tile(i), 0))
            + out_set(RH, RH, lambda i: (0, 0), lambda i: (0, 0, 0)))
    in_specs = [pl.BlockSpec((tm, D_MODEL), row),
                pl.BlockSpec((RH, D_MODEL), lambda i: (0, 0), pipeline_mode=once),
                _vrow("g_mix", l),
                pl.BlockSpec((D_MODEL, IN_WIDTH), lambda i: (0, 0), pipeline_mode=once),
                _vrow("g_q", l), _vrow("g_k", l),
                pl.BlockSpec((tm, LANES), rrow), pl.BlockSpec((tm, LANES), rrow), pl.BlockSpec((tm, LANES), rrow),
                full(RH, LANES), full(RH, LANES), full(RH, LANES)]
    args = [xm, xh, vecs, w, vecs, vecs, *rope_m, *rope_h]
    if cast is not None:
        _, cr, cc = cast.shape
        crows = cr // nm
        in_specs.append(pl.BlockSpec((None, crows, cc), lambda i: (l, tile(i), 0)))
        args.append(cast)
        outs.append((pl.BlockSpec((crows, cc), row), jax.ShapeDtypeStruct((cr, cc), BF16)))
    res = pl.pallas_call(
        _inproj_kernel,
        grid=(nm + 1,),
        in_specs=in_specs,
        out_specs=[o[0] for o in outs],
        out_shape=[o[1] for o in outs],
        compiler_params=_params("arbitrary"),
        name="inproj",
    )(*args)
    return res[0:5], res[5:10], (res[10] if cast is not None else None)


def _attn_bias():
    rows = GQA_GROUP * BLOCK
    i, r, c = np.meshgrid(np.arange(3), np.arange(rows) % BLOCK, np.arange(2 * BLOCK), indexing="ij")
    diff = BLOCK + r - c
    krow = (i - 1) * BLOCK + c
    mask = (diff >= 0) & (diff <= WINDOW) & (krow >= SEQ0)
    return np.where(mask, 0.0, -np.inf).astype(np.float32)


def _attn_block(q_blk, kp_blk, kc_blk, vp_blk, vc_blk, bias, sink_ref, l, g):
    return _attn_blocks([(q_blk, kp_blk, kc_blk, vp_blk, vc_blk, bias)], sink_ref, l, g)[0]


def _attn_blocks(blocks, sink_ref, l, g):
    rows = GQA_GROUP * BLOCK
    rgrp = jax.lax.broadcasted_iota(jnp.int32, (rows, 1), 0) // BLOCK
    sks = []
    for kh in range(N_KV_HEADS):
        sk = jnp.zeros((rows, 1), F32)
        for h in range(GQA_GROUP):
            sk = jnp.where(rgrp == h, sink_ref[l * N_HEADS + kh * GQA_GROUP + h] * LOG2E, sk)
        sks.append(sk)
    scores, values = [], []
    for q_blk, kp_blk, kc_blk, vp_blk, vc_blk, bias in blocks:
        for kh in range(N_KV_HEADS):
            ksl = slice(kh * HEAD_DIM, (kh + 1) * HEAD_DIM)
            qh = jnp.concatenate(
                [q_blk[:, (kh * GQA_GROUP + h) * HEAD_DIM:(kh * GQA_GROUP + h + 1) * HEAD_DIM]
                 for h in range(GQA_GROUP)], axis=0).astype(BF16)
            kk = jnp.concatenate([kp_blk[:, ksl], kc_blk[:, ksl]], axis=0).astype(BF16)
            values.append(jnp.concatenate([vp_blk[:, ksl], vc_blk[:, ksl]], axis=0).astype(BF16))
            scores.append(jax.lax.dot_general(qh, kk, (((1,), (1,)), ((), ())), preferred_element_type=F32)
                          * (HEAD_DIM ** -0.5 * LOG2E) + bias)
    probs = []
    for i, s in enumerate(scores):
        sk = sks[i % N_KV_HEADS]
        m = jnp.maximum(jnp.max(s, axis=-1, keepdims=True), sk)
        p = jnp.exp2(s - m)
        probs.append((p.astype(BF16), jnp.sum(p, axis=-1, keepdims=True) + jnp.exp2(sk - m)))
    res = []
    for b in range(len(blocks)):
        outs = []
        for kh in range(N_KV_HEADS):
            p, denom = probs[b * N_KV_HEADS + kh]
            o = jnp.dot(p, values[b * N_KV_HEADS + kh], preferred_element_type=F32) / denom
            outs.extend(o[h * BLOCK:(h + 1) * BLOCK] for h in range(GQA_GROUP))
        res.append(_rms(jnp.concatenate(outs, axis=1), g))
    return res


def _ssm_params_kernel(ar_ref, ai_ref, ldt_ref, kk_ref, br_ref, bi_ref, tr_ref, ti_ref, bbr_ref, bbi_ref):
    ar, ai = ar_ref[...], ai_ref[...]
    dt = jnp.exp(ldt_ref[...])
    kk = kk_ref[...]
    mag = jnp.exp(dt * ar * kk)
    ang = dt * ai * kk
    tr = mag * jnp.cos(ang)
    ti = mag * jnp.sin(ang)
    tr_ref[...] = tr
    ti_ref[...] = ti
    abr, abi = tr[0:1], ti[0:1]
    den = ar * ar + ai * ai
    fr = ((abr - 1.0) * ar + abi * ai) / den
    fi = (abi * ar - (abr - 1.0) * ai) / den
    br, bi = br_ref[...], bi_ref[...]
    bbr_ref[...] = fr * br - fi * bi
    bbi_ref[...] = fr * bi + fi * br


def _ssm_params(A_re, A_im, log_dt, B_re, B_im, C_re, C_im):
    depth = A_re.shape[0]
    n = depth * SSM_GROUPS * SSM_STATE
    row = lambda t: t.astype(F32).reshape(1, n)
    ldt = jnp.broadcast_to(log_dt.astype(F32)[:, :, None], (depth, SSM_GROUPS, SSM_STATE)).reshape(1, n)
    kk = jnp.array(SSM_POW_ROWS + (0,) * (SUBLANES - len(SSM_POW_ROWS)), F32).reshape(SUBLANES, 1)
    chan_first = lambda t: t.astype(F32).reshape(n, SSM_GROUP_SIZE).T
    shapes = [(SUBLANES, n), (SUBLANES, n), (SSM_GROUP_SIZE, n), (SSM_GROUP_SIZE, n)]
    tr, ti, bbr, bbi = pl.pallas_call(
        _ssm_params_kernel,
        out_shape=[jax.ShapeDtypeStruct(s, F32) for s in shapes],
        name="ssm_params",
    )(row(A_re), row(A_im), ldt, kk, chan_first(B_re), chan_first(B_im))
    J, G8 = SSM_LANE_BLOCKS, LANES // SSM_GROUP_SIZE

    def lanes(t):
        return t.reshape(SUBLANES, depth, J, SSM_BLOCK_STATES).transpose(1, 0, 2, 3)

    pw = jnp.concatenate([lanes(tr), lanes(ti)], axis=-1).reshape(depth, SUBLANES, SSM_STATE_LANES)
    eye = jnp.eye(G8, dtype=F32)

    def bdiag(t):
        t = t.reshape(SSM_GROUP_SIZE, depth, J, G8, SSM_STATE).transpose(1, 2, 3, 0, 4)
        t = t[:, :, :, :, None, :] * eye[None, None, :, None, :, None]
        return t.reshape(depth, J, LANES, SSM_BLOCK_STATES)

    bblk = jnp.concatenate([bdiag(bbr), bdiag(bbi)], axis=-1).astype(BF16)

    def cdiag(t):
        t = t.astype(F32).reshape(depth, J, G8, SSM_GROUP_SIZE, SSM_STATE).transpose(0, 1, 2, 4, 3)
        t = t[:, :, :, :, None, :] * eye[None, None, :, None, :, None]
        return t.reshape(depth, J, SSM_BLOCK_STATES, LANES)

    cblk = jnp.concatenate([cdiag(C_re), -cdiag(C_im)], axis=2).astype(BF16)
    return pw, bblk, cblk


def _ssm_tail(y, u, d_ref, wg_ref, bg_ref, g_ref):
    y = y + d_ref[...] * u
    z = jax.nn.gelu(y)
    gate = jax.nn.sigmoid(jnp.dot(z.astype(BF16), wg_ref[...], preferred_element_type=F32) + bg_ref[...])
    return _rms(z * gate, g_ref[...])


def _ssm_sweep(x_scr, n, a_tabs, init, store):
    S = SSM_BLOCK_STATES
    fins = []
    for j0 in range(0, SSM_LANE_BLOCKS, 2):
        js = (j0, j0 + 1)

        def body(k, carry, js=js):
            r0 = pl.multiple_of(k * SUBLANES, SUBLANES)
            out = []
            for idx, j in enumerate(js):
                hr, hi = carry[2 * idx], carry[2 * idx + 1]
                base = j * 2 * S
                ar, ai = a_tabs[j]
                nhr = ar * hr - ai * hi + x_scr[pl.ds(r0, SUBLANES), base:base + S]
                nhi = ar * hi + ai * hr + x_scr[pl.ds(r0, SUBLANES), base + S:base + 2 * S]
                if store:
                    x_scr[pl.ds(r0, SUBLANES), base:base + S] = nhr
                    x_scr[pl.ds(r0, SUBLANES), base + S:base + 2 * S] = nhi
                out += [nhr, nhi]
            return tuple(out)

        c0 = tuple(t for j in js for t in init[j])
        res = jax.lax.fori_loop(0, n // SUBLANES, body, c0, unroll=True)
        fins += [(res[0], res[1]), (res[2], res[3])]
    return fins


def _ssm_rows(u, pow_row, pw_ref, bblk_ref, cblk_ref, x_scr, s_scr, carry_scr):
    n = u.shape[0]
    S = SSM_BLOCK_STATES
    ub = u.astype(BF16)
    for j in range(SSM_LANE_BLOCKS):
        x_scr[0:n, j * 2 * S:(j + 1) * 2 * S] = jnp.dot(ub[:, j * LANES:(j + 1) * LANES], bblk_ref[j],
                                                       preferred_element_type=F32)
    bc = lambda t: jnp.broadcast_to(t, (SUBLANES, S))
    re = lambda ref, r0, r1, j: ref[r0:r1, j * 2 * S:j * 2 * S + S]
    im = lambda ref, r0, r1, j: ref[r0:r1, j * 2 * S + S:(j + 1) * 2 * S]
    a_tabs = [(bc(re(pw_ref, 0, 1, j)), bc(im(pw_ref, 0, 1, j))) for j in range(SSM_LANE_BLOCKS)]
    zero = jnp.zeros((SUBLANES, S), F32)
    fins = _ssm_sweep(x_scr, n, a_tabs, [(zero, zero)] * SSM_LANE_BLOCKS, store=False)
    for j in range(SSM_LANE_BLOCKS):
        base = j * 2 * S
        cr, ci = re(pw_ref, pow_row, pow_row + 1, j), im(pw_ref, pow_row, pow_row + 1, j)
        sr, si = re(carry_scr, 0, 1, j), im(carry_scr, 0, 1, j)
        fr, fi = fins[j]
        for c in range(SUBLANES):
            s_scr[c:c + 1, base:base + S] = sr
            s_scr[c:c + 1, base + S:base + 2 * S] = si
            sr, si = cr * sr - ci * si + fr[c:c + 1], cr * si + ci * sr + fi[c:c + 1]
        carry_scr[:, base:base + S] = bc(sr)
        carry_scr[:, base + S:base + 2 * S] = bc(si)
    init = [(re(s_scr, 0, SUBLANES, j), im(s_scr, 0, SUBLANES, j)) for j in range(SSM_LANE_BLOCKS)]
    _ssm_sweep(x_scr, n, a_tabs, init, store=True)
    ys = [jnp.dot(x_scr[0:n, j * 2 * S:(j + 1) * 2 * S].astype(BF16), cblk_ref[j], preferred_element_type=F32)
          for j in range(SSM_LANE_BLOCKS)]
    return jnp.concatenate(ys, axis=1)


def _ssm_tile(u_ref, n, pow_row, refs, o_ref, scr):
    pw_ref, bblk_ref, cblk_ref, d_ref, wg_ref, bg_ref, g_ref = refs
    up_scr, x_scr, s_scr, carry_scr, o_scr = scr
    q = n // SUBLANES
    for j in range(SSM_LANE_BLOCKS):
        for k in range(q):
            up_scr[k * SUBLANES:(k + 1) * SUBLANES, j * LANES:(j + 1) * LANES] = \
                u_ref[j, pl.ds(k, SUBLANES, stride=q), :]
    u = up_scr[0:n, :]
    y = _ssm_rows(u, pow_row, pw_ref, bblk_ref, cblk_ref, x_scr, s_scr, carry_scr)
    out = _ssm_tail(y, u, d_ref, wg_ref, bg_ref, g_ref)
    for j in range(SSM_LANE_BLOCKS):
        for k in range(q):
            o_scr[j, pl.ds(k, SUBLANES, stride=q), :] = out[k * SUBLANES:(k + 1) * SUBLANES,
                                                            j * LANES:(j + 1) * LANES]
    o_ref[...] = jnp.concatenate([o_scr[j, 0:n, :] for j in range(SSM_LANE_BLOCKS)], axis=1).astype(o_ref.dtype)


def _mixers_kernel(sink_ref,
                   qm_ref, qh_ref, kpm_ref, km_ref, kh_ref, vpm_ref, vm_ref, vh_ref, ga_ref, bias_ref,
                   um_ref, uh_ref, pw_ref, bblk_ref, cblk_ref, d_ref, wg_ref, bg_ref, gs_ref,
                   xm_ref, halo_ref, xh_ref, wp_ref, sc_ref, gp_ref, *rest, l, n_cast, n_dec):
    cast_in = rest[0:n_cast]
    am_ref, ah_ref, som_ref, soh_ref, st_ref, pom_ref, poh_ref = rest[n_cast:n_cast + 7]
    cast_out = rest[n_cast + 7:2 * n_cast + 7]
    scr = rest[2 * n_cast + 7:]
    carry_scr = scr[3]
    b = pl.program_id(0)
    t = pl.program_id(1)
    tm = xm_ref.shape[0]
    ssm_refs = (pw_ref, bblk_ref, cblk_ref, d_ref, wg_ref, bg_ref, gs_ref)
    blk = functools.partial(_attn_block, sink_ref=sink_ref, l=l, g=ga_ref[...])
    meta_rows = pl.ds(pl.multiple_of(n_dec + b * N_META, N_META), N_META)

    @pl.when((b == 0) & (t == 0))
    def _():
        for ref in (ah_ref, soh_ref, poh_ref):
            ref[0:n_dec, :] = jnp.zeros((n_dec, ref.shape[1]), ref.dtype)

    block = lambda ref: jnp.concatenate([jnp.zeros((SEQ0, ref.shape[1]), F32), ref[...]], axis=0)

    @pl.when(t == 0)
    def _():
        a = blk(block(qh_ref), kpm_ref[...], block(kh_ref), vpm_ref[...], block(vh_ref), bias_ref[0])
        ah_ref[meta_rows, :] = a[SEQ0:].astype(ah_ref.dtype)
        carry_scr[...] = jnp.zeros_like(carry_scr)
        _ssm_tile(uh_ref, N_META, POW_ROW_HEAD, ssm_refs, soh_ref.at[meta_rows, :], scr)
        prev = jnp.zeros((POOL_HALO, POOL_WIDTH), F32)
        poh_ref[meta_rows, :] = _pool_rows(xh_ref[...], prev, 0, wp_ref, sc_ref, gp_ref).astype(poh_ref.dtype)

    @pl.when(t > 0)
    def _():
        for src, dst in zip(cast_in, cast_out):
            dst[...] = src[...].astype(BF16)
        kp = jnp.where(t == 1, block(kh_ref), kpm_ref[...])
        vp = jnp.where(t == 1, block(vh_ref), vpm_ref[...])
        blocks = []
        for n in range(tm // BLOCK):
            rows = slice(n * BLOCK, (n + 1) * BLOCK)
            kc, vc = km_ref[rows, :], vm_ref[rows, :]
            bias = bias_ref[jnp.minimum(t, 2)] if n == 0 else bias_ref[2]
            blocks.append((qm_ref[rows, :], kp, kc, vp, vc, bias))
            kp, vp = kc, vc
        for n, a in enumerate(_attn_blocks(blocks, sink_ref, l, ga_ref[...])):
            am_ref[n * BLOCK:(n + 1) * BLOCK, :] = a.astype(am_ref.dtype)
        _ssm_tile(um_ref, um_ref.shape[1], POW_ROW_MAIN, ssm_refs, som_ref, scr)
        st_ref[...] = carry_scr[...]
        prev = jnp.where(t == 1, xh_ref[...], halo_ref[...])
        pom_ref[...] = _pool_rows(xm_ref[...], prev, N_META + (t - 1) * tm, wp_ref, sc_ref, gp_ref).astype(pom_ref.dtype)


def _mixers(sinks, qkv_m, qkv_h, um, uh, xm, xh, vecs, bias, pw, bblk, cblk, wg, wp, l, B, tm, casts):
    assert (1, N_META // SUBLANES, tm // SUBLANES) == SSM_POW_ROWS and N_META == POOL_HALO
    J = SSM_LANE_BLOCKS
    rm, rh = um.shape[1], uh.shape[1]
    nt = rm // (B * tm)
    r = tm // POOL_HALO
    bpt = tm // BLOCK
    tile = lambda b, t: b * nt + jnp.maximum(t - 1, 0)
    main = lambda b, t, _: (tile(b, t), 0)
    n_dec = uh.shape[1] - B * N_META
    head = lambda b, t, _: (n_dec // N_META + b, 0)
    halo = lambda b, t, _: (jnp.maximum((b * nt + t - 1) * r - 1, 0), 0)
    prevb = lambda b, t, _: (jnp.maximum(tile(b, t) * bpt - 1, 0), 0)
    cast_specs = []
    for w, wl in casts:
        _, cr, cc = w.shape
        rows = cr // (B * nt)
        cast_specs.append((pl.BlockSpec((None, rows, cc), lambda b, t, _, wl=wl: (wl, tile(b, t), 0)),
                           pl.BlockSpec((rows, cc), main), jax.ShapeDtypeStruct((cr, cc), BF16)))
    bf = lambda n, w: jax.ShapeDtypeStruct((n, w), BF16)
    tile_spec = lambda w: pl.BlockSpec((tm, w), main)
    head_spec = lambda w: pl.BlockSpec((N_META, w), head)
    in_specs = [tile_spec(ATTN_WIDTH), head_spec(ATTN_WIDTH),
                pl.BlockSpec((BLOCK, KV_WIDTH), prevb), tile_spec(KV_WIDTH), head_spec(KV_WIDTH),
                pl.BlockSpec((BLOCK, KV_WIDTH), prevb), tile_spec(KV_WIDTH), head_spec(KV_WIDTH),
                _vrow("g_out_attn", l), _full(bias.shape),
                pl.BlockSpec((J, tm, LANES), lambda b, t, _: (0, tile(b, t), 0)),
                pl.BlockSpec((J, N_META, LANES), lambda b, t, _: (0, n_dec // N_META + b, 0)),
                _layer(pw.shape[1:], l), _layer(bblk.shape[1:], l), _layer(cblk.shape[1:], l),
                _vrow("D_skip", l), _layer((SSM_WIDTH, SSM_WIDTH), l), _vrow("b_glu", l), _vrow("g_out_ssm", l),
                tile_spec(POOL_WIDTH), pl.BlockSpec((POOL_HALO, POOL_WIDTH), halo), head_spec(POOL_WIDTH),
                _layer(wp.shape[1:], l), _vrow("pool_scale", l), _vrow("g_out_pool", l)]
    in_specs += [c[0] for c in cast_specs]
    head_out = lambda w: pl.BlockSpec((rh, w), lambda b, t, _: (0, 0))
    out_specs = [tile_spec(ATTN_WIDTH), head_out(ATTN_WIDTH), tile_spec(SSM_WIDTH), head_out(SSM_WIDTH),
                 pl.BlockSpec((None, SUBLANES, SSM_STATE_LANES), lambda b, t, _: (b, 0, 0)),
                 tile_spec(POOL_WIDTH), head_out(POOL_WIDTH)] + [c[1] for c in cast_specs]
    out_shape = [bf(rm, ATTN_WIDTH), bf(rh, ATTN_WIDTH), bf(rm, SSM_WIDTH), bf(rh, SSM_WIDTH),
                 jax.ShapeDtypeStruct((B, SUBLANES, SSM_STATE_LANES), F32),
                 bf(rm, POOL_WIDTH), bf(rh, POOL_WIDTH)] + [c[2] for c in cast_specs]
    (qm, km, vm), (qh, kh, vh) = qkv_m, qkv_h
    res = pl.pallas_call(
        functools.partial(_mixers_kernel, l=l, n_cast=len(casts), n_dec=n_dec),
        grid_spec=pltpu.PrefetchScalarGridSpec(
            num_scalar_prefetch=1, grid=(B, nt + 1), in_specs=in_specs, out_specs=out_specs,
            scratch_shapes=[pltpu.VMEM((tm, SSM_WIDTH), F32),
                            pltpu.VMEM((tm, SSM_STATE_LANES), F32),
                            pltpu.VMEM((SUBLANES, SSM_STATE_LANES), F32),
                            pltpu.VMEM((SUBLANES, SSM_STATE_LANES), F32),
                            pltpu.VMEM((J, tm, LANES), F32)]),
        out_shape=out_shape,
        compiler_params=_params("arbitrary", "arbitrary"),
        name="mixers",
    )(sinks, qm, qh, km, km, kh, vm, vm, vh, vecs, bias, um, uh, pw, bblk, cblk, vecs, wg, vecs, vecs,
      xm, xm, xh, wp, vecs, vecs, *[w for w, _ in casts])
    return res[0:7], res[7:]


def _state_from_lanes(s):
    s = s.reshape(s.shape[0], SSM_LANE_BLOCKS, 2, SSM_BLOCK_STATES)
    return (s[:, :, 0].reshape(-1, SSM_GROUPS, SSM_STATE), s[:, :, 1].reshape(-1, SSM_GROUPS, SSM_STATE))


def _pool_tail(d_groups, w_ref, sc_ref, g_ref):
    y = jnp.concatenate(
        [jnp.dot(d.astype(BF16), w_ref[gi], preferred_element_type=F32) for gi, d in enumerate(d_groups)], axis=1)
    return _rms(y * sc_ref[...], g_ref[...])


def _pool_rows(x, prev, pos0, w_ref, sc_ref, g_ref):
    n = x.shape[0]
    xe = jnp.concatenate([prev, x], axis=0)
    pos = pos0 + jax.lax.broadcasted_iota(jnp.int32, (n, 1), 0)
    ds = []
    for gi, w in enumerate(POOL_WINDOWS):
        gsl = slice(gi * POOL_GROUP, (gi + 1) * POOL_GROUP)
        s = xe[:, gsl]
        k = 1
        while k < w:
            s = s + pltpu.roll(s, k, 0)
            k *= 2
        cnt = jnp.clip(pos + 1, 1, w).astype(F32)
        ds.append(s[POOL_HALO:] / cnt - x[:, gsl])
    return _pool_tail(ds, w_ref, sc_ref, g_ref)


def _attn_sample_kernel(q_ref, kn_ref, vn_ref, kc_ref, vc_ref, kd_ref, vd_ref, sink_ref, g_ref, *rest, first):
    a_ref, nk_ref, nv_ref, acc_scr = rest[-4:]
    step = pl.program_id(1 if first else 0)
    nsteps = pl.num_programs(1 if first else 0)
    slot = WINDOW - 1 if first else 0
    scale = HEAD_DIM ** -0.5
    if first:
        for bb in range(DEC_STEP):
            nk_ref[bb, 0:WINDOW - 1] = kc_ref[bb, 1:WINDOW]
            nv_ref[bb, 0:WINDOW - 1] = vc_ref[bb, 1:WINDOW]
            nk_ref[bb, WINDOW - 1] = jnp.zeros((N_KV_HEADS, HEAD_DIM), F32)
            nv_ref[bb, WINDOW - 1] = jnp.zeros((N_KV_HEADS, HEAD_DIM), F32)

    def attend():
        _attn_sample_rows(q_ref, kn_ref, vn_ref, kd_ref, vd_ref, sink_ref, g_ref, nk_ref, nv_ref, acc_scr,
                          step, slot, scale)

        @pl.when(step == nsteps - 1)
        def _():
            a_ref[...] = acc_scr[...].astype(a_ref.dtype)

    if first:
        pl.when(pl.program_id(0) == 0)(attend)
    else:
        attend()


def _attn_sample_rows(q_ref, kn_ref, vn_ref, kc_ref, vc_ref, sink_ref, g_ref, nk_ref, nv_ref, acc_scr,
                      step, slot, scale):
    ncols = WINDOW * N_KV_HEADS
    nrows = DEC_STEP * N_HEADS
    head = jax.lax.broadcasted_iota(jnp.int32, (nrows, ncols), 0) % N_HEADS
    col = jax.lax.broadcasted_iota(jnp.int32, (nrows, ncols), 1)
    own = (col % N_KV_HEADS) == (head // GQA_GROUP)
    sk = jnp.concatenate([sink_ref[kh] for kh in range(N_KV_HEADS)] * DEC_STEP, axis=0)
    per_head = lambda rows: [jnp.broadcast_to(r, (GQA_GROUP, HEAD_DIM)) for r in rows]
    qs, knh, vnh = [], [], []
    for bb in range(DEC_STEP):
        qs.append(jnp.concatenate(
            [q_ref[bb:bb + 1, h * HEAD_DIM:(h + 1) * HEAD_DIM] for h in range(N_HEADS)], axis=0))
        kn = [kn_ref[bb:bb + 1, kh * HEAD_DIM:(kh + 1) * HEAD_DIM] for kh in range(N_KV_HEADS)]
        vn = [vn_ref[bb:bb + 1, kh * HEAD_DIM:(kh + 1) * HEAD_DIM] for kh in range(N_KV_HEADS)]
        for kh in range(N_KV_HEADS):
            nk_ref[bb, slot, kh:kh + 1, :] = kn[kh]
            nv_ref[bb, slot, kh:kh + 1, :] = vn[kh]
        knh += per_head(kn)
        vnh += per_head(vn)
    sc = jnp.concatenate(
        [jax.lax.dot_general(qs[bb].astype(BF16), kc_ref[bb].astype(BF16), (((1,), (1,)), ((), ())),
                             preferred_element_type=F32) for bb in range(DEC_STEP)], axis=0) * scale
    sc = jnp.where(own, sc, -jnp.inf)
    q = jnp.concatenate(qs, axis=0)
    sn = jnp.sum(q * jnp.concatenate(knh, axis=0), axis=-1, keepdims=True) * scale
    m = jnp.maximum(jnp.maximum(jnp.max(sc, axis=-1, keepdims=True), sn), sk)
    pc = jnp.exp(sc - m)
    pn = jnp.exp(sn - m)
    denom = jnp.sum(pc, axis=-1, keepdims=True) + pn + jnp.exp(sk - m)
    o = jnp.concatenate(
        [jnp.dot(pc[bb * N_HEADS:(bb + 1) * N_HEADS].astype(BF16), vc_ref[bb].astype(BF16),
                 preferred_element_type=F32) for bb in range(DEC_STEP)], axis=0)
    o = (o + pn * jnp.concatenate(vnh, axis=0)) / denom
    a = jnp.concatenate(
        [jnp.concatenate([o[bb * N_HEADS + h:bb * N_HEADS + h + 1] for h in range(N_HEADS)], axis=1)
         for bb in range(DEC_STEP)], axis=0)
    acc_scr[pl.ds(pl.multiple_of(step * DEC_STEP, DEC_STEP), DEC_STEP), :] = _rms(a, g_ref[...])


def _attn_sample(qh, kh, vh, cache_k, cache_v, l, sinks, g, ah, nk, nv, N):
    depth = cache_k.shape[0]
    first = nk is None
    anyspec = pl.BlockSpec(memory_space=pl.ANY)
    cshape = jax.ShapeDtypeStruct(cache_k.shape, F32)
    slab = (None, DEC_STEP, WINDOW, N_KV_HEADS, HEAD_DIM)
    dense_slab = (None, DEC_STEP, WINDOW * N_KV_HEADS, HEAD_DIM)
    dense_view = lambda c: c.reshape(depth, N, WINDOW * N_KV_HEADS, HEAD_DIM)
    nsteps = N // DEC_STEP
    if first:
        assert l == 0
        grid = (depth, nsteps)
        rows = lambda w: pl.BlockSpec((DEC_STEP, w), lambda ll, s: (s, 0))
        cache = pl.BlockSpec(slab, lambda ll, s: (ll, s, 0, 0, 0))
        dense = pl.BlockSpec(dense_slab, lambda ll, s: (0, jnp.where(ll == 0, s, nsteps - 1), 0, 0))
        new = cache
        a_spec = pl.BlockSpec((N, ATTN_WIDTH), lambda ll, s: (0, 0))
        carried = []
    else:
        grid = (nsteps,)
        rows = lambda w: pl.BlockSpec((DEC_STEP, w), lambda s: (s, 0))
        cache = anyspec
        dense = pl.BlockSpec(dense_slab, lambda s: (l, s, 0, 0))
        new = pl.BlockSpec((None, DEC_STEP, 1, N_KV_HEADS, HEAD_DIM), lambda s: (l, s, WINDOW - 1, 0, 0))
        a_spec = pl.BlockSpec((N, ATTN_WIDTH), lambda s: (0, 0))
        carried = [nk, nv]
    n_in = 9
    return pl.pallas_call(
        functools.partial(_attn_sample_kernel, first=first),
        grid=grid,
        in_specs=[rows(ATTN_WIDTH), rows(KV_WIDTH), rows(KV_WIDTH), cache, cache, dense, dense,
                  _layer((N_KV_HEADS, GQA_GROUP, 1), l), _vrow("g_out_attn", l)] + [anyspec] * (1 + len(carried)),
        out_specs=[a_spec, new, new],
        out_shape=[jax.ShapeDtypeStruct(ah.shape, ah.dtype), cshape, cshape],
        scratch_shapes=[pltpu.VMEM((N, ATTN_WIDTH), F32)],
        input_output_aliases={n_in + i: i for i in range(1 + len(carried))},
        compiler_params=_params(*(["arbitrary"] * len(grid))),
        name="attn_sample",
    )(qh, kh, vh, cache_k, cache_v, dense_view(cache_k), dense_view(cache_v), sinks, g, ah, *carried)


def _mix_sample_kernel(u_ref, h0_ref, pw_ref, bblk_ref, cblk_ref, d_ref, wg_ref, bg_ref, gs_ref,
                       xp_ref, pb_ref, wp_ref, sc_ref, gp_ref, s_in_ref, p_in_ref, s_ref, st_ref, p_ref):
    del s_in_ref, p_in_ref
    S = SSM_BLOCK_STATES
    u = jnp.concatenate([u_ref[j] for j in range(SSM_LANE_BLOCKS)], axis=1)
    ub = u.astype(BF16)
    ys = []
    for j in range(SSM_LANE_BLOCKS):
        x = jnp.dot(ub[:, j * LANES:(j + 1) * LANES], bblk_ref[j], preferred_element_type=F32)
        base = j * 2 * S
        ar = pw_ref[0:1, base:base + S]
        ai = pw_ref[0:1, base + S:base + 2 * S]
        h0r = h0_ref[:, base:base + S]
        h0i = h0_ref[:, base + S:base + 2 * S]
        hr = x[:, 0:S] + ar * h0r - ai * h0i
        hi = x[:, S:] + ar * h0i + ai * h0r
        st_ref[:, base:base + S] = hr
        st_ref[:, base + S:base + 2 * S] = hi
        h = jnp.concatenate([hr, hi], axis=1).astype(BF16)
        ys.append(jnp.dot(h, cblk_ref[j], preferred_element_type=F32))
    s_ref[...] = _ssm_tail(jnp.concatenate(ys, axis=1), u, d_ref, wg_ref, bg_ref, gs_ref).astype(s_ref.dtype)

    xp = xp_ref[...]
    ds = []
    for gi, w in enumerate(POOL_WINDOWS):
        gsl = slice(gi * POOL_GROUP, (gi + 1) * POOL_GROUP)
        s = xp[:, gsl]
        for back in range(1, w):
            s = s + pb_ref[POOL_BUF - back][:, gsl]
        ds.append(s / float(w) - xp[:, gsl])
    p_ref[...] = _pool_tail(ds, wp_ref, sc_ref, gp_ref).astype(p_ref.dtype)


def _mix_sample(uh, xph, h0, pbuf, vecs, pw, bblk, cblk, wg, wp, l, sh, ph):
    N = h0.shape[1]
    rows = lambda w: pl.BlockSpec((N, w), lambda i: (0, 0))
    anyspec = pl.BlockSpec(memory_space=pl.ANY)
    return pl.pallas_call(
        _mix_sample_kernel,
        grid=(1,),
        in_specs=[pl.BlockSpec((SSM_LANE_BLOCKS, N, LANES), lambda i: (0, 0, 0)),
                  _layer(h0.shape[1:], l), _layer(pw.shape[1:], l), _layer(bblk.shape[1:], l),
                  _layer(cblk.shape[1:], l),
                  _vrow("D_skip", l), _layer((SSM_WIDTH, SSM_WIDTH), l), _vrow("b_glu", l),
                  _vrow("g_out_ssm", l), rows(POOL_WIDTH), _layer(pbuf.shape[1:], l),
                  _layer(wp.shape[1:], l),
                  _vrow("pool_scale", l), _vrow("g_out_pool", l), anyspec, anyspec],
        out_specs=[rows(SSM_WIDTH), _full((N, SSM_STATE_LANES)), rows(POOL_WIDTH)],
        out_shape=[jax.ShapeDtypeStruct(sh.shape, sh.dtype), jax.ShapeDtypeStruct((N, SSM_STATE_LANES), F32),
                   jax.ShapeDtypeStruct(ph.shape, ph.dtype)],
        input_output_aliases={14: 0, 15: 2},
        compiler_params=_params("arbitrary"),
        name="mix_sample",
    )(uh, h0, pw, bblk, cblk, vecs, wg, vecs, vecs, xph, pbuf, wp, vecs, vecs, sh, ph)


def _outproj_rows(x_ref, a_ref, s_ref, p_ref, w_ref, o_ref):
    o1 = ATTN_WIDTH
    o2 = o1 + SSM_WIDTH
    acc = x_ref[...]
    acc = acc + jnp.dot(a_ref[...], w_ref[0:o1, :], preferred_element_type=F32)
    acc = acc + jnp.dot(s_ref[...], w_ref[o1:o2, :], preferred_element_type=F32)
    acc = acc + jnp.dot(p_ref[...], w_ref[o2:, :], preferred_element_type=F32)
    o_ref[...] = acc


def _outproj_kernel(xm_ref, am_ref, sm_ref, pm_ref, xh_ref, ah_ref, sh_ref, ph_ref, w_ref, om_ref, oh_ref):
    i = pl.program_id(0)
    last = pl.num_programs(0) - 1

    @pl.when(i < last)
    def _():
        _outproj_rows(xm_ref, am_ref, sm_ref, pm_ref, w_ref, om_ref)

    @pl.when(i == last)
    def _():
        _outproj_rows(xh_ref, ah_ref, sh_ref, ph_ref, w_ref, oh_ref)


def _outproj(xm, am, sm, pm, xh, ah, sh, ph, w, tm):
    R, RH = xm.shape[0], xh.shape[0]
    nm = R // tm
    row = lambda i: (jnp.minimum(i, nm - 1), 0)
    once = pl.Buffered(1)
    head = lambda w_: pl.BlockSpec((RH, w_), lambda i: (0, 0), pipeline_mode=once)
    return pl.pallas_call(
        _outproj_kernel,
        grid=(nm + 1,),
        in_specs=[pl.BlockSpec((tm, D_MODEL), row), pl.BlockSpec((tm, ATTN_WIDTH), row),
                  pl.BlockSpec((tm, SSM_WIDTH), row), pl.BlockSpec((tm, POOL_WIDTH), row),
                  head(D_MODEL), head(ATTN_WIDTH), head(SSM_WIDTH), head(POOL_WIDTH),
                  pl.BlockSpec((D_MODEL, D_MODEL), lambda i: (0, 0), pipeline_mode=once)],
        out_specs=[pl.BlockSpec((tm, D_MODEL), row), pl.BlockSpec((RH, D_MODEL), lambda i: (0, 0))],
        out_shape=[jax.ShapeDtypeStruct((R, D_MODEL), F32), jax.ShapeDtypeStruct((RH, D_MODEL), F32)],
        compiler_params=_params("arbitrary"),
        name="outproj",
    )(xm, am, sm, pm, xh, ah, sh, ph, w)


def _ffn_kernel(xm_ref, xh_ref, g_ref, w1_ref, w2_ref, om_ref, oh_ref, hm_scr, hh_scr):
    i = pl.program_id(0)
    f = pl.program_id(1)

    def mlp(h):
        h1 = jnp.dot(h, w1_ref[...], preferred_element_type=F32)
        return jnp.dot(jnp.square(jnp.maximum(h1, 0.0)).astype(BF16), w2_ref[...], preferred_element_type=F32)

    def first(x_ref, h_scr, o_ref):
        x = x_ref[...]
        h = _rms(x, g_ref[...]).astype(BF16)
        h_scr[...] = h
        o_ref[...] = x + mlp(h)

    @pl.when(f == 0)
    def _():
        first(xm_ref, hm_scr, om_ref)

    @pl.when(f > 0)
    def _():
        om_ref[...] += mlp(hm_scr[...])

    @pl.when((i == 0) & (f == 0))
    def _():
        first(xh_ref, hh_scr, oh_ref)

    @pl.when((i == 0) & (f > 0))
    def _():
        oh_ref[...] += mlp(hh_scr[...])


def _ffn(xm, xh, g, l, w1, w2, tm, tf):
    R, RH = xm.shape[0], xh.shape[0]
    return pl.pallas_call(
        _ffn_kernel,
        grid=(R // tm, D_FF // tf),
        in_specs=[pl.BlockSpec((tm, D_MODEL), lambda i, f: (i, 0)),
                  pl.BlockSpec((RH, D_MODEL), lambda i, f: (0, 0), pipeline_mode=pl.Buffered(1)),
                  _vrow("g_ffn", l),
                  pl.BlockSpec((D_MODEL, tf), lambda i, f: (0, f)),
                  pl.BlockSpec((tf, D_MODEL), lambda i, f: (f, 0))],
        out_specs=[pl.BlockSpec((tm, D_MODEL), lambda i, f: (i, 0)),
                   pl.BlockSpec((RH, D_MODEL), lambda i, f: (0, 0))],
        out_shape=[jax.ShapeDtypeStruct((R, D_MODEL), F32), jax.ShapeDtypeStruct((RH, D_MODEL), F32)],
        scratch_shapes=[pltpu.VMEM((tm, D_MODEL), BF16), pltpu.VMEM((RH, D_MODEL), BF16)],
        compiler_params=_params("arbitrary", "arbitrary"),
        name="ffn",
    )(xm, xh, g, w1, w2)


def _rope_tables(pos):
    half = ROT_HALF
    inv = ROPE_THETA ** (-np.arange(0, ROT_DIM, 2, dtype=np.float64) / ROT_DIM)
    ang = np.asarray(pos, np.float64)[:, None] * inv
    cos, sin = np.cos(ang), np.sin(ang)
    n = ang.shape[0]
    z = np.zeros((n, HEAD_DIM - ROT_DIM))
    zh = np.zeros((n, half))
    rc = np.concatenate([cos, cos, z + 1.0], axis=1)
    rs1 = np.concatenate([zh, sin, z], axis=1)
    rs2 = np.concatenate([-sin, zh, z], axis=1)
    return tuple(t.astype(np.float32) for t in (rc, rs1, rs2))


def kernel(x_prompt, x_sample, cache_k, cache_v, state_ssm_re, state_ssm_im, state_pool, meta_tokens, g_mix, w_in, g_q, g_k, sinks, A_re, A_im, log_dt, B_re, B_im, C_re, C_im, D_skip, w_glu, b_glu, w_pool, pool_scale, g_out_attn, g_out_ssm, g_out_pool, w_out, g_ffn, w_ff1, w_ff2):
    B, T, _ = x_prompt.shape
    N = x_sample.shape[0]
    depth = w_in.shape[0]
    assert N % N_META == 0 and N % DEC_STEP == 0 and T % TM_FFN == 0
    xh = jnp.concatenate([x_sample.reshape(N, D_MODEL)] + [meta_tokens.astype(F32)] * B, axis=0)
    xm = x_prompt.reshape(B * T, D_MODEL)

    rope_m = _rope_tables(N_META + np.arange(T))
    rope_h = _rope_tables(np.concatenate([np.full(N, PAST_LEN)] + [np.arange(N_META)] * B))

    wi = w_in[0].astype(BF16)
    wg_all, wp_all = w_glu.astype(BF16), w_pool.astype(BF16)
    ck = cache_k.astype(F32)
    cv = cache_v.astype(F32)

    named = dict(g_mix=g_mix, g_ffn=g_ffn, g_out_attn=g_out_attn, D_skip=D_skip, b_glu=b_glu,
                 g_out_ssm=g_out_ssm, pool_scale=pool_scale, g_out_pool=g_out_pool, g_q=g_q, g_k=g_k)
    vecs = jnp.concatenate([named[name].astype(F32) for name, _ in VEC_LAYOUT], axis=1)[:, None, :]
    pw, bblk, cblk = _ssm_params(A_re, A_im, log_dt, B_re, B_im, C_re, C_im)
    sinks_flat = sinks.astype(F32).reshape(depth * N_HEADS)
    sinks_col = sinks.astype(F32).reshape(depth, N_KV_HEADS, GQA_GROUP, 1)
    bias = _attn_bias()
    h0 = jnp.concatenate([state_ssm_re.astype(F32).reshape(depth, N, SSM_LANE_BLOCKS, SSM_BLOCK_STATES),
                          state_ssm_im.astype(F32).reshape(depth, N, SSM_LANE_BLOCKS, SSM_BLOCK_STATES)],
                         axis=-1).reshape(depth, N, SSM_STATE_LANES)
    pbuf = state_pool.astype(F32).transpose(0, 2, 1, 3)

    nk = nv = None
    ks, vs, pls, sts, st_ss, phs = ([] for _ in range(6))
    for l in range(depth):
        (qm, km, vm, um, pm), (qh, kh, vh, uh, ph), w1 = _inproj(xm, xh, vecs, wi, l, rope_m, rope_h, TM_PROJ, w_ff1)

        (am, ah, sm, sh, st, plm, plh), wcast = _mixers(
            sinks_flat, (qm, km, vm), (qh, kh, vh), um, uh, pm, ph, vecs, bias, pw, bblk, cblk, wg_all, wp_all,
            l, B, TM_SEQ, [(w_ff2, l), (w_out, l)] + ([(w_in, l + 1)] if l + 1 < depth else []))
        w2, wo = wcast[0], wcast[1]

        ah, nk, nv = _attn_sample(qh, kh, vh, ck, cv, l, sinks_col, vecs, ah, nk, nv, N)
        sh, st_s, plh = _mix_sample(uh, ph, h0, pbuf, vecs, pw, bblk, cblk, wg_all, wp_all, l, sh, plh)

        xm, xh = _outproj(xm, am, sm, plm, xh, ah, sh, plh, wo, TM_PROJ)
        xm, xh = _ffn(xm, xh, vecs, l, w1, w2, TM_FFN, TF_FFN)
        if l + 1 < depth:
            wi = wcast[2]

        ks.append(km.reshape(B, T, KV_WIDTH)[:, T - WINDOW:])
        vs.append(vm.reshape(B, T, KV_WIDTH)[:, T - WINDOW:])
        pls.append(pm.reshape(B, T, POOL_WIDTH)[:, T - POOL_BUF:])
        sts.append(st[:, 0])
        st_ss.append(st_s)
        phs.append(ph[:N])

    y_prompt = xm.reshape(B, T, D_MODEL)
    y_sample = xh[:N].reshape(N, 1, D_MODEL)
    heads = lambda t: jnp.stack(t).reshape(depth, -1, WINDOW, N_KV_HEADS, HEAD_DIM)
    p_re, p_im = _state_from_lanes(jnp.stack(sts).reshape(depth * B, SSM_STATE_LANES))
    s_re, s_im = _state_from_lanes(jnp.stack(st_ss).reshape(depth * N, SSM_STATE_LANES))
    st4 = lambda t, n: t.reshape(depth, n, SSM_GROUPS, SSM_STATE)
    s_pool = jnp.concatenate([state_pool.astype(F32)[:, :, 1:], jnp.stack(phs)[:, :, None]], axis=2)
    return (y_prompt, y_sample, heads(ks), heads(vs), st4(p_re, B), st4(p_im, B), jnp.stack(pls),
            nk, nv, st4(s_re, N), st4(s_im, N), s_pool)
```
